```python
import jax, jax.numpy as jnp
from jax import lax
import numpy as np

D_MODEL = 2048
BATCH = 8
SEQ = 4096
DEPTH = 2

N_A_LAYERS = DEPTH // 2
N_B_LAYERS = DEPTH - N_A_LAYERS
POOL_WINDOWS = (2, 4, 8, 16)
N_POOL_GROUPS = len(POOL_WINDOWS)
POOL_GROUP_DIM = D_MODEL // N_POOL_GROUPS
HEAD_DIM = 64
N_Q_HEADS = D_MODEL // HEAD_DIM
N_KV_HEADS = N_Q_HEADS // 8
GQA_GROUP = N_Q_HEADS // N_KV_HEADS
ATTN_WIDTH = N_Q_HEADS * HEAD_DIM
KV_WIDTH = N_KV_HEADS * HEAD_DIM
WINDOW = 128
BLOCK = 128
ROPE_THETA = 10000.0
LN_EPS = 1e-5
NEG_INF = -1e30
DEEPNORM_ALPHA = (2 * DEPTH) ** 0.25
DEEPNORM_BETA = (8 * DEPTH) ** -0.25

kernel_name = "yoco_pool_swa_sink_hybrid"


def layer_norm(x, g, b):
    xf = x.astype(jnp.float32)
    mu = jnp.mean(xf, axis=-1, keepdims=True)
    var = jnp.mean(jnp.square(xf - mu), axis=-1, keepdims=True)
    y = (xf - mu) * lax.rsqrt(var + LN_EPS) * g.astype(jnp.float32) + b.astype(jnp.float32)
    return y.astype(x.dtype)


def rope(t, pos):
    d = t.shape[-1]
    inv_freq = ROPE_THETA ** (-jnp.arange(0, d, 2, dtype=jnp.float32) / d)
    ang = pos[:, None] * inv_freq[None, :]
    ang = jnp.concatenate([ang, ang], axis=-1)[:, None, :]
    tf = t.astype(jnp.float32)
    t1, t2 = tf[..., : d // 2], tf[..., d // 2:]
    rot = jnp.concatenate([-t2, t1], axis=-1)
    return (tf * jnp.cos(ang) + rot * jnp.sin(ang)).astype(t.dtype)


def causal_window_mean(u, w):
    s = u.shape[1]
    c = jnp.cumsum(u, axis=1)
    c_prev = jnp.pad(c, ((0, 0), (w, 0), (0, 0)))[:, :s]
    count = jnp.minimum(jnp.arange(s) + 1, w).astype(jnp.float32)
    return (c - c_prev) / count[None, :, None]


def pool_mixer(x, w_in, w_group, scale, w_out):
    b, s, _ = x.shape
    h = x @ w_in
    u, z = h[..., :D_MODEL], h[..., D_MODEL:]
    ug = u.astype(jnp.float32).reshape(b, s, N_POOL_GROUPS, POOL_GROUP_DIM)
    pooled = jnp.stack(
        [causal_window_mean(ug[:, :, g], w) - ug[:, :, g] for g, w in enumerate(POOL_WINDOWS)],
        axis=2)
    mixed = jnp.einsum('bsgc,gcd->bsgd', pooled.astype(x.dtype), w_group).reshape(b, s, D_MODEL)
    y = mixed * scale * jax.nn.silu(z)
    return y @ w_out


def shared_kv(x, w_k, w_v, pos):
    b, s, _ = x.shape
    k = rope((x @ w_k).reshape(b, s, N_KV_HEADS, HEAD_DIM), pos)
    v = (x @ w_v).reshape(b, s, N_KV_HEADS, HEAD_DIM)
    return k, v


def band(t, nb):
    b = t.shape[0]
    tp = jnp.pad(t, ((0, 0), (BLOCK, 0), (0, 0), (0, 0)))
    tb = tp.reshape(b, nb + 1, BLOCK, t.shape[2], t.shape[3])
    return jnp.concatenate([tb[:, :-1], tb[:, 1:]], axis=2)


def banded_swa_with_sinks(q, k, v, sinks):
    b, s, _, d = q.shape
    nb = s // BLOCK
    qb = q.reshape(b, nb, BLOCK, N_KV_HEADS, GQA_GROUP, d)
    kb, vb = band(k, nb), band(v, nb)
    scores = jnp.einsum('bnqkgd,bnskd->bnkgqs', qb, kb,
                        preferred_element_type=jnp.float32) * (d ** -0.5)
    blk = jnp.arange(nb)[:, None, None]
    q_pos = blk * BLOCK + jnp.arange(BLOCK)[None, :, None]
    k_pos = (blk - 1) * BLOCK + jnp.arange(2 * BLOCK)[None, None, :]
    valid = (k_pos <= q_pos) & (k_pos > q_pos - WINDOW) & (k_pos >= 0)
    scores = jnp.where(valid[None, :, None, None], scores, NEG_INF)
    sink = sinks.astype(jnp.float32).reshape(N_KV_HEADS, GQA_GROUP)[None, None, :, :, None, None]
    m = jnp.maximum(jnp.max(scores, axis=-1, keepdims=True), sink)
    p = jnp.exp(scores - m)
    probs = p / (jnp.sum(p, axis=-1, keepdims=True) + jnp.exp(sink - m))
    out = jnp.einsum('bnkgqs,bnskd->bnqkgd', probs.astype(v.dtype), vb)
    return out.reshape(b, s, N_Q_HEADS * d)


def swa_mixer(x, k, v, w_qg, sinks, w_out, pos):
    b, s, _ = x.shape
    h = x @ w_qg
    q = rope(h[..., :ATTN_WIDTH].reshape(b, s, N_Q_HEADS, HEAD_DIM), pos)
    z = h[..., ATTN_WIDTH:]
    o = banded_swa_with_sinks(q, k, v, sinks)
    return (o * jax.nn.silu(z)) @ w_out


def _fwd_setup_inputs(seed: int = 0) -> dict:
    key = jax.random.key(seed)
    ks = jax.random.split(key, 16)
    f32 = jnp.float32
    nrm = lambda k, shape, fan_in: jax.random.normal(k, shape, f32) * fan_in ** -0.5
    return {
        "x": jax.random.normal(ks[0], (BATCH, SEQ, D_MODEL), f32),
        "ln_g": 1.0 + 0.02 * jax.random.normal(ks[1], (DEPTH, D_MODEL), f32),
        "ln_b": 0.02 * jax.random.normal(ks[2], (DEPTH, D_MODEL), f32),
        "a_w_in": nrm(ks[3], (N_A_LAYERS, D_MODEL, 2 * D_MODEL), D_MODEL),
        "a_w_group": nrm(ks[4], (N_A_LAYERS, N_POOL_GROUPS, POOL_GROUP_DIM, POOL_GROUP_DIM), POOL_GROUP_DIM),
        "a_scale": 1.0 + 0.02 * jax.random.normal(ks[5], (N_A_LAYERS, D_MODEL), f32),
        "a_w_out": nrm(ks[6], (N_A_LAYERS, D_MODEL, D_MODEL), D_MODEL) * DEEPNORM_BETA,
        "b_w_k": nrm(ks[7], (D_MODEL, KV_WIDTH), D_MODEL),
        "b_w_v": nrm(ks[8], (D_MODEL, KV_WIDTH), D_MODEL) * DEEPNORM_BETA,
        "b_w_qg": nrm(ks[9], (N_B_LAYERS, D_MODEL, 2 * ATTN_WIDTH), D_MODEL),
        "b_sinks": 0.5 * jax.random.normal(ks[10], (N_B_LAYERS, N_Q_HEADS), f32),
        "b_w_out": nrm(ks[11], (N_B_LAYERS, ATTN_WIDTH, D_MODEL), ATTN_WIDTH) * DEEPNORM_BETA,
    }


def _fwd_reference(x, ln_g, ln_b, a_w_in, a_w_group, a_scale, a_w_out,
              b_w_k, b_w_v, b_w_qg, b_sinks, b_w_out):
    s = x.shape[1]
    pos = jnp.arange(s, dtype=jnp.float32)
    k_sh, v_sh = None, None
    for i in range(DEPTH):
        if i < N_A_LAYERS:
            y = pool_mixer(x, a_w_in[i], a_w_group[i], a_scale[i], a_w_out[i])
        else:
            j = i - N_A_LAYERS
            if j == 0:
                k_sh, v_sh = shared_kv(x, b_w_k, b_w_v, pos)
            y = swa_mixer(x, k_sh, v_sh, b_w_qg[j], b_sinks[j], b_w_out[j], pos)
        x = layer_norm(DEEPNORM_ALPHA * x + y, ln_g[i], ln_b[i])
    return x


import jax as _jax
import jax.numpy as _jnp

TWIN_FORMAT = 'train_step'
FWD_PARAMS = ['x', 'ln_g', 'ln_b', 'a_w_in', 'a_w_group', 'a_scale', 'a_w_out', 'b_w_k', 'b_w_v', 'b_w_qg', 'b_sinks', 'b_w_out']
TWIN_WEIGHTS = ['ln_g', 'ln_b', 'a_w_in', 'a_w_group', 'a_scale', 'a_w_out', 'b_w_k', 'b_w_v', 'b_w_qg', 'b_sinks', 'b_w_out']
TWIN_DIFF_INPUT = 'x'
TWIN_INPUTS = ['x', 'ln_g', 'ln_b', 'a_w_in', 'a_w_group', 'a_scale', 'a_w_out', 'b_w_k', 'b_w_v', 'b_w_qg', 'b_sinks', 'b_w_out', 'loss_target', 'm_ln_g', 'm_ln_b', 'm_a_w_in', 'm_a_w_group', 'm_a_scale', 'm_a_w_out', 'm_b_w_k', 'm_b_w_v', 'm_b_w_qg', 'm_b_sinks', 'm_b_w_out', 'v_ln_g', 'v_ln_b', 'v_a_w_in', 'v_a_w_group', 'v_a_scale', 'v_a_w_out', 'v_b_w_k', 'v_b_w_v', 'v_b_w_qg', 'v_b_sinks', 'v_b_w_out']
TWIN_OUTPUTS = ['loss', 'grad_x', 'grad_ln_g', 'grad_ln_b', 'grad_a_w_in', 'grad_a_w_group', 'grad_a_scale', 'grad_a_w_out', 'grad_b_w_k', 'grad_b_w_v', 'grad_b_w_qg', 'grad_b_sinks', 'grad_b_w_out', 'delta_ln_g', 'delta_ln_b', 'delta_a_w_in', 'delta_a_w_group', 'delta_a_scale', 'delta_a_w_out', 'delta_b_w_k', 'delta_b_w_v', 'delta_b_w_qg', 'delta_b_sinks', 'delta_b_w_out', 'new_m_ln_g', 'new_m_ln_b', 'new_m_a_w_in', 'new_m_a_w_group', 'new_m_a_scale', 'new_m_a_w_out', 'new_m_b_w_k', 'new_m_b_w_v', 'new_m_b_w_qg', 'new_m_b_sinks', 'new_m_b_w_out', 'new_v_ln_g', 'new_v_ln_b', 'new_v_a_w_in', 'new_v_a_w_group', 'new_v_a_scale', 'new_v_a_w_out', 'new_v_b_w_k', 'new_v_b_w_v', 'new_v_b_w_qg', 'new_v_b_sinks', 'new_v_b_w_out']
TWIN_LEAF_KINDS = {'loss': 'loss', 'grad_x': 'grad_x', 'grad_ln_g': 'grad_w', 'grad_ln_b': 'grad_w', 'grad_a_w_in': 'grad_w', 'grad_a_w_group': 'grad_w', 'grad_a_scale': 'grad_w', 'grad_a_w_out': 'grad_w', 'grad_b_w_k': 'grad_w', 'grad_b_w_v': 'grad_w', 'grad_b_w_qg': 'grad_w', 'grad_b_sinks': 'grad_w', 'grad_b_w_out': 'grad_w', 'delta_ln_g': 'delta_w', 'delta_ln_b': 'delta_w', 'delta_a_w_in': 'delta_w', 'delta_a_w_group': 'delta_w', 'delta_a_scale': 'delta_w', 'delta_a_w_out': 'delta_w', 'delta_b_w_k': 'delta_w', 'delta_b_w_v': 'delta_w', 'delta_b_w_qg': 'delta_w', 'delta_b_sinks': 'delta_w', 'delta_b_w_out': 'delta_w', 'new_m_ln_g': 'new_m', 'new_m_ln_b': 'new_m', 'new_m_a_w_in': 'new_m', 'new_m_a_w_group': 'new_m', 'new_m_a_scale': 'new_m', 'new_m_a_w_out': 'new_m', 'new_m_b_w_k': 'new_m', 'new_m_b_w_v': 'new_m', 'new_m_b_w_qg': 'new_m', 'new_m_b_sinks': 'new_m', 'new_m_b_w_out': 'new_m', 'new_v_ln_g': 'new_v', 'new_v_ln_b': 'new_v', 'new_v_a_w_in': 'new_v', 'new_v_a_w_group': 'new_v', 'new_v_a_scale': 'new_v', 'new_v_a_w_out': 'new_v', 'new_v_b_w_k': 'new_v', 'new_v_b_w_v': 'new_v', 'new_v_b_w_qg': 'new_v', 'new_v_b_sinks': 'new_v', 'new_v_b_w_out': 'new_v'}


def _forward(args):
    return _fwd_reference(*[args[k] for k in FWD_PARAMS])


def _output_shape():
    def fwd():
        inp = _fwd_setup_inputs(0)
        return _fwd_reference(*[inp[k] for k in FWD_PARAMS])
    out = _jax.eval_shape(fwd)
    return out.shape, out.dtype

N_MICROBATCH = 1
ADAM_LR = 0.001
ADAM_B1 = 0.9
ADAM_B2 = 0.999
ADAM_EPS = 1e-08
ADAM_WD = 0.01
ADAM_STEP = 10
PER_EXAMPLE_BATCH_AXIS = {'x': 0, 'loss_target': 0}
SHARED_INPUTS = []
_WEIGHT_DTYPES = {'ln_g': _jnp.float32, 'ln_b': _jnp.float32, 'a_w_in': _jnp.float32, 'a_w_group': _jnp.float32, 'a_scale': _jnp.float32, 'a_w_out': _jnp.float32, 'b_w_k': _jnp.float32, 'b_w_v': _jnp.float32, 'b_w_qg': _jnp.float32, 'b_sinks': _jnp.float32, 'b_w_out': _jnp.float32}
MOMENT_SCALE = {'ln_g': 1.132662e+01, 'ln_b': 3.517958e-01, 'a_w_in': 1.637372e-02, 'a_w_group': 1.611735e-02, 'a_scale': 1.654045e-02, 'a_w_out': 3.221854e-02, 'b_w_k': 4.761339e-03, 'b_w_v': 1.086364e-02, 'b_w_qg': 1.840631e-03, 'b_sinks': 1.773379e-03, 'b_w_out': 3.862514e-03}


def _to_microbatches(a, axis):
    t = _jnp.moveaxis(a, axis, 0)
    t = t.reshape((N_MICROBATCH, t.shape[0] // N_MICROBATCH) + t.shape[1:])
    return _jnp.moveaxis(t, 1, axis + 1)


def setup_inputs(seed: int = 0) -> dict:
    inp = _fwd_setup_inputs(seed)
    key = _jax.random.fold_in(_jax.random.key(seed), 7919)
    shape, _ = _output_shape()
    out = dict(inp)
    out["loss_target"] = _jax.random.normal(_jax.random.fold_in(key, 0), shape, _jnp.float32)
    for i, name in enumerate(TWIN_WEIGHTS):
        w = inp[name].astype(_jnp.float32)
        if MOMENT_SCALE is None:
            s = _jnp.sqrt(_jnp.mean(_jnp.square(w)) + 1e-30)
        else:
            s = MOMENT_SCALE[name]
        km, kv = _jax.random.split(_jax.random.fold_in(key, i + 1))
        out[name] = w
        out["m_" + name] = s * _jax.random.normal(km, w.shape, _jnp.float32)
        out["v_" + name] = (s * s) * _jax.random.uniform(kv, w.shape, _jnp.float32, 0.5, 1.5)
    if N_MICROBATCH > 1:
        for name, axis in PER_EXAMPLE_BATCH_AXIS.items():
            out[name] = _to_microbatches(out[name], axis)
    return {'x': out['x'], 'ln_g': out['ln_g'], 'ln_b': out['ln_b'], 'a_w_in': out['a_w_in'], 'a_w_group': out['a_w_group'], 'a_scale': out['a_scale'], 'a_w_out': out['a_w_out'], 'b_w_k': out['b_w_k'], 'b_w_v': out['b_w_v'], 'b_w_qg': out['b_w_qg'], 'b_sinks': out['b_sinks'], 'b_w_out': out['b_w_out'], 'loss_target': out['loss_target'], 'm_ln_g': out['m_ln_g'], 'm_ln_b': out['m_ln_b'], 'm_a_w_in': out['m_a_w_in'], 'm_a_w_group': out['m_a_w_group'], 'm_a_scale': out['m_a_scale'], 'm_a_w_out': out['m_a_w_out'], 'm_b_w_k': out['m_b_w_k'], 'm_b_w_v': out['m_b_w_v'], 'm_b_w_qg': out['m_b_w_qg'], 'm_b_sinks': out['m_b_sinks'], 'm_b_w_out': out['m_b_w_out'], 'v_ln_g': out['v_ln_g'], 'v_ln_b': out['v_ln_b'], 'v_a_w_in': out['v_a_w_in'], 'v_a_w_group': out['v_a_w_group'], 'v_a_scale': out['v_a_scale'], 'v_a_w_out': out['v_a_w_out'], 'v_b_w_k': out['v_b_w_k'], 'v_b_w_v': out['v_b_w_v'], 'v_b_w_qg': out['v_b_w_qg'], 'v_b_sinks': out['v_b_sinks'], 'v_b_w_out': out['v_b_w_out']}


def _loss(weights, diff, rest, loss_target):
    with _jax.named_scope("forward"):
        args = {**rest, TWIN_DIFF_INPUT: diff, **{k: w.astype(_WEIGHT_DTYPES[k]) for k, w in weights.items()}}
        y = _forward(args)
    with _jax.named_scope("loss_head"):
        err = _jnp.square(y.astype(_jnp.float32) - loss_target)
        return 0.5 * _jnp.sum(_jnp.mean(err, axis=-1)) if err.ndim else 0.5 * err


def _adamw(w, g, m, v):
    m = ADAM_B1 * m + (1.0 - ADAM_B1) * g
    v = ADAM_B2 * v + (1.0 - ADAM_B2) * _jnp.square(g)
    m_hat = m / (1.0 - ADAM_B1 ** ADAM_STEP)
    v_hat = v / (1.0 - ADAM_B2 ** ADAM_STEP)
    delta = -ADAM_LR * (m_hat / (_jnp.sqrt(v_hat) + ADAM_EPS) + ADAM_WD * w)
    return delta, m, v


def reference(x, ln_g, ln_b, a_w_in, a_w_group, a_scale, a_w_out, b_w_k, b_w_v, b_w_qg, b_sinks, b_w_out, loss_target, m_ln_g, m_ln_b, m_a_w_in, m_a_w_group, m_a_scale, m_a_w_out, m_b_w_k, m_b_w_v, m_b_w_qg, m_b_sinks, m_b_w_out, v_ln_g, v_ln_b, v_a_w_in, v_a_w_group, v_a_scale, v_a_w_out, v_b_w_k, v_b_w_v, v_b_w_qg, v_b_sinks, v_b_w_out):
    given = dict(x=x, ln_g=ln_g, ln_b=ln_b, a_w_in=a_w_in, a_w_group=a_w_group, a_scale=a_scale, a_w_out=a_w_out, b_w_k=b_w_k, b_w_v=b_w_v, b_w_qg=b_w_qg, b_sinks=b_sinks, b_w_out=b_w_out, loss_target=loss_target, m_ln_g=m_ln_g, m_ln_b=m_ln_b, m_a_w_in=m_a_w_in, m_a_w_group=m_a_w_group, m_a_scale=m_a_scale, m_a_w_out=m_a_w_out, m_b_w_k=m_b_w_k, m_b_w_v=m_b_w_v, m_b_w_qg=m_b_w_qg, m_b_sinks=m_b_sinks, m_b_w_out=m_b_w_out, v_ln_g=v_ln_g, v_ln_b=v_ln_b, v_a_w_in=v_a_w_in, v_a_w_group=v_a_w_group, v_a_scale=v_a_scale, v_a_w_out=v_a_w_out, v_b_w_k=v_b_w_k, v_b_w_v=v_b_w_v, v_b_w_qg=v_b_w_qg, v_b_sinks=v_b_sinks, v_b_w_out=v_b_w_out)
    weights = {n: given[n] for n in TWIN_WEIGHTS}
    shared = {n: given[n] for n in SHARED_INPUTS}
    per_example = {n: given[n] for n in ['x']}
    grad_fn = _jax.value_and_grad(_loss, argnums=(0, 1))

    def one_microbatch(ex, loss_target):
        ex = dict(ex)
        diff = ex.pop(TWIN_DIFF_INPUT)
        return grad_fn(weights, diff, {**shared, **ex}, loss_target)

    if N_MICROBATCH == 1:
        loss, (grad_w, grad_x) = one_microbatch(per_example, given["loss_target"])
    else:
        def body(carry, xs):
            loss_sum, grad_sum = carry
            l_k, (gw_k, gx_k) = one_microbatch(xs[0], xs[1])
            with _jax.named_scope("update"):
                return (loss_sum + l_k, _jax.tree.map(_jnp.add, grad_sum, gw_k)), gx_k

        init = (_jnp.zeros((), _jnp.float32), _jax.tree.map(_jnp.zeros_like, weights))
        (loss, grad_w), grad_x = _jax.lax.scan(body, init, (per_example, given["loss_target"]))
    with _jax.named_scope("update"):
        delta_w, new_m, new_v = {}, {}, {}
        for n in TWIN_WEIGHTS:
            delta_w[n], new_m[n], new_v[n] = _adamw(weights[n], grad_w[n], given["m_" + n], given["v_" + n])
    return (loss, grad_x, *[grad_w[n] for n in TWIN_WEIGHTS], *[delta_w[n] for n in TWIN_WEIGHTS],
            *[new_m[n] for n in TWIN_WEIGHTS], *[new_v[n] for n in TWIN_WEIGHTS])
```

```python
import functools

import jax
import jax.numpy as jnp
from jax import lax
from jax.experimental import pallas as pl
from jax.experimental.pallas import tpu as pltpu

F32 = jnp.float32
BF16 = jnp.bfloat16
MESH = pl.DeviceIdType.MESH
AXES = ("x", "y", "c")
N_DEV = 8

POOL_WINDOWS = (2, 4, 8, 16)
POOL_HALO = 16
HEAD_DIM = 64
GQA_GROUP = 8
ATTN_BLOCK = 128
ROPE_THETA = 10000.0
LN_EPS = 1e-5
NEG_INF = -1e30
DEPTH = 2
ALPHA = (2 * DEPTH) ** 0.25
ADAM_LR = 0.001
ADAM_B1 = 0.9
ADAM_B2 = 0.999
ADAM_EPS = 1e-08
ADAM_WD = 0.01
ADAM_STEP = 10

LANES = 128
STAT_ROWS = 8


def _tile(n, want):
    t = min(n, want)
    while n % t:
        t //= 2
    return t


def _params(*sem):
    return pltpu.CompilerParams(dimension_semantics=sem)


ANY = pl.BlockSpec(memory_space=pl.ANY)


def _my_pos():
    return lax.axis_index("x"), lax.axis_index("y"), lax.axis_index("c")


def _dev_index(p):
    return 4 * p[0] + 2 * p[1] + p[2]


def _gather_weights(streams, out_shapes):
    n_s = len(streams)
    n_out = len(out_shapes)

    def body(*refs):
        srcs = refs[:n_s]
        outs = refs[n_s:n_s + n_out]
        send_sems, recv_sems, local_sems = refs[n_s + n_out:]
        x, y, c = _my_pos()
        me, sibling = (x, y, c), (x, y, 1 - c)
        chips = [(1 - x, y), (x, 1 - y), (1 - x, 1 - y)]

        def copy(s, k, block, to, from_shard=False):
            out_ref = outs[streams[s][1]]
            win = streams[s][2](out_ref, _dev_index(block))
            return pltpu.make_async_remote_copy(
                src_ref=srcs[s] if from_shard else win, dst_ref=win,
                send_sem=send_sems.at[7 * s + k], recv_sem=recv_sems.at[7 * s + k],
                device_id=to, device_id_type=MESH)

        mine = [pltpu.make_async_copy(srcs[s], streams[s][2](outs[streams[s][1]], _dev_index(me)), local_sems.at[s])
                for s in range(n_s)]
        for cp in mine:
            cp.start()
        first = []
        for s in range(n_s):
            first.append(copy(s, 0, me, sibling, True))
            first += [copy(s, 1 + j, me, (*chip, c), True) for j, chip in enumerate(chips)]
        for cp in first:
            cp.start()
        passed = []
        for j, chip in enumerate(chips):
            for s in range(n_s):
                copy(s, 1 + j, (*chip, c), me).wait_recv()
                fwd = copy(s, 4 + j, (*chip, c), sibling)
                fwd.start()
                passed.append(fwd)
        for s in range(n_s):
            copy(s, 0, sibling, me).wait_recv()
            for j, chip in enumerate(chips):
                copy(s, 4 + j, (*chip, 1 - c), me).wait_recv()
        for cp in first + passed:
            cp.wait_send()
        for cp in mine:
            cp.wait()

    return pl.pallas_call(
        body, name="gather_weights",
        out_shape=tuple(out_shapes),
        in_specs=[ANY] * n_s, out_specs=tuple([ANY] * n_out),
        scratch_shapes=[pltpu.SemaphoreType.DMA((7 * n_s,)), pltpu.SemaphoreType.DMA((7 * n_s,)),
                        pltpu.SemaphoreType.DMA((n_s,))],
    )(*[s[0] for s in streams])


def _exchange_blocks(name, streams):
    n_s = len(streams)

    def body(*refs):
        srcs = refs[:n_s]
        outs = refs[n_s:2 * n_s]
        send_sems, recv_sems, local_sems = refs[2 * n_s:]
        x, y, c = _my_pos()
        me = _dev_index((x, y, c))

        def window(s, dev):
            return srcs[s] if streams[s][1] is None else streams[s][1](srcs[s], dev)

        mine = [pltpu.make_async_copy(window(s, me), outs[s].at[me], local_sems.at[s]) for s in range(n_s)]
        for cp in mine:
            cp.start()
        copies = []
        for k in (2, 4, 6, 3, 5, 7, 1):
            peer = (1 - x if k & 4 else x, 1 - y if k & 2 else y, 1 - c if k & 1 else c)
            for s in range(n_s):
                copies.append(pltpu.make_async_remote_copy(
                    src_ref=window(s, _dev_index(peer)), dst_ref=outs[s].at[me],
                    send_sem=send_sems.at[7 * s + k - 1], recv_sem=recv_sems.at[7 * s + k - 1],
                    device_id=peer, device_id_type=MESH))
        for cp in copies:
            cp.start()
        for cp in copies:
            cp.wait()
        for cp in mine:
            cp.wait()

    return pl.pallas_call(
        body, name=name,
        out_shape=tuple(jax.ShapeDtypeStruct((N_DEV,) + tuple(s[2]), s[0].dtype) for s in streams),
        in_specs=[ANY] * n_s, out_specs=tuple([ANY] * n_s),
        scratch_shapes=[pltpu.SemaphoreType.DMA((7 * n_s,)), pltpu.SemaphoreType.DMA((7 * n_s,)),
                        pltpu.SemaphoreType.DMA((n_s,))],
    )(*[s[0] for s in streams])


NN = (((1,), (0,)), ((), ()))
NT = (((1,), (1,)), ((), ()))
TN = (((0,), (0,)), ((), ()))


def _mm(name, a, b, *, dims, grid, a_spec, b_spec, out_shape, out_spec, nk=1,
        add=None, add_spec=None, add_scale=1.0, epilogue=None, extras=(), extra_specs=()):
    n_extra = len(extras)
    has_add = add is not None

    def body(*refs):
        a_ref, b_ref = refs[:2]
        pos = 2
        add_ref = None
        if has_add:
            add_ref = refs[pos]
            pos += 1
        extra_refs = refs[pos:pos + n_extra]
        o_ref = refs[pos + n_extra]
        acc_ref = refs[pos + n_extra + 1] if nk > 1 else None

        def finish(val):
            if has_add:
                val = val + add_scale * add_ref[...]
            if epilogue is not None:
                val = epilogue(val, *extra_refs)
            o_ref[...] = val.astype(o_ref.dtype)

        part = lax.dot_general(a_ref[...].astype(BF16), b_ref[...].astype(BF16), dims,
                               preferred_element_type=F32)
        if nk == 1:
            finish(part)
        else:
            k = pl.program_id(2)

            @pl.when(k == 0)
            def _():
                acc_ref[...] = part

            @pl.when(jnp.logical_and(k > 0, k < nk - 1))
            def _():
                acc_ref[...] += part

            @pl.when(k == nk - 1)
            def _():
                finish(acc_ref[...] + part)

    in_specs = [a_spec, b_spec] + ([add_spec] if has_add else []) + list(extra_specs)
    operands = [a, b] + ([add] if has_add else []) + list(extras)
    scratch = [pltpu.VMEM(out_spec.block_shape, F32)] if nk > 1 else []
    sem = ("parallel", "parallel") + (("arbitrary",) if nk > 1 else ())
    return pl.pallas_call(
        body, name=name, grid=grid, out_shape=out_shape,
        in_specs=in_specs, out_specs=out_spec, scratch_shapes=scratch,
        compiler_params=_params(*sem),
    )(*operands)


def _cast_bf16(name, a):
    rows, cols = a.shape
    tr = _tile(rows, 512)

    def body(a_ref, o_ref):
        o_ref[...] = a_ref[...].astype(BF16)

    return pl.pallas_call(
        body, name=name, grid=(rows // tr,),
        out_shape=jax.ShapeDtypeStruct(a.shape, BF16),
        in_specs=[pl.BlockSpec((tr, cols), lambda i: (i, 0))],
        out_specs=pl.BlockSpec((tr, cols), lambda i: (i, 0)),
        compiler_params=_params("parallel"),
    )(a)


def _rope_tables(seq):
    inv_freq = ROPE_THETA ** (-jnp.arange(0, HEAD_DIM, 2, dtype=F32) / HEAD_DIM)
    ang = jnp.arange(seq, dtype=F32)[:, None] * inv_freq[None, :]
    ang = jnp.concatenate([ang, ang, ang, ang], axis=-1)
    cos, sin = jnp.cos(ang), jnp.sin(ang)
    first_half = (jnp.arange(LANES) % HEAD_DIM < HEAD_DIM // 2)[None, :]
    return cos, jnp.where(first_half, -sin, 0.0), jnp.where(first_half, 0.0, sin)


def _rot(t, sin_a, sin_b):
    return pltpu.roll(t, LANES - HEAD_DIM // 2, 1) * sin_a + pltpu.roll(t, HEAD_DIM // 2, 1) * sin_b


def _rope(t, cos, sin_a, sin_b):
    return t * cos + _rot(t, sin_a, sin_b)


def _rope_transposed(dy, cos, sin_a, sin_b):
    return dy * cos - _rot(dy, sin_a, sin_b)


def _silu_parts(z):
    sig = jax.nn.sigmoid(z)
    return z * sig, sig * (1.0 + z * (1.0 - sig))


def _layer_norm_stats(r):
    mu = jnp.mean(r, axis=-1, keepdims=True)
    d = r - mu
    var = jnp.mean(d * d, axis=-1, keepdims=True)
    rstd = lax.rsqrt(var + LN_EPS)
    return d * rstd, rstd


def _layer_norm_backward(dout, xhat, rstd, gain):
    dxh = dout * gain
    m1 = jnp.mean(dxh, axis=-1, keepdims=True)
    m2 = jnp.mean(dxh * xhat, axis=-1, keepdims=True)
    return rstd * (dxh - m1 - xhat * m2)


def _col_sum(v):
    return jnp.sum(v, axis=0, keepdims=True)


def _pool_mid_forward(h, wg, scale):
    seq, d2 = h.shape
    dm = d2 // 2
    gd = dm // len(POOL_WINDOWS)
    tile = _tile(seq, 256)
    halo_blocks = tile // POOL_HALO

    def body(u_ref, up_ref, z_ref, wg_ref, sc_ref, y_ref, p_ref, mx_ref):
        i = pl.program_id(0)
        row = i * tile + lax.broadcasted_iota(jnp.int32, (tile, 1), 0)
        count = (row + 1).astype(F32)
        for g, w in enumerate(POOL_WINDOWS):
            cs = slice(g * gd, (g + 1) * gd)
            u = u_ref[:, cs]
            prev = jnp.where(i > 0, up_ref[:, cs], 0.0)
            s = jnp.concatenate([prev, u], axis=0)
            sh = 1
            while sh < w:
                s = s + pltpu.roll(s, sh, 0)
                sh *= 2
            p = s[POOL_HALO:, :] * (1.0 / jnp.minimum(count, float(w))) - u
            pb = p.astype(BF16)
            mx = jnp.dot(pb, wg_ref[g], preferred_element_type=F32)
            z = z_ref[:, cs]
            y = mx * sc_ref[:, cs] * (z * jax.nn.sigmoid(z))
            y_ref[:, cs] = y.astype(BF16)
            p_ref[:, cs] = pb
            mx_ref[:, cs] = mx

    row_spec = pl.BlockSpec((tile, dm), lambda i: (i, 0))
    return pl.pallas_call(
        body, name="pool_mid_fwd", grid=(seq // tile,),
        out_shape=(jax.ShapeDtypeStruct((seq, dm), BF16), jax.ShapeDtypeStruct((seq, dm), BF16),
                   jax.ShapeDtypeStruct((seq, dm), F32)),
        in_specs=[row_spec,
                  pl.BlockSpec((POOL_HALO, dm), lambda i: (jnp.maximum(i * halo_blocks - 1, 0), 0)),
                  pl.BlockSpec((tile, dm), lambda i: (i, 1)),
                  pl.BlockSpec(wg.shape, lambda i: (0, 0, 0)),
                  pl.BlockSpec((1, dm), lambda i: (0, 0))],
        out_specs=(row_spec, row_spec, row_spec),
        compiler_params=_params("parallel"),
    )(h, h, h, wg, scale)


def _pool_mid_backward(dy, mx, h, p, wg, scale):
    seq, dm = dy.shape
    gd = dm // len(POOL_WINDOWS)
    tile = _tile(seq, 256)
    n_i = seq // tile

    def body(dy_ref, mx_ref, z_ref, p_ref, wg_ref, sc_ref, dh_ref, dwg_ref, st_ref, dwg_acc, carry):
        i = pl.program_id(0)
        ti = n_i - 1 - i

        @pl.when(i == 0)
        def _():
            dwg_acc[...] = jnp.zeros_like(dwg_acc)
            carry[...] = jnp.zeros_like(carry)
            st_ref[...] = jnp.zeros_like(st_ref)

        row = ti * tile + lax.broadcasted_iota(jnp.int32, (tile, 1), 0)
        count = (row + 1).astype(F32)
        for g, w in enumerate(POOL_WINDOWS):
            cs = slice(g * gd, (g + 1) * gd)
            z = z_ref[:, cs]
            sz, dsz = _silu_parts(z)
            dyg = dy_ref[:, cs]
            mxg = mx_ref[:, cs]
            sc = sc_ref[:, cs]
            t1 = dyg * sz
            st_ref[0:1, cs] += _col_sum(t1 * mxg)
            dh_ref[1, :, cs] = (dyg * (mxg * sc) * dsz).astype(BF16)
            dmx = (t1 * sc).astype(BF16)
            dwg_acc[g] += lax.dot_general(p_ref[:, cs], dmx, TN, preferred_element_type=F32)
            dp = lax.dot_general(dmx, wg_ref[g], NT, preferred_element_type=F32)
            e = dp * (1.0 / jnp.minimum(count, float(w)))
            s = jnp.concatenate([e, carry[:, cs]], axis=0)
            n = tile + POOL_HALO
            sh = 1
            while sh < w:
                s = s + pltpu.roll(s, n - sh, 0)
                sh *= 2
            dh_ref[0, :, cs] = (s[:tile, :] - dp).astype(BF16)
            carry[:, cs] = e[:POOL_HALO, :]

        @pl.when(i == n_i - 1)
        def _():
            dwg_ref[...] = dwg_acc[...].astype(BF16)

    row_spec = pl.BlockSpec((tile, dm), lambda i: (n_i - 1 - i, 0))
    return pl.pallas_call(
        body, name="pool_mid_bwd", grid=(n_i,),
        out_shape=(jax.ShapeDtypeStruct((2, seq, dm), BF16), jax.ShapeDtypeStruct(wg.shape, BF16),
                   jax.ShapeDtypeStruct((STAT_ROWS, dm), F32)),
        in_specs=[row_spec, row_spec,
                  pl.BlockSpec((tile, dm), lambda i: (n_i - 1 - i, 1)),
                  row_spec,
                  pl.BlockSpec(wg.shape, lambda i: (0, 0, 0)),
                  pl.BlockSpec((1, dm), lambda i: (0, 0))],
        out_specs=(pl.BlockSpec((2, tile, dm), lambda i: (0, n_i - 1 - i, 0)),
                   pl.BlockSpec(wg.shape, lambda i: (0, 0, 0)),
                   pl.BlockSpec((STAT_ROWS, dm), lambda i: (0, 0))),
        scratch_shapes=[pltpu.VMEM(wg.shape, F32), pltpu.VMEM((POOL_HALO, dm), F32)],
        compiler_params=_params("arbitrary"),
    )(dy, mx, h, p, wg, scale)


def _out_proj_norm(y, w, x, gain, bias):
    seq, dm = x.shape
    tile = _tile(seq, 256)

    def body(y_ref, w_ref, x_ref, g_ref, b_ref, xhat_ref, rstd_ref, xb_ref):
        o = jnp.dot(y_ref[...], w_ref[...], preferred_element_type=F32)
        xhat, rstd = _layer_norm_stats(ALPHA * x_ref[...] + o)
        xhat_ref[...] = xhat
        rstd_ref[...] = rstd
        xb_ref[...] = (xhat * g_ref[...] + b_ref[...]).astype(BF16)

    row_spec = pl.BlockSpec((tile, dm), lambda i: (i, 0))
    vec_spec = pl.BlockSpec((1, dm), lambda i: (0, 0))
    return pl.pallas_call(
        body, name="out_proj_norm_a", grid=(seq // tile,),
        out_shape=(jax.ShapeDtypeStruct((seq, dm), F32), jax.ShapeDtypeStruct((seq, 1), F32),
                   jax.ShapeDtypeStruct((seq, dm), BF16)),
        in_specs=[row_spec, pl.BlockSpec(w.shape, lambda i: (0, 0)), row_spec, vec_spec, vec_spec],
        out_specs=(row_spec, pl.BlockSpec((tile, 1), lambda i: (i, 0)), row_spec),
        compiler_params=_params("parallel"),
    )(y, w, x, gain, bias)


def _kv_proj(xb, wkv, tables):
    seq, dm = xb.shape
    kvw = wkv.shape[1] // 2
    n_kv = kvw // HEAD_DIM
    tile = _tile(seq, 512)

    def body(x_ref, w_ref, cos_ref, sa_ref, sb_ref, kd_ref, vd_ref):
        kv = jnp.dot(x_ref[...], w_ref[...], preferred_element_type=F32)
        low = lax.broadcasted_iota(jnp.int32, (1, LANES), 1) < HEAD_DIM
        cos, sa, sb = cos_ref[...], sa_ref[...], sb_ref[...]
        for j in range(kvw // LANES):
            pair = _rope(kv[:, j * LANES:(j + 1) * LANES], cos, sa, sb)
            swapped = pltpu.roll(pair, HEAD_DIM, 1)
            kd_ref[2 * j] = jnp.where(low, pair, swapped).astype(BF16)
            kd_ref[2 * j + 1] = jnp.where(low, swapped, pair).astype(BF16)
            pair = kv[:, kvw + j * LANES:kvw + (j + 1) * LANES]
            swapped = pltpu.roll(pair, HEAD_DIM, 1)
            vd_ref[2 * j] = jnp.where(low, pair, swapped).astype(BF16)
            vd_ref[2 * j + 1] = jnp.where(low, swapped, pair).astype(BF16)

    tab_spec = pl.BlockSpec((tile, LANES), lambda i: (i, 0))
    dup_spec = pl.BlockSpec((n_kv, tile, LANES), lambda i: (0, i, 0))
    dup_shape = jax.ShapeDtypeStruct((n_kv, seq, LANES), BF16)
    return pl.pallas_call(
        body, name="kv_proj", grid=(seq // tile,),
        out_shape=(dup_shape, dup_shape),
        in_specs=[pl.BlockSpec((tile, dm), lambda i: (i, 0)), pl.BlockSpec(wkv.shape, lambda i: (0, 0)),
                  tab_spec, tab_spec, tab_spec],
        out_specs=(dup_spec, dup_spec),
        compiler_params=_params("parallel"),
    )(xb, wkv, *tables)


def _attn_masks(n):
    q_pos = n * ATTN_BLOCK + lax.broadcasted_iota(jnp.int32, (ATTN_BLOCK, 2 * ATTN_BLOCK), 0)
    k_pos = (n - 1) * ATTN_BLOCK + lax.broadcasted_iota(jnp.int32, (ATTN_BLOCK, 2 * ATTN_BLOCK), 1)
    valid = (k_pos <= q_pos) & (k_pos > q_pos - ATTN_BLOCK) & (k_pos >= 0)
    low = lax.broadcasted_iota(jnp.int32, (1, LANES), 1) < HEAD_DIM
    return valid, low


def _head_probs(qm, kcat, valid, sink):
    s = lax.dot_general(qm, kcat, NT, preferred_element_type=F32)
    s = jnp.where(valid, s, NEG_INF)
    m = jnp.maximum(jnp.max(s, axis=-1, keepdims=True), sink)
    e = jnp.exp(s - m)
    e_sink = jnp.exp(sink - m)
    den = jnp.sum(e, axis=-1, keepdims=True) + e_sink
    return e / den, e_sink / den


def _attn_specs(n_width):
    q_spec = pl.BlockSpec((ATTN_BLOCK, n_width), lambda kh, n: (n, kh))
    cur = pl.BlockSpec((None, ATTN_BLOCK, LANES), lambda kh, n: (kh, n, 0))
    prev = pl.BlockSpec((None, ATTN_BLOCK, LANES), lambda kh, n: (kh, jnp.maximum(n - 1, 0), 0))
    return q_spec, cur, prev


def _attn_forward(qs, kd, vd, zb, sinks):
    seq, dm = qs.shape
    n_kv = kd.shape[0]
    gw = GQA_GROUP * HEAD_DIM

    def body(q_ref, kp_ref, kc_ref, vp_ref, vc_ref, z_ref, sink_ref, att_ref, yb_ref):
        kh, n = pl.program_id(0), pl.program_id(1)
        valid, low = _attn_masks(n)
        kcat = jnp.concatenate([kp_ref[...], kc_ref[...]], axis=0)
        vcat = jnp.concatenate([vp_ref[...], vc_ref[...]], axis=0)
        for j in range(gw // LANES):
            cs = slice(j * LANES, (j + 1) * LANES)
            q2 = q_ref[:, cs]
            o2 = jnp.zeros((ATTN_BLOCK, LANES), F32)
            for half in (0, 1):
                mine = low if half == 0 else jnp.logical_not(low)
                sink = sink_ref[0, kh * GQA_GROUP + 2 * j + half]
                probs, _ = _head_probs(jnp.where(mine, q2, 0), kcat, valid, sink)
                o2 = o2 + jnp.dot(probs.astype(BF16), jnp.where(mine, vcat, 0), preferred_element_type=F32)
            att_ref[:, cs] = o2
            z = z_ref[:, cs]
            yb_ref[:, cs] = (o2 * (z * jax.nn.sigmoid(z))).astype(BF16)

    q_spec, cur, prev = _attn_specs(gw)
    return pl.pallas_call(
        body, name="attn_fwd", grid=(n_kv, seq // ATTN_BLOCK),
        out_shape=(jax.ShapeDtypeStruct((seq, dm), F32), jax.ShapeDtypeStruct((seq, dm), BF16)),
        in_specs=[q_spec, prev, cur, prev, cur, q_spec, pl.BlockSpec(memory_space=pltpu.SMEM)],
        out_specs=(q_spec, q_spec),
        compiler_params=_params("parallel", "parallel"),
    )(qs, kd, kd, vd, vd, zb, sinks)


def _attn_backward(qs, kd, vd, zb, att, dyb, sinks, tables):
    seq, dm = qs.shape
    n_kv = kd.shape[0]
    gw = GQA_GROUP * HEAD_DIM
    n_blocks = seq // ATTN_BLOCK

    def body(q_ref, kp_ref, kc_ref, vp_ref, vc_ref, z_ref, att_ref, dyb_ref, sink_ref, cos_ref, sa_ref, sb_ref,
             dh_ref, dk_ref, dv_ref, ds_ref):
        kh, n = pl.program_id(0), pl.program_id(1)

        @pl.when(n == 0)
        def _():
            dk_ref[...] = jnp.zeros_like(dk_ref)
            dv_ref[...] = jnp.zeros_like(dv_ref)

        @pl.when(jnp.logical_and(n == 0, kh == 0))
        def _():
            ds_ref[...] = jnp.zeros_like(ds_ref)

        valid, low = _attn_masks(n)
        head_lane = lax.broadcasted_iota(jnp.int32, (1, LANES), 1)
        kcat = jnp.concatenate([kp_ref[...], kc_ref[...]], axis=0)
        vcat = jnp.concatenate([vp_ref[...], vc_ref[...]], axis=0)
        cos, sa, sb = cos_ref[...], sa_ref[...], sb_ref[...]
        dk = jnp.zeros((2 * ATTN_BLOCK, LANES), F32)
        dv = jnp.zeros((2 * ATTN_BLOCK, LANES), F32)
        dsink = jnp.zeros((1, LANES), F32)
        for j in range(gw // LANES):
            cs = slice(j * LANES, (j + 1) * LANES)
            q2 = q_ref[:, cs]
            z = z_ref[:, cs]
            sz, dsz = _silu_parts(z)
            dy2 = dyb_ref[:, cs]
            o2 = att_ref[:, cs]
            datt = dy2 * sz
            dh_ref[1, :, cs] = (dy2 * o2 * dsz).astype(BF16)
            dq2 = jnp.zeros((ATTN_BLOCK, LANES), F32)
            for half in (0, 1):
                mine = low if half == 0 else jnp.logical_not(low)
                head = kh * GQA_GROUP + 2 * j + half
                qm = jnp.where(mine, q2, 0)
                probs, p_sink = _head_probs(qm, kcat, valid, sink_ref[0, head])
                dm_ = jnp.where(mine, datt, 0.0)
                row_dot = jnp.sum(dm_ * o2, axis=-1, keepdims=True)
                dmb = dm_.astype(BF16)
                dprobs = lax.dot_general(dmb, vcat, NT, preferred_element_type=F32)
                dsb = (probs * (dprobs - row_dot)).astype(BF16)
                dq2 = dq2 + jnp.dot(dsb, jnp.where(mine, kcat, 0), preferred_element_type=F32)
                dk = dk + lax.dot_general(dsb, qm, TN, preferred_element_type=F32)
                dv = dv + lax.dot_general(probs.astype(BF16), dmb, TN, preferred_element_type=F32)
                dsink = dsink - jnp.where(head_lane == head, _col_sum(p_sink * row_dot), 0.0)
            dh_ref[0, :, cs] = (_rope_transposed(dq2, cos, sa, sb) * 0.125).astype(BF16)
        ds_ref[0:1, :] += dsink

        @pl.when(n == 0)
        def _():
            dk_ref[pl.ds(0, ATTN_BLOCK), :] += dk[ATTN_BLOCK:, :]
            dv_ref[pl.ds(0, ATTN_BLOCK), :] += dv[ATTN_BLOCK:, :]

        @pl.when(n > 0)
        def _():
            start = pl.multiple_of((n - 1) * ATTN_BLOCK, ATTN_BLOCK)
            dk_ref[pl.ds(start, 2 * ATTN_BLOCK), :] += dk
            dv_ref[pl.ds(start, 2 * ATTN_BLOCK), :] += dv

    q_spec, cur, prev = _attn_specs(gw)
    tab_spec = pl.BlockSpec((ATTN_BLOCK, LANES), lambda kh, n: (n, 0))
    acc_spec = pl.BlockSpec((None, seq, LANES), lambda kh, n: (kh, 0, 0))
    acc_shape = jax.ShapeDtypeStruct((n_kv, seq, LANES), F32)
    return pl.pallas_call(
        body, name="attn_bwd", grid=(n_kv, n_blocks),
        out_shape=(jax.ShapeDtypeStruct((2, seq, dm), BF16), acc_shape, acc_shape,
                   jax.ShapeDtypeStruct((STAT_ROWS, LANES), F32)),
        in_specs=[q_spec, prev, cur, prev, cur, q_spec, q_spec, q_spec, pl.BlockSpec(memory_space=pltpu.SMEM),
                  tab_spec, tab_spec, tab_spec],
        out_specs=(pl.BlockSpec((2, ATTN_BLOCK, gw), lambda kh, n: (0, n, kh)), acc_spec, acc_spec,
                   pl.BlockSpec((STAT_ROWS, LANES), lambda kh, n: (0, 0))),
        compiler_params=_params("arbitrary", "arbitrary"),
    )(qs, kd, kd, vd, vd, zb, att, dyb, sinks, *tables)


def _kv_grad_fold(dk, dv, tables):
    n_kv, seq, _ = dk.shape
    kvw = n_kv * HEAD_DIM
    tile = _tile(seq, 512)

    def body(dk_ref, dv_ref, cos_ref, sa_ref, sb_ref, o_ref):
        low = lax.broadcasted_iota(jnp.int32, (1, LANES), 1) < HEAD_DIM
        cos, sa, sb = cos_ref[...], sa_ref[...], sb_ref[...]

        def folded(ref, h):
            t = ref[h]
            return t + pltpu.roll(t, HEAD_DIM, 1)

        for j in range(n_kv // 2):
            ka = _rope_transposed(folded(dk_ref, 2 * j), cos, sa, sb)
            kb = _rope_transposed(folded(dk_ref, 2 * j + 1), cos, sa, sb)
            o_ref[:, j * LANES:(j + 1) * LANES] = jnp.where(low, ka, kb).astype(BF16)
            o_ref[:, kvw + j * LANES:kvw + (j + 1) * LANES] = jnp.where(
                low, folded(dv_ref, 2 * j), folded(dv_ref, 2 * j + 1)).astype(BF16)

    tab_spec = pl.BlockSpec((tile, LANES), lambda i: (i, 0))
    in_spec = pl.BlockSpec((n_kv, tile, LANES), lambda i: (0, i, 0))
    return pl.pallas_call(
        body, name="kv_grad_fold", grid=(seq // tile,),
        out_shape=jax.ShapeDtypeStruct((seq, 2 * kvw), BF16),
        in_specs=[in_spec, in_spec, tab_spec, tab_spec, tab_spec],
        out_specs=pl.BlockSpec((tile, 2 * kvw), lambda i: (i, 0)),
        compiler_params=_params("parallel"),
    )(dk, dv, *tables)


def _out_proj_norm_loss(yb, w, xhat1, gain0, bias0, gain1, bias1, target):
    seq, dm = xhat1.shape
    tile = _tile(seq, 256)

    def body(y_ref, w_ref, xh1_ref, g0_ref, b0_ref, g1_ref, b1_ref, t_ref, dr_ref, drb_ref, st_ref):
        i = pl.program_id(0)

        @pl.when(i == 0)
        def _():
            st_ref[...] = jnp.zeros_like(st_ref)

        ob = jnp.dot(y_ref[...], w_ref[...], preferred_element_type=F32)
        x1 = xh1_ref[...] * g0_ref[...] + b0_ref[...]
        xhat, rstd = _layer_norm_stats(ALPHA * x1 + ob)
        err = xhat * g1_ref[...] + b1_ref[...] - t_ref[...]
        dout = err * (1.0 / dm)
        dr = _layer_norm_backward(dout, xhat, rstd, g1_ref[...])
        dr_ref[...] = dr
        drb_ref[...] = dr.astype(BF16)
        st_ref[0:1, :] += _col_sum(dout * xhat)
        st_ref[1:2, :] += _col_sum(dout)
        st_ref[2:3, :] += _col_sum(err * err)

    row_spec = pl.BlockSpec((tile, dm), lambda i: (i, 0))
    vec_spec = pl.BlockSpec((1, dm), lambda i: (0, 0))
    return pl.pallas_call(
        body, name="out_proj_norm_loss_b", grid=(seq // tile,),
        out_shape=(jax.ShapeDtypeStruct((seq, dm), F32), jax.ShapeDtypeStruct((seq, dm), BF16),
                   jax.ShapeDtypeStruct((STAT_ROWS, dm), F32)),
        in_specs=[row_spec, pl.BlockSpec(w.shape, lambda i: (0, 0)), row_spec, vec_spec, vec_spec, vec_spec,
                  vec_spec, row_spec],
        out_specs=(row_spec, row_spec, pl.BlockSpec((STAT_ROWS, dm), lambda i: (0, 0))),
        compiler_params=_params("arbitrary"),
    )(yb, w, xhat1, gain0, bias0, gain1, bias1, target)


def _stream_grad_norm_backward(dhq, wqg, dkv, wkv, dr2, xhat1, rstd1, gain0):
    seq, dm = dr2.shape
    tile = _tile(seq, 256)

    def body(dh_ref, wqg_ref, dkv_ref, wkv_ref, dr2_ref, xh_ref, rstd_ref, g_ref, dr_ref, drb_ref, st_ref, acc):
        i, k = pl.program_id(0), pl.program_id(1)
        part = lax.dot_general(dh_ref[...], wqg_ref[...], NT, preferred_element_type=F32)

        @pl.when(k == 0)
        def _():
            acc[...] = part + lax.dot_general(dkv_ref[...], wkv_ref[...], NT, preferred_element_type=F32)

        @pl.when(jnp.logical_and(k == 1, i == 0))
        def _():
            st_ref[...] = jnp.zeros_like(st_ref)

        @pl.when(k == 1)
        def _():
            dx1 = acc[...] + part + ALPHA * dr2_ref[...]
            xhat = xh_ref[...]
            dr = _layer_norm_backward(dx1, xhat, rstd_ref[...], g_ref[...])
            dr_ref[...] = dr
            drb_ref[...] = dr.astype(BF16)
            st_ref[0:1, :] += _col_sum(dx1 * xhat)
            st_ref[1:2, :] += _col_sum(dx1)

    row_spec = pl.BlockSpec((tile, dm), lambda i, k: (i, 0))
    return pl.pallas_call(
        body, name="stream_grad_norm_bwd", grid=(seq // tile, 2),
        out_shape=(jax.ShapeDtypeStruct((seq, dm), F32), jax.ShapeDtypeStruct((seq, dm), BF16),
                   jax.ShapeDtypeStruct((STAT_ROWS, dm), F32)),
        in_specs=[pl.BlockSpec((None, tile, dm), lambda i, k: (k, i, 0)),
                  pl.BlockSpec((dm, dm), lambda i, k: (0, k)),
                  pl.BlockSpec((tile, dkv.shape[1]), lambda i, k: (i, 0)),
                  pl.BlockSpec(wkv.shape, lambda i, k: (0, 0)),
                  row_spec, row_spec, pl.BlockSpec((tile, 1), lambda i, k: (i, 0)),
                  pl.BlockSpec((1, dm), lambda i, k: (0, 0))],
        out_specs=(row_spec, row_spec, pl.BlockSpec((STAT_ROWS, dm), lambda i, k: (0, 0))),
        scratch_shapes=[pltpu.VMEM((tile, dm), F32)],
        compiler_params=_params("arbitrary", "arbitrary"),
    )(dhq, wqg, dkv, wkv, dr2, xhat1, rstd1, gain0)


def _adamw_math(w, g, m, v):
    m = ADAM_B1 * m + (1.0 - ADAM_B1) * g
    v = ADAM_B2 * v + (1.0 - ADAM_B2) * (g * g)
    m_hat = m / (1.0 - ADAM_B1 ** ADAM_STEP)
    v_hat = v / (1.0 - ADAM_B2 ** ADAM_STEP)
    delta = -ADAM_LR * (m_hat / (jnp.sqrt(v_hat) + ADAM_EPS) + ADAM_WD * w)
    return delta, m, v


def _sum_devices(ref):
    total = ref[0].astype(F32)
    for d in range(1, N_DEV):
        total = total + ref[d].astype(F32)
    return total


def _adamw_shard(name, parts, w, m, v):
    rows, cols = w.shape
    tr = _tile(rows, max(8, (1 << 18) // cols)) if rows >= 8 else rows

    def body(p_ref, w_ref, m_ref, v_ref, g_out, d_out, m_out, v_out):
        g = _sum_devices(p_ref)
        delta, m_new, v_new = _adamw_math(w_ref[...], g, m_ref[...], v_ref[...])
        g_out[...] = g
        d_out[...] = delta
        m_out[...] = m_new
        v_out[...] = v_new

    spec = pl.BlockSpec((tr, cols), lambda i: (i, 0))
    shape = jax.ShapeDtypeStruct((rows, cols), F32)
    return pl.pallas_call(
        body, name=name, grid=(rows // tr,),
        out_shape=(shape, shape, shape, shape),
        in_specs=[pl.BlockSpec((N_DEV, tr, cols), lambda i: (0, i, 0)), spec, spec, spec],
        out_specs=(spec, spec, spec, spec),
        compiler_params=_params("parallel"),
    )(parts, w, m, v)


def _adamw_replicated(stats_b, stats_a, sink_parts, ln_g, ln_b, sinks, m_ln_g, m_ln_b, m_sinks, v_ln_g, v_ln_b,
                      v_sinks):
    n_q = sinks.shape[1]

    def body(sb_ref, sa_ref, sk_ref, g_ref, b_ref, s_ref, mg_ref, mb_ref, ms_ref, vg_ref, vb_ref, vs_ref, *outs):
        layer_sums = (_sum_devices(sa_ref), _sum_devices(sb_ref))
        for which, (w_ref, m_ref, v_ref) in enumerate(((g_ref, mg_ref, vg_ref), (b_ref, mb_ref, vb_ref))):
            for layer in range(DEPTH):
                row = slice(layer, layer + 1)
                g = layer_sums[layer][which:which + 1, :]
                res = (g,) + _adamw_math(w_ref[row, :], g, m_ref[row, :], v_ref[row, :])
                for o_ref, val in zip(outs[4 * which:4 * which + 4], res):
                    o_ref[row, :] = val
        g = _sum_devices(sk_ref)[0:1, 0:n_q]
        res = (g,) + _adamw_math(s_ref[...], g, ms_ref[...], vs_ref[...])
        for o_ref, val in zip(outs[8:12], res):
            o_ref[...] = val

    vmem = pl.BlockSpec(memory_space=pltpu.VMEM)
    shapes = [jax.ShapeDtypeStruct(a.shape, F32) for a in (ln_g, ln_b, sinks) for _ in range(4)]
    return pl.pallas_call(
        body, name="adamw_replicated", out_shape=tuple(shapes),
        in_specs=[vmem] * 12, out_specs=tuple([vmem] * 12),
    )(stats_b, stats_a, sink_parts, ln_g, ln_b, sinks, m_ln_g, m_ln_b, m_sinks, v_ln_g, v_ln_b, v_sinks)


def kernel(x, ln_g, ln_b, a_w_in, a_w_group, a_scale, a_w_out, b_w_k, b_w_v, b_w_qg, b_sinks, b_w_out, loss_target, m_ln_g, m_ln_b, m_a_w_in, m_a_w_group, m_a_scale, m_a_w_out, m_b_w_k, m_b_w_v, m_b_w_qg, m_b_sinks, m_b_w_out, v_ln_g, v_ln_b, v_a_w_in, v_a_w_group, v_a_scale, v_a_w_out, v_b_w_k, v_b_w_v, v_b_w_qg, v_b_sinks, v_b_w_out):
    _, seq, dm = x.shape
    n_groups = len(POOL_WINDOWS)
    gd = dm // n_groups
    kvw = b_w_k.shape[1]
    cb = 2 * dm // N_DEV
    rb = dm // N_DEV
    gb = gd // N_DEV

    x2 = x.reshape(seq, dm)
    target = loss_target.reshape(seq, dm)
    w_in_s = a_w_in.reshape(dm, cb)
    w_g_s = a_w_group.reshape(n_groups, gb, gd)
    w_out_s = a_w_out.reshape(rb, dm)
    w_qg_s = b_w_qg.reshape(dm, cb)
    w_outb_s = b_w_out.reshape(rb, dm)

    def cols(ref, dev):
        return ref.at[:, pl.ds(pl.multiple_of(dev * cb, LANES), cb)]

    def rows(ref, dev):
        return ref.at[pl.ds(pl.multiple_of(dev * rb, 8), rb), :]

    def group_rows(ref, dev):
        return ref.at[:, pl.ds(pl.multiple_of(dev * gb, 8), gb), :]

    def k_rows(ref, dev):
        return ref.at[pl.ds(pl.multiple_of(dev * rb, 8), rb), pl.ds(0, kvw)]

    def v_rows(ref, dev):
        return ref.at[pl.ds(pl.multiple_of(dev * rb, 8), rb), pl.ds(kvw, kvw)]

    def scale_cols(ref, dev):
        return ref.at[:, pl.ds(pl.multiple_of(dev * rb, LANES), rb)]

    bf = lambda a: a.astype(BF16)
    w_in, w_g, w_out, w_kv, w_qg, w_outb, scale = _gather_weights(
        [(bf(w_in_s), 0, cols), (bf(w_g_s), 1, group_rows), (bf(w_out_s), 2, rows), (bf(b_w_k), 3, k_rows),
         (bf(b_w_v), 3, v_rows), (bf(w_qg_s), 4, cols), (bf(w_outb_s), 5, rows), (a_scale, 6, scale_cols)],
        [jax.ShapeDtypeStruct((dm, 2 * dm), BF16), jax.ShapeDtypeStruct((n_groups, gd, gd), BF16),
         jax.ShapeDtypeStruct((dm, dm), BF16), jax.ShapeDtypeStruct((dm, 2 * kvw), BF16),
         jax.ShapeDtypeStruct((dm, 2 * dm), BF16), jax.ShapeDtypeStruct((dm, dm), BF16),
         jax.ShapeDtypeStruct((1, dm), F32)])

    tables = _rope_tables(seq)
    bm = _tile(seq, 1024)
    bn = _tile(dm, 1024)
    g0, g1, b0, b1 = ln_g[0:1], ln_g[1:2], ln_b[0:1], ln_b[1:2]

    xb = _cast_bf16("cast_x", x2)
    h = _mm("a_in_proj", xb, w_in, dims=NN, grid=(seq // bm, 2 * dm // bn),
            a_spec=pl.BlockSpec((bm, dm), lambda i, j: (i, 0)), b_spec=pl.BlockSpec((dm, bn), lambda i, j: (0, j)),
            out_shape=jax.ShapeDtypeStruct((seq, 2 * dm), F32), out_spec=pl.BlockSpec((bm, bn), lambda i, j: (i, j)))
    y, pooled, mixed = _pool_mid_forward(h, w_g, scale)
    xhat1, rstd1, x1b = _out_proj_norm(y, w_out, x2, g0, b0)

    kd, vd = _kv_proj(x1b, w_kv, tables)
    bmq = _tile(seq, 512)
    tab_spec = pl.BlockSpec((bmq, LANES), lambda i, j: (i, 0))

    def rope_scale(val, cos_ref, sa_ref, sb_ref):
        cos, sa, sb = cos_ref[...], sa_ref[...], sb_ref[...]
        return jnp.concatenate([_rope(val[:, j * LANES:(j + 1) * LANES], cos, sa, sb) * 0.125
                                for j in range(val.shape[1] // LANES)], axis=1)

    qs = _mm("b_q_proj", x1b, w_qg, dims=NN, grid=(seq // bmq, dm // bn),
             a_spec=pl.BlockSpec((bmq, dm), lambda i, j: (i, 0)), b_spec=pl.BlockSpec((dm, bn), lambda i, j: (0, j)),
             out_shape=jax.ShapeDtypeStruct((seq, dm), BF16), out_spec=pl.BlockSpec((bmq, bn), lambda i, j: (i, j)),
             epilogue=rope_scale, extras=tables, extra_specs=(tab_spec,) * 3)
    zb = _mm("b_gate_proj", x1b, w_qg, dims=NN, grid=(seq // bm, dm // bn),
             a_spec=pl.BlockSpec((bm, dm), lambda i, j: (i, 0)),
             b_spec=pl.BlockSpec((dm, bn), lambda i, j: (0, j + dm // bn)),
             out_shape=jax.ShapeDtypeStruct((seq, dm), F32), out_spec=pl.BlockSpec((bm, bn), lambda i, j: (i, j)))
    att, yb = _attn_forward(qs, kd, vd, zb, b_sinks)
    dr2, dr2b, stats_b = _out_proj_norm_loss(yb, w_outb, xhat1, g0, b0, g1, b1, target)

    def weight_grad(name, a, b, n_cols, b_spec=None):
        m_cols = a.shape[1]
        tm, tn = _tile(m_cols, 1024), _tile(n_cols, 512)
        return _mm(name, a, b, dims=TN, grid=(m_cols // tm, n_cols // tn),
                   a_spec=pl.BlockSpec((seq, tm), lambda i, j: (0, i)),
                   b_spec=b_spec(tn) if b_spec else pl.BlockSpec((seq, tn), lambda i, j: (0, j)),
                   out_shape=jax.ShapeDtypeStruct((m_cols, n_cols), BF16),
                   out_spec=pl.BlockSpec((tm, tn), lambda i, j: (i, j)))

    def halves_spec(tn):
        per = dm // tn
        return pl.BlockSpec((None, seq, tn), lambda i, j: (j // per, 0, j % per))

    def times_transposed(name, a, w):
        return _mm(name, a, w, dims=NT, grid=(seq // bm, dm // bn),
                   a_spec=pl.BlockSpec((bm, a.shape[1]), lambda i, j: (i, 0)),
                   b_spec=pl.BlockSpec((bn, w.shape[1]), lambda i, j: (j, 0)),
                   out_shape=jax.ShapeDtypeStruct((seq, dm), F32), out_spec=pl.BlockSpec((bm, bn), lambda i, j: (i, j)))

    dyb = times_transposed("b_out_proj_dx", dr2b, w_outb)
    d_w_outb = weight_grad("b_out_proj_dw", yb, dr2b, dm)
    dhq, dkd, dvd, dsink = _attn_backward(qs, kd, vd, zb, att, dyb, b_sinks, tables)
    dkv = _kv_grad_fold(dkd, dvd, tables)
    d_w_qg = weight_grad("b_qg_proj_dw", x1b, dhq, 2 * dm, halves_spec)
    d_w_kv = weight_grad("b_kv_proj_dw", x1b, dkv, 2 * kvw)
    dr1, dr1b, stats_a = _stream_grad_norm_backward(dhq, w_qg, dkv, w_kv, dr2, xhat1, rstd1, g0)

    dy = times_transposed("a_out_proj_dx", dr1b, w_out)
    d_w_out = weight_grad("a_out_proj_dw", y, dr1b, dm)
    dh, d_w_g, stats_s = _pool_mid_backward(dy, mixed, h, pooled, w_g, scale)
    d_w_in = weight_grad("a_in_proj_dw", xb, dh, 2 * dm, halves_spec)
    grad_x = _mm("a_in_proj_dx", dh, w_in, dims=NT, grid=(seq // bm, dm // bn, 2), nk=2,
                 a_spec=pl.BlockSpec((None, bm, dm), lambda i, j, k: (k, i, 0)),
                 b_spec=pl.BlockSpec((bn, dm), lambda i, j, k: (j, k)),
                 out_shape=jax.ShapeDtypeStruct((seq, dm), F32), out_spec=pl.BlockSpec((bm, bn), lambda i, j, k: (i, j)),
                 add=dr1, add_spec=pl.BlockSpec((bm, bn), lambda i, j, k: (i, j)), add_scale=ALPHA)

    def stat_row_cols(ref, dev):
        return ref.at[pl.ds(0, 1), pl.ds(pl.multiple_of(dev * rb, LANES), rb)]

    p_in, p_g, p_out, p_k, p_v, p_qg, p_outb, p_scale = _exchange_blocks("scatter_weight_grads", [
        (d_w_in, cols, (dm, cb)), (d_w_g, group_rows, (n_groups, gb, gd)), (d_w_out, rows, (rb, dm)),
        (d_w_kv, k_rows, (rb, kvw)), (d_w_kv, v_rows, (rb, kvw)), (d_w_qg, cols, (dm, cb)),
        (d_w_outb, rows, (rb, dm)), (stats_s, stat_row_cols, (1, rb))])
    all_b, all_a, all_sink = _exchange_blocks("gather_replicated_grads", [
        (stats_b, None, stats_b.shape), (stats_a, None, stats_a.shape), (dsink, None, dsink.shape)])

    def shard_update(name, parts, w, m, v):
        shape = w.shape
        flat = lambda a: a.reshape(-1, shape[-1])
        outs = _adamw_shard(name, parts.reshape(N_DEV, -1, shape[-1]), flat(w), flat(m), flat(v))
        return [o.reshape(shape) for o in outs]

    upd = {
        "a_w_in": shard_update("adamw_a_w_in", p_in, a_w_in, m_a_w_in, v_a_w_in),
        "a_w_group": shard_update("adamw_a_w_group", p_g, a_w_group, m_a_w_group, v_a_w_group),
        "a_scale": shard_update("adamw_a_scale", p_scale, a_scale, m_a_scale, v_a_scale),
        "a_w_out": shard_update("adamw_a_w_out", p_out, a_w_out, m_a_w_out, v_a_w_out),
        "b_w_k": shard_update("adamw_b_w_k", p_k, b_w_k, m_b_w_k, v_b_w_k),
        "b_w_v": shard_update("adamw_b_w_v", p_v, b_w_v, m_b_w_v, v_b_w_v),
        "b_w_qg": shard_update("adamw_b_w_qg", p_qg, b_w_qg, m_b_w_qg, v_b_w_qg),
        "b_w_out": shard_update("adamw_b_w_out", p_outb, b_w_out, m_b_w_out, v_b_w_out),
    }
    rep = _adamw_replicated(all_b, all_a, all_sink, ln_g, ln_b, b_sinks, m_ln_g, m_ln_b, m_b_sinks, v_ln_g, v_ln_b,
                            v_b_sinks)
    upd["ln_g"], upd["ln_b"], upd["b_sinks"] = list(rep[0:4]), list(rep[4:8]), list(rep[8:12])

    loss = lax.psum(0.5 * jnp.sum(stats_b[2]) / dm, AXES)
    order = ["ln_g", "ln_b", "a_w_in", "a_w_group", "a_scale", "a_w_out", "b_w_k", "b_w_v", "b_w_qg", "b_sinks",
             "b_w_out"]
    return (loss, grad_x.reshape(x.shape), *[upd[n][0] for n in order], *[upd[n][1] for n in order],
            *[upd[n][2] for n in order], *[upd[n][3] for n in order])
```

```python
import functools

import jax
import jax.numpy as jnp
from jax import lax
from jax.experimental import pallas as pl
from jax.experimental.pallas import tpu as pltpu

F32 = jnp.float32
BF16 = jnp.bfloat16
MESH = pl.DeviceIdType.MESH
AXES = ("x", "y", "c")
N_DEV = 8

POOL_WINDOWS = (2, 4, 8, 16)
POOL_HALO = 16
HEAD_DIM = 64
GQA_GROUP = 8
ATTN_BLOCK = 128
ROPE_THETA = 10000.0
LN_EPS = 1e-5
NEG_INF = -1e30
DEPTH = 2
ALPHA = (2 * DEPTH) ** 0.25
ADAM_LR = 0.001
ADAM_B1 = 0.9
ADAM_B2 = 0.999
ADAM_EPS = 1e-08
ADAM_WD = 0.01
ADAM_STEP = 10

LANES = 128
STAT_ROWS = 8


def _tile(n, want):
    t = min(n, want)
    while n % t:
        t //= 2
    return t


def _params(*sem):
    return pltpu.CompilerParams(dimension_semantics=sem)


ANY = pl.BlockSpec(memory_space=pl.ANY)


def _my_pos():
    return lax.axis_index("x"), lax.axis_index("y"), lax.axis_index("c")


def _dev_index(p):
    return 4 * p[0] + 2 * p[1] + p[2]


def _gather_weights(streams, out_shapes):
    n_s = len(streams)
    n_out = len(out_shapes)

    def body(*refs):
        srcs = refs[:n_s]
        outs = refs[n_s:n_s + n_out]
        send_sems, recv_sems, local_sems = refs[n_s + n_out:]
        x, y, c = _my_pos()
        me, sibling = (x, y, c), (x, y, 1 - c)
        chips = [(1 - x, y), (x, 1 - y), (1 - x, 1 - y)]

        def copy(s, k, block, to, from_shard=False):
            out_ref = outs[streams[s][1]]
            win = streams[s][2](out_ref, _dev_index(block))
            return pltpu.make_async_remote_copy(
                src_ref=srcs[s] if from_shard else win, dst_ref=win,
                send_sem=send_sems.at[7 * s + k], recv_sem=recv_sems.at[7 * s + k],
                device_id=to, device_id_type=MESH)

        mine = [pltpu.make_async_copy(srcs[s], streams[s][2](outs[streams[s][1]], _dev_index(me)), local_sems.at[s])
                for s in range(n_s)]
        for cp in mine:
            cp.start()
        first = []
        for s in range(n_s):
            first.append(copy(s, 0, me, sibling, True))
            first += [copy(s, 1 + j, me, (*chip, c), True) for j, chip in enumerate(chips)]
        for cp in first:
            cp.start()
        passed = []
        for j, chip in enumerate(chips):
            for s in range(n_s):
                copy(s, 1 + j, (*chip, c), me).wait_recv()
                fwd = copy(s, 4 + j, (*chip, c), sibling)
                fwd.start()
                passed.append(fwd)
        for s in range(n_s):
            copy(s, 0, sibling, me).wait_recv()
            for j, chip in enumerate(chips):
                copy(s, 4 + j, (*chip, 1 - c), me).wait_recv()
        for cp in first + passed:
            cp.wait_send()
        for cp in mine:
            cp.wait()

    return pl.pallas_call(
        body, name="gather_weights",
        out_shape=tuple(out_shapes),
        in_specs=[ANY] * n_s, out_specs=tuple([ANY] * n_out),
        scratch_shapes=[pltpu.SemaphoreType.DMA((7 * n_s,)), pltpu.SemaphoreType.DMA((7 * n_s,)),
                        pltpu.SemaphoreType.DMA((n_s,))],
    )(*[s[0] for s in streams])


def _exchange_blocks(name, streams):
    n_s = len(streams)

    def body(*refs):
        srcs = refs[:n_s]
        outs = refs[n_s:2 * n_s]
        send_sems, recv_sems, local_sems = refs[2 * n_s:]
        x, y, c = _my_pos()
        me = _dev_index((x, y, c))

        def window(s, dev):
            return srcs[s] if streams[s][1] is None else streams[s][1](srcs[s], dev)

        mine = [pltpu.make_async_copy(window(s, me), outs[s].at[me], local_sems.at[s]) for s in range(n_s)]
        for cp in mine:
            cp.start()
        copies = []
        for k in (2, 4, 6, 3, 5, 7, 1):
            peer = (1 - x if k & 4 else x, 1 - y if k & 2 else y, 1 - c if k & 1 else c)
            for s in range(n_s):
                copies.append(pltpu.make_async_remote_copy(
                    src_ref=window(s, _dev_index(peer)), dst_ref=outs[s].at[me],
                    send_sem=send_sems.at[7 * s + k - 1], recv_sem=recv_sems.at[7 * s + k - 1],
                    device_id=peer, device_id_type=MESH))
        for cp in copies:
            cp.start()
        for cp in copies:
            cp.wait()
        for cp in mine:
            cp.wait()

    return pl.pallas_call(
        body, name=name,
        out_shape=tuple(jax.ShapeDtypeStruct((N_DEV,) + tuple(s[2]), s[0].dtype) for s in streams),
        in_specs=[ANY] * n_s, out_specs=tuple([ANY] * n_s),
        scratch_shapes=[pltpu.SemaphoreType.DMA((7 * n_s,)), pltpu.SemaphoreType.DMA((7 * n_s,)),
                        pltpu.SemaphoreType.DMA((n_s,))],
    )(*[s[0] for s in streams])


NN = (((1,), (0,)), ((), ()))
NT = (((1,), (1,)), ((), ()))
TN = (((0,), (0,)), ((), ()))


def _mm(name, a, b, *, dims, grid, a_spec, b_spec, out_shape, out_spec, nk=1,
        add=None, add_spec=None, add_scale=1.0, epilogue=None, extras=(), extra_specs=()):
    n_extra = len(extras)
    has_add = add is not None

    def body(*refs):
        a_ref, b_ref = refs[:2]
        pos = 2
        add_ref = None
        if has_add:
            add_ref = refs[pos]
            pos += 1
        extra_refs = refs[pos:pos + n_extra]
        o_ref = refs[pos + n_extra]
        acc_ref = refs[pos + n_extra + 1] if nk > 1 else None

        def finish(val):
            if has_add:
                val = val + add_scale * add_ref[...]
            if epilogue is not None:
                val = epilogue(val, *extra_refs)
            o_ref[...] = val.astype(o_ref.dtype)

        part = lax.dot_general(a_ref[...].astype(BF16), b_ref[...].astype(BF16), dims,
                               preferred_element_type=F32)
        if nk == 1:
            finish(part)
        else:
            k = pl.program_id(2)

            @pl.when(k == 0)
            def _():
                acc_ref[...] = part

            @pl.when(jnp.logical_and(k > 0, k < nk - 1))
            def _():
                acc_ref[...] += part

            @pl.when(k == nk - 1)
            def _():
                finish(acc_ref[...] + part)

    in_specs = [a_spec, b_spec] + ([add_spec] if has_add else []) + list(extra_specs)
    operands = [a, b] + ([add] if has_add else []) + list(extras)
    scratch = [pltpu.VMEM(out_spec.block_shape, F32)] if nk > 1 else []
    sem = ("parallel", "parallel") + (("arbitrary",) if nk > 1 else ())
    return pl.pallas_call(
        body, name=name, grid=grid, out_shape=out_shape,
        in_specs=in_specs, out_specs=out_spec, scratch_shapes=scratch,
        compiler_params=_params(*sem),
    )(*operands)


def _cast_bf16(name, a):
    rows, cols = a.shape
    tr = _tile(rows, 512)

    def body(a_ref, o_ref):
        o_ref[...] = a_ref[...].astype(BF16)

    return pl.pallas_call(
        body, name=name, grid=(rows // tr,),
        out_shape=jax.ShapeDtypeStruct(a.shape, BF16),
        in_specs=[pl.BlockSpec((tr, cols), lambda i: (i, 0))],
        out_specs=pl.BlockSpec((tr, cols), lambda i: (i, 0)),
        compiler_params=_params("parallel"),
    )(a)


def _rope_tables(seq):
    inv_freq = ROPE_THETA ** (-jnp.arange(0, HEAD_DIM, 2, dtype=F32) / HEAD_DIM)
    ang = jnp.arange(seq, dtype=F32)[:, None] * inv_freq[None, :]
    ang = jnp.concatenate([ang, ang, ang, ang], axis=-1)
    cos, sin = jnp.cos(ang), jnp.sin(ang)
    first_half = (jnp.arange(LANES) % HEAD_DIM < HEAD_DIM // 2)[None, :]
    return cos, jnp.where(first_half, -sin, 0.0), jnp.where(first_half, 0.0, sin)


def _rot(t, sin_a, sin_b):
    return pltpu.roll(t, LANES - HEAD_DIM // 2, 1) * sin_a + pltpu.roll(t, HEAD_DIM // 2, 1) * sin_b


def _rope(t, cos, sin_a, sin_b):
    return t * cos + _rot(t, sin_a, sin_b)


def _rope_transposed(dy, cos, sin_a, sin_b):
    return dy * cos - _rot(dy, sin_a, sin_b)


def _silu_parts(z):
    sig = jax.nn.sigmoid(z)
    return z * sig, sig * (1.0 + z * (1.0 - sig))


def _layer_norm_stats(r):
    mu = jnp.mean(r, axis=-1, keepdims=True)
    d = r - mu
    var = jnp.mean(d * d, axis=-1, keepdims=True)
    rstd = lax.rsqrt(var + LN_EPS)
    return d * rstd, rstd


def _layer_norm_backward(dout, xhat, rstd, gain):
    dxh = dout * gain
    m1 = jnp.mean(dxh, axis=-1, keepdims=True)
    m2 = jnp.mean(dxh * xhat, axis=-1, keepdims=True)
    return rstd * (dxh - m1 - xhat * m2)


def _col_sum(v):
    return jnp.sum(v, axis=0, keepdims=True)


def _pool_mid_forward(h, wg, scale):
    seq, d2 = h.shape
    dm = d2 // 2
    gd = dm // len(POOL_WINDOWS)
    tile = _tile(seq, 256)
    halo_blocks = tile // POOL_HALO

    def body(u_ref, up_ref, z_ref, wg_ref, sc_ref, y_ref, p_ref, mx_ref):
        i = pl.program_id(0)
        row = i * tile + lax.broadcasted_iota(jnp.int32, (tile, 1), 0)
        count = (row + 1).astype(F32)
        for g, w in enumerate(POOL_WINDOWS):
            cs = slice(g * gd, (g + 1) * gd)
            u = u_ref[:, cs]
            prev = jnp.where(i > 0, up_ref[:, cs], 0.0)
            s = jnp.concatenate([prev, u], axis=0)
            sh = 1
            while sh < w:
                s = s + pltpu.roll(s, sh, 0)
                sh *= 2
            p = s[POOL_HALO:, :] * (1.0 / jnp.minimum(count, float(w))) - u
            pb = p.astype(BF16)
            mx = jnp.dot(pb, wg_ref[g], preferred_element_type=F32)
            z = z_ref[:, cs]
            y = mx * sc_ref[:, cs] * (z * jax.nn.sigmoid(z))
            y_ref[:, cs] = y.astype(BF16)
            p_ref[:, cs] = pb
            mx_ref[:, cs] = mx

    row_spec = pl.BlockSpec((tile, dm), lambda i: (i, 0))
    return pl.pallas_call(
        body, name="pool_mid_fwd", grid=(seq // tile,),
        out_shape=(jax.ShapeDtypeStruct((seq, dm), BF16), jax.ShapeDtypeStruct((seq, dm), BF16),
                   jax.ShapeDtypeStruct((seq, dm), F32)),
        in_specs=[row_spec,
                  pl.BlockSpec((POOL_HALO, dm), lambda i: (jnp.maximum(i * halo_blocks - 1, 0), 0)),
                  pl.BlockSpec((tile, dm), lambda i: (i, 1)),
                  pl.BlockSpec(wg.shape, lambda i: (0, 0, 0)),
                  pl.BlockSpec((1, dm), lambda i: (0, 0))],
        out_specs=(row_spec, row_spec, row_spec),
        compiler_params=_params("parallel"),
    )(h, h, h, wg, scale)


def _pool_mid_backward(dy, mx, h, p, wg, scale):
    seq, dm = dy.shape
    gd = dm // len(POOL_WINDOWS)
    tile = _tile(seq, 256)
    n_i = seq // tile

    def body(dy_ref, mx_ref, z_ref, p_ref, wg_ref, sc_ref, dh_ref, dwg_ref, st_ref, dwg_acc, carry):
        i = pl.program_id(0)
        ti = n_i - 1 - i

        @pl.when(i == 0)
        def _():
            dwg_acc[...] = jnp.zeros_like(dwg_acc)
            carry[...] = jnp.zeros_like(carry)
            st_ref[...] = jnp.zeros_like(st_ref)

        row = ti * tile + lax.broadcasted_iota(jnp.int32, (tile, 1), 0)
        count = (row + 1).astype(F32)
        for g, w in enumerate(POOL_WINDOWS):
            cs = slice(g * gd, (g + 1) * gd)
            z = z_ref[:, cs]
            sz, dsz = _silu_parts(z)
            dyg = dy_ref[:, cs]
            mxg = mx_ref[:, cs]
            sc = sc_ref[:, cs]
            t1 = dyg * sz
            st_ref[0:1, cs] += _col_sum(t1 * mxg)
            dh_ref[1, :, cs] = (dyg * (mxg * sc) * dsz).astype(BF16)
            dmx = (t1 * sc).astype(BF16)
            dwg_acc[g] += lax.dot_general(p_ref[:, cs], dmx, TN, preferred_element_type=F32)
            dp = lax.dot_general(dmx, wg_ref[g], NT, preferred_element_type=F32)
            e = dp * (1.0 / jnp.minimum(count, float(w)))
            s = jnp.concatenate([e, carry[:, cs]], axis=0)
            n = tile + POOL_HALO
            sh = 1
            while sh < w:
                s = s + pltpu.roll(s, n - sh, 0)
                sh *= 2
            dh_ref[0, :, cs] = (s[:tile, :] - dp).astype(BF16)
            carry[:, cs] = e[:POOL_HALO, :]

        @pl.when(i == n_i - 1)
        def _():
            dwg_ref[...] = dwg_acc[...].astype(BF16)

    row_spec = pl.BlockSpec((tile, dm), lambda i: (n_i - 1 - i, 0))
    return pl.pallas_call(
        body, name="pool_mid_bwd", grid=(n_i,),
        out_shape=(jax.ShapeDtypeStruct((2, seq, dm), BF16), jax.ShapeDtypeStruct(wg.shape, BF16),
                   jax.ShapeDtypeStruct((STAT_ROWS, dm), F32)),
        in_specs=[row_spec, row_spec,
                  pl.BlockSpec((tile, dm), lambda i: (n_i - 1 - i, 1)),
                  row_spec,
                  pl.BlockSpec(wg.shape, lambda i: (0, 0, 0)),
                  pl.BlockSpec((1, dm), lambda i: (0, 0))],
        out_specs=(pl.BlockSpec((2, tile, dm), lambda i: (0, n_i - 1 - i, 0)),
                   pl.BlockSpec(wg.shape, lambda i: (0, 0, 0)),
                   pl.BlockSpec((STAT_ROWS, dm), lambda i: (0, 0))),
        scratch_shapes=[pltpu.VMEM(wg.shape, F32), pltpu.VMEM((POOL_HALO, dm), F32)],
        compiler_params=_params("arbitrary"),
    )(dy, mx, h, p, wg, scale)


def _out_proj_norm(y, w, x, gain, bias):
    seq, dm = x.shape
    tile = _tile(seq, 256)

    def body(y_ref, w_ref, x_ref, g_ref, b_ref, xhat_ref, rstd_ref, xb_ref):
        o = jnp.dot(y_ref[...], w_ref[...], preferred_element_type=F32)
        xhat, rstd = _layer_norm_stats(ALPHA * x_ref[...] + o)
        xhat_ref[...] = xhat
        rstd_ref[...] = rstd
        xb_ref[...] = (xhat * g_ref[...] + b_ref[...]).astype(BF16)

    row_spec = pl.BlockSpec((tile, dm), lambda i: (i, 0))
    vec_spec = pl.BlockSpec((1, dm), lambda i: (0, 0))
    return pl.pallas_call(
        body, name="out_proj_norm_a", grid=(seq // tile,),
        out_shape=(jax.ShapeDtypeStruct((seq, dm), F32), jax.ShapeDtypeStruct((seq, 1), F32),
                   jax.ShapeDtypeStruct((seq, dm), BF16)),
        in_specs=[row_spec, pl.BlockSpec(w.shape, lambda i: (0, 0)), row_spec, vec_spec, vec_spec],
        out_specs=(row_spec, pl.BlockSpec((tile, 1), lambda i: (i, 0)), row_spec),
        compiler_params=_params("parallel"),
    )(y, w, x, gain, bias)


def _kv_proj(xb, wkv, tables):
    seq, dm = xb.shape
    kvw = wkv.shape[1] // 2
    n_kv = kvw // HEAD_DIM
    tile = _tile(seq, 512)

    def body(x_ref, w_ref, cos_ref, sa_ref, sb_ref, kd_ref, vd_ref, kt_ref, vt_ref):
        kv = jnp.dot(x_ref[...], w_ref[...], preferred_element_type=F32)
        low = lax.broadcasted_iota(jnp.int32, (1, LANES), 1) < HEAD_DIM
        cos, sa, sb = cos_ref[...], sa_ref[...], sb_ref[...]

        def put(pair, h, nat_ref, t_ref):
            swapped = pltpu.roll(pair, HEAD_DIM, 1)
            for head, dup in ((h, jnp.where(low, pair, swapped)), (h + 1, jnp.where(low, swapped, pair))):
                nat_ref[head] = dup.astype(BF16)
                t_ref[head] = dup.T.astype(BF16)

        for j in range(kvw // LANES):
            put(_rope(kv[:, j * LANES:(j + 1) * LANES], cos, sa, sb), 2 * j, kd_ref, kt_ref)
            put(kv[:, kvw + j * LANES:kvw + (j + 1) * LANES], 2 * j, vd_ref, vt_ref)

    tab_spec = pl.BlockSpec((tile, LANES), lambda i: (i, 0))
    dup_spec = pl.BlockSpec((n_kv, tile, LANES), lambda i: (0, i, 0))
    dup_shape = jax.ShapeDtypeStruct((n_kv, seq, LANES), BF16)
    t_spec = pl.BlockSpec((n_kv, LANES, tile), lambda i: (0, 0, i))
    t_shape = jax.ShapeDtypeStruct((n_kv, LANES, seq), BF16)
    return pl.pallas_call(
        body, name="kv_proj", grid=(seq // tile,),
        out_shape=(dup_shape, dup_shape, t_shape, t_shape),
        in_specs=[pl.BlockSpec((tile, dm), lambda i: (i, 0)), pl.BlockSpec(wkv.shape, lambda i: (0, 0)),
                  tab_spec, tab_spec, tab_spec],
        out_specs=(dup_spec, dup_spec, t_spec, t_spec),
        compiler_params=_params("parallel"),
    )(xb, wkv, *tables)


def _head_queries(q_ref, low):
    parts = []
    for j in range(GQA_GROUP // 2):
        q2 = q_ref[:, j * LANES:(j + 1) * LANES]
        parts += [jnp.where(low, q2, 0), jnp.where(low, 0, q2)]
    return parts


def _probs_transposed(n, kh, kcat, q_all, sink_ref):
    st = lax.dot_general(kcat, q_all, NT, preferred_element_type=F32)
    key = lax.broadcasted_iota(jnp.int32, (2 * ATTN_BLOCK, ATTN_BLOCK), 0)
    qry = lax.broadcasted_iota(jnp.int32, (2 * ATTN_BLOCK, ATTN_BLOCK), 1)
    valid = (key > qry) & (key <= qry + ATTN_BLOCK) & ((key >= ATTN_BLOCK) | (n > 0))
    st = st + jnp.tile(jnp.where(valid, 0.0, NEG_INF), (1, GQA_GROUP))
    sink = jnp.concatenate([jnp.full((1, ATTN_BLOCK), sink_ref[0, kh * GQA_GROUP + h], F32)
                            for h in range(GQA_GROUP)], axis=1)
    m = jnp.maximum(jnp.max(st, axis=0, keepdims=True), sink)
    e = jnp.exp(st - m)
    e_sink = jnp.exp(sink - m)
    inv = 1.0 / (jnp.sum(e, axis=0, keepdims=True) + e_sink)
    return e * inv, e_sink * inv


def _attn_specs(n_width):
    q_spec = pl.BlockSpec((ATTN_BLOCK, n_width), lambda kh, n: (n, kh))
    cur = pl.BlockSpec((None, ATTN_BLOCK, LANES), lambda kh, n: (kh, n, 0))
    prev = pl.BlockSpec((None, ATTN_BLOCK, LANES), lambda kh, n: (kh, jnp.maximum(n - 1, 0), 0))
    cur_t = pl.BlockSpec((None, LANES, ATTN_BLOCK), lambda kh, n: (kh, 0, n))
    prev_t = pl.BlockSpec((None, LANES, ATTN_BLOCK), lambda kh, n: (kh, 0, jnp.maximum(n - 1, 0)))
    return q_spec, cur, prev, cur_t, prev_t


def _pair_product_transposed(mat_t, rhs, j, low_rows):
    a = rhs[:, 2 * j * ATTN_BLOCK:(2 * j + 1) * ATTN_BLOCK]
    b = rhs[:, (2 * j + 1) * ATTN_BLOCK:(2 * j + 2) * ATTN_BLOCK]
    out_t = (jnp.dot(jnp.where(low_rows, mat_t, 0), a, preferred_element_type=F32)
             + jnp.dot(jnp.where(low_rows, 0, mat_t), b, preferred_element_type=F32))
    return out_t.T


def _attn_forward(qs, kd, vt, zb, sinks):
    seq, dm = qs.shape
    n_kv = kd.shape[0]
    gw = GQA_GROUP * HEAD_DIM

    def body(q_ref, kp_ref, kc_ref, vtp_ref, vtc_ref, z_ref, sink_ref, att_ref, yb_ref):
        kh, n = pl.program_id(0), pl.program_id(1)
        low = lax.broadcasted_iota(jnp.int32, (1, LANES), 1) < HEAD_DIM
        low_rows = lax.broadcasted_iota(jnp.int32, (LANES, 1), 0) < HEAD_DIM
        kcat = jnp.concatenate([kp_ref[...], kc_ref[...]], axis=0)
        vt = jnp.concatenate([vtp_ref[...], vtc_ref[...]], axis=1)
        q_all = jnp.concatenate(_head_queries(q_ref, low), axis=0)
        probs_t, _ = _probs_transposed(n, kh, kcat, q_all, sink_ref)
        pt = probs_t.astype(BF16)
        for j in range(GQA_GROUP // 2):
            cs = slice(j * LANES, (j + 1) * LANES)
            o2 = _pair_product_transposed(vt, pt, j, low_rows)
            att_ref[:, cs] = o2
            z = z_ref[:, cs]
            yb_ref[:, cs] = (o2 * (z * jax.nn.sigmoid(z))).astype(BF16)

    q_spec, cur, prev, cur_t, prev_t = _attn_specs(gw)
    return pl.pallas_call(
        body, name="attn_fwd", grid=(n_kv, seq // ATTN_BLOCK),
        out_shape=(jax.ShapeDtypeStruct((seq, dm), F32), jax.ShapeDtypeStruct((seq, dm), BF16)),
        in_specs=[q_spec, prev, cur, prev_t, cur_t, q_spec, pl.BlockSpec(memory_space=pltpu.SMEM)],
        out_specs=(q_spec, q_spec),
        compiler_params=_params("parallel", "parallel"),
    )(qs, kd, kd, vt, vt, zb, sinks)


def _attn_backward(qs, kd, vd, kt, zb, att, dyb, sinks, tables):
    seq, dm = qs.shape
    n_kv = kd.shape[0]
    gw = GQA_GROUP * HEAD_DIM
    n_blocks = seq // ATTN_BLOCK

    def body(q_ref, kp_ref, kc_ref, vp_ref, vc_ref, ktp_ref, ktc_ref, z_ref, att_ref, dyb_ref, sink_ref,
             cos_ref, sa_ref, sb_ref, dh_ref, dk_ref, dv_ref, ds_ref):
        kh, n = pl.program_id(0), pl.program_id(1)

        @pl.when(n == 0)
        def _():
            dk_ref[...] = jnp.zeros_like(dk_ref)
            dv_ref[...] = jnp.zeros_like(dv_ref)

        @pl.when(jnp.logical_and(n == 0, kh == 0))
        def _():
            ds_ref[...] = jnp.zeros_like(ds_ref)

        low = lax.broadcasted_iota(jnp.int32, (1, LANES), 1) < HEAD_DIM
        low_rows = lax.broadcasted_iota(jnp.int32, (LANES, 1), 0) < HEAD_DIM
        head_lane = lax.broadcasted_iota(jnp.int32, (1, LANES), 1)
        kcat = jnp.concatenate([kp_ref[...], kc_ref[...]], axis=0)
        vcat = jnp.concatenate([vp_ref[...], vc_ref[...]], axis=0)
        kt = jnp.concatenate([ktp_ref[...], ktc_ref[...]], axis=1)
        cos, sa, sb = cos_ref[...], sa_ref[...], sb_ref[...]
        q_parts = _head_queries(q_ref, low)
        q_all = jnp.concatenate(q_parts, axis=0)
        d_parts = []
        for j in range(GQA_GROUP // 2):
            cs = slice(j * LANES, (j + 1) * LANES)
            sz, dsz = _silu_parts(z_ref[:, cs])
            dy2 = dyb_ref[:, cs]
            dh_ref[1, :, cs] = (dy2 * att_ref[:, cs] * dsz).astype(BF16)
            datt = (dy2 * sz).astype(BF16)
            d_parts += [jnp.where(low, datt, 0), jnp.where(low, 0, datt)]
        d_all = jnp.concatenate(d_parts, axis=0)
        probs_t, sink_p = _probs_transposed(n, kh, kcat, q_all, sink_ref)
        dprobs_t = lax.dot_general(vcat, d_all, NT, preferred_element_type=F32)
        row_dot = jnp.sum(probs_t * dprobs_t, axis=0, keepdims=True)
        ds_t = (probs_t * (dprobs_t - row_dot)).astype(BF16)
        dk = jnp.dot(ds_t, q_all, preferred_element_type=F32)
        dv = jnp.dot(probs_t.astype(BF16), d_all, preferred_element_type=F32)
        for j in range(GQA_GROUP // 2):
            dq2 = _pair_product_transposed(kt, ds_t, j, low_rows)
            dh_ref[0, :, j * LANES:(j + 1) * LANES] = (_rope_transposed(dq2, cos, sa, sb) * 0.125).astype(BF16)
        sink_dot = sink_p * row_dot
        dsink = jnp.zeros((1, LANES), F32)
        for h in range(GQA_GROUP):
            part = jnp.sum(sink_dot[:, h * ATTN_BLOCK:(h + 1) * ATTN_BLOCK], axis=1, keepdims=True)
            dsink = dsink - jnp.where(head_lane == kh * GQA_GROUP + h, part, 0.0)
        ds_ref[0:1, :] += dsink

        @pl.when(n == 0)
        def _():
            dk_ref[pl.ds(0, ATTN_BLOCK), :] += dk[ATTN_BLOCK:, :]
            dv_ref[pl.ds(0, ATTN_BLOCK), :] += dv[ATTN_BLOCK:, :]

        @pl.when(n > 0)
        def _():
            start = pl.multiple_of((n - 1) * ATTN_BLOCK, ATTN_BLOCK)
            dk_ref[pl.ds(start, 2 * ATTN_BLOCK), :] += dk
            dv_ref[pl.ds(start, 2 * ATTN_BLOCK), :] += dv

    q_spec, cur, prev, cur_t, prev_t = _attn_specs(gw)
    tab_spec = pl.BlockSpec((ATTN_BLOCK, LANES), lambda kh, n: (n, 0))
    acc_spec = pl.BlockSpec((None, seq, LANES), lambda kh, n: (kh, 0, 0))
    acc_shape = jax.ShapeDtypeStruct((n_kv, seq, LANES), F32)
    return pl.pallas_call(
        body, name="attn_bwd", grid=(n_kv, n_blocks),
        out_shape=(jax.ShapeDtypeStruct((2, seq, dm), BF16), acc_shape, acc_shape,
                   jax.ShapeDtypeStruct((STAT_ROWS, LANES), F32)),
        in_specs=[q_spec, prev, cur, prev, cur, prev_t, cur_t, q_spec, q_spec, q_spec,
                  pl.BlockSpec(memory_space=pltpu.SMEM), tab_spec, tab_spec, tab_spec],
        out_specs=(pl.BlockSpec((2, ATTN_BLOCK, gw), lambda kh, n: (0, n, kh)), acc_spec, acc_spec,
                   pl.BlockSpec((STAT_ROWS, LANES), lambda kh, n: (0, 0))),
        compiler_params=_params("arbitrary", "arbitrary"),
    )(qs, kd, kd, vd, vd, kt, kt, zb, att, dyb, sinks, *tables)


def _kv_grad_fold(dk, dv, tables):
    n_kv, seq, _ = dk.shape
    kvw = n_kv * HEAD_DIM
    tile = _tile(seq, 512)

    def body(dk_ref, dv_ref, cos_ref, sa_ref, sb_ref, o_ref):
        low = lax.broadcasted_iota(jnp.int32, (1, LANES), 1) < HEAD_DIM
        cos, sa, sb = cos_ref[...], sa_ref[...], sb_ref[...]

        def folded(ref, h):
            t = ref[h]
            return t + pltpu.roll(t, HEAD_DIM, 1)

        for j in range(n_kv // 2):
            ka = _rope_transposed(folded(dk_ref, 2 * j), cos, sa, sb)
            kb = _rope_transposed(folded(dk_ref, 2 * j + 1), cos, sa, sb)
            o_ref[:, j * LANES:(j + 1) * LANES] = jnp.where(low, ka, kb).astype(BF16)
            o_ref[:, kvw + j * LANES:kvw + (j + 1) * LANES] = jnp.where(
                low, folded(dv_ref, 2 * j), folded(dv_ref, 2 * j + 1)).astype(BF16)

    tab_spec = pl.BlockSpec((tile, LANES), lambda i: (i, 0))
    in_spec = pl.BlockSpec((n_kv, tile, LANES), lambda i: (0, i, 0))
    return pl.pallas_call(
        body, name="kv_grad_fold", grid=(seq // tile,),
        out_shape=jax.ShapeDtypeStruct((seq, 2 * kvw), BF16),
        in_specs=[in_spec, in_spec, tab_spec, tab_spec, tab_spec],
        out_specs=pl.BlockSpec((tile, 2 * kvw), lambda i: (i, 0)),
        compiler_params=_params("parallel"),
    )(dk, dv, *tables)


def _out_proj_norm_loss(yb, w, xhat1, gain0, bias0, gain1, bias1, target):
    seq, dm = xhat1.shape
    tile = _tile(seq, 256)

    def body(y_ref, w_ref, xh1_ref, g0_ref, b0_ref, g1_ref, b1_ref, t_ref, dr_ref, drb_ref, st_ref):
        i = pl.program_id(0)

        @pl.when(i == 0)
        def _():
            st_ref[...] = jnp.zeros_like(st_ref)

        ob = jnp.dot(y_ref[...], w_ref[...], preferred_element_type=F32)
        x1 = xh1_ref[...] * g0_ref[...] + b0_ref[...]
        xhat, rstd = _layer_norm_stats(ALPHA * x1 + ob)
        err = xhat * g1_ref[...] + b1_ref[...] - t_ref[...]
        dout = err * (1.0 / dm)
        dr = _layer_norm_backward(dout, xhat, rstd, g1_ref[...])
        dr_ref[...] = dr
        drb_ref[...] = dr.astype(BF16)
        st_ref[0:1, :] += _col_sum(dout * xhat)
        st_ref[1:2, :] += _col_sum(dout)
        st_ref[2:3, :] += _col_sum(err * err)

    row_spec = pl.BlockSpec((tile, dm), lambda i: (i, 0))
    vec_spec = pl.BlockSpec((1, dm), lambda i: (0, 0))
    return pl.pallas_call(
        body, name="out_proj_norm_loss_b", grid=(seq // tile,),
        out_shape=(jax.ShapeDtypeStruct((seq, dm), F32), jax.ShapeDtypeStruct((seq, dm), BF16),
                   jax.ShapeDtypeStruct((STAT_ROWS, dm), F32)),
        in_specs=[row_spec, pl.BlockSpec(w.shape, lambda i: (0, 0)), row_spec, vec_spec, vec_spec, vec_spec,
                  vec_spec, row_spec],
        out_specs=(row_spec, row_spec, pl.BlockSpec((STAT_ROWS, dm), lambda i: (0, 0))),
        compiler_params=_params("arbitrary"),
    )(yb, w, xhat1, gain0, bias0, gain1, bias1, target)


def _stream_grad_norm_backward(dhq, wqg, dkv, wkv, dr2, xhat1, rstd1, gain0):
    seq, dm = dr2.shape
    tile = _tile(seq, 256)

    def body(dh_ref, wqg_ref, dkv_ref, wkv_ref, dr2_ref, xh_ref, rstd_ref, g_ref, dr_ref, drb_ref, st_ref, acc):
        i, k = pl.program_id(0), pl.program_id(1)
        part = lax.dot_general(dh_ref[...], wqg_ref[...], NT, preferred_element_type=F32)

        @pl.when(k == 0)
        def _():
            acc[...] = part + lax.dot_general(dkv_ref[...], wkv_ref[...], NT, preferred_element_type=F32)

        @pl.when(jnp.logical_and(k == 1, i == 0))
        def _():
            st_ref[...] = jnp.zeros_like(st_ref)

        @pl.when(k == 1)
        def _():
            dx1 = acc[...] + part + ALPHA * dr2_ref[...]
            xhat = xh_ref[...]
            dr = _layer_norm_backward(dx1, xhat, rstd_ref[...], g_ref[...])
            dr_ref[...] = dr
            drb_ref[...] = dr.astype(BF16)
            st_ref[0:1, :] += _col_sum(dx1 * xhat)
            st_ref[1:2, :] += _col_sum(dx1)

    row_spec = pl.BlockSpec((tile, dm), lambda i, k: (i, 0))
    return pl.pallas_call(
        body, name="stream_grad_norm_bwd", grid=(seq // tile, 2),
        out_shape=(jax.ShapeDtypeStruct((seq, dm), F32), jax.ShapeDtypeStruct((seq, dm), BF16),
                   jax.ShapeDtypeStruct((STAT_ROWS, dm), F32)),
        in_specs=[pl.BlockSpec((None, tile, dm), lambda i, k: (k, i, 0)),
                  pl.BlockSpec((dm, dm), lambda i, k: (0, k)),
                  pl.BlockSpec((tile, dkv.shape[1]), lambda i, k: (i, 0)),
                  pl.BlockSpec(wkv.shape, lambda i, k: (0, 0)),
                  row_spec, row_spec, pl.BlockSpec((tile, 1), lambda i, k: (i, 0)),
                  pl.BlockSpec((1, dm), lambda i, k: (0, 0))],
        out_specs=(row_spec, row_spec, pl.BlockSpec((STAT_ROWS, dm), lambda i, k: (0, 0))),
        scratch_shapes=[pltpu.VMEM((tile, dm), F32)],
        compiler_params=_params("arbitrary", "arbitrary"),
    )(dhq, wqg, dkv, wkv, dr2, xhat1, rstd1, gain0)


def _adamw_math(w, g, m, v):
    m = ADAM_B1 * m + (1.0 - ADAM_B1) * g
    v = ADAM_B2 * v + (1.0 - ADAM_B2) * (g * g)
    m_hat = m / (1.0 - ADAM_B1 ** ADAM_STEP)
    v_hat = v / (1.0 - ADAM_B2 ** ADAM_STEP)
    delta = -ADAM_LR * (m_hat / (jnp.sqrt(v_hat) + ADAM_EPS) + ADAM_WD * w)
    return delta, m, v


def _sum_devices(ref):
    total = ref[0].astype(F32)
    for d in range(1, N_DEV):
        total = total + ref[d].astype(F32)
    return total


def _adamw_shard(name, parts, w, m, v):
    rows, cols = w.shape
    tr = _tile(rows, max(8, (1 << 18) // cols)) if rows >= 8 else rows

    def body(p_ref, w_ref, m_ref, v_ref, g_out, d_out, m_out, v_out):
        g = _sum_devices(p_ref)
        delta, m_new, v_new = _adamw_math(w_ref[...], g, m_ref[...], v_ref[...])
        g_out[...] = g
        d_out[...] = delta
        m_out[...] = m_new
        v_out[...] = v_new

    spec = pl.BlockSpec((tr, cols), lambda i: (i, 0))
    shape = jax.ShapeDtypeStruct((rows, cols), F32)
    return pl.pallas_call(
        body, name=name, grid=(rows // tr,),
        out_shape=(shape, shape, shape, shape),
        in_specs=[pl.BlockSpec((N_DEV, tr, cols), lambda i: (0, i, 0)), spec, spec, spec],
        out_specs=(spec, spec, spec, spec),
        compiler_params=_params("parallel"),
    )(parts, w, m, v)


def _adamw_replicated(stats_b, stats_a, sink_parts, ln_g, ln_b, sinks, m_ln_g, m_ln_b, m_sinks, v_ln_g, v_ln_b,
                      v_sinks):
    n_q = sinks.shape[1]

    def body(sb_ref, sa_ref, sk_ref, g_ref, b_ref, s_ref, mg_ref, mb_ref, ms_ref, vg_ref, vb_ref, vs_ref, *outs):
        layer_sums = (_sum_devices(sa_ref), _sum_devices(sb_ref))
        for which, (w_ref, m_ref, v_ref) in enumerate(((g_ref, mg_ref, vg_ref), (b_ref, mb_ref, vb_ref))):
            for layer in range(DEPTH):
                row = slice(layer, layer + 1)
                g = layer_sums[layer][which:which + 1, :]
                res = (g,) + _adamw_math(w_ref[row, :], g, m_ref[row, :], v_ref[row, :])
                for o_ref, val in zip(outs[4 * which:4 * which + 4], res):
                    o_ref[row, :] = val
        g = _sum_devices(sk_ref)[0:1, 0:n_q]
        res = (g,) + _adamw_math(s_ref[...], g, ms_ref[...], vs_ref[...])
        for o_ref, val in zip(outs[8:12], res):
            o_ref[...] = val

    vmem = pl.BlockSpec(memory_space=pltpu.VMEM)
    shapes = [jax.ShapeDtypeStruct(a.shape, F32) for a in (ln_g, ln_b, sinks) for _ in range(4)]
    return pl.pallas_call(
        body, name="adamw_replicated", out_shape=tuple(shapes),
        in_specs=[vmem] * 12, out_specs=tuple([vmem] * 12),
    )(stats_b, stats_a, sink_parts, ln_g, ln_b, sinks, m_ln_g, m_ln_b, m_sinks, v_ln_g, v_ln_b, v_sinks)


def kernel(x, ln_g, ln_b, a_w_in, a_w_group, a_scale, a_w_out, b_w_k, b_w_v, b_w_qg, b_sinks, b_w_out, loss_target, m_ln_g, m_ln_b, m_a_w_in, m_a_w_group, m_a_scale, m_a_w_out, m_b_w_k, m_b_w_v, m_b_w_qg, m_b_sinks, m_b_w_out, v_ln_g, v_ln_b, v_a_w_in, v_a_w_group, v_a_scale, v_a_w_out, v_b_w_k, v_b_w_v, v_b_w_qg, v_b_sinks, v_b_w_out):
    _, seq, dm = x.shape
    n_groups = len(POOL_WINDOWS)
    gd = dm // n_groups
    kvw = b_w_k.shape[1]
    cb = 2 * dm // N_DEV
    rb = dm // N_DEV
    gb = gd // N_DEV

    x2 = x.reshape(seq, dm)
    target = loss_target.reshape(seq, dm)
    w_in_s = a_w_in.reshape(dm, cb)
    w_g_s = a_w_group.reshape(n_groups, gb, gd)
    w_out_s = a_w_out.reshape(rb, dm)
    w_qg_s = b_w_qg.reshape(dm, cb)
    w_outb_s = b_w_out.reshape(rb, dm)

    def cols(ref, dev):
        return ref.at[:, pl.ds(pl.multiple_of(dev * cb, LANES), cb)]

    def rows(ref, dev):
        return ref.at[pl.ds(pl.multiple_of(dev * rb, 8), rb), :]

    def group_rows(ref, dev):
        return ref.at[:, pl.ds(pl.multiple_of(dev * gb, 8), gb), :]

    def k_rows(ref, dev):
        return ref.at[pl.ds(pl.multiple_of(dev * rb, 8), rb), pl.ds(0, kvw)]

    def v_rows(ref, dev):
        return ref.at[pl.ds(pl.multiple_of(dev * rb, 8), rb), pl.ds(kvw, kvw)]

    def scale_cols(ref, dev):
        return ref.at[:, pl.ds(pl.multiple_of(dev * rb, LANES), rb)]

    bf = lambda a: a.astype(BF16)
    w_in, w_g, w_out, w_kv, w_qg, w_outb, scale = _gather_weights(
        [(bf(w_in_s), 0, cols), (bf(w_g_s), 1, group_rows), (bf(w_out_s), 2, rows), (bf(b_w_k), 3, k_rows),
         (bf(b_w_v), 3, v_rows), (bf(w_qg_s), 4, cols), (bf(w_outb_s), 5, rows), (a_scale, 6, scale_cols)],
        [jax.ShapeDtypeStruct((dm, 2 * dm), BF16), jax.ShapeDtypeStruct((n_groups, gd, gd), BF16),
         jax.ShapeDtypeStruct((dm, dm), BF16), jax.ShapeDtypeStruct((dm, 2 * kvw), BF16),
         jax.ShapeDtypeStruct((dm, 2 * dm), BF16), jax.ShapeDtypeStruct((dm, dm), BF16),
         jax.ShapeDtypeStruct((1, dm), F32)])

    tables = _rope_tables(seq)
    bm = _tile(seq, 1024)
    bn = _tile(dm, 1024)
    g0, g1, b0, b1 = ln_g[0:1], ln_g[1:2], ln_b[0:1], ln_b[1:2]

    xb = _cast_bf16("cast_x", x2)
    h = _mm("a_in_proj", xb, w_in, dims=NN, grid=(seq // bm, 2 * dm // bn),
            a_spec=pl.BlockSpec((bm, dm), lambda i, j: (i, 0)), b_spec=pl.BlockSpec((dm, bn), lambda i, j: (0, j)),
            out_shape=jax.ShapeDtypeStruct((seq, 2 * dm), F32), out_spec=pl.BlockSpec((bm, bn), lambda i, j: (i, j)))
    y, pooled, mixed = _pool_mid_forward(h, w_g, scale)
    xhat1, rstd1, x1b = _out_proj_norm(y, w_out, x2, g0, b0)

    kd, vd, kt, vt = _kv_proj(x1b, w_kv, tables)
    bmq = _tile(seq, 512)
    tab_spec = pl.BlockSpec((bmq, LANES), lambda i, j: (i, 0))

    def rope_scale(val, cos_ref, sa_ref, sb_ref):
        cos, sa, sb = cos_ref[...], sa_ref[...], sb_ref[...]
        return jnp.concatenate([_rope(val[:, j * LANES:(j + 1) * LANES], cos, sa, sb) * 0.125
                                for j in range(val.shape[1] // LANES)], axis=1)

    qs = _mm("b_q_proj", x1b, w_qg, dims=NN, grid=(seq // bmq, dm // bn),
             a_spec=pl.BlockSpec((bmq, dm), lambda i, j: (i, 0)), b_spec=pl.BlockSpec((dm, bn), lambda i, j: (0, j)),
             out_shape=jax.ShapeDtypeStruct((seq, dm), BF16), out_spec=pl.BlockSpec((bmq, bn), lambda i, j: (i, j)),
             epilogue=rope_scale, extras=tables, extra_specs=(tab_spec,) * 3)
    zb = _mm("b_gate_proj", x1b, w_qg, dims=NN, grid=(seq // bm, dm // bn),
             a_spec=pl.BlockSpec((bm, dm), lambda i, j: (i, 0)),
             b_spec=pl.BlockSpec((dm, bn), lambda i, j: (0, j + dm // bn)),
             out_shape=jax.ShapeDtypeStruct((seq, dm), F32), out_spec=pl.BlockSpec((bm, bn), lambda i, j: (i, j)))
    att, yb = _attn_forward(qs, kd, vt, zb, b_sinks)
    dr2, dr2b, stats_b = _out_proj_norm_loss(yb, w_outb, xhat1, g0, b0, g1, b1, target)

    def weight_grad(name, a, b, n_cols, b_spec=None):
        m_cols = a.shape[1]
        tm, tn = _tile(m_cols, 1024), _tile(n_cols, 512)
        return _mm(name, a, b, dims=TN, grid=(m_cols // tm, n_cols // tn),
                   a_spec=pl.BlockSpec((seq, tm), lambda i, j: (0, i)),
                   b_spec=b_spec(tn) if b_spec else pl.BlockSpec((seq, tn), lambda i, j: (0, j)),
                   out_shape=jax.ShapeDtypeStruct((m_cols, n_cols), BF16),
                   out_spec=pl.BlockSpec((tm, tn), lambda i, j: (i, j)))

    def halves_spec(tn):
        per = dm // tn
        return pl.BlockSpec((None, seq, tn), lambda i, j: (j // per, 0, j % per))

    def times_transposed(name, a, w):
        return _mm(name, a, w, dims=NT, grid=(seq // bm, dm // bn),
                   a_spec=pl.BlockSpec((bm, a.shape[1]), lambda i, j: (i, 0)),
                   b_spec=pl.BlockSpec((bn, w.shape[1]), lambda i, j: (j, 0)),
                   out_shape=jax.ShapeDtypeStruct((seq, dm), F32), out_spec=pl.BlockSpec((bm, bn), lambda i, j: (i, j)))

    dyb = times_transposed("b_out_proj_dx", dr2b, w_outb)
    d_w_outb = weight_grad("b_out_proj_dw", yb, dr2b, dm)
    dhq, dkd, dvd, dsink = _attn_backward(qs, kd, vd, kt, zb, att, dyb, b_sinks, tables)
    dkv = _kv_grad_fold(dkd, dvd, tables)
    d_w_qg = weight_grad("b_qg_proj_dw", x1b, dhq, 2 * dm, halves_spec)
    d_w_kv = weight_grad("b_kv_proj_dw", x1b, dkv, 2 * kvw)
    dr1, dr1b, stats_a = _stream_grad_norm_backward(dhq, w_qg, dkv, w_kv, dr2, xhat1, rstd1, g0)

    dy = times_transposed("a_out_proj_dx", dr1b, w_out)
    d_w_out = weight_grad("a_out_proj_dw", y, dr1b, dm)
    dh, d_w_g, stats_s = _pool_mid_backward(dy, mixed, h, pooled, w_g, scale)
    d_w_in = weight_grad("a_in_proj_dw", xb, dh, 2 * dm, halves_spec)
    grad_x = _mm("a_in_proj_dx", dh, w_in, dims=NT, grid=(seq // bm, dm // bn, 2), nk=2,
                 a_spec=pl.BlockSpec((None, bm, dm), lambda i, j, k: (k, i, 0)),
                 b_spec=pl.BlockSpec((bn, dm), lambda i, j, k: (j, k)),
                 out_shape=jax.ShapeDtypeStruct((seq, dm), F32), out_spec=pl.BlockSpec((bm, bn), lambda i, j, k: (i, j)),
                 add=dr1, add_spec=pl.BlockSpec((bm, bn), lambda i, j, k: (i, j)), add_scale=ALPHA)

    def stat_row_cols(ref, dev):
        return ref.at[pl.ds(0, 1), pl.ds(pl.multiple_of(dev * rb, LANES), rb)]

    p_in, p_g, p_out, p_k, p_v, p_qg, p_outb, p_scale = _exchange_blocks("scatter_weight_grads", [
        (d_w_in, cols, (dm, cb)), (d_w_g, group_rows, (n_groups, gb, gd)), (d_w_out, rows, (rb, dm)),
        (d_w_kv, k_rows, (rb, kvw)), (d_w_kv, v_rows, (rb, kvw)), (d_w_qg, cols, (dm, cb)),
        (d_w_outb, rows, (rb, dm)), (stats_s, stat_row_cols, (1, rb))])
    all_b, all_a, all_sink = _exchange_blocks("gather_replicated_grads", [
        (stats_b, None, stats_b.shape), (stats_a, None, stats_a.shape), (dsink, None, dsink.shape)])

    def shard_update(name, parts, w, m, v):
        shape = w.shape
        flat = lambda a: a.reshape(-1, shape[-1])
        outs = _adamw_shard(name, parts.reshape(N_DEV, -1, shape[-1]), flat(w), flat(m), flat(v))
        return [o.reshape(shape) for o in outs]

    upd = {
        "a_w_in": shard_update("adamw_a_w_in", p_in, a_w_in, m_a_w_in, v_a_w_in),
        "a_w_group": shard_update("adamw_a_w_group", p_g, a_w_group, m_a_w_group, v_a_w_group),
        "a_scale": shard_update("adamw_a_scale", p_scale, a_scale, m_a_scale, v_a_scale),
        "a_w_out": shard_update("adamw_a_w_out", p_out, a_w_out, m_a_w_out, v_a_w_out),
        "b_w_k": shard_update("adamw_b_w_k", p_k, b_w_k, m_b_w_k, v_b_w_k),
        "b_w_v": shard_update("adamw_b_w_v", p_v, b_w_v, m_b_w_v, v_b_w_v),
        "b_w_qg": shard_update("adamw_b_w_qg", p_qg, b_w_qg, m_b_w_qg, v_b_w_qg),
        "b_w_out": shard_update("adamw_b_w_out", p_outb, b_w_out, m_b_w_out, v_b_w_out),
    }
    rep = _adamw_replicated(all_b, all_a, all_sink, ln_g, ln_b, b_sinks, m_ln_g, m_ln_b, m_b_sinks, v_ln_g, v_ln_b,
                            v_b_sinks)
    upd["ln_g"], upd["ln_b"], upd["b_sinks"] = list(rep[0:4]), list(rep[4:8]), list(rep[8:12])

    loss = lax.psum(0.5 * jnp.sum(stats_b[2]) / dm, AXES)
    order = ["ln_g", "ln_b", "a_w_in", "a_w_group", "a_scale", "a_w_out", "b_w_k", "b_w_v", "b_w_qg", "b_sinks",
             "b_w_out"]
    return (loss, grad_x.reshape(x.shape), *[upd[n][0] for n in order], *[upd[n][1] for n in order],
            *[upd[n][2] for n in order], *[upd[n][3] for n in order])
```

```python
import functools

import jax
import jax.numpy as jnp
from jax import lax
from jax.experimental import pallas as pl
from jax.experimental.pallas import tpu as pltpu
from jax.experimental.pallas import tpu_sc as plsc

F32 = jnp.float32
BF16 = jnp.bfloat16
MESH = pl.DeviceIdType.MESH
AXES = ("x", "y", "c")
N_DEV = 8

POOL_WINDOWS = (2, 4, 8, 16)
POOL_HALO = 16
HEAD_DIM = 64
GQA_GROUP = 8
ATTN_BLOCK = 128
ROPE_THETA = 10000.0
LN_EPS = 1e-5
NEG_INF = -1e30
DEPTH = 2
ALPHA = (2 * DEPTH) ** 0.25
ADAM_LR = 0.001
ADAM_B1 = 0.9
ADAM_B2 = 0.999
ADAM_EPS = 1e-08
ADAM_WD = 0.01
ADAM_STEP = 10

LANES = 128
STAT_ROWS = 8


def _tile(n, want):
    t = min(n, want)
    while n % t:
        t //= 2
    return t


def _params(*sem):
    return pltpu.CompilerParams(dimension_semantics=sem)


ANY = pl.BlockSpec(memory_space=pl.ANY)


def _my_pos():
    return lax.axis_index("x"), lax.axis_index("y"), lax.axis_index("c")


def _dev_index(p):
    return 4 * p[0] + 2 * p[1] + p[2]


def _handshake(peers):
    barrier = pltpu.get_barrier_semaphore()
    for peer in peers:
        pl.semaphore_signal(barrier, inc=1, device_id=peer, device_id_type=MESH)
    pl.semaphore_wait(barrier, len(peers))


def _launch_on_tensorcore(name, body, operands, out_shapes, scratch):
    return pl.pallas_call(
        body, name=name, out_shape=tuple(out_shapes),
        in_specs=[ANY] * len(operands), out_specs=tuple([ANY] * len(out_shapes)), scratch_shapes=scratch,
    )(*operands)


def _launch_on_sequencer(name, collective_id, body, operands, out_shapes, scratch):
    return pl.kernel(
        body, out_type=tuple(out_shapes), name=name,
        mesh=plsc.ScalarSubcoreMesh(axis_name="sequencer", num_cores=1), scratch_types=scratch,
        compiler_params=pltpu.CompilerParams(collective_id=collective_id),
    )(*operands)


def _gather_weights(name, collective_id, streams, out_shapes):
    n_s = len(streams)
    n_out = len(out_shapes)

    def body(*refs):
        srcs = refs[:n_s]
        outs = refs[n_s:n_s + n_out]
        send_sems, recv_sems, local_sems = refs[n_s + n_out:]
        x, y, c = _my_pos()
        me, sibling = (x, y, c), (x, y, 1 - c)
        chips = [(1 - x, y), (x, 1 - y), (1 - x, 1 - y)]
        _handshake([sibling] + [(*chip, c) for chip in chips])

        def copy(s, k, block, to, from_shard=False):
            out_ref = outs[streams[s][1]]
            win = streams[s][2](out_ref, _dev_index(block))
            return pltpu.make_async_remote_copy(
                src_ref=srcs[s] if from_shard else win, dst_ref=win,
                send_sem=send_sems.at[7 * s + k], recv_sem=recv_sems.at[7 * s + k],
                device_id=to, device_id_type=MESH)

        mine = [pltpu.make_async_copy(srcs[s], streams[s][2](outs[streams[s][1]], _dev_index(me)), local_sems.at[s])
                for s in range(n_s)]
        for cp in mine:
            cp.start()
        first = []
        for s in range(n_s):
            first.append(copy(s, 0, me, sibling, True))
            first += [copy(s, 1 + j, me, (*chip, c), True) for j, chip in enumerate(chips)]
        for cp in first:
            cp.start()
        passed = []
        for j, chip in enumerate(chips):
            for s in range(n_s):
                copy(s, 1 + j, (*chip, c), me).wait_recv()
                fwd = copy(s, 4 + j, (*chip, c), sibling)
                fwd.start()
                passed.append(fwd)
        for s in range(n_s):
            copy(s, 0, sibling, me).wait_recv()
            for j, chip in enumerate(chips):
                copy(s, 4 + j, (*chip, 1 - c), me).wait_recv()
        for cp in first + passed:
            cp.wait_send()
        for cp in mine:
            cp.wait()

    scratch = [pltpu.SemaphoreType.DMA((7 * n_s,)), pltpu.SemaphoreType.DMA((7 * n_s,)),
               pltpu.SemaphoreType.DMA((n_s,))]
    return _launch_on_sequencer(name, collective_id, body, [s[0] for s in streams], out_shapes, scratch)


def _exchange_blocks(name, streams, collective_id=None):
    n_s = len(streams)

    def body(*refs):
        srcs = refs[:n_s]
        outs = refs[n_s:2 * n_s]
        send_sems, recv_sems, local_sems = refs[2 * n_s:]
        x, y, c = _my_pos()
        me = _dev_index((x, y, c))
        if collective_id is not None:
            _handshake([(1 - x if k & 4 else x, 1 - y if k & 2 else y, 1 - c if k & 1 else c) for k in range(1, N_DEV)])

        def window(s, dev):
            return srcs[s] if streams[s][1] is None else streams[s][1](srcs[s], dev)

        mine = [pltpu.make_async_copy(window(s, me), outs[s].at[me], local_sems.at[s]) for s in range(n_s)]
        for cp in mine:
            cp.start()
        copies = []
        for k in (2, 4, 6, 3, 5, 7, 1):
            peer = (1 - x if k & 4 else x, 1 - y if k & 2 else y, 1 - c if k & 1 else c)
            for s in range(n_s):
                copies.append(pltpu.make_async_remote_copy(
                    src_ref=window(s, _dev_index(peer)), dst_ref=outs[s].at[me],
                    send_sem=send_sems.at[7 * s + k - 1], recv_sem=recv_sems.at[7 * s + k - 1],
                    device_id=peer, device_id_type=MESH))
        for cp in copies:
            cp.start()
        for cp in copies:
            cp.wait()
        for cp in mine:
            cp.wait()

    out_shapes = [jax.ShapeDtypeStruct((N_DEV,) + tuple(s[2]), s[0].dtype) for s in streams]
    scratch = [pltpu.SemaphoreType.DMA((7 * n_s,)), pltpu.SemaphoreType.DMA((7 * n_s,)),
               pltpu.SemaphoreType.DMA((n_s,))]
    operands = [s[0] for s in streams]
    if collective_id is None:
        return _launch_on_tensorcore(name, body, operands, out_shapes, scratch)
    return _launch_on_sequencer(name, collective_id, body, operands, out_shapes, scratch)


NN = (((1,), (0,)), ((), ()))
NT = (((1,), (1,)), ((), ()))
TN = (((0,), (0,)), ((), ()))


def _mm(name, a, b, *, dims, grid, a_spec, b_spec, out_shape, out_spec, nk=1,
        add=None, add_spec=None, add_scale=1.0, epilogue=None, extras=(), extra_specs=()):
    n_extra = len(extras)
    has_add = add is not None

    def body(*refs):
        a_ref, b_ref = refs[:2]
        pos = 2
        add_ref = None
        if has_add:
            add_ref = refs[pos]
            pos += 1
        extra_refs = refs[pos:pos + n_extra]
        o_ref = refs[pos + n_extra]
        acc_ref = refs[pos + n_extra + 1] if nk > 1 else None

        def finish(val):
            if has_add:
                val = val + add_scale * add_ref[...]
            if epilogue is not None:
                val = epilogue(val, *extra_refs)
            o_ref[...] = val.astype(o_ref.dtype)

        part = lax.dot_general(a_ref[...].astype(BF16), b_ref[...].astype(BF16), dims,
                               preferred_element_type=F32)
        if nk == 1:
            finish(part)
        else:
            k = pl.program_id(2)

            @pl.when(k == 0)
            def _():
                acc_ref[...] = part

            @pl.when(jnp.logical_and(k > 0, k < nk - 1))
            def _():
                acc_ref[...] += part

            @pl.when(k == nk - 1)
            def _():
                finish(acc_ref[...] + part)

    in_specs = [a_spec, b_spec] + ([add_spec] if has_add else []) + list(extra_specs)
    operands = [a, b] + ([add] if has_add else []) + list(extras)
    scratch = [pltpu.VMEM(out_spec.block_shape, F32)] if nk > 1 else []
    sem = ("parallel", "parallel") + (("arbitrary",) if nk > 1 else ())
    return pl.pallas_call(
        body, name=name, grid=grid, out_shape=out_shape,
        in_specs=in_specs, out_specs=out_spec, scratch_shapes=scratch,
        compiler_params=_params(*sem),
    )(*operands)


def _cast_bf16(name, a):
    rows, cols = a.shape
    tr = _tile(rows, 512)

    def body(a_ref, o_ref):
        o_ref[...] = a_ref[...].astype(BF16)

    return pl.pallas_call(
        body, name=name, grid=(rows // tr,),
        out_shape=jax.ShapeDtypeStruct(a.shape, BF16),
        in_specs=[pl.BlockSpec((tr, cols), lambda i: (i, 0))],
        out_specs=pl.BlockSpec((tr, cols), lambda i: (i, 0)),
        compiler_params=_params("parallel"),
    )(a)


def _rope_tables(seq):
    inv_freq = ROPE_THETA ** (-jnp.arange(0, HEAD_DIM, 2, dtype=F32) / HEAD_DIM)
    ang = jnp.arange(seq, dtype=F32)[:, None] * inv_freq[None, :]
    ang = jnp.concatenate([ang, ang, ang, ang], axis=-1)
    cos, sin = jnp.cos(ang), jnp.sin(ang)
    first_half = (jnp.arange(LANES) % HEAD_DIM < HEAD_DIM // 2)[None, :]
    return cos, jnp.where(first_half, -sin, 0.0), jnp.where(first_half, 0.0, sin)


def _rot(t, sin_a, sin_b):
    return pltpu.roll(t, LANES - HEAD_DIM // 2, 1) * sin_a + pltpu.roll(t, HEAD_DIM // 2, 1) * sin_b


def _rope(t, cos, sin_a, sin_b):
    return t * cos + _rot(t, sin_a, sin_b)


def _rope_transposed(dy, cos, sin_a, sin_b):
    return dy * cos - _rot(dy, sin_a, sin_b)


def _silu_parts(z):
    sig = jax.nn.sigmoid(z)
    return z * sig, sig * (1.0 + z * (1.0 - sig))


def _layer_norm_stats(r):
    mu = jnp.mean(r, axis=-1, keepdims=True)
    d = r - mu
    var = jnp.mean(d * d, axis=-1, keepdims=True)
    rstd = lax.rsqrt(var + LN_EPS)
    return d * rstd, rstd


def _layer_norm_backward(dout, xhat, rstd, gain):
    dxh = dout * gain
    m1 = jnp.mean(dxh, axis=-1, keepdims=True)
    m2 = jnp.mean(dxh * xhat, axis=-1, keepdims=True)
    return rstd * (dxh - m1 - xhat * m2)


def _col_sum(v):
    return jnp.sum(v, axis=0, keepdims=True)


def _pool_mid_forward(h, wg, scale):
    seq, d2 = h.shape
    dm = d2 // 2
    gd = dm // len(POOL_WINDOWS)
    tile = _tile(seq, 256)
    halo_blocks = tile // POOL_HALO

    def body(u_ref, up_ref, z_ref, wg_ref, sc_ref, y_ref, p_ref, mx_ref):
        i = pl.program_id(0)
        row = i * tile + lax.broadcasted_iota(jnp.int32, (tile, 1), 0)
        count = (row + 1).astype(F32)
        for g, w in enumerate(POOL_WINDOWS):
            cs = slice(g * gd, (g + 1) * gd)
            u = u_ref[:, cs]
            prev = jnp.where(i > 0, up_ref[:, cs], 0.0)
            s = jnp.concatenate([prev, u], axis=0)
            sh = 1
            while sh < w:
                s = s + pltpu.roll(s, sh, 0)
                sh *= 2
            p = s[POOL_HALO:, :] * (1.0 / jnp.minimum(count, float(w))) - u
            pb = p.astype(BF16)
            mx = jnp.dot(pb, wg_ref[g], preferred_element_type=F32)
            z = z_ref[:, cs]
            y = mx * sc_ref[:, cs] * (z * jax.nn.sigmoid(z))
            y_ref[:, cs] = y.astype(BF16)
            p_ref[:, cs] = pb
            mx_ref[:, cs] = mx

    row_spec = pl.BlockSpec((tile, dm), lambda i: (i, 0))
    return pl.pallas_call(
        body, name="pool_mid_fwd", grid=(seq // tile,),
        out_shape=(jax.ShapeDtypeStruct((seq, dm), BF16), jax.ShapeDtypeStruct((seq, dm), BF16),
                   jax.ShapeDtypeStruct((seq, dm), F32)),
        in_specs=[row_spec,
                  pl.BlockSpec((POOL_HALO, dm), lambda i: (jnp.maximum(i * halo_blocks - 1, 0), 0)),
                  pl.BlockSpec((tile, dm), lambda i: (i, 1)),
                  pl.BlockSpec(wg.shape, lambda i: (0, 0, 0)),
                  pl.BlockSpec((1, dm), lambda i: (0, 0))],
        out_specs=(row_spec, row_spec, row_spec),
        compiler_params=_params("parallel"),
    )(h, h, h, wg, scale)


def _pool_mid_backward(dy, mx, h, p, wg, scale, after):
    seq, dm = dy.shape
    gd = dm // len(POOL_WINDOWS)
    tile = _tile(seq, 256)
    n_i = seq // tile

    def body(dy_ref, mx_ref, z_ref, p_ref, wg_ref, sc_ref, after_ref, dh_ref, dwg_ref, st_ref, dwg_acc, carry):
        del after_ref
        i = pl.program_id(0)
        ti = n_i - 1 - i

        @pl.when(i == 0)
        def _():
            dwg_acc[...] = jnp.zeros_like(dwg_acc)
            carry[...] = jnp.zeros_like(carry)
            st_ref[...] = jnp.zeros_like(st_ref)

        row = ti * tile + lax.broadcasted_iota(jnp.int32, (tile, 1), 0)
        count = (row + 1).astype(F32)
        for g, w in enumerate(POOL_WINDOWS):
            cs = slice(g * gd, (g + 1) * gd)
            z = z_ref[:, cs]
            sz, dsz = _silu_parts(z)
            dyg = dy_ref[:, cs]
            mxg = mx_ref[:, cs]
            sc = sc_ref[:, cs]
            t1 = dyg * sz
            st_ref[0:1, cs] += _col_sum(t1 * mxg)
            dh_ref[1, :, cs] = (dyg * (mxg * sc) * dsz).astype(BF16)
            dmx = (t1 * sc).astype(BF16)
            dwg_acc[g] += lax.dot_general(p_ref[:, cs], dmx, TN, preferred_element_type=F32)
            dp = lax.dot_general(dmx, wg_ref[g], NT, preferred_element_type=F32)
            e = dp * (1.0 / jnp.minimum(count, float(w)))
            s = jnp.concatenate([e, carry[:, cs]], axis=0)
            n = tile + POOL_HALO
            sh = 1
            while sh < w:
                s = s + pltpu.roll(s, n - sh, 0)
                sh *= 2
            dh_ref[0, :, cs] = (s[:tile, :] - dp).astype(BF16)
            carry[:, cs] = e[:POOL_HALO, :]

        @pl.when(i == n_i - 1)
        def _():
            dwg_ref[...] = dwg_acc[...].astype(BF16)

    row_spec = pl.BlockSpec((tile, dm), lambda i: (n_i - 1 - i, 0))
    return pl.pallas_call(
        body, name="pool_mid_bwd", grid=(n_i,),
        out_shape=(jax.ShapeDtypeStruct((2, seq, dm), BF16), jax.ShapeDtypeStruct(wg.shape, BF16),
                   jax.ShapeDtypeStruct((STAT_ROWS, dm), F32)),
        in_specs=[row_spec, row_spec,
                  pl.BlockSpec((tile, dm), lambda i: (n_i - 1 - i, 1)),
                  row_spec,
                  pl.BlockSpec(wg.shape, lambda i: (0, 0, 0)),
                  pl.BlockSpec((1, dm), lambda i: (0, 0)), ANY],
        out_specs=(pl.BlockSpec((2, tile, dm), lambda i: (0, n_i - 1 - i, 0)),
                   pl.BlockSpec(wg.shape, lambda i: (0, 0, 0)),
                   pl.BlockSpec((STAT_ROWS, dm), lambda i: (0, 0))),
        scratch_shapes=[pltpu.VMEM(wg.shape, F32), pltpu.VMEM((POOL_HALO, dm), F32)],
        compiler_params=_params("arbitrary"),
    )(dy, mx, h, p, wg, scale, after)


def _out_proj_norm(y, w, x, gain, bias):
    seq, dm = x.shape
    tile = _tile(seq, 256)

    def body(y_ref, w_ref, x_ref, g_ref, b_ref, xhat_ref, rstd_ref, xb_ref):
        o = jnp.dot(y_ref[...], w_ref[...], preferred_element_type=F32)
        xhat, rstd = _layer_norm_stats(ALPHA * x_ref[...] + o)
        xhat_ref[...] = xhat
        rstd_ref[...] = rstd
        xb_ref[...] = (xhat * g_ref[...] + b_ref[...]).astype(BF16)

    row_spec = pl.BlockSpec((tile, dm), lambda i: (i, 0))
    vec_spec = pl.BlockSpec((1, dm), lambda i: (0, 0))
    return pl.pallas_call(
        body, name="out_proj_norm_a", grid=(seq // tile,),
        out_shape=(jax.ShapeDtypeStruct((seq, dm), F32), jax.ShapeDtypeStruct((seq, 1), F32),
                   jax.ShapeDtypeStruct((seq, dm), BF16)),
        in_specs=[row_spec, pl.BlockSpec(w.shape, lambda i: (0, 0)), row_spec, vec_spec, vec_spec],
        out_specs=(row_spec, pl.BlockSpec((tile, 1), lambda i: (i, 0)), row_spec),
        compiler_params=_params("parallel"),
    )(y, w, x, gain, bias)


def _kv_proj(xb, wkv, tables):
    seq, dm = xb.shape
    kvw = wkv.shape[1] // 2
    n_kv = kvw // HEAD_DIM
    tile = _tile(seq, 512)

    def body(x_ref, w_ref, cos_ref, sa_ref, sb_ref, kd_ref, vd_ref, kt_ref, vt_ref):
        kv = jnp.dot(x_ref[...], w_ref[...], preferred_element_type=F32)
        low = lax.broadcasted_iota(jnp.int32, (1, LANES), 1) < HEAD_DIM
        cos, sa, sb = cos_ref[...], sa_ref[...], sb_ref[...]

        def put(pair, h, nat_ref, t_ref):
            swapped = pltpu.roll(pair, HEAD_DIM, 1)
            for head, dup in ((h, jnp.where(low, pair, swapped)), (h + 1, jnp.where(low, swapped, pair))):
                nat_ref[head] = dup.astype(BF16)
                t_ref[head] = dup.T.astype(BF16)

        for j in range(kvw // LANES):
            put(_rope(kv[:, j * LANES:(j + 1) * LANES], cos, sa, sb), 2 * j, kd_ref, kt_ref)
            put(kv[:, kvw + j * LANES:kvw + (j + 1) * LANES], 2 * j, vd_ref, vt_ref)

    tab_spec = pl.BlockSpec((tile, LANES), lambda i: (i, 0))
    dup_spec = pl.BlockSpec((n_kv, tile, LANES), lambda i: (0, i, 0))
    dup_shape = jax.ShapeDtypeStruct((n_kv, seq, LANES), BF16)
    t_spec = pl.BlockSpec((n_kv, LANES, tile), lambda i: (0, 0, i))
    t_shape = jax.ShapeDtypeStruct((n_kv, LANES, seq), BF16)
    return pl.pallas_call(
        body, name="kv_proj", grid=(seq // tile,),
        out_shape=(dup_shape, dup_shape, t_shape, t_shape),
        in_specs=[pl.BlockSpec((tile, dm), lambda i: (i, 0)), pl.BlockSpec(wkv.shape, lambda i: (0, 0)),
                  tab_spec, tab_spec, tab_spec],
        out_specs=(dup_spec, dup_spec, t_spec, t_spec),
        compiler_params=_params("parallel"),
    )(xb, wkv, *tables)


def _head_queries(q_ref, low):
    parts = []
    for j in range(GQA_GROUP // 2):
        q2 = q_ref[:, j * LANES:(j + 1) * LANES]
        parts += [jnp.where(low, q2, 0), jnp.where(low, 0, q2)]
    return parts


def _probs_transposed(n, kh, kcat, q_all, sink_ref):
    st = lax.dot_general(kcat, q_all, NT, preferred_element_type=F32)
    key = lax.broadcasted_iota(jnp.int32, (2 * ATTN_BLOCK, ATTN_BLOCK), 0)
    qry = lax.broadcasted_iota(jnp.int32, (2 * ATTN_BLOCK, ATTN_BLOCK), 1)
    valid = (key > qry) & (key <= qry + ATTN_BLOCK) & ((key >= ATTN_BLOCK) | (n > 0))
    st = st + jnp.tile(jnp.where(valid, 0.0, NEG_INF), (1, GQA_GROUP))
    sink = jnp.concatenate([jnp.full((1, ATTN_BLOCK), sink_ref[0, kh * GQA_GROUP + h], F32)
                            for h in range(GQA_GROUP)], axis=1)
    m = jnp.maximum(jnp.max(st, axis=0, keepdims=True), sink)
    e = jnp.exp(st - m)
    e_sink = jnp.exp(sink - m)
    inv = 1.0 / (jnp.sum(e, axis=0, keepdims=True) + e_sink)
    return e * inv, e_sink * inv


def _attn_specs(n_width):
    q_spec = pl.BlockSpec((ATTN_BLOCK, n_width), lambda kh, n: (n, kh))
    cur = pl.BlockSpec((None, ATTN_BLOCK, LANES), lambda kh, n: (kh, n, 0))
    prev = pl.BlockSpec((None, ATTN_BLOCK, LANES), lambda kh, n: (kh, jnp.maximum(n - 1, 0), 0))
    cur_t = pl.BlockSpec((None, LANES, ATTN_BLOCK), lambda kh, n: (kh, 0, n))
    prev_t = pl.BlockSpec((None, LANES, ATTN_BLOCK), lambda kh, n: (kh, 0, jnp.maximum(n - 1, 0)))
    return q_spec, cur, prev, cur_t, prev_t


def _pair_product_transposed(mat_t, rhs, j, low_rows):
    a = rhs[:, 2 * j * ATTN_BLOCK:(2 * j + 1) * ATTN_BLOCK]
    b = rhs[:, (2 * j + 1) * ATTN_BLOCK:(2 * j + 2) * ATTN_BLOCK]
    out_t = (jnp.dot(jnp.where(low_rows, mat_t, 0), a, preferred_element_type=F32)
             + jnp.dot(jnp.where(low_rows, 0, mat_t), b, preferred_element_type=F32))
    return out_t.T


def _attn_forward(qs, kd, vt, zb, sinks):
    seq, dm = qs.shape
    n_kv = kd.shape[0]
    gw = GQA_GROUP * HEAD_DIM

    def body(q_ref, kp_ref, kc_ref, vtp_ref, vtc_ref, z_ref, sink_ref, att_ref, yb_ref):
        kh, n = pl.program_id(0), pl.program_id(1)
        low = lax.broadcasted_iota(jnp.int32, (1, LANES), 1) < HEAD_DIM
        low_rows = lax.broadcasted_iota(jnp.int32, (LANES, 1), 0) < HEAD_DIM
        kcat = jnp.concatenate([kp_ref[...], kc_ref[...]], axis=0)
        vt = jnp.concatenate([vtp_ref[...], vtc_ref[...]], axis=1)
        q_all = jnp.concatenate(_head_queries(q_ref, low), axis=0)
        probs_t, _ = _probs_transposed(n, kh, kcat, q_all, sink_ref)
        pt = probs_t.astype(BF16)
        for j in range(GQA_GROUP // 2):
            cs = slice(j * LANES, (j + 1) * LANES)
            o2 = _pair_product_transposed(vt, pt, j, low_rows)
            att_ref[:, cs] = o2
            z = z_ref[:, cs]
            yb_ref[:, cs] = (o2 * (z * jax.nn.sigmoid(z))).astype(BF16)

    q_spec, cur, prev, cur_t, prev_t = _attn_specs(gw)
    return pl.pallas_call(
        body, name="attn_fwd", grid=(n_kv, seq // ATTN_BLOCK),
        out_shape=(jax.ShapeDtypeStruct((seq, dm), F32), jax.ShapeDtypeStruct((seq, dm), BF16)),
        in_specs=[q_spec, prev, cur, prev_t, cur_t, q_spec, pl.BlockSpec(memory_space=pltpu.SMEM)],
        out_specs=(q_spec, q_spec),
        compiler_params=_params("parallel", "parallel"),
    )(qs, kd, kd, vt, vt, zb, sinks)


def _attn_backward(qs, kd, vd, kt, zb, att, dyb, sinks, tables, after):
    seq, dm = qs.shape
    n_kv = kd.shape[0]
    gw = GQA_GROUP * HEAD_DIM
    n_blocks = seq // ATTN_BLOCK

    def body(q_ref, kp_ref, kc_ref, vp_ref, vc_ref, ktp_ref, ktc_ref, z_ref, att_ref, dyb_ref, sink_ref,
             cos_ref, sa_ref, sb_ref, after_ref, dh_ref, dk_ref, dv_ref, ds_ref):
        del after_ref
        kh, n = pl.program_id(0), pl.program_id(1)

        @pl.when(n == 0)
        def _():
            dk_ref[...] = jnp.zeros_like(dk_ref)
            dv_ref[...] = jnp.zeros_like(dv_ref)

        @pl.when(jnp.logical_and(n == 0, kh == 0))
        def _():
            ds_ref[...] = jnp.zeros_like(ds_ref)

        low = lax.broadcasted_iota(jnp.int32, (1, LANES), 1) < HEAD_DIM
        low_rows = lax.broadcasted_iota(jnp.int32, (LANES, 1), 0) < HEAD_DIM
        head_lane = lax.broadcasted_iota(jnp.int32, (1, LANES), 1)
        kcat = jnp.concatenate([kp_ref[...], kc_ref[...]], axis=0)
        vcat = jnp.concatenate([vp_ref[...], vc_ref[...]], axis=0)
        kt = jnp.concatenate([ktp_ref[...], ktc_ref[...]], axis=1)
        cos, sa, sb = cos_ref[...], sa_ref[...], sb_ref[...]
        q_parts = _head_queries(q_ref, low)
        q_all = jnp.concatenate(q_parts, axis=0)
        d_parts = []
        for j in range(GQA_GROUP // 2):
            cs = slice(j * LANES, (j + 1) * LANES)
            sz, dsz = _silu_parts(z_ref[:, cs])
            dy2 = dyb_ref[:, cs]
            dh_ref[1, :, cs] = (dy2 * att_ref[:, cs] * dsz).astype(BF16)
            datt = (dy2 * sz).astype(BF16)
            d_parts += [jnp.where(low, datt, 0), jnp.where(low, 0, datt)]
        d_all = jnp.concatenate(d_parts, axis=0)
        probs_t, sink_p = _probs_transposed(n, kh, kcat, q_all, sink_ref)
        dprobs_t = lax.dot_general(vcat, d_all, NT, preferred_element_type=F32)
        row_dot = jnp.sum(probs_t * dprobs_t, axis=0, keepdims=True)
        ds_t = (probs_t * (dprobs_t - row_dot)).astype(BF16)
        dk = jnp.dot(ds_t, q_all, preferred_element_type=F32)
        dv = jnp.dot(probs_t.astype(BF16), d_all, preferred_element_type=F32)
        for j in range(GQA_GROUP // 2):
            dq2 = _pair_product_transposed(kt, ds_t, j, low_rows)
            dh_ref[0, :, j * LANES:(j + 1) * LANES] = (_rope_transposed(dq2, cos, sa, sb) * 0.125).astype(BF16)
        sink_dot = sink_p * row_dot
        dsink = jnp.zeros((1, LANES), F32)
        for h in range(GQA_GROUP):
            part = jnp.sum(sink_dot[:, h * ATTN_BLOCK:(h + 1) * ATTN_BLOCK], axis=1, keepdims=True)
            dsink = dsink - jnp.where(head_lane == kh * GQA_GROUP + h, part, 0.0)
        ds_ref[0:1, :] += dsink

        @pl.when(n == 0)
        def _():
            dk_ref[pl.ds(0, ATTN_BLOCK), :] += dk[ATTN_BLOCK:, :]
            dv_ref[pl.ds(0, ATTN_BLOCK), :] += dv[ATTN_BLOCK:, :]

        @pl.when(n > 0)
        def _():
            start = pl.multiple_of((n - 1) * ATTN_BLOCK, ATTN_BLOCK)
            dk_ref[pl.ds(start, 2 * ATTN_BLOCK), :] += dk
            dv_ref[pl.ds(start, 2 * ATTN_BLOCK), :] += dv

    q_spec, cur, prev, cur_t, prev_t = _attn_specs(gw)
    tab_spec = pl.BlockSpec((ATTN_BLOCK, LANES), lambda kh, n: (n, 0))
    acc_spec = pl.BlockSpec((None, seq, LANES), lambda kh, n: (kh, 0, 0))
    acc_shape = jax.ShapeDtypeStruct((n_kv, seq, LANES), F32)
    return pl.pallas_call(
        body, name="attn_bwd", grid=(n_kv, n_blocks),
        out_shape=(jax.ShapeDtypeStruct((2, seq, dm), BF16), acc_shape, acc_shape,
                   jax.ShapeDtypeStruct((STAT_ROWS, LANES), F32)),
        in_specs=[q_spec, prev, cur, prev, cur, prev_t, cur_t, q_spec, q_spec, q_spec,
                  pl.BlockSpec(memory_space=pltpu.SMEM), tab_spec, tab_spec, tab_spec, ANY],
        out_specs=(pl.BlockSpec((2, ATTN_BLOCK, gw), lambda kh, n: (0, n, kh)), acc_spec, acc_spec,
                   pl.BlockSpec((STAT_ROWS, LANES), lambda kh, n: (0, 0))),
        compiler_params=_params("arbitrary", "arbitrary"),
    )(qs, kd, kd, vd, vd, kt, kt, zb, att, dyb, sinks, *tables, after)


def _kv_grad_fold(dk, dv, tables):
    n_kv, seq, _ = dk.shape
    kvw = n_kv * HEAD_DIM
    tile = _tile(seq, 512)

    def body(dk_ref, dv_ref, cos_ref, sa_ref, sb_ref, o_ref):
        low = lax.broadcasted_iota(jnp.int32, (1, LANES), 1) < HEAD_DIM
        cos, sa, sb = cos_ref[...], sa_ref[...], sb_ref[...]

        def folded(ref, h):
            t = ref[h]
            return t + pltpu.roll(t, HEAD_DIM, 1)

        for j in range(n_kv // 2):
            ka = _rope_transposed(folded(dk_ref, 2 * j), cos, sa, sb)
            kb = _rope_transposed(folded(dk_ref, 2 * j + 1), cos, sa, sb)
            o_ref[:, j * LANES:(j + 1) * LANES] = jnp.where(low, ka, kb).astype(BF16)
            o_ref[:, kvw + j * LANES:kvw + (j + 1) * LANES] = jnp.where(
                low, folded(dv_ref, 2 * j), folded(dv_ref, 2 * j + 1)).astype(BF16)

    tab_spec = pl.BlockSpec((tile, LANES), lambda i: (i, 0))
    in_spec = pl.BlockSpec((n_kv, tile, LANES), lambda i: (0, i, 0))
    return pl.pallas_call(
        body, name="kv_grad_fold", grid=(seq // tile,),
        out_shape=jax.ShapeDtypeStruct((seq, 2 * kvw), BF16),
        in_specs=[in_spec, in_spec, tab_spec, tab_spec, tab_spec],
        out_specs=pl.BlockSpec((tile, 2 * kvw), lambda i: (i, 0)),
        compiler_params=_params("parallel"),
    )(dk, dv, *tables)


def _out_proj_norm_loss(yb, w, xhat1, gain0, bias0, gain1, bias1, target):
    seq, dm = xhat1.shape
    tile = _tile(seq, 256)

    def body(y_ref, w_ref, xh1_ref, g0_ref, b0_ref, g1_ref, b1_ref, t_ref, dr_ref, drb_ref, st_ref):
        i = pl.program_id(0)

        @pl.when(i == 0)
        def _():
            st_ref[...] = jnp.zeros_like(st_ref)

        ob = jnp.dot(y_ref[...], w_ref[...], preferred_element_type=F32)
        x1 = xh1_ref[...] * g0_ref[...] + b0_ref[...]
        xhat, rstd = _layer_norm_stats(ALPHA * x1 + ob)
        err = xhat * g1_ref[...] + b1_ref[...] - t_ref[...]
        dout = err * (1.0 / dm)
        dr = _layer_norm_backward(dout, xhat, rstd, g1_ref[...])
        dr_ref[...] = dr
        drb_ref[...] = dr.astype(BF16)
        st_ref[0:1, :] += _col_sum(dout * xhat)
        st_ref[1:2, :] += _col_sum(dout)
        st_ref[2:3, :] += _col_sum(err * err)

    row_spec = pl.BlockSpec((tile, dm), lambda i: (i, 0))
    vec_spec = pl.BlockSpec((1, dm), lambda i: (0, 0))
    return pl.pallas_call(
        body, name="out_proj_norm_loss_b", grid=(seq // tile,),
        out_shape=(jax.ShapeDtypeStruct((seq, dm), F32), jax.ShapeDtypeStruct((seq, dm), BF16),
                   jax.ShapeDtypeStruct((STAT_ROWS, dm), F32)),
        in_specs=[row_spec, pl.BlockSpec(w.shape, lambda i: (0, 0)), row_spec, vec_spec, vec_spec, vec_spec,
                  vec_spec, row_spec],
        out_specs=(row_spec, row_spec, pl.BlockSpec((STAT_ROWS, dm), lambda i: (0, 0))),
        compiler_params=_params("arbitrary"),
    )(yb, w, xhat1, gain0, bias0, gain1, bias1, target)


def _stream_grad_norm_backward(dhq, wqg, dkv, wkv, dr2, xhat1, rstd1, gain0):
    seq, dm = dr2.shape
    tile = _tile(seq, 256)

    def body(dh_ref, wqg_ref, dkv_ref, wkv_ref, dr2_ref, xh_ref, rstd_ref, g_ref, dr_ref, drb_ref, st_ref, acc):
        i, k = pl.program_id(0), pl.program_id(1)
        part = lax.dot_general(dh_ref[...], wqg_ref[...], NT, preferred_element_type=F32)

        @pl.when(k == 0)
        def _():
            acc[...] = part + lax.dot_general(dkv_ref[...], wkv_ref[...], NT, preferred_element_type=F32)

        @pl.when(jnp.logical_and(k == 1, i == 0))
        def _():
            st_ref[...] = jnp.zeros_like(st_ref)

        @pl.when(k == 1)
        def _():
            dx1 = acc[...] + part + ALPHA * dr2_ref[...]
            xhat = xh_ref[...]
            dr = _layer_norm_backward(dx1, xhat, rstd_ref[...], g_ref[...])
            dr_ref[...] = dr
            drb_ref[...] = dr.astype(BF16)
            st_ref[0:1, :] += _col_sum(dx1 * xhat)
            st_ref[1:2, :] += _col_sum(dx1)

    row_spec = pl.BlockSpec((tile, dm), lambda i, k: (i, 0))
    return pl.pallas_call(
        body, name="stream_grad_norm_bwd", grid=(seq // tile, 2),
        out_shape=(jax.ShapeDtypeStruct((seq, dm), F32), jax.ShapeDtypeStruct((seq, dm), BF16),
                   jax.ShapeDtypeStruct((STAT_ROWS, dm), F32)),
        in_specs=[pl.BlockSpec((None, tile, dm), lambda i, k: (k, i, 0)),
                  pl.BlockSpec((dm, dm), lambda i, k: (0, k)),
                  pl.BlockSpec((tile, dkv.shape[1]), lambda i, k: (i, 0)),
                  pl.BlockSpec(wkv.shape, lambda i, k: (0, 0)),
                  row_spec, row_spec, pl.BlockSpec((tile, 1), lambda i, k: (i, 0)),
                  pl.BlockSpec((1, dm), lambda i, k: (0, 0))],
        out_specs=(row_spec, row_spec, pl.BlockSpec((STAT_ROWS, dm), lambda i, k: (0, 0))),
        scratch_shapes=[pltpu.VMEM((tile, dm), F32)],
        compiler_params=_params("arbitrary", "arbitrary"),
    )(dhq, wqg, dkv, wkv, dr2, xhat1, rstd1, gain0)


def _adamw_math(w, g, m, v):
    m = ADAM_B1 * m + (1.0 - ADAM_B1) * g
    v = ADAM_B2 * v + (1.0 - ADAM_B2) * (g * g)
    m_hat = m / (1.0 - ADAM_B1 ** ADAM_STEP)
    v_hat = v / (1.0 - ADAM_B2 ** ADAM_STEP)
    delta = -ADAM_LR * (m_hat / (jnp.sqrt(v_hat) + ADAM_EPS) + ADAM_WD * w)
    return delta, m, v


def _sum_devices(ref):
    total = ref[0].astype(F32)
    for d in range(1, N_DEV):
        total = total + ref[d].astype(F32)
    return total


def _adamw_shard(name, parts, w, m, v, after):
    rows, cols = w.shape
    tr = _tile(rows, max(8, (1 << 18) // cols)) if rows >= 8 else rows

    def body(p_ref, w_ref, m_ref, v_ref, after_ref, g_out, d_out, m_out, v_out):
        del after_ref
        g = _sum_devices(p_ref)
        delta, m_new, v_new = _adamw_math(w_ref[...], g, m_ref[...], v_ref[...])
        g_out[...] = g
        d_out[...] = delta
        m_out[...] = m_new
        v_out[...] = v_new

    spec = pl.BlockSpec((tr, cols), lambda i: (i, 0))
    shape = jax.ShapeDtypeStruct((rows, cols), F32)
    return pl.pallas_call(
        body, name=name, grid=(rows // tr,),
        out_shape=(shape, shape, shape, shape),
        in_specs=[pl.BlockSpec((N_DEV, tr, cols), lambda i: (0, i, 0)), spec, spec, spec, ANY],
        out_specs=(spec, spec, spec, spec),
        compiler_params=_params("parallel"),
    )(parts, w, m, v, after)


def _adamw_replicated(stats_b, stats_a, sink_parts, ln_g, ln_b, sinks, m_ln_g, m_ln_b, m_sinks, v_ln_g, v_ln_b,
                      v_sinks):
    n_q = sinks.shape[1]

    def body(sb_ref, sa_ref, sk_ref, g_ref, b_ref, s_ref, mg_ref, mb_ref, ms_ref, vg_ref, vb_ref, vs_ref, *outs):
        layer_sums = (_sum_devices(sa_ref), _sum_devices(sb_ref))
        for which, (w_ref, m_ref, v_ref) in enumerate(((g_ref, mg_ref, vg_ref), (b_ref, mb_ref, vb_ref))):
            for layer in range(DEPTH):
                row = slice(layer, layer + 1)
                g = layer_sums[layer][which:which + 1, :]
                res = (g,) + _adamw_math(w_ref[row, :], g, m_ref[row, :], v_ref[row, :])
                for o_ref, val in zip(outs[4 * which:4 * which + 4], res):
                    o_ref[row, :] = val
        g = _sum_devices(sk_ref)[0:1, 0:n_q]
        res = (g,) + _adamw_math(s_ref[...], g, ms_ref[...], vs_ref[...])
        for o_ref, val in zip(outs[8:12], res):
            o_ref[...] = val

    vmem = pl.BlockSpec(memory_space=pltpu.VMEM)
    shapes = [jax.ShapeDtypeStruct(a.shape, F32) for a in (ln_g, ln_b, sinks) for _ in range(4)]
    return pl.pallas_call(
        body, name="adamw_replicated", out_shape=tuple(shapes),
        in_specs=[vmem] * 12, out_specs=tuple([vmem] * 12),
    )(stats_b, stats_a, sink_parts, ln_g, ln_b, sinks, m_ln_g, m_ln_b, m_sinks, v_ln_g, v_ln_b, v_sinks)


def kernel(x, ln_g, ln_b, a_w_in, a_w_group, a_scale, a_w_out, b_w_k, b_w_v, b_w_qg, b_sinks, b_w_out, loss_target, m_ln_g, m_ln_b, m_a_w_in, m_a_w_group, m_a_scale, m_a_w_out, m_b_w_k, m_b_w_v, m_b_w_qg, m_b_sinks, m_b_w_out, v_ln_g, v_ln_b, v_a_w_in, v_a_w_group, v_a_scale, v_a_w_out, v_b_w_k, v_b_w_v, v_b_w_qg, v_b_sinks, v_b_w_out):
    _, seq, dm = x.shape
    n_groups = len(POOL_WINDOWS)
    gd = dm // n_groups
    kvw = b_w_k.shape[1]
    cb = 2 * dm // N_DEV
    rb = dm // N_DEV
    gb = gd // N_DEV

    x2 = x.reshape(seq, dm)
    target = loss_target.reshape(seq, dm)
    w_in_s = a_w_in.reshape(dm, cb)
    w_g_s = a_w_group.reshape(n_groups, gb, gd)
    w_out_s = a_w_out.reshape(rb, dm)
    w_qg_s = b_w_qg.reshape(dm, cb)
    w_outb_s = b_w_out.reshape(rb, dm)

    def cols(ref, dev):
        return ref.at[:, pl.ds(pl.multiple_of(dev * cb, LANES), cb)]

    def rows(ref, dev):
        return ref.at[pl.ds(pl.multiple_of(dev * rb, 8), rb), :]

    def group_rows(ref, dev):
        return ref.at[:, pl.ds(pl.multiple_of(dev * gb, 8), gb), :]

    def k_rows(ref, dev):
        return ref.at[pl.ds(pl.multiple_of(dev * rb, 8), rb), pl.ds(0, kvw)]

    def v_rows(ref, dev):
        return ref.at[pl.ds(pl.multiple_of(dev * rb, 8), rb), pl.ds(kvw, kvw)]

    def scale_cols(ref, dev):
        return ref.at[:, pl.ds(pl.multiple_of(dev * rb, LANES), rb)]

    bf = lambda a: a.astype(BF16)
    wide, square = jax.ShapeDtypeStruct((dm, 2 * dm), BF16), jax.ShapeDtypeStruct((dm, dm), BF16)
    (w_in,) = _gather_weights("gather_a_in", 0, [(bf(w_in_s), 0, cols)], [wide])
    w_g, scale, w_out = _gather_weights(
        "gather_a_rest", 1, [(bf(w_g_s), 0, group_rows), (a_scale, 1, scale_cols), (bf(w_out_s), 2, rows)],
        [jax.ShapeDtypeStruct((n_groups, gd, gd), BF16), jax.ShapeDtypeStruct((1, dm), F32), square])
    w_kv, w_qg = _gather_weights(
        "gather_b_in", 2, [(bf(b_w_k), 0, k_rows), (bf(b_w_v), 0, v_rows), (bf(w_qg_s), 1, cols)],
        [jax.ShapeDtypeStruct((dm, 2 * kvw), BF16), wide])
    (w_outb,) = _gather_weights("gather_b_out", 3, [(bf(w_outb_s), 0, rows)], [square])

    tables = _rope_tables(seq)
    bm = _tile(seq, 1024)
    bn = _tile(dm, 1024)
    g0, g1, b0, b1 = ln_g[0:1], ln_g[1:2], ln_b[0:1], ln_b[1:2]

    xb = _cast_bf16("cast_x", x2)
    h = _mm("a_in_proj", xb, w_in, dims=NN, grid=(seq // bm, 2 * dm // bn),
            a_spec=pl.BlockSpec((bm, dm), lambda i, j: (i, 0)), b_spec=pl.BlockSpec((dm, bn), lambda i, j: (0, j)),
            out_shape=jax.ShapeDtypeStruct((seq, 2 * dm), F32), out_spec=pl.BlockSpec((bm, bn), lambda i, j: (i, j)))
    y, pooled, mixed = _pool_mid_forward(h, w_g, scale)
    xhat1, rstd1, x1b = _out_proj_norm(y, w_out, x2, g0, b0)

    kd, vd, kt, vt = _kv_proj(x1b, w_kv, tables)
    bmq = _tile(seq, 512)
    tab_spec = pl.BlockSpec((bmq, LANES), lambda i, j: (i, 0))

    def rope_scale(val, cos_ref, sa_ref, sb_ref):
        cos, sa, sb = cos_ref[...], sa_ref[...], sb_ref[...]
        return jnp.concatenate([_rope(val[:, j * LANES:(j + 1) * LANES], cos, sa, sb) * 0.125
                                for j in range(val.shape[1] // LANES)], axis=1)

    qs = _mm("b_q_proj", x1b, w_qg, dims=NN, grid=(seq // bmq, dm // bn),
             a_spec=pl.BlockSpec((bmq, dm), lambda i, j: (i, 0)), b_spec=pl.BlockSpec((dm, bn), lambda i, j: (0, j)),
             out_shape=jax.ShapeDtypeStruct((seq, dm), BF16), out_spec=pl.BlockSpec((bmq, bn), lambda i, j: (i, j)),
             epilogue=rope_scale, extras=tables, extra_specs=(tab_spec,) * 3)
    zb = _mm("b_gate_proj", x1b, w_qg, dims=NN, grid=(seq // bm, dm // bn),
             a_spec=pl.BlockSpec((bm, dm), lambda i, j: (i, 0)),
             b_spec=pl.BlockSpec((dm, bn), lambda i, j: (0, j + dm // bn)),
             out_shape=jax.ShapeDtypeStruct((seq, dm), F32), out_spec=pl.BlockSpec((bm, bn), lambda i, j: (i, j)))
    att, yb = _attn_forward(qs, kd, vt, zb, b_sinks)
    dr2, dr2b, stats_b = _out_proj_norm_loss(yb, w_outb, xhat1, g0, b0, g1, b1, target)

    def weight_grad(name, a, b, n_cols, b_spec=None):
        m_cols = a.shape[1]
        tm, tn = _tile(m_cols, 1024), _tile(n_cols, 512)
        return _mm(name, a, b, dims=TN, grid=(m_cols // tm, n_cols // tn),
                   a_spec=pl.BlockSpec((seq, tm), lambda i, j: (0, i)),
                   b_spec=b_spec(tn) if b_spec else pl.BlockSpec((seq, tn), lambda i, j: (0, j)),
                   out_shape=jax.ShapeDtypeStruct((m_cols, n_cols), BF16),
                   out_spec=pl.BlockSpec((tm, tn), lambda i, j: (i, j)))

    def halves_spec(tn):
        per = dm // tn
        return pl.BlockSpec((None, seq, tn), lambda i, j: (j // per, 0, j % per))

    def times_transposed(name, a, w):
        return _mm(name, a, w, dims=NT, grid=(seq // bm, dm // bn),
                   a_spec=pl.BlockSpec((bm, a.shape[1]), lambda i, j: (i, 0)),
                   b_spec=pl.BlockSpec((bn, w.shape[1]), lambda i, j: (j, 0)),
                   out_shape=jax.ShapeDtypeStruct((seq, dm), F32), out_spec=pl.BlockSpec((bm, bn), lambda i, j: (i, j)))

    def stat_row_cols(ref, dev):
        return ref.at[pl.ds(0, 1), pl.ds(pl.multiple_of(dev * rb, LANES), rb)]

    dyb = times_transposed("b_out_proj_dx", dr2b, w_outb)
    d_w_outb = weight_grad("b_out_proj_dw", yb, dr2b, dm)
    (p_outb,) = _exchange_blocks("scatter_b_out", [(d_w_outb, rows, (rb, dm))], 4)
    dhq, dkd, dvd, dsink = _attn_backward(qs, kd, vd, kt, zb, att, dyb, b_sinks, tables, after=d_w_outb)
    dkv = _kv_grad_fold(dkd, dvd, tables)
    d_w_qg = weight_grad("b_qg_proj_dw", x1b, dhq, 2 * dm, halves_spec)
    d_w_kv = weight_grad("b_kv_proj_dw", x1b, dkv, 2 * kvw)
    p_qg, p_k, p_v = _exchange_blocks("scatter_b_in", [
        (d_w_qg, cols, (dm, cb)), (d_w_kv, k_rows, (rb, kvw)), (d_w_kv, v_rows, (rb, kvw))], 5)
    dr1, dr1b, stats_a = _stream_grad_norm_backward(dhq, w_qg, dkv, w_kv, dr2, xhat1, rstd1, g0)

    dy = times_transposed("a_out_proj_dx", dr1b, w_out)
    d_w_out = weight_grad("a_out_proj_dw", y, dr1b, dm)
    (p_out,) = _exchange_blocks("scatter_a_out", [(d_w_out, rows, (rb, dm))], 6)
    dh, d_w_g, stats_s = _pool_mid_backward(dy, mixed, h, pooled, w_g, scale, after=d_w_out)
    p_g, p_scale = _exchange_blocks("scatter_a_mid", [
        (d_w_g, group_rows, (n_groups, gb, gd)), (stats_s, stat_row_cols, (1, rb))], 7)
    d_w_in = weight_grad("a_in_proj_dw", xb, dh, 2 * dm, halves_spec)
    (p_in,) = _exchange_blocks("scatter_a_in", [(d_w_in, cols, (dm, cb))], 8)
    grad_x = _mm("a_in_proj_dx", dh, w_in, dims=NT, grid=(seq // bm, dm // bn, 2), nk=2,
                 a_spec=pl.BlockSpec((None, bm, dm), lambda i, j, k: (k, i, 0)),
                 b_spec=pl.BlockSpec((bn, dm), lambda i, j, k: (j, k)),
                 out_shape=jax.ShapeDtypeStruct((seq, dm), F32), out_spec=pl.BlockSpec((bm, bn), lambda i, j, k: (i, j)),
                 add=dr1, add_spec=pl.BlockSpec((bm, bn), lambda i, j, k: (i, j)), add_scale=ALPHA,
                 extras=(d_w_in,), extra_specs=(ANY,))

    all_b, all_a, all_sink = _exchange_blocks("gather_replicated_grads", [
        (stats_b, None, stats_b.shape), (stats_a, None, stats_a.shape), (dsink, None, dsink.shape)])

    upd = {}
    last = [grad_x]

    def shard_update(key, parts, w, m, v):
        shape = w.shape
        flat = lambda a: a.reshape(-1, shape[-1])
        outs = _adamw_shard("adamw_" + key, parts.reshape(N_DEV, -1, shape[-1]), flat(w), flat(m), flat(v), last[0])
        last[0] = outs[0]
        upd[key] = [o.reshape(shape) for o in outs]

    shard_update("b_w_out", p_outb, b_w_out, m_b_w_out, v_b_w_out)
    shard_update("b_w_qg", p_qg, b_w_qg, m_b_w_qg, v_b_w_qg)
    shard_update("b_w_k", p_k, b_w_k, m_b_w_k, v_b_w_k)
    shard_update("b_w_v", p_v, b_w_v, m_b_w_v, v_b_w_v)
    shard_update("a_w_out", p_out, a_w_out, m_a_w_out, v_a_w_out)
    shard_update("a_w_group", p_g, a_w_group, m_a_w_group, v_a_w_group)
    shard_update("a_scale", p_scale, a_scale, m_a_scale, v_a_scale)
    shard_update("a_w_in", p_in, a_w_in, m_a_w_in, v_a_w_in)
    rep = _adamw_replicated(all_b, all_a, all_sink, ln_g, ln_b, b_sinks, m_ln_g, m_ln_b, m_b_sinks, v_ln_g, v_ln_b,
                            v_b_sinks)
    upd["ln_g"], upd["ln_b"], upd["b_sinks"] = list(rep[0:4]), list(rep[4:8]), list(rep[8:12])

    loss = lax.psum(0.5 * jnp.sum(stats_b[2]) / dm, AXES)
    order = ["ln_g", "ln_b", "a_w_in", "a_w_group", "a_scale", "a_w_out", "b_w_k", "b_w_v", "b_w_qg", "b_sinks",
             "b_w_out"]
    return (loss, grad_x.reshape(x.shape), *[upd[n][0] for n in order], *[upd[n][1] for n in order],
            *[upd[n][2] for n in order], *[upd[n][3] for n in order])
```

```python
import functools

import jax
import jax.numpy as jnp
from jax import lax
from jax.experimental import pallas as pl
from jax.experimental.pallas import tpu as pltpu
from jax.experimental.pallas import tpu_sc as plsc

F32 = jnp.float32
BF16 = jnp.bfloat16
MESH = pl.DeviceIdType.MESH
AXES = ("x", "y", "c")
N_DEV = 8

POOL_WINDOWS = (2, 4, 8, 16)
POOL_HALO = 16
HEAD_DIM = 64
GQA_GROUP = 8
ATTN_BLOCK = 128
ROPE_THETA = 10000.0
LN_EPS = 1e-5
NEG_INF = -1e30
DEPTH = 2
ALPHA = (2 * DEPTH) ** 0.25
ADAM_LR = 0.001
ADAM_B1 = 0.9
ADAM_B2 = 0.999
ADAM_EPS = 1e-08
ADAM_WD = 0.01
ADAM_STEP = 10

LANES = 128
STAT_ROWS = 8


def _tile(n, want):
    t = min(n, want)
    while n % t:
        t //= 2
    return t


def _params(*sem):
    return pltpu.CompilerParams(dimension_semantics=sem)


ANY = pl.BlockSpec(memory_space=pl.ANY)


def _my_pos():
    return lax.axis_index("x"), lax.axis_index("y"), lax.axis_index("c")


def _dev_index(p):
    return 4 * p[0] + 2 * p[1] + p[2]


def _handshake(peers):
    barrier = pltpu.get_barrier_semaphore()
    for peer in peers:
        pl.semaphore_signal(barrier, inc=1, device_id=peer, device_id_type=MESH)
    pl.semaphore_wait(barrier, len(peers))


def _launch_on_sequencer(name, collective_id, body, operands, out_shapes, scratch):
    return pl.kernel(
        body, out_type=tuple(out_shapes), name=name,
        mesh=plsc.ScalarSubcoreMesh(axis_name="sequencer", num_cores=1), scratch_types=scratch,
        compiler_params=pltpu.CompilerParams(collective_id=collective_id),
    )(*operands)


def _gather_weights(name, collective_id, streams, out_shapes):
    n_s = len(streams)
    n_out = len(out_shapes)

    def body(*refs):
        srcs = refs[:n_s]
        outs = refs[n_s:n_s + n_out]
        send_sems, recv_sems, local_sems = refs[n_s + n_out:]
        x, y, c = _my_pos()
        me, sibling = (x, y, c), (x, y, 1 - c)
        chips = [(1 - x, y), (x, 1 - y), (1 - x, 1 - y)]
        _handshake([sibling] + [(*chip, c) for chip in chips])

        def copy(s, k, block, to, from_shard=False):
            out_ref = outs[streams[s][1]]
            win = streams[s][2](out_ref, _dev_index(block))
            return pltpu.make_async_remote_copy(
                src_ref=srcs[s] if from_shard else win, dst_ref=win,
                send_sem=send_sems.at[7 * s + k], recv_sem=recv_sems.at[7 * s + k],
                device_id=to, device_id_type=MESH)

        mine = [pltpu.make_async_copy(srcs[s], streams[s][2](outs[streams[s][1]], _dev_index(me)), local_sems.at[s])
                for s in range(n_s)]
        for cp in mine:
            cp.start()
        first = []
        for s in range(n_s):
            first.append(copy(s, 0, me, sibling, True))
            first += [copy(s, 1 + j, me, (*chip, c), True) for j, chip in enumerate(chips)]
        for cp in first:
            cp.start()
        passed = []
        for j, chip in enumerate(chips):
            for s in range(n_s):
                copy(s, 1 + j, (*chip, c), me).wait_recv()
                fwd = copy(s, 4 + j, (*chip, c), sibling)
                fwd.start()
                passed.append(fwd)
        for s in range(n_s):
            copy(s, 0, sibling, me).wait_recv()
            for j, chip in enumerate(chips):
                copy(s, 4 + j, (*chip, 1 - c), me).wait_recv()
        for cp in first + passed:
            cp.wait_send()
        for cp in mine:
            cp.wait()

    scratch = [pltpu.SemaphoreType.DMA((7 * n_s,)), pltpu.SemaphoreType.DMA((7 * n_s,)),
               pltpu.SemaphoreType.DMA((n_s,))]
    return _launch_on_sequencer(name, collective_id, body, [s[0] for s in streams], out_shapes, scratch)


def _exchange_blocks(name, streams, collective_id):
    n_s = len(streams)

    def body(*refs):
        srcs = refs[:n_s]
        outs = refs[n_s:2 * n_s]
        send_sems, recv_sems, local_sems = refs[2 * n_s:]
        x, y, c = _my_pos()
        me = _dev_index((x, y, c))
        _handshake([(1 - x if k & 4 else x, 1 - y if k & 2 else y, 1 - c if k & 1 else c) for k in range(1, N_DEV)])

        def window(s, dev):
            return srcs[s] if streams[s][1] is None else streams[s][1](srcs[s], dev)

        mine = [pltpu.make_async_copy(window(s, me), outs[s].at[me], local_sems.at[s]) for s in range(n_s)]
        for cp in mine:
            cp.start()
        copies = []
        for k in (2, 4, 6, 3, 5, 7, 1):
            peer = (1 - x if k & 4 else x, 1 - y if k & 2 else y, 1 - c if k & 1 else c)
            for s in range(n_s):
                copies.append(pltpu.make_async_remote_copy(
                    src_ref=window(s, _dev_index(peer)), dst_ref=outs[s].at[me],
                    send_sem=send_sems.at[7 * s + k - 1], recv_sem=recv_sems.at[7 * s + k - 1],
                    device_id=peer, device_id_type=MESH))
        for cp in copies:
            cp.start()
        for cp in copies:
            cp.wait()
        for cp in mine:
            cp.wait()

    out_shapes = [jax.ShapeDtypeStruct((N_DEV,) + tuple(s[2]), s[0].dtype) for s in streams]
    scratch = [pltpu.SemaphoreType.DMA((7 * n_s,)), pltpu.SemaphoreType.DMA((7 * n_s,)),
               pltpu.SemaphoreType.DMA((n_s,))]
    return _launch_on_sequencer(name, collective_id, body, [s[0] for s in streams], out_shapes, scratch)


NN = (((1,), (0,)), ((), ()))
NT = (((1,), (1,)), ((), ()))
TN = (((0,), (0,)), ((), ()))


def _mm(name, a, b, *, dims, grid, a_spec, b_spec, out_shape, out_spec, nk=1,
        add=None, add_spec=None, add_scale=1.0, epilogue=None, extras=(), extra_specs=()):
    n_extra = len(extras)
    has_add = add is not None

    def body(*refs):
        a_ref, b_ref = refs[:2]
        pos = 2
        add_ref = None
        if has_add:
            add_ref = refs[pos]
            pos += 1
        extra_refs = refs[pos:pos + n_extra]
        o_ref = refs[pos + n_extra]
        acc_ref = refs[pos + n_extra + 1] if nk > 1 else None

        def finish(val):
            if has_add:
                val = val + add_scale * add_ref[...]
            if epilogue is not None:
                val = epilogue(val, *extra_refs)
            o_ref[...] = val.astype(o_ref.dtype)

        part = lax.dot_general(a_ref[...].astype(BF16), b_ref[...].astype(BF16), dims,
                               preferred_element_type=F32)
        if nk == 1:
            finish(part)
        else:
            k = pl.program_id(2)

            @pl.when(k == 0)
            def _():
                acc_ref[...] = part

            @pl.when(jnp.logical_and(k > 0, k < nk - 1))
            def _():
                acc_ref[...] += part

            @pl.when(k == nk - 1)
            def _():
                finish(acc_ref[...] + part)

    in_specs = [a_spec, b_spec] + ([add_spec] if has_add else []) + list(extra_specs)
    operands = [a, b] + ([add] if has_add else []) + list(extras)
    scratch = [pltpu.VMEM(out_spec.block_shape, F32)] if nk > 1 else []
    sem = ("parallel", "parallel") + (("arbitrary",) if nk > 1 else ())
    return pl.pallas_call(
        body, name=name, grid=grid, out_shape=out_shape,
        in_specs=in_specs, out_specs=out_spec, scratch_shapes=scratch,
        compiler_params=_params(*sem),
    )(*operands)


def _cast_bf16(name, a):
    rows, cols = a.shape
    tr = _tile(rows, 512)

    def body(a_ref, o_ref):
        o_ref[...] = a_ref[...].astype(BF16)

    return pl.pallas_call(
        body, name=name, grid=(rows // tr,),
        out_shape=jax.ShapeDtypeStruct(a.shape, BF16),
        in_specs=[pl.BlockSpec((tr, cols), lambda i: (i, 0))],
        out_specs=pl.BlockSpec((tr, cols), lambda i: (i, 0)),
        compiler_params=_params("parallel"),
    )(a)


def _rope_tables(seq):
    inv_freq = ROPE_THETA ** (-jnp.arange(0, HEAD_DIM, 2, dtype=F32) / HEAD_DIM)
    ang = jnp.arange(seq, dtype=F32)[:, None] * inv_freq[None, :]
    ang = jnp.concatenate([ang, ang, ang, ang], axis=-1)
    cos, sin = jnp.cos(ang), jnp.sin(ang)
    first_half = (jnp.arange(LANES) % HEAD_DIM < HEAD_DIM // 2)[None, :]
    return cos, jnp.where(first_half, -sin, 0.0), jnp.where(first_half, 0.0, sin)


def _rot(t, sin_a, sin_b):
    return pltpu.roll(t, LANES - HEAD_DIM // 2, 1) * sin_a + pltpu.roll(t, HEAD_DIM // 2, 1) * sin_b


def _rope(t, cos, sin_a, sin_b):
    return t * cos + _rot(t, sin_a, sin_b)


def _rope_transposed(dy, cos, sin_a, sin_b):
    return dy * cos - _rot(dy, sin_a, sin_b)


def _silu_parts(z):
    sig = jax.nn.sigmoid(z)
    return z * sig, sig * (1.0 + z * (1.0 - sig))


def _layer_norm_stats(r):
    mu = jnp.mean(r, axis=-1, keepdims=True)
    d = r - mu
    var = jnp.mean(d * d, axis=-1, keepdims=True)
    rstd = lax.rsqrt(var + LN_EPS)
    return d * rstd, rstd


def _layer_norm_backward(dout, xhat, rstd, gain):
    dxh = dout * gain
    m1 = jnp.mean(dxh, axis=-1, keepdims=True)
    m2 = jnp.mean(dxh * xhat, axis=-1, keepdims=True)
    return rstd * (dxh - m1 - xhat * m2)


def _col_sum(v):
    return jnp.sum(v, axis=0, keepdims=True)


def _pool_mid_forward(h, wg, scale):
    seq, d2 = h.shape
    dm = d2 // 2
    gd = dm // len(POOL_WINDOWS)
    tile = _tile(seq, 256)
    halo_blocks = tile // POOL_HALO

    def body(u_ref, up_ref, z_ref, wg_ref, sc_ref, y_ref, p_ref, mx_ref):
        i = pl.program_id(0)
        row = i * tile + lax.broadcasted_iota(jnp.int32, (tile, 1), 0)
        count = (row + 1).astype(F32)
        for g, w in enumerate(POOL_WINDOWS):
            cs = slice(g * gd, (g + 1) * gd)
            u = u_ref[:, cs]
            prev = jnp.where(i > 0, up_ref[:, cs], 0.0)
            s = jnp.concatenate([prev, u], axis=0)
            sh = 1
            while sh < w:
                s = s + pltpu.roll(s, sh, 0)
                sh *= 2
            p = s[POOL_HALO:, :] * (1.0 / jnp.minimum(count, float(w))) - u
            pb = p.astype(BF16)
            mx = jnp.dot(pb, wg_ref[g], preferred_element_type=F32)
            z = z_ref[:, cs]
            y = mx * sc_ref[:, cs] * (z * jax.nn.sigmoid(z))
            y_ref[:, cs] = y.astype(BF16)
            p_ref[:, cs] = pb
            mx_ref[:, cs] = mx

    row_spec = pl.BlockSpec((tile, dm), lambda i: (i, 0))
    return pl.pallas_call(
        body, name="pool_mid_fwd", grid=(seq // tile,),
        out_shape=(jax.ShapeDtypeStruct((seq, dm), BF16), jax.ShapeDtypeStruct((seq, dm), BF16),
                   jax.ShapeDtypeStruct((seq, dm), F32)),
        in_specs=[row_spec,
                  pl.BlockSpec((POOL_HALO, dm), lambda i: (jnp.maximum(i * halo_blocks - 1, 0), 0)),
                  pl.BlockSpec((tile, dm), lambda i: (i, 1)),
                  pl.BlockSpec(wg.shape, lambda i: (0, 0, 0)),
                  pl.BlockSpec((1, dm), lambda i: (0, 0))],
        out_specs=(row_spec, row_spec, row_spec),
        compiler_params=_params("parallel"),
    )(h, h, h, wg, scale)


def _pool_mid_backward(dy, mx, h, p, wg, scale, after):
    seq, dm = dy.shape
    gd = dm // len(POOL_WINDOWS)
    tile = _tile(seq, 256)
    n_i = seq // tile

    def body(dy_ref, mx_ref, z_ref, p_ref, wg_ref, sc_ref, after_ref, dh_ref, dwg_ref, st_ref, dwg_acc, carry):
        del after_ref
        i = pl.program_id(0)
        ti = n_i - 1 - i

        @pl.when(i == 0)
        def _():
            dwg_acc[...] = jnp.zeros_like(dwg_acc)
            carry[...] = jnp.zeros_like(carry)
            st_ref[...] = jnp.zeros_like(st_ref)

        row = ti * tile + lax.broadcasted_iota(jnp.int32, (tile, 1), 0)
        count = (row + 1).astype(F32)
        for g, w in enumerate(POOL_WINDOWS):
            cs = slice(g * gd, (g + 1) * gd)
            z = z_ref[:, cs]
            sz, dsz = _silu_parts(z)
            dyg = dy_ref[:, cs]
            mxg = mx_ref[:, cs]
            sc = sc_ref[:, cs]
            t1 = dyg * sz
            st_ref[0:1, cs] += _col_sum(t1 * mxg)
            dh_ref[1, :, cs] = (dyg * (mxg * sc) * dsz).astype(BF16)
            dmx = (t1 * sc).astype(BF16)
            dwg_acc[g] += lax.dot_general(p_ref[:, cs], dmx, TN, preferred_element_type=F32)
            dp = lax.dot_general(dmx, wg_ref[g], NT, preferred_element_type=F32)
            e = dp * (1.0 / jnp.minimum(count, float(w)))
            s = jnp.concatenate([e, carry[:, cs]], axis=0)
            n = tile + POOL_HALO
            sh = 1
            while sh < w:
                s = s + pltpu.roll(s, n - sh, 0)
                sh *= 2
            dh_ref[0, :, cs] = (s[:tile, :] - dp).astype(BF16)
            carry[:, cs] = e[:POOL_HALO, :]

        @pl.when(i == n_i - 1)
        def _():
            dwg_ref[...] = dwg_acc[...].astype(BF16)

    row_spec = pl.BlockSpec((tile, dm), lambda i: (n_i - 1 - i, 0))
    return pl.pallas_call(
        body, name="pool_mid_bwd", grid=(n_i,),
        out_shape=(jax.ShapeDtypeStruct((2, seq, dm), BF16), jax.ShapeDtypeStruct(wg.shape, BF16),
                   jax.ShapeDtypeStruct((STAT_ROWS, dm), F32)),
        in_specs=[row_spec, row_spec,
                  pl.BlockSpec((tile, dm), lambda i: (n_i - 1 - i, 1)),
                  row_spec,
                  pl.BlockSpec(wg.shape, lambda i: (0, 0, 0)),
                  pl.BlockSpec((1, dm), lambda i: (0, 0)), ANY],
        out_specs=(pl.BlockSpec((2, tile, dm), lambda i: (0, n_i - 1 - i, 0)),
                   pl.BlockSpec(wg.shape, lambda i: (0, 0, 0)),
                   pl.BlockSpec((STAT_ROWS, dm), lambda i: (0, 0))),
        scratch_shapes=[pltpu.VMEM(wg.shape, F32), pltpu.VMEM((POOL_HALO, dm), F32)],
        compiler_params=_params("arbitrary"),
    )(dy, mx, h, p, wg, scale, after)


def _out_proj_norm(y, w, x, gain, bias):
    seq, dm = x.shape
    tile = _tile(seq, 512)

    def body(y_ref, w_ref, x_ref, g_ref, b_ref, xhat_ref, rstd_ref, xb_ref):
        o = jnp.dot(y_ref[...], w_ref[...], preferred_element_type=F32)
        xhat, rstd = _layer_norm_stats(ALPHA * x_ref[...] + o)
        xhat_ref[...] = xhat
        rstd_ref[...] = rstd
        xb_ref[...] = (xhat * g_ref[...] + b_ref[...]).astype(BF16)

    row_spec = pl.BlockSpec((tile, dm), lambda i: (i, 0))
    vec_spec = pl.BlockSpec((1, dm), lambda i: (0, 0))
    return pl.pallas_call(
        body, name="out_proj_norm_a", grid=(seq // tile,),
        out_shape=(jax.ShapeDtypeStruct((seq, dm), F32), jax.ShapeDtypeStruct((seq, 1), F32),
                   jax.ShapeDtypeStruct((seq, dm), BF16)),
        in_specs=[row_spec, pl.BlockSpec(w.shape, lambda i: (0, 0), pipeline_mode=pl.Buffered(1)), row_spec, vec_spec,
                  vec_spec],
        out_specs=(row_spec, pl.BlockSpec((tile, 1), lambda i: (i, 0)), row_spec),
        compiler_params=_params("parallel"),
    )(y, w, x, gain, bias)


def _kv_proj(xb, wkv, tables):
    seq, dm = xb.shape
    kvw = wkv.shape[1] // 2
    n_kv = kvw // HEAD_DIM
    tile = _tile(seq, 512)

    def body(x_ref, w_ref, cos_ref, sa_ref, sb_ref, kd_ref, vd_ref, kt_ref, vt_ref):
        kv = jnp.dot(x_ref[...], w_ref[...], preferred_element_type=F32)
        low = lax.broadcasted_iota(jnp.int32, (1, LANES), 1) < HEAD_DIM
        cos, sa, sb = cos_ref[...], sa_ref[...], sb_ref[...]

        def put(pair, h, nat_ref, t_ref):
            swapped = pltpu.roll(pair, HEAD_DIM, 1)
            for head, dup in ((h, jnp.where(low, pair, swapped)), (h + 1, jnp.where(low, swapped, pair))):
                nat_ref[head] = dup.astype(BF16)
                t_ref[head] = dup.T.astype(BF16)

        for j in range(kvw // LANES):
            put(_rope(kv[:, j * LANES:(j + 1) * LANES], cos, sa, sb), 2 * j, kd_ref, kt_ref)
            put(kv[:, kvw + j * LANES:kvw + (j + 1) * LANES], 2 * j, vd_ref, vt_ref)

    tab_spec = pl.BlockSpec((tile, LANES), lambda i: (i, 0))
    dup_spec = pl.BlockSpec((n_kv, tile, LANES), lambda i: (0, i, 0))
    dup_shape = jax.ShapeDtypeStruct((n_kv, seq, LANES), BF16)
    t_spec = pl.BlockSpec((n_kv, LANES, tile), lambda i: (0, 0, i))
    t_shape = jax.ShapeDtypeStruct((n_kv, LANES, seq), BF16)
    return pl.pallas_call(
        body, name="kv_proj", grid=(seq // tile,),
        out_shape=(dup_shape, dup_shape, t_shape, t_shape),
        in_specs=[pl.BlockSpec((tile, dm), lambda i: (i, 0)), pl.BlockSpec(wkv.shape, lambda i: (0, 0)),
                  tab_spec, tab_spec, tab_spec],
        out_specs=(dup_spec, dup_spec, t_spec, t_spec),
        compiler_params=_params("parallel"),
    )(xb, wkv, *tables)


def _head_queries(q_ref, low):
    parts = []
    for j in range(GQA_GROUP // 2):
        q2 = q_ref[:, j * LANES:(j + 1) * LANES]
        parts += [jnp.where(low, q2, 0), jnp.where(low, 0, q2)]
    return parts


def _probs_transposed(n, kh, kcat, q_all, sink_ref):
    st = lax.dot_general(kcat, q_all, NT, preferred_element_type=F32)
    key = lax.broadcasted_iota(jnp.int32, (2 * ATTN_BLOCK, ATTN_BLOCK), 0)
    qry = lax.broadcasted_iota(jnp.int32, (2 * ATTN_BLOCK, ATTN_BLOCK), 1)
    valid = (key > qry) & (key <= qry + ATTN_BLOCK) & ((key >= ATTN_BLOCK) | (n > 0))
    st = st + jnp.tile(jnp.where(valid, 0.0, NEG_INF), (1, GQA_GROUP))
    sink = jnp.concatenate([jnp.full((1, ATTN_BLOCK), sink_ref[0, kh * GQA_GROUP + h], F32)
                            for h in range(GQA_GROUP)], axis=1)
    m = jnp.maximum(jnp.max(st, axis=0, keepdims=True), sink)
    e = jnp.exp(st - m)
    e_sink = jnp.exp(sink - m)
    inv = 1.0 / (jnp.sum(e, axis=0, keepdims=True) + e_sink)
    return e * inv, e_sink * inv


def _attn_specs(n_width):
    q_spec = pl.BlockSpec((ATTN_BLOCK, n_width), lambda kh, n: (n, kh))
    cur = pl.BlockSpec((None, ATTN_BLOCK, LANES), lambda kh, n: (kh, n, 0))
    prev = pl.BlockSpec((None, ATTN_BLOCK, LANES), lambda kh, n: (kh, jnp.maximum(n - 1, 0), 0))
    cur_t = pl.BlockSpec((None, LANES, ATTN_BLOCK), lambda kh, n: (kh, 0, n))
    prev_t = pl.BlockSpec((None, LANES, ATTN_BLOCK), lambda kh, n: (kh, 0, jnp.maximum(n - 1, 0)))
    return q_spec, cur, prev, cur_t, prev_t


def _pair_product_transposed(mat_t, rhs, j, low_rows):
    a = rhs[:, 2 * j * ATTN_BLOCK:(2 * j + 1) * ATTN_BLOCK]
    b = rhs[:, (2 * j + 1) * ATTN_BLOCK:(2 * j + 2) * ATTN_BLOCK]
    out_t = (jnp.dot(jnp.where(low_rows, mat_t, 0), a, preferred_element_type=F32)
             + jnp.dot(jnp.where(low_rows, 0, mat_t), b, preferred_element_type=F32))
    return out_t.T


def _attn_forward(qs, kd, vt, zb, sinks):
    seq, dm = qs.shape
    n_kv = kd.shape[0]
    gw = GQA_GROUP * HEAD_DIM

    def body(q_ref, kp_ref, kc_ref, vtp_ref, vtc_ref, z_ref, sink_ref, att_ref, yb_ref):
        kh, n = pl.program_id(0), pl.program_id(1)
        low = lax.broadcasted_iota(jnp.int32, (1, LANES), 1) < HEAD_DIM
        low_rows = lax.broadcasted_iota(jnp.int32, (LANES, 1), 0) < HEAD_DIM
        kcat = jnp.concatenate([kp_ref[...], kc_ref[...]], axis=0)
        vt = jnp.concatenate([vtp_ref[...], vtc_ref[...]], axis=1)
        q_all = jnp.concatenate(_head_queries(q_ref, low), axis=0)
        probs_t, _ = _probs_transposed(n, kh, kcat, q_all, sink_ref)
        pt = probs_t.astype(BF16)
        for j in range(GQA_GROUP // 2):
            cs = slice(j * LANES, (j + 1) * LANES)
            o2 = _pair_product_transposed(vt, pt, j, low_rows)
            att_ref[:, cs] = o2
            z = z_ref[:, cs]
            yb_ref[:, cs] = (o2 * (z * jax.nn.sigmoid(z))).astype(BF16)

    q_spec, cur, prev, cur_t, prev_t = _attn_specs(gw)
    return pl.pallas_call(
        body, name="attn_fwd", grid=(n_kv, seq // ATTN_BLOCK),
        out_shape=(jax.ShapeDtypeStruct((seq, dm), F32), jax.ShapeDtypeStruct((seq, dm), BF16)),
        in_specs=[q_spec, prev, cur, prev_t, cur_t, q_spec, pl.BlockSpec(memory_space=pltpu.SMEM)],
        out_specs=(q_spec, q_spec),
        compiler_params=_params("parallel", "parallel"),
    )(qs, kd, kd, vt, vt, zb, sinks)


def _attn_backward(qs, kd, vd, kt, zb, att, dyb, sinks, tables, after):
    seq, dm = qs.shape
    n_kv = kd.shape[0]
    gw = GQA_GROUP * HEAD_DIM
    n_blocks = seq // ATTN_BLOCK

    def body(q_ref, kp_ref, kc_ref, vp_ref, vc_ref, ktp_ref, ktc_ref, z_ref, att_ref, dyb_ref, sink_ref,
             cos_ref, sa_ref, sb_ref, after_ref, dh_ref, dk_ref, dv_ref, ds_ref):
        del after_ref
        kh, n = pl.program_id(0), pl.program_id(1)

        @pl.when(n == 0)
        def _():
            dk_ref[...] = jnp.zeros_like(dk_ref)
            dv_ref[...] = jnp.zeros_like(dv_ref)

        @pl.when(jnp.logical_and(n == 0, kh == 0))
        def _():
            ds_ref[...] = jnp.zeros_like(ds_ref)

        low = lax.broadcasted_iota(jnp.int32, (1, LANES), 1) < HEAD_DIM
        low_rows = lax.broadcasted_iota(jnp.int32, (LANES, 1), 0) < HEAD_DIM
        head_lane = lax.broadcasted_iota(jnp.int32, (1, LANES), 1)
        kcat = jnp.concatenate([kp_ref[...], kc_ref[...]], axis=0)
        vcat = jnp.concatenate([vp_ref[...], vc_ref[...]], axis=0)
        kt = jnp.concatenate([ktp_ref[...], ktc_ref[...]], axis=1)
        cos, sa, sb = cos_ref[...], sa_ref[...], sb_ref[...]
        q_parts = _head_queries(q_ref, low)
        q_all = jnp.concatenate(q_parts, axis=0)
        d_parts = []
        for j in range(GQA_GROUP // 2):
            cs = slice(j * LANES, (j + 1) * LANES)
            sz, dsz = _silu_parts(z_ref[:, cs])
            dy2 = dyb_ref[:, cs]
            dh_ref[1, :, cs] = (dy2 * att_ref[:, cs] * dsz).astype(BF16)
            datt = (dy2 * sz).astype(BF16)
            d_parts += [jnp.where(low, datt, 0), jnp.where(low, 0, datt)]
        d_all = jnp.concatenate(d_parts, axis=0)
        probs_t, sink_p = _probs_transposed(n, kh, kcat, q_all, sink_ref)
        dprobs_t = lax.dot_general(vcat, d_all, NT, preferred_element_type=F32)
        row_dot = jnp.sum(probs_t * dprobs_t, axis=0, keepdims=True)
        ds_t = (probs_t * (dprobs_t - row_dot)).astype(BF16)
        dk = jnp.dot(ds_t, q_all, preferred_element_type=F32)
        dv = jnp.dot(probs_t.astype(BF16), d_all, preferred_element_type=F32)
        for j in range(GQA_GROUP // 2):
            dq2 = _pair_product_transposed(kt, ds_t, j, low_rows)
            dh_ref[0, :, j * LANES:(j + 1) * LANES] = (_rope_transposed(dq2, cos, sa, sb) * 0.125).astype(BF16)
        sink_dot = sink_p * row_dot
        dsink = jnp.zeros((1, LANES), F32)
        for h in range(GQA_GROUP):
            part = jnp.sum(sink_dot[:, h * ATTN_BLOCK:(h + 1) * ATTN_BLOCK], axis=1, keepdims=True)
            dsink = dsink - jnp.where(head_lane == kh * GQA_GROUP + h, part, 0.0)
        ds_ref[0:1, :] += dsink

        @pl.when(n == 0)
        def _():
            dk_ref[pl.ds(0, ATTN_BLOCK), :] += dk[ATTN_BLOCK:, :]
            dv_ref[pl.ds(0, ATTN_BLOCK), :] += dv[ATTN_BLOCK:, :]

        @pl.when(n > 0)
        def _():
            start = pl.multiple_of((n - 1) * ATTN_BLOCK, ATTN_BLOCK)
            dk_ref[pl.ds(start, 2 * ATTN_BLOCK), :] += dk
            dv_ref[pl.ds(start, 2 * ATTN_BLOCK), :] += dv

    q_spec, cur, prev, cur_t, prev_t = _attn_specs(gw)
    tab_spec = pl.BlockSpec((ATTN_BLOCK, LANES), lambda kh, n: (n, 0))
    acc_spec = pl.BlockSpec((None, seq, LANES), lambda kh, n: (kh, 0, 0))
    acc_shape = jax.ShapeDtypeStruct((n_kv, seq, LANES), F32)
    return pl.pallas_call(
        body, name="attn_bwd", grid=(n_kv, n_blocks),
        out_shape=(jax.ShapeDtypeStruct((2, seq, dm), BF16), acc_shape, acc_shape,
                   jax.ShapeDtypeStruct((STAT_ROWS, LANES), F32)),
        in_specs=[q_spec, prev, cur, prev, cur, prev_t, cur_t, q_spec, q_spec, q_spec,
                  pl.BlockSpec(memory_space=pltpu.SMEM), tab_spec, tab_spec, tab_spec, ANY],
        out_specs=(pl.BlockSpec((2, ATTN_BLOCK, gw), lambda kh, n: (0, n, kh)), acc_spec, acc_spec,
                   pl.BlockSpec((STAT_ROWS, LANES), lambda kh, n: (0, 0))),
        compiler_params=_params("arbitrary", "arbitrary"),
    )(qs, kd, kd, vd, vd, kt, kt, zb, att, dyb, sinks, *tables, after)


def _kv_grad_fold(dk, dv, tables):
    n_kv, seq, _ = dk.shape
    kvw = n_kv * HEAD_DIM
    tile = _tile(seq, 512)

    def body(dk_ref, dv_ref, cos_ref, sa_ref, sb_ref, o_ref):
        low = lax.broadcasted_iota(jnp.int32, (1, LANES), 1) < HEAD_DIM
        cos, sa, sb = cos_ref[...], sa_ref[...], sb_ref[...]

        def folded(ref, h):
            t = ref[h]
            return t + pltpu.roll(t, HEAD_DIM, 1)

        for j in range(n_kv // 2):
            ka = _rope_transposed(folded(dk_ref, 2 * j), cos, sa, sb)
            kb = _rope_transposed(folded(dk_ref, 2 * j + 1), cos, sa, sb)
            o_ref[:, j * LANES:(j + 1) * LANES] = jnp.where(low, ka, kb).astype(BF16)
            o_ref[:, kvw + j * LANES:kvw + (j + 1) * LANES] = jnp.where(
                low, folded(dv_ref, 2 * j), folded(dv_ref, 2 * j + 1)).astype(BF16)

    tab_spec = pl.BlockSpec((tile, LANES), lambda i: (i, 0))
    in_spec = pl.BlockSpec((n_kv, tile, LANES), lambda i: (0, i, 0))
    return pl.pallas_call(
        body, name="kv_grad_fold", grid=(seq // tile,),
        out_shape=jax.ShapeDtypeStruct((seq, 2 * kvw), BF16),
        in_specs=[in_spec, in_spec, tab_spec, tab_spec, tab_spec],
        out_specs=pl.BlockSpec((tile, 2 * kvw), lambda i: (i, 0)),
        compiler_params=_params("parallel"),
    )(dk, dv, *tables)


def _out_proj_norm_loss(yb, w, xhat1, gain0, bias0, gain1, bias1, target):
    seq, dm = xhat1.shape
    tile = _tile(seq, 512)

    def body(y_ref, w_ref, xh1_ref, g0_ref, b0_ref, g1_ref, b1_ref, t_ref, dr_ref, drb_ref, st_ref):
        i = pl.program_id(0)

        @pl.when(i == 0)
        def _():
            st_ref[...] = jnp.zeros_like(st_ref)

        ob = jnp.dot(y_ref[...], w_ref[...], preferred_element_type=F32)
        x1 = xh1_ref[...] * g0_ref[...] + b0_ref[...]
        xhat, rstd = _layer_norm_stats(ALPHA * x1 + ob)
        err = xhat * g1_ref[...] + b1_ref[...] - t_ref[...]
        dout = err * (1.0 / dm)
        dr = _layer_norm_backward(dout, xhat, rstd, g1_ref[...])
        dr_ref[...] = dr
        drb_ref[...] = dr.astype(BF16)
        st_ref[0:1, :] += _col_sum(dout * xhat)
        st_ref[1:2, :] += _col_sum(dout)
        st_ref[2:3, :] += _col_sum(err * err)

    row_spec = pl.BlockSpec((tile, dm), lambda i: (i, 0))
    vec_spec = pl.BlockSpec((1, dm), lambda i: (0, 0))
    return pl.pallas_call(
        body, name="out_proj_norm_loss_b", grid=(seq // tile,),
        out_shape=(jax.ShapeDtypeStruct((seq, dm), F32), jax.ShapeDtypeStruct((seq, dm), BF16),
                   jax.ShapeDtypeStruct((STAT_ROWS, dm), F32)),
        in_specs=[row_spec, pl.BlockSpec(w.shape, lambda i: (0, 0), pipeline_mode=pl.Buffered(1)), row_spec, vec_spec,
                  vec_spec, vec_spec,
                  vec_spec, row_spec],
        out_specs=(row_spec, row_spec, pl.BlockSpec((STAT_ROWS, dm), lambda i: (0, 0))),
        compiler_params=_params("arbitrary"),
    )(yb, w, xhat1, gain0, bias0, gain1, bias1, target)


def _stream_grad_norm_backward(dhq, wqg, dkv, wkv, dr2, xhat1, rstd1, gain0):
    seq, dm = dr2.shape
    tile = _tile(seq, 256)

    def body(dh_ref, wqg_ref, dkv_ref, wkv_ref, dr2_ref, xh_ref, rstd_ref, g_ref, dr_ref, drb_ref, st_ref):
        @pl.when(pl.program_id(0) == 0)
        def _():
            st_ref[...] = jnp.zeros_like(st_ref)

        dx1 = (lax.dot_general(dh_ref[0], wqg_ref[:, :dm], NT, preferred_element_type=F32)
               + lax.dot_general(dh_ref[1], wqg_ref[:, dm:], NT, preferred_element_type=F32)
               + lax.dot_general(dkv_ref[...], wkv_ref[...], NT, preferred_element_type=F32)
               + ALPHA * dr2_ref[...])
        xhat = xh_ref[...]
        dr = _layer_norm_backward(dx1, xhat, rstd_ref[...], g_ref[...])
        dr_ref[...] = dr
        drb_ref[...] = dr.astype(BF16)
        st_ref[0:1, :] += _col_sum(dx1 * xhat)
        st_ref[1:2, :] += _col_sum(dx1)

    row_spec = pl.BlockSpec((tile, dm), lambda i: (i, 0))
    resident = pl.Buffered(1)
    return pl.pallas_call(
        body, name="stream_grad_norm_bwd", grid=(seq // tile,),
        out_shape=(jax.ShapeDtypeStruct((seq, dm), F32), jax.ShapeDtypeStruct((seq, dm), BF16),
                   jax.ShapeDtypeStruct((STAT_ROWS, dm), F32)),
        in_specs=[pl.BlockSpec((2, tile, dm), lambda i: (0, i, 0)),
                  pl.BlockSpec(wqg.shape, lambda i: (0, 0), pipeline_mode=resident),
                  pl.BlockSpec((tile, dkv.shape[1]), lambda i: (i, 0)),
                  pl.BlockSpec(wkv.shape, lambda i: (0, 0), pipeline_mode=resident),
                  row_spec, row_spec, pl.BlockSpec((tile, 1), lambda i: (i, 0)),
                  pl.BlockSpec((1, dm), lambda i: (0, 0))],
        out_specs=(row_spec, row_spec, pl.BlockSpec((STAT_ROWS, dm), lambda i: (0, 0))),
        compiler_params=_params("arbitrary"),
    )(dhq, wqg, dkv, wkv, dr2, xhat1, rstd1, gain0)


def _adamw_math(w, g, m, v):
    m = ADAM_B1 * m + (1.0 - ADAM_B1) * g
    v = ADAM_B2 * v + (1.0 - ADAM_B2) * (g * g)
    m_hat = m / (1.0 - ADAM_B1 ** ADAM_STEP)
    v_hat = v / (1.0 - ADAM_B2 ** ADAM_STEP)
    delta = -ADAM_LR * (m_hat / (jnp.sqrt(v_hat) + ADAM_EPS) + ADAM_WD * w)
    return delta, m, v


def _sum_devices(ref):
    total = ref[0].astype(F32)
    for d in range(1, N_DEV):
        total = total + ref[d].astype(F32)
    return total


def _adamw_shard(name, parts, w, m, v, after):
    rows, cols = w.shape
    tr = _tile(rows, max(8, (1 << 18) // cols)) if rows >= 8 else rows

    def body(p_ref, w_ref, m_ref, v_ref, after_ref, g_out, d_out, m_out, v_out):
        del after_ref
        g = _sum_devices(p_ref)
        delta, m_new, v_new = _adamw_math(w_ref[...], g, m_ref[...], v_ref[...])
        g_out[...] = g
        d_out[...] = delta
        m_out[...] = m_new
        v_out[...] = v_new

    spec = pl.BlockSpec((tr, cols), lambda i: (i, 0))
    shape = jax.ShapeDtypeStruct((rows, cols), F32)
    return pl.pallas_call(
        body, name=name, grid=(rows // tr,),
        out_shape=(shape, shape, shape, shape),
        in_specs=[pl.BlockSpec((N_DEV, tr, cols), lambda i: (0, i, 0)), spec, spec, spec, ANY],
        out_specs=(spec, spec, spec, spec),
        compiler_params=_params("parallel"),
    )(parts, w, m, v, after)


def _adamw_replicated(stats_b, stats_a, sink_parts, ln_g, ln_b, sinks, m_ln_g, m_ln_b, m_sinks, v_ln_g, v_ln_b,
                      v_sinks, after):
    n_q = sinks.shape[1]
    dm = ln_g.shape[1]

    def body(sb_ref, sa_ref, sk_ref, g_ref, b_ref, s_ref, mg_ref, mb_ref, ms_ref, vg_ref, vb_ref, vs_ref, after_ref,
             *outs):
        del after_ref
        layer_sums = (_sum_devices(sa_ref), _sum_devices(sb_ref))
        outs[12][...] = jnp.sum(layer_sums[1][2:3, :], axis=1, keepdims=True) * (0.5 / dm)
        for which, (w_ref, m_ref, v_ref) in enumerate(((g_ref, mg_ref, vg_ref), (b_ref, mb_ref, vb_ref))):
            for layer in range(DEPTH):
                row = slice(layer, layer + 1)
                g = layer_sums[layer][which:which + 1, :]
                res = (g,) + _adamw_math(w_ref[row, :], g, m_ref[row, :], v_ref[row, :])
                for o_ref, val in zip(outs[4 * which:4 * which + 4], res):
                    o_ref[row, :] = val
        g = _sum_devices(sk_ref)[0:1, 0:n_q]
        res = (g,) + _adamw_math(s_ref[...], g, ms_ref[...], vs_ref[...])
        for o_ref, val in zip(outs[8:12], res):
            o_ref[...] = val

    vmem = pl.BlockSpec(memory_space=pltpu.VMEM)
    shapes = [jax.ShapeDtypeStruct(a.shape, F32) for a in (ln_g, ln_b, sinks) for _ in range(4)]
    shapes.append(jax.ShapeDtypeStruct((1, 1), F32))
    return pl.pallas_call(
        body, name="adamw_replicated", out_shape=tuple(shapes),
        in_specs=[vmem] * 12 + [ANY], out_specs=tuple([vmem] * 13),
    )(stats_b, stats_a, sink_parts, ln_g, ln_b, sinks, m_ln_g, m_ln_b, m_sinks, v_ln_g, v_ln_b, v_sinks, after)


def kernel(x, ln_g, ln_b, a_w_in, a_w_group, a_scale, a_w_out, b_w_k, b_w_v, b_w_qg, b_sinks, b_w_out, loss_target, m_ln_g, m_ln_b, m_a_w_in, m_a_w_group, m_a_scale, m_a_w_out, m_b_w_k, m_b_w_v, m_b_w_qg, m_b_sinks, m_b_w_out, v_ln_g, v_ln_b, v_a_w_in, v_a_w_group, v_a_scale, v_a_w_out, v_b_w_k, v_b_w_v, v_b_w_qg, v_b_sinks, v_b_w_out):
    _, seq, dm = x.shape
    n_groups = len(POOL_WINDOWS)
    gd = dm // n_groups
    kvw = b_w_k.shape[1]
    cb = 2 * dm // N_DEV
    rb = dm // N_DEV
    gb = gd // N_DEV

    x2 = x.reshape(seq, dm)
    target = loss_target.reshape(seq, dm)
    w_in_s = a_w_in.reshape(dm, cb)
    w_g_s = a_w_group.reshape(n_groups, gb, gd)
    w_out_s = a_w_out.reshape(rb, dm)
    w_qg_s = b_w_qg.reshape(dm, cb)
    w_outb_s = b_w_out.reshape(rb, dm)

    def cols(ref, dev):
        return ref.at[:, pl.ds(pl.multiple_of(dev * cb, LANES), cb)]

    def rows(ref, dev):
        return ref.at[pl.ds(pl.multiple_of(dev * rb, 8), rb), :]

    def group_rows(ref, dev):
        return ref.at[:, pl.ds(pl.multiple_of(dev * gb, 8), gb), :]

    def k_rows(ref, dev):
        return ref.at[pl.ds(pl.multiple_of(dev * rb, 8), rb), pl.ds(0, kvw)]

    def v_rows(ref, dev):
        return ref.at[pl.ds(pl.multiple_of(dev * rb, 8), rb), pl.ds(kvw, kvw)]

    def scale_cols(ref, dev):
        return ref.at[:, pl.ds(pl.multiple_of(dev * rb, LANES), rb)]

    bf = lambda a: a.astype(BF16)
    wide, square = jax.ShapeDtypeStruct((dm, 2 * dm), BF16), jax.ShapeDtypeStruct((dm, dm), BF16)
    (w_in,) = _gather_weights("gather_a_in", 0, [(bf(w_in_s), 0, cols)], [wide])
    w_g, scale, w_out = _gather_weights(
        "gather_a_rest", 1, [(bf(w_g_s), 0, group_rows), (a_scale, 1, scale_cols), (bf(w_out_s), 2, rows)],
        [jax.ShapeDtypeStruct((n_groups, gd, gd), BF16), jax.ShapeDtypeStruct((1, dm), F32), square])
    w_kv, w_qg = _gather_weights(
        "gather_b_in", 2, [(bf(b_w_k), 0, k_rows), (bf(b_w_v), 0, v_rows), (bf(w_qg_s), 1, cols)],
        [jax.ShapeDtypeStruct((dm, 2 * kvw), BF16), wide])
    (w_outb,) = _gather_weights("gather_b_out", 3, [(bf(w_outb_s), 0, rows)], [square])

    tables = _rope_tables(seq)
    bm = _tile(seq, 1024)
    bn = _tile(dm, 1024)
    g0, g1, b0, b1 = ln_g[0:1], ln_g[1:2], ln_b[0:1], ln_b[1:2]

    xb = _cast_bf16("cast_x", x2)
    h = _mm("a_in_proj", xb, w_in, dims=NN, grid=(seq // bm, 2 * dm // bn),
            a_spec=pl.BlockSpec((bm, dm), lambda i, j: (i, 0)), b_spec=pl.BlockSpec((dm, bn), lambda i, j: (0, j)),
            out_shape=jax.ShapeDtypeStruct((seq, 2 * dm), F32), out_spec=pl.BlockSpec((bm, bn), lambda i, j: (i, j)))
    y, pooled, mixed = _pool_mid_forward(h, w_g, scale)
    xhat1, rstd1, x1b = _out_proj_norm(y, w_out, x2, g0, b0)

    kd, vd, kt, vt = _kv_proj(x1b, w_kv, tables)
    bmq = _tile(seq, 512)
    tab_spec = pl.BlockSpec((bmq, LANES), lambda i, j: (i, 0))

    def rope_scale(val, cos_ref, sa_ref, sb_ref):
        cos, sa, sb = cos_ref[...], sa_ref[...], sb_ref[...]
        return jnp.concatenate([_rope(val[:, j * LANES:(j + 1) * LANES], cos, sa, sb) * 0.125
                                for j in range(val.shape[1] // LANES)], axis=1)

    qs = _mm("b_q_proj", x1b, w_qg, dims=NN, grid=(seq // bmq, dm // bn),
             a_spec=pl.BlockSpec((bmq, dm), lambda i, j: (i, 0)), b_spec=pl.BlockSpec((dm, bn), lambda i, j: (0, j)),
             out_shape=jax.ShapeDtypeStruct((seq, dm), BF16), out_spec=pl.BlockSpec((bmq, bn), lambda i, j: (i, j)),
             epilogue=rope_scale, extras=tables, extra_specs=(tab_spec,) * 3)
    zb = _mm("b_gate_proj", x1b, w_qg, dims=NN, grid=(seq // bm, dm // bn),
             a_spec=pl.BlockSpec((bm, dm), lambda i, j: (i, 0)),
             b_spec=pl.BlockSpec((dm, bn), lambda i, j: (0, j + dm // bn)),
             out_shape=jax.ShapeDtypeStruct((seq, dm), F32), out_spec=pl.BlockSpec((bm, bn), lambda i, j: (i, j)))
    att, yb = _attn_forward(qs, kd, vt, zb, b_sinks)
    dr2, dr2b, stats_b = _out_proj_norm_loss(yb, w_outb, xhat1, g0, b0, g1, b1, target)

    def weight_grad(name, a, b, n_cols, b_spec=None):
        m_cols = a.shape[1]
        tm, tn = _tile(m_cols, 1024), _tile(n_cols, 512)
        return _mm(name, a, b, dims=TN, grid=(m_cols // tm, n_cols // tn),
                   a_spec=pl.BlockSpec((seq, tm), lambda i, j: (0, i)),
                   b_spec=b_spec(tn) if b_spec else pl.BlockSpec((seq, tn), lambda i, j: (0, j)),
                   out_shape=jax.ShapeDtypeStruct((m_cols, n_cols), BF16),
                   out_spec=pl.BlockSpec((tm, tn), lambda i, j: (i, j)))

    def halves_spec(tn):
        per = dm // tn
        return pl.BlockSpec((None, seq, tn), lambda i, j: (j // per, 0, j % per))

    def times_transposed(name, a, w):
        return _mm(name, a, w, dims=NT, grid=(seq // bm, dm // bn),
                   a_spec=pl.BlockSpec((bm, a.shape[1]), lambda i, j: (i, 0)),
                   b_spec=pl.BlockSpec((bn, w.shape[1]), lambda i, j: (j, 0)),
                   out_shape=jax.ShapeDtypeStruct((seq, dm), F32), out_spec=pl.BlockSpec((bm, bn), lambda i, j: (i, j)))

    def stat_row_cols(ref, dev):
        return ref.at[pl.ds(0, 1), pl.ds(pl.multiple_of(dev * rb, LANES), rb)]

    dyb = times_transposed("b_out_proj_dx", dr2b, w_outb)
    d_w_outb = weight_grad("b_out_proj_dw", yb, dr2b, dm)
    (p_outb,) = _exchange_blocks("scatter_b_out", [(d_w_outb, rows, (rb, dm))], 4)
    dhq, dkd, dvd, dsink = _attn_backward(qs, kd, vd, kt, zb, att, dyb, b_sinks, tables, after=d_w_outb)
    dkv = _kv_grad_fold(dkd, dvd, tables)
    d_w_qg = weight_grad("b_qg_proj_dw", x1b, dhq, 2 * dm, halves_spec)
    d_w_kv = weight_grad("b_kv_proj_dw", x1b, dkv, 2 * kvw)
    p_qg, p_k, p_v = _exchange_blocks("scatter_b_in", [
        (d_w_qg, cols, (dm, cb)), (d_w_kv, k_rows, (rb, kvw)), (d_w_kv, v_rows, (rb, kvw))], 5)
    dr1, dr1b, stats_a = _stream_grad_norm_backward(dhq, w_qg, dkv, w_kv, dr2, xhat1, rstd1, g0)
    all_b, all_a, all_sink = _exchange_blocks("gather_replicated_grads", [
        (stats_b, None, stats_b.shape), (stats_a, None, stats_a.shape), (dsink, None, dsink.shape)], 9)

    dy = times_transposed("a_out_proj_dx", dr1b, w_out)
    d_w_out = weight_grad("a_out_proj_dw", y, dr1b, dm)
    (p_out,) = _exchange_blocks("scatter_a_out", [(d_w_out, rows, (rb, dm))], 6)
    dh, d_w_g, stats_s = _pool_mid_backward(dy, mixed, h, pooled, w_g, scale, after=d_w_out)
    p_g, p_scale = _exchange_blocks("scatter_a_mid", [
        (d_w_g, group_rows, (n_groups, gb, gd)), (stats_s, stat_row_cols, (1, rb))], 7)
    d_w_in = weight_grad("a_in_proj_dw", xb, dh, 2 * dm, halves_spec)
    (p_in,) = _exchange_blocks("scatter_a_in", [(d_w_in, cols, (dm, cb))], 8)
    grad_x = _mm("a_in_proj_dx", dh, w_in, dims=NT, grid=(seq // bm, dm // bn, 2), nk=2,
                 a_spec=pl.BlockSpec((None, bm, dm), lambda i, j, k: (k, i, 0)),
                 b_spec=pl.BlockSpec((bn, dm), lambda i, j, k: (j, k)),
                 out_shape=jax.ShapeDtypeStruct((seq, dm), F32), out_spec=pl.BlockSpec((bm, bn), lambda i, j, k: (i, j)),
                 add=dr1, add_spec=pl.BlockSpec((bm, bn), lambda i, j, k: (i, j)), add_scale=ALPHA,
                 extras=(d_w_in,), extra_specs=(ANY,))

    upd = {}
    last = [grad_x]

    def shard_update(key, parts, w, m, v):
        shape = w.shape
        flat = lambda a: a.reshape(-1, shape[-1])
        outs = _adamw_shard("adamw_" + key, parts.reshape(N_DEV, -1, shape[-1]), flat(w), flat(m), flat(v), last[0])
        last[0] = outs[0]
        upd[key] = [o.reshape(shape) for o in outs]

    shard_update("b_w_out", p_outb, b_w_out, m_b_w_out, v_b_w_out)
    shard_update("b_w_qg", p_qg, b_w_qg, m_b_w_qg, v_b_w_qg)
    shard_update("b_w_k", p_k, b_w_k, m_b_w_k, v_b_w_k)
    shard_update("b_w_v", p_v, b_w_v, m_b_w_v, v_b_w_v)
    rep = _adamw_replicated(all_b, all_a, all_sink, ln_g, ln_b, b_sinks, m_ln_g, m_ln_b, m_b_sinks, v_ln_g, v_ln_b,
                            v_b_sinks, last[0])
    last[0] = rep[0]
    upd["ln_g"], upd["ln_b"], upd["b_sinks"] = list(rep[0:4]), list(rep[4:8]), list(rep[8:12])
    shard_update("a_w_out", p_out, a_w_out, m_a_w_out, v_a_w_out)
    shard_update("a_w_group", p_g, a_w_group, m_a_w_group, v_a_w_group)
    shard_update("a_scale", p_scale, a_scale, m_a_scale, v_a_scale)
    shard_update("a_w_in", p_in, a_w_in, m_a_w_in, v_a_w_in)

    loss = rep[12].reshape(())
    order = ["ln_g", "ln_b", "a_w_in", "a_w_group", "a_scale", "a_w_out", "b_w_k", "b_w_v", "b_w_qg", "b_sinks",
             "b_w_out"]
    return (loss, grad_x.reshape(x.shape), *[upd[n][0] for n in order], *[upd[n][1] for n in order],
            *[upd[n][2] for n in order], *[upd[n][3] for n in order])
```

```python
import functools

import jax
import jax.numpy as jnp
from jax import lax
from jax.experimental import pallas as pl
from jax.experimental.pallas import tpu as pltpu
from jax.experimental.pallas import tpu_sc as plsc

F32 = jnp.float32
BF16 = jnp.bfloat16
MESH = pl.DeviceIdType.MESH
AXES = ("x", "y", "c")
N_DEV = 8

POOL_WINDOWS = (2, 4, 8, 16)
POOL_HALO = 16
HEAD_DIM = 64
GQA_GROUP = 8
ATTN_BLOCK = 128
ROPE_THETA = 10000.0
LN_EPS = 1e-5
NEG_INF = -1e30
DEPTH = 2
ALPHA = (2 * DEPTH) ** 0.25
ADAM_LR = 0.001
ADAM_B1 = 0.9
ADAM_B2 = 0.999
ADAM_EPS = 1e-08
ADAM_WD = 0.01
ADAM_STEP = 10

LANES = 128
STAT_ROWS = 8


def _tile(n, want):
    t = min(n, want)
    while n % t:
        t //= 2
    return t


def _params(*sem):
    return pltpu.CompilerParams(dimension_semantics=sem)


ANY = pl.BlockSpec(memory_space=pl.ANY)


def _my_pos():
    return lax.axis_index("x"), lax.axis_index("y"), lax.axis_index("c")


def _dev_index(p):
    return 4 * p[0] + 2 * p[1] + p[2]


def _handshake(peers):
    barrier = pltpu.get_barrier_semaphore()
    for peer in peers:
        pl.semaphore_signal(barrier, inc=1, device_id=peer, device_id_type=MESH)
    pl.semaphore_wait(barrier, len(peers))


def _launch_on_sequencer(name, collective_id, body, operands, out_shapes, scratch):
    return pl.kernel(
        body, out_type=tuple(out_shapes), name=name,
        mesh=plsc.ScalarSubcoreMesh(axis_name="sequencer", num_cores=1), scratch_types=scratch,
        compiler_params=pltpu.CompilerParams(collective_id=collective_id),
    )(*operands)


def _gather_weights(name, collective_id, streams, out_shapes):
    n_s = len(streams)
    n_out = len(out_shapes)

    def body(*refs):
        srcs = refs[:n_s]
        outs = refs[n_s:n_s + n_out]
        send_sems, recv_sems, local_sems = refs[n_s + n_out:]
        x, y, c = _my_pos()
        me, sibling = (x, y, c), (x, y, 1 - c)
        chips = [(1 - x, y), (x, 1 - y), (1 - x, 1 - y)]
        _handshake([sibling] + [(*chip, c) for chip in chips])

        def copy(s, k, block, to, from_shard=False):
            out_ref = outs[streams[s][1]]
            win = streams[s][2](out_ref, _dev_index(block))
            return pltpu.make_async_remote_copy(
                src_ref=srcs[s] if from_shard else win, dst_ref=win,
                send_sem=send_sems.at[7 * s + k], recv_sem=recv_sems.at[7 * s + k],
                device_id=to, device_id_type=MESH)

        mine = [pltpu.make_async_copy(srcs[s], streams[s][2](outs[streams[s][1]], _dev_index(me)), local_sems.at[s])
                for s in range(n_s)]
        for cp in mine:
            cp.start()
        first = []
        for s in range(n_s):
            first.append(copy(s, 0, me, sibling, True))
            first += [copy(s, 1 + j, me, (*chip, c), True) for j, chip in enumerate(chips)]
        for cp in first:
            cp.start()
        passed = []
        for j, chip in enumerate(chips):
            for s in range(n_s):
                copy(s, 1 + j, (*chip, c), me).wait_recv()
                fwd = copy(s, 4 + j, (*chip, c), sibling)
                fwd.start()
                passed.append(fwd)
        for s in range(n_s):
            copy(s, 0, sibling, me).wait_recv()
            for j, chip in enumerate(chips):
                copy(s, 4 + j, (*chip, 1 - c), me).wait_recv()
        for cp in first + passed:
            cp.wait_send()
        for cp in mine:
            cp.wait()

    scratch = [pltpu.SemaphoreType.DMA((7 * n_s,)), pltpu.SemaphoreType.DMA((7 * n_s,)),
               pltpu.SemaphoreType.DMA((n_s,))]
    return _launch_on_sequencer(name, collective_id, body, [s[0] for s in streams], out_shapes, scratch)


def _exchange_blocks(name, streams, collective_id):
    n_s = len(streams)

    def body(*refs):
        srcs = refs[:n_s]
        outs = refs[n_s:2 * n_s]
        send_sems, recv_sems, local_sems = refs[2 * n_s:]
        x, y, c = _my_pos()
        me = _dev_index((x, y, c))
        _handshake([(1 - x if k & 4 else x, 1 - y if k & 2 else y, 1 - c if k & 1 else c) for k in range(1, N_DEV)])

        def window(s, dev):
            return srcs[s] if streams[s][1] is None else streams[s][1](srcs[s], dev)

        mine = [pltpu.make_async_copy(window(s, me), outs[s].at[me], local_sems.at[s]) for s in range(n_s)]
        for cp in mine:
            cp.start()
        copies = []
        for k in (2, 4, 6, 3, 5, 7, 1):
            peer = (1 - x if k & 4 else x, 1 - y if k & 2 else y, 1 - c if k & 1 else c)
            for s in range(n_s):
                copies.append(pltpu.make_async_remote_copy(
                    src_ref=window(s, _dev_index(peer)), dst_ref=outs[s].at[me],
                    send_sem=send_sems.at[7 * s + k - 1], recv_sem=recv_sems.at[7 * s + k - 1],
                    device_id=peer, device_id_type=MESH))
        for cp in copies:
            cp.start()
        for cp in copies:
            cp.wait()
        for cp in mine:
            cp.wait()

    out_shapes = [jax.ShapeDtypeStruct((N_DEV,) + tuple(s[2]), s[0].dtype) for s in streams]
    scratch = [pltpu.SemaphoreType.DMA((7 * n_s,)), pltpu.SemaphoreType.DMA((7 * n_s,)),
               pltpu.SemaphoreType.DMA((n_s,))]
    return _launch_on_sequencer(name, collective_id, body, [s[0] for s in streams], out_shapes, scratch)


NN = (((1,), (0,)), ((), ()))
NT = (((1,), (1,)), ((), ()))
TN = (((0,), (0,)), ((), ()))


def _mm(name, a, b, *, dims, grid, a_spec, b_spec, out_shape, out_spec, nk=1,
        add=None, add_spec=None, add_scale=1.0, epilogue=None, extras=(), extra_specs=()):
    n_extra = len(extras)
    has_add = add is not None

    def body(*refs):
        a_ref, b_ref = refs[:2]
        pos = 2
        add_ref = None
        if has_add:
            add_ref = refs[pos]
            pos += 1
        extra_refs = refs[pos:pos + n_extra]
        o_ref = refs[pos + n_extra]
        acc_ref = refs[pos + n_extra + 1] if nk > 1 else None

        def finish(val):
            if has_add:
                val = val + add_scale * add_ref[...]
            if epilogue is not None:
                val = epilogue(val, *extra_refs)
            o_ref[...] = val.astype(o_ref.dtype)

        part = lax.dot_general(a_ref[...].astype(BF16), b_ref[...].astype(BF16), dims,
                               preferred_element_type=F32)
        if nk == 1:
            finish(part)
        else:
            k = pl.program_id(2)

            @pl.when(k == 0)
            def _():
                acc_ref[...] = part

            @pl.when(jnp.logical_and(k > 0, k < nk - 1))
            def _():
                acc_ref[...] += part

            @pl.when(k == nk - 1)
            def _():
                finish(acc_ref[...] + part)

    in_specs = [a_spec, b_spec] + ([add_spec] if has_add else []) + list(extra_specs)
    operands = [a, b] + ([add] if has_add else []) + list(extras)
    scratch = [pltpu.VMEM(out_spec.block_shape, F32)] if nk > 1 else []
    sem = ("parallel", "parallel") + (("arbitrary",) if nk > 1 else ())
    return pl.pallas_call(
        body, name=name, grid=grid, out_shape=out_shape,
        in_specs=in_specs, out_specs=out_spec, scratch_shapes=scratch,
        compiler_params=_params(*sem),
    )(*operands)


def _cast_bf16(name, a):
    rows, cols = a.shape
    tr = _tile(rows, 512)

    def body(a_ref, o_ref):
        o_ref[...] = a_ref[...].astype(BF16)

    return pl.pallas_call(
        body, name=name, grid=(rows // tr,),
        out_shape=jax.ShapeDtypeStruct(a.shape, BF16),
        in_specs=[pl.BlockSpec((tr, cols), lambda i: (i, 0))],
        out_specs=pl.BlockSpec((tr, cols), lambda i: (i, 0)),
        compiler_params=_params("parallel"),
    )(a)


def _rope_tables(seq):
    inv_freq = ROPE_THETA ** (-jnp.arange(0, HEAD_DIM, 2, dtype=F32) / HEAD_DIM)
    ang = jnp.arange(seq, dtype=F32)[:, None] * inv_freq[None, :]
    cos, sin = jnp.cos(ang), jnp.sin(ang)
    cos, sin = (jnp.concatenate([t, t, t, t], axis=-1) for t in (cos, sin))
    first_half = (jnp.arange(LANES) % HEAD_DIM < HEAD_DIM // 2)[None, :]
    return cos, jnp.where(first_half, -sin, 0.0), jnp.where(first_half, 0.0, sin)


def _rot(t, sin_a, sin_b):
    return pltpu.roll(t, LANES - HEAD_DIM // 2, 1) * sin_a + pltpu.roll(t, HEAD_DIM // 2, 1) * sin_b


def _rope(t, cos, sin_a, sin_b):
    return t * cos + _rot(t, sin_a, sin_b)


def _rope_transposed(dy, cos, sin_a, sin_b):
    return dy * cos - _rot(dy, sin_a, sin_b)


def _silu_parts(z):
    sig = jax.nn.sigmoid(z)
    return z * sig, sig * (1.0 + z * (1.0 - sig))


def _layer_norm_stats(r):
    mu = jnp.mean(r, axis=-1, keepdims=True)
    d = r - mu
    var = jnp.mean(d * d, axis=-1, keepdims=True)
    rstd = lax.rsqrt(var + LN_EPS)
    return d * rstd, rstd


def _layer_norm_backward(dout, xhat, rstd, gain):
    dxh = dout * gain
    m1 = jnp.mean(dxh, axis=-1, keepdims=True)
    m2 = jnp.mean(dxh * xhat, axis=-1, keepdims=True)
    return rstd * (dxh - m1 - xhat * m2)


def _col_sum(v):
    return jnp.sum(v, axis=0, keepdims=True)


def _pool_forward(xb, w_in, wg, scale):
    seq, dm = xb.shape
    n_g = len(POOL_WINDOWS)
    gd = dm // n_g
    tile = _tile(seq, 1024)
    halo_blocks = tile // POOL_HALO

    def body(x_ref, xp_ref, wu_ref, wz_ref, wg_ref, sc_ref, y_ref, p_ref, mx_ref, z_ref):
        i, g = pl.program_id(0), pl.program_id(1)
        u = jnp.dot(x_ref[...], wu_ref[...], preferred_element_type=F32)
        z = jnp.dot(x_ref[...], wz_ref[...], preferred_element_type=F32)
        prev = jnp.where(i > 0, jnp.dot(xp_ref[...], wu_ref[...], preferred_element_type=F32), 0.0)
        s = jnp.concatenate([prev, u], axis=0)
        sums, sh = [], 1
        while sh < POOL_WINDOWS[-1]:
            s = s + pltpu.roll(s, sh, 0)
            sums.append(s)
            sh *= 2
        win = sums[-1]
        for k in range(n_g - 2, -1, -1):
            win = jnp.where(g == k, sums[k], win)
        row = i * tile + lax.broadcasted_iota(jnp.int32, (tile, 1), 0)
        window = jnp.left_shift(2, g).astype(F32)
        p = win[POOL_HALO:, :] * (1.0 / jnp.minimum((row + 1).astype(F32), window)) - u
        pb = p.astype(BF16)
        mx = jnp.dot(pb, wg_ref[...], preferred_element_type=F32)
        y_ref[...] = (mx * sc_ref[...] * (z * jax.nn.sigmoid(z))).astype(BF16)
        p_ref[...] = pb
        mx_ref[...] = mx
        z_ref[...] = z

    out_spec = pl.BlockSpec((tile, gd), lambda i, g: (i, g))
    return pl.pallas_call(
        body, name="pool_fwd", grid=(seq // tile, n_g),
        out_shape=(jax.ShapeDtypeStruct((seq, dm), BF16), jax.ShapeDtypeStruct((seq, dm), BF16),
                   jax.ShapeDtypeStruct((seq, dm), F32), jax.ShapeDtypeStruct((seq, dm), F32)),
        in_specs=[pl.BlockSpec((tile, dm), lambda i, g: (i, 0)),
                  pl.BlockSpec((POOL_HALO, dm), lambda i, g: (jnp.maximum(i * halo_blocks - 1, 0), 0)),
                  pl.BlockSpec((dm, gd), lambda i, g: (0, g)),
                  pl.BlockSpec((dm, gd), lambda i, g: (0, n_g + g)),
                  pl.BlockSpec((None, gd, gd), lambda i, g: (g, 0, 0)),
                  pl.BlockSpec((1, gd), lambda i, g: (0, g))],
        out_specs=(out_spec, out_spec, out_spec, out_spec),
        compiler_params=_params("parallel", "parallel"),
    )(xb, xb, w_in, w_in, wg, scale)


def _pool_mid_backward(dy, mx, z, p, wg, scale, after):
    seq, dm = dy.shape
    gd = dm // len(POOL_WINDOWS)
    tile = _tile(seq, 256)
    n_i = seq // tile

    def body(dy_ref, mx_ref, z_ref, p_ref, wg_ref, sc_ref, after_ref, dh_ref, dwg_ref, st_ref, dwg_acc, carry):
        del after_ref
        i = pl.program_id(0)
        ti = n_i - 1 - i

        @pl.when(i == 0)
        def _():
            dwg_acc[...] = jnp.zeros_like(dwg_acc)
            carry[...] = jnp.zeros_like(carry)
            st_ref[...] = jnp.zeros_like(st_ref)

        row = ti * tile + lax.broadcasted_iota(jnp.int32, (tile, 1), 0)
        count = (row + 1).astype(F32)
        for g, w in enumerate(POOL_WINDOWS):
            cs = slice(g * gd, (g + 1) * gd)
            z = z_ref[:, cs]
            sz, dsz = _silu_parts(z)
            dyg = dy_ref[:, cs]
            mxg = mx_ref[:, cs]
            sc = sc_ref[:, cs]
            t1 = dyg * sz
            st_ref[0:1, cs] += _col_sum(t1 * mxg)
            dh_ref[1, :, cs] = (dyg * (mxg * sc) * dsz).astype(BF16)
            dmx = (t1 * sc).astype(BF16)
            dwg_acc[g] += lax.dot_general(p_ref[:, cs], dmx, TN, preferred_element_type=F32)
            dp = lax.dot_general(dmx, wg_ref[g], NT, preferred_element_type=F32)
            e = dp * (1.0 / jnp.minimum(count, float(w)))
            s = jnp.concatenate([e, carry[:, cs]], axis=0)
            n = tile + POOL_HALO
            sh = 1
            while sh < w:
                s = s + pltpu.roll(s, n - sh, 0)
                sh *= 2
            dh_ref[0, :, cs] = (s[:tile, :] - dp).astype(BF16)
            carry[:, cs] = e[:POOL_HALO, :]

        @pl.when(i == n_i - 1)
        def _():
            dwg_ref[...] = dwg_acc[...].astype(BF16)

    row_spec = pl.BlockSpec((tile, dm), lambda i: (n_i - 1 - i, 0))
    return pl.pallas_call(
        body, name="pool_mid_bwd", grid=(n_i,),
        out_shape=(jax.ShapeDtypeStruct((2, seq, dm), BF16), jax.ShapeDtypeStruct(wg.shape, BF16),
                   jax.ShapeDtypeStruct((STAT_ROWS, dm), F32)),
        in_specs=[row_spec, row_spec, row_spec, row_spec,
                  pl.BlockSpec(wg.shape, lambda i: (0, 0, 0)),
                  pl.BlockSpec((1, dm), lambda i: (0, 0)), ANY],
        out_specs=(pl.BlockSpec((2, tile, dm), lambda i: (0, n_i - 1 - i, 0)),
                   pl.BlockSpec(wg.shape, lambda i: (0, 0, 0)),
                   pl.BlockSpec((STAT_ROWS, dm), lambda i: (0, 0))),
        scratch_shapes=[pltpu.VMEM(wg.shape, F32), pltpu.VMEM((POOL_HALO, dm), F32)],
        compiler_params=_params("arbitrary"),
    )(dy, mx, z, p, wg, scale, after)


def _out_proj_norm(y, w, x, gain, bias):
    seq, dm = x.shape
    tile = _tile(seq, 512)

    def body(y_ref, w_ref, x_ref, g_ref, b_ref, xhat_ref, rstd_ref, xb_ref):
        o = jnp.dot(y_ref[...], w_ref[...], preferred_element_type=F32)
        xhat, rstd = _layer_norm_stats(ALPHA * x_ref[...] + o)
        xhat_ref[...] = xhat
        rstd_ref[...] = rstd
        xb_ref[...] = (xhat * g_ref[...] + b_ref[...]).astype(BF16)

    row_spec = pl.BlockSpec((tile, dm), lambda i: (i, 0))
    vec_spec = pl.BlockSpec((1, dm), lambda i: (0, 0))
    return pl.pallas_call(
        body, name="out_proj_norm_a", grid=(seq // tile,),
        out_shape=(jax.ShapeDtypeStruct((seq, dm), F32), jax.ShapeDtypeStruct((seq, 1), F32),
                   jax.ShapeDtypeStruct((seq, dm), BF16)),
        in_specs=[row_spec, pl.BlockSpec(w.shape, lambda i: (0, 0), pipeline_mode=pl.Buffered(1)), row_spec, vec_spec,
                  vec_spec],
        out_specs=(row_spec, pl.BlockSpec((tile, 1), lambda i: (i, 0)), row_spec),
        compiler_params=_params("parallel"),
    )(y, w, x, gain, bias)


def _kv_proj(xb, wkv, tables):
    seq, dm = xb.shape
    kvw = wkv.shape[1] // 2
    n_kv = kvw // HEAD_DIM
    tile = _tile(seq, 512)

    def body(x_ref, w_ref, cos_ref, sa_ref, sb_ref, kd_ref, vd_ref, kt_ref, vt_ref):
        kv = jnp.dot(x_ref[...], w_ref[...], preferred_element_type=F32)
        low = lax.broadcasted_iota(jnp.int32, (1, LANES), 1) < HEAD_DIM
        cos, sa, sb = cos_ref[...], sa_ref[...], sb_ref[...]

        def put(pair, h, nat_ref, t_ref):
            swapped = pltpu.roll(pair, HEAD_DIM, 1)
            for head, dup in ((h, jnp.where(low, pair, swapped)), (h + 1, jnp.where(low, swapped, pair))):
                nat_ref[head] = dup.astype(BF16)
                t_ref[head] = dup.T.astype(BF16)

        for j in range(kvw // LANES):
            put(_rope(kv[:, j * LANES:(j + 1) * LANES], cos, sa, sb), 2 * j, kd_ref, kt_ref)
            put(kv[:, kvw + j * LANES:kvw + (j + 1) * LANES], 2 * j, vd_ref, vt_ref)

    tab_spec = pl.BlockSpec((tile, LANES), lambda i: (i, 0))
    dup_spec = pl.BlockSpec((n_kv, tile, LANES), lambda i: (0, i, 0))
    dup_shape = jax.ShapeDtypeStruct((n_kv, seq, LANES), BF16)
    t_spec = pl.BlockSpec((n_kv, LANES, tile), lambda i: (0, 0, i))
    t_shape = jax.ShapeDtypeStruct((n_kv, LANES, seq), BF16)
    return pl.pallas_call(
        body, name="kv_proj", grid=(seq // tile,),
        out_shape=(dup_shape, dup_shape, t_shape, t_shape),
        in_specs=[pl.BlockSpec((tile, dm), lambda i: (i, 0)), pl.BlockSpec(wkv.shape, lambda i: (0, 0)),
                  tab_spec, tab_spec, tab_spec],
        out_specs=(dup_spec, dup_spec, t_spec, t_spec),
        compiler_params=_params("parallel"),
    )(xb, wkv, *tables)


def _head_queries(q_ref, low):
    parts = []
    for j in range(GQA_GROUP // 2):
        q2 = q_ref[:, j * LANES:(j + 1) * LANES]
        parts += [jnp.where(low, q2, 0), jnp.where(low, 0, q2)]
    return parts


def _probs_transposed(n, kh, kcat, q_all, sink_ref):
    st = lax.dot_general(kcat, q_all, NT, preferred_element_type=F32)
    key = lax.broadcasted_iota(jnp.int32, (2 * ATTN_BLOCK, ATTN_BLOCK), 0)
    qry = lax.broadcasted_iota(jnp.int32, (2 * ATTN_BLOCK, ATTN_BLOCK), 1)
    valid = (key > qry) & (key <= qry + ATTN_BLOCK) & ((key >= ATTN_BLOCK) | (n > 0))
    st = st + jnp.tile(jnp.where(valid, 0.0, NEG_INF), (1, GQA_GROUP))
    sink = jnp.concatenate([jnp.full((1, ATTN_BLOCK), sink_ref[0, kh * GQA_GROUP + h], F32)
                            for h in range(GQA_GROUP)], axis=1)
    m = jnp.maximum(jnp.max(st, axis=0, keepdims=True), sink)
    e = jnp.exp(st - m)
    e_sink = jnp.exp(sink - m)
    inv = 1.0 / (jnp.sum(e, axis=0, keepdims=True) + e_sink)
    return e * inv, e_sink * inv


def _attn_specs(n_width):
    q_spec = pl.BlockSpec((ATTN_BLOCK, n_width), lambda kh, n: (n, kh))
    cur = pl.BlockSpec((None, ATTN_BLOCK, LANES), lambda kh, n: (kh, n, 0))
    prev = pl.BlockSpec((None, ATTN_BLOCK, LANES), lambda kh, n: (kh, jnp.maximum(n - 1, 0), 0))
    cur_t = pl.BlockSpec((None, LANES, ATTN_BLOCK), lambda kh, n: (kh, 0, n))
    prev_t = pl.BlockSpec((None, LANES, ATTN_BLOCK), lambda kh, n: (kh, 0, jnp.maximum(n - 1, 0)))
    return q_spec, cur, prev, cur_t, prev_t


def _pair_product_transposed(mat_t, rhs, j, low_rows):
    a = rhs[:, 2 * j * ATTN_BLOCK:(2 * j + 1) * ATTN_BLOCK]
    b = rhs[:, (2 * j + 1) * ATTN_BLOCK:(2 * j + 2) * ATTN_BLOCK]
    out_t = (jnp.dot(jnp.where(low_rows, mat_t, 0), a, preferred_element_type=F32)
             + jnp.dot(jnp.where(low_rows, 0, mat_t), b, preferred_element_type=F32))
    return out_t.T


def _attn_forward(qs, kd, vt, zb, sinks):
    seq, dm = qs.shape
    n_kv = kd.shape[0]
    gw = GQA_GROUP * HEAD_DIM

    def body(q_ref, kp_ref, kc_ref, vtp_ref, vtc_ref, z_ref, sink_ref, att_ref, yb_ref):
        kh, n = pl.program_id(0), pl.program_id(1)
        low = lax.broadcasted_iota(jnp.int32, (1, LANES), 1) < HEAD_DIM
        low_rows = lax.broadcasted_iota(jnp.int32, (LANES, 1), 0) < HEAD_DIM
        kcat = jnp.concatenate([kp_ref[...], kc_ref[...]], axis=0)
        vt = jnp.concatenate([vtp_ref[...], vtc_ref[...]], axis=1)
        q_all = jnp.concatenate(_head_queries(q_ref, low), axis=0)
        probs_t, _ = _probs_transposed(n, kh, kcat, q_all, sink_ref)
        pt = probs_t.astype(BF16)
        for j in range(GQA_GROUP // 2):
            cs = slice(j * LANES, (j + 1) * LANES)
            o2 = _pair_product_transposed(vt, pt, j, low_rows)
            att_ref[:, cs] = o2
            z = z_ref[:, cs]
            yb_ref[:, cs] = (o2 * (z * jax.nn.sigmoid(z))).astype(BF16)

    q_spec, cur, prev, cur_t, prev_t = _attn_specs(gw)
    return pl.pallas_call(
        body, name="attn_fwd", grid=(n_kv, seq // ATTN_BLOCK),
        out_shape=(jax.ShapeDtypeStruct((seq, dm), F32), jax.ShapeDtypeStruct((seq, dm), BF16)),
        in_specs=[q_spec, prev, cur, prev_t, cur_t, q_spec, pl.BlockSpec(memory_space=pltpu.SMEM)],
        out_specs=(q_spec, q_spec),
        compiler_params=_params("parallel", "parallel"),
    )(qs, kd, kd, vt, vt, zb, sinks)


def _attn_backward(qs, kd, vd, kt, zb, att, dyb, sinks, tables, after):
    seq, dm = qs.shape
    n_kv = kd.shape[0]
    gw = GQA_GROUP * HEAD_DIM
    n_blocks = seq // ATTN_BLOCK

    def body(q_ref, kp_ref, kc_ref, vp_ref, vc_ref, ktp_ref, ktc_ref, z_ref, att_ref, dyb_ref, sink_ref,
             cos_ref, sa_ref, sb_ref, after_ref, dh_ref, dk_ref, dv_ref, ds_ref):
        del after_ref
        kh, n = pl.program_id(0), pl.program_id(1)

        @pl.when(n == 0)
        def _():
            dk_ref[...] = jnp.zeros_like(dk_ref)
            dv_ref[...] = jnp.zeros_like(dv_ref)

        @pl.when(jnp.logical_and(n == 0, kh == 0))
        def _():
            ds_ref[...] = jnp.zeros_like(ds_ref)

        low = lax.broadcasted_iota(jnp.int32, (1, LANES), 1) < HEAD_DIM
        low_rows = lax.broadcasted_iota(jnp.int32, (LANES, 1), 0) < HEAD_DIM
        head_lane = lax.broadcasted_iota(jnp.int32, (1, LANES), 1)
        kcat = jnp.concatenate([kp_ref[...], kc_ref[...]], axis=0)
        vcat = jnp.concatenate([vp_ref[...], vc_ref[...]], axis=0)
        kt = jnp.concatenate([ktp_ref[...], ktc_ref[...]], axis=1)
        cos, sa, sb = cos_ref[...], sa_ref[...], sb_ref[...]
        q_parts = _head_queries(q_ref, low)
        q_all = jnp.concatenate(q_parts, axis=0)
        d_parts = []
        for j in range(GQA_GROUP // 2):
            cs = slice(j * LANES, (j + 1) * LANES)
            sz, dsz = _silu_parts(z_ref[:, cs])
            dy2 = dyb_ref[:, cs]
            dh_ref[1, :, cs] = (dy2 * att_ref[:, cs] * dsz).astype(BF16)
            datt = (dy2 * sz).astype(BF16)
            d_parts += [jnp.where(low, datt, 0), jnp.where(low, 0, datt)]
        d_all = jnp.concatenate(d_parts, axis=0)
        probs_t, sink_p = _probs_transposed(n, kh, kcat, q_all, sink_ref)
        dprobs_t = lax.dot_general(vcat, d_all, NT, preferred_element_type=F32)
        row_dot = jnp.sum(probs_t * dprobs_t, axis=0, keepdims=True)
        ds_t = (probs_t * (dprobs_t - row_dot)).astype(BF16)
        dk = jnp.dot(ds_t, q_all, preferred_element_type=F32)
        dv = jnp.dot(probs_t.astype(BF16), d_all, preferred_element_type=F32)
        for j in range(GQA_GROUP // 2):
            dq2 = _pair_product_transposed(kt, ds_t, j, low_rows)
            dh_ref[0, :, j * LANES:(j + 1) * LANES] = (_rope_transposed(dq2, cos, sa, sb) * 0.125).astype(BF16)
        sink_dot = sink_p * row_dot
        dsink = jnp.zeros((1, LANES), F32)
        for h in range(GQA_GROUP):
            part = jnp.sum(sink_dot[:, h * ATTN_BLOCK:(h + 1) * ATTN_BLOCK], axis=1, keepdims=True)
            dsink = dsink - jnp.where(head_lane == kh * GQA_GROUP + h, part, 0.0)
        ds_ref[0:1, :] += dsink

        @pl.when(n == 0)
        def _():
            dk_ref[pl.ds(0, ATTN_BLOCK), :] += dk[ATTN_BLOCK:, :]
            dv_ref[pl.ds(0, ATTN_BLOCK), :] += dv[ATTN_BLOCK:, :]

        @pl.when(n > 0)
        def _():
            start = pl.multiple_of((n - 1) * ATTN_BLOCK, ATTN_BLOCK)
            dk_ref[pl.ds(start, 2 * ATTN_BLOCK), :] += dk
            dv_ref[pl.ds(start, 2 * ATTN_BLOCK), :] += dv

    q_spec, cur, prev, cur_t, prev_t = _attn_specs(gw)
    tab_spec = pl.BlockSpec((ATTN_BLOCK, LANES), lambda kh, n: (n, 0))
    acc_spec = pl.BlockSpec((None, seq, LANES), lambda kh, n: (kh, 0, 0))
    acc_shape = jax.ShapeDtypeStruct((n_kv, seq, LANES), F32)
    return pl.pallas_call(
        body, name="attn_bwd", grid=(n_kv, n_blocks),
        out_shape=(jax.ShapeDtypeStruct((2, seq, dm), BF16), acc_shape, acc_shape,
                   jax.ShapeDtypeStruct((STAT_ROWS, LANES), F32)),
        in_specs=[q_spec, prev, cur, prev, cur, prev_t, cur_t, q_spec, q_spec, q_spec,
                  pl.BlockSpec(memory_space=pltpu.SMEM), tab_spec, tab_spec, tab_spec, ANY],
        out_specs=(pl.BlockSpec((2, ATTN_BLOCK, gw), lambda kh, n: (0, n, kh)), acc_spec, acc_spec,
                   pl.BlockSpec((STAT_ROWS, LANES), lambda kh, n: (0, 0))),
        compiler_params=_params("arbitrary", "arbitrary"),
    )(qs, kd, kd, vd, vd, kt, kt, zb, att, dyb, sinks, *tables, after)


def _kv_grad_fold(dk, dv, tables):
    n_kv, seq, _ = dk.shape
    kvw = n_kv * HEAD_DIM
    tile = _tile(seq, 512)

    def body(dk_ref, dv_ref, cos_ref, sa_ref, sb_ref, o_ref):
        low = lax.broadcasted_iota(jnp.int32, (1, LANES), 1) < HEAD_DIM
        cos, sa, sb = cos_ref[...], sa_ref[...], sb_ref[...]

        def folded(ref, h):
            t = ref[h]
            return t + pltpu.roll(t, HEAD_DIM, 1)

        for j in range(n_kv // 2):
            ka = _rope_transposed(folded(dk_ref, 2 * j), cos, sa, sb)
            kb = _rope_transposed(folded(dk_ref, 2 * j + 1), cos, sa, sb)
            o_ref[:, j * LANES:(j + 1) * LANES] = jnp.where(low, ka, kb).astype(BF16)
            o_ref[:, kvw + j * LANES:kvw + (j + 1) * LANES] = jnp.where(
                low, folded(dv_ref, 2 * j), folded(dv_ref, 2 * j + 1)).astype(BF16)

    tab_spec = pl.BlockSpec((tile, LANES), lambda i: (i, 0))
    in_spec = pl.BlockSpec((n_kv, tile, LANES), lambda i: (0, i, 0))
    return pl.pallas_call(
        body, name="kv_grad_fold", grid=(seq // tile,),
        out_shape=jax.ShapeDtypeStruct((seq, 2 * kvw), BF16),
        in_specs=[in_spec, in_spec, tab_spec, tab_spec, tab_spec],
        out_specs=pl.BlockSpec((tile, 2 * kvw), lambda i: (i, 0)),
        compiler_params=_params("parallel"),
    )(dk, dv, *tables)


def _out_proj_norm_loss(yb, w, xhat1, gain0, bias0, gain1, bias1, target):
    seq, dm = xhat1.shape
    tile = _tile(seq, 512)

    def body(y_ref, w_ref, xh1_ref, g0_ref, b0_ref, g1_ref, b1_ref, t_ref, dr_ref, drb_ref, st_ref):
        i = pl.program_id(0)

        @pl.when(i == 0)
        def _():
            st_ref[...] = jnp.zeros_like(st_ref)

        ob = jnp.dot(y_ref[...], w_ref[...], preferred_element_type=F32)
        x1 = xh1_ref[...] * g0_ref[...] + b0_ref[...]
        xhat, rstd = _layer_norm_stats(ALPHA * x1 + ob)
        err = xhat * g1_ref[...] + b1_ref[...] - t_ref[...]
        dout = err * (1.0 / dm)
        dr = _layer_norm_backward(dout, xhat, rstd, g1_ref[...])
        dr_ref[...] = dr
        drb_ref[...] = dr.astype(BF16)
        st_ref[0:1, :] += _col_sum(dout * xhat)
        st_ref[1:2, :] += _col_sum(dout)
        st_ref[2:3, :] += _col_sum(err * err)

    row_spec = pl.BlockSpec((tile, dm), lambda i: (i, 0))
    vec_spec = pl.BlockSpec((1, dm), lambda i: (0, 0))
    return pl.pallas_call(
        body, name="out_proj_norm_loss_b", grid=(seq // tile,),
        out_shape=(jax.ShapeDtypeStruct((seq, dm), F32), jax.ShapeDtypeStruct((seq, dm), BF16),
                   jax.ShapeDtypeStruct((STAT_ROWS, dm), F32)),
        in_specs=[row_spec, pl.BlockSpec(w.shape, lambda i: (0, 0), pipeline_mode=pl.Buffered(1)), row_spec, vec_spec,
                  vec_spec, vec_spec,
                  vec_spec, row_spec],
        out_specs=(row_spec, row_spec, pl.BlockSpec((STAT_ROWS, dm), lambda i: (0, 0))),
        compiler_params=_params("arbitrary"),
    )(yb, w, xhat1, gain0, bias0, gain1, bias1, target)


def _stream_grad_norm_backward(dhq, wqg, dkv, wkv, dr2, xhat1, rstd1, gain0):
    seq, dm = dr2.shape
    tile = _tile(seq, 256)

    def body(dh_ref, wqg_ref, dkv_ref, wkv_ref, dr2_ref, xh_ref, rstd_ref, g_ref, dr_ref, drb_ref, st_ref):
        @pl.when(pl.program_id(0) == 0)
        def _():
            st_ref[...] = jnp.zeros_like(st_ref)

        dx1 = (lax.dot_general(dh_ref[0], wqg_ref[:, :dm], NT, preferred_element_type=F32)
               + lax.dot_general(dh_ref[1], wqg_ref[:, dm:], NT, preferred_element_type=F32)
               + lax.dot_general(dkv_ref[...], wkv_ref[...], NT, preferred_element_type=F32)
               + ALPHA * dr2_ref[...])
        xhat = xh_ref[...]
        dr = _layer_norm_backward(dx1, xhat, rstd_ref[...], g_ref[...])
        dr_ref[...] = dr
        drb_ref[...] = dr.astype(BF16)
        st_ref[0:1, :] += _col_sum(dx1 * xhat)
        st_ref[1:2, :] += _col_sum(dx1)

    row_spec = pl.BlockSpec((tile, dm), lambda i: (i, 0))
    resident = pl.Buffered(1)
    return pl.pallas_call(
        body, name="stream_grad_norm_bwd", grid=(seq // tile,),
        out_shape=(jax.ShapeDtypeStruct((seq, dm), F32), jax.ShapeDtypeStruct((seq, dm), BF16),
                   jax.ShapeDtypeStruct((STAT_ROWS, dm), F32)),
        in_specs=[pl.BlockSpec((2, tile, dm), lambda i: (0, i, 0)),
                  pl.BlockSpec(wqg.shape, lambda i: (0, 0), pipeline_mode=resident),
                  pl.BlockSpec((tile, dkv.shape[1]), lambda i: (i, 0)),
                  pl.BlockSpec(wkv.shape, lambda i: (0, 0), pipeline_mode=resident),
                  row_spec, row_spec, pl.BlockSpec((tile, 1), lambda i: (i, 0)),
                  pl.BlockSpec((1, dm), lambda i: (0, 0))],
        out_specs=(row_spec, row_spec, pl.BlockSpec((STAT_ROWS, dm), lambda i: (0, 0))),
        compiler_params=_params("arbitrary"),
    )(dhq, wqg, dkv, wkv, dr2, xhat1, rstd1, gain0)


def _adamw_math(w, g, m, v):
    m = ADAM_B1 * m + (1.0 - ADAM_B1) * g
    v = ADAM_B2 * v + (1.0 - ADAM_B2) * (g * g)
    m_hat = m / (1.0 - ADAM_B1 ** ADAM_STEP)
    v_hat = v / (1.0 - ADAM_B2 ** ADAM_STEP)
    delta = -ADAM_LR * (m_hat / (jnp.sqrt(v_hat) + ADAM_EPS) + ADAM_WD * w)
    return delta, m, v


def _sum_devices(ref):
    total = ref[0].astype(F32)
    for d in range(1, N_DEV):
        total = total + ref[d].astype(F32)
    return total


def _adamw_shard(name, parts, w, m, v, after):
    rows, cols = w.shape
    n_parts = len(parts)
    part_rows = rows // n_parts
    tr = _tile(part_rows, max(8, (1 << 18) // cols)) if part_rows >= 8 else part_rows
    per_part = part_rows // tr

    def body(*refs):
        p_refs = refs[:n_parts]
        w_ref, m_ref, v_ref, _, g_out, d_out, m_out, v_out = refs[n_parts:]
        g = _sum_devices(p_refs[0])
        for k in range(1, n_parts):
            g = jnp.where(pl.program_id(0) >= k * per_part, _sum_devices(p_refs[k]), g)
        delta, m_new, v_new = _adamw_math(w_ref[...], g, m_ref[...], v_ref[...])
        g_out[...] = g
        d_out[...] = delta
        m_out[...] = m_new
        v_out[...] = v_new

    def part_spec(k):
        return pl.BlockSpec((N_DEV, tr, cols), lambda i: (0, jnp.clip(i - k * per_part, 0, per_part - 1), 0))

    spec = pl.BlockSpec((tr, cols), lambda i: (i, 0))
    shape = jax.ShapeDtypeStruct((rows, cols), F32)
    return pl.pallas_call(
        body, name=name, grid=(rows // tr,),
        out_shape=(shape, shape, shape, shape),
        in_specs=[part_spec(k) for k in range(n_parts)] + [spec, spec, spec, ANY],
        out_specs=(spec, spec, spec, spec),
        compiler_params=_params("arbitrary"),
    )(*parts, w, m, v, after)


def _adamw_replicated(stats_b, stats_a, sink_parts, ln_g, ln_b, sinks, m_ln_g, m_ln_b, m_sinks, v_ln_g, v_ln_b,
                      v_sinks, after):
    n_q = sinks.shape[1]
    dm = ln_g.shape[1]

    def body(sb_ref, sa_ref, sk_ref, g_ref, b_ref, s_ref, mg_ref, mb_ref, ms_ref, vg_ref, vb_ref, vs_ref, after_ref,
             *outs):
        del after_ref
        layer_sums = (_sum_devices(sa_ref), _sum_devices(sb_ref))
        outs[12][...] = jnp.sum(layer_sums[1][2:3, :], axis=1, keepdims=True) * (0.5 / dm)
        for which, (w_ref, m_ref, v_ref) in enumerate(((g_ref, mg_ref, vg_ref), (b_ref, mb_ref, vb_ref))):
            for layer in range(DEPTH):
                row = slice(layer, layer + 1)
                g = layer_sums[layer][which:which + 1, :]
                res = (g,) + _adamw_math(w_ref[row, :], g, m_ref[row, :], v_ref[row, :])
                for o_ref, val in zip(outs[4 * which:4 * which + 4], res):
                    o_ref[row, :] = val
        g = _sum_devices(sk_ref)[0:1, 0:n_q]
        res = (g,) + _adamw_math(s_ref[...], g, ms_ref[...], vs_ref[...])
        for o_ref, val in zip(outs[8:12], res):
            o_ref[...] = val

    vmem = pl.BlockSpec(memory_space=pltpu.VMEM)
    shapes = [jax.ShapeDtypeStruct(a.shape, F32) for a in (ln_g, ln_b, sinks) for _ in range(4)]
    shapes.append(jax.ShapeDtypeStruct((1, 1), F32))
    return pl.pallas_call(
        body, name="adamw_replicated", out_shape=tuple(shapes),
        in_specs=[vmem] * 12 + [ANY], out_specs=tuple([vmem] * 13),
    )(stats_b, stats_a, sink_parts, ln_g, ln_b, sinks, m_ln_g, m_ln_b, m_sinks, v_ln_g, v_ln_b, v_sinks, after)


def kernel(x, ln_g, ln_b, a_w_in, a_w_group, a_scale, a_w_out, b_w_k, b_w_v, b_w_qg, b_sinks, b_w_out, loss_target, m_ln_g, m_ln_b, m_a_w_in, m_a_w_group, m_a_scale, m_a_w_out, m_b_w_k, m_b_w_v, m_b_w_qg, m_b_sinks, m_b_w_out, v_ln_g, v_ln_b, v_a_w_in, v_a_w_group, v_a_scale, v_a_w_out, v_b_w_k, v_b_w_v, v_b_w_qg, v_b_sinks, v_b_w_out):
    _, seq, dm = x.shape
    n_groups = len(POOL_WINDOWS)
    gd = dm // n_groups
    kvw = b_w_k.shape[1]
    cb = 2 * dm // N_DEV
    rb = dm // N_DEV
    gb = gd // N_DEV

    x2 = x.reshape(seq, dm)
    target = loss_target.reshape(seq, dm)
    w_in_s = a_w_in.reshape(dm, cb)
    w_g_s = a_w_group.reshape(n_groups, gb, gd)
    w_out_s = a_w_out.reshape(rb, dm)
    w_qg_s = b_w_qg.reshape(dm, cb)
    w_outb_s = b_w_out.reshape(rb, dm)

    def cols(ref, dev):
        return ref.at[:, pl.ds(pl.multiple_of(dev * cb, LANES), cb)]

    def rows(ref, dev):
        return ref.at[pl.ds(pl.multiple_of(dev * rb, 8), rb), :]

    def group_rows(ref, dev):
        return ref.at[:, pl.ds(pl.multiple_of(dev * gb, 8), gb), :]

    def k_rows(ref, dev):
        return ref.at[pl.ds(pl.multiple_of(dev * rb, 8), rb), pl.ds(0, kvw)]

    def v_rows(ref, dev):
        return ref.at[pl.ds(pl.multiple_of(dev * rb, 8), rb), pl.ds(kvw, kvw)]

    def scale_cols(ref, dev):
        return ref.at[:, pl.ds(pl.multiple_of(dev * rb, LANES), rb)]

    bf = lambda a: a.astype(BF16)
    wide, square = jax.ShapeDtypeStruct((dm, 2 * dm), BF16), jax.ShapeDtypeStruct((dm, dm), BF16)
    w_g, scale, w_in = _gather_weights(
        "gather_a_in", 0, [(bf(w_g_s), 0, group_rows), (a_scale, 1, scale_cols), (bf(w_in_s), 2, cols)],
        [jax.ShapeDtypeStruct((n_groups, gd, gd), BF16), jax.ShapeDtypeStruct((1, dm), F32), wide])
    (w_out,) = _gather_weights("gather_a_out", 1, [(bf(w_out_s), 0, rows)], [square])
    w_kv, w_qg = _gather_weights(
        "gather_b_in", 2, [(bf(b_w_k), 0, k_rows), (bf(b_w_v), 0, v_rows), (bf(w_qg_s), 1, cols)],
        [jax.ShapeDtypeStruct((dm, 2 * kvw), BF16), wide])
    (w_outb,) = _gather_weights("gather_b_out", 3, [(bf(w_outb_s), 0, rows)], [square])

    tables = _rope_tables(seq)
    bm = _tile(seq, 1024)
    bn = _tile(dm, 1024)
    g0, g1, b0, b1 = ln_g[0:1], ln_g[1:2], ln_b[0:1], ln_b[1:2]

    xb = _cast_bf16("cast_x", x2)
    y, pooled, mixed, z_a = _pool_forward(xb, w_in, w_g, scale)
    xhat1, rstd1, x1b = _out_proj_norm(y, w_out, x2, g0, b0)

    kd, vd, kt, vt = _kv_proj(x1b, w_kv, tables)
    bmq = bm
    tab_spec = pl.BlockSpec((bmq, LANES), lambda i, j: (i, 0))

    def rope_scale(val, cos_ref, sa_ref, sb_ref):
        cos, sa, sb = cos_ref[...], sa_ref[...], sb_ref[...]
        return jnp.concatenate([_rope(val[:, j * LANES:(j + 1) * LANES], cos, sa, sb) * 0.125
                                for j in range(val.shape[1] // LANES)], axis=1)

    qs = _mm("b_q_proj", x1b, w_qg, dims=NN, grid=(seq // bmq, dm // bn),
             a_spec=pl.BlockSpec((bmq, dm), lambda i, j: (i, 0)), b_spec=pl.BlockSpec((dm, bn), lambda i, j: (0, j)),
             out_shape=jax.ShapeDtypeStruct((seq, dm), BF16), out_spec=pl.BlockSpec((bmq, bn), lambda i, j: (i, j)),
             epilogue=rope_scale, extras=tables, extra_specs=(tab_spec,) * 3)
    zb = _mm("b_gate_proj", x1b, w_qg, dims=NN, grid=(seq // bm, dm // bn),
             a_spec=pl.BlockSpec((bm, dm), lambda i, j: (i, 0)),
             b_spec=pl.BlockSpec((dm, bn), lambda i, j: (0, j + dm // bn)),
             out_shape=jax.ShapeDtypeStruct((seq, dm), F32), out_spec=pl.BlockSpec((bm, bn), lambda i, j: (i, j)))
    att, yb = _attn_forward(qs, kd, vt, zb, b_sinks)
    dr2, dr2b, stats_b = _out_proj_norm_loss(yb, w_outb, xhat1, g0, b0, g1, b1, target)

    def weight_grad(name, a, b, n_cols, b_spec=None, part=(0, 1)):
        m_cols = a.shape[1] // part[1]
        tm, tn = _tile(m_cols, 1024), _tile(n_cols, 512)
        first = part[0] * (m_cols // tm)
        return _mm(name, a, b, dims=TN, grid=(m_cols // tm, n_cols // tn),
                   a_spec=pl.BlockSpec((seq, tm), lambda i, j: (0, first + i)),
                   b_spec=b_spec(tn) if b_spec else pl.BlockSpec((seq, tn), lambda i, j: (0, j)),
                   out_shape=jax.ShapeDtypeStruct((m_cols, n_cols), BF16),
                   out_spec=pl.BlockSpec((tm, tn), lambda i, j: (i, j)))

    def halves_spec(tn):
        per = dm // tn
        return pl.BlockSpec((None, seq, tn), lambda i, j: (j // per, 0, j % per))

    def times_transposed(name, a, w):
        return _mm(name, a, w, dims=NT, grid=(seq // bm, dm // bn),
                   a_spec=pl.BlockSpec((bm, a.shape[1]), lambda i, j: (i, 0)),
                   b_spec=pl.BlockSpec((bn, w.shape[1]), lambda i, j: (j, 0)),
                   out_shape=jax.ShapeDtypeStruct((seq, dm), F32), out_spec=pl.BlockSpec((bm, bn), lambda i, j: (i, j)))

    def stat_row_cols(ref, dev):
        return ref.at[pl.ds(0, 1), pl.ds(pl.multiple_of(dev * rb, LANES), rb)]

    dyb = times_transposed("b_out_proj_dx", dr2b, w_outb)
    d_w_outb = weight_grad("b_out_proj_dw", yb, dr2b, dm)
    (p_outb,) = _exchange_blocks("scatter_b_out", [(d_w_outb, rows, (rb, dm))], 4)
    dhq, dkd, dvd, dsink = _attn_backward(qs, kd, vd, kt, zb, att, dyb, b_sinks, tables, after=d_w_outb)
    dkv = _kv_grad_fold(dkd, dvd, tables)
    d_w_qg = weight_grad("b_qg_proj_dw", x1b, dhq, 2 * dm, halves_spec)
    d_w_kv = weight_grad("b_kv_proj_dw", x1b, dkv, 2 * kvw)
    p_qg, p_k, p_v = _exchange_blocks("scatter_b_in", [
        (d_w_qg, cols, (dm, cb)), (d_w_kv, k_rows, (rb, kvw)), (d_w_kv, v_rows, (rb, kvw))], 5)
    dr1, dr1b, stats_a = _stream_grad_norm_backward(dhq, w_qg, dkv, w_kv, dr2, xhat1, rstd1, g0)
    all_b, all_a, all_sink = _exchange_blocks("gather_replicated_grads", [
        (stats_b, None, stats_b.shape), (stats_a, None, stats_a.shape), (dsink, None, dsink.shape)], 9)

    dy = times_transposed("a_out_proj_dx", dr1b, w_out)
    d_w_out = weight_grad("a_out_proj_dw", y, dr1b, dm)
    (p_out,) = _exchange_blocks("scatter_a_out", [(d_w_out, rows, (rb, dm))], 6)
    dh, d_w_g, stats_s = _pool_mid_backward(dy, mixed, z_a, pooled, w_g, scale, after=d_w_out)
    p_g, p_scale = _exchange_blocks("scatter_a_mid", [
        (d_w_g, group_rows, (n_groups, gb, gd)), (stats_s, stat_row_cols, (1, rb))], 7)
    p_in = []
    for k in range(2):
        d_w_in = weight_grad(f"a_in_proj_dw_{k}", xb, dh, 2 * dm, halves_spec, part=(k, 2))
        p_in += _exchange_blocks(f"scatter_a_in_{k}", [(d_w_in, cols, (dm // 2, cb))], 8 + 2 * k)
    grad_x = _mm("a_in_proj_dx", dh, w_in, dims=NT, grid=(seq // bm, dm // bn, 2), nk=2,
                 a_spec=pl.BlockSpec((None, bm, dm), lambda i, j, k: (k, i, 0)),
                 b_spec=pl.BlockSpec((bn, dm), lambda i, j, k: (j, k)),
                 out_shape=jax.ShapeDtypeStruct((seq, dm), F32), out_spec=pl.BlockSpec((bm, bn), lambda i, j, k: (i, j)),
                 add=dr1, add_spec=pl.BlockSpec((bm, bn), lambda i, j, k: (i, j)), add_scale=ALPHA,
                 extras=(d_w_in,), extra_specs=(ANY,))

    upd = {}
    last = [grad_x]

    def shard_update(key, parts, w, m, v):
        shape = w.shape
        flat = lambda a: a.reshape(-1, shape[-1])
        parts = parts if isinstance(parts, list) else [parts]
        outs = _adamw_shard("adamw_" + key, [p.reshape(N_DEV, -1, shape[-1]) for p in parts], flat(w), flat(m),
                            flat(v), last[0])
        last[0] = outs[0]
        upd[key] = [o.reshape(shape) for o in outs]

    shard_update("b_w_out", p_outb, b_w_out, m_b_w_out, v_b_w_out)
    shard_update("b_w_qg", p_qg, b_w_qg, m_b_w_qg, v_b_w_qg)
    shard_update("b_w_k", p_k, b_w_k, m_b_w_k, v_b_w_k)
    shard_update("b_w_v", p_v, b_w_v, m_b_w_v, v_b_w_v)
    rep = _adamw_replicated(all_b, all_a, all_sink, ln_g, ln_b, b_sinks, m_ln_g, m_ln_b, m_b_sinks, v_ln_g, v_ln_b,
                            v_b_sinks, last[0])
    last[0] = rep[0]
    upd["ln_g"], upd["ln_b"], upd["b_sinks"] = list(rep[0:4]), list(rep[4:8]), list(rep[8:12])
    shard_update("a_w_out", p_out, a_w_out, m_a_w_out, v_a_w_out)
    shard_update("a_w_group", p_g, a_w_group, m_a_w_group, v_a_w_group)
    shard_update("a_scale", p_scale, a_scale, m_a_scale, v_a_scale)
    shard_update("a_w_in", p_in, a_w_in, m_a_w_in, v_a_w_in)

    loss = rep[12].reshape(())
    order = ["ln_g", "ln_b", "a_w_in", "a_w_group", "a_scale", "a_w_out", "b_w_k", "b_w_v", "b_w_qg", "b_sinks",
             "b_w_out"]
    return (loss, grad_x.reshape(x.shape), *[upd[n][0] for n in order], *[upd[n][1] for n in order],
            *[upd[n][2] for n in order], *[upd[n][3] for n in order])
```

```python
import functools

import jax
import jax.numpy as jnp
from jax import lax
from jax.experimental import pallas as pl
from jax.experimental.pallas import tpu as pltpu
from jax.experimental.pallas import tpu_sc as plsc

F32 = jnp.float32
BF16 = jnp.bfloat16
MESH = pl.DeviceIdType.MESH
AXES = ("x", "y", "c")
N_DEV = 8

POOL_WINDOWS = (2, 4, 8, 16)
POOL_HALO = 16
HEAD_DIM = 64
GQA_GROUP = 8
ATTN_BLOCK = 128
ROPE_THETA = 10000.0
LN_EPS = 1e-5
NEG_INF = -1e30
DEPTH = 2
ALPHA = (2 * DEPTH) ** 0.25
ADAM_LR = 0.001
ADAM_B1 = 0.9
ADAM_B2 = 0.999
ADAM_EPS = 1e-08
ADAM_WD = 0.01
ADAM_STEP = 10

LANES = 128
STAT_ROWS = 8


def _tile(n, want):
    t = min(n, want)
    while n % t:
        t //= 2
    return t


def _params(*sem):
    return pltpu.CompilerParams(dimension_semantics=sem)


ANY = pl.BlockSpec(memory_space=pl.ANY)


def _my_pos():
    return lax.axis_index("x"), lax.axis_index("y"), lax.axis_index("c")


def _dev_index(p):
    return 4 * p[0] + 2 * p[1] + p[2]


def _handshake(peers):
    barrier = pltpu.get_barrier_semaphore()
    for peer in peers:
        pl.semaphore_signal(barrier, inc=1, device_id=peer, device_id_type=MESH)
    pl.semaphore_wait(barrier, len(peers))


def _launch_on_sequencer(name, collective_id, body, operands, out_shapes, scratch):
    return pl.kernel(
        body, out_type=tuple(out_shapes), name=name,
        mesh=plsc.ScalarSubcoreMesh(axis_name="sequencer", num_cores=1), scratch_types=scratch,
        compiler_params=pltpu.CompilerParams(collective_id=collective_id),
    )(*operands)


def _gather_weights(name, collective_id, streams, out_shapes):
    n_s = len(streams)
    n_out = len(out_shapes)

    def body(*refs):
        srcs = refs[:n_s]
        outs = refs[n_s:n_s + n_out]
        send_sems, recv_sems, local_sems = refs[n_s + n_out:]
        x, y, c = _my_pos()
        me, sibling = (x, y, c), (x, y, 1 - c)
        chips = [(1 - x, y), (x, 1 - y), (1 - x, 1 - y)]
        _handshake([sibling] + [(*chip, c) for chip in chips])

        def copy(s, k, block, to, from_shard=False):
            out_ref = outs[streams[s][1]]
            win = streams[s][2](out_ref, _dev_index(block))
            return pltpu.make_async_remote_copy(
                src_ref=srcs[s] if from_shard else win, dst_ref=win,
                send_sem=send_sems.at[7 * s + k], recv_sem=recv_sems.at[7 * s + k],
                device_id=to, device_id_type=MESH)

        mine = [pltpu.make_async_copy(srcs[s], streams[s][2](outs[streams[s][1]], _dev_index(me)), local_sems.at[s])
                for s in range(n_s)]
        for cp in mine:
            cp.start()
        first = []
        for s in range(n_s):
            first.append(copy(s, 0, me, sibling, True))
            first += [copy(s, 1 + j, me, (*chip, c), True) for j, chip in enumerate(chips)]
        for cp in first:
            cp.start()
        passed = []
        for j, chip in enumerate(chips):
            for s in range(n_s):
                copy(s, 1 + j, (*chip, c), me).wait_recv()
                fwd = copy(s, 4 + j, (*chip, c), sibling)
                fwd.start()
                passed.append(fwd)
        for s in range(n_s):
            copy(s, 0, sibling, me).wait_recv()
            for j, chip in enumerate(chips):
                copy(s, 4 + j, (*chip, 1 - c), me).wait_recv()
        for cp in first + passed:
            cp.wait_send()
        for cp in mine:
            cp.wait()

    scratch = [pltpu.SemaphoreType.DMA((7 * n_s,)), pltpu.SemaphoreType.DMA((7 * n_s,)),
               pltpu.SemaphoreType.DMA((n_s,))]
    return _launch_on_sequencer(name, collective_id, body, [s[0] for s in streams], out_shapes, scratch)


def _exchange_blocks(name, streams, collective_id):
    n_s = len(streams)

    def body(*refs):
        srcs = refs[:n_s]
        outs = refs[n_s:2 * n_s]
        send_sems, recv_sems, local_sems = refs[2 * n_s:]
        x, y, c = _my_pos()
        me = _dev_index((x, y, c))
        _handshake([(1 - x if k & 4 else x, 1 - y if k & 2 else y, 1 - c if k & 1 else c) for k in range(1, N_DEV)])

        def window(s, dev):
            return srcs[s] if streams[s][1] is None else streams[s][1](srcs[s], dev)

        mine = [pltpu.make_async_copy(window(s, me), outs[s].at[me], local_sems.at[s]) for s in range(n_s)]
        for cp in mine:
            cp.start()
        copies = []
        for k in (2, 4, 6, 3, 5, 7, 1):
            peer = (1 - x if k & 4 else x, 1 - y if k & 2 else y, 1 - c if k & 1 else c)
            for s in range(n_s):
                copies.append(pltpu.make_async_remote_copy(
                    src_ref=window(s, _dev_index(peer)), dst_ref=outs[s].at[me],
                    send_sem=send_sems.at[7 * s + k - 1], recv_sem=recv_sems.at[7 * s + k - 1],
                    device_id=peer, device_id_type=MESH))
        for cp in copies:
            cp.start()
        for cp in copies:
            cp.wait()
        for cp in mine:
            cp.wait()

    out_shapes = [jax.ShapeDtypeStruct((N_DEV,) + tuple(s[2]), s[0].dtype) for s in streams]
    scratch = [pltpu.SemaphoreType.DMA((7 * n_s,)), pltpu.SemaphoreType.DMA((7 * n_s,)),
               pltpu.SemaphoreType.DMA((n_s,))]
    return _launch_on_sequencer(name, collective_id, body, [s[0] for s in streams], out_shapes, scratch)


N_CHIPS = 4


def _sibling_exchange(name, streams, collective_id):
    n_s = len(streams)

    def body(*refs):
        srcs = refs[:n_s]
        outs = refs[n_s:2 * n_s]
        send_sems, recv_sems = refs[2 * n_s:]
        x, y, c = _my_pos()
        sibling = (x, y, 1 - c)
        _handshake([sibling])
        copies = [pltpu.make_async_remote_copy(
            src_ref=streams[s][1](srcs[s], 2 * chip + (1 - c)), dst_ref=outs[s].at[chip],
            send_sem=send_sems.at[N_CHIPS * s + chip], recv_sem=recv_sems.at[N_CHIPS * s + chip],
            device_id=sibling, device_id_type=MESH) for s in range(n_s) for chip in range(N_CHIPS)]
        for cp in copies:
            cp.start()
        for cp in copies:
            cp.wait()

    out_shapes = [jax.ShapeDtypeStruct((N_CHIPS,) + tuple(s[2]), s[0].dtype) for s in streams]
    scratch = [pltpu.SemaphoreType.DMA((N_CHIPS * n_s,)), pltpu.SemaphoreType.DMA((N_CHIPS * n_s,))]
    return _launch_on_sequencer(name, collective_id, body, [s[0] for s in streams], out_shapes, scratch)


def _pair_sum(name, array, window, from_sibling, after):
    block = from_sibling.shape[1:]
    zeros = (0,) * len(block)

    def body(src_ref, sib_ref, after_ref, o_ref, buf, sem):
        del after_ref
        mine = pltpu.make_async_copy(window(src_ref, 2 * pl.program_id(0) + lax.axis_index("c")), buf, sem)
        mine.start()
        mine.wait()
        o_ref[...] = (buf[...].astype(F32) + sib_ref[...].astype(F32)).astype(o_ref.dtype)

    spec = pl.BlockSpec((None,) + tuple(block), lambda k: (k,) + zeros)
    return pl.pallas_call(
        body, name=name, grid=(N_CHIPS,), out_shape=jax.ShapeDtypeStruct(from_sibling.shape, array.dtype),
        in_specs=[ANY, spec, ANY], out_specs=spec,
        scratch_shapes=[pltpu.VMEM(tuple(block), array.dtype), pltpu.SemaphoreType.DMA],
        compiler_params=_params("arbitrary"),
    )(array, from_sibling, after)


def _chip_exchange(name, pair_sums, collective_id):
    n_s = len(pair_sums)

    def body(*refs):
        srcs = refs[:n_s]
        outs = refs[n_s:2 * n_s]
        send_sems, recv_sems, local_sems = refs[2 * n_s:]
        x, y, c = _my_pos()
        my_chip = 2 * x + y
        chips = [(1 - x, y), (x, 1 - y), (1 - x, 1 - y)]
        _handshake([(*chip, c) for chip in chips])
        mine = [pltpu.make_async_copy(srcs[s].at[my_chip], outs[s].at[my_chip], local_sems.at[s]) for s in range(n_s)]
        copies = [pltpu.make_async_remote_copy(
            src_ref=srcs[s].at[2 * chip[0] + chip[1]], dst_ref=outs[s].at[my_chip],
            send_sem=send_sems.at[3 * s + j], recv_sem=recv_sems.at[3 * s + j],
            device_id=(*chip, c), device_id_type=MESH) for s in range(n_s) for j, chip in enumerate(chips)]
        for cp in mine + copies:
            cp.start()
        for cp in copies:
            cp.wait()
        for cp in mine:
            cp.wait()

    out_shapes = [jax.ShapeDtypeStruct(p.shape, p.dtype) for p in pair_sums]
    scratch = [pltpu.SemaphoreType.DMA((3 * n_s,)), pltpu.SemaphoreType.DMA((3 * n_s,)),
               pltpu.SemaphoreType.DMA((n_s,))]
    return _launch_on_sequencer(name, collective_id, body, list(pair_sums), out_shapes, scratch)


NN = (((1,), (0,)), ((), ()))
NT = (((1,), (1,)), ((), ()))
TN = (((0,), (0,)), ((), ()))


def _mm(name, a, b, *, dims, grid, a_spec, b_spec, out_shape, out_spec, nk=1,
        add=None, add_spec=None, add_scale=1.0, epilogue=None, extras=(), extra_specs=()):
    n_extra = len(extras)
    has_add = add is not None

    def body(*refs):
        a_ref, b_ref = refs[:2]
        pos = 2
        add_ref = None
        if has_add:
            add_ref = refs[pos]
            pos += 1
        extra_refs = refs[pos:pos + n_extra]
        o_ref = refs[pos + n_extra]
        acc_ref = refs[pos + n_extra + 1] if nk > 1 else None

        def finish(val):
            if has_add:
                val = val + add_scale * add_ref[...]
            if epilogue is not None:
                val = epilogue(val, *extra_refs)
            o_ref[...] = val.astype(o_ref.dtype)

        part = lax.dot_general(a_ref[...].astype(BF16), b_ref[...].astype(BF16), dims,
                               preferred_element_type=F32)
        if nk == 1:
            finish(part)
        else:
            k = pl.program_id(2)

            @pl.when(k == 0)
            def _():
                acc_ref[...] = part

            @pl.when(jnp.logical_and(k > 0, k < nk - 1))
            def _():
                acc_ref[...] += part

            @pl.when(k == nk - 1)
            def _():
                finish(acc_ref[...] + part)

    in_specs = [a_spec, b_spec] + ([add_spec] if has_add else []) + list(extra_specs)
    operands = [a, b] + ([add] if has_add else []) + list(extras)
    scratch = [pltpu.VMEM(out_spec.block_shape, F32)] if nk > 1 else []
    sem = ("parallel", "parallel") + (("arbitrary",) if nk > 1 else ())
    return pl.pallas_call(
        body, name=name, grid=grid, out_shape=out_shape,
        in_specs=in_specs, out_specs=out_spec, scratch_shapes=scratch,
        compiler_params=_params(*sem),
    )(*operands)


def _cast_bf16(name, a):
    rows, cols = a.shape
    tr = _tile(rows, 512)

    def body(a_ref, o_ref):
        o_ref[...] = a_ref[...].astype(BF16)

    return pl.pallas_call(
        body, name=name, grid=(rows // tr,),
        out_shape=jax.ShapeDtypeStruct(a.shape, BF16),
        in_specs=[pl.BlockSpec((tr, cols), lambda i: (i, 0))],
        out_specs=pl.BlockSpec((tr, cols), lambda i: (i, 0)),
        compiler_params=_params("parallel"),
    )(a)


def _rope_tables(seq):
    inv_freq = ROPE_THETA ** (-jnp.arange(0, HEAD_DIM, 2, dtype=F32) / HEAD_DIM)
    ang = jnp.arange(seq, dtype=F32)[:, None] * inv_freq[None, :]
    cos, sin = jnp.cos(ang), jnp.sin(ang)
    cos, sin = (jnp.concatenate([t, t, t, t], axis=-1) for t in (cos, sin))
    first_half = (jnp.arange(LANES) % HEAD_DIM < HEAD_DIM // 2)[None, :]
    return cos, jnp.where(first_half, -sin, 0.0), jnp.where(first_half, 0.0, sin)


def _rot(t, sin_a, sin_b):
    return pltpu.roll(t, LANES - HEAD_DIM // 2, 1) * sin_a + pltpu.roll(t, HEAD_DIM // 2, 1) * sin_b


def _rope(t, cos, sin_a, sin_b):
    return t * cos + _rot(t, sin_a, sin_b)


def _rope_transposed(dy, cos, sin_a, sin_b):
    return dy * cos - _rot(dy, sin_a, sin_b)


def _silu_parts(z):
    sig = jax.nn.sigmoid(z)
    return z * sig, sig * (1.0 + z * (1.0 - sig))


def _layer_norm_stats(r):
    mu = jnp.mean(r, axis=-1, keepdims=True)
    d = r - mu
    var = jnp.mean(d * d, axis=-1, keepdims=True)
    rstd = lax.rsqrt(var + LN_EPS)
    return d * rstd, rstd


def _layer_norm_backward(dout, xhat, rstd, gain):
    dxh = dout * gain
    m1 = jnp.mean(dxh, axis=-1, keepdims=True)
    m2 = jnp.mean(dxh * xhat, axis=-1, keepdims=True)
    return rstd * (dxh - m1 - xhat * m2)


def _col_sum(v):
    return jnp.sum(v, axis=0, keepdims=True)


def _pool_forward(xb, w_in, wg, scale):
    seq, dm = xb.shape
    n_g = len(POOL_WINDOWS)
    gd = dm // n_g
    tile = _tile(seq, 1024)
    halo_blocks = tile // POOL_HALO

    def body(x_ref, xp_ref, wu_ref, wz_ref, wg_ref, sc_ref, y_ref, p_ref, mx_ref, z_ref):
        i, g = pl.program_id(0), pl.program_id(1)
        u = jnp.dot(x_ref[...], wu_ref[...], preferred_element_type=F32)
        z = jnp.dot(x_ref[...], wz_ref[...], preferred_element_type=F32)
        prev = jnp.where(i > 0, jnp.dot(xp_ref[...], wu_ref[...], preferred_element_type=F32), 0.0)
        s = jnp.concatenate([prev, u], axis=0)
        sums, sh = [], 1
        while sh < POOL_WINDOWS[-1]:
            s = s + pltpu.roll(s, sh, 0)
            sums.append(s)
            sh *= 2
        win = sums[-1]
        for k in range(n_g - 2, -1, -1):
            win = jnp.where(g == k, sums[k], win)
        row = i * tile + lax.broadcasted_iota(jnp.int32, (tile, 1), 0)
        window = jnp.left_shift(2, g).astype(F32)
        p = win[POOL_HALO:, :] * (1.0 / jnp.minimum((row + 1).astype(F32), window)) - u
        pb = p.astype(BF16)
        mx = jnp.dot(pb, wg_ref[...], preferred_element_type=F32)
        y_ref[...] = (mx * sc_ref[...] * (z * jax.nn.sigmoid(z))).astype(BF16)
        p_ref[...] = pb
        mx_ref[...] = mx
        z_ref[...] = z

    out_spec = pl.BlockSpec((tile, gd), lambda i, g: (i, g))
    return pl.pallas_call(
        body, name="pool_fwd", grid=(seq // tile, n_g),
        out_shape=(jax.ShapeDtypeStruct((seq, dm), BF16), jax.ShapeDtypeStruct((seq, dm), BF16),
                   jax.ShapeDtypeStruct((seq, dm), F32), jax.ShapeDtypeStruct((seq, dm), F32)),
        in_specs=[pl.BlockSpec((tile, dm), lambda i, g: (i, 0)),
                  pl.BlockSpec((POOL_HALO, dm), lambda i, g: (jnp.maximum(i * halo_blocks - 1, 0), 0)),
                  pl.BlockSpec((dm, gd), lambda i, g: (0, g)),
                  pl.BlockSpec((dm, gd), lambda i, g: (0, n_g + g)),
                  pl.BlockSpec((None, gd, gd), lambda i, g: (g, 0, 0)),
                  pl.BlockSpec((1, gd), lambda i, g: (0, g))],
        out_specs=(out_spec, out_spec, out_spec, out_spec),
        compiler_params=_params("parallel", "parallel"),
    )(xb, xb, w_in, w_in, wg, scale)


def _pool_mid_backward(dy, mx, z, p, wg, scale, after):
    seq, dm = dy.shape
    gd = dm // len(POOL_WINDOWS)
    tile = _tile(seq, 256)
    n_i = seq // tile

    def body(dy_ref, mx_ref, z_ref, p_ref, wg_ref, sc_ref, after_ref, dh_ref, dwg_ref, st_ref, dwg_acc, carry):
        del after_ref
        i = pl.program_id(0)
        ti = n_i - 1 - i

        @pl.when(i == 0)
        def _():
            dwg_acc[...] = jnp.zeros_like(dwg_acc)
            carry[...] = jnp.zeros_like(carry)
            st_ref[...] = jnp.zeros_like(st_ref)

        row = ti * tile + lax.broadcasted_iota(jnp.int32, (tile, 1), 0)
        count = (row + 1).astype(F32)
        for g, w in enumerate(POOL_WINDOWS):
            cs = slice(g * gd, (g + 1) * gd)
            z = z_ref[:, cs]
            sz, dsz = _silu_parts(z)
            dyg = dy_ref[:, cs]
            mxg = mx_ref[:, cs]
            sc = sc_ref[:, cs]
            t1 = dyg * sz
            st_ref[0:1, cs] += _col_sum(t1 * mxg)
            dh_ref[1, :, cs] = (dyg * (mxg * sc) * dsz).astype(BF16)
            dmx = (t1 * sc).astype(BF16)
            dwg_acc[g] += lax.dot_general(p_ref[:, cs], dmx, TN, preferred_element_type=F32)
            dp = lax.dot_general(dmx, wg_ref[g], NT, preferred_element_type=F32)
            e = dp * (1.0 / jnp.minimum(count, float(w)))
            s = jnp.concatenate([e, carry[:, cs]], axis=0)
            n = tile + POOL_HALO
            sh = 1
            while sh < w:
                s = s + pltpu.roll(s, n - sh, 0)
                sh *= 2
            dh_ref[0, :, cs] = (s[:tile, :] - dp).astype(BF16)
            carry[:, cs] = e[:POOL_HALO, :]

        @pl.when(i == n_i - 1)
        def _():
            dwg_ref[...] = dwg_acc[...].astype(BF16)

    row_spec = pl.BlockSpec((tile, dm), lambda i: (n_i - 1 - i, 0))
    return pl.pallas_call(
        body, name="pool_mid_bwd", grid=(n_i,),
        out_shape=(jax.ShapeDtypeStruct((2, seq, dm), BF16), jax.ShapeDtypeStruct(wg.shape, BF16),
                   jax.ShapeDtypeStruct((STAT_ROWS, dm), F32)),
        in_specs=[row_spec, row_spec, row_spec, row_spec,
                  pl.BlockSpec(wg.shape, lambda i: (0, 0, 0)),
                  pl.BlockSpec((1, dm), lambda i: (0, 0)), ANY],
        out_specs=(pl.BlockSpec((2, tile, dm), lambda i: (0, n_i - 1 - i, 0)),
                   pl.BlockSpec(wg.shape, lambda i: (0, 0, 0)),
                   pl.BlockSpec((STAT_ROWS, dm), lambda i: (0, 0))),
        scratch_shapes=[pltpu.VMEM(wg.shape, F32), pltpu.VMEM((POOL_HALO, dm), F32)],
        compiler_params=_params("arbitrary"),
    )(dy, mx, z, p, wg, scale, after)


def _out_proj_norm(y, w, x, gain, bias):
    seq, dm = x.shape
    tile = _tile(seq, 512)

    def body(y_ref, w_ref, x_ref, g_ref, b_ref, xhat_ref, rstd_ref, xb_ref):
        o = jnp.dot(y_ref[...], w_ref[...], preferred_element_type=F32)
        xhat, rstd = _layer_norm_stats(ALPHA * x_ref[...] + o)
        xhat_ref[...] = xhat
        rstd_ref[...] = rstd
        xb_ref[...] = (xhat * g_ref[...] + b_ref[...]).astype(BF16)

    row_spec = pl.BlockSpec((tile, dm), lambda i: (i, 0))
    vec_spec = pl.BlockSpec((1, dm), lambda i: (0, 0))
    return pl.pallas_call(
        body, name="out_proj_norm_a", grid=(seq // tile,),
        out_shape=(jax.ShapeDtypeStruct((seq, dm), F32), jax.ShapeDtypeStruct((seq, 1), F32),
                   jax.ShapeDtypeStruct((seq, dm), BF16)),
        in_specs=[row_spec, pl.BlockSpec(w.shape, lambda i: (0, 0), pipeline_mode=pl.Buffered(1)), row_spec, vec_spec,
                  vec_spec],
        out_specs=(row_spec, pl.BlockSpec((tile, 1), lambda i: (i, 0)), row_spec),
        compiler_params=_params("parallel"),
    )(y, w, x, gain, bias)


def _kv_proj(xb, wkv, tables):
    seq, dm = xb.shape
    kvw = wkv.shape[1] // 2
    n_kv = kvw // HEAD_DIM
    tile = _tile(seq, 512)

    def body(x_ref, w_ref, cos_ref, sa_ref, sb_ref, kd_ref, vd_ref, kt_ref, vt_ref):
        kv = jnp.dot(x_ref[...], w_ref[...], preferred_element_type=F32)
        low = lax.broadcasted_iota(jnp.int32, (1, LANES), 1) < HEAD_DIM
        cos, sa, sb = cos_ref[...], sa_ref[...], sb_ref[...]

        def put(pair, h, nat_ref, t_ref):
            swapped = pltpu.roll(pair, HEAD_DIM, 1)
            for head, dup in ((h, jnp.where(low, pair, swapped)), (h + 1, jnp.where(low, swapped, pair))):
                nat_ref[head] = dup.astype(BF16)
                t_ref[head] = dup.T.astype(BF16)

        for j in range(kvw // LANES):
            put(_rope(kv[:, j * LANES:(j + 1) * LANES], cos, sa, sb), 2 * j, kd_ref, kt_ref)
            put(kv[:, kvw + j * LANES:kvw + (j + 1) * LANES], 2 * j, vd_ref, vt_ref)

    tab_spec = pl.BlockSpec((tile, LANES), lambda i: (i, 0))
    dup_spec = pl.BlockSpec((n_kv, tile, LANES), lambda i: (0, i, 0))
    dup_shape = jax.ShapeDtypeStruct((n_kv, seq, LANES), BF16)
    t_spec = pl.BlockSpec((n_kv, LANES, tile), lambda i: (0, 0, i))
    t_shape = jax.ShapeDtypeStruct((n_kv, LANES, seq), BF16)
    return pl.pallas_call(
        body, name="kv_proj", grid=(seq // tile,),
        out_shape=(dup_shape, dup_shape, t_shape, t_shape),
        in_specs=[pl.BlockSpec((tile, dm), lambda i: (i, 0)), pl.BlockSpec(wkv.shape, lambda i: (0, 0)),
                  tab_spec, tab_spec, tab_spec],
        out_specs=(dup_spec, dup_spec, t_spec, t_spec),
        compiler_params=_params("parallel"),
    )(xb, wkv, *tables)


def _head_queries(q_ref, low):
    parts = []
    for j in range(GQA_GROUP // 2):
        q2 = q_ref[:, j * LANES:(j + 1) * LANES]
        parts += [jnp.where(low, q2, 0), jnp.where(low, 0, q2)]
    return parts


def _probs_transposed(n, kh, kcat, q_all, sink_ref):
    st = lax.dot_general(kcat, q_all, NT, preferred_element_type=F32)
    key = lax.broadcasted_iota(jnp.int32, (2 * ATTN_BLOCK, ATTN_BLOCK), 0)
    qry = lax.broadcasted_iota(jnp.int32, (2 * ATTN_BLOCK, ATTN_BLOCK), 1)
    valid = (key > qry) & (key <= qry + ATTN_BLOCK) & ((key >= ATTN_BLOCK) | (n > 0))
    st = st + jnp.tile(jnp.where(valid, 0.0, NEG_INF), (1, GQA_GROUP))
    sink = jnp.concatenate([jnp.full((1, ATTN_BLOCK), sink_ref[0, kh * GQA_GROUP + h], F32)
                            for h in range(GQA_GROUP)], axis=1)
    m = jnp.maximum(jnp.max(st, axis=0, keepdims=True), sink)
    e = jnp.exp(st - m)
    e_sink = jnp.exp(sink - m)
    inv = 1.0 / (jnp.sum(e, axis=0, keepdims=True) + e_sink)
    return e * inv, e_sink * inv


def _attn_specs(n_width):
    q_spec = pl.BlockSpec((ATTN_BLOCK, n_width), lambda kh, n: (n, kh))
    cur = pl.BlockSpec((None, ATTN_BLOCK, LANES), lambda kh, n: (kh, n, 0))
    prev = pl.BlockSpec((None, ATTN_BLOCK, LANES), lambda kh, n: (kh, jnp.maximum(n - 1, 0), 0))
    cur_t = pl.BlockSpec((None, LANES, ATTN_BLOCK), lambda kh, n: (kh, 0, n))
    prev_t = pl.BlockSpec((None, LANES, ATTN_BLOCK), lambda kh, n: (kh, 0, jnp.maximum(n - 1, 0)))
    return q_spec, cur, prev, cur_t, prev_t


def _pair_product_transposed(mat_t, rhs, j, low_rows):
    a = rhs[:, 2 * j * ATTN_BLOCK:(2 * j + 1) * ATTN_BLOCK]
    b = rhs[:, (2 * j + 1) * ATTN_BLOCK:(2 * j + 2) * ATTN_BLOCK]
    out_t = (jnp.dot(jnp.where(low_rows, mat_t, 0), a, preferred_element_type=F32)
             + jnp.dot(jnp.where(low_rows, 0, mat_t), b, preferred_element_type=F32))
    return out_t.T


def _attn_forward(qs, kd, vt, zb, sinks):
    seq, dm = qs.shape
    n_kv = kd.shape[0]
    gw = GQA_GROUP * HEAD_DIM

    def body(q_ref, kp_ref, kc_ref, vtp_ref, vtc_ref, z_ref, sink_ref, att_ref, yb_ref):
        kh, n = pl.program_id(0), pl.program_id(1)
        low = lax.broadcasted_iota(jnp.int32, (1, LANES), 1) < HEAD_DIM
        low_rows = lax.broadcasted_iota(jnp.int32, (LANES, 1), 0) < HEAD_DIM
        kcat = jnp.concatenate([kp_ref[...], kc_ref[...]], axis=0)
        vt = jnp.concatenate([vtp_ref[...], vtc_ref[...]], axis=1)
        q_all = jnp.concatenate(_head_queries(q_ref, low), axis=0)
        probs_t, _ = _probs_transposed(n, kh, kcat, q_all, sink_ref)
        pt = probs_t.astype(BF16)
        for j in range(GQA_GROUP // 2):
            cs = slice(j * LANES, (j + 1) * LANES)
            o2 = _pair_product_transposed(vt, pt, j, low_rows)
            att_ref[:, cs] = o2
            z = z_ref[:, cs]
            yb_ref[:, cs] = (o2 * (z * jax.nn.sigmoid(z))).astype(BF16)

    q_spec, cur, prev, cur_t, prev_t = _attn_specs(gw)
    return pl.pallas_call(
        body, name="attn_fwd", grid=(n_kv, seq // ATTN_BLOCK),
        out_shape=(jax.ShapeDtypeStruct((seq, dm), F32), jax.ShapeDtypeStruct((seq, dm), BF16)),
        in_specs=[q_spec, prev, cur, prev_t, cur_t, q_spec, pl.BlockSpec(memory_space=pltpu.SMEM)],
        out_specs=(q_spec, q_spec),
        compiler_params=_params("parallel", "parallel"),
    )(qs, kd, kd, vt, vt, zb, sinks)


def _attn_backward(qs, kd, vd, kt, zb, att, dyb, sinks, tables, after):
    seq, dm = qs.shape
    n_kv = kd.shape[0]
    gw = GQA_GROUP * HEAD_DIM
    n_blocks = seq // ATTN_BLOCK

    def body(q_ref, kp_ref, kc_ref, vp_ref, vc_ref, ktp_ref, ktc_ref, z_ref, att_ref, dyb_ref, sink_ref,
             cos_ref, sa_ref, sb_ref, after_ref, dh_ref, dk_ref, dv_ref, ds_ref):
        del after_ref
        kh, n = pl.program_id(0), pl.program_id(1)

        @pl.when(n == 0)
        def _():
            dk_ref[...] = jnp.zeros_like(dk_ref)
            dv_ref[...] = jnp.zeros_like(dv_ref)

        @pl.when(jnp.logical_and(n == 0, kh == 0))
        def _():
            ds_ref[...] = jnp.zeros_like(ds_ref)

        low = lax.broadcasted_iota(jnp.int32, (1, LANES), 1) < HEAD_DIM
        low_rows = lax.broadcasted_iota(jnp.int32, (LANES, 1), 0) < HEAD_DIM
        head_lane = lax.broadcasted_iota(jnp.int32, (1, LANES), 1)
        kcat = jnp.concatenate([kp_ref[...], kc_ref[...]], axis=0)
        vcat = jnp.concatenate([vp_ref[...], vc_ref[...]], axis=0)
        kt = jnp.concatenate([ktp_ref[...], ktc_ref[...]], axis=1)
        cos, sa, sb = cos_ref[...], sa_ref[...], sb_ref[...]
        q_parts = _head_queries(q_ref, low)
        q_all = jnp.concatenate(q_parts, axis=0)
        d_parts = []
        for j in range(GQA_GROUP // 2):
            cs = slice(j * LANES, (j + 1) * LANES)
            sz, dsz = _silu_parts(z_ref[:, cs])
            dy2 = dyb_ref[:, cs]
            dh_ref[1, :, cs] = (dy2 * att_ref[:, cs] * dsz).astype(BF16)
            datt = (dy2 * sz).astype(BF16)
            d_parts += [jnp.where(low, datt, 0), jnp.where(low, 0, datt)]
        d_all = jnp.concatenate(d_parts, axis=0)
        probs_t, sink_p = _probs_transposed(n, kh, kcat, q_all, sink_ref)
        dprobs_t = lax.dot_general(vcat, d_all, NT, preferred_element_type=F32)
        row_dot = jnp.sum(probs_t * dprobs_t, axis=0, keepdims=True)
        ds_t = (probs_t * (dprobs_t - row_dot)).astype(BF16)
        dk = jnp.dot(ds_t, q_all, preferred_element_type=F32)
        dv = jnp.dot(probs_t.astype(BF16), d_all, preferred_element_type=F32)
        for j in range(GQA_GROUP // 2):
            dq2 = _pair_product_transposed(kt, ds_t, j, low_rows)
            dh_ref[0, :, j * LANES:(j + 1) * LANES] = (_rope_transposed(dq2, cos, sa, sb) * 0.125).astype(BF16)
        sink_dot = sink_p * row_dot
        dsink = jnp.zeros((1, LANES), F32)
        for h in range(GQA_GROUP):
            part = jnp.sum(sink_dot[:, h * ATTN_BLOCK:(h + 1) * ATTN_BLOCK], axis=1, keepdims=True)
            dsink = dsink - jnp.where(head_lane == kh * GQA_GROUP + h, part, 0.0)
        ds_ref[0:1, :] += dsink

        @pl.when(n == 0)
        def _():
            dk_ref[pl.ds(0, ATTN_BLOCK), :] += dk[ATTN_BLOCK:, :]
            dv_ref[pl.ds(0, ATTN_BLOCK), :] += dv[ATTN_BLOCK:, :]

        @pl.when(n > 0)
        def _():
            start = pl.multiple_of((n - 1) * ATTN_BLOCK, ATTN_BLOCK)
            dk_ref[pl.ds(start, 2 * ATTN_BLOCK), :] += dk
            dv_ref[pl.ds(start, 2 * ATTN_BLOCK), :] += dv

    q_spec, cur, prev, cur_t, prev_t = _attn_specs(gw)
    tab_spec = pl.BlockSpec((ATTN_BLOCK, LANES), lambda kh, n: (n, 0))
    acc_spec = pl.BlockSpec((None, seq, LANES), lambda kh, n: (kh, 0, 0))
    acc_shape = jax.ShapeDtypeStruct((n_kv, seq, LANES), F32)
    return pl.pallas_call(
        body, name="attn_bwd", grid=(n_kv, n_blocks),
        out_shape=(jax.ShapeDtypeStruct((2, seq, dm), BF16), acc_shape, acc_shape,
                   jax.ShapeDtypeStruct((STAT_ROWS, LANES), F32)),
        in_specs=[q_spec, prev, cur, prev, cur, prev_t, cur_t, q_spec, q_spec, q_spec,
                  pl.BlockSpec(memory_space=pltpu.SMEM), tab_spec, tab_spec, tab_spec, ANY],
        out_specs=(pl.BlockSpec((2, ATTN_BLOCK, gw), lambda kh, n: (0, n, kh)), acc_spec, acc_spec,
                   pl.BlockSpec((STAT_ROWS, LANES), lambda kh, n: (0, 0))),
        compiler_params=_params("arbitrary", "arbitrary"),
    )(qs, kd, kd, vd, vd, kt, kt, zb, att, dyb, sinks, *tables, after)


def _kv_grad_fold(dk, dv, tables):
    n_kv, seq, _ = dk.shape
    kvw = n_kv * HEAD_DIM
    tile = _tile(seq, 512)

    def body(dk_ref, dv_ref, cos_ref, sa_ref, sb_ref, o_ref):
        low = lax.broadcasted_iota(jnp.int32, (1, LANES), 1) < HEAD_DIM
        cos, sa, sb = cos_ref[...], sa_ref[...], sb_ref[...]

        def folded(ref, h):
            t = ref[h]
            return t + pltpu.roll(t, HEAD_DIM, 1)

        for j in range(n_kv // 2):
            ka = _rope_transposed(folded(dk_ref, 2 * j), cos, sa, sb)
            kb = _rope_transposed(folded(dk_ref, 2 * j + 1), cos, sa, sb)
            o_ref[:, j * LANES:(j + 1) * LANES] = jnp.where(low, ka, kb).astype(BF16)
            o_ref[:, kvw + j * LANES:kvw + (j + 1) * LANES] = jnp.where(
                low, folded(dv_ref, 2 * j), folded(dv_ref, 2 * j + 1)).astype(BF16)

    tab_spec = pl.BlockSpec((tile, LANES), lambda i: (i, 0))
    in_spec = pl.BlockSpec((n_kv, tile, LANES), lambda i: (0, i, 0))
    return pl.pallas_call(
        body, name="kv_grad_fold", grid=(seq // tile,),
        out_shape=jax.ShapeDtypeStruct((seq, 2 * kvw), BF16),
        in_specs=[in_spec, in_spec, tab_spec, tab_spec, tab_spec],
        out_specs=pl.BlockSpec((tile, 2 * kvw), lambda i: (i, 0)),
        compiler_params=_params("parallel"),
    )(dk, dv, *tables)


def _out_proj_norm_loss(yb, w, xhat1, gain0, bias0, gain1, bias1, target):
    seq, dm = xhat1.shape
    tile = _tile(seq, 512)

    def body(y_ref, w_ref, xh1_ref, g0_ref, b0_ref, g1_ref, b1_ref, t_ref, dr_ref, drb_ref, st_ref):
        i = pl.program_id(0)

        @pl.when(i == 0)
        def _():
            st_ref[...] = jnp.zeros_like(st_ref)

        ob = jnp.dot(y_ref[...], w_ref[...], preferred_element_type=F32)
        x1 = xh1_ref[...] * g0_ref[...] + b0_ref[...]
        xhat, rstd = _layer_norm_stats(ALPHA * x1 + ob)
        err = xhat * g1_ref[...] + b1_ref[...] - t_ref[...]
        dout = err * (1.0 / dm)
        dr = _layer_norm_backward(dout, xhat, rstd, g1_ref[...])
        dr_ref[...] = dr
        drb_ref[...] = dr.astype(BF16)
        st_ref[0:1, :] += _col_sum(dout * xhat)
        st_ref[1:2, :] += _col_sum(dout)
        st_ref[2:3, :] += _col_sum(err * err)

    row_spec = pl.BlockSpec((tile, dm), lambda i: (i, 0))
    vec_spec = pl.BlockSpec((1, dm), lambda i: (0, 0))
    return pl.pallas_call(
        body, name="out_proj_norm_loss_b", grid=(seq // tile,),
        out_shape=(jax.ShapeDtypeStruct((seq, dm), F32), jax.ShapeDtypeStruct((seq, dm), BF16),
                   jax.ShapeDtypeStruct((STAT_ROWS, dm), F32)),
        in_specs=[row_spec, pl.BlockSpec(w.shape, lambda i: (0, 0), pipeline_mode=pl.Buffered(1)), row_spec, vec_spec,
                  vec_spec, vec_spec,
                  vec_spec, row_spec],
        out_specs=(row_spec, row_spec, pl.BlockSpec((STAT_ROWS, dm), lambda i: (0, 0))),
        compiler_params=_params("arbitrary"),
    )(yb, w, xhat1, gain0, bias0, gain1, bias1, target)


def _stream_grad_norm_backward(dhq, wqg, dkv, wkv, dr2, xhat1, rstd1, gain0, after):
    seq, dm = dr2.shape
    tile = _tile(seq, 256)

    def body(dh_ref, wqg_ref, dkv_ref, wkv_ref, dr2_ref, xh_ref, rstd_ref, g_ref, after_ref, dr_ref, drb_ref, st_ref):
        del after_ref

        @pl.when(pl.program_id(0) == 0)
        def _():
            st_ref[...] = jnp.zeros_like(st_ref)

        dx1 = (lax.dot_general(dh_ref[0], wqg_ref[:, :dm], NT, preferred_element_type=F32)
               + lax.dot_general(dh_ref[1], wqg_ref[:, dm:], NT, preferred_element_type=F32)
               + lax.dot_general(dkv_ref[...], wkv_ref[...], NT, preferred_element_type=F32)
               + ALPHA * dr2_ref[...])
        xhat = xh_ref[...]
        dr = _layer_norm_backward(dx1, xhat, rstd_ref[...], g_ref[...])
        dr_ref[...] = dr
        drb_ref[...] = dr.astype(BF16)
        st_ref[0:1, :] += _col_sum(dx1 * xhat)
        st_ref[1:2, :] += _col_sum(dx1)

    row_spec = pl.BlockSpec((tile, dm), lambda i: (i, 0))
    resident = pl.Buffered(1)
    return pl.pallas_call(
        body, name="stream_grad_norm_bwd", grid=(seq // tile,),
        out_shape=(jax.ShapeDtypeStruct((seq, dm), F32), jax.ShapeDtypeStruct((seq, dm), BF16),
                   jax.ShapeDtypeStruct((STAT_ROWS, dm), F32)),
        in_specs=[pl.BlockSpec((2, tile, dm), lambda i: (0, i, 0)),
                  pl.BlockSpec(wqg.shape, lambda i: (0, 0), pipeline_mode=resident),
                  pl.BlockSpec((tile, dkv.shape[1]), lambda i: (i, 0)),
                  pl.BlockSpec(wkv.shape, lambda i: (0, 0), pipeline_mode=resident),
                  row_spec, row_spec, pl.BlockSpec((tile, 1), lambda i: (i, 0)),
                  pl.BlockSpec((1, dm), lambda i: (0, 0)), ANY],
        out_specs=(row_spec, row_spec, pl.BlockSpec((STAT_ROWS, dm), lambda i: (0, 0))),
        compiler_params=_params("arbitrary"),
    )(dhq, wqg, dkv, wkv, dr2, xhat1, rstd1, gain0, after)


def _adamw_math(w, g, m, v):
    m = ADAM_B1 * m + (1.0 - ADAM_B1) * g
    v = ADAM_B2 * v + (1.0 - ADAM_B2) * (g * g)
    m_hat = m / (1.0 - ADAM_B1 ** ADAM_STEP)
    v_hat = v / (1.0 - ADAM_B2 ** ADAM_STEP)
    delta = -ADAM_LR * (m_hat / (jnp.sqrt(v_hat) + ADAM_EPS) + ADAM_WD * w)
    return delta, m, v


def _sum_devices(ref):
    total = ref[0].astype(F32)
    for d in range(1, ref.shape[0]):
        total = total + ref[d].astype(F32)
    return total


def _adamw_shard(name, parts, w, m, v, after):
    rows, cols = w.shape
    n_parts = len(parts)
    part_rows = rows // n_parts
    tr = _tile(part_rows, max(8, (1 << 18) // cols)) if part_rows >= 8 else part_rows
    per_part = part_rows // tr

    def body(*refs):
        p_refs = refs[:n_parts]
        w_ref, m_ref, v_ref, _, g_out, d_out, m_out, v_out = refs[n_parts:]
        g = _sum_devices(p_refs[0])
        for k in range(1, n_parts):
            g = jnp.where(pl.program_id(0) >= k * per_part, _sum_devices(p_refs[k]), g)
        delta, m_new, v_new = _adamw_math(w_ref[...], g, m_ref[...], v_ref[...])
        g_out[...] = g
        d_out[...] = delta
        m_out[...] = m_new
        v_out[...] = v_new

    def part_spec(k):
        return pl.BlockSpec((parts[k].shape[0], tr, cols),
                            lambda i: (0, jnp.clip(i - k * per_part, 0, per_part - 1), 0))

    spec = pl.BlockSpec((tr, cols), lambda i: (i, 0))
    shape = jax.ShapeDtypeStruct((rows, cols), F32)
    return pl.pallas_call(
        body, name=name, grid=(rows // tr,),
        out_shape=(shape, shape, shape, shape),
        in_specs=[part_spec(k) for k in range(n_parts)] + [spec, spec, spec, ANY],
        out_specs=(spec, spec, spec, spec),
        compiler_params=_params("arbitrary"),
    )(*parts, w, m, v, after)


def _adamw_replicated(stats_b, stats_a, sink_parts, ln_g, ln_b, sinks, m_ln_g, m_ln_b, m_sinks, v_ln_g, v_ln_b,
                      v_sinks, after):
    n_q = sinks.shape[1]
    dm = ln_g.shape[1]

    def body(sb_ref, sa_ref, sk_ref, g_ref, b_ref, s_ref, mg_ref, mb_ref, ms_ref, vg_ref, vb_ref, vs_ref, after_ref,
             *outs):
        del after_ref
        layer_sums = (_sum_devices(sa_ref), _sum_devices(sb_ref))
        outs[12][...] = jnp.sum(layer_sums[1][2:3, :], axis=1, keepdims=True) * (0.5 / dm)
        for which, (w_ref, m_ref, v_ref) in enumerate(((g_ref, mg_ref, vg_ref), (b_ref, mb_ref, vb_ref))):
            for layer in range(DEPTH):
                row = slice(layer, layer + 1)
                g = layer_sums[layer][which:which + 1, :]
                res = (g,) + _adamw_math(w_ref[row, :], g, m_ref[row, :], v_ref[row, :])
                for o_ref, val in zip(outs[4 * which:4 * which + 4], res):
                    o_ref[row, :] = val
        g = _sum_devices(sk_ref)[0:1, 0:n_q]
        res = (g,) + _adamw_math(s_ref[...], g, ms_ref[...], vs_ref[...])
        for o_ref, val in zip(outs[8:12], res):
            o_ref[...] = val

    vmem = pl.BlockSpec(memory_space=pltpu.VMEM)
    shapes = [jax.ShapeDtypeStruct(a.shape, F32) for a in (ln_g, ln_b, sinks) for _ in range(4)]
    shapes.append(jax.ShapeDtypeStruct((1, 1), F32))
    return pl.pallas_call(
        body, name="adamw_replicated", out_shape=tuple(shapes),
        in_specs=[vmem] * 12 + [ANY], out_specs=tuple([vmem] * 13),
    )(stats_b, stats_a, sink_parts, ln_g, ln_b, sinks, m_ln_g, m_ln_b, m_sinks, v_ln_g, v_ln_b, v_sinks, after)


def kernel(x, ln_g, ln_b, a_w_in, a_w_group, a_scale, a_w_out, b_w_k, b_w_v, b_w_qg, b_sinks, b_w_out, loss_target, m_ln_g, m_ln_b, m_a_w_in, m_a_w_group, m_a_scale, m_a_w_out, m_b_w_k, m_b_w_v, m_b_w_qg, m_b_sinks, m_b_w_out, v_ln_g, v_ln_b, v_a_w_in, v_a_w_group, v_a_scale, v_a_w_out, v_b_w_k, v_b_w_v, v_b_w_qg, v_b_sinks, v_b_w_out):
    _, seq, dm = x.shape
    n_groups = len(POOL_WINDOWS)
    gd = dm // n_groups
    kvw = b_w_k.shape[1]
    cb = 2 * dm // N_DEV
    rb = dm // N_DEV
    gb = gd // N_DEV

    x2 = x.reshape(seq, dm)
    target = loss_target.reshape(seq, dm)
    w_in_s = a_w_in.reshape(dm, cb)
    w_g_s = a_w_group.reshape(n_groups, gb, gd)
    w_out_s = a_w_out.reshape(rb, dm)
    w_qg_s = b_w_qg.reshape(dm, cb)
    w_outb_s = b_w_out.reshape(rb, dm)

    def cols(ref, dev):
        return ref.at[:, pl.ds(pl.multiple_of(dev * cb, LANES), cb)]

    def rows(ref, dev):
        return ref.at[pl.ds(pl.multiple_of(dev * rb, 8), rb), :]

    def group_rows(ref, dev):
        return ref.at[:, pl.ds(pl.multiple_of(dev * gb, 8), gb), :]

    def k_rows(ref, dev):
        return ref.at[pl.ds(pl.multiple_of(dev * rb, 8), rb), pl.ds(0, kvw)]

    def v_rows(ref, dev):
        return ref.at[pl.ds(pl.multiple_of(dev * rb, 8), rb), pl.ds(kvw, kvw)]

    def scale_cols(ref, dev):
        return ref.at[:, pl.ds(pl.multiple_of(dev * rb, LANES), rb)]

    bf = lambda a: a.astype(BF16)
    wide, square = jax.ShapeDtypeStruct((dm, 2 * dm), BF16), jax.ShapeDtypeStruct((dm, dm), BF16)
    w_g, scale, w_in = _gather_weights(
        "gather_a_in", 0, [(bf(w_g_s), 0, group_rows), (a_scale, 1, scale_cols), (bf(w_in_s), 2, cols)],
        [jax.ShapeDtypeStruct((n_groups, gd, gd), BF16), jax.ShapeDtypeStruct((1, dm), F32), wide])
    (w_out,) = _gather_weights("gather_a_out", 1, [(bf(w_out_s), 0, rows)], [square])
    w_kv, w_qg = _gather_weights(
        "gather_b_in", 2, [(bf(b_w_k), 0, k_rows), (bf(b_w_v), 0, v_rows), (bf(w_qg_s), 1, cols)],
        [jax.ShapeDtypeStruct((dm, 2 * kvw), BF16), wide])
    (w_outb,) = _gather_weights("gather_b_out", 3, [(bf(w_outb_s), 0, rows)], [square])

    tables = _rope_tables(seq)
    bm = _tile(seq, 1024)
    bn = _tile(dm, 1024)
    g0, g1, b0, b1 = ln_g[0:1], ln_g[1:2], ln_b[0:1], ln_b[1:2]

    xb = _cast_bf16("cast_x", x2)
    y, pooled, mixed, z_a = _pool_forward(xb, w_in, w_g, scale)
    xhat1, rstd1, x1b = _out_proj_norm(y, w_out, x2, g0, b0)

    kd, vd, kt, vt = _kv_proj(x1b, w_kv, tables)
    bmq = bm
    tab_spec = pl.BlockSpec((bmq, LANES), lambda i, j: (i, 0))

    def rope_scale(val, cos_ref, sa_ref, sb_ref):
        cos, sa, sb = cos_ref[...], sa_ref[...], sb_ref[...]
        return jnp.concatenate([_rope(val[:, j * LANES:(j + 1) * LANES], cos, sa, sb) * 0.125
                                for j in range(val.shape[1] // LANES)], axis=1)

    qs = _mm("b_q_proj", x1b, w_qg, dims=NN, grid=(seq // bmq, dm // bn),
             a_spec=pl.BlockSpec((bmq, dm), lambda i, j: (i, 0)), b_spec=pl.BlockSpec((dm, bn), lambda i, j: (0, j)),
             out_shape=jax.ShapeDtypeStruct((seq, dm), BF16), out_spec=pl.BlockSpec((bmq, bn), lambda i, j: (i, j)),
             epilogue=rope_scale, extras=tables, extra_specs=(tab_spec,) * 3)
    zb = _mm("b_gate_proj", x1b, w_qg, dims=NN, grid=(seq // bm, dm // bn),
             a_spec=pl.BlockSpec((bm, dm), lambda i, j: (i, 0)),
             b_spec=pl.BlockSpec((dm, bn), lambda i, j: (0, j + dm // bn)),
             out_shape=jax.ShapeDtypeStruct((seq, dm), F32), out_spec=pl.BlockSpec((bm, bn), lambda i, j: (i, j)))
    att, yb = _attn_forward(qs, kd, vt, zb, b_sinks)
    dr2, dr2b, stats_b = _out_proj_norm_loss(yb, w_outb, xhat1, g0, b0, g1, b1, target)

    def weight_grad(name, a, b, n_cols, b_spec=None, part=(0, 1), after=None):
        m_cols = a.shape[1] // part[1]
        tm, tn = _tile(m_cols, 1024), _tile(n_cols, 512)
        first = part[0] * (m_cols // tm)
        return _mm(name, a, b, dims=TN, grid=(m_cols // tm, n_cols // tn),
                   a_spec=pl.BlockSpec((seq, tm), lambda i, j: (0, first + i)),
                   b_spec=b_spec(tn) if b_spec else pl.BlockSpec((seq, tn), lambda i, j: (0, j)),
                   out_shape=jax.ShapeDtypeStruct((m_cols, n_cols), BF16),
                   out_spec=pl.BlockSpec((tm, tn), lambda i, j: (i, j)),
                   extras=() if after is None else (after,), extra_specs=() if after is None else (ANY,))

    def halves_spec(tn):
        per = dm // tn
        return pl.BlockSpec((None, seq, tn), lambda i, j: (j // per, 0, j % per))

    def times_transposed(name, a, w):
        return _mm(name, a, w, dims=NT, grid=(seq // bm, dm // bn),
                   a_spec=pl.BlockSpec((bm, a.shape[1]), lambda i, j: (i, 0)),
                   b_spec=pl.BlockSpec((bn, w.shape[1]), lambda i, j: (j, 0)),
                   out_shape=jax.ShapeDtypeStruct((seq, dm), F32), out_spec=pl.BlockSpec((bm, bn), lambda i, j: (i, j)))

    def stat_row_cols(ref, dev):
        return ref.at[pl.ds(0, 1), pl.ds(pl.multiple_of(dev * rb, LANES), rb)]

    upd = {}
    last = [dr2b]

    def then(value):
        last[0] = value[0] if isinstance(value, (list, tuple)) else value
        return value

    def shard_update(key, parts, w, m, v):
        shape = w.shape
        flat = lambda a: a.reshape(-1, shape[-1])
        parts = list(parts) if isinstance(parts, (list, tuple)) else [parts]
        outs = then(_adamw_shard("adamw_" + key, [p.reshape(p.shape[0], -1, shape[-1]) for p in parts], flat(w),
                                 flat(m), flat(v), last[0]))
        upd[key] = [o.reshape(shape) for o in outs]

    def two_level_scatter(name, ids, streams):
        staged = _sibling_exchange(name + "_pair", streams, ids[0])

        def finish():
            sums = [then(_pair_sum(f"{name}_sum{s}", st[0], st[1], got, last[0]))
                    for s, (st, got) in enumerate(zip(streams, staged))]
            return _chip_exchange(name + "_chip", sums, ids[1])
        return finish

    d_w_outb = then(weight_grad("b_out_proj_dw", yb, dr2b, dm))
    (p_outb,) = _exchange_blocks("scatter_b_out", [(d_w_outb, rows, (rb, dm))], 4)
    dyb = times_transposed("b_out_proj_dx", dr2b, w_outb)
    dhq, dkd, dvd, dsink = then(_attn_backward(qs, kd, vd, kt, zb, att, dyb, b_sinks, tables, after=last[0]))
    dkv = _kv_grad_fold(dkd, dvd, tables)
    d_w_kv = weight_grad("b_kv_proj_dw", x1b, dkv, 2 * kvw)
    d_w_qg = then(weight_grad("b_qg_proj_dw", x1b, dhq, 2 * dm, halves_spec, after=d_w_kv))
    finish_b_in = two_level_scatter("scatter_b_in", (5, 11), [
        (d_w_qg, cols, (dm, cb)), (d_w_kv, k_rows, (rb, kvw)), (d_w_kv, v_rows, (rb, kvw))])
    dr1, dr1b, stats_a = _stream_grad_norm_backward(dhq, w_qg, dkv, w_kv, dr2, xhat1, rstd1, g0, after=last[0])
    last[0] = dr1b
    shard_update("b_w_out", p_outb, b_w_out, m_b_w_out, v_b_w_out)
    p_qg, p_k, p_v = finish_b_in()
    all_b, all_a, all_sink = _exchange_blocks("gather_replicated_grads", [
        (stats_b, None, stats_b.shape), (stats_a, None, stats_a.shape), (dsink, None, dsink.shape)], 9)

    d_w_out = then(weight_grad("a_out_proj_dw", y, dr1b, dm, after=last[0]))
    (p_out,) = _exchange_blocks("scatter_a_out", [(d_w_out, rows, (rb, dm))], 6)
    dy = times_transposed("a_out_proj_dx", dr1b, w_out)
    dh, d_w_g, stats_s = then(_pool_mid_backward(dy, mixed, z_a, pooled, w_g, scale, after=last[0]))
    p_g, p_scale = _exchange_blocks("scatter_a_mid", [
        (d_w_g, group_rows, (n_groups, gb, gd)), (stats_s, stat_row_cols, (1, rb))], 7)
    shard_update("b_w_qg", p_qg, b_w_qg, m_b_w_qg, v_b_w_qg)
    shard_update("b_w_k", p_k, b_w_k, m_b_w_k, v_b_w_k)
    shard_update("b_w_v", p_v, b_w_v, m_b_w_v, v_b_w_v)
    rep = then(_adamw_replicated(all_b, all_a, all_sink, ln_g, ln_b, b_sinks, m_ln_g, m_ln_b, m_b_sinks, v_ln_g,
                                 v_ln_b, v_b_sinks, last[0]))
    upd["ln_g"], upd["ln_b"], upd["b_sinks"] = list(rep[0:4]), list(rep[4:8]), list(rep[8:12])
    finish_a_in = []
    for k in range(2):
        d_w_in = then(weight_grad(f"a_in_proj_dw_{k}", xb, dh, 2 * dm, halves_spec, part=(k, 2), after=last[0]))
        finish_a_in.append(two_level_scatter(f"scatter_a_in_{k}", (8 + 2 * k, 12 + k), [(d_w_in, cols, (dm // 2, cb))]))
    shard_update("a_w_out", p_out, a_w_out, m_a_w_out, v_a_w_out)
    shard_update("a_w_group", p_g, a_w_group, m_a_w_group, v_a_w_group)
    shard_update("a_scale", p_scale, a_scale, m_a_scale, v_a_scale)
    p_in = list(finish_a_in[0]()) + list(finish_a_in[1]())
    grad_x = then(_mm("a_in_proj_dx", dh, w_in, dims=NT, grid=(seq // bm, dm // bn, 2), nk=2,
                      a_spec=pl.BlockSpec((None, bm, dm), lambda i, j, k: (k, i, 0)),
                      b_spec=pl.BlockSpec((bn, dm), lambda i, j, k: (j, k)),
                      out_shape=jax.ShapeDtypeStruct((seq, dm), F32),
                      out_spec=pl.BlockSpec((bm, bn), lambda i, j, k: (i, j)),
                      add=dr1, add_spec=pl.BlockSpec((bm, bn), lambda i, j, k: (i, j)), add_scale=ALPHA,
                      extras=(last[0],), extra_specs=(ANY,)))
    shard_update("a_w_in", p_in, a_w_in, m_a_w_in, v_a_w_in)

    loss = rep[12].reshape(())
    order = ["ln_g", "ln_b", "a_w_in", "a_w_group", "a_scale", "a_w_out", "b_w_k", "b_w_v", "b_w_qg", "b_sinks",
             "b_w_out"]
    return (loss, grad_x.reshape(x.shape), *[upd[n][0] for n in order], *[upd[n][1] for n in order],
            *[upd[n][2] for n in order], *[upd[n][3] for n in order])
```

```python
import functools

import jax
import jax.numpy as jnp
from jax import lax
from jax.experimental import pallas as pl
from jax.experimental.pallas import tpu as pltpu
from jax.experimental.pallas import tpu_sc as plsc

F32 = jnp.float32
BF16 = jnp.bfloat16
MESH = pl.DeviceIdType.MESH
AXES = ("x", "y", "c")
N_DEV = 8

POOL_WINDOWS = (2, 4, 8, 16)
POOL_HALO = 16
HEAD_DIM = 64
GQA_GROUP = 8
ATTN_BLOCK = 128
ROPE_THETA = 10000.0
LN_EPS = 1e-5
NEG_INF = -1e30
DEPTH = 2
ALPHA = (2 * DEPTH) ** 0.25
ADAM_LR = 0.001
ADAM_B1 = 0.9
ADAM_B2 = 0.999
ADAM_EPS = 1e-08
ADAM_WD = 0.01
ADAM_STEP = 10

LANES = 128
STAT_ROWS = 8


def _tile(n, want):
    t = min(n, want)
    while n % t:
        t //= 2
    return t


def _params(*sem):
    return pltpu.CompilerParams(dimension_semantics=sem)


ANY = pl.BlockSpec(memory_space=pl.ANY)


def _my_pos():
    return lax.axis_index("x"), lax.axis_index("y"), lax.axis_index("c")


def _dev_index(p):
    return 4 * p[0] + 2 * p[1] + p[2]


def _handshake(peers):
    barrier = pltpu.get_barrier_semaphore()
    for peer in peers:
        pl.semaphore_signal(barrier, inc=1, device_id=peer, device_id_type=MESH)
    pl.semaphore_wait(barrier, len(peers))


def _launch_on_sequencer(name, collective_id, body, operands, out_shapes, scratch):
    return pl.kernel(
        body, out_type=tuple(out_shapes), name=name,
        mesh=plsc.ScalarSubcoreMesh(axis_name="sequencer", num_cores=1), scratch_types=scratch,
        compiler_params=pltpu.CompilerParams(collective_id=collective_id),
    )(*operands)


def _gather_weights(name, collective_id, streams, out_shapes):
    n_s = len(streams)
    n_out = len(out_shapes)

    def body(*refs):
        srcs = refs[:n_s]
        outs = refs[n_s:n_s + n_out]
        send_sems, recv_sems, local_sems = refs[n_s + n_out:]
        x, y, c = _my_pos()
        me, sibling = (x, y, c), (x, y, 1 - c)
        chips = [(1 - x, y), (x, 1 - y), (1 - x, 1 - y)]
        _handshake([sibling] + [(*chip, c) for chip in chips])

        def copy(s, k, block, to, from_shard=False):
            out_ref = outs[streams[s][1]]
            win = streams[s][2](out_ref, _dev_index(block))
            return pltpu.make_async_remote_copy(
                src_ref=srcs[s] if from_shard else win, dst_ref=win,
                send_sem=send_sems.at[7 * s + k], recv_sem=recv_sems.at[7 * s + k],
                device_id=to, device_id_type=MESH)

        mine = [pltpu.make_async_copy(srcs[s], streams[s][2](outs[streams[s][1]], _dev_index(me)), local_sems.at[s])
                for s in range(n_s)]
        for cp in mine:
            cp.start()
        first = []
        for s in range(n_s):
            first.append(copy(s, 0, me, sibling, True))
            first += [copy(s, 1 + j, me, (*chip, c), True) for j, chip in enumerate(chips)]
        for cp in first:
            cp.start()
        passed = []
        for j, chip in enumerate(chips):
            for s in range(n_s):
                copy(s, 1 + j, (*chip, c), me).wait_recv()
                fwd = copy(s, 4 + j, (*chip, c), sibling)
                fwd.start()
                passed.append(fwd)
        for s in range(n_s):
            copy(s, 0, sibling, me).wait_recv()
            for j, chip in enumerate(chips):
                copy(s, 4 + j, (*chip, 1 - c), me).wait_recv()
        for cp in first + passed:
            cp.wait_send()
        for cp in mine:
            cp.wait()

    scratch = [pltpu.SemaphoreType.DMA((7 * n_s,)), pltpu.SemaphoreType.DMA((7 * n_s,)),
               pltpu.SemaphoreType.DMA((n_s,))]
    return _launch_on_sequencer(name, collective_id, body, [s[0] for s in streams], out_shapes, scratch)


def _exchange_blocks(name, streams, collective_id):
    n_s = len(streams)

    def body(*refs):
        srcs = refs[:n_s]
        outs = refs[n_s:2 * n_s]
        send_sems, recv_sems, local_sems = refs[2 * n_s:]
        x, y, c = _my_pos()
        me = _dev_index((x, y, c))
        _handshake([(1 - x if k & 4 else x, 1 - y if k & 2 else y, 1 - c if k & 1 else c) for k in range(1, N_DEV)])

        def window(s, dev):
            return srcs[s] if streams[s][1] is None else streams[s][1](srcs[s], dev)

        mine = [pltpu.make_async_copy(window(s, me), outs[s].at[me], local_sems.at[s]) for s in range(n_s)]
        for cp in mine:
            cp.start()
        copies = []
        for k in (2, 4, 6, 3, 5, 7, 1):
            peer = (1 - x if k & 4 else x, 1 - y if k & 2 else y, 1 - c if k & 1 else c)
            for s in range(n_s):
                copies.append(pltpu.make_async_remote_copy(
                    src_ref=window(s, _dev_index(peer)), dst_ref=outs[s].at[me],
                    send_sem=send_sems.at[7 * s + k - 1], recv_sem=recv_sems.at[7 * s + k - 1],
                    device_id=peer, device_id_type=MESH))
        for cp in copies:
            cp.start()
        for cp in copies:
            cp.wait()
        for cp in mine:
            cp.wait()

    out_shapes = [jax.ShapeDtypeStruct((N_DEV,) + tuple(s[2]), s[0].dtype) for s in streams]
    scratch = [pltpu.SemaphoreType.DMA((7 * n_s,)), pltpu.SemaphoreType.DMA((7 * n_s,)),
               pltpu.SemaphoreType.DMA((n_s,))]
    return _launch_on_sequencer(name, collective_id, body, [s[0] for s in streams], out_shapes, scratch)


N_CHIPS = 4


def _sibling_exchange(name, streams, collective_id):
    n_s = len(streams)

    def body(*refs):
        srcs = refs[:n_s]
        outs = refs[n_s:2 * n_s]
        send_sems, recv_sems = refs[2 * n_s:]
        x, y, c = _my_pos()
        sibling = (x, y, 1 - c)
        _handshake([sibling])
        copies = [pltpu.make_async_remote_copy(
            src_ref=streams[s][1](srcs[s], 2 * chip + (1 - c)), dst_ref=outs[s].at[chip],
            send_sem=send_sems.at[N_CHIPS * s + chip], recv_sem=recv_sems.at[N_CHIPS * s + chip],
            device_id=sibling, device_id_type=MESH) for s in range(n_s) for chip in range(N_CHIPS)]
        for cp in copies:
            cp.start()
        for cp in copies:
            cp.wait()

    out_shapes = [jax.ShapeDtypeStruct((N_CHIPS,) + tuple(s[2]), s[0].dtype) for s in streams]
    scratch = [pltpu.SemaphoreType.DMA((N_CHIPS * n_s,)), pltpu.SemaphoreType.DMA((N_CHIPS * n_s,))]
    return _launch_on_sequencer(name, collective_id, body, [s[0] for s in streams], out_shapes, scratch)


def _pair_sum(name, array, from_sibling, my_core, after):
    _, rows, cols = from_sibling.shape
    tr = _tile(rows, 512)

    def body(core_ref, own_ref, sib_ref, after_ref, o_ref):
        del core_ref, after_ref
        o_ref[...] = (own_ref[...].astype(F32) + sib_ref[...].astype(F32)).astype(o_ref.dtype)

    staged_spec = pl.BlockSpec((None, tr, cols), lambda k, i, core: (k, i, 0))
    return pl.pallas_call(
        body, name=name, out_shape=jax.ShapeDtypeStruct(from_sibling.shape, array.dtype),
        grid_spec=pltpu.PrefetchScalarGridSpec(
            num_scalar_prefetch=1, grid=(N_CHIPS, rows // tr),
            in_specs=[pl.BlockSpec((tr, cols), lambda k, i, core: (i, 2 * k + core[0])), staged_spec, ANY],
            out_specs=staged_spec),
        compiler_params=_params("parallel", "parallel"),
    )(my_core, array, from_sibling, after)


def _chip_exchange(name, pair_sums, collective_id):
    n_s = len(pair_sums)

    def body(*refs):
        srcs = refs[:n_s]
        outs = refs[n_s:2 * n_s]
        send_sems, recv_sems, local_sems = refs[2 * n_s:]
        x, y, c = _my_pos()
        my_chip = 2 * x + y
        chips = [(1 - x, y), (x, 1 - y), (1 - x, 1 - y)]
        _handshake([(*chip, c) for chip in chips])
        mine = [pltpu.make_async_copy(srcs[s].at[my_chip], outs[s].at[my_chip], local_sems.at[s]) for s in range(n_s)]
        copies = [pltpu.make_async_remote_copy(
            src_ref=srcs[s].at[2 * chip[0] + chip[1]], dst_ref=outs[s].at[my_chip],
            send_sem=send_sems.at[3 * s + j], recv_sem=recv_sems.at[3 * s + j],
            device_id=(*chip, c), device_id_type=MESH) for s in range(n_s) for j, chip in enumerate(chips)]
        for cp in mine + copies:
            cp.start()
        for cp in copies:
            cp.wait()
        for cp in mine:
            cp.wait()

    out_shapes = [jax.ShapeDtypeStruct(p.shape, p.dtype) for p in pair_sums]
    scratch = [pltpu.SemaphoreType.DMA((3 * n_s,)), pltpu.SemaphoreType.DMA((3 * n_s,)),
               pltpu.SemaphoreType.DMA((n_s,))]
    return _launch_on_sequencer(name, collective_id, body, list(pair_sums), out_shapes, scratch)


NN = (((1,), (0,)), ((), ()))
NT = (((1,), (1,)), ((), ()))
TN = (((0,), (0,)), ((), ()))


def _mm(name, a, b, *, dims, grid, a_spec, b_spec, out_shape, out_spec, nk=1,
        add=None, add_spec=None, add_scale=1.0, epilogue=None, extras=(), extra_specs=()):
    n_extra = len(extras)
    has_add = add is not None

    def body(*refs):
        a_ref, b_ref = refs[:2]
        pos = 2
        add_ref = None
        if has_add:
            add_ref = refs[pos]
            pos += 1
        extra_refs = refs[pos:pos + n_extra]
        o_ref = refs[pos + n_extra]
        acc_ref = refs[pos + n_extra + 1] if nk > 1 else None

        def finish(val):
            if has_add:
                val = val + add_scale * add_ref[...]
            if epilogue is not None:
                val = epilogue(val, *extra_refs)
            o_ref[...] = val.astype(o_ref.dtype)

        part = lax.dot_general(a_ref[...].astype(BF16), b_ref[...].astype(BF16), dims,
                               preferred_element_type=F32)
        if nk == 1:
            finish(part)
        else:
            k = pl.program_id(2)

            @pl.when(k == 0)
            def _():
                acc_ref[...] = part

            @pl.when(jnp.logical_and(k > 0, k < nk - 1))
            def _():
                acc_ref[...] += part

            @pl.when(k == nk - 1)
            def _():
                finish(acc_ref[...] + part)

    in_specs = [a_spec, b_spec] + ([add_spec] if has_add else []) + list(extra_specs)
    operands = [a, b] + ([add] if has_add else []) + list(extras)
    scratch = [pltpu.VMEM(out_spec.block_shape, F32)] if nk > 1 else []
    sem = ("parallel", "parallel") + (("arbitrary",) if nk > 1 else ())
    return pl.pallas_call(
        body, name=name, grid=grid, out_shape=out_shape,
        in_specs=in_specs, out_specs=out_spec, scratch_shapes=scratch,
        compiler_params=_params(*sem),
    )(*operands)


def _cast_bf16(name, a):
    rows, cols = a.shape
    tr = _tile(rows, 512)

    def body(a_ref, o_ref):
        o_ref[...] = a_ref[...].astype(BF16)

    return pl.pallas_call(
        body, name=name, grid=(rows // tr,),
        out_shape=jax.ShapeDtypeStruct(a.shape, BF16),
        in_specs=[pl.BlockSpec((tr, cols), lambda i: (i, 0))],
        out_specs=pl.BlockSpec((tr, cols), lambda i: (i, 0)),
        compiler_params=_params("parallel"),
    )(a)


def _rope_tables(seq):
    inv_freq = ROPE_THETA ** (-jnp.arange(0, HEAD_DIM, 2, dtype=F32) / HEAD_DIM)
    ang = jnp.arange(seq, dtype=F32)[:, None] * inv_freq[None, :]
    cos, sin = jnp.cos(ang), jnp.sin(ang)
    cos, sin = (jnp.concatenate([t, t, t, t], axis=-1) for t in (cos, sin))
    first_half = (jnp.arange(LANES) % HEAD_DIM < HEAD_DIM // 2)[None, :]
    return cos, jnp.where(first_half, -sin, 0.0), jnp.where(first_half, 0.0, sin)


def _rot(t, sin_a, sin_b):
    return pltpu.roll(t, LANES - HEAD_DIM // 2, 1) * sin_a + pltpu.roll(t, HEAD_DIM // 2, 1) * sin_b


def _rope(t, cos, sin_a, sin_b):
    return t * cos + _rot(t, sin_a, sin_b)


def _rope_transposed(dy, cos, sin_a, sin_b):
    return dy * cos - _rot(dy, sin_a, sin_b)


def _silu_parts(z):
    sig = jax.nn.sigmoid(z)
    return z * sig, sig * (1.0 + z * (1.0 - sig))


def _layer_norm_stats(r):
    mu = jnp.mean(r, axis=-1, keepdims=True)
    d = r - mu
    var = jnp.mean(d * d, axis=-1, keepdims=True)
    rstd = lax.rsqrt(var + LN_EPS)
    return d * rstd, rstd


def _layer_norm_backward(dout, xhat, rstd, gain):
    dxh = dout * gain
    m1 = jnp.mean(dxh, axis=-1, keepdims=True)
    m2 = jnp.mean(dxh * xhat, axis=-1, keepdims=True)
    return rstd * (dxh - m1 - xhat * m2)


def _col_sum(v):
    return jnp.sum(v, axis=0, keepdims=True)


def _pool_forward(xb, w_in, wg, scale):
    seq, dm = xb.shape
    n_g = len(POOL_WINDOWS)
    gd = dm // n_g
    tile = _tile(seq, 1024)
    halo_blocks = tile // POOL_HALO

    def body(x_ref, xp_ref, wu_ref, wz_ref, wg_ref, sc_ref, y_ref, p_ref, mx_ref, z_ref):
        i, g = pl.program_id(0), pl.program_id(1)
        u = jnp.dot(x_ref[...], wu_ref[...], preferred_element_type=F32)
        z = jnp.dot(x_ref[...], wz_ref[...], preferred_element_type=F32)
        prev = jnp.where(i > 0, jnp.dot(xp_ref[...], wu_ref[...], preferred_element_type=F32), 0.0)
        s = jnp.concatenate([prev, u], axis=0)
        sums, sh = [], 1
        while sh < POOL_WINDOWS[-1]:
            s = s + pltpu.roll(s, sh, 0)
            sums.append(s)
            sh *= 2
        win = sums[-1]
        for k in range(n_g - 2, -1, -1):
            win = jnp.where(g == k, sums[k], win)
        row = i * tile + lax.broadcasted_iota(jnp.int32, (tile, 1), 0)
        window = jnp.left_shift(2, g).astype(F32)
        p = win[POOL_HALO:, :] * (1.0 / jnp.minimum((row + 1).astype(F32), window)) - u
        pb = p.astype(BF16)
        mx = jnp.dot(pb, wg_ref[...], preferred_element_type=F32)
        y_ref[...] = (mx * sc_ref[...] * (z * jax.nn.sigmoid(z))).astype(BF16)
        p_ref[...] = pb
        mx_ref[...] = mx
        z_ref[...] = z

    out_spec = pl.BlockSpec((tile, gd), lambda i, g: (i, g))
    return pl.pallas_call(
        body, name="pool_fwd", grid=(seq // tile, n_g),
        out_shape=(jax.ShapeDtypeStruct((seq, dm), BF16), jax.ShapeDtypeStruct((seq, dm), BF16),
                   jax.ShapeDtypeStruct((seq, dm), F32), jax.ShapeDtypeStruct((seq, dm), F32)),
        in_specs=[pl.BlockSpec((tile, dm), lambda i, g: (i, 0)),
                  pl.BlockSpec((POOL_HALO, dm), lambda i, g: (jnp.maximum(i * halo_blocks - 1, 0), 0)),
                  pl.BlockSpec((dm, gd), lambda i, g: (0, g)),
                  pl.BlockSpec((dm, gd), lambda i, g: (0, n_g + g)),
                  pl.BlockSpec((None, gd, gd), lambda i, g: (g, 0, 0)),
                  pl.BlockSpec((1, gd), lambda i, g: (0, g))],
        out_specs=(out_spec, out_spec, out_spec, out_spec),
        compiler_params=_params("parallel", "parallel"),
    )(xb, xb, w_in, w_in, wg, scale)


def _pool_mid_backward(dy, mx, z, p, wg, scale, after):
    seq, dm = dy.shape
    gd = dm // len(POOL_WINDOWS)
    tile = _tile(seq, 256)
    n_i = seq // tile

    def body(dy_ref, mx_ref, z_ref, p_ref, wg_ref, sc_ref, after_ref, dh_ref, dwg_ref, st_ref, dwg_acc, carry):
        del after_ref
        i = pl.program_id(0)
        ti = n_i - 1 - i

        @pl.when(i == 0)
        def _():
            dwg_acc[...] = jnp.zeros_like(dwg_acc)
            carry[...] = jnp.zeros_like(carry)
            st_ref[...] = jnp.zeros_like(st_ref)

        row = ti * tile + lax.broadcasted_iota(jnp.int32, (tile, 1), 0)
        count = (row + 1).astype(F32)
        for g, w in enumerate(POOL_WINDOWS):
            cs = slice(g * gd, (g + 1) * gd)
            z = z_ref[:, cs]
            sz, dsz = _silu_parts(z)
            dyg = dy_ref[:, cs]
            mxg = mx_ref[:, cs]
            sc = sc_ref[:, cs]
            t1 = dyg * sz
            st_ref[0:1, cs] += _col_sum(t1 * mxg)
            dh_ref[1, :, cs] = (dyg * (mxg * sc) * dsz).astype(BF16)
            dmx = (t1 * sc).astype(BF16)
            dwg_acc[g] += lax.dot_general(p_ref[:, cs], dmx, TN, preferred_element_type=F32)
            dp = lax.dot_general(dmx, wg_ref[g], NT, preferred_element_type=F32)
            e = dp * (1.0 / jnp.minimum(count, float(w)))
            s = jnp.concatenate([e, carry[:, cs]], axis=0)
            n = tile + POOL_HALO
            sh = 1
            while sh < w:
                s = s + pltpu.roll(s, n - sh, 0)
                sh *= 2
            dh_ref[0, :, cs] = (s[:tile, :] - dp).astype(BF16)
            carry[:, cs] = e[:POOL_HALO, :]

        @pl.when(i == n_i - 1)
        def _():
            dwg_ref[...] = dwg_acc[...].astype(BF16)

    row_spec = pl.BlockSpec((tile, dm), lambda i: (n_i - 1 - i, 0))
    return pl.pallas_call(
        body, name="pool_mid_bwd", grid=(n_i,),
        out_shape=(jax.ShapeDtypeStruct((2, seq, dm), BF16), jax.ShapeDtypeStruct(wg.shape, BF16),
                   jax.ShapeDtypeStruct((STAT_ROWS, dm), F32)),
        in_specs=[row_spec, row_spec, row_spec, row_spec,
                  pl.BlockSpec(wg.shape, lambda i: (0, 0, 0)),
                  pl.BlockSpec((1, dm), lambda i: (0, 0)), ANY],
        out_specs=(pl.BlockSpec((2, tile, dm), lambda i: (0, n_i - 1 - i, 0)),
                   pl.BlockSpec(wg.shape, lambda i: (0, 0, 0)),
                   pl.BlockSpec((STAT_ROWS, dm), lambda i: (0, 0))),
        scratch_shapes=[pltpu.VMEM(wg.shape, F32), pltpu.VMEM((POOL_HALO, dm), F32)],
        compiler_params=_params("arbitrary"),
    )(dy, mx, z, p, wg, scale, after)


def _out_proj_norm(y, w, x, gain, bias):
    seq, dm = x.shape
    tile = _tile(seq, 512)

    def body(y_ref, w_ref, x_ref, g_ref, b_ref, xhat_ref, rstd_ref, xb_ref):
        o = jnp.dot(y_ref[...], w_ref[...], preferred_element_type=F32)
        xhat, rstd = _layer_norm_stats(ALPHA * x_ref[...] + o)
        xhat_ref[...] = xhat
        rstd_ref[...] = rstd
        xb_ref[...] = (xhat * g_ref[...] + b_ref[...]).astype(BF16)

    row_spec = pl.BlockSpec((tile, dm), lambda i: (i, 0))
    vec_spec = pl.BlockSpec((1, dm), lambda i: (0, 0))
    return pl.pallas_call(
        body, name="out_proj_norm_a", grid=(seq // tile,),
        out_shape=(jax.ShapeDtypeStruct((seq, dm), F32), jax.ShapeDtypeStruct((seq, 1), F32),
                   jax.ShapeDtypeStruct((seq, dm), BF16)),
        in_specs=[row_spec, pl.BlockSpec(w.shape, lambda i: (0, 0), pipeline_mode=pl.Buffered(1)), row_spec, vec_spec,
                  vec_spec],
        out_specs=(row_spec, pl.BlockSpec((tile, 1), lambda i: (i, 0)), row_spec),
        compiler_params=_params("parallel"),
    )(y, w, x, gain, bias)


def _kv_proj(xb, wkv, tables):
    seq, dm = xb.shape
    kvw = wkv.shape[1] // 2
    n_kv = kvw // HEAD_DIM
    tile = _tile(seq, 512)

    def body(x_ref, w_ref, cos_ref, sa_ref, sb_ref, kd_ref, vd_ref, kt_ref, vt_ref):
        kv = jnp.dot(x_ref[...], w_ref[...], preferred_element_type=F32)
        low = lax.broadcasted_iota(jnp.int32, (1, LANES), 1) < HEAD_DIM
        cos, sa, sb = cos_ref[...], sa_ref[...], sb_ref[...]

        def put(pair, h, nat_ref, t_ref):
            swapped = pltpu.roll(pair, HEAD_DIM, 1)
            for head, dup in ((h, jnp.where(low, pair, swapped)), (h + 1, jnp.where(low, swapped, pair))):
                nat_ref[head] = dup.astype(BF16)
                t_ref[head] = dup.T.astype(BF16)

        for j in range(kvw // LANES):
            put(_rope(kv[:, j * LANES:(j + 1) * LANES], cos, sa, sb), 2 * j, kd_ref, kt_ref)
            put(kv[:, kvw + j * LANES:kvw + (j + 1) * LANES], 2 * j, vd_ref, vt_ref)

    tab_spec = pl.BlockSpec((tile, LANES), lambda i: (i, 0))
    dup_spec = pl.BlockSpec((n_kv, tile, LANES), lambda i: (0, i, 0))
    dup_shape = jax.ShapeDtypeStruct((n_kv, seq, LANES), BF16)
    t_spec = pl.BlockSpec((n_kv, LANES, tile), lambda i: (0, 0, i))
    t_shape = jax.ShapeDtypeStruct((n_kv, LANES, seq), BF16)
    return pl.pallas_call(
        body, name="kv_proj", grid=(seq // tile,),
        out_shape=(dup_shape, dup_shape, t_shape, t_shape),
        in_specs=[pl.BlockSpec((tile, dm), lambda i: (i, 0)), pl.BlockSpec(wkv.shape, lambda i: (0, 0)),
                  tab_spec, tab_spec, tab_spec],
        out_specs=(dup_spec, dup_spec, t_spec, t_spec),
        compiler_params=_params("parallel"),
    )(xb, wkv, *tables)


def _head_queries(q_ref, low):
    parts = []
    for j in range(GQA_GROUP // 2):
        q2 = q_ref[:, j * LANES:(j + 1) * LANES]
        parts += [jnp.where(low, q2, 0), jnp.where(low, 0, q2)]
    return parts


def _probs_transposed(n, kh, kcat, q_all, sink_ref):
    st = lax.dot_general(kcat, q_all, NT, preferred_element_type=F32)
    key = lax.broadcasted_iota(jnp.int32, (2 * ATTN_BLOCK, ATTN_BLOCK), 0)
    qry = lax.broadcasted_iota(jnp.int32, (2 * ATTN_BLOCK, ATTN_BLOCK), 1)
    valid = (key > qry) & (key <= qry + ATTN_BLOCK) & ((key >= ATTN_BLOCK) | (n > 0))
    st = st + jnp.tile(jnp.where(valid, 0.0, NEG_INF), (1, GQA_GROUP))
    sink = jnp.concatenate([jnp.full((1, ATTN_BLOCK), sink_ref[0, kh * GQA_GROUP + h], F32)
                            for h in range(GQA_GROUP)], axis=1)
    m = jnp.maximum(jnp.max(st, axis=0, keepdims=True), sink)
    e = jnp.exp(st - m)
    e_sink = jnp.exp(sink - m)
    inv = 1.0 / (jnp.sum(e, axis=0, keepdims=True) + e_sink)
    return e * inv, e_sink * inv


def _attn_specs(n_width):
    q_spec = pl.BlockSpec((ATTN_BLOCK, n_width), lambda kh, n: (n, kh))
    cur = pl.BlockSpec((None, ATTN_BLOCK, LANES), lambda kh, n: (kh, n, 0))
    prev = pl.BlockSpec((None, ATTN_BLOCK, LANES), lambda kh, n: (kh, jnp.maximum(n - 1, 0), 0))
    cur_t = pl.BlockSpec((None, LANES, ATTN_BLOCK), lambda kh, n: (kh, 0, n))
    prev_t = pl.BlockSpec((None, LANES, ATTN_BLOCK), lambda kh, n: (kh, 0, jnp.maximum(n - 1, 0)))
    return q_spec, cur, prev, cur_t, prev_t


def _pair_product_transposed(mat_t, rhs, j, low_rows):
    a = rhs[:, 2 * j * ATTN_BLOCK:(2 * j + 1) * ATTN_BLOCK]
    b = rhs[:, (2 * j + 1) * ATTN_BLOCK:(2 * j + 2) * ATTN_BLOCK]
    out_t = (jnp.dot(jnp.where(low_rows, mat_t, 0), a, preferred_element_type=F32)
             + jnp.dot(jnp.where(low_rows, 0, mat_t), b, preferred_element_type=F32))
    return out_t.T


def _attn_forward(qs, kd, vt, zb, sinks):
    seq, dm = qs.shape
    n_kv = kd.shape[0]
    gw = GQA_GROUP * HEAD_DIM

    def body(q_ref, kp_ref, kc_ref, vtp_ref, vtc_ref, z_ref, sink_ref, att_ref, yb_ref):
        kh, n = pl.program_id(0), pl.program_id(1)
        low = lax.broadcasted_iota(jnp.int32, (1, LANES), 1) < HEAD_DIM
        low_rows = lax.broadcasted_iota(jnp.int32, (LANES, 1), 0) < HEAD_DIM
        kcat = jnp.concatenate([kp_ref[...], kc_ref[...]], axis=0)
        vt = jnp.concatenate([vtp_ref[...], vtc_ref[...]], axis=1)
        q_all = jnp.concatenate(_head_queries(q_ref, low), axis=0)
        probs_t, _ = _probs_transposed(n, kh, kcat, q_all, sink_ref)
        pt = probs_t.astype(BF16)
        for j in range(GQA_GROUP // 2):
            cs = slice(j * LANES, (j + 1) * LANES)
            o2 = _pair_product_transposed(vt, pt, j, low_rows)
            att_ref[:, cs] = o2
            z = z_ref[:, cs]
            yb_ref[:, cs] = (o2 * (z * jax.nn.sigmoid(z))).astype(BF16)

    q_spec, cur, prev, cur_t, prev_t = _attn_specs(gw)
    return pl.pallas_call(
        body, name="attn_fwd", grid=(n_kv, seq // ATTN_BLOCK),
        out_shape=(jax.ShapeDtypeStruct((seq, dm), F32), jax.ShapeDtypeStruct((seq, dm), BF16)),
        in_specs=[q_spec, prev, cur, prev_t, cur_t, q_spec, pl.BlockSpec(memory_space=pltpu.SMEM)],
        out_specs=(q_spec, q_spec),
        compiler_params=_params("parallel", "parallel"),
    )(qs, kd, kd, vt, vt, zb, sinks)


def _attn_backward(qs, kd, vd, kt, zb, att, dyb, sinks, tables, after):
    seq, dm = qs.shape
    n_kv = kd.shape[0]
    gw = GQA_GROUP * HEAD_DIM
    n_blocks = seq // ATTN_BLOCK

    def body(q_ref, kp_ref, kc_ref, vp_ref, vc_ref, ktp_ref, ktc_ref, z_ref, att_ref, dyb_ref, sink_ref,
             cos_ref, sa_ref, sb_ref, after_ref, dh_ref, dk_ref, dv_ref, ds_ref):
        del after_ref
        kh, n = pl.program_id(0), pl.program_id(1)

        @pl.when(n == 0)
        def _():
            dk_ref[...] = jnp.zeros_like(dk_ref)
            dv_ref[...] = jnp.zeros_like(dv_ref)

        @pl.when(jnp.logical_and(n == 0, kh == 0))
        def _():
            ds_ref[...] = jnp.zeros_like(ds_ref)

        low = lax.broadcasted_iota(jnp.int32, (1, LANES), 1) < HEAD_DIM
        low_rows = lax.broadcasted_iota(jnp.int32, (LANES, 1), 0) < HEAD_DIM
        head_lane = lax.broadcasted_iota(jnp.int32, (1, LANES), 1)
        kcat = jnp.concatenate([kp_ref[...], kc_ref[...]], axis=0)
        vcat = jnp.concatenate([vp_ref[...], vc_ref[...]], axis=0)
        kt = jnp.concatenate([ktp_ref[...], ktc_ref[...]], axis=1)
        cos, sa, sb = cos_ref[...], sa_ref[...], sb_ref[...]
        q_parts = _head_queries(q_ref, low)
        q_all = jnp.concatenate(q_parts, axis=0)
        d_parts = []
        for j in range(GQA_GROUP // 2):
            cs = slice(j * LANES, (j + 1) * LANES)
            sz, dsz = _silu_parts(z_ref[:, cs])
            dy2 = dyb_ref[:, cs]
            dh_ref[1, :, cs] = (dy2 * att_ref[:, cs] * dsz).astype(BF16)
            datt = (dy2 * sz).astype(BF16)
            d_parts += [jnp.where(low, datt, 0), jnp.where(low, 0, datt)]
        d_all = jnp.concatenate(d_parts, axis=0)
        probs_t, sink_p = _probs_transposed(n, kh, kcat, q_all, sink_ref)
        dprobs_t = lax.dot_general(vcat, d_all, NT, preferred_element_type=F32)
        row_dot = jnp.sum(probs_t * dprobs_t, axis=0, keepdims=True)
        ds_t = (probs_t * (dprobs_t - row_dot)).astype(BF16)
        dk = jnp.dot(ds_t, q_all, preferred_element_type=F32)
        dv = jnp.dot(probs_t.astype(BF16), d_all, preferred_element_type=F32)
        for j in range(GQA_GROUP // 2):
            dq2 = _pair_product_transposed(kt, ds_t, j, low_rows)
            dh_ref[0, :, j * LANES:(j + 1) * LANES] = (_rope_transposed(dq2, cos, sa, sb) * 0.125).astype(BF16)
        sink_dot = sink_p * row_dot
        dsink = jnp.zeros((1, LANES), F32)
        for h in range(GQA_GROUP):
            part = jnp.sum(sink_dot[:, h * ATTN_BLOCK:(h + 1) * ATTN_BLOCK], axis=1, keepdims=True)
            dsink = dsink - jnp.where(head_lane == kh * GQA_GROUP + h, part, 0.0)
        ds_ref[0:1, :] += dsink

        @pl.when(n == 0)
        def _():
            dk_ref[pl.ds(0, ATTN_BLOCK), :] += dk[ATTN_BLOCK:, :]
            dv_ref[pl.ds(0, ATTN_BLOCK), :] += dv[ATTN_BLOCK:, :]

        @pl.when(n > 0)
        def _():
            start = pl.multiple_of((n - 1) * ATTN_BLOCK, ATTN_BLOCK)
            dk_ref[pl.ds(start, 2 * ATTN_BLOCK), :] += dk
            dv_ref[pl.ds(start, 2 * ATTN_BLOCK), :] += dv

    q_spec, cur, prev, cur_t, prev_t = _attn_specs(gw)
    tab_spec = pl.BlockSpec((ATTN_BLOCK, LANES), lambda kh, n: (n, 0))
    acc_spec = pl.BlockSpec((None, seq, LANES), lambda kh, n: (kh, 0, 0))
    acc_shape = jax.ShapeDtypeStruct((n_kv, seq, LANES), F32)
    return pl.pallas_call(
        body, name="attn_bwd", grid=(n_kv, n_blocks),
        out_shape=(jax.ShapeDtypeStruct((2, seq, dm), BF16), acc_shape, acc_shape,
                   jax.ShapeDtypeStruct((STAT_ROWS, LANES), F32)),
        in_specs=[q_spec, prev, cur, prev, cur, prev_t, cur_t, q_spec, q_spec, q_spec,
                  pl.BlockSpec(memory_space=pltpu.SMEM), tab_spec, tab_spec, tab_spec, ANY],
        out_specs=(pl.BlockSpec((2, ATTN_BLOCK, gw), lambda kh, n: (0, n, kh)), acc_spec, acc_spec,
                   pl.BlockSpec((STAT_ROWS, LANES), lambda kh, n: (0, 0))),
        compiler_params=_params("arbitrary", "arbitrary"),
    )(qs, kd, kd, vd, vd, kt, kt, zb, att, dyb, sinks, *tables, after)


def _kv_grad_fold(dk, dv, tables):
    n_kv, seq, _ = dk.shape
    kvw = n_kv * HEAD_DIM
    tile = _tile(seq, 512)

    def body(dk_ref, dv_ref, cos_ref, sa_ref, sb_ref, o_ref):
        low = lax.broadcasted_iota(jnp.int32, (1, LANES), 1) < HEAD_DIM
        cos, sa, sb = cos_ref[...], sa_ref[...], sb_ref[...]

        def folded(ref, h):
            t = ref[h]
            return t + pltpu.roll(t, HEAD_DIM, 1)

        for j in range(n_kv // 2):
            ka = _rope_transposed(folded(dk_ref, 2 * j), cos, sa, sb)
            kb = _rope_transposed(folded(dk_ref, 2 * j + 1), cos, sa, sb)
            o_ref[:, j * LANES:(j + 1) * LANES] = jnp.where(low, ka, kb).astype(BF16)
            o_ref[:, kvw + j * LANES:kvw + (j + 1) * LANES] = jnp.where(
                low, folded(dv_ref, 2 * j), folded(dv_ref, 2 * j + 1)).astype(BF16)

    tab_spec = pl.BlockSpec((tile, LANES), lambda i: (i, 0))
    in_spec = pl.BlockSpec((n_kv, tile, LANES), lambda i: (0, i, 0))
    return pl.pallas_call(
        body, name="kv_grad_fold", grid=(seq // tile,),
        out_shape=jax.ShapeDtypeStruct((seq, 2 * kvw), BF16),
        in_specs=[in_spec, in_spec, tab_spec, tab_spec, tab_spec],
        out_specs=pl.BlockSpec((tile, 2 * kvw), lambda i: (i, 0)),
        compiler_params=_params("parallel"),
    )(dk, dv, *tables)


def _out_proj_norm_loss(yb, w, xhat1, gain0, bias0, gain1, bias1, target):
    seq, dm = xhat1.shape
    tile = _tile(seq, 512)

    def body(y_ref, w_ref, xh1_ref, g0_ref, b0_ref, g1_ref, b1_ref, t_ref, dr_ref, drb_ref, st_ref):
        i = pl.program_id(0)

        @pl.when(i == 0)
        def _():
            st_ref[...] = jnp.zeros_like(st_ref)

        ob = jnp.dot(y_ref[...], w_ref[...], preferred_element_type=F32)
        x1 = xh1_ref[...] * g0_ref[...] + b0_ref[...]
        xhat, rstd = _layer_norm_stats(ALPHA * x1 + ob)
        err = xhat * g1_ref[...] + b1_ref[...] - t_ref[...]
        dout = err * (1.0 / dm)
        dr = _layer_norm_backward(dout, xhat, rstd, g1_ref[...])
        dr_ref[...] = dr
        drb_ref[...] = dr.astype(BF16)
        st_ref[0:1, :] += _col_sum(dout * xhat)
        st_ref[1:2, :] += _col_sum(dout)
        st_ref[2:3, :] += _col_sum(err * err)

    row_spec = pl.BlockSpec((tile, dm), lambda i: (i, 0))
    vec_spec = pl.BlockSpec((1, dm), lambda i: (0, 0))
    return pl.pallas_call(
        body, name="out_proj_norm_loss_b", grid=(seq // tile,),
        out_shape=(jax.ShapeDtypeStruct((seq, dm), F32), jax.ShapeDtypeStruct((seq, dm), BF16),
                   jax.ShapeDtypeStruct((STAT_ROWS, dm), F32)),
        in_specs=[row_spec, pl.BlockSpec(w.shape, lambda i: (0, 0), pipeline_mode=pl.Buffered(1)), row_spec, vec_spec,
                  vec_spec, vec_spec,
                  vec_spec, row_spec],
        out_specs=(row_spec, row_spec, pl.BlockSpec((STAT_ROWS, dm), lambda i: (0, 0))),
        compiler_params=_params("arbitrary"),
    )(yb, w, xhat1, gain0, bias0, gain1, bias1, target)


def _stream_grad_norm_backward(dhq, wqg, dkv, wkv, dr2, xhat1, rstd1, gain0, after):
    seq, dm = dr2.shape
    tile = _tile(seq, 256)

    def body(dh_ref, wqg_ref, dkv_ref, wkv_ref, dr2_ref, xh_ref, rstd_ref, g_ref, after_ref, dr_ref, drb_ref, st_ref):
        del after_ref

        @pl.when(pl.program_id(0) == 0)
        def _():
            st_ref[...] = jnp.zeros_like(st_ref)

        dx1 = (lax.dot_general(dh_ref[0], wqg_ref[:, :dm], NT, preferred_element_type=F32)
               + lax.dot_general(dh_ref[1], wqg_ref[:, dm:], NT, preferred_element_type=F32)
               + lax.dot_general(dkv_ref[...], wkv_ref[...], NT, preferred_element_type=F32)
               + ALPHA * dr2_ref[...])
        xhat = xh_ref[...]
        dr = _layer_norm_backward(dx1, xhat, rstd_ref[...], g_ref[...])
        dr_ref[...] = dr
        drb_ref[...] = dr.astype(BF16)
        st_ref[0:1, :] += _col_sum(dx1 * xhat)
        st_ref[1:2, :] += _col_sum(dx1)

    row_spec = pl.BlockSpec((tile, dm), lambda i: (i, 0))
    resident = pl.Buffered(1)
    return pl.pallas_call(
        body, name="stream_grad_norm_bwd", grid=(seq // tile,),
        out_shape=(jax.ShapeDtypeStruct((seq, dm), F32), jax.ShapeDtypeStruct((seq, dm), BF16),
                   jax.ShapeDtypeStruct((STAT_ROWS, dm), F32)),
        in_specs=[pl.BlockSpec((2, tile, dm), lambda i: (0, i, 0)),
                  pl.BlockSpec(wqg.shape, lambda i: (0, 0), pipeline_mode=resident),
                  pl.BlockSpec((tile, dkv.shape[1]), lambda i: (i, 0)),
                  pl.BlockSpec(wkv.shape, lambda i: (0, 0), pipeline_mode=resident),
                  row_spec, row_spec, pl.BlockSpec((tile, 1), lambda i: (i, 0)),
                  pl.BlockSpec((1, dm), lambda i: (0, 0)), ANY],
        out_specs=(row_spec, row_spec, pl.BlockSpec((STAT_ROWS, dm), lambda i: (0, 0))),
        compiler_params=_params("arbitrary"),
    )(dhq, wqg, dkv, wkv, dr2, xhat1, rstd1, gain0, after)


def _adamw_math(w, g, m, v):
    m = ADAM_B1 * m + (1.0 - ADAM_B1) * g
    v = ADAM_B2 * v + (1.0 - ADAM_B2) * (g * g)
    m_hat = m / (1.0 - ADAM_B1 ** ADAM_STEP)
    v_hat = v / (1.0 - ADAM_B2 ** ADAM_STEP)
    delta = -ADAM_LR * (m_hat / (jnp.sqrt(v_hat) + ADAM_EPS) + ADAM_WD * w)
    return delta, m, v


def _sum_devices(ref):
    total = ref[0].astype(F32)
    for d in range(1, ref.shape[0]):
        total = total + ref[d].astype(F32)
    return total


def _adamw_shard(name, parts, w, m, v, after):
    rows, cols = w.shape
    n_parts = len(parts)
    part_rows = rows // n_parts
    tr = _tile(part_rows, max(8, (1 << 18) // cols)) if part_rows >= 8 else part_rows
    per_part = part_rows // tr

    def body(*refs):
        p_refs = refs[:n_parts]
        w_ref, m_ref, v_ref, _, g_out, d_out, m_out, v_out = refs[n_parts:]
        g = _sum_devices(p_refs[0])
        for k in range(1, n_parts):
            g = jnp.where(pl.program_id(0) >= k * per_part, _sum_devices(p_refs[k]), g)
        delta, m_new, v_new = _adamw_math(w_ref[...], g, m_ref[...], v_ref[...])
        g_out[...] = g
        d_out[...] = delta
        m_out[...] = m_new
        v_out[...] = v_new

    def part_spec(k):
        return pl.BlockSpec((parts[k].shape[0], tr, cols),
                            lambda i: (0, jnp.clip(i - k * per_part, 0, per_part - 1), 0))

    spec = pl.BlockSpec((tr, cols), lambda i: (i, 0))
    shape = jax.ShapeDtypeStruct((rows, cols), F32)
    return pl.pallas_call(
        body, name=name, grid=(rows // tr,),
        out_shape=(shape, shape, shape, shape),
        in_specs=[part_spec(k) for k in range(n_parts)] + [spec, spec, spec, ANY],
        out_specs=(spec, spec, spec, spec),
        compiler_params=_params("arbitrary"),
    )(*parts, w, m, v, after)


def _adamw_replicated(stats_b, stats_a, sink_parts, ln_g, ln_b, sinks, m_ln_g, m_ln_b, m_sinks, v_ln_g, v_ln_b,
                      v_sinks, after):
    n_q = sinks.shape[1]
    dm = ln_g.shape[1]

    def body(sb_ref, sa_ref, sk_ref, g_ref, b_ref, s_ref, mg_ref, mb_ref, ms_ref, vg_ref, vb_ref, vs_ref, after_ref,
             *outs):
        del after_ref
        layer_sums = (_sum_devices(sa_ref), _sum_devices(sb_ref))
        outs[12][...] = jnp.sum(layer_sums[1][2:3, :], axis=1, keepdims=True) * (0.5 / dm)
        for which, (w_ref, m_ref, v_ref) in enumerate(((g_ref, mg_ref, vg_ref), (b_ref, mb_ref, vb_ref))):
            for layer in range(DEPTH):
                row = slice(layer, layer + 1)
                g = layer_sums[layer][which:which + 1, :]
                res = (g,) + _adamw_math(w_ref[row, :], g, m_ref[row, :], v_ref[row, :])
                for o_ref, val in zip(outs[4 * which:4 * which + 4], res):
                    o_ref[row, :] = val
        g = _sum_devices(sk_ref)[0:1, 0:n_q]
        res = (g,) + _adamw_math(s_ref[...], g, ms_ref[...], vs_ref[...])
        for o_ref, val in zip(outs[8:12], res):
            o_ref[...] = val

    vmem = pl.BlockSpec(memory_space=pltpu.VMEM)
    shapes = [jax.ShapeDtypeStruct(a.shape, F32) for a in (ln_g, ln_b, sinks) for _ in range(4)]
    shapes.append(jax.ShapeDtypeStruct((1, 1), F32))
    return pl.pallas_call(
        body, name="adamw_replicated", out_shape=tuple(shapes),
        in_specs=[vmem] * 12 + [ANY], out_specs=tuple([vmem] * 13),
    )(stats_b, stats_a, sink_parts, ln_g, ln_b, sinks, m_ln_g, m_ln_b, m_sinks, v_ln_g, v_ln_b, v_sinks, after)


def kernel(x, ln_g, ln_b, a_w_in, a_w_group, a_scale, a_w_out, b_w_k, b_w_v, b_w_qg, b_sinks, b_w_out, loss_target, m_ln_g, m_ln_b, m_a_w_in, m_a_w_group, m_a_scale, m_a_w_out, m_b_w_k, m_b_w_v, m_b_w_qg, m_b_sinks, m_b_w_out, v_ln_g, v_ln_b, v_a_w_in, v_a_w_group, v_a_scale, v_a_w_out, v_b_w_k, v_b_w_v, v_b_w_qg, v_b_sinks, v_b_w_out):
    _, seq, dm = x.shape
    n_groups = len(POOL_WINDOWS)
    gd = dm // n_groups
    kvw = b_w_k.shape[1]
    cb = 2 * dm // N_DEV
    rb = dm // N_DEV
    gb = gd // N_DEV

    x2 = x.reshape(seq, dm)
    target = loss_target.reshape(seq, dm)
    w_in_s = a_w_in.reshape(dm, cb)
    w_g_s = a_w_group.reshape(n_groups, gb, gd)
    w_out_s = a_w_out.reshape(rb, dm)
    w_qg_s = b_w_qg.reshape(dm, cb)
    w_outb_s = b_w_out.reshape(rb, dm)

    def cols(ref, dev):
        return ref.at[:, pl.ds(pl.multiple_of(dev * cb, LANES), cb)]

    def rows(ref, dev):
        return ref.at[pl.ds(pl.multiple_of(dev * rb, 8), rb), :]

    def group_rows(ref, dev):
        return ref.at[:, pl.ds(pl.multiple_of(dev * gb, 8), gb), :]

    def k_rows(ref, dev):
        return ref.at[pl.ds(pl.multiple_of(dev * rb, 8), rb), pl.ds(0, kvw)]

    def v_rows(ref, dev):
        return ref.at[pl.ds(pl.multiple_of(dev * rb, 8), rb), pl.ds(kvw, kvw)]

    def scale_cols(ref, dev):
        return ref.at[:, pl.ds(pl.multiple_of(dev * rb, LANES), rb)]

    bf = lambda a: a.astype(BF16)
    wide, square = jax.ShapeDtypeStruct((dm, 2 * dm), BF16), jax.ShapeDtypeStruct((dm, dm), BF16)
    w_g, scale, w_in = _gather_weights(
        "gather_a_in", 0, [(bf(w_g_s), 0, group_rows), (a_scale, 1, scale_cols), (bf(w_in_s), 2, cols)],
        [jax.ShapeDtypeStruct((n_groups, gd, gd), BF16), jax.ShapeDtypeStruct((1, dm), F32), wide])
    (w_out,) = _gather_weights("gather_a_out", 1, [(bf(w_out_s), 0, rows)], [square])
    w_kv, w_qg = _gather_weights(
        "gather_b_in", 2, [(bf(b_w_k), 0, k_rows), (bf(b_w_v), 0, v_rows), (bf(w_qg_s), 1, cols)],
        [jax.ShapeDtypeStruct((dm, 2 * kvw), BF16), wide])
    (w_outb,) = _gather_weights("gather_b_out", 3, [(bf(w_outb_s), 0, rows)], [square])

    tables = _rope_tables(seq)
    bm = _tile(seq, 1024)
    bn = _tile(dm, 1024)
    g0, g1, b0, b1 = ln_g[0:1], ln_g[1:2], ln_b[0:1], ln_b[1:2]

    xb = _cast_bf16("cast_x", x2)
    y, pooled, mixed, z_a = _pool_forward(xb, w_in, w_g, scale)
    xhat1, rstd1, x1b = _out_proj_norm(y, w_out, x2, g0, b0)

    kd, vd, kt, vt = _kv_proj(x1b, w_kv, tables)
    bmq = bm
    tab_spec = pl.BlockSpec((bmq, LANES), lambda i, j: (i, 0))

    def rope_scale(val, cos_ref, sa_ref, sb_ref):
        cos, sa, sb = cos_ref[...], sa_ref[...], sb_ref[...]
        return jnp.concatenate([_rope(val[:, j * LANES:(j + 1) * LANES], cos, sa, sb) * 0.125
                                for j in range(val.shape[1] // LANES)], axis=1)

    qs = _mm("b_q_proj", x1b, w_qg, dims=NN, grid=(seq // bmq, dm // bn),
             a_spec=pl.BlockSpec((bmq, dm), lambda i, j: (i, 0)), b_spec=pl.BlockSpec((dm, bn), lambda i, j: (0, j)),
             out_shape=jax.ShapeDtypeStruct((seq, dm), BF16), out_spec=pl.BlockSpec((bmq, bn), lambda i, j: (i, j)),
             epilogue=rope_scale, extras=tables, extra_specs=(tab_spec,) * 3)
    zb = _mm("b_gate_proj", x1b, w_qg, dims=NN, grid=(seq // bm, dm // bn),
             a_spec=pl.BlockSpec((bm, dm), lambda i, j: (i, 0)),
             b_spec=pl.BlockSpec((dm, bn), lambda i, j: (0, j + dm // bn)),
             out_shape=jax.ShapeDtypeStruct((seq, dm), F32), out_spec=pl.BlockSpec((bm, bn), lambda i, j: (i, j)))
    att, yb = _attn_forward(qs, kd, vt, zb, b_sinks)
    dr2, dr2b, stats_b = _out_proj_norm_loss(yb, w_outb, xhat1, g0, b0, g1, b1, target)

    def weight_grad(name, a, b, n_cols, b_spec=None, part=(0, 1), after=None):
        m_cols = a.shape[1] // part[1]
        tm, tn = _tile(m_cols, 1024), _tile(n_cols, 512)
        first = part[0] * (m_cols // tm)
        return _mm(name, a, b, dims=TN, grid=(m_cols // tm, n_cols // tn),
                   a_spec=pl.BlockSpec((seq, tm), lambda i, j: (0, first + i)),
                   b_spec=b_spec(tn) if b_spec else pl.BlockSpec((seq, tn), lambda i, j: (0, j)),
                   out_shape=jax.ShapeDtypeStruct((m_cols, n_cols), BF16),
                   out_spec=pl.BlockSpec((tm, tn), lambda i, j: (i, j)),
                   extras=() if after is None else (after,), extra_specs=() if after is None else (ANY,))

    def halves_spec(tn):
        per = dm // tn
        return pl.BlockSpec((None, seq, tn), lambda i, j: (j // per, 0, j % per))

    def times_transposed(name, a, w):
        return _mm(name, a, w, dims=NT, grid=(seq // bm, dm // bn),
                   a_spec=pl.BlockSpec((bm, a.shape[1]), lambda i, j: (i, 0)),
                   b_spec=pl.BlockSpec((bn, w.shape[1]), lambda i, j: (j, 0)),
                   out_shape=jax.ShapeDtypeStruct((seq, dm), F32), out_spec=pl.BlockSpec((bm, bn), lambda i, j: (i, j)))

    def stat_row_cols(ref, dev):
        return ref.at[pl.ds(0, 1), pl.ds(pl.multiple_of(dev * rb, LANES), rb)]

    upd = {}
    last = [dr2b]
    my_core = lax.axis_index("c").astype(jnp.int32).reshape(1)

    def then(value):
        last[0] = value[0] if isinstance(value, (list, tuple)) else value
        return value

    def shard_update(key, parts, w, m, v):
        shape = w.shape
        flat = lambda a: a.reshape(-1, shape[-1])
        parts = list(parts) if isinstance(parts, (list, tuple)) else [parts]
        outs = then(_adamw_shard("adamw_" + key, [p.reshape(p.shape[0], -1, shape[-1]) for p in parts], flat(w),
                                 flat(m), flat(v), last[0]))
        upd[key] = [o.reshape(shape) for o in outs]

    def two_level_scatter(name, ids, streams):
        staged = _sibling_exchange(name + "_pair", streams, ids[0])

        def finish():
            sums = [then(_pair_sum(f"{name}_sum{s}", st[0], got, my_core, last[0]))
                    for s, (st, got) in enumerate(zip(streams, staged))]
            return _chip_exchange(name + "_chip", sums, ids[1])
        return finish

    d_w_outb = then(weight_grad("b_out_proj_dw", yb, dr2b, dm))
    (p_outb,) = _exchange_blocks("scatter_b_out", [(d_w_outb, rows, (rb, dm))], 4)
    dyb = times_transposed("b_out_proj_dx", dr2b, w_outb)
    dhq, dkd, dvd, dsink = then(_attn_backward(qs, kd, vd, kt, zb, att, dyb, b_sinks, tables, after=last[0]))
    dkv = _kv_grad_fold(dkd, dvd, tables)
    d_w_kv = weight_grad("b_kv_proj_dw", x1b, dkv, 2 * kvw)
    d_w_qg = then(weight_grad("b_qg_proj_dw", x1b, dhq, 2 * dm, halves_spec, after=d_w_kv))
    finish_b_in = two_level_scatter("scatter_b_in", (5, 11), [(d_w_qg, cols, (dm, cb))])
    dr1, dr1b, stats_a = _stream_grad_norm_backward(dhq, w_qg, dkv, w_kv, dr2, xhat1, rstd1, g0, after=last[0])
    last[0] = dr1b
    shard_update("b_w_out", p_outb, b_w_out, m_b_w_out, v_b_w_out)
    (p_qg,) = finish_b_in()
    all_b, all_a, all_sink = _exchange_blocks("gather_replicated_grads", [
        (stats_b, None, stats_b.shape), (stats_a, None, stats_a.shape), (dsink, None, dsink.shape)], 9)

    d_w_out = then(weight_grad("a_out_proj_dw", y, dr1b, dm, after=last[0]))
    p_out, p_k, p_v = _exchange_blocks("scatter_a_out", [
        (d_w_out, rows, (rb, dm)), (d_w_kv, k_rows, (rb, kvw)), (d_w_kv, v_rows, (rb, kvw))], 6)
    dy = times_transposed("a_out_proj_dx", dr1b, w_out)
    dh, d_w_g, stats_s = then(_pool_mid_backward(dy, mixed, z_a, pooled, w_g, scale, after=last[0]))
    p_g, p_scale = _exchange_blocks("scatter_a_mid", [
        (d_w_g, group_rows, (n_groups, gb, gd)), (stats_s, stat_row_cols, (1, rb))], 7)
    shard_update("b_w_qg", p_qg, b_w_qg, m_b_w_qg, v_b_w_qg)
    rep = then(_adamw_replicated(all_b, all_a, all_sink, ln_g, ln_b, b_sinks, m_ln_g, m_ln_b, m_b_sinks, v_ln_g,
                                 v_ln_b, v_b_sinks, last[0]))
    upd["ln_g"], upd["ln_b"], upd["b_sinks"] = list(rep[0:4]), list(rep[4:8]), list(rep[8:12])
    finish_a_in = []
    for k in range(2):
        d_w_in = then(weight_grad(f"a_in_proj_dw_{k}", xb, dh, 2 * dm, halves_spec, part=(k, 2), after=last[0]))
        finish_a_in.append(two_level_scatter(f"scatter_a_in_{k}", (8 + 2 * k, 12 + k), [(d_w_in, cols, (dm // 2, cb))]))
    shard_update("a_w_out", p_out, a_w_out, m_a_w_out, v_a_w_out)
    shard_update("b_w_k", p_k, b_w_k, m_b_w_k, v_b_w_k)
    shard_update("b_w_v", p_v, b_w_v, m_b_w_v, v_b_w_v)
    shard_update("a_w_group", p_g, a_w_group, m_a_w_group, v_a_w_group)
    shard_update("a_scale", p_scale, a_scale, m_a_scale, v_a_scale)
    p_in = list(finish_a_in[0]()) + list(finish_a_in[1]())
    grad_x = then(_mm("a_in_proj_dx", dh, w_in, dims=NT, grid=(seq // bm, dm // bn, 2), nk=2,
                      a_spec=pl.BlockSpec((None, bm, dm), lambda i, j, k: (k, i, 0)),
                      b_spec=pl.BlockSpec((bn, dm), lambda i, j, k: (j, k)),
                      out_shape=jax.ShapeDtypeStruct((seq, dm), F32),
                      out_spec=pl.BlockSpec((bm, bn), lambda i, j, k: (i, j)),
                      add=dr1, add_spec=pl.BlockSpec((bm, bn), lambda i, j, k: (i, j)), add_scale=ALPHA,
                      extras=(last[0],), extra_specs=(ANY,)))
    shard_update("a_w_in", p_in, a_w_in, m_a_w_in, v_a_w_in)

    loss = rep[12].reshape(())
    order = ["ln_g", "ln_b", "a_w_in", "a_w_group", "a_scale", "a_w_out", "b_w_k", "b_w_v", "b_w_qg", "b_sinks",
             "b_w_out"]
    return (loss, grad_x.reshape(x.shape), *[upd[n][0] for n in order], *[upd[n][1] for n in order],
            *[upd[n][2] for n in order], *[upd[n][3] for n in order])
```

```python
import functools

import jax
import jax.numpy as jnp
from jax import lax
from jax.experimental import pallas as pl
from jax.experimental.pallas import tpu as pltpu
from jax.experimental.pallas import tpu_sc as plsc

F32 = jnp.float32
BF16 = jnp.bfloat16
MESH = pl.DeviceIdType.MESH
AXES = ("x", "y", "c")
N_DEV = 8

POOL_WINDOWS = (2, 4, 8, 16)
POOL_HALO = 16
HEAD_DIM = 64
GQA_GROUP = 8
ATTN_BLOCK = 128
ROPE_THETA = 10000.0
LN_EPS = 1e-5
NEG_INF = -1e30
DEPTH = 2
ALPHA = (2 * DEPTH) ** 0.25
ADAM_LR = 0.001
ADAM_B1 = 0.9
ADAM_B2 = 0.999
ADAM_EPS = 1e-08
ADAM_WD = 0.01
ADAM_STEP = 10

LANES = 128
STAT_ROWS = 8


def _tile(n, want):
    t = min(n, want)
    while n % t:
        t //= 2
    return t


def _params(*sem):
    return pltpu.CompilerParams(dimension_semantics=sem)


ANY = pl.BlockSpec(memory_space=pl.ANY)


def _my_pos():
    return lax.axis_index("x"), lax.axis_index("y"), lax.axis_index("c")


def _dev_index(p):
    return 4 * p[0] + 2 * p[1] + p[2]


def _handshake(peers):
    barrier = pltpu.get_barrier_semaphore()
    for peer in peers:
        pl.semaphore_signal(barrier, inc=1, device_id=peer, device_id_type=MESH)
    pl.semaphore_wait(barrier, len(peers))


def _launch_on_sequencer(name, collective_id, body, operands, out_shapes, scratch):
    return pl.kernel(
        body, out_type=tuple(out_shapes), name=name,
        mesh=plsc.ScalarSubcoreMesh(axis_name="sequencer", num_cores=1), scratch_types=scratch,
        compiler_params=pltpu.CompilerParams(collective_id=collective_id),
    )(*operands)


def _gather_weights(name, collective_id, streams, out_shapes):
    n_s = len(streams)
    n_out = len(out_shapes)

    def body(*refs):
        srcs = refs[:n_s]
        outs = refs[n_s:n_s + n_out]
        send_sems, recv_sems, local_sems = refs[n_s + n_out:]
        x, y, c = _my_pos()
        me, sibling = (x, y, c), (x, y, 1 - c)
        x_nbr, y_nbr, diag = (1 - x, y), (x, 1 - y), (1 - x, 1 - y)
        _handshake([sibling, (*x_nbr, c), (*y_nbr, c)])
        south = c == 0
        relay_from = (jnp.where(south, 1 - x, x), jnp.where(south, y, 1 - y))
        relay_to = (jnp.where(south, x, 1 - x), jnp.where(south, 1 - y, y))
        early, late = jnp.where(south, 1, 2), jnp.where(south, 2, 1)

        def copy(s, k, block, to, from_shard=False):
            out_ref = outs[streams[s][1]]
            win = streams[s][2](out_ref, _dev_index(block))
            return pltpu.make_async_remote_copy(
                src_ref=srcs[s] if from_shard else win, dst_ref=win,
                send_sem=send_sems.at[7 * s + k], recv_sem=recv_sems.at[7 * s + k],
                device_id=to, device_id_type=MESH)

        mine = [pltpu.make_async_copy(srcs[s], streams[s][2](outs[streams[s][1]], _dev_index(me)), local_sems.at[s])
                for s in range(n_s)]
        for cp in mine:
            cp.start()
        sent = []
        for s in range(n_s):
            sent += [copy(s, 0, me, sibling, True), copy(s, 1, me, (*x_nbr, c), True), copy(s, 2, me, (*y_nbr, c), True)]
        for cp in sent:
            cp.start()
        for s in range(n_s):
            copy(s, early, (*relay_from, c), me).wait_recv()
            sent += [copy(s, 3, (*relay_from, c), (*relay_to, c)), copy(s, 3 + early, (*relay_from, c), sibling)]
            for cp in sent[-2:]:
                cp.start()
        for s in range(n_s):
            copy(s, late, (*relay_to, c), me).wait_recv()
            sent.append(copy(s, 3 + late, (*relay_to, c), sibling))
            sent[-1].start()
        for s in range(n_s):
            copy(s, 3, (*diag, c), me).wait_recv()
            sent.append(copy(s, 6, (*diag, c), sibling))
            sent[-1].start()
        for s in range(n_s):
            copy(s, 0, sibling, me).wait_recv()
            for k, chip in ((4, x_nbr), (5, y_nbr), (6, diag)):
                copy(s, k, (*chip, 1 - c), me).wait_recv()
        for cp in sent:
            cp.wait_send()
        for cp in mine:
            cp.wait()

    scratch = [pltpu.SemaphoreType.DMA((7 * n_s,)), pltpu.SemaphoreType.DMA((7 * n_s,)),
               pltpu.SemaphoreType.DMA((n_s,))]
    return _launch_on_sequencer(name, collective_id, body, [s[0] for s in streams], out_shapes, scratch)


def _exchange_blocks(name, streams, collective_id):
    n_s = len(streams)

    def body(*refs):
        srcs = refs[:n_s]
        outs = refs[n_s:2 * n_s]
        send_sems, recv_sems, local_sems = refs[2 * n_s:]
        x, y, c = _my_pos()
        me = _dev_index((x, y, c))
        _handshake([(1 - x if k & 4 else x, 1 - y if k & 2 else y, 1 - c if k & 1 else c) for k in range(1, N_DEV)])

        def window(s, dev):
            return srcs[s] if streams[s][1] is None else streams[s][1](srcs[s], dev)

        mine = [pltpu.make_async_copy(window(s, me), outs[s].at[me], local_sems.at[s]) for s in range(n_s)]
        for cp in mine:
            cp.start()
        copies = []
        for k in (2, 4, 6, 3, 5, 7, 1):
            peer = (1 - x if k & 4 else x, 1 - y if k & 2 else y, 1 - c if k & 1 else c)
            for s in range(n_s):
                copies.append(pltpu.make_async_remote_copy(
                    src_ref=window(s, _dev_index(peer)), dst_ref=outs[s].at[me],
                    send_sem=send_sems.at[7 * s + k - 1], recv_sem=recv_sems.at[7 * s + k - 1],
                    device_id=peer, device_id_type=MESH))
        for cp in copies:
            cp.start()
        for cp in copies:
            cp.wait()
        for cp in mine:
            cp.wait()

    out_shapes = [jax.ShapeDtypeStruct((N_DEV,) + tuple(s[2]), s[0].dtype) for s in streams]
    scratch = [pltpu.SemaphoreType.DMA((7 * n_s,)), pltpu.SemaphoreType.DMA((7 * n_s,)),
               pltpu.SemaphoreType.DMA((n_s,))]
    return _launch_on_sequencer(name, collective_id, body, [s[0] for s in streams], out_shapes, scratch)


N_CHIPS = 4


def _sibling_exchange(name, streams, collective_id):
    n_s = len(streams)

    def body(*refs):
        srcs = refs[:n_s]
        outs = refs[n_s:2 * n_s]
        send_sems, recv_sems = refs[2 * n_s:]
        x, y, c = _my_pos()
        sibling = (x, y, 1 - c)
        _handshake([sibling])
        copies = [pltpu.make_async_remote_copy(
            src_ref=streams[s][1](srcs[s], 2 * chip + (1 - c)), dst_ref=outs[s].at[chip],
            send_sem=send_sems.at[N_CHIPS * s + chip], recv_sem=recv_sems.at[N_CHIPS * s + chip],
            device_id=sibling, device_id_type=MESH) for s in range(n_s) for chip in range(N_CHIPS)]
        for cp in copies:
            cp.start()
        for cp in copies:
            cp.wait()

    out_shapes = [jax.ShapeDtypeStruct((N_CHIPS,) + tuple(s[2]), s[0].dtype) for s in streams]
    scratch = [pltpu.SemaphoreType.DMA((N_CHIPS * n_s,)), pltpu.SemaphoreType.DMA((N_CHIPS * n_s,))]
    return _launch_on_sequencer(name, collective_id, body, [s[0] for s in streams], out_shapes, scratch)


def _pair_sum(name, array, from_sibling, my_core, after):
    _, rows, cols = from_sibling.shape
    tr = _tile(rows, 512)

    def body(core_ref, own_ref, sib_ref, after_ref, o_ref):
        del core_ref, after_ref
        o_ref[...] = (own_ref[...].astype(F32) + sib_ref[...].astype(F32)).astype(o_ref.dtype)

    staged_spec = pl.BlockSpec((None, tr, cols), lambda k, i, core: (k, i, 0))
    return pl.pallas_call(
        body, name=name, out_shape=jax.ShapeDtypeStruct(from_sibling.shape, array.dtype),
        grid_spec=pltpu.PrefetchScalarGridSpec(
            num_scalar_prefetch=1, grid=(N_CHIPS, rows // tr),
            in_specs=[pl.BlockSpec((tr, cols), lambda k, i, core: (i, 2 * k + core[0])), staged_spec, ANY],
            out_specs=staged_spec),
        compiler_params=_params("parallel", "parallel"),
    )(my_core, array, from_sibling, after)


def _chip_exchange(name, pair_sums, collective_id):
    n_s = len(pair_sums)

    def body(*refs):
        srcs = refs[:n_s]
        outs = refs[n_s:2 * n_s]
        send_sems, recv_sems, local_sems = refs[2 * n_s:]
        x, y, c = _my_pos()
        my_chip = 2 * x + y
        chips = [(1 - x, y), (x, 1 - y), (1 - x, 1 - y)]
        _handshake([(*chip, c) for chip in chips])
        mine = [pltpu.make_async_copy(srcs[s].at[my_chip], outs[s].at[my_chip], local_sems.at[s]) for s in range(n_s)]
        copies = [pltpu.make_async_remote_copy(
            src_ref=srcs[s].at[2 * chip[0] + chip[1]], dst_ref=outs[s].at[my_chip],
            send_sem=send_sems.at[3 * s + j], recv_sem=recv_sems.at[3 * s + j],
            device_id=(*chip, c), device_id_type=MESH) for s in range(n_s) for j, chip in enumerate(chips)]
        for cp in mine + copies:
            cp.start()
        for cp in copies:
            cp.wait()
        for cp in mine:
            cp.wait()

    out_shapes = [jax.ShapeDtypeStruct(p.shape, p.dtype) for p in pair_sums]
    scratch = [pltpu.SemaphoreType.DMA((3 * n_s,)), pltpu.SemaphoreType.DMA((3 * n_s,)),
               pltpu.SemaphoreType.DMA((n_s,))]
    return _launch_on_sequencer(name, collective_id, body, list(pair_sums), out_shapes, scratch)


NN = (((1,), (0,)), ((), ()))
NT = (((1,), (1,)), ((), ()))
TN = (((0,), (0,)), ((), ()))


def _mm(name, a, b, *, dims, grid, a_spec, b_spec, out_shape, out_spec, nk=1,
        add=None, add_spec=None, add_scale=1.0, epilogue=None, extras=(), extra_specs=()):
    n_extra = len(extras)
    has_add = add is not None

    def body(*refs):
        a_ref, b_ref = refs[:2]
        pos = 2
        add_ref = None
        if has_add:
            add_ref = refs[pos]
            pos += 1
        extra_refs = refs[pos:pos + n_extra]
        o_ref = refs[pos + n_extra]
        acc_ref = refs[pos + n_extra + 1] if nk > 1 else None

        def finish(val):
            if has_add:
                val = val + add_scale * add_ref[...]
            if epilogue is not None:
                val = epilogue(val, *extra_refs)
            o_ref[...] = val.astype(o_ref.dtype)

        part = lax.dot_general(a_ref[...].astype(BF16), b_ref[...].astype(BF16), dims,
                               preferred_element_type=F32)
        if nk == 1:
            finish(part)
        else:
            k = pl.program_id(2)

            @pl.when(k == 0)
            def _():
                acc_ref[...] = part

            @pl.when(jnp.logical_and(k > 0, k < nk - 1))
            def _():
                acc_ref[...] += part

            @pl.when(k == nk - 1)
            def _():
                finish(acc_ref[...] + part)

    in_specs = [a_spec, b_spec] + ([add_spec] if has_add else []) + list(extra_specs)
    operands = [a, b] + ([add] if has_add else []) + list(extras)
    scratch = [pltpu.VMEM(out_spec.block_shape, F32)] if nk > 1 else []
    sem = ("parallel", "parallel") + (("arbitrary",) if nk > 1 else ())
    return pl.pallas_call(
        body, name=name, grid=grid, out_shape=out_shape,
        in_specs=in_specs, out_specs=out_spec, scratch_shapes=scratch,
        compiler_params=_params(*sem),
    )(*operands)


def _cast_bf16(name, a):
    rows, cols = a.shape
    tr = _tile(rows, 512)

    def body(a_ref, o_ref):
        o_ref[...] = a_ref[...].astype(BF16)

    return pl.pallas_call(
        body, name=name, grid=(rows // tr,),
        out_shape=jax.ShapeDtypeStruct(a.shape, BF16),
        in_specs=[pl.BlockSpec((tr, cols), lambda i: (i, 0))],
        out_specs=pl.BlockSpec((tr, cols), lambda i: (i, 0)),
        compiler_params=_params("parallel"),
    )(a)


def _rope_tables(seq):
    inv_freq = ROPE_THETA ** (-jnp.arange(0, HEAD_DIM, 2, dtype=F32) / HEAD_DIM)
    ang = jnp.arange(seq, dtype=F32)[:, None] * inv_freq[None, :]
    cos, sin = jnp.cos(ang), jnp.sin(ang)
    cos, sin = (jnp.concatenate([t, t, t, t], axis=-1) for t in (cos, sin))
    first_half = (jnp.arange(LANES) % HEAD_DIM < HEAD_DIM // 2)[None, :]
    return cos, jnp.where(first_half, -sin, 0.0), jnp.where(first_half, 0.0, sin)


def _rot(t, sin_a, sin_b):
    return pltpu.roll(t, LANES - HEAD_DIM // 2, 1) * sin_a + pltpu.roll(t, HEAD_DIM // 2, 1) * sin_b


def _rope(t, cos, sin_a, sin_b):
    return t * cos + _rot(t, sin_a, sin_b)


def _rope_transposed(dy, cos, sin_a, sin_b):
    return dy * cos - _rot(dy, sin_a, sin_b)


def _silu_parts(z):
    sig = jax.nn.sigmoid(z)
    return z * sig, sig * (1.0 + z * (1.0 - sig))


def _layer_norm_stats(r):
    mu = jnp.mean(r, axis=-1, keepdims=True)
    d = r - mu
    var = jnp.mean(d * d, axis=-1, keepdims=True)
    rstd = lax.rsqrt(var + LN_EPS)
    return d * rstd, rstd


def _layer_norm_backward(dout, xhat, rstd, gain):
    dxh = dout * gain
    m1 = jnp.mean(dxh, axis=-1, keepdims=True)
    m2 = jnp.mean(dxh * xhat, axis=-1, keepdims=True)
    return rstd * (dxh - m1 - xhat * m2)


def _col_sum(v):
    return jnp.sum(v, axis=0, keepdims=True)


def _pool_forward(xb, w_in, wg, scale):
    seq, dm = xb.shape
    n_g = len(POOL_WINDOWS)
    gd = dm // n_g
    tile = _tile(seq, 1024)
    halo_blocks = tile // POOL_HALO

    def body(x_ref, xp_ref, wu_ref, wz_ref, wg_ref, sc_ref, y_ref, p_ref, mx_ref, z_ref):
        i, g = pl.program_id(0), pl.program_id(1)
        u = jnp.dot(x_ref[...], wu_ref[...], preferred_element_type=F32)
        z = jnp.dot(x_ref[...], wz_ref[...], preferred_element_type=F32)
        prev = jnp.where(i > 0, jnp.dot(xp_ref[...], wu_ref[...], preferred_element_type=F32), 0.0)
        s = jnp.concatenate([prev, u], axis=0)
        sums, sh = [], 1
        while sh < POOL_WINDOWS[-1]:
            s = s + pltpu.roll(s, sh, 0)
            sums.append(s)
            sh *= 2
        win = sums[-1]
        for k in range(n_g - 2, -1, -1):
            win = jnp.where(g == k, sums[k], win)
        row = i * tile + lax.broadcasted_iota(jnp.int32, (tile, 1), 0)
        window = jnp.left_shift(2, g).astype(F32)
        p = win[POOL_HALO:, :] * (1.0 / jnp.minimum((row + 1).astype(F32), window)) - u
        pb = p.astype(BF16)
        mx = jnp.dot(pb, wg_ref[...], preferred_element_type=F32)
        y_ref[...] = (mx * sc_ref[...] * (z * jax.nn.sigmoid(z))).astype(BF16)
        p_ref[...] = pb
        mx_ref[...] = mx
        z_ref[...] = z

    out_spec = pl.BlockSpec((tile, gd), lambda i, g: (i, g))
    return pl.pallas_call(
        body, name="pool_fwd", grid=(seq // tile, n_g),
        out_shape=(jax.ShapeDtypeStruct((seq, dm), BF16), jax.ShapeDtypeStruct((seq, dm), BF16),
                   jax.ShapeDtypeStruct((seq, dm), F32), jax.ShapeDtypeStruct((seq, dm), F32)),
        in_specs=[pl.BlockSpec((tile, dm), lambda i, g: (i, 0)),
                  pl.BlockSpec((POOL_HALO, dm), lambda i, g: (jnp.maximum(i * halo_blocks - 1, 0), 0)),
                  pl.BlockSpec((dm, gd), lambda i, g: (0, g)),
                  pl.BlockSpec((dm, gd), lambda i, g: (0, n_g + g)),
                  pl.BlockSpec((None, gd, gd), lambda i, g: (g, 0, 0)),
                  pl.BlockSpec((1, gd), lambda i, g: (0, g))],
        out_specs=(out_spec, out_spec, out_spec, out_spec),
        compiler_params=_params("parallel", "parallel"),
    )(xb, xb, w_in, w_in, wg, scale)


def _pool_mid_backward(dy, mx, z, p, wg, scale, after):
    seq, dm = dy.shape
    gd = dm // len(POOL_WINDOWS)
    tile = _tile(seq, 256)
    n_i = seq // tile

    def body(dy_ref, mx_ref, z_ref, p_ref, wg_ref, sc_ref, after_ref, dh_ref, dwg_ref, st_ref, dwg_acc, carry):
        del after_ref
        i = pl.program_id(0)
        ti = n_i - 1 - i

        @pl.when(i == 0)
        def _():
            dwg_acc[...] = jnp.zeros_like(dwg_acc)
            carry[...] = jnp.zeros_like(carry)
            st_ref[...] = jnp.zeros_like(st_ref)

        row = ti * tile + lax.broadcasted_iota(jnp.int32, (tile, 1), 0)
        count = (row + 1).astype(F32)
        for g, w in enumerate(POOL_WINDOWS):
            cs = slice(g * gd, (g + 1) * gd)
            z = z_ref[:, cs]
            sz, dsz = _silu_parts(z)
            dyg = dy_ref[:, cs]
            mxg = mx_ref[:, cs]
            sc = sc_ref[:, cs]
            t1 = dyg * sz
            st_ref[0:1, cs] += _col_sum(t1 * mxg)
            dh_ref[1, :, cs] = (dyg * (mxg * sc) * dsz).astype(BF16)
            dmx = (t1 * sc).astype(BF16)
            dwg_acc[g] += lax.dot_general(p_ref[:, cs], dmx, TN, preferred_element_type=F32)
            dp = lax.dot_general(dmx, wg_ref[g], NT, preferred_element_type=F32)
            e = dp * (1.0 / jnp.minimum(count, float(w)))
            s = jnp.concatenate([e, carry[:, cs]], axis=0)
            n = tile + POOL_HALO
            sh = 1
            while sh < w:
                s = s + pltpu.roll(s, n - sh, 0)
                sh *= 2
            dh_ref[0, :, cs] = (s[:tile, :] - dp).astype(BF16)
            carry[:, cs] = e[:POOL_HALO, :]

        @pl.when(i == n_i - 1)
        def _():
            dwg_ref[...] = dwg_acc[...].astype(BF16)

    row_spec = pl.BlockSpec((tile, dm), lambda i: (n_i - 1 - i, 0))
    return pl.pallas_call(
        body, name="pool_mid_bwd", grid=(n_i,),
        out_shape=(jax.ShapeDtypeStruct((2, seq, dm), BF16), jax.ShapeDtypeStruct(wg.shape, BF16),
                   jax.ShapeDtypeStruct((STAT_ROWS, dm), F32)),
        in_specs=[row_spec, row_spec, row_spec, row_spec,
                  pl.BlockSpec(wg.shape, lambda i: (0, 0, 0)),
                  pl.BlockSpec((1, dm), lambda i: (0, 0)), ANY],
        out_specs=(pl.BlockSpec((2, tile, dm), lambda i: (0, n_i - 1 - i, 0)),
                   pl.BlockSpec(wg.shape, lambda i: (0, 0, 0)),
                   pl.BlockSpec((STAT_ROWS, dm), lambda i: (0, 0))),
        scratch_shapes=[pltpu.VMEM(wg.shape, F32), pltpu.VMEM((POOL_HALO, dm), F32)],
        compiler_params=_params("arbitrary"),
    )(dy, mx, z, p, wg, scale, after)


def _out_proj_norm(y, w, x, gain, bias):
    seq, dm = x.shape
    tile = _tile(seq, 512)

    def body(y_ref, w_ref, x_ref, g_ref, b_ref, xhat_ref, rstd_ref, xb_ref):
        o = jnp.dot(y_ref[...], w_ref[...], preferred_element_type=F32)
        xhat, rstd = _layer_norm_stats(ALPHA * x_ref[...] + o)
        xhat_ref[...] = xhat
        rstd_ref[...] = rstd
        xb_ref[...] = (xhat * g_ref[...] + b_ref[...]).astype(BF16)

    row_spec = pl.BlockSpec((tile, dm), lambda i: (i, 0))
    vec_spec = pl.BlockSpec((1, dm), lambda i: (0, 0))
    return pl.pallas_call(
        body, name="out_proj_norm_a", grid=(seq // tile,),
        out_shape=(jax.ShapeDtypeStruct((seq, dm), F32), jax.ShapeDtypeStruct((seq, 1), F32),
                   jax.ShapeDtypeStruct((seq, dm), BF16)),
        in_specs=[row_spec, pl.BlockSpec(w.shape, lambda i: (0, 0), pipeline_mode=pl.Buffered(1)), row_spec, vec_spec,
                  vec_spec],
        out_specs=(row_spec, pl.BlockSpec((tile, 1), lambda i: (i, 0)), row_spec),
        compiler_params=_params("parallel"),
    )(y, w, x, gain, bias)


def _kv_proj(xb, wkv, tables):
    seq, dm = xb.shape
    kvw = wkv.shape[1] // 2
    n_kv = kvw // HEAD_DIM
    tile = _tile(seq, 512)

    def body(x_ref, w_ref, cos_ref, sa_ref, sb_ref, kd_ref, vd_ref, kt_ref, vt_ref):
        kv = jnp.dot(x_ref[...], w_ref[...], preferred_element_type=F32)
        low = lax.broadcasted_iota(jnp.int32, (1, LANES), 1) < HEAD_DIM
        cos, sa, sb = cos_ref[...], sa_ref[...], sb_ref[...]

        def put(pair, h, nat_ref, t_ref):
            swapped = pltpu.roll(pair, HEAD_DIM, 1)
            for head, dup in ((h, jnp.where(low, pair, swapped)), (h + 1, jnp.where(low, swapped, pair))):
                nat_ref[head] = dup.astype(BF16)
                t_ref[head] = dup.T.astype(BF16)

        for j in range(kvw // LANES):
            put(_rope(kv[:, j * LANES:(j + 1) * LANES], cos, sa, sb), 2 * j, kd_ref, kt_ref)
            put(kv[:, kvw + j * LANES:kvw + (j + 1) * LANES], 2 * j, vd_ref, vt_ref)

    tab_spec = pl.BlockSpec((tile, LANES), lambda i: (i, 0))
    dup_spec = pl.BlockSpec((n_kv, tile, LANES), lambda i: (0, i, 0))
    dup_shape = jax.ShapeDtypeStruct((n_kv, seq, LANES), BF16)
    t_spec = pl.BlockSpec((n_kv, LANES, tile), lambda i: (0, 0, i))
    t_shape = jax.ShapeDtypeStruct((n_kv, LANES, seq), BF16)
    return pl.pallas_call(
        body, name="kv_proj", grid=(seq // tile,),
        out_shape=(dup_shape, dup_shape, t_shape, t_shape),
        in_specs=[pl.BlockSpec((tile, dm), lambda i: (i, 0)), pl.BlockSpec(wkv.shape, lambda i: (0, 0)),
                  tab_spec, tab_spec, tab_spec],
        out_specs=(dup_spec, dup_spec, t_spec, t_spec),
        compiler_params=_params("parallel"),
    )(xb, wkv, *tables)


def _head_queries(q_ref, low):
    parts = []
    for j in range(GQA_GROUP // 2):
        q2 = q_ref[:, j * LANES:(j + 1) * LANES]
        parts += [jnp.where(low, q2, 0), jnp.where(low, 0, q2)]
    return parts


def _probs_transposed(n, kh, kcat, q_all, sink_ref):
    st = lax.dot_general(kcat, q_all, NT, preferred_element_type=F32)
    key = lax.broadcasted_iota(jnp.int32, (2 * ATTN_BLOCK, ATTN_BLOCK), 0)
    qry = lax.broadcasted_iota(jnp.int32, (2 * ATTN_BLOCK, ATTN_BLOCK), 1)
    valid = (key > qry) & (key <= qry + ATTN_BLOCK) & ((key >= ATTN_BLOCK) | (n > 0))
    st = st + jnp.tile(jnp.where(valid, 0.0, NEG_INF), (1, GQA_GROUP))
    sink = jnp.concatenate([jnp.full((1, ATTN_BLOCK), sink_ref[0, kh * GQA_GROUP + h], F32)
                            for h in range(GQA_GROUP)], axis=1)
    m = jnp.maximum(jnp.max(st, axis=0, keepdims=True), sink)
    e = jnp.exp(st - m)
    e_sink = jnp.exp(sink - m)
    inv = 1.0 / (jnp.sum(e, axis=0, keepdims=True) + e_sink)
    return e * inv, e_sink * inv


def _attn_specs(n_width):
    q_spec = pl.BlockSpec((ATTN_BLOCK, n_width), lambda kh, n: (n, kh))
    cur = pl.BlockSpec((None, ATTN_BLOCK, LANES), lambda kh, n: (kh, n, 0))
    prev = pl.BlockSpec((None, ATTN_BLOCK, LANES), lambda kh, n: (kh, jnp.maximum(n - 1, 0), 0))
    cur_t = pl.BlockSpec((None, LANES, ATTN_BLOCK), lambda kh, n: (kh, 0, n))
    prev_t = pl.BlockSpec((None, LANES, ATTN_BLOCK), lambda kh, n: (kh, 0, jnp.maximum(n - 1, 0)))
    return q_spec, cur, prev, cur_t, prev_t


def _pair_product_transposed(mat_t, rhs, j, low_rows):
    a = rhs[:, 2 * j * ATTN_BLOCK:(2 * j + 1) * ATTN_BLOCK]
    b = rhs[:, (2 * j + 1) * ATTN_BLOCK:(2 * j + 2) * ATTN_BLOCK]
    out_t = (jnp.dot(jnp.where(low_rows, mat_t, 0), a, preferred_element_type=F32)
             + jnp.dot(jnp.where(low_rows, 0, mat_t), b, preferred_element_type=F32))
    return out_t.T


def _attn_forward(qs, kd, vt, zb, sinks):
    seq, dm = qs.shape
    n_kv = kd.shape[0]
    gw = GQA_GROUP * HEAD_DIM

    def body(q_ref, kp_ref, kc_ref, vtp_ref, vtc_ref, z_ref, sink_ref, att_ref, yb_ref):
        kh, n = pl.program_id(0), pl.program_id(1)
        low = lax.broadcasted_iota(jnp.int32, (1, LANES), 1) < HEAD_DIM
        low_rows = lax.broadcasted_iota(jnp.int32, (LANES, 1), 0) < HEAD_DIM
        kcat = jnp.concatenate([kp_ref[...], kc_ref[...]], axis=0)
        vt = jnp.concatenate([vtp_ref[...], vtc_ref[...]], axis=1)
        q_all = jnp.concatenate(_head_queries(q_ref, low), axis=0)
        probs_t, _ = _probs_transposed(n, kh, kcat, q_all, sink_ref)
        pt = probs_t.astype(BF16)
        for j in range(GQA_GROUP // 2):
            cs = slice(j * LANES, (j + 1) * LANES)
            o2 = _pair_product_transposed(vt, pt, j, low_rows)
            att_ref[:, cs] = o2
            z = z_ref[:, cs]
            yb_ref[:, cs] = (o2 * (z * jax.nn.sigmoid(z))).astype(BF16)

    q_spec, cur, prev, cur_t, prev_t = _attn_specs(gw)
    return pl.pallas_call(
        body, name="attn_fwd", grid=(n_kv, seq // ATTN_BLOCK),
        out_shape=(jax.ShapeDtypeStruct((seq, dm), F32), jax.ShapeDtypeStruct((seq, dm), BF16)),
        in_specs=[q_spec, prev, cur, prev_t, cur_t, q_spec, pl.BlockSpec(memory_space=pltpu.SMEM)],
        out_specs=(q_spec, q_spec),
        compiler_params=_params("parallel", "parallel"),
    )(qs, kd, kd, vt, vt, zb, sinks)


def _attn_backward(qs, kd, vd, kt, zb, att, dyb, sinks, tables, after):
    seq, dm = qs.shape
    n_kv = kd.shape[0]
    gw = GQA_GROUP * HEAD_DIM
    n_blocks = seq // ATTN_BLOCK

    def body(q_ref, kp_ref, kc_ref, vp_ref, vc_ref, ktp_ref, ktc_ref, z_ref, att_ref, dyb_ref, sink_ref,
             cos_ref, sa_ref, sb_ref, after_ref, dh_ref, dk_ref, dv_ref, ds_ref):
        del after_ref
        kh, n = pl.program_id(0), pl.program_id(1)

        @pl.when(n == 0)
        def _():
            dk_ref[...] = jnp.zeros_like(dk_ref)
            dv_ref[...] = jnp.zeros_like(dv_ref)

        @pl.when(jnp.logical_and(n == 0, kh == 0))
        def _():
            ds_ref[...] = jnp.zeros_like(ds_ref)

        low = lax.broadcasted_iota(jnp.int32, (1, LANES), 1) < HEAD_DIM
        low_rows = lax.broadcasted_iota(jnp.int32, (LANES, 1), 0) < HEAD_DIM
        head_lane = lax.broadcasted_iota(jnp.int32, (1, LANES), 1)
        kcat = jnp.concatenate([kp_ref[...], kc_ref[...]], axis=0)
        vcat = jnp.concatenate([vp_ref[...], vc_ref[...]], axis=0)
        kt = jnp.concatenate([ktp_ref[...], ktc_ref[...]], axis=1)
        cos, sa, sb = cos_ref[...], sa_ref[...], sb_ref[...]
        q_parts = _head_queries(q_ref, low)
        q_all = jnp.concatenate(q_parts, axis=0)
        d_parts = []
        for j in range(GQA_GROUP // 2):
            cs = slice(j * LANES, (j + 1) * LANES)
            sz, dsz = _silu_parts(z_ref[:, cs])
            dy2 = dyb_ref[:, cs]
            dh_ref[1, :, cs] = (dy2 * att_ref[:, cs] * dsz).astype(BF16)
            datt = (dy2 * sz).astype(BF16)
            d_parts += [jnp.where(low, datt, 0), jnp.where(low, 0, datt)]
        d_all = jnp.concatenate(d_parts, axis=0)
        probs_t, sink_p = _probs_transposed(n, kh, kcat, q_all, sink_ref)
        dprobs_t = lax.dot_general(vcat, d_all, NT, preferred_element_type=F32)
        row_dot = jnp.sum(probs_t * dprobs_t, axis=0, keepdims=True)
        ds_t = (probs_t * (dprobs_t - row_dot)).astype(BF16)
        dk = jnp.dot(ds_t, q_all, preferred_element_type=F32)
        dv = jnp.dot(probs_t.astype(BF16), d_all, preferred_element_type=F32)
        for j in range(GQA_GROUP // 2):
            dq2 = _pair_product_transposed(kt, ds_t, j, low_rows)
            dh_ref[0, :, j * LANES:(j + 1) * LANES] = (_rope_transposed(dq2, cos, sa, sb) * 0.125).astype(BF16)
        sink_dot = sink_p * row_dot
        dsink = jnp.zeros((1, LANES), F32)
        for h in range(GQA_GROUP):
            part = jnp.sum(sink_dot[:, h * ATTN_BLOCK:(h + 1) * ATTN_BLOCK], axis=1, keepdims=True)
            dsink = dsink - jnp.where(head_lane == kh * GQA_GROUP + h, part, 0.0)
        ds_ref[0:1, :] += dsink

        @pl.when(n == 0)
        def _():
            dk_ref[pl.ds(0, ATTN_BLOCK), :] += dk[ATTN_BLOCK:, :]
            dv_ref[pl.ds(0, ATTN_BLOCK), :] += dv[ATTN_BLOCK:, :]

        @pl.when(n > 0)
        def _():
            start = pl.multiple_of((n - 1) * ATTN_BLOCK, ATTN_BLOCK)
            dk_ref[pl.ds(start, 2 * ATTN_BLOCK), :] += dk
            dv_ref[pl.ds(start, 2 * ATTN_BLOCK), :] += dv

    q_spec, cur, prev, cur_t, prev_t = _attn_specs(gw)
    tab_spec = pl.BlockSpec((ATTN_BLOCK, LANES), lambda kh, n: (n, 0))
    acc_spec = pl.BlockSpec((None, seq, LANES), lambda kh, n: (kh, 0, 0))
    acc_shape = jax.ShapeDtypeStruct((n_kv, seq, LANES), F32)
    return pl.pallas_call(
        body, name="attn_bwd", grid=(n_kv, n_blocks),
        out_shape=(jax.ShapeDtypeStruct((2, seq, dm), BF16), acc_shape, acc_shape,
                   jax.ShapeDtypeStruct((STAT_ROWS, LANES), F32)),
        in_specs=[q_spec, prev, cur, prev, cur, prev_t, cur_t, q_spec, q_spec, q_spec,
                  pl.BlockSpec(memory_space=pltpu.SMEM), tab_spec, tab_spec, tab_spec, ANY],
        out_specs=(pl.BlockSpec((2, ATTN_BLOCK, gw), lambda kh, n: (0, n, kh)), acc_spec, acc_spec,
                   pl.BlockSpec((STAT_ROWS, LANES), lambda kh, n: (0, 0))),
        compiler_params=_params("arbitrary", "arbitrary"),
    )(qs, kd, kd, vd, vd, kt, kt, zb, att, dyb, sinks, *tables, after)


def _kv_grad_fold(dk, dv, tables):
    n_kv, seq, _ = dk.shape
    kvw = n_kv * HEAD_DIM
    tile = _tile(seq, 512)

    def body(dk_ref, dv_ref, cos_ref, sa_ref, sb_ref, o_ref):
        low = lax.broadcasted_iota(jnp.int32, (1, LANES), 1) < HEAD_DIM
        cos, sa, sb = cos_ref[...], sa_ref[...], sb_ref[...]

        def folded(ref, h):
            t = ref[h]
            return t + pltpu.roll(t, HEAD_DIM, 1)

        for j in range(n_kv // 2):
            ka = _rope_transposed(folded(dk_ref, 2 * j), cos, sa, sb)
            kb = _rope_transposed(folded(dk_ref, 2 * j + 1), cos, sa, sb)
            o_ref[:, j * LANES:(j + 1) * LANES] = jnp.where(low, ka, kb).astype(BF16)
            o_ref[:, kvw + j * LANES:kvw + (j + 1) * LANES] = jnp.where(
                low, folded(dv_ref, 2 * j), folded(dv_ref, 2 * j + 1)).astype(BF16)

    tab_spec = pl.BlockSpec((tile, LANES), lambda i: (i, 0))
    in_spec = pl.BlockSpec((n_kv, tile, LANES), lambda i: (0, i, 0))
    return pl.pallas_call(
        body, name="kv_grad_fold", grid=(seq // tile,),
        out_shape=jax.ShapeDtypeStruct((seq, 2 * kvw), BF16),
        in_specs=[in_spec, in_spec, tab_spec, tab_spec, tab_spec],
        out_specs=pl.BlockSpec((tile, 2 * kvw), lambda i: (i, 0)),
        compiler_params=_params("parallel"),
    )(dk, dv, *tables)


def _out_proj_norm_loss(yb, w, xhat1, gain0, bias0, gain1, bias1, target):
    seq, dm = xhat1.shape
    tile = _tile(seq, 512)

    def body(y_ref, w_ref, xh1_ref, g0_ref, b0_ref, g1_ref, b1_ref, t_ref, dr_ref, drb_ref, st_ref):
        i = pl.program_id(0)

        @pl.when(i == 0)
        def _():
            st_ref[...] = jnp.zeros_like(st_ref)

        ob = jnp.dot(y_ref[...], w_ref[...], preferred_element_type=F32)
        x1 = xh1_ref[...] * g0_ref[...] + b0_ref[...]
        xhat, rstd = _layer_norm_stats(ALPHA * x1 + ob)
        err = xhat * g1_ref[...] + b1_ref[...] - t_ref[...]
        dout = err * (1.0 / dm)
        dr = _layer_norm_backward(dout, xhat, rstd, g1_ref[...])
        dr_ref[...] = dr
        drb_ref[...] = dr.astype(BF16)
        st_ref[0:1, :] += _col_sum(dout * xhat)
        st_ref[1:2, :] += _col_sum(dout)
        st_ref[2:3, :] += _col_sum(err * err)

    row_spec = pl.BlockSpec((tile, dm), lambda i: (i, 0))
    vec_spec = pl.BlockSpec((1, dm), lambda i: (0, 0))
    return pl.pallas_call(
        body, name="out_proj_norm_loss_b", grid=(seq // tile,),
        out_shape=(jax.ShapeDtypeStruct((seq, dm), F32), jax.ShapeDtypeStruct((seq, dm), BF16),
                   jax.ShapeDtypeStruct((STAT_ROWS, dm), F32)),
        in_specs=[row_spec, pl.BlockSpec(w.shape, lambda i: (0, 0), pipeline_mode=pl.Buffered(1)), row_spec, vec_spec,
                  vec_spec, vec_spec,
                  vec_spec, row_spec],
        out_specs=(row_spec, row_spec, pl.BlockSpec((STAT_ROWS, dm), lambda i: (0, 0))),
        compiler_params=_params("arbitrary"),
    )(yb, w, xhat1, gain0, bias0, gain1, bias1, target)


def _stream_grad_norm_backward(dhq, wqg, dkv, wkv, dr2, xhat1, rstd1, gain0, after):
    seq, dm = dr2.shape
    tile = _tile(seq, 256)

    def body(dh_ref, wqg_ref, dkv_ref, wkv_ref, dr2_ref, xh_ref, rstd_ref, g_ref, after_ref, dr_ref, drb_ref, st_ref):
        del after_ref

        @pl.when(pl.program_id(0) == 0)
        def _():
            st_ref[...] = jnp.zeros_like(st_ref)

        dx1 = (lax.dot_general(dh_ref[0], wqg_ref[:, :dm], NT, preferred_element_type=F32)
               + lax.dot_general(dh_ref[1], wqg_ref[:, dm:], NT, preferred_element_type=F32)
               + lax.dot_general(dkv_ref[...], wkv_ref[...], NT, preferred_element_type=F32)
               + ALPHA * dr2_ref[...])
        xhat = xh_ref[...]
        dr = _layer_norm_backward(dx1, xhat, rstd_ref[...], g_ref[...])
        dr_ref[...] = dr
        drb_ref[...] = dr.astype(BF16)
        st_ref[0:1, :] += _col_sum(dx1 * xhat)
        st_ref[1:2, :] += _col_sum(dx1)

    row_spec = pl.BlockSpec((tile, dm), lambda i: (i, 0))
    resident = pl.Buffered(1)
    return pl.pallas_call(
        body, name="stream_grad_norm_bwd", grid=(seq // tile,),
        out_shape=(jax.ShapeDtypeStruct((seq, dm), F32), jax.ShapeDtypeStruct((seq, dm), BF16),
                   jax.ShapeDtypeStruct((STAT_ROWS, dm), F32)),
        in_specs=[pl.BlockSpec((2, tile, dm), lambda i: (0, i, 0)),
                  pl.BlockSpec(wqg.shape, lambda i: (0, 0), pipeline_mode=resident),
                  pl.BlockSpec((tile, dkv.shape[1]), lambda i: (i, 0)),
                  pl.BlockSpec(wkv.shape, lambda i: (0, 0), pipeline_mode=resident),
                  row_spec, row_spec, pl.BlockSpec((tile, 1), lambda i: (i, 0)),
                  pl.BlockSpec((1, dm), lambda i: (0, 0)), ANY],
        out_specs=(row_spec, row_spec, pl.BlockSpec((STAT_ROWS, dm), lambda i: (0, 0))),
        compiler_params=_params("arbitrary"),
    )(dhq, wqg, dkv, wkv, dr2, xhat1, rstd1, gain0, after)


def _adamw_math(w, g, m, v):
    m = ADAM_B1 * m + (1.0 - ADAM_B1) * g
    v = ADAM_B2 * v + (1.0 - ADAM_B2) * (g * g)
    m_hat = m / (1.0 - ADAM_B1 ** ADAM_STEP)
    v_hat = v / (1.0 - ADAM_B2 ** ADAM_STEP)
    delta = -ADAM_LR * (m_hat / (jnp.sqrt(v_hat) + ADAM_EPS) + ADAM_WD * w)
    return delta, m, v


def _sum_devices(ref):
    total = ref[0].astype(F32)
    for d in range(1, ref.shape[0]):
        total = total + ref[d].astype(F32)
    return total


def _adamw_shard(name, parts, w, m, v, after):
    rows, cols = w.shape
    n_parts = len(parts)
    part_rows = rows // n_parts
    tr = _tile(part_rows, max(8, (1 << 18) // cols)) if part_rows >= 8 else part_rows
    per_part = part_rows // tr

    def body(*refs):
        p_refs = refs[:n_parts]
        w_ref, m_ref, v_ref, _, g_out, d_out, m_out, v_out = refs[n_parts:]
        g = _sum_devices(p_refs[0])
        for k in range(1, n_parts):
            g = jnp.where(pl.program_id(0) >= k * per_part, _sum_devices(p_refs[k]), g)
        delta, m_new, v_new = _adamw_math(w_ref[...], g, m_ref[...], v_ref[...])
        g_out[...] = g
        d_out[...] = delta
        m_out[...] = m_new
        v_out[...] = v_new

    def part_spec(k):
        return pl.BlockSpec((parts[k].shape[0], tr, cols),
                            lambda i: (0, jnp.clip(i - k * per_part, 0, per_part - 1), 0))

    spec = pl.BlockSpec((tr, cols), lambda i: (i, 0))
    shape = jax.ShapeDtypeStruct((rows, cols), F32)
    return pl.pallas_call(
        body, name=name, grid=(rows // tr,),
        out_shape=(shape, shape, shape, shape),
        in_specs=[part_spec(k) for k in range(n_parts)] + [spec, spec, spec, ANY],
        out_specs=(spec, spec, spec, spec),
        compiler_params=_params("arbitrary"),
    )(*parts, w, m, v, after)


def _adamw_replicated(stats_b, stats_a, sink_parts, ln_g, ln_b, sinks, m_ln_g, m_ln_b, m_sinks, v_ln_g, v_ln_b,
                      v_sinks, after):
    n_q = sinks.shape[1]
    dm = ln_g.shape[1]

    def body(sb_ref, sa_ref, sk_ref, g_ref, b_ref, s_ref, mg_ref, mb_ref, ms_ref, vg_ref, vb_ref, vs_ref, after_ref,
             *outs):
        del after_ref
        layer_sums = (_sum_devices(sa_ref), _sum_devices(sb_ref))
        outs[12][...] = jnp.sum(layer_sums[1][2:3, :], axis=1, keepdims=True) * (0.5 / dm)
        for which, (w_ref, m_ref, v_ref) in enumerate(((g_ref, mg_ref, vg_ref), (b_ref, mb_ref, vb_ref))):
            for layer in range(DEPTH):
                row = slice(layer, layer + 1)
                g = layer_sums[layer][which:which + 1, :]
                res = (g,) + _adamw_math(w_ref[row, :], g, m_ref[row, :], v_ref[row, :])
                for o_ref, val in zip(outs[4 * which:4 * which + 4], res):
                    o_ref[row, :] = val
        g = _sum_devices(sk_ref)[0:1, 0:n_q]
        res = (g,) + _adamw_math(s_ref[...], g, ms_ref[...], vs_ref[...])
        for o_ref, val in zip(outs[8:12], res):
            o_ref[...] = val

    vmem = pl.BlockSpec(memory_space=pltpu.VMEM)
    shapes = [jax.ShapeDtypeStruct(a.shape, F32) for a in (ln_g, ln_b, sinks) for _ in range(4)]
    shapes.append(jax.ShapeDtypeStruct((1, 1), F32))
    return pl.pallas_call(
        body, name="adamw_replicated", out_shape=tuple(shapes),
        in_specs=[vmem] * 12 + [ANY], out_specs=tuple([vmem] * 13),
    )(stats_b, stats_a, sink_parts, ln_g, ln_b, sinks, m_ln_g, m_ln_b, m_sinks, v_ln_g, v_ln_b, v_sinks, after)


def kernel(x, ln_g, ln_b, a_w_in, a_w_group, a_scale, a_w_out, b_w_k, b_w_v, b_w_qg, b_sinks, b_w_out, loss_target, m_ln_g, m_ln_b, m_a_w_in, m_a_w_group, m_a_scale, m_a_w_out, m_b_w_k, m_b_w_v, m_b_w_qg, m_b_sinks, m_b_w_out, v_ln_g, v_ln_b, v_a_w_in, v_a_w_group, v_a_scale, v_a_w_out, v_b_w_k, v_b_w_v, v_b_w_qg, v_b_sinks, v_b_w_out):
    _, seq, dm = x.shape
    n_groups = len(POOL_WINDOWS)
    gd = dm // n_groups
    kvw = b_w_k.shape[1]
    cb = 2 * dm // N_DEV
    rb = dm // N_DEV
    gb = gd // N_DEV

    x2 = x.reshape(seq, dm)
    target = loss_target.reshape(seq, dm)
    w_in_s = a_w_in.reshape(dm, cb)
    w_g_s = a_w_group.reshape(n_groups, gb, gd)
    w_out_s = a_w_out.reshape(rb, dm)
    w_qg_s = b_w_qg.reshape(dm, cb)
    w_outb_s = b_w_out.reshape(rb, dm)

    def cols(ref, dev):
        return ref.at[:, pl.ds(pl.multiple_of(dev * cb, LANES), cb)]

    def rows(ref, dev):
        return ref.at[pl.ds(pl.multiple_of(dev * rb, 8), rb), :]

    def group_rows(ref, dev):
        return ref.at[:, pl.ds(pl.multiple_of(dev * gb, 8), gb), :]

    def k_rows(ref, dev):
        return ref.at[pl.ds(pl.multiple_of(dev * rb, 8), rb), pl.ds(0, kvw)]

    def v_rows(ref, dev):
        return ref.at[pl.ds(pl.multiple_of(dev * rb, 8), rb), pl.ds(kvw, kvw)]

    def scale_cols(ref, dev):
        return ref.at[:, pl.ds(pl.multiple_of(dev * rb, LANES), rb)]

    bf = lambda a: a.astype(BF16)
    wide, square = jax.ShapeDtypeStruct((dm, 2 * dm), BF16), jax.ShapeDtypeStruct((dm, dm), BF16)
    w_g, scale, w_in = _gather_weights(
        "gather_a_in", 0, [(bf(w_g_s), 0, group_rows), (a_scale, 1, scale_cols), (bf(w_in_s), 2, cols)],
        [jax.ShapeDtypeStruct((n_groups, gd, gd), BF16), jax.ShapeDtypeStruct((1, dm), F32), wide])
    (w_out,) = _gather_weights("gather_a_out", 1, [(bf(w_out_s), 0, rows)], [square])
    w_kv, w_qg = _gather_weights(
        "gather_b_in", 2, [(bf(b_w_k), 0, k_rows), (bf(b_w_v), 0, v_rows), (bf(w_qg_s), 1, cols)],
        [jax.ShapeDtypeStruct((dm, 2 * kvw), BF16), wide])
    (w_outb,) = _gather_weights("gather_b_out", 3, [(bf(w_outb_s), 0, rows)], [square])

    tables = _rope_tables(seq)
    bm = _tile(seq, 1024)
    bn = _tile(dm, 1024)
    g0, g1, b0, b1 = ln_g[0:1], ln_g[1:2], ln_b[0:1], ln_b[1:2]

    xb = _cast_bf16("cast_x", x2)
    y, pooled, mixed, z_a = _pool_forward(xb, w_in, w_g, scale)
    xhat1, rstd1, x1b = _out_proj_norm(y, w_out, x2, g0, b0)

    kd, vd, kt, vt = _kv_proj(x1b, w_kv, tables)
    bmq = bm
    tab_spec = pl.BlockSpec((bmq, LANES), lambda i, j: (i, 0))

    def rope_scale(val, cos_ref, sa_ref, sb_ref):
        cos, sa, sb = cos_ref[...], sa_ref[...], sb_ref[...]
        return jnp.concatenate([_rope(val[:, j * LANES:(j + 1) * LANES], cos, sa, sb) * 0.125
                                for j in range(val.shape[1] // LANES)], axis=1)

    qs = _mm("b_q_proj", x1b, w_qg, dims=NN, grid=(seq // bmq, dm // bn),
             a_spec=pl.BlockSpec((bmq, dm), lambda i, j: (i, 0)), b_spec=pl.BlockSpec((dm, bn), lambda i, j: (0, j)),
             out_shape=jax.ShapeDtypeStruct((seq, dm), BF16), out_spec=pl.BlockSpec((bmq, bn), lambda i, j: (i, j)),
             epilogue=rope_scale, extras=tables, extra_specs=(tab_spec,) * 3)
    zb = _mm("b_gate_proj", x1b, w_qg, dims=NN, grid=(seq // bm, dm // bn),
             a_spec=pl.BlockSpec((bm, dm), lambda i, j: (i, 0)),
             b_spec=pl.BlockSpec((dm, bn), lambda i, j: (0, j + dm // bn)),
             out_shape=jax.ShapeDtypeStruct((seq, dm), F32), out_spec=pl.BlockSpec((bm, bn), lambda i, j: (i, j)))
    att, yb = _attn_forward(qs, kd, vt, zb, b_sinks)
    dr2, dr2b, stats_b = _out_proj_norm_loss(yb, w_outb, xhat1, g0, b0, g1, b1, target)

    def weight_grad(name, a, b, n_cols, b_spec=None, part=(0, 1), after=None):
        m_cols = a.shape[1] // part[1]
        tm, tn = _tile(m_cols, 1024), _tile(n_cols, 512)
        first = part[0] * (m_cols // tm)
        return _mm(name, a, b, dims=TN, grid=(m_cols // tm, n_cols // tn),
                   a_spec=pl.BlockSpec((seq, tm), lambda i, j: (0, first + i)),
                   b_spec=b_spec(tn) if b_spec else pl.BlockSpec((seq, tn), lambda i, j: (0, j)),
                   out_shape=jax.ShapeDtypeStruct((m_cols, n_cols), BF16),
                   out_spec=pl.BlockSpec((tm, tn), lambda i, j: (i, j)),
                   extras=() if after is None else (after,), extra_specs=() if after is None else (ANY,))

    def halves_spec(tn):
        per = dm // tn
        return pl.BlockSpec((None, seq, tn), lambda i, j: (j // per, 0, j % per))

    def times_transposed(name, a, w):
        return _mm(name, a, w, dims=NT, grid=(seq // bm, dm // bn),
                   a_spec=pl.BlockSpec((bm, a.shape[1]), lambda i, j: (i, 0)),
                   b_spec=pl.BlockSpec((bn, w.shape[1]), lambda i, j: (j, 0)),
                   out_shape=jax.ShapeDtypeStruct((seq, dm), F32), out_spec=pl.BlockSpec((bm, bn), lambda i, j: (i, j)))

    def stat_row_cols(ref, dev):
        return ref.at[pl.ds(0, 1), pl.ds(pl.multiple_of(dev * rb, LANES), rb)]

    upd = {}
    last = [dr2b]
    my_core = lax.axis_index("c").astype(jnp.int32).reshape(1)

    def then(value):
        last[0] = value[0] if isinstance(value, (list, tuple)) else value
        return value

    def shard_update(key, parts, w, m, v):
        shape = w.shape
        flat = lambda a: a.reshape(-1, shape[-1])
        parts = list(parts) if isinstance(parts, (list, tuple)) else [parts]
        outs = then(_adamw_shard("adamw_" + key, [p.reshape(p.shape[0], -1, shape[-1]) for p in parts], flat(w),
                                 flat(m), flat(v), last[0]))
        upd[key] = [o.reshape(shape) for o in outs]

    def two_level_scatter(name, ids, streams):
        staged = _sibling_exchange(name + "_pair", streams, ids[0])

        def finish():
            sums = [then(_pair_sum(f"{name}_sum{s}", st[0], got, my_core, last[0]))
                    for s, (st, got) in enumerate(zip(streams, staged))]
            return _chip_exchange(name + "_chip", sums, ids[1])
        return finish

    d_w_outb = then(weight_grad("b_out_proj_dw", yb, dr2b, dm))
    (p_outb,) = _exchange_blocks("scatter_b_out", [(d_w_outb, rows, (rb, dm))], 4)
    dyb = times_transposed("b_out_proj_dx", dr2b, w_outb)
    dhq, dkd, dvd, dsink = then(_attn_backward(qs, kd, vd, kt, zb, att, dyb, b_sinks, tables, after=last[0]))
    dkv = _kv_grad_fold(dkd, dvd, tables)
    d_w_kv = weight_grad("b_kv_proj_dw", x1b, dkv, 2 * kvw)
    d_w_qg = then(weight_grad("b_qg_proj_dw", x1b, dhq, 2 * dm, halves_spec, after=d_w_kv))
    finish_b_in = two_level_scatter("scatter_b_in", (5, 11), [(d_w_qg, cols, (dm, cb))])
    dr1, dr1b, stats_a = _stream_grad_norm_backward(dhq, w_qg, dkv, w_kv, dr2, xhat1, rstd1, g0, after=last[0])
    last[0] = dr1b
    shard_update("b_w_out", p_outb, b_w_out, m_b_w_out, v_b_w_out)
    (p_qg,) = finish_b_in()
    all_b, all_a, all_sink = _exchange_blocks("gather_replicated_grads", [
        (stats_b, None, stats_b.shape), (stats_a, None, stats_a.shape), (dsink, None, dsink.shape)], 9)

    d_w_out = then(weight_grad("a_out_proj_dw", y, dr1b, dm, after=last[0]))
    p_out, p_k, p_v = _exchange_blocks("scatter_a_out", [
        (d_w_out, rows, (rb, dm)), (d_w_kv, k_rows, (rb, kvw)), (d_w_kv, v_rows, (rb, kvw))], 6)
    dy = times_transposed("a_out_proj_dx", dr1b, w_out)
    dh, d_w_g, stats_s = then(_pool_mid_backward(dy, mixed, z_a, pooled, w_g, scale, after=last[0]))
    p_g, p_scale = _exchange_blocks("scatter_a_mid", [
        (d_w_g, group_rows, (n_groups, gb, gd)), (stats_s, stat_row_cols, (1, rb))], 7)
    shard_update("b_w_qg", p_qg, b_w_qg, m_b_w_qg, v_b_w_qg)
    rep = then(_adamw_replicated(all_b, all_a, all_sink, ln_g, ln_b, b_sinks, m_ln_g, m_ln_b, m_b_sinks, v_ln_g,
                                 v_ln_b, v_b_sinks, last[0]))
    upd["ln_g"], upd["ln_b"], upd["b_sinks"] = list(rep[0:4]), list(rep[4:8]), list(rep[8:12])
    finish_a_in = []
    for k in range(2):
        d_w_in = then(weight_grad(f"a_in_proj_dw_{k}", xb, dh, 2 * dm, halves_spec, part=(k, 2), after=last[0]))
        finish_a_in.append(two_level_scatter(f"scatter_a_in_{k}", (8 + 2 * k, 12 + k), [(d_w_in, cols, (dm // 2, cb))]))
    shard_update("a_w_out", p_out, a_w_out, m_a_w_out, v_a_w_out)
    shard_update("b_w_k", p_k, b_w_k, m_b_w_k, v_b_w_k)
    shard_update("b_w_v", p_v, b_w_v, m_b_w_v, v_b_w_v)
    shard_update("a_w_group", p_g, a_w_group, m_a_w_group, v_a_w_group)
    shard_update("a_scale", p_scale, a_scale, m_a_scale, v_a_scale)
    p_in = list(finish_a_in[0]()) + list(finish_a_in[1]())
    grad_x = then(_mm("a_in_proj_dx", dh, w_in, dims=NT, grid=(seq // bm, dm // bn, 2), nk=2,
                      a_spec=pl.BlockSpec((None, bm, dm), lambda i, j, k: (k, i, 0)),
                      b_spec=pl.BlockSpec((bn, dm), lambda i, j, k: (j, k)),
                      out_shape=jax.ShapeDtypeStruct((seq, dm), F32),
                      out_spec=pl.BlockSpec((bm, bn), lambda i, j, k: (i, j)),
                      add=dr1, add_spec=pl.BlockSpec((bm, bn), lambda i, j, k: (i, j)), add_scale=ALPHA,
                      extras=(last[0],), extra_specs=(ANY,)))
    shard_update("a_w_in", p_in, a_w_in, m_a_w_in, v_a_w_in)

    loss = rep[12].reshape(())
    order = ["ln_g", "ln_b", "a_w_in", "a_w_group", "a_scale", "a_w_out", "b_w_k", "b_w_v", "b_w_qg", "b_sinks",
             "b_w_out"]
    return (loss, grad_x.reshape(x.shape), *[upd[n][0] for n in order], *[upd[n][1] for n in order],
            *[upd[n][2] for n in order], *[upd[n][3] for n in order])
```

```python
import functools

import jax
import jax.numpy as jnp
from jax import lax
from jax.experimental import pallas as pl
from jax.experimental.pallas import tpu as pltpu
from jax.experimental.pallas import tpu_sc as plsc

F32 = jnp.float32
BF16 = jnp.bfloat16
MESH = pl.DeviceIdType.MESH
AXES = ("x", "y", "c")
N_DEV = 8

POOL_WINDOWS = (2, 4, 8, 16)
POOL_HALO = 16
HEAD_DIM = 64
GQA_GROUP = 8
ATTN_BLOCK = 128
ROPE_THETA = 10000.0
LN_EPS = 1e-5
NEG_INF = -1e30
DEPTH = 2
ALPHA = (2 * DEPTH) ** 0.25
ADAM_LR = 0.001
ADAM_B1 = 0.9
ADAM_B2 = 0.999
ADAM_EPS = 1e-08
ADAM_WD = 0.01
ADAM_STEP = 10

LANES = 128
STAT_ROWS = 8


def _tile(n, want):
    t = min(n, want)
    while n % t:
        t //= 2
    return t


def _params(*sem):
    return pltpu.CompilerParams(dimension_semantics=sem)


ANY = pl.BlockSpec(memory_space=pl.ANY)


def _my_pos():
    return lax.axis_index("x"), lax.axis_index("y"), lax.axis_index("c")


def _dev_index(p):
    return 4 * p[0] + 2 * p[1] + p[2]


def _handshake(peers):
    barrier = pltpu.get_barrier_semaphore()
    for peer in peers:
        pl.semaphore_signal(barrier, inc=1, device_id=peer, device_id_type=MESH)
    pl.semaphore_wait(barrier, len(peers))


def _launch_on_sequencer(name, collective_id, body, operands, out_shapes, scratch):
    return pl.kernel(
        body, out_type=tuple(out_shapes), name=name,
        mesh=plsc.ScalarSubcoreMesh(axis_name="sequencer", num_cores=1), scratch_types=scratch,
        compiler_params=pltpu.CompilerParams(collective_id=collective_id),
    )(*operands)


def _gather_weights(name, collective_id, streams, out_shapes):
    n_s = len(streams)
    n_out = len(out_shapes)

    def body(*refs):
        srcs = refs[:n_s]
        outs = refs[n_s:n_s + n_out]
        send_sems, recv_sems, local_sems = refs[n_s + n_out:]
        x, y, c = _my_pos()
        me, sibling = (x, y, c), (x, y, 1 - c)
        x_nbr, y_nbr, diag = (1 - x, y), (x, 1 - y), (1 - x, 1 - y)
        _handshake([sibling, (*x_nbr, c), (*y_nbr, c)])
        south = c == 0
        relay_from = (jnp.where(south, 1 - x, x), jnp.where(south, y, 1 - y))
        relay_to = (jnp.where(south, x, 1 - x), jnp.where(south, 1 - y, y))
        early, late = jnp.where(south, 1, 2), jnp.where(south, 2, 1)

        def copy(s, k, block, to, from_shard=False):
            out_ref = outs[streams[s][1]]
            win = streams[s][2](out_ref, _dev_index(block))
            return pltpu.make_async_remote_copy(
                src_ref=srcs[s] if from_shard else win, dst_ref=win,
                send_sem=send_sems.at[7 * s + k], recv_sem=recv_sems.at[7 * s + k],
                device_id=to, device_id_type=MESH)

        mine = [pltpu.make_async_copy(srcs[s], streams[s][2](outs[streams[s][1]], _dev_index(me)), local_sems.at[s])
                for s in range(n_s)]
        for cp in mine:
            cp.start()
        sent = []
        for s in range(n_s):
            sent += [copy(s, 0, me, sibling, True), copy(s, 1, me, (*x_nbr, c), True), copy(s, 2, me, (*y_nbr, c), True)]
        for cp in sent:
            cp.start()
        for s in range(n_s):
            copy(s, early, (*relay_from, c), me).wait_recv()
            sent += [copy(s, 3, (*relay_from, c), (*relay_to, c)), copy(s, 3 + early, (*relay_from, c), sibling)]
            for cp in sent[-2:]:
                cp.start()
        for s in range(n_s):
            copy(s, late, (*relay_to, c), me).wait_recv()
            sent.append(copy(s, 3 + late, (*relay_to, c), sibling))
            sent[-1].start()
        for s in range(n_s):
            copy(s, 3, (*diag, c), me).wait_recv()
            sent.append(copy(s, 6, (*diag, c), sibling))
            sent[-1].start()
        for s in range(n_s):
            copy(s, 0, sibling, me).wait_recv()
            for k, chip in ((4, x_nbr), (5, y_nbr), (6, diag)):
                copy(s, k, (*chip, 1 - c), me).wait_recv()
        for cp in sent:
            cp.wait_send()
        for cp in mine:
            cp.wait()

    scratch = [pltpu.SemaphoreType.DMA((7 * n_s,)), pltpu.SemaphoreType.DMA((7 * n_s,)),
               pltpu.SemaphoreType.DMA((n_s,))]
    return _launch_on_sequencer(name, collective_id, body, [s[0] for s in streams], out_shapes, scratch)


def _exchange_blocks(name, streams, collective_id):
    n_s = len(streams)

    def body(*refs):
        srcs = refs[:n_s]
        outs = refs[n_s:2 * n_s]
        send_sems, recv_sems, local_sems = refs[2 * n_s:]
        x, y, c = _my_pos()
        me = _dev_index((x, y, c))
        _handshake([(1 - x if k & 4 else x, 1 - y if k & 2 else y, 1 - c if k & 1 else c) for k in range(1, N_DEV)])

        def window(s, dev):
            return srcs[s] if streams[s][1] is None else streams[s][1](srcs[s], dev)

        mine = [pltpu.make_async_copy(window(s, me), outs[s].at[me], local_sems.at[s]) for s in range(n_s)]
        for cp in mine:
            cp.start()
        copies = []
        for k in (2, 4, 6, 3, 5, 7, 1):
            peer = (1 - x if k & 4 else x, 1 - y if k & 2 else y, 1 - c if k & 1 else c)
            for s in range(n_s):
                copies.append(pltpu.make_async_remote_copy(
                    src_ref=window(s, _dev_index(peer)), dst_ref=outs[s].at[me],
                    send_sem=send_sems.at[7 * s + k - 1], recv_sem=recv_sems.at[7 * s + k - 1],
                    device_id=peer, device_id_type=MESH))
        for cp in copies:
            cp.start()
        for cp in copies:
            cp.wait()
        for cp in mine:
            cp.wait()

    out_shapes = [jax.ShapeDtypeStruct((N_DEV,) + tuple(s[2]), s[0].dtype) for s in streams]
    scratch = [pltpu.SemaphoreType.DMA((7 * n_s,)), pltpu.SemaphoreType.DMA((7 * n_s,)),
               pltpu.SemaphoreType.DMA((n_s,))]
    return _launch_on_sequencer(name, collective_id, body, [s[0] for s in streams], out_shapes, scratch)


N_CHIPS = 4


def _sibling_exchange(name, streams, collective_id):
    n_s = len(streams)

    def body(*refs):
        srcs = refs[:n_s]
        outs = refs[n_s:2 * n_s]
        send_sems, recv_sems = refs[2 * n_s:]
        x, y, c = _my_pos()
        sibling = (x, y, 1 - c)
        _handshake([sibling])
        copies = [pltpu.make_async_remote_copy(
            src_ref=streams[s][1](srcs[s], 2 * chip + (1 - c)), dst_ref=outs[s].at[chip],
            send_sem=send_sems.at[N_CHIPS * s + chip], recv_sem=recv_sems.at[N_CHIPS * s + chip],
            device_id=sibling, device_id_type=MESH) for s in range(n_s) for chip in range(N_CHIPS)]
        for cp in copies:
            cp.start()
        for cp in copies:
            cp.wait()

    out_shapes = [jax.ShapeDtypeStruct((N_CHIPS,) + tuple(s[2]), s[0].dtype) for s in streams]
    scratch = [pltpu.SemaphoreType.DMA((N_CHIPS * n_s,)), pltpu.SemaphoreType.DMA((N_CHIPS * n_s,))]
    return _launch_on_sequencer(name, collective_id, body, [s[0] for s in streams], out_shapes, scratch)


def _pair_sum(name, array, from_sibling, my_core, after):
    _, rows, cols = from_sibling.shape
    tr = _tile(rows, 512)

    def body(core_ref, own_ref, sib_ref, after_ref, o_ref):
        del core_ref, after_ref
        o_ref[...] = (own_ref[...].astype(F32) + sib_ref[...].astype(F32)).astype(o_ref.dtype)

    staged_spec = pl.BlockSpec((None, tr, cols), lambda k, i, core: (k, i, 0))
    return pl.pallas_call(
        body, name=name, out_shape=jax.ShapeDtypeStruct(from_sibling.shape, array.dtype),
        grid_spec=pltpu.PrefetchScalarGridSpec(
            num_scalar_prefetch=1, grid=(N_CHIPS, rows // tr),
            in_specs=[pl.BlockSpec((tr, cols), lambda k, i, core: (i, 2 * k + core[0])), staged_spec, ANY],
            out_specs=staged_spec),
        compiler_params=_params("parallel", "parallel"),
    )(my_core, array, from_sibling, after)


def _chip_exchange(name, pair_sums, collective_id):
    n_s = len(pair_sums)

    def body(*refs):
        srcs = refs[:n_s]
        outs = refs[n_s:2 * n_s]
        send_sems, recv_sems, local_sems = refs[2 * n_s:]
        x, y, c = _my_pos()
        my_chip = 2 * x + y
        chips = [(1 - x, y), (x, 1 - y), (1 - x, 1 - y)]
        _handshake([(*chip, c) for chip in chips])
        mine = [pltpu.make_async_copy(srcs[s].at[my_chip], outs[s].at[my_chip], local_sems.at[s]) for s in range(n_s)]
        copies = [pltpu.make_async_remote_copy(
            src_ref=srcs[s].at[2 * chip[0] + chip[1]], dst_ref=outs[s].at[my_chip],
            send_sem=send_sems.at[3 * s + j], recv_sem=recv_sems.at[3 * s + j],
            device_id=(*chip, c), device_id_type=MESH) for s in range(n_s) for j, chip in enumerate(chips)]
        for cp in mine + copies:
            cp.start()
        for cp in copies:
            cp.wait()
        for cp in mine:
            cp.wait()

    out_shapes = [jax.ShapeDtypeStruct(p.shape, p.dtype) for p in pair_sums]
    scratch = [pltpu.SemaphoreType.DMA((3 * n_s,)), pltpu.SemaphoreType.DMA((3 * n_s,)),
               pltpu.SemaphoreType.DMA((n_s,))]
    return _launch_on_sequencer(name, collective_id, body, list(pair_sums), out_shapes, scratch)


NN = (((1,), (0,)), ((), ()))
NT = (((1,), (1,)), ((), ()))
TN = (((0,), (0,)), ((), ()))


def _mm(name, a, b, *, dims, grid, a_spec, b_spec, out_shape, out_spec, nk=1,
        add=None, add_spec=None, add_scale=1.0, epilogue=None, extras=(), extra_specs=()):
    n_extra = len(extras)
    has_add = add is not None

    def body(*refs):
        a_ref, b_ref = refs[:2]
        pos = 2
        add_ref = None
        if has_add:
            add_ref = refs[pos]
            pos += 1
        extra_refs = refs[pos:pos + n_extra]
        o_ref = refs[pos + n_extra]
        acc_ref = refs[pos + n_extra + 1] if nk > 1 else None

        def finish(val):
            if has_add:
                val = val + add_scale * add_ref[...]
            if epilogue is not None:
                val = epilogue(val, *extra_refs)
            o_ref[...] = val.astype(o_ref.dtype)

        part = lax.dot_general(a_ref[...].astype(BF16), b_ref[...].astype(BF16), dims,
                               preferred_element_type=F32)
        if nk == 1:
            finish(part)
        else:
            k = pl.program_id(2)

            @pl.when(k == 0)
            def _():
                acc_ref[...] = part

            @pl.when(jnp.logical_and(k > 0, k < nk - 1))
            def _():
                acc_ref[...] += part

            @pl.when(k == nk - 1)
            def _():
                finish(acc_ref[...] + part)

    in_specs = [a_spec, b_spec] + ([add_spec] if has_add else []) + list(extra_specs)
    operands = [a, b] + ([add] if has_add else []) + list(extras)
    scratch = [pltpu.VMEM(out_spec.block_shape, F32)] if nk > 1 else []
    sem = ("parallel", "parallel") + (("arbitrary",) if nk > 1 else ())
    return pl.pallas_call(
        body, name=name, grid=grid, out_shape=out_shape,
        in_specs=in_specs, out_specs=out_spec, scratch_shapes=scratch,
        compiler_params=_params(*sem),
    )(*operands)


def _cast_bf16(name, a):
    rows, cols = a.shape
    tr = _tile(rows, 512)

    def body(a_ref, o_ref):
        o_ref[...] = a_ref[...].astype(BF16)

    return pl.pallas_call(
        body, name=name, grid=(rows // tr,),
        out_shape=jax.ShapeDtypeStruct(a.shape, BF16),
        in_specs=[pl.BlockSpec((tr, cols), lambda i: (i, 0))],
        out_specs=pl.BlockSpec((tr, cols), lambda i: (i, 0)),
        compiler_params=_params("parallel"),
    )(a)


def _rope_tables(seq):
    inv_freq = ROPE_THETA ** (-jnp.arange(0, HEAD_DIM, 2, dtype=F32) / HEAD_DIM)
    ang = jnp.arange(seq, dtype=F32)[:, None] * inv_freq[None, :]
    cos, sin = jnp.cos(ang), jnp.sin(ang)
    cos, sin = (jnp.concatenate([t, t, t, t], axis=-1) for t in (cos, sin))
    first_half = (jnp.arange(LANES) % HEAD_DIM < HEAD_DIM // 2)[None, :]
    return cos, jnp.where(first_half, -sin, 0.0), jnp.where(first_half, 0.0, sin)


def _rot(t, sin_a, sin_b):
    return pltpu.roll(t, LANES - HEAD_DIM // 2, 1) * sin_a + pltpu.roll(t, HEAD_DIM // 2, 1) * sin_b


def _rope(t, cos, sin_a, sin_b):
    return t * cos + _rot(t, sin_a, sin_b)


def _rope_transposed(dy, cos, sin_a, sin_b):
    return dy * cos - _rot(dy, sin_a, sin_b)


def _silu_parts(z):
    sig = jax.nn.sigmoid(z)
    return z * sig, sig * (1.0 + z * (1.0 - sig))


def _layer_norm_stats(r):
    mu = jnp.mean(r, axis=-1, keepdims=True)
    d = r - mu
    var = jnp.mean(d * d, axis=-1, keepdims=True)
    rstd = lax.rsqrt(var + LN_EPS)
    return d * rstd, rstd


def _layer_norm_backward(dout, xhat, rstd, gain):
    dxh = dout * gain
    m1 = jnp.mean(dxh, axis=-1, keepdims=True)
    m2 = jnp.mean(dxh * xhat, axis=-1, keepdims=True)
    return rstd * (dxh - m1 - xhat * m2)


def _col_sum(v):
    return jnp.sum(v, axis=0, keepdims=True)


def _pool_forward(xb, w_in, wg, scale):
    seq, dm = xb.shape
    n_g = len(POOL_WINDOWS)
    gd = dm // n_g
    tile = _tile(seq, 1024)
    halo_blocks = tile // POOL_HALO

    def body(x_ref, xp_ref, wu_ref, wz_ref, wg_ref, sc_ref, y_ref, p_ref, mx_ref, z_ref):
        i, g = pl.program_id(0), pl.program_id(1)
        u = jnp.dot(x_ref[...], wu_ref[...], preferred_element_type=F32)
        z = jnp.dot(x_ref[...], wz_ref[...], preferred_element_type=F32)
        prev = jnp.where(i > 0, jnp.dot(xp_ref[...], wu_ref[...], preferred_element_type=F32), 0.0)
        s = jnp.concatenate([prev, u], axis=0)
        sums, sh = [], 1
        while sh < POOL_WINDOWS[-1]:
            s = s + pltpu.roll(s, sh, 0)
            sums.append(s)
            sh *= 2
        win = sums[-1]
        for k in range(n_g - 2, -1, -1):
            win = jnp.where(g == k, sums[k], win)
        row = i * tile + lax.broadcasted_iota(jnp.int32, (tile, 1), 0)
        window = jnp.left_shift(2, g).astype(F32)
        p = win[POOL_HALO:, :] * (1.0 / jnp.minimum((row + 1).astype(F32), window)) - u
        pb = p.astype(BF16)
        mx = jnp.dot(pb, wg_ref[...], preferred_element_type=F32)
        y_ref[...] = (mx * sc_ref[...] * (z * jax.nn.sigmoid(z))).astype(BF16)
        p_ref[...] = pb
        mx_ref[...] = mx
        z_ref[...] = z

    out_spec = pl.BlockSpec((tile, gd), lambda i, g: (i, g))
    return pl.pallas_call(
        body, name="pool_fwd", grid=(seq // tile, n_g),
        out_shape=(jax.ShapeDtypeStruct((seq, dm), BF16), jax.ShapeDtypeStruct((seq, dm), BF16),
                   jax.ShapeDtypeStruct((seq, dm), F32), jax.ShapeDtypeStruct((seq, dm), F32)),
        in_specs=[pl.BlockSpec((tile, dm), lambda i, g: (i, 0)),
                  pl.BlockSpec((POOL_HALO, dm), lambda i, g: (jnp.maximum(i * halo_blocks - 1, 0), 0)),
                  pl.BlockSpec((dm, gd), lambda i, g: (0, g)),
                  pl.BlockSpec((dm, gd), lambda i, g: (0, n_g + g)),
                  pl.BlockSpec((None, gd, gd), lambda i, g: (g, 0, 0)),
                  pl.BlockSpec((1, gd), lambda i, g: (0, g))],
        out_specs=(out_spec, out_spec, out_spec, out_spec),
        compiler_params=_params("parallel", "parallel"),
    )(xb, xb, w_in, w_in, wg, scale)


def _pool_mid_backward(dy, mx, z, p, wg, scale, after):
    seq, dm = dy.shape
    gd = dm // len(POOL_WINDOWS)
    tile = _tile(seq, 256)
    n_i = seq // tile

    def body(dy_ref, mx_ref, z_ref, p_ref, wg_ref, sc_ref, after_ref, dh_ref, dwg_ref, st_ref, dwg_acc, carry):
        del after_ref
        i = pl.program_id(0)
        ti = n_i - 1 - i

        @pl.when(i == 0)
        def _():
            dwg_acc[...] = jnp.zeros_like(dwg_acc)
            carry[...] = jnp.zeros_like(carry)
            st_ref[...] = jnp.zeros_like(st_ref)

        row = ti * tile + lax.broadcasted_iota(jnp.int32, (tile, 1), 0)
        count = (row + 1).astype(F32)
        for g, w in enumerate(POOL_WINDOWS):
            cs = slice(g * gd, (g + 1) * gd)
            z = z_ref[:, cs]
            sz, dsz = _silu_parts(z)
            dyg = dy_ref[:, cs]
            mxg = mx_ref[:, cs]
            sc = sc_ref[:, cs]
            t1 = dyg * sz
            st_ref[0:1, cs] += _col_sum(t1 * mxg)
            dh_ref[1, :, cs] = (dyg * (mxg * sc) * dsz).astype(BF16)
            dmx = (t1 * sc).astype(BF16)
            dwg_acc[g] += lax.dot_general(p_ref[:, cs], dmx, TN, preferred_element_type=F32)
            dp = lax.dot_general(dmx, wg_ref[g], NT, preferred_element_type=F32)
            e = dp * (1.0 / jnp.minimum(count, float(w)))
            s = jnp.concatenate([e, carry[:, cs]], axis=0)
            n = tile + POOL_HALO
            sh = 1
            while sh < w:
                s = s + pltpu.roll(s, n - sh, 0)
                sh *= 2
            dh_ref[0, :, cs] = (s[:tile, :] - dp).astype(BF16)
            carry[:, cs] = e[:POOL_HALO, :]

        @pl.when(i == n_i - 1)
        def _():
            dwg_ref[...] = dwg_acc[...].astype(BF16)

    row_spec = pl.BlockSpec((tile, dm), lambda i: (n_i - 1 - i, 0))
    return pl.pallas_call(
        body, name="pool_mid_bwd", grid=(n_i,),
        out_shape=(jax.ShapeDtypeStruct((2, seq, dm), BF16), jax.ShapeDtypeStruct(wg.shape, BF16),
                   jax.ShapeDtypeStruct((STAT_ROWS, dm), F32)),
        in_specs=[row_spec, row_spec, row_spec, row_spec,
                  pl.BlockSpec(wg.shape, lambda i: (0, 0, 0)),
                  pl.BlockSpec((1, dm), lambda i: (0, 0)), ANY],
        out_specs=(pl.BlockSpec((2, tile, dm), lambda i: (0, n_i - 1 - i, 0)),
                   pl.BlockSpec(wg.shape, lambda i: (0, 0, 0)),
                   pl.BlockSpec((STAT_ROWS, dm), lambda i: (0, 0))),
        scratch_shapes=[pltpu.VMEM(wg.shape, F32), pltpu.VMEM((POOL_HALO, dm), F32)],
        compiler_params=_params("arbitrary"),
    )(dy, mx, z, p, wg, scale, after)


def _out_proj_norm(y, w, x, gain, bias):
    seq, dm = x.shape
    tile = _tile(seq, 512)

    def body(y_ref, w_ref, x_ref, g_ref, b_ref, xhat_ref, rstd_ref, xb_ref):
        o = jnp.dot(y_ref[...], w_ref[...], preferred_element_type=F32)
        xhat, rstd = _layer_norm_stats(ALPHA * x_ref[...] + o)
        xhat_ref[...] = xhat
        rstd_ref[...] = rstd
        xb_ref[...] = (xhat * g_ref[...] + b_ref[...]).astype(BF16)

    row_spec = pl.BlockSpec((tile, dm), lambda i: (i, 0))
    vec_spec = pl.BlockSpec((1, dm), lambda i: (0, 0))
    return pl.pallas_call(
        body, name="out_proj_norm_a", grid=(seq // tile,),
        out_shape=(jax.ShapeDtypeStruct((seq, dm), F32), jax.ShapeDtypeStruct((seq, 1), F32),
                   jax.ShapeDtypeStruct((seq, dm), BF16)),
        in_specs=[row_spec, pl.BlockSpec(w.shape, lambda i: (0, 0), pipeline_mode=pl.Buffered(1)), row_spec, vec_spec,
                  vec_spec],
        out_specs=(row_spec, pl.BlockSpec((tile, 1), lambda i: (i, 0)), row_spec),
        compiler_params=_params("parallel"),
    )(y, w, x, gain, bias)


def _kv_proj(xb, wkv, tables):
    seq, dm = xb.shape
    kvw = wkv.shape[1] // 2
    n_kv = kvw // HEAD_DIM
    tile = _tile(seq, 512)

    def body(x_ref, w_ref, cos_ref, sa_ref, sb_ref, kd_ref, vd_ref, kt_ref, vt_ref):
        kv = jnp.dot(x_ref[...], w_ref[...], preferred_element_type=F32)
        low = lax.broadcasted_iota(jnp.int32, (1, LANES), 1) < HEAD_DIM
        cos, sa, sb = cos_ref[...], sa_ref[...], sb_ref[...]

        def put(pair, h, nat_ref, t_ref):
            swapped = pltpu.roll(pair, HEAD_DIM, 1)
            for head, dup in ((h, jnp.where(low, pair, swapped)), (h + 1, jnp.where(low, swapped, pair))):
                nat_ref[head] = dup.astype(BF16)
                t_ref[head] = dup.T.astype(BF16)

        for j in range(kvw // LANES):
            put(_rope(kv[:, j * LANES:(j + 1) * LANES], cos, sa, sb), 2 * j, kd_ref, kt_ref)
            put(kv[:, kvw + j * LANES:kvw + (j + 1) * LANES], 2 * j, vd_ref, vt_ref)

    tab_spec = pl.BlockSpec((tile, LANES), lambda i: (i, 0))
    dup_spec = pl.BlockSpec((n_kv, tile, LANES), lambda i: (0, i, 0))
    dup_shape = jax.ShapeDtypeStruct((n_kv, seq, LANES), BF16)
    t_spec = pl.BlockSpec((n_kv, LANES, tile), lambda i: (0, 0, i))
    t_shape = jax.ShapeDtypeStruct((n_kv, LANES, seq), BF16)
    return pl.pallas_call(
        body, name="kv_proj", grid=(seq // tile,),
        out_shape=(dup_shape, dup_shape, t_shape, t_shape),
        in_specs=[pl.BlockSpec((tile, dm), lambda i: (i, 0)), pl.BlockSpec(wkv.shape, lambda i: (0, 0)),
                  tab_spec, tab_spec, tab_spec],
        out_specs=(dup_spec, dup_spec, t_spec, t_spec),
        compiler_params=_params("parallel"),
    )(xb, wkv, *tables)


ATTN_STEP_BLOCKS = 8


def _head_queries(q_ref, rows, low):
    parts = []
    for j in range(GQA_GROUP // 2):
        q2 = q_ref[rows, j * LANES:(j + 1) * LANES]
        parts += [jnp.where(low, q2, 0), jnp.where(low, 0, q2)]
    return parts


def _key_window(prev_ref, cur_ref, b, axis):
    def block(i):
        sl = slice(i * ATTN_BLOCK, (i + 1) * ATTN_BLOCK)
        return cur_ref[sl, :] if axis == 0 else cur_ref[:, sl]
    return jnp.concatenate([prev_ref[...] if b == 0 else block(b - 1), block(b)], axis=axis)


def _probs_transposed(n, kh, kcat, q_all, sink_ref):
    st = lax.dot_general(kcat, q_all, NT, preferred_element_type=F32)
    key = lax.broadcasted_iota(jnp.int32, (2 * ATTN_BLOCK, ATTN_BLOCK), 0)
    qry = lax.broadcasted_iota(jnp.int32, (2 * ATTN_BLOCK, ATTN_BLOCK), 1)
    valid = (key > qry) & (key <= qry + ATTN_BLOCK) & ((key >= ATTN_BLOCK) | (n > 0))
    st = st + jnp.tile(jnp.where(valid, 0.0, NEG_INF), (1, GQA_GROUP))
    sink = jnp.concatenate([jnp.full((1, ATTN_BLOCK), sink_ref[0, kh * GQA_GROUP + h], F32)
                            for h in range(GQA_GROUP)], axis=1)
    m = jnp.maximum(jnp.max(st, axis=0, keepdims=True), sink)
    e = jnp.exp(st - m)
    e_sink = jnp.exp(sink - m)
    inv = 1.0 / (jnp.sum(e, axis=0, keepdims=True) + e_sink)
    return e * inv, e_sink * inv


def _attn_specs(n_width, qb):
    rows = qb * ATTN_BLOCK
    before = lambda n: jnp.maximum(n * qb - 1, 0)
    q_spec = pl.BlockSpec((rows, n_width), lambda kh, n: (n, kh))
    cur = pl.BlockSpec((None, rows, LANES), lambda kh, n: (kh, n, 0))
    prev = pl.BlockSpec((None, ATTN_BLOCK, LANES), lambda kh, n: (kh, before(n), 0))
    cur_t = pl.BlockSpec((None, LANES, rows), lambda kh, n: (kh, 0, n))
    prev_t = pl.BlockSpec((None, LANES, ATTN_BLOCK), lambda kh, n: (kh, 0, before(n)))
    return q_spec, cur, prev, cur_t, prev_t


def _pair_product_transposed(mat_t, rhs, j, low_rows):
    a = rhs[:, 2 * j * ATTN_BLOCK:(2 * j + 1) * ATTN_BLOCK]
    b = rhs[:, (2 * j + 1) * ATTN_BLOCK:(2 * j + 2) * ATTN_BLOCK]
    out_t = (jnp.dot(jnp.where(low_rows, mat_t, 0), a, preferred_element_type=F32)
             + jnp.dot(jnp.where(low_rows, 0, mat_t), b, preferred_element_type=F32))
    return out_t.T


def _attn_forward(qs, kd, vt, zb, sinks):
    seq, dm = qs.shape
    n_kv = kd.shape[0]
    gw = GQA_GROUP * HEAD_DIM

    qb = ATTN_STEP_BLOCKS if (seq // ATTN_BLOCK) % ATTN_STEP_BLOCKS == 0 else 1

    def body(q_ref, kp_ref, kc_ref, vtp_ref, vtc_ref, z_ref, sink_ref, att_ref, yb_ref):
        kh, n = pl.program_id(0), pl.program_id(1)
        low = lax.broadcasted_iota(jnp.int32, (1, LANES), 1) < HEAD_DIM
        low_rows = lax.broadcasted_iota(jnp.int32, (LANES, 1), 0) < HEAD_DIM
        for b in range(qb):
            rows = slice(b * ATTN_BLOCK, (b + 1) * ATTN_BLOCK)
            kcat = _key_window(kp_ref, kc_ref, b, 0)
            vt = _key_window(vtp_ref, vtc_ref, b, 1)
            q_all = jnp.concatenate(_head_queries(q_ref, rows, low), axis=0)
            probs_t, _ = _probs_transposed(n * qb + b, kh, kcat, q_all, sink_ref)
            pt = probs_t.astype(BF16)
            for j in range(GQA_GROUP // 2):
                cs = slice(j * LANES, (j + 1) * LANES)
                o2 = _pair_product_transposed(vt, pt, j, low_rows)
                att_ref[rows, cs] = o2
                z = z_ref[rows, cs]
                yb_ref[rows, cs] = (o2 * (z * jax.nn.sigmoid(z))).astype(BF16)

    q_spec, cur, prev, cur_t, prev_t = _attn_specs(gw, qb)
    return pl.pallas_call(
        body, name="attn_fwd", grid=(n_kv, seq // (qb * ATTN_BLOCK)),
        out_shape=(jax.ShapeDtypeStruct((seq, dm), F32), jax.ShapeDtypeStruct((seq, dm), BF16)),
        in_specs=[q_spec, prev, cur, prev_t, cur_t, q_spec, pl.BlockSpec(memory_space=pltpu.SMEM)],
        out_specs=(q_spec, q_spec),
        compiler_params=_params("parallel", "parallel"),
    )(qs, kd, kd, vt, vt, zb, sinks)


def _attn_backward(qs, kd, vd, kt, zb, att, dyb, sinks, tables, after):
    seq, dm = qs.shape
    n_kv = kd.shape[0]
    gw = GQA_GROUP * HEAD_DIM
    n_blocks = seq // ATTN_BLOCK
    qb = ATTN_STEP_BLOCKS if n_blocks % ATTN_STEP_BLOCKS == 0 else 1

    def body(q_ref, kp_ref, kc_ref, vp_ref, vc_ref, ktp_ref, ktc_ref, z_ref, att_ref, dyb_ref, sink_ref,
             cos_ref, sa_ref, sb_ref, after_ref, dh_ref, dk_ref, dv_ref, ds_ref):
        del after_ref
        kh, n = pl.program_id(0), pl.program_id(1)

        @pl.when(n == 0)
        def _():
            dk_ref[...] = jnp.zeros_like(dk_ref)
            dv_ref[...] = jnp.zeros_like(dv_ref)

        @pl.when(jnp.logical_and(n == 0, kh == 0))
        def _():
            ds_ref[...] = jnp.zeros_like(ds_ref)

        low = lax.broadcasted_iota(jnp.int32, (1, LANES), 1) < HEAD_DIM
        low_rows = lax.broadcasted_iota(jnp.int32, (LANES, 1), 0) < HEAD_DIM
        head_lane = lax.broadcasted_iota(jnp.int32, (1, LANES), 1)
        dsink = jnp.zeros((1, LANES), F32)
        for b in range(qb):
            rows = slice(b * ATTN_BLOCK, (b + 1) * ATTN_BLOCK)
            kcat = _key_window(kp_ref, kc_ref, b, 0)
            vcat = _key_window(vp_ref, vc_ref, b, 0)
            kt = _key_window(ktp_ref, ktc_ref, b, 1)
            cos, sa, sb = cos_ref[rows, :], sa_ref[rows, :], sb_ref[rows, :]
            q_all = jnp.concatenate(_head_queries(q_ref, rows, low), axis=0)
            d_parts = []
            for j in range(GQA_GROUP // 2):
                cs = slice(j * LANES, (j + 1) * LANES)
                sz, dsz = _silu_parts(z_ref[rows, cs])
                dy2 = dyb_ref[rows, cs]
                dh_ref[1, rows, cs] = (dy2 * att_ref[rows, cs] * dsz).astype(BF16)
                datt = (dy2 * sz).astype(BF16)
                d_parts += [jnp.where(low, datt, 0), jnp.where(low, 0, datt)]
            d_all = jnp.concatenate(d_parts, axis=0)
            probs_t, sink_p = _probs_transposed(n * qb + b, kh, kcat, q_all, sink_ref)
            dprobs_t = lax.dot_general(vcat, d_all, NT, preferred_element_type=F32)
            row_dot = jnp.sum(probs_t * dprobs_t, axis=0, keepdims=True)
            ds_t = (probs_t * (dprobs_t - row_dot)).astype(BF16)
            dk = jnp.dot(ds_t, q_all, preferred_element_type=F32)
            dv = jnp.dot(probs_t.astype(BF16), d_all, preferred_element_type=F32)
            for j in range(GQA_GROUP // 2):
                dq2 = _pair_product_transposed(kt, ds_t, j, low_rows)
                dh_ref[0, rows, j * LANES:(j + 1) * LANES] = (
                    _rope_transposed(dq2, cos, sa, sb) * 0.125).astype(BF16)
            sink_dot = sink_p * row_dot
            for h in range(GQA_GROUP):
                part = jnp.sum(sink_dot[:, h * ATTN_BLOCK:(h + 1) * ATTN_BLOCK], axis=1, keepdims=True)
                dsink = dsink - jnp.where(head_lane == kh * GQA_GROUP + h, part, 0.0)

            def add_window(dk=dk, dv=dv, b=b):
                start = pl.multiple_of((n * qb + b - 1) * ATTN_BLOCK, ATTN_BLOCK)
                dk_ref[pl.ds(start, 2 * ATTN_BLOCK), :] += dk
                dv_ref[pl.ds(start, 2 * ATTN_BLOCK), :] += dv

            if b > 0:
                add_window()
            else:
                pl.when(n > 0)(add_window)

                @pl.when(n == 0)
                def _(dk=dk, dv=dv):
                    dk_ref[pl.ds(0, ATTN_BLOCK), :] += dk[ATTN_BLOCK:, :]
                    dv_ref[pl.ds(0, ATTN_BLOCK), :] += dv[ATTN_BLOCK:, :]
        ds_ref[0:1, :] += dsink

    q_spec, cur, prev, cur_t, prev_t = _attn_specs(gw, qb)
    tab_spec = pl.BlockSpec((qb * ATTN_BLOCK, LANES), lambda kh, n: (n, 0))
    acc_spec = pl.BlockSpec((None, seq, LANES), lambda kh, n: (kh, 0, 0))
    acc_shape = jax.ShapeDtypeStruct((n_kv, seq, LANES), F32)
    return pl.pallas_call(
        body, name="attn_bwd", grid=(n_kv, n_blocks // qb),
        out_shape=(jax.ShapeDtypeStruct((2, seq, dm), BF16), acc_shape, acc_shape,
                   jax.ShapeDtypeStruct((STAT_ROWS, LANES), F32)),
        in_specs=[q_spec, prev, cur, prev, cur, prev_t, cur_t, q_spec, q_spec, q_spec,
                  pl.BlockSpec(memory_space=pltpu.SMEM), tab_spec, tab_spec, tab_spec, ANY],
        out_specs=(pl.BlockSpec((2, qb * ATTN_BLOCK, gw), lambda kh, n: (0, n, kh)), acc_spec, acc_spec,
                   pl.BlockSpec((STAT_ROWS, LANES), lambda kh, n: (0, 0))),
        compiler_params=_params("arbitrary", "arbitrary"),
    )(qs, kd, kd, vd, vd, kt, kt, zb, att, dyb, sinks, *tables, after)


def _kv_grad_fold(dk, dv, tables):
    n_kv, seq, _ = dk.shape
    kvw = n_kv * HEAD_DIM
    tile = _tile(seq, 512)

    def body(dk_ref, dv_ref, cos_ref, sa_ref, sb_ref, o_ref):
        low = lax.broadcasted_iota(jnp.int32, (1, LANES), 1) < HEAD_DIM
        cos, sa, sb = cos_ref[...], sa_ref[...], sb_ref[...]

        def folded(ref, h):
            t = ref[h]
            return t + pltpu.roll(t, HEAD_DIM, 1)

        for j in range(n_kv // 2):
            ka = _rope_transposed(folded(dk_ref, 2 * j), cos, sa, sb)
            kb = _rope_transposed(folded(dk_ref, 2 * j + 1), cos, sa, sb)
            o_ref[:, j * LANES:(j + 1) * LANES] = jnp.where(low, ka, kb).astype(BF16)
            o_ref[:, kvw + j * LANES:kvw + (j + 1) * LANES] = jnp.where(
                low, folded(dv_ref, 2 * j), folded(dv_ref, 2 * j + 1)).astype(BF16)

    tab_spec = pl.BlockSpec((tile, LANES), lambda i: (i, 0))
    in_spec = pl.BlockSpec((n_kv, tile, LANES), lambda i: (0, i, 0))
    return pl.pallas_call(
        body, name="kv_grad_fold", grid=(seq // tile,),
        out_shape=jax.ShapeDtypeStruct((seq, 2 * kvw), BF16),
        in_specs=[in_spec, in_spec, tab_spec, tab_spec, tab_spec],
        out_specs=pl.BlockSpec((tile, 2 * kvw), lambda i: (i, 0)),
        compiler_params=_params("parallel"),
    )(dk, dv, *tables)


def _out_proj_norm_loss(yb, w, xhat1, gain0, bias0, gain1, bias1, target):
    seq, dm = xhat1.shape
    tile = _tile(seq, 512)

    def body(y_ref, w_ref, xh1_ref, g0_ref, b0_ref, g1_ref, b1_ref, t_ref, dr_ref, drb_ref, st_ref):
        i = pl.program_id(0)

        @pl.when(i == 0)
        def _():
            st_ref[...] = jnp.zeros_like(st_ref)

        ob = jnp.dot(y_ref[...], w_ref[...], preferred_element_type=F32)
        x1 = xh1_ref[...] * g0_ref[...] + b0_ref[...]
        xhat, rstd = _layer_norm_stats(ALPHA * x1 + ob)
        err = xhat * g1_ref[...] + b1_ref[...] - t_ref[...]
        dout = err * (1.0 / dm)
        dr = _layer_norm_backward(dout, xhat, rstd, g1_ref[...])
        dr_ref[...] = dr
        drb_ref[...] = dr.astype(BF16)
        st_ref[0:1, :] += _col_sum(dout * xhat)
        st_ref[1:2, :] += _col_sum(dout)
        st_ref[2:3, :] += _col_sum(err * err)

    row_spec = pl.BlockSpec((tile, dm), lambda i: (i, 0))
    vec_spec = pl.BlockSpec((1, dm), lambda i: (0, 0))
    return pl.pallas_call(
        body, name="out_proj_norm_loss_b", grid=(seq // tile,),
        out_shape=(jax.ShapeDtypeStruct((seq, dm), F32), jax.ShapeDtypeStruct((seq, dm), BF16),
                   jax.ShapeDtypeStruct((STAT_ROWS, dm), F32)),
        in_specs=[row_spec, pl.BlockSpec(w.shape, lambda i: (0, 0), pipeline_mode=pl.Buffered(1)), row_spec, vec_spec,
                  vec_spec, vec_spec,
                  vec_spec, row_spec],
        out_specs=(row_spec, row_spec, pl.BlockSpec((STAT_ROWS, dm), lambda i: (0, 0))),
        compiler_params=_params("arbitrary"),
    )(yb, w, xhat1, gain0, bias0, gain1, bias1, target)


def _stream_grad_norm_backward(dhq, wqg, dkv, wkv, dr2, xhat1, rstd1, gain0, after):
    seq, dm = dr2.shape
    tile = _tile(seq, 256)

    def body(dh_ref, wqg_ref, dkv_ref, wkv_ref, dr2_ref, xh_ref, rstd_ref, g_ref, after_ref, dr_ref, drb_ref, st_ref):
        del after_ref

        @pl.when(pl.program_id(0) == 0)
        def _():
            st_ref[...] = jnp.zeros_like(st_ref)

        dx1 = (lax.dot_general(dh_ref[0], wqg_ref[:, :dm], NT, preferred_element_type=F32)
               + lax.dot_general(dh_ref[1], wqg_ref[:, dm:], NT, preferred_element_type=F32)
               + lax.dot_general(dkv_ref[...], wkv_ref[...], NT, preferred_element_type=F32)
               + ALPHA * dr2_ref[...])
        xhat = xh_ref[...]
        dr = _layer_norm_backward(dx1, xhat, rstd_ref[...], g_ref[...])
        dr_ref[...] = dr
        drb_ref[...] = dr.astype(BF16)
        st_ref[0:1, :] += _col_sum(dx1 * xhat)
        st_ref[1:2, :] += _col_sum(dx1)

    row_spec = pl.BlockSpec((tile, dm), lambda i: (i, 0))
    resident = pl.Buffered(1)
    return pl.pallas_call(
        body, name="stream_grad_norm_bwd", grid=(seq // tile,),
        out_shape=(jax.ShapeDtypeStruct((seq, dm), F32), jax.ShapeDtypeStruct((seq, dm), BF16),
                   jax.ShapeDtypeStruct((STAT_ROWS, dm), F32)),
        in_specs=[pl.BlockSpec((2, tile, dm), lambda i: (0, i, 0)),
                  pl.BlockSpec(wqg.shape, lambda i: (0, 0), pipeline_mode=resident),
                  pl.BlockSpec((tile, dkv.shape[1]), lambda i: (i, 0)),
                  pl.BlockSpec(wkv.shape, lambda i: (0, 0), pipeline_mode=resident),
                  row_spec, row_spec, pl.BlockSpec((tile, 1), lambda i: (i, 0)),
                  pl.BlockSpec((1, dm), lambda i: (0, 0)), ANY],
        out_specs=(row_spec, row_spec, pl.BlockSpec((STAT_ROWS, dm), lambda i: (0, 0))),
        compiler_params=_params("arbitrary"),
    )(dhq, wqg, dkv, wkv, dr2, xhat1, rstd1, gain0, after)


def _adamw_math(w, g, m, v):
    m = ADAM_B1 * m + (1.0 - ADAM_B1) * g
    v = ADAM_B2 * v + (1.0 - ADAM_B2) * (g * g)
    m_hat = m / (1.0 - ADAM_B1 ** ADAM_STEP)
    v_hat = v / (1.0 - ADAM_B2 ** ADAM_STEP)
    delta = -ADAM_LR * (m_hat / (jnp.sqrt(v_hat) + ADAM_EPS) + ADAM_WD * w)
    return delta, m, v


def _sum_devices(ref):
    total = ref[0].astype(F32)
    for d in range(1, ref.shape[0]):
        total = total + ref[d].astype(F32)
    return total


def _adamw_shard(name, parts, w, m, v, after):
    rows, cols = w.shape
    n_parts = len(parts)
    part_rows = rows // n_parts
    tr = _tile(part_rows, max(8, (1 << 18) // cols)) if part_rows >= 8 else part_rows
    per_part = part_rows // tr

    def body(*refs):
        p_refs = refs[:n_parts]
        w_ref, m_ref, v_ref, _, g_out, d_out, m_out, v_out = refs[n_parts:]
        g = _sum_devices(p_refs[0])
        for k in range(1, n_parts):
            g = jnp.where(pl.program_id(0) >= k * per_part, _sum_devices(p_refs[k]), g)
        delta, m_new, v_new = _adamw_math(w_ref[...], g, m_ref[...], v_ref[...])
        g_out[...] = g
        d_out[...] = delta
        m_out[...] = m_new
        v_out[...] = v_new

    def part_spec(k):
        return pl.BlockSpec((parts[k].shape[0], tr, cols),
                            lambda i: (0, jnp.clip(i - k * per_part, 0, per_part - 1), 0))

    spec = pl.BlockSpec((tr, cols), lambda i: (i, 0))
    shape = jax.ShapeDtypeStruct((rows, cols), F32)
    return pl.pallas_call(
        body, name=name, grid=(rows // tr,),
        out_shape=(shape, shape, shape, shape),
        in_specs=[part_spec(k) for k in range(n_parts)] + [spec, spec, spec, ANY],
        out_specs=(spec, spec, spec, spec),
        compiler_params=_params("arbitrary"),
    )(*parts, w, m, v, after)


def _adamw_replicated(stats_b, stats_a, sink_parts, ln_g, ln_b, sinks, m_ln_g, m_ln_b, m_sinks, v_ln_g, v_ln_b,
                      v_sinks, after):
    n_q = sinks.shape[1]
    dm = ln_g.shape[1]

    def body(sb_ref, sa_ref, sk_ref, g_ref, b_ref, s_ref, mg_ref, mb_ref, ms_ref, vg_ref, vb_ref, vs_ref, after_ref,
             *outs):
        del after_ref
        layer_sums = (_sum_devices(sa_ref), _sum_devices(sb_ref))
        outs[12][...] = jnp.sum(layer_sums[1][2:3, :], axis=1, keepdims=True) * (0.5 / dm)
        for which, (w_ref, m_ref, v_ref) in enumerate(((g_ref, mg_ref, vg_ref), (b_ref, mb_ref, vb_ref))):
            for layer in range(DEPTH):
                row = slice(layer, layer + 1)
                g = layer_sums[layer][which:which + 1, :]
                res = (g,) + _adamw_math(w_ref[row, :], g, m_ref[row, :], v_ref[row, :])
                for o_ref, val in zip(outs[4 * which:4 * which + 4], res):
                    o_ref[row, :] = val
        g = _sum_devices(sk_ref)[0:1, 0:n_q]
        res = (g,) + _adamw_math(s_ref[...], g, ms_ref[...], vs_ref[...])
        for o_ref, val in zip(outs[8:12], res):
            o_ref[...] = val

    vmem = pl.BlockSpec(memory_space=pltpu.VMEM)
    shapes = [jax.ShapeDtypeStruct(a.shape, F32) for a in (ln_g, ln_b, sinks) for _ in range(4)]
    shapes.append(jax.ShapeDtypeStruct((1, 1), F32))
    return pl.pallas_call(
        body, name="adamw_replicated", out_shape=tuple(shapes),
        in_specs=[vmem] * 12 + [ANY], out_specs=tuple([vmem] * 13),
    )(stats_b, stats_a, sink_parts, ln_g, ln_b, sinks, m_ln_g, m_ln_b, m_sinks, v_ln_g, v_ln_b, v_sinks, after)


def kernel(x, ln_g, ln_b, a_w_in, a_w_group, a_scale, a_w_out, b_w_k, b_w_v, b_w_qg, b_sinks, b_w_out, loss_target, m_ln_g, m_ln_b, m_a_w_in, m_a_w_group, m_a_scale, m_a_w_out, m_b_w_k, m_b_w_v, m_b_w_qg, m_b_sinks, m_b_w_out, v_ln_g, v_ln_b, v_a_w_in, v_a_w_group, v_a_scale, v_a_w_out, v_b_w_k, v_b_w_v, v_b_w_qg, v_b_sinks, v_b_w_out):
    _, seq, dm = x.shape
    n_groups = len(POOL_WINDOWS)
    gd = dm // n_groups
    kvw = b_w_k.shape[1]
    cb = 2 * dm // N_DEV
    rb = dm // N_DEV
    gb = gd // N_DEV

    x2 = x.reshape(seq, dm)
    target = loss_target.reshape(seq, dm)
    w_in_s = a_w_in.reshape(dm, cb)
    w_g_s = a_w_group.reshape(n_groups, gb, gd)
    w_out_s = a_w_out.reshape(rb, dm)
    w_qg_s = b_w_qg.reshape(dm, cb)
    w_outb_s = b_w_out.reshape(rb, dm)

    def cols(ref, dev):
        return ref.at[:, pl.ds(pl.multiple_of(dev * cb, LANES), cb)]

    def rows(ref, dev):
        return ref.at[pl.ds(pl.multiple_of(dev * rb, 8), rb), :]

    def group_rows(ref, dev):
        return ref.at[:, pl.ds(pl.multiple_of(dev * gb, 8), gb), :]

    def k_rows(ref, dev):
        return ref.at[pl.ds(pl.multiple_of(dev * rb, 8), rb), pl.ds(0, kvw)]

    def v_rows(ref, dev):
        return ref.at[pl.ds(pl.multiple_of(dev * rb, 8), rb), pl.ds(kvw, kvw)]

    def scale_cols(ref, dev):
        return ref.at[:, pl.ds(pl.multiple_of(dev * rb, LANES), rb)]

    bf = lambda a: a.astype(BF16)
    wide, square = jax.ShapeDtypeStruct((dm, 2 * dm), BF16), jax.ShapeDtypeStruct((dm, dm), BF16)
    w_g, scale, w_in = _gather_weights(
        "gather_a_in", 0, [(bf(w_g_s), 0, group_rows), (a_scale, 1, scale_cols), (bf(w_in_s), 2, cols)],
        [jax.ShapeDtypeStruct((n_groups, gd, gd), BF16), jax.ShapeDtypeStruct((1, dm), F32), wide])
    (w_out,) = _gather_weights("gather_a_out", 1, [(bf(w_out_s), 0, rows)], [square])
    w_kv, w_qg = _gather_weights(
        "gather_b_in", 2, [(bf(b_w_k), 0, k_rows), (bf(b_w_v), 0, v_rows), (bf(w_qg_s), 1, cols)],
        [jax.ShapeDtypeStruct((dm, 2 * kvw), BF16), wide])
    (w_outb,) = _gather_weights("gather_b_out", 3, [(bf(w_outb_s), 0, rows)], [square])

    tables = _rope_tables(seq)
    bm = _tile(seq, 1024)
    bn = _tile(dm, 1024)
    g0, g1, b0, b1 = ln_g[0:1], ln_g[1:2], ln_b[0:1], ln_b[1:2]

    xb = _cast_bf16("cast_x", x2)
    y, pooled, mixed, z_a = _pool_forward(xb, w_in, w_g, scale)
    xhat1, rstd1, x1b = _out_proj_norm(y, w_out, x2, g0, b0)

    kd, vd, kt, vt = _kv_proj(x1b, w_kv, tables)
    bmq = bm
    tab_spec = pl.BlockSpec((bmq, LANES), lambda i, j: (i, 0))

    def rope_scale(val, cos_ref, sa_ref, sb_ref):
        cos, sa, sb = cos_ref[...], sa_ref[...], sb_ref[...]
        return jnp.concatenate([_rope(val[:, j * LANES:(j + 1) * LANES], cos, sa, sb) * 0.125
                                for j in range(val.shape[1] // LANES)], axis=1)

    qs = _mm("b_q_proj", x1b, w_qg, dims=NN, grid=(seq // bmq, dm // bn),
             a_spec=pl.BlockSpec((bmq, dm), lambda i, j: (i, 0)), b_spec=pl.BlockSpec((dm, bn), lambda i, j: (0, j)),
             out_shape=jax.ShapeDtypeStruct((seq, dm), BF16), out_spec=pl.BlockSpec((bmq, bn), lambda i, j: (i, j)),
             epilogue=rope_scale, extras=tables, extra_specs=(tab_spec,) * 3)
    zb = _mm("b_gate_proj", x1b, w_qg, dims=NN, grid=(seq // bm, dm // bn),
             a_spec=pl.BlockSpec((bm, dm), lambda i, j: (i, 0)),
             b_spec=pl.BlockSpec((dm, bn), lambda i, j: (0, j + dm // bn)),
             out_shape=jax.ShapeDtypeStruct((seq, dm), F32), out_spec=pl.BlockSpec((bm, bn), lambda i, j: (i, j)))
    att, yb = _attn_forward(qs, kd, vt, zb, b_sinks)
    dr2, dr2b, stats_b = _out_proj_norm_loss(yb, w_outb, xhat1, g0, b0, g1, b1, target)

    def weight_grad(name, a, b, n_cols, b_spec=None, part=(0, 1), after=None):
        m_cols = a.shape[1] // part[1]
        tm, tn = _tile(m_cols, 1024), _tile(n_cols, 512)
        first = part[0] * (m_cols // tm)
        return _mm(name, a, b, dims=TN, grid=(m_cols // tm, n_cols // tn),
                   a_spec=pl.BlockSpec((seq, tm), lambda i, j: (0, first + i)),
                   b_spec=b_spec(tn) if b_spec else pl.BlockSpec((seq, tn), lambda i, j: (0, j)),
                   out_shape=jax.ShapeDtypeStruct((m_cols, n_cols), BF16),
                   out_spec=pl.BlockSpec((tm, tn), lambda i, j: (i, j)),
                   extras=() if after is None else (after,), extra_specs=() if after is None else (ANY,))

    def halves_spec(tn):
        per = dm // tn
        return pl.BlockSpec((None, seq, tn), lambda i, j: (j // per, 0, j % per))

    def times_transposed(name, a, w):
        return _mm(name, a, w, dims=NT, grid=(seq // bm, dm // bn),
                   a_spec=pl.BlockSpec((bm, a.shape[1]), lambda i, j: (i, 0)),
                   b_spec=pl.BlockSpec((bn, w.shape[1]), lambda i, j: (j, 0)),
                   out_shape=jax.ShapeDtypeStruct((seq, dm), F32), out_spec=pl.BlockSpec((bm, bn), lambda i, j: (i, j)))

    def stat_row_cols(ref, dev):
        return ref.at[pl.ds(0, 1), pl.ds(pl.multiple_of(dev * rb, LANES), rb)]

    upd = {}
    last = [dr2b]
    my_core = lax.axis_index("c").astype(jnp.int32).reshape(1)

    def then(value):
        last[0] = value[0] if isinstance(value, (list, tuple)) else value
        return value

    def shard_update(key, parts, w, m, v):
        shape = w.shape
        flat = lambda a: a.reshape(-1, shape[-1])
        parts = list(parts) if isinstance(parts, (list, tuple)) else [parts]
        outs = then(_adamw_shard("adamw_" + key, [p.reshape(p.shape[0], -1, shape[-1]) for p in parts], flat(w),
                                 flat(m), flat(v), last[0]))
        upd[key] = [o.reshape(shape) for o in outs]

    def two_level_scatter(name, ids, streams):
        staged = _sibling_exchange(name + "_pair", streams, ids[0])

        def finish():
            sums = [then(_pair_sum(f"{name}_sum{s}", st[0], got, my_core, last[0]))
                    for s, (st, got) in enumerate(zip(streams, staged))]
            return _chip_exchange(name + "_chip", sums, ids[1])
        return finish

    d_w_outb = then(weight_grad("b_out_proj_dw", yb, dr2b, dm))
    (p_outb,) = _exchange_blocks("scatter_b_out", [(d_w_outb, rows, (rb, dm))], 4)
    dyb = times_transposed("b_out_proj_dx", dr2b, w_outb)
    dhq, dkd, dvd, dsink = then(_attn_backward(qs, kd, vd, kt, zb, att, dyb, b_sinks, tables, after=last[0]))
    dkv = _kv_grad_fold(dkd, dvd, tables)
    d_w_kv = weight_grad("b_kv_proj_dw", x1b, dkv, 2 * kvw)
    d_w_qg = then(weight_grad("b_qg_proj_dw", x1b, dhq, 2 * dm, halves_spec, after=d_w_kv))
    finish_b_in = two_level_scatter("scatter_b_in", (5, 11), [(d_w_qg, cols, (dm, cb))])
    dr1, dr1b, stats_a = _stream_grad_norm_backward(dhq, w_qg, dkv, w_kv, dr2, xhat1, rstd1, g0, after=last[0])
    last[0] = dr1b
    shard_update("b_w_out", p_outb, b_w_out, m_b_w_out, v_b_w_out)
    (p_qg,) = finish_b_in()
    all_b, all_a, all_sink = _exchange_blocks("gather_replicated_grads", [
        (stats_b, None, stats_b.shape), (stats_a, None, stats_a.shape), (dsink, None, dsink.shape)], 9)

    d_w_out = then(weight_grad("a_out_proj_dw", y, dr1b, dm, after=last[0]))
    p_out, p_k, p_v = _exchange_blocks("scatter_a_out", [
        (d_w_out, rows, (rb, dm)), (d_w_kv, k_rows, (rb, kvw)), (d_w_kv, v_rows, (rb, kvw))], 6)
    dy = times_transposed("a_out_proj_dx", dr1b, w_out)
    dh, d_w_g, stats_s = then(_pool_mid_backward(dy, mixed, z_a, pooled, w_g, scale, after=last[0]))
    p_g, p_scale = _exchange_blocks("scatter_a_mid", [
        (d_w_g, group_rows, (n_groups, gb, gd)), (stats_s, stat_row_cols, (1, rb))], 7)
    shard_update("b_w_qg", p_qg, b_w_qg, m_b_w_qg, v_b_w_qg)
    rep = then(_adamw_replicated(all_b, all_a, all_sink, ln_g, ln_b, b_sinks, m_ln_g, m_ln_b, m_b_sinks, v_ln_g,
                                 v_ln_b, v_b_sinks, last[0]))
    upd["ln_g"], upd["ln_b"], upd["b_sinks"] = list(rep[0:4]), list(rep[4:8]), list(rep[8:12])
    finish_a_in = []
    for k in range(2):
        d_w_in = then(weight_grad(f"a_in_proj_dw_{k}", xb, dh, 2 * dm, halves_spec, part=(k, 2), after=last[0]))
        finish_a_in.append(two_level_scatter(f"scatter_a_in_{k}", (8 + 2 * k, 12 + k), [(d_w_in, cols, (dm // 2, cb))]))
    shard_update("a_w_out", p_out, a_w_out, m_a_w_out, v_a_w_out)
    shard_update("b_w_k", p_k, b_w_k, m_b_w_k, v_b_w_k)
    shard_update("b_w_v", p_v, b_w_v, m_b_w_v, v_b_w_v)
    shard_update("a_w_group", p_g, a_w_group, m_a_w_group, v_a_w_group)
    shard_update("a_scale", p_scale, a_scale, m_a_scale, v_a_scale)
    p_in = list(finish_a_in[0]()) + list(finish_a_in[1]())
    grad_x = then(_mm("a_in_proj_dx", dh, w_in, dims=NT, grid=(seq // bm, dm // bn, 2), nk=2,
                      a_spec=pl.BlockSpec((None, bm, dm), lambda i, j, k: (k, i, 0)),
                      b_spec=pl.BlockSpec((bn, dm), lambda i, j, k: (j, k)),
                      out_shape=jax.ShapeDtypeStruct((seq, dm), F32),
                      out_spec=pl.BlockSpec((bm, bn), lambda i, j, k: (i, j)),
                      add=dr1, add_spec=pl.BlockSpec((bm, bn), lambda i, j, k: (i, j)), add_scale=ALPHA,
                      extras=(last[0],), extra_specs=(ANY,)))
    shard_update("a_w_in", p_in, a_w_in, m_a_w_in, v_a_w_in)

    loss = rep[12].reshape(())
    order = ["ln_g", "ln_b", "a_w_in", "a_w_group", "a_scale", "a_w_out", "b_w_k", "b_w_v", "b_w_qg", "b_sinks",
             "b_w_out"]
    return (loss, grad_x.reshape(x.shape), *[upd[n][0] for n in order], *[upd[n][1] for n in order],
            *[upd[n][2] for n in order], *[upd[n][3] for n in order])
```

```python
import functools

import jax
import jax.numpy as jnp
from jax import lax
from jax.experimental import pallas as pl
from jax.experimental.pallas import tpu as pltpu
from jax.experimental.pallas import tpu_sc as plsc

F32 = jnp.float32
BF16 = jnp.bfloat16
MESH = pl.DeviceIdType.MESH
AXES = ("x", "y", "c")
N_DEV = 8

POOL_WINDOWS = (2, 4, 8, 16)
POOL_HALO = 16
HEAD_DIM = 64
GQA_GROUP = 8
ATTN_BLOCK = 128
ROPE_THETA = 10000.0
LN_EPS = 1e-5
NEG_INF = -1e30
DEPTH = 2
ALPHA = (2 * DEPTH) ** 0.25
ADAM_LR = 0.001
ADAM_B1 = 0.9
ADAM_B2 = 0.999
ADAM_EPS = 1e-08
ADAM_WD = 0.01
ADAM_STEP = 10

LANES = 128
STAT_ROWS = 8


def _tile(n, want):
    t = min(n, want)
    while n % t:
        t //= 2
    return t


def _params(*sem):
    return pltpu.CompilerParams(dimension_semantics=sem)


ANY = pl.BlockSpec(memory_space=pl.ANY)


def _my_pos():
    return lax.axis_index("x"), lax.axis_index("y"), lax.axis_index("c")


def _dev_index(p):
    return 4 * p[0] + 2 * p[1] + p[2]


def _handshake(peers):
    barrier = pltpu.get_barrier_semaphore()
    for peer in peers:
        pl.semaphore_signal(barrier, inc=1, device_id=peer, device_id_type=MESH)
    pl.semaphore_wait(barrier, len(peers))


def _launch_on_sequencer(name, collective_id, body, operands, out_shapes, scratch):
    return pl.kernel(
        body, out_type=tuple(out_shapes), name=name,
        mesh=plsc.ScalarSubcoreMesh(axis_name="sequencer", num_cores=1), scratch_types=scratch,
        compiler_params=pltpu.CompilerParams(collective_id=collective_id),
    )(*operands)


def _gather_weights(name, collective_id, streams, out_shapes):
    n_s = len(streams)
    n_out = len(out_shapes)

    def body(*refs):
        srcs = refs[:n_s]
        outs = refs[n_s:n_s + n_out]
        send_sems, recv_sems, local_sems = refs[n_s + n_out:]
        x, y, c = _my_pos()
        me, sibling = (x, y, c), (x, y, 1 - c)
        x_nbr, y_nbr, diag = (1 - x, y), (x, 1 - y), (1 - x, 1 - y)
        _handshake([sibling, (*x_nbr, c), (*y_nbr, c)])
        south = c == 0
        relay_from = (jnp.where(south, 1 - x, x), jnp.where(south, y, 1 - y))
        relay_to = (jnp.where(south, x, 1 - x), jnp.where(south, 1 - y, y))
        early, late = jnp.where(south, 1, 2), jnp.where(south, 2, 1)

        def copy(s, k, block, to, from_shard=False):
            out_ref = outs[streams[s][1]]
            win = streams[s][2](out_ref, _dev_index(block))
            return pltpu.make_async_remote_copy(
                src_ref=srcs[s] if from_shard else win, dst_ref=win,
                send_sem=send_sems.at[7 * s + k], recv_sem=recv_sems.at[7 * s + k],
                device_id=to, device_id_type=MESH)

        mine = [pltpu.make_async_copy(srcs[s], streams[s][2](outs[streams[s][1]], _dev_index(me)), local_sems.at[s])
                for s in range(n_s)]
        for cp in mine:
            cp.start()
        sent = []
        for s in range(n_s):
            sent += [copy(s, 0, me, sibling, True), copy(s, 1, me, (*x_nbr, c), True), copy(s, 2, me, (*y_nbr, c), True)]
        for cp in sent:
            cp.start()
        for s in range(n_s):
            copy(s, early, (*relay_from, c), me).wait_recv()
            sent += [copy(s, 3, (*relay_from, c), (*relay_to, c)), copy(s, 3 + early, (*relay_from, c), sibling)]
            for cp in sent[-2:]:
                cp.start()
        for s in range(n_s):
            copy(s, late, (*relay_to, c), me).wait_recv()
            sent.append(copy(s, 3 + late, (*relay_to, c), sibling))
            sent[-1].start()
        for s in range(n_s):
            copy(s, 3, (*diag, c), me).wait_recv()
            sent.append(copy(s, 6, (*diag, c), sibling))
            sent[-1].start()
        for s in range(n_s):
            copy(s, 0, sibling, me).wait_recv()
            for k, chip in ((4, x_nbr), (5, y_nbr), (6, diag)):
                copy(s, k, (*chip, 1 - c), me).wait_recv()
        for cp in sent:
            cp.wait_send()
        for cp in mine:
            cp.wait()

    scratch = [pltpu.SemaphoreType.DMA((7 * n_s,)), pltpu.SemaphoreType.DMA((7 * n_s,)),
               pltpu.SemaphoreType.DMA((n_s,))]
    return _launch_on_sequencer(name, collective_id, body, [s[0] for s in streams], out_shapes, scratch)


def _exchange_blocks(name, streams, collective_id):
    n_s = len(streams)

    def body(*refs):
        srcs = refs[:n_s]
        outs = refs[n_s:2 * n_s]
        send_sems, recv_sems, local_sems = refs[2 * n_s:]
        x, y, c = _my_pos()
        me = _dev_index((x, y, c))
        _handshake([(1 - x if k & 4 else x, 1 - y if k & 2 else y, 1 - c if k & 1 else c) for k in range(1, N_DEV)])

        def window(s, dev):
            return srcs[s] if streams[s][1] is None else streams[s][1](srcs[s], dev)

        mine = [pltpu.make_async_copy(window(s, me), outs[s].at[me], local_sems.at[s]) for s in range(n_s)]
        for cp in mine:
            cp.start()
        copies = []
        for k in (2, 4, 6, 3, 5, 7, 1):
            peer = (1 - x if k & 4 else x, 1 - y if k & 2 else y, 1 - c if k & 1 else c)
            for s in range(n_s):
                copies.append(pltpu.make_async_remote_copy(
                    src_ref=window(s, _dev_index(peer)), dst_ref=outs[s].at[me],
                    send_sem=send_sems.at[7 * s + k - 1], recv_sem=recv_sems.at[7 * s + k - 1],
                    device_id=peer, device_id_type=MESH))
        for cp in copies:
            cp.start()
        for cp in copies:
            cp.wait()
        for cp in mine:
            cp.wait()

    out_shapes = [jax.ShapeDtypeStruct((N_DEV,) + tuple(s[2]), s[0].dtype) for s in streams]
    scratch = [pltpu.SemaphoreType.DMA((7 * n_s,)), pltpu.SemaphoreType.DMA((7 * n_s,)),
               pltpu.SemaphoreType.DMA((n_s,))]
    return _launch_on_sequencer(name, collective_id, body, [s[0] for s in streams], out_shapes, scratch)


N_CHIPS = 4


def _sibling_exchange(name, streams, collective_id):
    n_s = len(streams)

    def body(*refs):
        srcs = refs[:n_s]
        outs = refs[n_s:2 * n_s]
        send_sems, recv_sems = refs[2 * n_s:]
        x, y, c = _my_pos()
        sibling = (x, y, 1 - c)
        _handshake([sibling])
        copies = [pltpu.make_async_remote_copy(
            src_ref=streams[s][1](srcs[s], 2 * chip + (1 - c)), dst_ref=outs[s].at[chip],
            send_sem=send_sems.at[N_CHIPS * s + chip], recv_sem=recv_sems.at[N_CHIPS * s + chip],
            device_id=sibling, device_id_type=MESH) for s in range(n_s) for chip in range(N_CHIPS)]
        for cp in copies:
            cp.start()
        for cp in copies:
            cp.wait()

    out_shapes = [jax.ShapeDtypeStruct((N_CHIPS,) + tuple(s[2]), s[0].dtype) for s in streams]
    scratch = [pltpu.SemaphoreType.DMA((N_CHIPS * n_s,)), pltpu.SemaphoreType.DMA((N_CHIPS * n_s,))]
    return _launch_on_sequencer(name, collective_id, body, [s[0] for s in streams], out_shapes, scratch)


def _pair_sum(name, array, from_sibling, my_core, after):
    _, rows, cols = from_sibling.shape
    tr = _tile(rows, 512)

    def body(core_ref, own_ref, sib_ref, after_ref, o_ref):
        del core_ref, after_ref
        o_ref[...] = (own_ref[...].astype(F32) + sib_ref[...].astype(F32)).astype(o_ref.dtype)

    staged_spec = pl.BlockSpec((None, tr, cols), lambda k, i, core: (k, i, 0))
    return pl.pallas_call(
        body, name=name, out_shape=jax.ShapeDtypeStruct(from_sibling.shape, array.dtype),
        grid_spec=pltpu.PrefetchScalarGridSpec(
            num_scalar_prefetch=1, grid=(N_CHIPS, rows // tr),
            in_specs=[pl.BlockSpec((tr, cols), lambda k, i, core: (i, 2 * k + core[0])), staged_spec, ANY],
            out_specs=staged_spec),
        compiler_params=_params("parallel", "parallel"),
    )(my_core, array, from_sibling, after)


def _chip_exchange(name, pair_sums, collective_id):
    n_s = len(pair_sums)

    def body(*refs):
        srcs = refs[:n_s]
        outs = refs[n_s:2 * n_s]
        send_sems, recv_sems, local_sems = refs[2 * n_s:]
        x, y, c = _my_pos()
        my_chip = 2 * x + y
        chips = [(1 - x, y), (x, 1 - y), (1 - x, 1 - y)]
        _handshake([(*chip, c) for chip in chips])
        mine = [pltpu.make_async_copy(srcs[s].at[my_chip], outs[s].at[my_chip], local_sems.at[s]) for s in range(n_s)]
        copies = [pltpu.make_async_remote_copy(
            src_ref=srcs[s].at[2 * chip[0] + chip[1]], dst_ref=outs[s].at[my_chip],
            send_sem=send_sems.at[3 * s + j], recv_sem=recv_sems.at[3 * s + j],
            device_id=(*chip, c), device_id_type=MESH) for s in range(n_s) for j, chip in enumerate(chips)]
        for cp in mine + copies:
            cp.start()
        for cp in copies:
            cp.wait()
        for cp in mine:
            cp.wait()

    out_shapes = [jax.ShapeDtypeStruct(p.shape, p.dtype) for p in pair_sums]
    scratch = [pltpu.SemaphoreType.DMA((3 * n_s,)), pltpu.SemaphoreType.DMA((3 * n_s,)),
               pltpu.SemaphoreType.DMA((n_s,))]
    return _launch_on_sequencer(name, collective_id, body, list(pair_sums), out_shapes, scratch)


NN = (((1,), (0,)), ((), ()))
NT = (((1,), (1,)), ((), ()))
TN = (((0,), (0,)), ((), ()))


def _mm(name, a, b, *, dims, grid, a_spec, b_spec, out_shape, out_spec, nk=1,
        add=None, add_spec=None, add_scale=1.0, epilogue=None, extras=(), extra_specs=()):
    n_extra = len(extras)
    has_add = add is not None

    def body(*refs):
        a_ref, b_ref = refs[:2]
        pos = 2
        add_ref = None
        if has_add:
            add_ref = refs[pos]
            pos += 1
        extra_refs = refs[pos:pos + n_extra]
        o_ref = refs[pos + n_extra]
        acc_ref = refs[pos + n_extra + 1] if nk > 1 else None

        def finish(val):
            if has_add:
                val = val + add_scale * add_ref[...]
            if epilogue is not None:
                val = epilogue(val, *extra_refs)
            o_ref[...] = val.astype(o_ref.dtype)

        part = lax.dot_general(a_ref[...].astype(BF16), b_ref[...].astype(BF16), dims,
                               preferred_element_type=F32)
        if nk == 1:
            finish(part)
        else:
            k = pl.program_id(2)

            @pl.when(k == 0)
            def _():
                acc_ref[...] = part

            @pl.when(jnp.logical_and(k > 0, k < nk - 1))
            def _():
                acc_ref[...] += part

            @pl.when(k == nk - 1)
            def _():
                finish(acc_ref[...] + part)

    in_specs = [a_spec, b_spec] + ([add_spec] if has_add else []) + list(extra_specs)
    operands = [a, b] + ([add] if has_add else []) + list(extras)
    scratch = [pltpu.VMEM(out_spec.block_shape, F32)] if nk > 1 else []
    sem = ("parallel", "parallel") + (("arbitrary",) if nk > 1 else ())
    return pl.pallas_call(
        body, name=name, grid=grid, out_shape=out_shape,
        in_specs=in_specs, out_specs=out_spec, scratch_shapes=scratch,
        compiler_params=_params(*sem),
    )(*operands)


def _input_grad(dh, w_in, dr1, after):
    _, seq, dm = dh.shape
    bm, bn = _tile(seq, 1024), _tile(dm, 512)

    def body(dh_ref, w_ref, dr_ref, after_ref, o_ref):
        del after_ref
        o_ref[...] = (lax.dot_general(dh_ref[0], w_ref[:, :dm], NT, preferred_element_type=F32)
                      + lax.dot_general(dh_ref[1], w_ref[:, dm:], NT, preferred_element_type=F32)
                      + ALPHA * dr_ref[...])

    tile_spec = pl.BlockSpec((bm, bn), lambda i, j: (i, j))
    return pl.pallas_call(
        body, name="a_in_proj_dx", grid=(seq // bm, dm // bn),
        out_shape=jax.ShapeDtypeStruct((seq, dm), F32),
        in_specs=[pl.BlockSpec((2, bm, dm), lambda i, j: (0, i, 0)), pl.BlockSpec((bn, 2 * dm), lambda i, j: (j, 0)),
                  tile_spec, ANY],
        out_specs=tile_spec,
        compiler_params=_params("parallel", "parallel"),
    )(dh, w_in, dr1, after)


def _cast_bf16(name, a):
    rows, cols = a.shape
    tr = _tile(rows, 512)

    def body(a_ref, o_ref):
        o_ref[...] = a_ref[...].astype(BF16)

    return pl.pallas_call(
        body, name=name, grid=(rows // tr,),
        out_shape=jax.ShapeDtypeStruct(a.shape, BF16),
        in_specs=[pl.BlockSpec((tr, cols), lambda i: (i, 0))],
        out_specs=pl.BlockSpec((tr, cols), lambda i: (i, 0)),
        compiler_params=_params("parallel"),
    )(a)


def _rope_tables(seq):
    inv_freq = ROPE_THETA ** (-jnp.arange(0, HEAD_DIM, 2, dtype=F32) / HEAD_DIM)
    ang = jnp.arange(seq, dtype=F32)[:, None] * inv_freq[None, :]
    cos, sin = jnp.cos(ang), jnp.sin(ang)
    cos, sin = (jnp.concatenate([t, t, t, t], axis=-1) for t in (cos, sin))
    first_half = (jnp.arange(LANES) % HEAD_DIM < HEAD_DIM // 2)[None, :]
    return cos, jnp.where(first_half, -sin, 0.0), jnp.where(first_half, 0.0, sin)


def _rot(t, sin_a, sin_b):
    return pltpu.roll(t, LANES - HEAD_DIM // 2, 1) * sin_a + pltpu.roll(t, HEAD_DIM // 2, 1) * sin_b


def _rope(t, cos, sin_a, sin_b):
    return t * cos + _rot(t, sin_a, sin_b)


def _rope_transposed(dy, cos, sin_a, sin_b):
    return dy * cos - _rot(dy, sin_a, sin_b)


def _silu_parts(z):
    sig = jax.nn.sigmoid(z)
    return z * sig, sig * (1.0 + z * (1.0 - sig))


def _layer_norm_stats(r):
    mu = jnp.mean(r, axis=-1, keepdims=True)
    d = r - mu
    var = jnp.mean(d * d, axis=-1, keepdims=True)
    rstd = lax.rsqrt(var + LN_EPS)
    return d * rstd, rstd


def _layer_norm_backward(dout, xhat, rstd, gain):
    dxh = dout * gain
    m1 = jnp.mean(dxh, axis=-1, keepdims=True)
    m2 = jnp.mean(dxh * xhat, axis=-1, keepdims=True)
    return rstd * (dxh - m1 - xhat * m2)


def _col_sum(v):
    return jnp.sum(v, axis=0, keepdims=True)


ROW_PART = 256


def _row_parts(tile):
    part = min(tile, ROW_PART)
    return [slice(r, r + part) for r in range(0, tile, part)]


def _pool_forward(xb, w_in, wg, scale):
    seq, dm = xb.shape
    n_g = len(POOL_WINDOWS)
    gd = dm // n_g
    tile = _tile(seq, 1024)
    halo_blocks = tile // POOL_HALO

    def body(x_ref, xp_ref, wu_ref, wz_ref, wg_ref, sc_ref, y_ref, p_ref, mx_ref, z_ref):
        i, g = pl.program_id(0), pl.program_id(1)
        u = jnp.dot(x_ref[...], wu_ref[...], preferred_element_type=F32)
        z = jnp.dot(x_ref[...], wz_ref[...], preferred_element_type=F32)
        prev = jnp.where(i > 0, jnp.dot(xp_ref[...], wu_ref[...], preferred_element_type=F32), 0.0)
        s = jnp.concatenate([prev, u], axis=0)
        sums, sh = [], 1
        while sh < POOL_WINDOWS[-1]:
            s = s + pltpu.roll(s, sh, 0)
            sums.append(s)
            sh *= 2
        win = sums[-1]
        for k in range(n_g - 2, -1, -1):
            win = jnp.where(g == k, sums[k], win)
        row = i * tile + lax.broadcasted_iota(jnp.int32, (tile, 1), 0)
        window = jnp.left_shift(2, g).astype(F32)
        p = win[POOL_HALO:, :] * (1.0 / jnp.minimum((row + 1).astype(F32), window)) - u
        pb = p.astype(BF16)
        mx = jnp.dot(pb, wg_ref[...], preferred_element_type=F32)
        y_ref[...] = (mx * sc_ref[...] * (z * jax.nn.sigmoid(z))).astype(BF16)
        p_ref[...] = pb
        mx_ref[...] = mx
        z_ref[...] = z

    out_spec = pl.BlockSpec((tile, gd), lambda i, g: (i, g))
    return pl.pallas_call(
        body, name="pool_fwd", grid=(seq // tile, n_g),
        out_shape=(jax.ShapeDtypeStruct((seq, dm), BF16), jax.ShapeDtypeStruct((seq, dm), BF16),
                   jax.ShapeDtypeStruct((seq, dm), F32), jax.ShapeDtypeStruct((seq, dm), F32)),
        in_specs=[pl.BlockSpec((tile, dm), lambda i, g: (i, 0)),
                  pl.BlockSpec((POOL_HALO, dm), lambda i, g: (jnp.maximum(i * halo_blocks - 1, 0), 0)),
                  pl.BlockSpec((dm, gd), lambda i, g: (0, g)),
                  pl.BlockSpec((dm, gd), lambda i, g: (0, n_g + g)),
                  pl.BlockSpec((None, gd, gd), lambda i, g: (g, 0, 0)),
                  pl.BlockSpec((1, gd), lambda i, g: (0, g))],
        out_specs=(out_spec, out_spec, out_spec, out_spec),
        compiler_params=_params("parallel", "parallel"),
    )(xb, xb, w_in, w_in, wg, scale)


def _pool_mid_backward(dy, mx, z, p, wg, scale, after):
    seq, dm = dy.shape
    gd = dm // len(POOL_WINDOWS)
    tile = _tile(seq, 256)
    n_i = seq // tile

    def body(dy_ref, mx_ref, z_ref, p_ref, wg_ref, sc_ref, after_ref, dh_ref, dwg_ref, st_ref, dwg_acc, carry):
        del after_ref
        i = pl.program_id(0)
        ti = n_i - 1 - i

        @pl.when(i == 0)
        def _():
            dwg_acc[...] = jnp.zeros_like(dwg_acc)
            carry[...] = jnp.zeros_like(carry)
            st_ref[...] = jnp.zeros_like(st_ref)

        row = ti * tile + lax.broadcasted_iota(jnp.int32, (tile, 1), 0)
        count = (row + 1).astype(F32)
        for g, w in enumerate(POOL_WINDOWS):
            cs = slice(g * gd, (g + 1) * gd)
            z = z_ref[:, cs]
            sz, dsz = _silu_parts(z)
            dyg = dy_ref[:, cs]
            mxg = mx_ref[:, cs]
            sc = sc_ref[:, cs]
            t1 = dyg * sz
            st_ref[0:1, cs] += _col_sum(t1 * mxg)
            dh_ref[1, :, cs] = (dyg * (mxg * sc) * dsz).astype(BF16)
            dmx = (t1 * sc).astype(BF16)
            dwg_acc[g] += lax.dot_general(p_ref[:, cs], dmx, TN, preferred_element_type=F32)
            dp = lax.dot_general(dmx, wg_ref[g], NT, preferred_element_type=F32)
            e = dp * (1.0 / jnp.minimum(count, float(w)))
            s = jnp.concatenate([e, carry[:, cs]], axis=0)
            n = tile + POOL_HALO
            sh = 1
            while sh < w:
                s = s + pltpu.roll(s, n - sh, 0)
                sh *= 2
            dh_ref[0, :, cs] = (s[:tile, :] - dp).astype(BF16)
            carry[:, cs] = e[:POOL_HALO, :]

        @pl.when(i == n_i - 1)
        def _():
            dwg_ref[...] = dwg_acc[...].astype(BF16)

    row_spec = pl.BlockSpec((tile, dm), lambda i: (n_i - 1 - i, 0))
    return pl.pallas_call(
        body, name="pool_mid_bwd", grid=(n_i,),
        out_shape=(jax.ShapeDtypeStruct((2, seq, dm), BF16), jax.ShapeDtypeStruct(wg.shape, BF16),
                   jax.ShapeDtypeStruct((STAT_ROWS, dm), F32)),
        in_specs=[row_spec, row_spec, row_spec, row_spec,
                  pl.BlockSpec(wg.shape, lambda i: (0, 0, 0)),
                  pl.BlockSpec((1, dm), lambda i: (0, 0)), ANY],
        out_specs=(pl.BlockSpec((2, tile, dm), lambda i: (0, n_i - 1 - i, 0)),
                   pl.BlockSpec(wg.shape, lambda i: (0, 0, 0)),
                   pl.BlockSpec((STAT_ROWS, dm), lambda i: (0, 0))),
        scratch_shapes=[pltpu.VMEM(wg.shape, F32), pltpu.VMEM((POOL_HALO, dm), F32)],
        compiler_params=_params("arbitrary"),
    )(dy, mx, z, p, wg, scale, after)


def _out_proj_norm(y, w, x, gain, bias):
    seq, dm = x.shape
    tile = _tile(seq, 512)

    def body(y_ref, w_ref, x_ref, g_ref, b_ref, xhat_ref, rstd_ref, xb_ref):
        for rows in _row_parts(tile):
            o = jnp.dot(y_ref[rows, :], w_ref[...], preferred_element_type=F32)
            xhat, rstd = _layer_norm_stats(ALPHA * x_ref[rows, :] + o)
            xhat_ref[rows, :] = xhat
            rstd_ref[rows, :] = rstd
            xb_ref[rows, :] = (xhat * g_ref[...] + b_ref[...]).astype(BF16)

    row_spec = pl.BlockSpec((tile, dm), lambda i: (i, 0))
    vec_spec = pl.BlockSpec((1, dm), lambda i: (0, 0))
    return pl.pallas_call(
        body, name="out_proj_norm_a", grid=(seq // tile,),
        out_shape=(jax.ShapeDtypeStruct((seq, dm), F32), jax.ShapeDtypeStruct((seq, 1), F32),
                   jax.ShapeDtypeStruct((seq, dm), BF16)),
        in_specs=[row_spec, pl.BlockSpec(w.shape, lambda i: (0, 0), pipeline_mode=pl.Buffered(1)), row_spec, vec_spec,
                  vec_spec],
        out_specs=(row_spec, pl.BlockSpec((tile, 1), lambda i: (i, 0)), row_spec),
        compiler_params=_params("parallel"),
    )(y, w, x, gain, bias)


def _kv_proj(xb, wkv, tables):
    seq, dm = xb.shape
    kvw = wkv.shape[1] // 2
    n_kv = kvw // HEAD_DIM
    tile = _tile(seq, 512)

    def body(x_ref, w_ref, cos_ref, sa_ref, sb_ref, kd_ref, vd_ref, kt_ref, vt_ref):
        kv = jnp.dot(x_ref[...], w_ref[...], preferred_element_type=F32)
        low = lax.broadcasted_iota(jnp.int32, (1, LANES), 1) < HEAD_DIM
        cos, sa, sb = cos_ref[...], sa_ref[...], sb_ref[...]

        def put(pair, h, nat_ref, t_ref):
            swapped = pltpu.roll(pair, HEAD_DIM, 1)
            for head, dup in ((h, jnp.where(low, pair, swapped)), (h + 1, jnp.where(low, swapped, pair))):
                nat_ref[head] = dup.astype(BF16)
                t_ref[head] = dup.T.astype(BF16)

        for j in range(kvw // LANES):
            put(_rope(kv[:, j * LANES:(j + 1) * LANES], cos, sa, sb), 2 * j, kd_ref, kt_ref)
            put(kv[:, kvw + j * LANES:kvw + (j + 1) * LANES], 2 * j, vd_ref, vt_ref)

    tab_spec = pl.BlockSpec((tile, LANES), lambda i: (i, 0))
    dup_spec = pl.BlockSpec((n_kv, tile, LANES), lambda i: (0, i, 0))
    dup_shape = jax.ShapeDtypeStruct((n_kv, seq, LANES), BF16)
    t_spec = pl.BlockSpec((n_kv, LANES, tile), lambda i: (0, 0, i))
    t_shape = jax.ShapeDtypeStruct((n_kv, LANES, seq), BF16)
    return pl.pallas_call(
        body, name="kv_proj", grid=(seq // tile,),
        out_shape=(dup_shape, dup_shape, t_shape, t_shape),
        in_specs=[pl.BlockSpec((tile, dm), lambda i: (i, 0)), pl.BlockSpec(wkv.shape, lambda i: (0, 0)),
                  tab_spec, tab_spec, tab_spec],
        out_specs=(dup_spec, dup_spec, t_spec, t_spec),
        compiler_params=_params("parallel"),
    )(xb, wkv, *tables)


ATTN_STEP_BLOCKS = 16


def _head_queries(q_ref, rows, low):
    parts = []
    for j in range(GQA_GROUP // 2):
        q2 = q_ref[rows, j * LANES:(j + 1) * LANES]
        parts += [jnp.where(low, q2, 0), jnp.where(low, 0, q2)]
    return parts


def _key_window(prev_ref, cur_ref, b, axis):
    def block(i):
        sl = slice(i * ATTN_BLOCK, (i + 1) * ATTN_BLOCK)
        return cur_ref[sl, :] if axis == 0 else cur_ref[:, sl]
    return jnp.concatenate([prev_ref[...] if b == 0 else block(b - 1), block(b)], axis=axis)


def _probs_transposed(n, kh, kcat, q_all, sink_ref):
    st = lax.dot_general(kcat, q_all, NT, preferred_element_type=F32)
    key = lax.broadcasted_iota(jnp.int32, (2 * ATTN_BLOCK, ATTN_BLOCK), 0)
    qry = lax.broadcasted_iota(jnp.int32, (2 * ATTN_BLOCK, ATTN_BLOCK), 1)
    valid = (key > qry) & (key <= qry + ATTN_BLOCK) & ((key >= ATTN_BLOCK) | (n > 0))
    st = st + jnp.tile(jnp.where(valid, 0.0, NEG_INF), (1, GQA_GROUP))
    sink = jnp.concatenate([jnp.full((1, ATTN_BLOCK), sink_ref[0, kh * GQA_GROUP + h], F32)
                            for h in range(GQA_GROUP)], axis=1)
    m = jnp.maximum(jnp.max(st, axis=0, keepdims=True), sink)
    e = jnp.exp(st - m)
    e_sink = jnp.exp(sink - m)
    inv = 1.0 / (jnp.sum(e, axis=0, keepdims=True) + e_sink)
    return e * inv, e_sink * inv


def _attn_specs(n_width, qb):
    rows = qb * ATTN_BLOCK
    before = lambda n: jnp.maximum(n * qb - 1, 0)
    q_spec = pl.BlockSpec((rows, n_width), lambda kh, n: (n, kh))
    cur = pl.BlockSpec((None, rows, LANES), lambda kh, n: (kh, n, 0))
    prev = pl.BlockSpec((None, ATTN_BLOCK, LANES), lambda kh, n: (kh, before(n), 0))
    cur_t = pl.BlockSpec((None, LANES, rows), lambda kh, n: (kh, 0, n))
    prev_t = pl.BlockSpec((None, LANES, ATTN_BLOCK), lambda kh, n: (kh, 0, before(n)))
    return q_spec, cur, prev, cur_t, prev_t


def _pair_product_transposed(mat_t, rhs, j, low_rows):
    a = rhs[:, 2 * j * ATTN_BLOCK:(2 * j + 1) * ATTN_BLOCK]
    b = rhs[:, (2 * j + 1) * ATTN_BLOCK:(2 * j + 2) * ATTN_BLOCK]
    out_t = (jnp.dot(jnp.where(low_rows, mat_t, 0), a, preferred_element_type=F32)
             + jnp.dot(jnp.where(low_rows, 0, mat_t), b, preferred_element_type=F32))
    return out_t.T


def _attn_forward(qs, kd, vt, zb, sinks):
    seq, dm = qs.shape
    n_kv = kd.shape[0]
    gw = GQA_GROUP * HEAD_DIM

    qb = ATTN_STEP_BLOCKS if (seq // ATTN_BLOCK) % ATTN_STEP_BLOCKS == 0 else 1

    def body(q_ref, kp_ref, kc_ref, vtp_ref, vtc_ref, z_ref, sink_ref, att_ref, yb_ref):
        kh, n = pl.program_id(0), pl.program_id(1)
        low = lax.broadcasted_iota(jnp.int32, (1, LANES), 1) < HEAD_DIM
        low_rows = lax.broadcasted_iota(jnp.int32, (LANES, 1), 0) < HEAD_DIM
        for b in range(qb):
            rows = slice(b * ATTN_BLOCK, (b + 1) * ATTN_BLOCK)
            kcat = _key_window(kp_ref, kc_ref, b, 0)
            vt = _key_window(vtp_ref, vtc_ref, b, 1)
            q_all = jnp.concatenate(_head_queries(q_ref, rows, low), axis=0)
            probs_t, _ = _probs_transposed(n * qb + b, kh, kcat, q_all, sink_ref)
            pt = probs_t.astype(BF16)
            for j in range(GQA_GROUP // 2):
                cs = slice(j * LANES, (j + 1) * LANES)
                o2 = _pair_product_transposed(vt, pt, j, low_rows)
                att_ref[rows, cs] = o2
                z = z_ref[rows, cs]
                yb_ref[rows, cs] = (o2 * (z * jax.nn.sigmoid(z))).astype(BF16)

    q_spec, cur, prev, cur_t, prev_t = _attn_specs(gw, qb)
    return pl.pallas_call(
        body, name="attn_fwd", grid=(n_kv, seq // (qb * ATTN_BLOCK)),
        out_shape=(jax.ShapeDtypeStruct((seq, dm), F32), jax.ShapeDtypeStruct((seq, dm), BF16)),
        in_specs=[q_spec, prev, cur, prev_t, cur_t, q_spec, pl.BlockSpec(memory_space=pltpu.SMEM)],
        out_specs=(q_spec, q_spec),
        compiler_params=_params("parallel", "parallel"),
    )(qs, kd, kd, vt, vt, zb, sinks)


def _attn_backward(qs, kd, vd, kt, zb, att, dyb, sinks, tables, after):
    seq, dm = qs.shape
    n_kv = kd.shape[0]
    gw = GQA_GROUP * HEAD_DIM
    n_blocks = seq // ATTN_BLOCK
    qb = ATTN_STEP_BLOCKS if n_blocks % ATTN_STEP_BLOCKS == 0 else 1

    def body(q_ref, kp_ref, kc_ref, vp_ref, vc_ref, ktp_ref, ktc_ref, z_ref, att_ref, dyb_ref, sink_ref,
             cos_ref, sa_ref, sb_ref, after_ref, dh_ref, dk_ref, dv_ref, ds_ref):
        del after_ref
        kh, n = pl.program_id(0), pl.program_id(1)

        @pl.when(n == 0)
        def _():
            dk_ref[...] = jnp.zeros_like(dk_ref)
            dv_ref[...] = jnp.zeros_like(dv_ref)

        @pl.when(jnp.logical_and(n == 0, kh == 0))
        def _():
            ds_ref[...] = jnp.zeros_like(ds_ref)

        low = lax.broadcasted_iota(jnp.int32, (1, LANES), 1) < HEAD_DIM
        low_rows = lax.broadcasted_iota(jnp.int32, (LANES, 1), 0) < HEAD_DIM
        head_lane = lax.broadcasted_iota(jnp.int32, (1, LANES), 1)
        dsink = jnp.zeros((1, LANES), F32)
        for b in range(qb):
            rows = slice(b * ATTN_BLOCK, (b + 1) * ATTN_BLOCK)
            kcat = _key_window(kp_ref, kc_ref, b, 0)
            vcat = _key_window(vp_ref, vc_ref, b, 0)
            kt = _key_window(ktp_ref, ktc_ref, b, 1)
            cos, sa, sb = cos_ref[rows, :], sa_ref[rows, :], sb_ref[rows, :]
            q_all = jnp.concatenate(_head_queries(q_ref, rows, low), axis=0)
            d_parts = []
            for j in range(GQA_GROUP // 2):
                cs = slice(j * LANES, (j + 1) * LANES)
                sz, dsz = _silu_parts(z_ref[rows, cs])
                dy2 = dyb_ref[rows, cs]
                dh_ref[1, rows, cs] = (dy2 * att_ref[rows, cs] * dsz).astype(BF16)
                datt = (dy2 * sz).astype(BF16)
                d_parts += [jnp.where(low, datt, 0), jnp.where(low, 0, datt)]
            d_all = jnp.concatenate(d_parts, axis=0)
            probs_t, sink_p = _probs_transposed(n * qb + b, kh, kcat, q_all, sink_ref)
            dprobs_t = lax.dot_general(vcat, d_all, NT, preferred_element_type=F32)
            row_dot = jnp.sum(probs_t * dprobs_t, axis=0, keepdims=True)
            ds_t = (probs_t * (dprobs_t - row_dot)).astype(BF16)
            dk = jnp.dot(ds_t, q_all, preferred_element_type=F32)
            dv = jnp.dot(probs_t.astype(BF16), d_all, preferred_element_type=F32)
            for j in range(GQA_GROUP // 2):
                dq2 = _pair_product_transposed(kt, ds_t, j, low_rows)
                dh_ref[0, rows, j * LANES:(j + 1) * LANES] = (
                    _rope_transposed(dq2, cos, sa, sb) * 0.125).astype(BF16)
            sink_dot = sink_p * row_dot
            for h in range(GQA_GROUP):
                part = jnp.sum(sink_dot[:, h * ATTN_BLOCK:(h + 1) * ATTN_BLOCK], axis=1, keepdims=True)
                dsink = dsink - jnp.where(head_lane == kh * GQA_GROUP + h, part, 0.0)

            def add_window(dk=dk, dv=dv, b=b):
                start = pl.multiple_of((n * qb + b - 1) * ATTN_BLOCK, ATTN_BLOCK)
                dk_ref[pl.ds(start, 2 * ATTN_BLOCK), :] += dk
                dv_ref[pl.ds(start, 2 * ATTN_BLOCK), :] += dv

            if b > 0:
                add_window()
            else:
                pl.when(n > 0)(add_window)

                @pl.when(n == 0)
                def _(dk=dk, dv=dv):
                    dk_ref[pl.ds(0, ATTN_BLOCK), :] += dk[ATTN_BLOCK:, :]
                    dv_ref[pl.ds(0, ATTN_BLOCK), :] += dv[ATTN_BLOCK:, :]
        ds_ref[0:1, :] += dsink

    q_spec, cur, prev, cur_t, prev_t = _attn_specs(gw, qb)
    tab_spec = pl.BlockSpec((qb * ATTN_BLOCK, LANES), lambda kh, n: (n, 0))
    acc_spec = pl.BlockSpec((None, seq, LANES), lambda kh, n: (kh, 0, 0))
    acc_shape = jax.ShapeDtypeStruct((n_kv, seq, LANES), F32)
    return pl.pallas_call(
        body, name="attn_bwd", grid=(n_kv, n_blocks // qb),
        out_shape=(jax.ShapeDtypeStruct((2, seq, dm), BF16), acc_shape, acc_shape,
                   jax.ShapeDtypeStruct((STAT_ROWS, LANES), F32)),
        in_specs=[q_spec, prev, cur, prev, cur, prev_t, cur_t, q_spec, q_spec, q_spec,
                  pl.BlockSpec(memory_space=pltpu.SMEM), tab_spec, tab_spec, tab_spec, ANY],
        out_specs=(pl.BlockSpec((2, qb * ATTN_BLOCK, gw), lambda kh, n: (0, n, kh)), acc_spec, acc_spec,
                   pl.BlockSpec((STAT_ROWS, LANES), lambda kh, n: (0, 0))),
        compiler_params=_params("arbitrary", "arbitrary"),
    )(qs, kd, kd, vd, vd, kt, kt, zb, att, dyb, sinks, *tables, after)


def _kv_grad_fold(dk, dv, tables):
    n_kv, seq, _ = dk.shape
    kvw = n_kv * HEAD_DIM
    tile = _tile(seq, 512)

    def body(dk_ref, dv_ref, cos_ref, sa_ref, sb_ref, o_ref):
        low = lax.broadcasted_iota(jnp.int32, (1, LANES), 1) < HEAD_DIM
        cos, sa, sb = cos_ref[...], sa_ref[...], sb_ref[...]

        def folded(ref, h):
            t = ref[h]
            return t + pltpu.roll(t, HEAD_DIM, 1)

        for j in range(n_kv // 2):
            ka = _rope_transposed(folded(dk_ref, 2 * j), cos, sa, sb)
            kb = _rope_transposed(folded(dk_ref, 2 * j + 1), cos, sa, sb)
            o_ref[:, j * LANES:(j + 1) * LANES] = jnp.where(low, ka, kb).astype(BF16)
            o_ref[:, kvw + j * LANES:kvw + (j + 1) * LANES] = jnp.where(
                low, folded(dv_ref, 2 * j), folded(dv_ref, 2 * j + 1)).astype(BF16)

    tab_spec = pl.BlockSpec((tile, LANES), lambda i: (i, 0))
    in_spec = pl.BlockSpec((n_kv, tile, LANES), lambda i: (0, i, 0))
    return pl.pallas_call(
        body, name="kv_grad_fold", grid=(seq // tile,),
        out_shape=jax.ShapeDtypeStruct((seq, 2 * kvw), BF16),
        in_specs=[in_spec, in_spec, tab_spec, tab_spec, tab_spec],
        out_specs=pl.BlockSpec((tile, 2 * kvw), lambda i: (i, 0)),
        compiler_params=_params("parallel"),
    )(dk, dv, *tables)


def _out_proj_norm_loss(yb, w, xhat1, gain0, bias0, gain1, bias1, target):
    seq, dm = xhat1.shape
    tile = _tile(seq, 512)

    def body(y_ref, w_ref, xh1_ref, g0_ref, b0_ref, g1_ref, b1_ref, t_ref, dr_ref, drb_ref, st_ref):
        i = pl.program_id(0)

        @pl.when(i == 0)
        def _():
            st_ref[...] = jnp.zeros_like(st_ref)

        for rows in _row_parts(tile):
            ob = jnp.dot(y_ref[rows, :], w_ref[...], preferred_element_type=F32)
            x1 = xh1_ref[rows, :] * g0_ref[...] + b0_ref[...]
            xhat, rstd = _layer_norm_stats(ALPHA * x1 + ob)
            err = xhat * g1_ref[...] + b1_ref[...] - t_ref[rows, :]
            dout = err * (1.0 / dm)
            dr = _layer_norm_backward(dout, xhat, rstd, g1_ref[...])
            dr_ref[rows, :] = dr
            drb_ref[rows, :] = dr.astype(BF16)
            st_ref[0:1, :] += _col_sum(dout * xhat)
            st_ref[1:2, :] += _col_sum(dout)
            st_ref[2:3, :] += _col_sum(err * err)

    row_spec = pl.BlockSpec((tile, dm), lambda i: (i, 0))
    vec_spec = pl.BlockSpec((1, dm), lambda i: (0, 0))
    return pl.pallas_call(
        body, name="out_proj_norm_loss_b", grid=(seq // tile,),
        out_shape=(jax.ShapeDtypeStruct((seq, dm), F32), jax.ShapeDtypeStruct((seq, dm), BF16),
                   jax.ShapeDtypeStruct((STAT_ROWS, dm), F32)),
        in_specs=[row_spec, pl.BlockSpec(w.shape, lambda i: (0, 0), pipeline_mode=pl.Buffered(1)), row_spec, vec_spec,
                  vec_spec, vec_spec,
                  vec_spec, row_spec],
        out_specs=(row_spec, row_spec, pl.BlockSpec((STAT_ROWS, dm), lambda i: (0, 0))),
        compiler_params=_params("arbitrary"),
    )(yb, w, xhat1, gain0, bias0, gain1, bias1, target)


def _stream_grad_norm_backward(dhq, wqg, dkv, wkv, dr2, xhat1, rstd1, gain0, after):
    seq, dm = dr2.shape
    tile = _tile(seq, 256)

    def body(dh_ref, wqg_ref, dkv_ref, wkv_ref, dr2_ref, xh_ref, rstd_ref, g_ref, after_ref, dr_ref, drb_ref, st_ref):
        del after_ref

        @pl.when(pl.program_id(0) == 0)
        def _():
            st_ref[...] = jnp.zeros_like(st_ref)

        dx1 = (lax.dot_general(dh_ref[0], wqg_ref[:, :dm], NT, preferred_element_type=F32)
               + lax.dot_general(dh_ref[1], wqg_ref[:, dm:], NT, preferred_element_type=F32)
               + lax.dot_general(dkv_ref[...], wkv_ref[...], NT, preferred_element_type=F32)
               + ALPHA * dr2_ref[...])
        xhat = xh_ref[...]
        dr = _layer_norm_backward(dx1, xhat, rstd_ref[...], g_ref[...])
        dr_ref[...] = dr
        drb_ref[...] = dr.astype(BF16)
        st_ref[0:1, :] += _col_sum(dx1 * xhat)
        st_ref[1:2, :] += _col_sum(dx1)

    row_spec = pl.BlockSpec((tile, dm), lambda i: (i, 0))
    resident = pl.Buffered(1)
    return pl.pallas_call(
        body, name="stream_grad_norm_bwd", grid=(seq // tile,),
        out_shape=(jax.ShapeDtypeStruct((seq, dm), F32), jax.ShapeDtypeStruct((seq, dm), BF16),
                   jax.ShapeDtypeStruct((STAT_ROWS, dm), F32)),
        in_specs=[pl.BlockSpec((2, tile, dm), lambda i: (0, i, 0)),
                  pl.BlockSpec(wqg.shape, lambda i: (0, 0), pipeline_mode=resident),
                  pl.BlockSpec((tile, dkv.shape[1]), lambda i: (i, 0)),
                  pl.BlockSpec(wkv.shape, lambda i: (0, 0), pipeline_mode=resident),
                  row_spec, row_spec, pl.BlockSpec((tile, 1), lambda i: (i, 0)),
                  pl.BlockSpec((1, dm), lambda i: (0, 0)), ANY],
        out_specs=(row_spec, row_spec, pl.BlockSpec((STAT_ROWS, dm), lambda i: (0, 0))),
        compiler_params=_params("arbitrary"),
    )(dhq, wqg, dkv, wkv, dr2, xhat1, rstd1, gain0, after)


def _adamw_math(w, g, m, v):
    m = ADAM_B1 * m + (1.0 - ADAM_B1) * g
    v = ADAM_B2 * v + (1.0 - ADAM_B2) * (g * g)
    m_hat = m / (1.0 - ADAM_B1 ** ADAM_STEP)
    v_hat = v / (1.0 - ADAM_B2 ** ADAM_STEP)
    delta = -ADAM_LR * (m_hat / (jnp.sqrt(v_hat) + ADAM_EPS) + ADAM_WD * w)
    return delta, m, v


def _sum_devices(ref):
    total = ref[0].astype(F32)
    for d in range(1, ref.shape[0]):
        total = total + ref[d].astype(F32)
    return total


def _adamw_shard(name, parts, w, m, v, after):
    rows, cols = w.shape
    n_parts = len(parts)
    part_rows = rows // n_parts
    tr = _tile(part_rows, max(8, (1 << 18) // cols)) if part_rows >= 8 else part_rows
    per_part = part_rows // tr

    def body(*refs):
        p_refs = refs[:n_parts]
        w_ref, m_ref, v_ref, _, g_out, d_out, m_out, v_out = refs[n_parts:]
        g = _sum_devices(p_refs[0])
        for k in range(1, n_parts):
            g = jnp.where(pl.program_id(0) >= k * per_part, _sum_devices(p_refs[k]), g)
        delta, m_new, v_new = _adamw_math(w_ref[...], g, m_ref[...], v_ref[...])
        g_out[...] = g
        d_out[...] = delta
        m_out[...] = m_new
        v_out[...] = v_new

    def part_spec(k):
        return pl.BlockSpec((parts[k].shape[0], tr, cols),
                            lambda i: (0, jnp.clip(i - k * per_part, 0, per_part - 1), 0))

    spec = pl.BlockSpec((tr, cols), lambda i: (i, 0))
    shape = jax.ShapeDtypeStruct((rows, cols), F32)
    return pl.pallas_call(
        body, name=name, grid=(rows // tr,),
        out_shape=(shape, shape, shape, shape),
        in_specs=[part_spec(k) for k in range(n_parts)] + [spec, spec, spec, ANY],
        out_specs=(spec, spec, spec, spec),
        compiler_params=_params("arbitrary"),
    )(*parts, w, m, v, after)


def _adamw_replicated(stats_b, stats_a, sink_parts, ln_g, ln_b, sinks, m_ln_g, m_ln_b, m_sinks, v_ln_g, v_ln_b,
                      v_sinks, after):
    n_q = sinks.shape[1]
    dm = ln_g.shape[1]

    def body(sb_ref, sa_ref, sk_ref, g_ref, b_ref, s_ref, mg_ref, mb_ref, ms_ref, vg_ref, vb_ref, vs_ref, after_ref,
             *outs):
        del after_ref
        layer_sums = (_sum_devices(sa_ref), _sum_devices(sb_ref))
        outs[12][...] = jnp.sum(layer_sums[1][2:3, :], axis=1, keepdims=True) * (0.5 / dm)
        for which, (w_ref, m_ref, v_ref) in enumerate(((g_ref, mg_ref, vg_ref), (b_ref, mb_ref, vb_ref))):
            for layer in range(DEPTH):
                row = slice(layer, layer + 1)
                g = layer_sums[layer][which:which + 1, :]
                res = (g,) + _adamw_math(w_ref[row, :], g, m_ref[row, :], v_ref[row, :])
                for o_ref, val in zip(outs[4 * which:4 * which + 4], res):
                    o_ref[row, :] = val
        g = _sum_devices(sk_ref)[0:1, 0:n_q]
        res = (g,) + _adamw_math(s_ref[...], g, ms_ref[...], vs_ref[...])
        for o_ref, val in zip(outs[8:12], res):
            o_ref[...] = val

    vmem = pl.BlockSpec(memory_space=pltpu.VMEM)
    shapes = [jax.ShapeDtypeStruct(a.shape, F32) for a in (ln_g, ln_b, sinks) for _ in range(4)]
    shapes.append(jax.ShapeDtypeStruct((1, 1), F32))
    return pl.pallas_call(
        body, name="adamw_replicated", out_shape=tuple(shapes),
        in_specs=[vmem] * 12 + [ANY], out_specs=tuple([vmem] * 13),
    )(stats_b, stats_a, sink_parts, ln_g, ln_b, sinks, m_ln_g, m_ln_b, m_sinks, v_ln_g, v_ln_b, v_sinks, after)


def kernel(x, ln_g, ln_b, a_w_in, a_w_group, a_scale, a_w_out, b_w_k, b_w_v, b_w_qg, b_sinks, b_w_out, loss_target, m_ln_g, m_ln_b, m_a_w_in, m_a_w_group, m_a_scale, m_a_w_out, m_b_w_k, m_b_w_v, m_b_w_qg, m_b_sinks, m_b_w_out, v_ln_g, v_ln_b, v_a_w_in, v_a_w_group, v_a_scale, v_a_w_out, v_b_w_k, v_b_w_v, v_b_w_qg, v_b_sinks, v_b_w_out):
    _, seq, dm = x.shape
    n_groups = len(POOL_WINDOWS)
    gd = dm // n_groups
    kvw = b_w_k.shape[1]
    cb = 2 * dm // N_DEV
    rb = dm // N_DEV
    gb = gd // N_DEV

    x2 = x.reshape(seq, dm)
    target = loss_target.reshape(seq, dm)
    w_in_s = a_w_in.reshape(dm, cb)
    w_g_s = a_w_group.reshape(n_groups, gb, gd)
    w_out_s = a_w_out.reshape(rb, dm)
    w_qg_s = b_w_qg.reshape(dm, cb)
    w_outb_s = b_w_out.reshape(rb, dm)

    def cols(ref, dev):
        return ref.at[:, pl.ds(pl.multiple_of(dev * cb, LANES), cb)]

    def rows(ref, dev):
        return ref.at[pl.ds(pl.multiple_of(dev * rb, 8), rb), :]

    def group_rows(ref, dev):
        return ref.at[:, pl.ds(pl.multiple_of(dev * gb, 8), gb), :]

    def k_rows(ref, dev):
        return ref.at[pl.ds(pl.multiple_of(dev * rb, 8), rb), pl.ds(0, kvw)]

    def v_rows(ref, dev):
        return ref.at[pl.ds(pl.multiple_of(dev * rb, 8), rb), pl.ds(kvw, kvw)]

    def scale_cols(ref, dev):
        return ref.at[:, pl.ds(pl.multiple_of(dev * rb, LANES), rb)]

    bf = lambda a: a.astype(BF16)
    wide, square = jax.ShapeDtypeStruct((dm, 2 * dm), BF16), jax.ShapeDtypeStruct((dm, dm), BF16)
    w_g, scale, w_in = _gather_weights(
        "gather_a_in", 0, [(bf(w_g_s), 0, group_rows), (a_scale, 1, scale_cols), (bf(w_in_s), 2, cols)],
        [jax.ShapeDtypeStruct((n_groups, gd, gd), BF16), jax.ShapeDtypeStruct((1, dm), F32), wide])
    (w_out,) = _gather_weights("gather_a_out", 1, [(bf(w_out_s), 0, rows)], [square])
    w_kv, w_qg = _gather_weights(
        "gather_b_in", 2, [(bf(b_w_k), 0, k_rows), (bf(b_w_v), 0, v_rows), (bf(w_qg_s), 1, cols)],
        [jax.ShapeDtypeStruct((dm, 2 * kvw), BF16), wide])
    (w_outb,) = _gather_weights("gather_b_out", 3, [(bf(w_outb_s), 0, rows)], [square])

    tables = _rope_tables(seq)
    bm = _tile(seq, 1024)
    bn = _tile(dm, 1024)
    g0, g1, b0, b1 = ln_g[0:1], ln_g[1:2], ln_b[0:1], ln_b[1:2]

    xb = _cast_bf16("cast_x", x2)
    y, pooled, mixed, z_a = _pool_forward(xb, w_in, w_g, scale)
    xhat1, rstd1, x1b = _out_proj_norm(y, w_out, x2, g0, b0)

    kd, vd, kt, vt = _kv_proj(x1b, w_kv, tables)
    bmq = bm
    tab_spec = pl.BlockSpec((bmq, LANES), lambda i, j: (i, 0))

    def rope_scale(val, cos_ref, sa_ref, sb_ref):
        cos, sa, sb = cos_ref[...], sa_ref[...], sb_ref[...]
        return jnp.concatenate([_rope(val[:, j * LANES:(j + 1) * LANES], cos, sa, sb) * 0.125
                                for j in range(val.shape[1] // LANES)], axis=1)

    qs = _mm("b_q_proj", x1b, w_qg, dims=NN, grid=(seq // bmq, dm // bn),
             a_spec=pl.BlockSpec((bmq, dm), lambda i, j: (i, 0)), b_spec=pl.BlockSpec((dm, bn), lambda i, j: (0, j)),
             out_shape=jax.ShapeDtypeStruct((seq, dm), BF16), out_spec=pl.BlockSpec((bmq, bn), lambda i, j: (i, j)),
             epilogue=rope_scale, extras=tables, extra_specs=(tab_spec,) * 3)
    zb = _mm("b_gate_proj", x1b, w_qg, dims=NN, grid=(seq // bm, dm // bn),
             a_spec=pl.BlockSpec((bm, dm), lambda i, j: (i, 0)),
             b_spec=pl.BlockSpec((dm, bn), lambda i, j: (0, j + dm // bn)),
             out_shape=jax.ShapeDtypeStruct((seq, dm), F32), out_spec=pl.BlockSpec((bm, bn), lambda i, j: (i, j)))
    att, yb = _attn_forward(qs, kd, vt, zb, b_sinks)
    dr2, dr2b, stats_b = _out_proj_norm_loss(yb, w_outb, xhat1, g0, b0, g1, b1, target)

    def weight_grad(name, a, b, n_cols, b_spec=None, part=(0, 1), after=None):
        m_cols = a.shape[1] // part[1]
        tm, tn = _tile(m_cols, 1024), _tile(n_cols, 512)
        first = part[0] * (m_cols // tm)
        return _mm(name, a, b, dims=TN, grid=(m_cols // tm, n_cols // tn),
                   a_spec=pl.BlockSpec((seq, tm), lambda i, j: (0, first + i)),
                   b_spec=b_spec(tn) if b_spec else pl.BlockSpec((seq, tn), lambda i, j: (0, j)),
                   out_shape=jax.ShapeDtypeStruct((m_cols, n_cols), BF16),
                   out_spec=pl.BlockSpec((tm, tn), lambda i, j: (i, j)),
                   extras=() if after is None else (after,), extra_specs=() if after is None else (ANY,))

    def halves_spec(tn):
        per = dm // tn
        return pl.BlockSpec((None, seq, tn), lambda i, j: (j // per, 0, j % per))

    def times_transposed(name, a, w):
        return _mm(name, a, w, dims=NT, grid=(seq // bm, dm // bn),
                   a_spec=pl.BlockSpec((bm, a.shape[1]), lambda i, j: (i, 0)),
                   b_spec=pl.BlockSpec((bn, w.shape[1]), lambda i, j: (j, 0)),
                   out_shape=jax.ShapeDtypeStruct((seq, dm), F32), out_spec=pl.BlockSpec((bm, bn), lambda i, j: (i, j)))

    def stat_row_cols(ref, dev):
        return ref.at[pl.ds(0, 1), pl.ds(pl.multiple_of(dev * rb, LANES), rb)]

    upd = {}
    last = [dr2b]
    my_core = lax.axis_index("c").astype(jnp.int32).reshape(1)

    def then(value):
        last[0] = value[0] if isinstance(value, (list, tuple)) else value
        return value

    def shard_update(key, parts, w, m, v):
        shape = w.shape
        flat = lambda a: a.reshape(-1, shape[-1])
        parts = list(parts) if isinstance(parts, (list, tuple)) else [parts]
        outs = then(_adamw_shard("adamw_" + key, [p.reshape(p.shape[0], -1, shape[-1]) for p in parts], flat(w),
                                 flat(m), flat(v), last[0]))
        upd[key] = [o.reshape(shape) for o in outs]

    def two_level_scatter(name, ids, streams):
        staged = _sibling_exchange(name + "_pair", streams, ids[0])

        def finish():
            sums = [then(_pair_sum(f"{name}_sum{s}", st[0], got, my_core, last[0]))
                    for s, (st, got) in enumerate(zip(streams, staged))]
            return _chip_exchange(name + "_chip", sums, ids[1])
        return finish

    d_w_outb = then(weight_grad("b_out_proj_dw", yb, dr2b, dm))
    (p_outb,) = _exchange_blocks("scatter_b_out", [(d_w_outb, rows, (rb, dm))], 4)
    dyb = times_transposed("b_out_proj_dx", dr2b, w_outb)
    dhq, dkd, dvd, dsink = then(_attn_backward(qs, kd, vd, kt, zb, att, dyb, b_sinks, tables, after=last[0]))
    dkv = _kv_grad_fold(dkd, dvd, tables)
    d_w_kv = weight_grad("b_kv_proj_dw", x1b, dkv, 2 * kvw)
    d_w_qg = then(weight_grad("b_qg_proj_dw", x1b, dhq, 2 * dm, halves_spec, after=d_w_kv))
    finish_b_in = two_level_scatter("scatter_b_in", (5, 11), [(d_w_qg, cols, (dm, cb))])
    dr1, dr1b, stats_a = _stream_grad_norm_backward(dhq, w_qg, dkv, w_kv, dr2, xhat1, rstd1, g0, after=last[0])
    last[0] = dr1b
    shard_update("b_w_out", p_outb, b_w_out, m_b_w_out, v_b_w_out)
    (p_qg,) = finish_b_in()
    all_b, all_a, all_sink = _exchange_blocks("gather_replicated_grads", [
        (stats_b, None, stats_b.shape), (stats_a, None, stats_a.shape), (dsink, None, dsink.shape)], 9)

    d_w_out = then(weight_grad("a_out_proj_dw", y, dr1b, dm, after=last[0]))
    p_out, p_k, p_v = _exchange_blocks("scatter_a_out", [
        (d_w_out, rows, (rb, dm)), (d_w_kv, k_rows, (rb, kvw)), (d_w_kv, v_rows, (rb, kvw))], 6)
    dy = times_transposed("a_out_proj_dx", dr1b, w_out)
    dh, d_w_g, stats_s = then(_pool_mid_backward(dy, mixed, z_a, pooled, w_g, scale, after=last[0]))
    p_g, p_scale = _exchange_blocks("scatter_a_mid", [
        (d_w_g, group_rows, (n_groups, gb, gd)), (stats_s, stat_row_cols, (1, rb))], 7)
    shard_update("b_w_qg", p_qg, b_w_qg, m_b_w_qg, v_b_w_qg)
    rep = then(_adamw_replicated(all_b, all_a, all_sink, ln_g, ln_b, b_sinks, m_ln_g, m_ln_b, m_b_sinks, v_ln_g,
                                 v_ln_b, v_b_sinks, last[0]))
    upd["ln_g"], upd["ln_b"], upd["b_sinks"] = list(rep[0:4]), list(rep[4:8]), list(rep[8:12])
    finish_a_in = []
    for k in range(2):
        d_w_in = then(weight_grad(f"a_in_proj_dw_{k}", xb, dh, 2 * dm, halves_spec, part=(k, 2), after=last[0]))
        finish_a_in.append(two_level_scatter(f"scatter_a_in_{k}", (8 + 2 * k, 12 + k), [(d_w_in, cols, (dm // 2, cb))]))
    shard_update("a_w_out", p_out, a_w_out, m_a_w_out, v_a_w_out)
    shard_update("b_w_k", p_k, b_w_k, m_b_w_k, v_b_w_k)
    shard_update("b_w_v", p_v, b_w_v, m_b_w_v, v_b_w_v)
    shard_update("a_w_group", p_g, a_w_group, m_a_w_group, v_a_w_group)
    shard_update("a_scale", p_scale, a_scale, m_a_scale, v_a_scale)
    p_in = list(finish_a_in[0]()) + list(finish_a_in[1]())
    grad_x = then(_input_grad(dh, w_in, dr1, last[0]))
    shard_update("a_w_in", p_in, a_w_in, m_a_w_in, v_a_w_in)

    loss = rep[12].reshape(())
    order = ["ln_g", "ln_b", "a_w_in", "a_w_group", "a_scale", "a_w_out", "b_w_k", "b_w_v", "b_w_qg", "b_sinks",
             "b_w_out"]
    return (loss, grad_x.reshape(x.shape), *[upd[n][0] for n in order], *[upd[n][1] for n in order],
            *[upd[n][2] for n in order], *[upd[n][3] for n in order])
```

```python
import functools

import jax
import jax.numpy as jnp
from jax import lax
from jax.experimental import pallas as pl
from jax.experimental.pallas import tpu as pltpu
from jax.experimental.pallas import tpu_sc as plsc

F32 = jnp.float32
BF16 = jnp.bfloat16
MESH = pl.DeviceIdType.MESH
AXES = ("x", "y", "c")
N_DEV = 8

POOL_WINDOWS = (2, 4, 8, 16)
POOL_HALO = 16
HEAD_DIM = 64
GQA_GROUP = 8
ATTN_BLOCK = 128
ROPE_THETA = 10000.0
LN_EPS = 1e-5
NEG_INF = -1e30
DEPTH = 2
ALPHA = (2 * DEPTH) ** 0.25
ADAM_LR = 0.001
ADAM_B1 = 0.9
ADAM_B2 = 0.999
ADAM_EPS = 1e-08
ADAM_WD = 0.01
ADAM_STEP = 10

LANES = 128
STAT_ROWS = 8


def _tile(n, want):
    t = min(n, want)
    while n % t:
        t //= 2
    return t


def _params(*sem):
    return pltpu.CompilerParams(dimension_semantics=sem)


ANY = pl.BlockSpec(memory_space=pl.ANY)


def _my_pos():
    return lax.axis_index("x"), lax.axis_index("y"), lax.axis_index("c")


def _dev_index(p):
    return 4 * p[0] + 2 * p[1] + p[2]


def _handshake(peers):
    barrier = pltpu.get_barrier_semaphore()
    for peer in peers:
        pl.semaphore_signal(barrier, inc=1, device_id=peer, device_id_type=MESH)
    pl.semaphore_wait(barrier, len(peers))


def _launch_on_sequencer(name, collective_id, body, operands, out_shapes, scratch):
    return pl.kernel(
        body, out_type=tuple(out_shapes), name=name,
        mesh=plsc.ScalarSubcoreMesh(axis_name="sequencer", num_cores=1), scratch_types=scratch,
        compiler_params=pltpu.CompilerParams(collective_id=collective_id),
    )(*operands)


def _gather_weights(name, collective_id, streams, out_shapes):
    n_s = len(streams)
    n_out = len(out_shapes)

    def body(*refs):
        srcs = refs[:n_s]
        outs = refs[n_s:n_s + n_out]
        send_sems, recv_sems, local_sems = refs[n_s + n_out:]
        x, y, c = _my_pos()
        me, sibling = (x, y, c), (x, y, 1 - c)
        x_nbr, y_nbr, diag = (1 - x, y), (x, 1 - y), (1 - x, 1 - y)
        _handshake([sibling, (*x_nbr, c), (*y_nbr, c)])
        south = c == 0
        relay_from = (jnp.where(south, 1 - x, x), jnp.where(south, y, 1 - y))
        relay_to = (jnp.where(south, x, 1 - x), jnp.where(south, 1 - y, y))
        early, late = jnp.where(south, 1, 2), jnp.where(south, 2, 1)

        def copy(s, k, block, to, from_shard=False):
            out_ref = outs[streams[s][1]]
            win = streams[s][2](out_ref, _dev_index(block))
            return pltpu.make_async_remote_copy(
                src_ref=srcs[s] if from_shard else win, dst_ref=win,
                send_sem=send_sems.at[7 * s + k], recv_sem=recv_sems.at[7 * s + k],
                device_id=to, device_id_type=MESH)

        mine = [pltpu.make_async_copy(srcs[s], streams[s][2](outs[streams[s][1]], _dev_index(me)), local_sems.at[s])
                for s in range(n_s)]
        for cp in mine:
            cp.start()
        sent = []
        for s in range(n_s):
            sent += [copy(s, 0, me, sibling, True), copy(s, 1, me, (*x_nbr, c), True), copy(s, 2, me, (*y_nbr, c), True)]
        for cp in sent:
            cp.start()
        for s in range(n_s):
            copy(s, early, (*relay_from, c), me).wait_recv()
            sent += [copy(s, 3, (*relay_from, c), (*relay_to, c)), copy(s, 3 + early, (*relay_from, c), sibling)]
            for cp in sent[-2:]:
                cp.start()
        for s in range(n_s):
            copy(s, late, (*relay_to, c), me).wait_recv()
            sent.append(copy(s, 3 + late, (*relay_to, c), sibling))
            sent[-1].start()
        for s in range(n_s):
            copy(s, 3, (*diag, c), me).wait_recv()
            sent.append(copy(s, 6, (*diag, c), sibling))
            sent[-1].start()
        for s in range(n_s):
            copy(s, 0, sibling, me).wait_recv()
            for k, chip in ((4, x_nbr), (5, y_nbr), (6, diag)):
                copy(s, k, (*chip, 1 - c), me).wait_recv()
        for cp in sent:
            cp.wait_send()
        for cp in mine:
            cp.wait()

    scratch = [pltpu.SemaphoreType.DMA((7 * n_s,)), pltpu.SemaphoreType.DMA((7 * n_s,)),
               pltpu.SemaphoreType.DMA((n_s,))]
    return _launch_on_sequencer(name, collective_id, body, [s[0] for s in streams], out_shapes, scratch)


def _exchange_blocks(name, streams, collective_id):
    n_s = len(streams)

    def body(*refs):
        srcs = refs[:n_s]
        outs = refs[n_s:2 * n_s]
        send_sems, recv_sems, local_sems = refs[2 * n_s:]
        x, y, c = _my_pos()
        me = _dev_index((x, y, c))
        _handshake([(1 - x if k & 4 else x, 1 - y if k & 2 else y, 1 - c if k & 1 else c) for k in range(1, N_DEV)])

        def window(s, dev):
            return srcs[s] if streams[s][1] is None else streams[s][1](srcs[s], dev)

        mine = [pltpu.make_async_copy(window(s, me), outs[s].at[me], local_sems.at[s]) for s in range(n_s)]
        for cp in mine:
            cp.start()
        copies = []
        for k in (2, 4, 6, 3, 5, 7, 1):
            peer = (1 - x if k & 4 else x, 1 - y if k & 2 else y, 1 - c if k & 1 else c)
            for s in range(n_s):
                copies.append(pltpu.make_async_remote_copy(
                    src_ref=window(s, _dev_index(peer)), dst_ref=outs[s].at[me],
                    send_sem=send_sems.at[7 * s + k - 1], recv_sem=recv_sems.at[7 * s + k - 1],
                    device_id=peer, device_id_type=MESH))
        for cp in copies:
            cp.start()
        for cp in copies:
            cp.wait()
        for cp in mine:
            cp.wait()

    out_shapes = [jax.ShapeDtypeStruct((N_DEV,) + tuple(s[2]), s[0].dtype) for s in streams]
    scratch = [pltpu.SemaphoreType.DMA((7 * n_s,)), pltpu.SemaphoreType.DMA((7 * n_s,)),
               pltpu.SemaphoreType.DMA((n_s,))]
    return _launch_on_sequencer(name, collective_id, body, [s[0] for s in streams], out_shapes, scratch)


N_CHIPS = 4


def _sibling_exchange(name, streams, collective_id):
    n_s = len(streams)

    def body(*refs):
        srcs = refs[:n_s]
        outs = refs[n_s:2 * n_s]
        send_sems, recv_sems = refs[2 * n_s:]
        x, y, c = _my_pos()
        sibling = (x, y, 1 - c)
        _handshake([sibling])
        copies = [pltpu.make_async_remote_copy(
            src_ref=streams[s][1](srcs[s], 2 * chip + (1 - c)), dst_ref=outs[s].at[chip],
            send_sem=send_sems.at[N_CHIPS * s + chip], recv_sem=recv_sems.at[N_CHIPS * s + chip],
            device_id=sibling, device_id_type=MESH) for s in range(n_s) for chip in range(N_CHIPS)]
        for cp in copies:
            cp.start()
        for cp in copies:
            cp.wait()

    out_shapes = [jax.ShapeDtypeStruct((N_CHIPS,) + tuple(s[2]), s[0].dtype) for s in streams]
    scratch = [pltpu.SemaphoreType.DMA((N_CHIPS * n_s,)), pltpu.SemaphoreType.DMA((N_CHIPS * n_s,))]
    return _launch_on_sequencer(name, collective_id, body, [s[0] for s in streams], out_shapes, scratch)


def _pair_sum(name, array, from_sibling, my_core, after):
    _, rows, cols = from_sibling.shape
    tr = _tile(rows, 2048)

    def body(core_ref, own_ref, sib_ref, after_ref, o_ref):
        del core_ref, after_ref
        o_ref[...] = (own_ref[...].astype(F32) + sib_ref[...].astype(F32)).astype(o_ref.dtype)

    staged_spec = pl.BlockSpec((None, tr, cols), lambda k, i, core: (k, i, 0))
    return pl.pallas_call(
        body, name=name, out_shape=jax.ShapeDtypeStruct(from_sibling.shape, array.dtype),
        grid_spec=pltpu.PrefetchScalarGridSpec(
            num_scalar_prefetch=1, grid=(N_CHIPS, rows // tr),
            in_specs=[pl.BlockSpec((tr, cols), lambda k, i, core: (i, 2 * k + core[0])), staged_spec, ANY],
            out_specs=staged_spec),
        compiler_params=_params("parallel", "parallel"),
    )(my_core, array, from_sibling, after)


def _chip_exchange(name, pair_sums, collective_id):
    n_s = len(pair_sums)

    def body(*refs):
        srcs = refs[:n_s]
        outs = refs[n_s:2 * n_s]
        send_sems, recv_sems, local_sems = refs[2 * n_s:]
        x, y, c = _my_pos()
        my_chip = 2 * x + y
        chips = [(1 - x, y), (x, 1 - y), (1 - x, 1 - y)]
        _handshake([(*chip, c) for chip in chips])
        mine = [pltpu.make_async_copy(srcs[s].at[my_chip], outs[s].at[my_chip], local_sems.at[s]) for s in range(n_s)]
        copies = [pltpu.make_async_remote_copy(
            src_ref=srcs[s].at[2 * chip[0] + chip[1]], dst_ref=outs[s].at[my_chip],
            send_sem=send_sems.at[3 * s + j], recv_sem=recv_sems.at[3 * s + j],
            device_id=(*chip, c), device_id_type=MESH) for s in range(n_s) for j, chip in enumerate(chips)]
        for cp in mine + copies:
            cp.start()
        for cp in copies:
            cp.wait()
        for cp in mine:
            cp.wait()

    out_shapes = [jax.ShapeDtypeStruct(p.shape, p.dtype) for p in pair_sums]
    scratch = [pltpu.SemaphoreType.DMA((3 * n_s,)), pltpu.SemaphoreType.DMA((3 * n_s,)),
               pltpu.SemaphoreType.DMA((n_s,))]
    return _launch_on_sequencer(name, collective_id, body, list(pair_sums), out_shapes, scratch)


NN = (((1,), (0,)), ((), ()))
NT = (((1,), (1,)), ((), ()))
TN = (((0,), (0,)), ((), ()))


def _mm(name, a, b, *, dims, grid, a_spec, b_spec, out_shape, out_spec, nk=1,
        add=None, add_spec=None, add_scale=1.0, epilogue=None, extras=(), extra_specs=()):
    n_extra = len(extras)
    has_add = add is not None

    def body(*refs):
        a_ref, b_ref = refs[:2]
        pos = 2
        add_ref = None
        if has_add:
            add_ref = refs[pos]
            pos += 1
        extra_refs = refs[pos:pos + n_extra]
        o_ref = refs[pos + n_extra]
        acc_ref = refs[pos + n_extra + 1] if nk > 1 else None

        def finish(val):
            if has_add:
                val = val + add_scale * add_ref[...]
            if epilogue is not None:
                val = epilogue(val, *extra_refs)
            o_ref[...] = val.astype(o_ref.dtype)

        part = lax.dot_general(a_ref[...].astype(BF16), b_ref[...].astype(BF16), dims,
                               preferred_element_type=F32)
        if nk == 1:
            finish(part)
        else:
            k = pl.program_id(2)

            @pl.when(k == 0)
            def _():
                acc_ref[...] = part

            @pl.when(jnp.logical_and(k > 0, k < nk - 1))
            def _():
                acc_ref[...] += part

            @pl.when(k == nk - 1)
            def _():
                finish(acc_ref[...] + part)

    in_specs = [a_spec, b_spec] + ([add_spec] if has_add else []) + list(extra_specs)
    operands = [a, b] + ([add] if has_add else []) + list(extras)
    scratch = [pltpu.VMEM(out_spec.block_shape, F32)] if nk > 1 else []
    sem = ("parallel", "parallel") + (("arbitrary",) if nk > 1 else ())
    return pl.pallas_call(
        body, name=name, grid=grid, out_shape=out_shape,
        in_specs=in_specs, out_specs=out_spec, scratch_shapes=scratch,
        compiler_params=_params(*sem),
    )(*operands)


def _input_grad(dh, w_in, dr1, after):
    _, seq, dm = dh.shape
    bm, bn = _tile(seq, 1024), _tile(dm, 512)

    def body(dh_ref, w_ref, dr_ref, after_ref, o_ref):
        del after_ref
        o_ref[...] = (lax.dot_general(dh_ref[0], w_ref[:, :dm], NT, preferred_element_type=F32)
                      + lax.dot_general(dh_ref[1], w_ref[:, dm:], NT, preferred_element_type=F32)
                      + ALPHA * dr_ref[...])

    tile_spec = pl.BlockSpec((bm, bn), lambda i, j: (i, j))
    return pl.pallas_call(
        body, name="a_in_proj_dx", grid=(seq // bm, dm // bn),
        out_shape=jax.ShapeDtypeStruct((seq, dm), F32),
        in_specs=[pl.BlockSpec((2, bm, dm), lambda i, j: (0, i, 0)), pl.BlockSpec((bn, 2 * dm), lambda i, j: (j, 0)),
                  tile_spec, ANY],
        out_specs=tile_spec,
        compiler_params=_params("parallel", "parallel"),
    )(dh, w_in, dr1, after)


def _cast_bf16(name, a):
    rows, cols = a.shape
    tr = _tile(rows, 512)

    def body(a_ref, o_ref):
        o_ref[...] = a_ref[...].astype(BF16)

    return pl.pallas_call(
        body, name=name, grid=(rows // tr,),
        out_shape=jax.ShapeDtypeStruct(a.shape, BF16),
        in_specs=[pl.BlockSpec((tr, cols), lambda i: (i, 0))],
        out_specs=pl.BlockSpec((tr, cols), lambda i: (i, 0)),
        compiler_params=_params("parallel"),
    )(a)


def _rope_tables(seq):
    inv_freq = ROPE_THETA ** (-jnp.arange(0, HEAD_DIM, 2, dtype=F32) / HEAD_DIM)
    ang = jnp.arange(seq, dtype=F32)[:, None] * inv_freq[None, :]
    cos, sin = jnp.cos(ang), jnp.sin(ang)
    cos, sin = (jnp.concatenate([t, t, t, t], axis=-1) for t in (cos, sin))
    first_half = (jnp.arange(LANES) % HEAD_DIM < HEAD_DIM // 2)[None, :]
    return cos, jnp.where(first_half, -sin, 0.0), jnp.where(first_half, 0.0, sin)


def _rot(t, sin_a, sin_b):
    return pltpu.roll(t, LANES - HEAD_DIM // 2, 1) * sin_a + pltpu.roll(t, HEAD_DIM // 2, 1) * sin_b


def _rope(t, cos, sin_a, sin_b):
    return t * cos + _rot(t, sin_a, sin_b)


def _rope_transposed(dy, cos, sin_a, sin_b):
    return dy * cos - _rot(dy, sin_a, sin_b)


def _silu_parts(z):
    sig = jax.nn.sigmoid(z)
    return z * sig, sig * (1.0 + z * (1.0 - sig))


def _layer_norm_stats(r):
    mu = jnp.mean(r, axis=-1, keepdims=True)
    d = r - mu
    var = jnp.mean(d * d, axis=-1, keepdims=True)
    rstd = lax.rsqrt(var + LN_EPS)
    return d * rstd, rstd


def _layer_norm_backward(dout, xhat, rstd, gain):
    dxh = dout * gain
    m1 = jnp.mean(dxh, axis=-1, keepdims=True)
    m2 = jnp.mean(dxh * xhat, axis=-1, keepdims=True)
    return rstd * (dxh - m1 - xhat * m2)


def _col_sum(v):
    return jnp.sum(v, axis=0, keepdims=True)


ROW_PART = 256


def _row_parts(tile):
    part = min(tile, ROW_PART)
    return [slice(r, r + part) for r in range(0, tile, part)]


def _pool_forward(xb, w_in, wg, scale):
    seq, dm = xb.shape
    n_g = len(POOL_WINDOWS)
    gd = dm // n_g
    tile = _tile(seq, 1024)
    halo_blocks = tile // POOL_HALO

    def body(x_ref, xp_ref, wu_ref, wz_ref, wg_ref, sc_ref, y_ref, p_ref, mx_ref, z_ref):
        i, g = pl.program_id(0), pl.program_id(1)
        u = jnp.dot(x_ref[...], wu_ref[...], preferred_element_type=F32)
        z = jnp.dot(x_ref[...], wz_ref[...], preferred_element_type=F32)
        prev = jnp.where(i > 0, jnp.dot(xp_ref[...], wu_ref[...], preferred_element_type=F32), 0.0)
        s = jnp.concatenate([prev, u], axis=0)
        sums, sh = [], 1
        while sh < POOL_WINDOWS[-1]:
            s = s + pltpu.roll(s, sh, 0)
            sums.append(s)
            sh *= 2
        win = sums[-1]
        for k in range(n_g - 2, -1, -1):
            win = jnp.where(g == k, sums[k], win)
        row = i * tile + lax.broadcasted_iota(jnp.int32, (tile, 1), 0)
        window = jnp.left_shift(2, g).astype(F32)
        p = win[POOL_HALO:, :] * (1.0 / jnp.minimum((row + 1).astype(F32), window)) - u
        pb = p.astype(BF16)
        mx = jnp.dot(pb, wg_ref[...], preferred_element_type=F32)
        y_ref[...] = (mx * sc_ref[...] * (z * jax.nn.sigmoid(z))).astype(BF16)
        p_ref[...] = pb
        mx_ref[...] = mx
        z_ref[...] = z

    out_spec = pl.BlockSpec((tile, gd), lambda i, g: (i, g))
    return pl.pallas_call(
        body, name="pool_fwd", grid=(seq // tile, n_g),
        out_shape=(jax.ShapeDtypeStruct((seq, dm), BF16), jax.ShapeDtypeStruct((seq, dm), BF16),
                   jax.ShapeDtypeStruct((seq, dm), F32), jax.ShapeDtypeStruct((seq, dm), F32)),
        in_specs=[pl.BlockSpec((tile, dm), lambda i, g: (i, 0)),
                  pl.BlockSpec((POOL_HALO, dm), lambda i, g: (jnp.maximum(i * halo_blocks - 1, 0), 0)),
                  pl.BlockSpec((dm, gd), lambda i, g: (0, g)),
                  pl.BlockSpec((dm, gd), lambda i, g: (0, n_g + g)),
                  pl.BlockSpec((None, gd, gd), lambda i, g: (g, 0, 0)),
                  pl.BlockSpec((1, gd), lambda i, g: (0, g))],
        out_specs=(out_spec, out_spec, out_spec, out_spec),
        compiler_params=_params("parallel", "parallel"),
    )(xb, xb, w_in, w_in, wg, scale)


def _pool_mid_backward(dy, mx, z, p, wg, scale, after):
    seq, dm = dy.shape
    gd = dm // len(POOL_WINDOWS)
    tile = _tile(seq, 256)
    n_i = seq // tile

    def body(dy_ref, mx_ref, z_ref, p_ref, wg_ref, sc_ref, after_ref, dh_ref, dwg_ref, st_ref, dwg_acc, carry):
        del after_ref
        i = pl.program_id(0)
        ti = n_i - 1 - i

        @pl.when(i == 0)
        def _():
            dwg_acc[...] = jnp.zeros_like(dwg_acc)
            carry[...] = jnp.zeros_like(carry)
            st_ref[...] = jnp.zeros_like(st_ref)

        row = ti * tile + lax.broadcasted_iota(jnp.int32, (tile, 1), 0)
        count = (row + 1).astype(F32)
        for g, w in enumerate(POOL_WINDOWS):
            cs = slice(g * gd, (g + 1) * gd)
            z = z_ref[:, cs]
            sz, dsz = _silu_parts(z)
            dyg = dy_ref[:, cs]
            mxg = mx_ref[:, cs]
            sc = sc_ref[:, cs]
            t1 = dyg * sz
            st_ref[0:1, cs] += _col_sum(t1 * mxg)
            dh_ref[1, :, cs] = (dyg * (mxg * sc) * dsz).astype(BF16)
            dmx = (t1 * sc).astype(BF16)
            dwg_acc[g] += lax.dot_general(p_ref[:, cs], dmx, TN, preferred_element_type=F32)
            dp = lax.dot_general(dmx, wg_ref[g], NT, preferred_element_type=F32)
            e = dp * (1.0 / jnp.minimum(count, float(w)))
            s = jnp.concatenate([e, carry[:, cs]], axis=0)
            n = tile + POOL_HALO
            sh = 1
            while sh < w:
                s = s + pltpu.roll(s, n - sh, 0)
                sh *= 2
            dh_ref[0, :, cs] = (s[:tile, :] - dp).astype(BF16)
            carry[:, cs] = e[:POOL_HALO, :]

        @pl.when(i == n_i - 1)
        def _():
            dwg_ref[...] = dwg_acc[...].astype(BF16)

    row_spec = pl.BlockSpec((tile, dm), lambda i: (n_i - 1 - i, 0))
    return pl.pallas_call(
        body, name="pool_mid_bwd", grid=(n_i,),
        out_shape=(jax.ShapeDtypeStruct((2, seq, dm), BF16), jax.ShapeDtypeStruct(wg.shape, BF16),
                   jax.ShapeDtypeStruct((STAT_ROWS, dm), F32)),
        in_specs=[row_spec, row_spec, row_spec, row_spec,
                  pl.BlockSpec(wg.shape, lambda i: (0, 0, 0)),
                  pl.BlockSpec((1, dm), lambda i: (0, 0)), ANY],
        out_specs=(pl.BlockSpec((2, tile, dm), lambda i: (0, n_i - 1 - i, 0)),
                   pl.BlockSpec(wg.shape, lambda i: (0, 0, 0)),
                   pl.BlockSpec((STAT_ROWS, dm), lambda i: (0, 0))),
        scratch_shapes=[pltpu.VMEM(wg.shape, F32), pltpu.VMEM((POOL_HALO, dm), F32)],
        compiler_params=_params("arbitrary"),
    )(dy, mx, z, p, wg, scale, after)


def _out_proj_norm(y, w, x, gain, bias):
    seq, dm = x.shape
    tile = _tile(seq, 512)

    def body(y_ref, w_ref, x_ref, g_ref, b_ref, xhat_ref, rstd_ref, xb_ref):
        for rows in _row_parts(tile):
            o = jnp.dot(y_ref[rows, :], w_ref[...], preferred_element_type=F32)
            xhat, rstd = _layer_norm_stats(ALPHA * x_ref[rows, :] + o)
            xhat_ref[rows, :] = xhat
            rstd_ref[rows, :] = rstd
            xb_ref[rows, :] = (xhat * g_ref[...] + b_ref[...]).astype(BF16)

    row_spec = pl.BlockSpec((tile, dm), lambda i: (i, 0))
    vec_spec = pl.BlockSpec((1, dm), lambda i: (0, 0))
    return pl.pallas_call(
        body, name="out_proj_norm_a", grid=(seq // tile,),
        out_shape=(jax.ShapeDtypeStruct((seq, dm), F32), jax.ShapeDtypeStruct((seq, 1), F32),
                   jax.ShapeDtypeStruct((seq, dm), BF16)),
        in_specs=[row_spec, pl.BlockSpec(w.shape, lambda i: (0, 0), pipeline_mode=pl.Buffered(1)), row_spec, vec_spec,
                  vec_spec],
        out_specs=(row_spec, pl.BlockSpec((tile, 1), lambda i: (i, 0)), row_spec),
        compiler_params=_params("parallel"),
    )(y, w, x, gain, bias)


def _kv_proj(xb, wkv, tables):
    seq, dm = xb.shape
    kvw = wkv.shape[1] // 2
    n_kv = kvw // HEAD_DIM
    tile = _tile(seq, 512)

    def body(x_ref, w_ref, cos_ref, sa_ref, sb_ref, kd_ref, vd_ref, kt_ref, vt_ref):
        kv = jnp.dot(x_ref[...], w_ref[...], preferred_element_type=F32)
        low = lax.broadcasted_iota(jnp.int32, (1, LANES), 1) < HEAD_DIM
        cos, sa, sb = cos_ref[...], sa_ref[...], sb_ref[...]

        def put(pair, h, nat_ref, t_ref):
            swapped = pltpu.roll(pair, HEAD_DIM, 1)
            for head, dup in ((h, jnp.where(low, pair, swapped)), (h + 1, jnp.where(low, swapped, pair))):
                nat_ref[head] = dup.astype(BF16)
                t_ref[head] = dup.T.astype(BF16)

        for j in range(kvw // LANES):
            put(_rope(kv[:, j * LANES:(j + 1) * LANES], cos, sa, sb), 2 * j, kd_ref, kt_ref)
            put(kv[:, kvw + j * LANES:kvw + (j + 1) * LANES], 2 * j, vd_ref, vt_ref)

    tab_spec = pl.BlockSpec((tile, LANES), lambda i: (i, 0))
    dup_spec = pl.BlockSpec((n_kv, tile, LANES), lambda i: (0, i, 0))
    dup_shape = jax.ShapeDtypeStruct((n_kv, seq, LANES), BF16)
    t_spec = pl.BlockSpec((n_kv, LANES, tile), lambda i: (0, 0, i))
    t_shape = jax.ShapeDtypeStruct((n_kv, LANES, seq), BF16)
    return pl.pallas_call(
        body, name="kv_proj", grid=(seq // tile,),
        out_shape=(dup_shape, dup_shape, t_shape, t_shape),
        in_specs=[pl.BlockSpec((tile, dm), lambda i: (i, 0)), pl.BlockSpec(wkv.shape, lambda i: (0, 0)),
                  tab_spec, tab_spec, tab_spec],
        out_specs=(dup_spec, dup_spec, t_spec, t_spec),
        compiler_params=_params("parallel"),
    )(xb, wkv, *tables)


ATTN_STEP_BLOCKS = 16


def _head_queries(q_ref, rows, low):
    parts = []
    for j in range(GQA_GROUP // 2):
        q2 = q_ref[rows, j * LANES:(j + 1) * LANES]
        parts += [jnp.where(low, q2, 0), jnp.where(low, 0, q2)]
    return parts


def _key_window(prev_ref, cur_ref, b, axis):
    def block(i):
        sl = slice(i * ATTN_BLOCK, (i + 1) * ATTN_BLOCK)
        return cur_ref[sl, :] if axis == 0 else cur_ref[:, sl]
    return jnp.concatenate([prev_ref[...] if b == 0 else block(b - 1), block(b)], axis=axis)


def _mask_bias(n):
    key = lax.broadcasted_iota(jnp.int32, (2 * ATTN_BLOCK, ATTN_BLOCK), 0)
    qry = lax.broadcasted_iota(jnp.int32, (2 * ATTN_BLOCK, ATTN_BLOCK), 1)
    valid = (key > qry) & (key <= qry + ATTN_BLOCK) & ((key >= ATTN_BLOCK) | (n > 0))
    return jnp.where(valid, 0.0, NEG_INF)


def _head_probs_transposed(kcat, qm, bias, sink):
    st = lax.dot_general(kcat, qm, NT, preferred_element_type=F32) + bias
    m = jnp.maximum(jnp.max(st, axis=0, keepdims=True), sink)
    e = jnp.exp(st - m)
    e_sink = jnp.exp(sink - m)
    inv = 1.0 / (jnp.sum(e, axis=0, keepdims=True) + e_sink)
    return e * inv, e_sink * inv


def _probs_transposed(n, kh, kcat, q_all, sink_ref):
    st = lax.dot_general(kcat, q_all, NT, preferred_element_type=F32)
    st = st + jnp.tile(_mask_bias(n), (1, GQA_GROUP))
    sink = jnp.concatenate([jnp.full((1, ATTN_BLOCK), sink_ref[0, kh * GQA_GROUP + h], F32)
                            for h in range(GQA_GROUP)], axis=1)
    m = jnp.maximum(jnp.max(st, axis=0, keepdims=True), sink)
    e = jnp.exp(st - m)
    e_sink = jnp.exp(sink - m)
    inv = 1.0 / (jnp.sum(e, axis=0, keepdims=True) + e_sink)
    return e * inv, e_sink * inv


def _attn_specs(n_width, qb):
    rows = qb * ATTN_BLOCK
    before = lambda n: jnp.maximum(n * qb - 1, 0)
    q_spec = pl.BlockSpec((rows, n_width), lambda kh, n: (n, kh))
    cur = pl.BlockSpec((None, rows, LANES), lambda kh, n: (kh, n, 0))
    prev = pl.BlockSpec((None, ATTN_BLOCK, LANES), lambda kh, n: (kh, before(n), 0))
    cur_t = pl.BlockSpec((None, LANES, rows), lambda kh, n: (kh, 0, n))
    prev_t = pl.BlockSpec((None, LANES, ATTN_BLOCK), lambda kh, n: (kh, 0, before(n)))
    return q_spec, cur, prev, cur_t, prev_t


def _pair_product_transposed(mat_t, rhs, j, low_rows):
    head_a = slice(2 * j * ATTN_BLOCK, (2 * j + 1) * ATTN_BLOCK)
    head_b = slice((2 * j + 1) * ATTN_BLOCK, (2 * j + 2) * ATTN_BLOCK)
    out_t = (jnp.dot(jnp.where(low_rows, mat_t, 0), rhs[:, head_a], preferred_element_type=F32)
             + jnp.dot(jnp.where(low_rows, 0, mat_t), rhs[:, head_b], preferred_element_type=F32))
    return out_t.T


def _attn_forward(qs, kd, vt, zb, sinks):
    seq, dm = qs.shape
    n_kv = kd.shape[0]
    gw = GQA_GROUP * HEAD_DIM

    qb = ATTN_STEP_BLOCKS if (seq // ATTN_BLOCK) % ATTN_STEP_BLOCKS == 0 else 1

    def body(q_ref, kp_ref, kc_ref, vtp_ref, vtc_ref, z_ref, sink_ref, att_ref, yb_ref):
        kh, n = pl.program_id(0), pl.program_id(1)
        low = lax.broadcasted_iota(jnp.int32, (1, LANES), 1) < HEAD_DIM
        low_rows = lax.broadcasted_iota(jnp.int32, (LANES, 1), 0) < HEAD_DIM
        for b in range(qb):
            rows = slice(b * ATTN_BLOCK, (b + 1) * ATTN_BLOCK)
            kcat = _key_window(kp_ref, kc_ref, b, 0)
            vt = _key_window(vtp_ref, vtc_ref, b, 1)
            bias = _mask_bias(n * qb + b)
            pt = jnp.concatenate(
                [_head_probs_transposed(kcat, qm, bias, sink_ref[0, kh * GQA_GROUP + h])[0].astype(BF16)
                 for h, qm in enumerate(_head_queries(q_ref, rows, low))], axis=1)
            for j in range(GQA_GROUP // 2):
                cs = slice(j * LANES, (j + 1) * LANES)
                o2 = _pair_product_transposed(vt, pt, j, low_rows)
                att_ref[rows, cs] = o2
                z = z_ref[rows, cs]
                yb_ref[rows, cs] = (o2 * (z * jax.nn.sigmoid(z))).astype(BF16)

    q_spec, cur, prev, cur_t, prev_t = _attn_specs(gw, qb)
    return pl.pallas_call(
        body, name="attn_fwd", grid=(n_kv, seq // (qb * ATTN_BLOCK)),
        out_shape=(jax.ShapeDtypeStruct((seq, dm), F32), jax.ShapeDtypeStruct((seq, dm), BF16)),
        in_specs=[q_spec, prev, cur, prev_t, cur_t, q_spec, pl.BlockSpec(memory_space=pltpu.SMEM)],
        out_specs=(q_spec, q_spec),
        compiler_params=_params("parallel", "parallel"),
    )(qs, kd, kd, vt, vt, zb, sinks)


def _attn_backward(qs, kd, vd, kt, zb, att, dyb, sinks, tables, after):
    seq, dm = qs.shape
    n_kv = kd.shape[0]
    gw = GQA_GROUP * HEAD_DIM
    n_blocks = seq // ATTN_BLOCK
    qb = ATTN_STEP_BLOCKS if n_blocks % ATTN_STEP_BLOCKS == 0 else 1

    def body(q_ref, kp_ref, kc_ref, vp_ref, vc_ref, ktp_ref, ktc_ref, z_ref, att_ref, dyb_ref, sink_ref,
             cos_ref, sa_ref, sb_ref, after_ref, dh_ref, dk_ref, dv_ref, ds_ref):
        del after_ref
        kh, n = pl.program_id(0), pl.program_id(1)

        @pl.when(n == 0)
        def _():
            dk_ref[...] = jnp.zeros_like(dk_ref)
            dv_ref[...] = jnp.zeros_like(dv_ref)

        @pl.when(jnp.logical_and(n == 0, kh == 0))
        def _():
            ds_ref[...] = jnp.zeros_like(ds_ref)

        low = lax.broadcasted_iota(jnp.int32, (1, LANES), 1) < HEAD_DIM
        low_rows = lax.broadcasted_iota(jnp.int32, (LANES, 1), 0) < HEAD_DIM
        head_lane = lax.broadcasted_iota(jnp.int32, (1, LANES), 1)
        dsink = jnp.zeros((1, LANES), F32)
        for b in range(qb):
            rows = slice(b * ATTN_BLOCK, (b + 1) * ATTN_BLOCK)
            kcat = _key_window(kp_ref, kc_ref, b, 0)
            vcat = _key_window(vp_ref, vc_ref, b, 0)
            kt = _key_window(ktp_ref, ktc_ref, b, 1)
            cos, sa, sb = cos_ref[rows, :], sa_ref[rows, :], sb_ref[rows, :]
            q_all = jnp.concatenate(_head_queries(q_ref, rows, low), axis=0)
            d_parts = []
            for j in range(GQA_GROUP // 2):
                cs = slice(j * LANES, (j + 1) * LANES)
                sz, dsz = _silu_parts(z_ref[rows, cs])
                dy2 = dyb_ref[rows, cs]
                dh_ref[1, rows, cs] = (dy2 * att_ref[rows, cs] * dsz).astype(BF16)
                datt = (dy2 * sz).astype(BF16)
                d_parts += [jnp.where(low, datt, 0), jnp.where(low, 0, datt)]
            d_all = jnp.concatenate(d_parts, axis=0)
            probs_t, sink_p = _probs_transposed(n * qb + b, kh, kcat, q_all, sink_ref)
            dprobs_t = lax.dot_general(vcat, d_all, NT, preferred_element_type=F32)
            row_dot = jnp.sum(probs_t * dprobs_t, axis=0, keepdims=True)
            ds_t = (probs_t * (dprobs_t - row_dot)).astype(BF16)
            dk = jnp.dot(ds_t, q_all, preferred_element_type=F32)
            dv = jnp.dot(probs_t.astype(BF16), d_all, preferred_element_type=F32)
            for j in range(GQA_GROUP // 2):
                dq2 = _pair_product_transposed(kt, ds_t, j, low_rows)
                dh_ref[0, rows, j * LANES:(j + 1) * LANES] = (
                    _rope_transposed(dq2, cos, sa, sb) * 0.125).astype(BF16)
            sink_dot = sink_p * row_dot
            for h in range(GQA_GROUP):
                part = jnp.sum(sink_dot[:, h * ATTN_BLOCK:(h + 1) * ATTN_BLOCK], axis=1, keepdims=True)
                dsink = dsink - jnp.where(head_lane == kh * GQA_GROUP + h, part, 0.0)

            def add_window(dk=dk, dv=dv, b=b):
                start = pl.multiple_of((n * qb + b - 1) * ATTN_BLOCK, ATTN_BLOCK)
                dk_ref[pl.ds(start, 2 * ATTN_BLOCK), :] += dk
                dv_ref[pl.ds(start, 2 * ATTN_BLOCK), :] += dv

            if b > 0:
                add_window()
            else:
                pl.when(n > 0)(add_window)

                @pl.when(n == 0)
                def _(dk=dk, dv=dv):
                    dk_ref[pl.ds(0, ATTN_BLOCK), :] += dk[ATTN_BLOCK:, :]
                    dv_ref[pl.ds(0, ATTN_BLOCK), :] += dv[ATTN_BLOCK:, :]
        ds_ref[0:1, :] += dsink

    q_spec, cur, prev, cur_t, prev_t = _attn_specs(gw, qb)
    tab_spec = pl.BlockSpec((qb * ATTN_BLOCK, LANES), lambda kh, n: (n, 0))
    acc_spec = pl.BlockSpec((None, seq, LANES), lambda kh, n: (kh, 0, 0))
    acc_shape = jax.ShapeDtypeStruct((n_kv, seq, LANES), F32)
    return pl.pallas_call(
        body, name="attn_bwd", grid=(n_kv, n_blocks // qb),
        out_shape=(jax.ShapeDtypeStruct((2, seq, dm), BF16), acc_shape, acc_shape,
                   jax.ShapeDtypeStruct((STAT_ROWS, LANES), F32)),
        in_specs=[q_spec, prev, cur, prev, cur, prev_t, cur_t, q_spec, q_spec, q_spec,
                  pl.BlockSpec(memory_space=pltpu.SMEM), tab_spec, tab_spec, tab_spec, ANY],
        out_specs=(pl.BlockSpec((2, qb * ATTN_BLOCK, gw), lambda kh, n: (0, n, kh)), acc_spec, acc_spec,
                   pl.BlockSpec((STAT_ROWS, LANES), lambda kh, n: (0, 0))),
        compiler_params=_params("arbitrary", "arbitrary"),
    )(qs, kd, kd, vd, vd, kt, kt, zb, att, dyb, sinks, *tables, after)


def _kv_grad_fold(dk, dv, tables):
    n_kv, seq, _ = dk.shape
    kvw = n_kv * HEAD_DIM
    tile = _tile(seq, 512)

    def body(dk_ref, dv_ref, cos_ref, sa_ref, sb_ref, o_ref):
        low = lax.broadcasted_iota(jnp.int32, (1, LANES), 1) < HEAD_DIM
        cos, sa, sb = cos_ref[...], sa_ref[...], sb_ref[...]

        def folded(ref, h):
            t = ref[h]
            return t + pltpu.roll(t, HEAD_DIM, 1)

        for j in range(n_kv // 2):
            ka = _rope_transposed(folded(dk_ref, 2 * j), cos, sa, sb)
            kb = _rope_transposed(folded(dk_ref, 2 * j + 1), cos, sa, sb)
            o_ref[:, j * LANES:(j + 1) * LANES] = jnp.where(low, ka, kb).astype(BF16)
            o_ref[:, kvw + j * LANES:kvw + (j + 1) * LANES] = jnp.where(
                low, folded(dv_ref, 2 * j), folded(dv_ref, 2 * j + 1)).astype(BF16)

    tab_spec = pl.BlockSpec((tile, LANES), lambda i: (i, 0))
    in_spec = pl.BlockSpec((n_kv, tile, LANES), lambda i: (0, i, 0))
    return pl.pallas_call(
        body, name="kv_grad_fold", grid=(seq // tile,),
        out_shape=jax.ShapeDtypeStruct((seq, 2 * kvw), BF16),
        in_specs=[in_spec, in_spec, tab_spec, tab_spec, tab_spec],
        out_specs=pl.BlockSpec((tile, 2 * kvw), lambda i: (i, 0)),
        compiler_params=_params("parallel"),
    )(dk, dv, *tables)


def _out_proj_norm_loss(yb, w, xhat1, gain0, bias0, gain1, bias1, target):
    seq, dm = xhat1.shape
    tile = _tile(seq, 512)

    def body(y_ref, w_ref, xh1_ref, g0_ref, b0_ref, g1_ref, b1_ref, t_ref, dr_ref, drb_ref, st_ref):
        i = pl.program_id(0)

        @pl.when(i == 0)
        def _():
            st_ref[...] = jnp.zeros_like(st_ref)

        for rows in _row_parts(tile):
            ob = jnp.dot(y_ref[rows, :], w_ref[...], preferred_element_type=F32)
            x1 = xh1_ref[rows, :] * g0_ref[...] + b0_ref[...]
            xhat, rstd = _layer_norm_stats(ALPHA * x1 + ob)
            err = xhat * g1_ref[...] + b1_ref[...] - t_ref[rows, :]
            dout = err * (1.0 / dm)
            dr = _layer_norm_backward(dout, xhat, rstd, g1_ref[...])
            dr_ref[rows, :] = dr
            drb_ref[rows, :] = dr.astype(BF16)
            st_ref[0:1, :] += _col_sum(dout * xhat)
            st_ref[1:2, :] += _col_sum(dout)
            st_ref[2:3, :] += _col_sum(err * err)

    row_spec = pl.BlockSpec((tile, dm), lambda i: (i, 0))
    vec_spec = pl.BlockSpec((1, dm), lambda i: (0, 0))
    return pl.pallas_call(
        body, name="out_proj_norm_loss_b", grid=(seq // tile,),
        out_shape=(jax.ShapeDtypeStruct((seq, dm), F32), jax.ShapeDtypeStruct((seq, dm), BF16),
                   jax.ShapeDtypeStruct((STAT_ROWS, dm), F32)),
        in_specs=[row_spec, pl.BlockSpec(w.shape, lambda i: (0, 0), pipeline_mode=pl.Buffered(1)), row_spec, vec_spec,
                  vec_spec, vec_spec,
                  vec_spec, row_spec],
        out_specs=(row_spec, row_spec, pl.BlockSpec((STAT_ROWS, dm), lambda i: (0, 0))),
        compiler_params=_params("arbitrary"),
    )(yb, w, xhat1, gain0, bias0, gain1, bias1, target)


def _stream_grad_norm_backward(dhq, wqg, dkv, wkv, dr2, xhat1, rstd1, gain0, after):
    seq, dm = dr2.shape
    tile = _tile(seq, 256)

    def body(dh_ref, wqg_ref, dkv_ref, wkv_ref, dr2_ref, xh_ref, rstd_ref, g_ref, after_ref, dr_ref, drb_ref, st_ref):
        del after_ref

        @pl.when(pl.program_id(0) == 0)
        def _():
            st_ref[...] = jnp.zeros_like(st_ref)

        dx1 = (lax.dot_general(dh_ref[0], wqg_ref[:, :dm], NT, preferred_element_type=F32)
               + lax.dot_general(dh_ref[1], wqg_ref[:, dm:], NT, preferred_element_type=F32)
               + lax.dot_general(dkv_ref[...], wkv_ref[...], NT, preferred_element_type=F32)
               + ALPHA * dr2_ref[...])
        xhat = xh_ref[...]
        dr = _layer_norm_backward(dx1, xhat, rstd_ref[...], g_ref[...])
        dr_ref[...] = dr
        drb_ref[...] = dr.astype(BF16)
        st_ref[0:1, :] += _col_sum(dx1 * xhat)
        st_ref[1:2, :] += _col_sum(dx1)

    row_spec = pl.BlockSpec((tile, dm), lambda i: (i, 0))
    resident = pl.Buffered(1)
    return pl.pallas_call(
        body, name="stream_grad_norm_bwd", grid=(seq // tile,),
        out_shape=(jax.ShapeDtypeStruct((seq, dm), F32), jax.ShapeDtypeStruct((seq, dm), BF16),
                   jax.ShapeDtypeStruct((STAT_ROWS, dm), F32)),
        in_specs=[pl.BlockSpec((2, tile, dm), lambda i: (0, i, 0)),
                  pl.BlockSpec(wqg.shape, lambda i: (0, 0), pipeline_mode=resident),
                  pl.BlockSpec((tile, dkv.shape[1]), lambda i: (i, 0)),
                  pl.BlockSpec(wkv.shape, lambda i: (0, 0), pipeline_mode=resident),
                  row_spec, row_spec, pl.BlockSpec((tile, 1), lambda i: (i, 0)),
                  pl.BlockSpec((1, dm), lambda i: (0, 0)), ANY],
        out_specs=(row_spec, row_spec, pl.BlockSpec((STAT_ROWS, dm), lambda i: (0, 0))),
        compiler_params=_params("arbitrary"),
    )(dhq, wqg, dkv, wkv, dr2, xhat1, rstd1, gain0, after)


def _adamw_math(w, g, m, v):
    m = ADAM_B1 * m + (1.0 - ADAM_B1) * g
    v = ADAM_B2 * v + (1.0 - ADAM_B2) * (g * g)
    m_hat = m / (1.0 - ADAM_B1 ** ADAM_STEP)
    v_hat = v / (1.0 - ADAM_B2 ** ADAM_STEP)
    delta = -ADAM_LR * (m_hat / (jnp.sqrt(v_hat) + ADAM_EPS) + ADAM_WD * w)
    return delta, m, v


def _sum_devices(ref):
    total = ref[0].astype(F32)
    for d in range(1, ref.shape[0]):
        total = total + ref[d].astype(F32)
    return total


def _adamw_shard(name, parts, w, m, v, after):
    rows, cols = w.shape
    n_parts = len(parts)
    part_rows = rows // n_parts
    tr = _tile(part_rows, max(8, (1 << 18) // cols)) if part_rows >= 8 else part_rows
    per_part = part_rows // tr

    def body(*refs):
        p_refs = refs[:n_parts]
        w_ref, m_ref, v_ref, _, g_out, d_out, m_out, v_out = refs[n_parts:]
        g = _sum_devices(p_refs[0])
        for k in range(1, n_parts):
            g = jnp.where(pl.program_id(0) >= k * per_part, _sum_devices(p_refs[k]), g)
        delta, m_new, v_new = _adamw_math(w_ref[...], g, m_ref[...], v_ref[...])
        g_out[...] = g
        d_out[...] = delta
        m_out[...] = m_new
        v_out[...] = v_new

    def part_spec(k):
        return pl.BlockSpec((parts[k].shape[0], tr, cols),
                            lambda i: (0, jnp.clip(i - k * per_part, 0, per_part - 1), 0))

    spec = pl.BlockSpec((tr, cols), lambda i: (i, 0))
    shape = jax.ShapeDtypeStruct((rows, cols), F32)
    return pl.pallas_call(
        body, name=name, grid=(rows // tr,),
        out_shape=(shape, shape, shape, shape),
        in_specs=[part_spec(k) for k in range(n_parts)] + [spec, spec, spec, ANY],
        out_specs=(spec, spec, spec, spec),
        compiler_params=_params("arbitrary"),
    )(*parts, w, m, v, after)


def _adamw_replicated(stats_b, stats_a, sink_parts, ln_g, ln_b, sinks, m_ln_g, m_ln_b, m_sinks, v_ln_g, v_ln_b,
                      v_sinks, after):
    n_q = sinks.shape[1]
    dm = ln_g.shape[1]

    def body(sb_ref, sa_ref, sk_ref, g_ref, b_ref, s_ref, mg_ref, mb_ref, ms_ref, vg_ref, vb_ref, vs_ref, after_ref,
             *outs):
        del after_ref
        layer_sums = (_sum_devices(sa_ref), _sum_devices(sb_ref))
        outs[12][...] = jnp.sum(layer_sums[1][2:3, :], axis=1, keepdims=True) * (0.5 / dm)
        for which, (w_ref, m_ref, v_ref) in enumerate(((g_ref, mg_ref, vg_ref), (b_ref, mb_ref, vb_ref))):
            for layer in range(DEPTH):
                row = slice(layer, layer + 1)
                g = layer_sums[layer][which:which + 1, :]
                res = (g,) + _adamw_math(w_ref[row, :], g, m_ref[row, :], v_ref[row, :])
                for o_ref, val in zip(outs[4 * which:4 * which + 4], res):
                    o_ref[row, :] = val
        g = _sum_devices(sk_ref)[0:1, 0:n_q]
        res = (g,) + _adamw_math(s_ref[...], g, ms_ref[...], vs_ref[...])
        for o_ref, val in zip(outs[8:12], res):
            o_ref[...] = val

    vmem = pl.BlockSpec(memory_space=pltpu.VMEM)
    shapes = [jax.ShapeDtypeStruct(a.shape, F32) for a in (ln_g, ln_b, sinks) for _ in range(4)]
    shapes.append(jax.ShapeDtypeStruct((1, 1), F32))
    return pl.pallas_call(
        body, name="adamw_replicated", out_shape=tuple(shapes),
        in_specs=[vmem] * 12 + [ANY], out_specs=tuple([vmem] * 13),
    )(stats_b, stats_a, sink_parts, ln_g, ln_b, sinks, m_ln_g, m_ln_b, m_sinks, v_ln_g, v_ln_b, v_sinks, after)


def kernel(x, ln_g, ln_b, a_w_in, a_w_group, a_scale, a_w_out, b_w_k, b_w_v, b_w_qg, b_sinks, b_w_out, loss_target, m_ln_g, m_ln_b, m_a_w_in, m_a_w_group, m_a_scale, m_a_w_out, m_b_w_k, m_b_w_v, m_b_w_qg, m_b_sinks, m_b_w_out, v_ln_g, v_ln_b, v_a_w_in, v_a_w_group, v_a_scale, v_a_w_out, v_b_w_k, v_b_w_v, v_b_w_qg, v_b_sinks, v_b_w_out):
    _, seq, dm = x.shape
    n_groups = len(POOL_WINDOWS)
    gd = dm // n_groups
    kvw = b_w_k.shape[1]
    cb = 2 * dm // N_DEV
    rb = dm // N_DEV
    gb = gd // N_DEV

    x2 = x.reshape(seq, dm)
    target = loss_target.reshape(seq, dm)
    w_in_s = a_w_in.reshape(dm, cb)
    w_g_s = a_w_group.reshape(n_groups, gb, gd)
    w_out_s = a_w_out.reshape(rb, dm)
    w_qg_s = b_w_qg.reshape(dm, cb)
    w_outb_s = b_w_out.reshape(rb, dm)

    def cols(ref, dev):
        return ref.at[:, pl.ds(pl.multiple_of(dev * cb, LANES), cb)]

    def rows(ref, dev):
        return ref.at[pl.ds(pl.multiple_of(dev * rb, 8), rb), :]

    def group_rows(ref, dev):
        return ref.at[:, pl.ds(pl.multiple_of(dev * gb, 8), gb), :]

    def k_rows(ref, dev):
        return ref.at[pl.ds(pl.multiple_of(dev * rb, 8), rb), pl.ds(0, kvw)]

    def v_rows(ref, dev):
        return ref.at[pl.ds(pl.multiple_of(dev * rb, 8), rb), pl.ds(kvw, kvw)]

    def scale_cols(ref, dev):
        return ref.at[:, pl.ds(pl.multiple_of(dev * rb, LANES), rb)]

    bf = lambda a: a.astype(BF16)
    wide, square = jax.ShapeDtypeStruct((dm, 2 * dm), BF16), jax.ShapeDtypeStruct((dm, dm), BF16)
    w_g, scale, w_in = _gather_weights(
        "gather_a_in", 0, [(bf(w_g_s), 0, group_rows), (a_scale, 1, scale_cols), (bf(w_in_s), 2, cols)],
        [jax.ShapeDtypeStruct((n_groups, gd, gd), BF16), jax.ShapeDtypeStruct((1, dm), F32), wide])
    (w_out,) = _gather_weights("gather_a_out", 1, [(bf(w_out_s), 0, rows)], [square])
    w_kv, w_qg = _gather_weights(
        "gather_b_in", 2, [(bf(b_w_k), 0, k_rows), (bf(b_w_v), 0, v_rows), (bf(w_qg_s), 1, cols)],
        [jax.ShapeDtypeStruct((dm, 2 * kvw), BF16), wide])
    (w_outb,) = _gather_weights("gather_b_out", 3, [(bf(w_outb_s), 0, rows)], [square])

    tables = _rope_tables(seq)
    bm = _tile(seq, 1024)
    bn = _tile(dm, 1024)
    g0, g1, b0, b1 = ln_g[0:1], ln_g[1:2], ln_b[0:1], ln_b[1:2]

    xb = _cast_bf16("cast_x", x2)
    y, pooled, mixed, z_a = _pool_forward(xb, w_in, w_g, scale)
    xhat1, rstd1, x1b = _out_proj_norm(y, w_out, x2, g0, b0)

    kd, vd, kt, vt = _kv_proj(x1b, w_kv, tables)
    bmq = bm
    tab_spec = pl.BlockSpec((bmq, LANES), lambda i, j: (i, 0))

    def rope_scale(val, cos_ref, sa_ref, sb_ref):
        cos, sa, sb = cos_ref[...], sa_ref[...], sb_ref[...]
        return jnp.concatenate([_rope(val[:, j * LANES:(j + 1) * LANES], cos, sa, sb) * 0.125
                                for j in range(val.shape[1] // LANES)], axis=1)

    qs = _mm("b_q_proj", x1b, w_qg, dims=NN, grid=(seq // bmq, dm // bn),
             a_spec=pl.BlockSpec((bmq, dm), lambda i, j: (i, 0)), b_spec=pl.BlockSpec((dm, bn), lambda i, j: (0, j)),
             out_shape=jax.ShapeDtypeStruct((seq, dm), BF16), out_spec=pl.BlockSpec((bmq, bn), lambda i, j: (i, j)),
             epilogue=rope_scale, extras=tables, extra_specs=(tab_spec,) * 3)
    zb = _mm("b_gate_proj", x1b, w_qg, dims=NN, grid=(seq // bm, dm // bn),
             a_spec=pl.BlockSpec((bm, dm), lambda i, j: (i, 0)),
             b_spec=pl.BlockSpec((dm, bn), lambda i, j: (0, j + dm // bn)),
             out_shape=jax.ShapeDtypeStruct((seq, dm), F32), out_spec=pl.BlockSpec((bm, bn), lambda i, j: (i, j)))
    att, yb = _attn_forward(qs, kd, vt, zb, b_sinks)
    dr2, dr2b, stats_b = _out_proj_norm_loss(yb, w_outb, xhat1, g0, b0, g1, b1, target)

    def weight_grad(name, a, b, n_cols, b_spec=None, part=(0, 1), after=None):
        m_cols = a.shape[1] // part[1]
        tm, tn = _tile(m_cols, 1024), _tile(n_cols, 512)
        first = part[0] * (m_cols // tm)
        return _mm(name, a, b, dims=TN, grid=(m_cols // tm, n_cols // tn),
                   a_spec=pl.BlockSpec((seq, tm), lambda i, j: (0, first + i)),
                   b_spec=b_spec(tn) if b_spec else pl.BlockSpec((seq, tn), lambda i, j: (0, j)),
                   out_shape=jax.ShapeDtypeStruct((m_cols, n_cols), BF16),
                   out_spec=pl.BlockSpec((tm, tn), lambda i, j: (i, j)),
                   extras=() if after is None else (after,), extra_specs=() if after is None else (ANY,))

    def halves_spec(tn):
        per = dm // tn
        return pl.BlockSpec((None, seq, tn), lambda i, j: (j // per, 0, j % per))

    def times_transposed(name, a, w):
        return _mm(name, a, w, dims=NT, grid=(seq // bm, dm // bn),
                   a_spec=pl.BlockSpec((bm, a.shape[1]), lambda i, j: (i, 0)),
                   b_spec=pl.BlockSpec((bn, w.shape[1]), lambda i, j: (j, 0)),
                   out_shape=jax.ShapeDtypeStruct((seq, dm), F32), out_spec=pl.BlockSpec((bm, bn), lambda i, j: (i, j)))

    def stat_row_cols(ref, dev):
        return ref.at[pl.ds(0, 1), pl.ds(pl.multiple_of(dev * rb, LANES), rb)]

    upd = {}
    last = [dr2b]
    my_core = lax.axis_index("c").astype(jnp.int32).reshape(1)

    def then(value):
        last[0] = value[0] if isinstance(value, (list, tuple)) else value
        return value

    def shard_update(key, parts, w, m, v):
        shape = w.shape
        flat = lambda a: a.reshape(-1, shape[-1])
        parts = list(parts) if isinstance(parts, (list, tuple)) else [parts]
        outs = then(_adamw_shard("adamw_" + key, [p.reshape(p.shape[0], -1, shape[-1]) for p in parts], flat(w),
                                 flat(m), flat(v), last[0]))
        upd[key] = [o.reshape(shape) for o in outs]

    def two_level_scatter(name, ids, streams):
        staged = _sibling_exchange(name + "_pair", streams, ids[0])

        def finish():
            sums = [then(_pair_sum(f"{name}_sum{s}", st[0], got, my_core, last[0]))
                    for s, (st, got) in enumerate(zip(streams, staged))]
            return _chip_exchange(name + "_chip", sums, ids[1])
        return finish

    d_w_outb = then(weight_grad("b_out_proj_dw", yb, dr2b, dm))
    (p_outb,) = _exchange_blocks("scatter_b_out", [(d_w_outb, rows, (rb, dm))], 4)
    dyb = times_transposed("b_out_proj_dx", dr2b, w_outb)
    dhq, dkd, dvd, dsink = then(_attn_backward(qs, kd, vd, kt, zb, att, dyb, b_sinks, tables, after=last[0]))
    dkv = _kv_grad_fold(dkd, dvd, tables)
    d_w_kv = weight_grad("b_kv_proj_dw", x1b, dkv, 2 * kvw)
    d_w_qg = then(weight_grad("b_qg_proj_dw", x1b, dhq, 2 * dm, halves_spec, after=d_w_kv))
    finish_b_in = two_level_scatter("scatter_b_in", (5, 11), [(d_w_qg, cols, (dm, cb))])
    dr1, dr1b, stats_a = _stream_grad_norm_backward(dhq, w_qg, dkv, w_kv, dr2, xhat1, rstd1, g0, after=last[0])
    last[0] = dr1b
    shard_update("b_w_out", p_outb, b_w_out, m_b_w_out, v_b_w_out)
    (p_qg,) = finish_b_in()
    all_b, all_a, all_sink = _exchange_blocks("gather_replicated_grads", [
        (stats_b, None, stats_b.shape), (stats_a, None, stats_a.shape), (dsink, None, dsink.shape)], 9)

    d_w_out = then(weight_grad("a_out_proj_dw", y, dr1b, dm, after=last[0]))
    p_out, p_k, p_v = _exchange_blocks("scatter_a_out", [
        (d_w_out, rows, (rb, dm)), (d_w_kv, k_rows, (rb, kvw)), (d_w_kv, v_rows, (rb, kvw))], 6)
    dy = times_transposed("a_out_proj_dx", dr1b, w_out)
    dh, d_w_g, stats_s = then(_pool_mid_backward(dy, mixed, z_a, pooled, w_g, scale, after=last[0]))
    p_g, p_scale = _exchange_blocks("scatter_a_mid", [
        (d_w_g, group_rows, (n_groups, gb, gd)), (stats_s, stat_row_cols, (1, rb))], 7)
    shard_update("b_w_qg", p_qg, b_w_qg, m_b_w_qg, v_b_w_qg)
    rep = then(_adamw_replicated(all_b, all_a, all_sink, ln_g, ln_b, b_sinks, m_ln_g, m_ln_b, m_b_sinks, v_ln_g,
                                 v_ln_b, v_b_sinks, last[0]))
    upd["ln_g"], upd["ln_b"], upd["b_sinks"] = list(rep[0:4]), list(rep[4:8]), list(rep[8:12])
    finish_a_in = []
    for k in range(2):
        d_w_in = then(weight_grad(f"a_in_proj_dw_{k}", xb, dh, 2 * dm, halves_spec, part=(k, 2), after=last[0]))
        finish_a_in.append(two_level_scatter(f"scatter_a_in_{k}", (8 + 2 * k, 12 + k), [(d_w_in, cols, (dm // 2, cb))]))
    shard_update("a_w_out", p_out, a_w_out, m_a_w_out, v_a_w_out)
    shard_update("b_w_k", p_k, b_w_k, m_b_w_k, v_b_w_k)
    shard_update("b_w_v", p_v, b_w_v, m_b_w_v, v_b_w_v)
    shard_update("a_w_group", p_g, a_w_group, m_a_w_group, v_a_w_group)
    shard_update("a_scale", p_scale, a_scale, m_a_scale, v_a_scale)
    p_in = list(finish_a_in[0]()) + list(finish_a_in[1]())
    grad_x = then(_input_grad(dh, w_in, dr1, last[0]))
    shard_update("a_w_in", p_in, a_w_in, m_a_w_in, v_a_w_in)

    loss = rep[12].reshape(())
    order = ["ln_g", "ln_b", "a_w_in", "a_w_group", "a_scale", "a_w_out", "b_w_k", "b_w_v", "b_w_qg", "b_sinks",
             "b_w_out"]
    return (loss, grad_x.reshape(x.shape), *[upd[n][0] for n in order], *[upd[n][1] for n in order],
            *[upd[n][2] for n in order], *[upd[n][3] for n in order])
```

```python
import functools

import jax
import jax.numpy as jnp
from jax import lax
from jax.experimental import pallas as pl
from jax.experimental.pallas import tpu as pltpu
from jax.experimental.pallas import tpu_sc as plsc

F32 = jnp.float32
BF16 = jnp.bfloat16
MESH = pl.DeviceIdType.MESH
AXES = ("x", "y", "c")
N_DEV = 8

POOL_WINDOWS = (2, 4, 8, 16)
POOL_HALO = 16
HEAD_DIM = 64
GQA_GROUP = 8
ATTN_BLOCK = 128
ROPE_THETA = 10000.0
LN_EPS = 1e-5
NEG_INF = -1e30
DEPTH = 2
ALPHA = (2 * DEPTH) ** 0.25
ADAM_LR = 0.001
ADAM_B1 = 0.9
ADAM_B2 = 0.999
ADAM_EPS = 1e-08
ADAM_WD = 0.01
ADAM_STEP = 10

LANES = 128
STAT_ROWS = 8


def _tile(n, want):
    t = min(n, want)
    while n % t:
        t //= 2
    return t


def _params(*sem):
    return pltpu.CompilerParams(dimension_semantics=sem)


ANY = pl.BlockSpec(memory_space=pl.ANY)


def _my_pos():
    return lax.axis_index("x"), lax.axis_index("y"), lax.axis_index("c")


def _dev_index(p):
    return 4 * p[0] + 2 * p[1] + p[2]


def _handshake(peers):
    barrier = pltpu.get_barrier_semaphore()
    for peer in peers:
        pl.semaphore_signal(barrier, inc=1, device_id=peer, device_id_type=MESH)
    pl.semaphore_wait(barrier, len(peers))


def _launch_on_sequencer(name, collective_id, body, operands, out_shapes, scratch):
    return pl.kernel(
        body, out_type=tuple(out_shapes), name=name,
        mesh=plsc.ScalarSubcoreMesh(axis_name="sequencer", num_cores=1), scratch_types=scratch,
        compiler_params=pltpu.CompilerParams(collective_id=collective_id),
    )(*operands)


def _gather_weights(name, collective_id, streams, out_shapes):
    n_s = len(streams)
    n_out = len(out_shapes)

    def body(*refs):
        srcs = refs[:n_s]
        outs = refs[n_s:n_s + n_out]
        send_sems, recv_sems, local_sems = refs[n_s + n_out:]
        x, y, c = _my_pos()
        me, sibling = (x, y, c), (x, y, 1 - c)
        x_nbr, y_nbr, diag = (1 - x, y), (x, 1 - y), (1 - x, 1 - y)
        _handshake([sibling, (*x_nbr, c), (*y_nbr, c)])
        south = c == 0
        relay_from = (jnp.where(south, 1 - x, x), jnp.where(south, y, 1 - y))
        relay_to = (jnp.where(south, x, 1 - x), jnp.where(south, 1 - y, y))
        early, late = jnp.where(south, 1, 2), jnp.where(south, 2, 1)

        def copy(s, k, block, to, from_shard=False):
            out_ref = outs[streams[s][1]]
            win = streams[s][2](out_ref, _dev_index(block))
            return pltpu.make_async_remote_copy(
                src_ref=srcs[s] if from_shard else win, dst_ref=win,
                send_sem=send_sems.at[7 * s + k], recv_sem=recv_sems.at[7 * s + k],
                device_id=to, device_id_type=MESH)

        mine = [pltpu.make_async_copy(srcs[s], streams[s][2](outs[streams[s][1]], _dev_index(me)), local_sems.at[s])
                for s in range(n_s)]
        for cp in mine:
            cp.start()
        sent = []
        for s in range(n_s):
            sent += [copy(s, 0, me, sibling, True), copy(s, 1, me, (*x_nbr, c), True), copy(s, 2, me, (*y_nbr, c), True)]
        for cp in sent:
            cp.start()
        for s in range(n_s):
            copy(s, early, (*relay_from, c), me).wait_recv()
            sent += [copy(s, 3, (*relay_from, c), (*relay_to, c)), copy(s, 3 + early, (*relay_from, c), sibling)]
            for cp in sent[-2:]:
                cp.start()
        for s in range(n_s):
            copy(s, late, (*relay_to, c), me).wait_recv()
            sent.append(copy(s, 3 + late, (*relay_to, c), sibling))
            sent[-1].start()
        for s in range(n_s):
            copy(s, 3, (*diag, c), me).wait_recv()
            sent.append(copy(s, 6, (*diag, c), sibling))
            sent[-1].start()
        for s in range(n_s):
            copy(s, 0, sibling, me).wait_recv()
            for k, chip in ((4, x_nbr), (5, y_nbr), (6, diag)):
                copy(s, k, (*chip, 1 - c), me).wait_recv()
        for cp in sent:
            cp.wait_send()
        for cp in mine:
            cp.wait()

    scratch = [pltpu.SemaphoreType.DMA((7 * n_s,)), pltpu.SemaphoreType.DMA((7 * n_s,)),
               pltpu.SemaphoreType.DMA((n_s,))]
    return _launch_on_sequencer(name, collective_id, body, [s[0] for s in streams], out_shapes, scratch)


def _exchange_blocks(name, streams, collective_id):
    n_s = len(streams)

    def body(*refs):
        srcs = refs[:n_s]
        outs = refs[n_s:2 * n_s]
        send_sems, recv_sems, local_sems = refs[2 * n_s:]
        x, y, c = _my_pos()
        me = _dev_index((x, y, c))
        _handshake([(1 - x if k & 4 else x, 1 - y if k & 2 else y, 1 - c if k & 1 else c) for k in range(1, N_DEV)])

        def window(s, dev):
            return srcs[s] if streams[s][1] is None else streams[s][1](srcs[s], dev)

        mine = [pltpu.make_async_copy(window(s, me), outs[s].at[me], local_sems.at[s]) for s in range(n_s)]
        for cp in mine:
            cp.start()
        copies = []
        for k in (2, 4, 6, 3, 5, 7, 1):
            peer = (1 - x if k & 4 else x, 1 - y if k & 2 else y, 1 - c if k & 1 else c)
            for s in range(n_s):
                copies.append(pltpu.make_async_remote_copy(
                    src_ref=window(s, _dev_index(peer)), dst_ref=outs[s].at[me],
                    send_sem=send_sems.at[7 * s + k - 1], recv_sem=recv_sems.at[7 * s + k - 1],
                    device_id=peer, device_id_type=MESH))
        for cp in copies:
            cp.start()
        for cp in copies:
            cp.wait()
        for cp in mine:
            cp.wait()

    out_shapes = [jax.ShapeDtypeStruct((N_DEV,) + tuple(s[2]), s[0].dtype) for s in streams]
    scratch = [pltpu.SemaphoreType.DMA((7 * n_s,)), pltpu.SemaphoreType.DMA((7 * n_s,)),
               pltpu.SemaphoreType.DMA((n_s,))]
    return _launch_on_sequencer(name, collective_id, body, [s[0] for s in streams], out_shapes, scratch)


N_CHIPS = 4


def _sibling_exchange(name, streams, collective_id):
    n_s = len(streams)

    def body(*refs):
        srcs = refs[:n_s]
        outs = refs[n_s:2 * n_s]
        send_sems, recv_sems = refs[2 * n_s:]
        x, y, c = _my_pos()
        sibling = (x, y, 1 - c)
        _handshake([sibling])
        copies = [pltpu.make_async_remote_copy(
            src_ref=streams[s][1](srcs[s], 2 * chip + (1 - c)), dst_ref=outs[s].at[chip],
            send_sem=send_sems.at[N_CHIPS * s + chip], recv_sem=recv_sems.at[N_CHIPS * s + chip],
            device_id=sibling, device_id_type=MESH) for s in range(n_s) for chip in range(N_CHIPS)]
        for cp in copies:
            cp.start()
        for cp in copies:
            cp.wait()

    out_shapes = [jax.ShapeDtypeStruct((N_CHIPS,) + tuple(s[2]), s[0].dtype) for s in streams]
    scratch = [pltpu.SemaphoreType.DMA((N_CHIPS * n_s,)), pltpu.SemaphoreType.DMA((N_CHIPS * n_s,))]
    return _launch_on_sequencer(name, collective_id, body, [s[0] for s in streams], out_shapes, scratch)


def _pair_sum(name, array, from_sibling, my_core, after):
    _, rows, cols = from_sibling.shape
    tr = _tile(rows, 2048)

    def body(core_ref, own_ref, sib_ref, after_ref, o_ref):
        del core_ref, after_ref
        o_ref[...] = (own_ref[...].astype(F32) + sib_ref[...].astype(F32)).astype(o_ref.dtype)

    staged_spec = pl.BlockSpec((None, tr, cols), lambda k, i, core: (k, i, 0))
    return pl.pallas_call(
        body, name=name, out_shape=jax.ShapeDtypeStruct(from_sibling.shape, array.dtype),
        grid_spec=pltpu.PrefetchScalarGridSpec(
            num_scalar_prefetch=1, grid=(N_CHIPS, rows // tr),
            in_specs=[pl.BlockSpec((tr, cols), lambda k, i, core: (i, 2 * k + core[0])), staged_spec, ANY],
            out_specs=staged_spec),
        compiler_params=_params("parallel", "parallel"),
    )(my_core, array, from_sibling, after)


def _chip_exchange(name, pair_sums, collective_id):
    n_s = len(pair_sums)

    def body(*refs):
        srcs = refs[:n_s]
        outs = refs[n_s:2 * n_s]
        send_sems, recv_sems, local_sems = refs[2 * n_s:]
        x, y, c = _my_pos()
        my_chip = 2 * x + y
        chips = [(1 - x, y), (x, 1 - y), (1 - x, 1 - y)]
        _handshake([(*chip, c) for chip in chips])
        mine = [pltpu.make_async_copy(srcs[s].at[my_chip], outs[s].at[my_chip], local_sems.at[s]) for s in range(n_s)]
        copies = [pltpu.make_async_remote_copy(
            src_ref=srcs[s].at[2 * chip[0] + chip[1]], dst_ref=outs[s].at[my_chip],
            send_sem=send_sems.at[3 * s + j], recv_sem=recv_sems.at[3 * s + j],
            device_id=(*chip, c), device_id_type=MESH) for s in range(n_s) for j, chip in enumerate(chips)]
        for cp in mine + copies:
            cp.start()
        for cp in copies:
            cp.wait()
        for cp in mine:
            cp.wait()

    out_shapes = [jax.ShapeDtypeStruct(p.shape, p.dtype) for p in pair_sums]
    scratch = [pltpu.SemaphoreType.DMA((3 * n_s,)), pltpu.SemaphoreType.DMA((3 * n_s,)),
               pltpu.SemaphoreType.DMA((n_s,))]
    return _launch_on_sequencer(name, collective_id, body, list(pair_sums), out_shapes, scratch)


NN = (((1,), (0,)), ((), ()))
NT = (((1,), (1,)), ((), ()))
TN = (((0,), (0,)), ((), ()))


def _mm(name, a, b, *, dims, grid, a_spec, b_spec, out_shape, out_spec, nk=1,
        add=None, add_spec=None, add_scale=1.0, epilogue=None, extras=(), extra_specs=()):
    n_extra = len(extras)
    has_add = add is not None

    def body(*refs):
        a_ref, b_ref = refs[:2]
        pos = 2
        add_ref = None
        if has_add:
            add_ref = refs[pos]
            pos += 1
        extra_refs = refs[pos:pos + n_extra]
        o_ref = refs[pos + n_extra]
        acc_ref = refs[pos + n_extra + 1] if nk > 1 else None

        def finish(val):
            if has_add:
                val = val + add_scale * add_ref[...]
            if epilogue is not None:
                val = epilogue(val, *extra_refs)
            o_ref[...] = val.astype(o_ref.dtype)

        part = lax.dot_general(a_ref[...].astype(BF16), b_ref[...].astype(BF16), dims,
                               preferred_element_type=F32)
        if nk == 1:
            finish(part)
        else:
            k = pl.program_id(2)

            @pl.when(k == 0)
            def _():
                acc_ref[...] = part

            @pl.when(jnp.logical_and(k > 0, k < nk - 1))
            def _():
                acc_ref[...] += part

            @pl.when(k == nk - 1)
            def _():
                finish(acc_ref[...] + part)

    in_specs = [a_spec, b_spec] + ([add_spec] if has_add else []) + list(extra_specs)
    operands = [a, b] + ([add] if has_add else []) + list(extras)
    scratch = [pltpu.VMEM(out_spec.block_shape, F32)] if nk > 1 else []
    sem = ("parallel", "parallel") + (("arbitrary",) if nk > 1 else ())
    return pl.pallas_call(
        body, name=name, grid=grid, out_shape=out_shape,
        in_specs=in_specs, out_specs=out_spec, scratch_shapes=scratch,
        compiler_params=_params(*sem),
    )(*operands)


def _input_grad(dh, w_in, dr1, after):
    _, seq, dm = dh.shape
    bm, bn = _tile(seq, 1024), _tile(dm, 512)

    def body(dh_ref, w_ref, dr_ref, after_ref, o_ref):
        del after_ref
        o_ref[...] = (lax.dot_general(dh_ref[0], w_ref[:, :dm], NT, preferred_element_type=F32)
                      + lax.dot_general(dh_ref[1], w_ref[:, dm:], NT, preferred_element_type=F32)
                      + ALPHA * dr_ref[...])

    tile_spec = pl.BlockSpec((bm, bn), lambda i, j: (i, j))
    return pl.pallas_call(
        body, name="a_in_proj_dx", grid=(seq // bm, dm // bn),
        out_shape=jax.ShapeDtypeStruct((seq, dm), F32),
        in_specs=[pl.BlockSpec((2, bm, dm), lambda i, j: (0, i, 0)), pl.BlockSpec((bn, 2 * dm), lambda i, j: (j, 0)),
                  tile_spec, ANY],
        out_specs=tile_spec,
        compiler_params=_params("parallel", "parallel"),
    )(dh, w_in, dr1, after)


def _cast_bf16(name, a):
    rows, cols = a.shape
    tr = _tile(rows, 512)

    def body(a_ref, o_ref):
        o_ref[...] = a_ref[...].astype(BF16)

    return pl.pallas_call(
        body, name=name, grid=(rows // tr,),
        out_shape=jax.ShapeDtypeStruct(a.shape, BF16),
        in_specs=[pl.BlockSpec((tr, cols), lambda i: (i, 0))],
        out_specs=pl.BlockSpec((tr, cols), lambda i: (i, 0)),
        compiler_params=_params("parallel"),
    )(a)


def _rope_tables(seq):
    inv_freq = ROPE_THETA ** (-jnp.arange(0, HEAD_DIM, 2, dtype=F32) / HEAD_DIM)
    ang = jnp.arange(seq, dtype=F32)[:, None] * inv_freq[None, :]
    cos, sin = jnp.cos(ang), jnp.sin(ang)
    cos, sin = (jnp.concatenate([t, t, t, t], axis=-1) for t in (cos, sin))
    first_half = (jnp.arange(LANES) % HEAD_DIM < HEAD_DIM // 2)[None, :]
    return cos, jnp.where(first_half, -sin, 0.0), jnp.where(first_half, 0.0, sin)


def _rot(t, sin_a, sin_b):
    return pltpu.roll(t, LANES - HEAD_DIM // 2, 1) * sin_a + pltpu.roll(t, HEAD_DIM // 2, 1) * sin_b


def _rope(t, cos, sin_a, sin_b):
    return t * cos + _rot(t, sin_a, sin_b)


def _rope_transposed(dy, cos, sin_a, sin_b):
    return dy * cos - _rot(dy, sin_a, sin_b)


def _silu_parts(z):
    sig = jax.nn.sigmoid(z)
    return z * sig, sig * (1.0 + z * (1.0 - sig))


def _layer_norm_stats(r):
    mu = jnp.mean(r, axis=-1, keepdims=True)
    d = r - mu
    var = jnp.mean(d * d, axis=-1, keepdims=True)
    rstd = lax.rsqrt(var + LN_EPS)
    return d * rstd, rstd


def _layer_norm_backward(dout, xhat, rstd, gain):
    dxh = dout * gain
    m1 = jnp.mean(dxh, axis=-1, keepdims=True)
    m2 = jnp.mean(dxh * xhat, axis=-1, keepdims=True)
    return rstd * (dxh - m1 - xhat * m2)


def _col_sum(v):
    return jnp.sum(v, axis=0, keepdims=True)


ROW_PART = 128


def _row_parts(tile):
    part = min(tile, ROW_PART)
    return [slice(r, r + part) for r in range(0, tile, part)]


def _pool_forward(xb, w_in, wg, scale):
    seq, dm = xb.shape
    n_g = len(POOL_WINDOWS)
    gd = dm // n_g
    tile = _tile(seq, 1024)
    halo_blocks = tile // POOL_HALO

    def body(x_ref, xp_ref, wu_ref, wz_ref, wg_ref, sc_ref, y_ref, p_ref, mx_ref, z_ref):
        i, g = pl.program_id(0), pl.program_id(1)
        u = jnp.dot(x_ref[...], wu_ref[...], preferred_element_type=F32)
        z = jnp.dot(x_ref[...], wz_ref[...], preferred_element_type=F32)
        prev = jnp.where(i > 0, jnp.dot(xp_ref[...], wu_ref[...], preferred_element_type=F32), 0.0)
        s = jnp.concatenate([prev, u], axis=0)
        sums, sh = [], 1
        while sh < POOL_WINDOWS[-1]:
            s = s + pltpu.roll(s, sh, 0)
            sums.append(s)
            sh *= 2
        win = sums[-1]
        for k in range(n_g - 2, -1, -1):
            win = jnp.where(g == k, sums[k], win)
        row = i * tile + lax.broadcasted_iota(jnp.int32, (tile, 1), 0)
        window = jnp.left_shift(2, g).astype(F32)
        p = win[POOL_HALO:, :] * (1.0 / jnp.minimum((row + 1).astype(F32), window)) - u
        pb = p.astype(BF16)
        mx = jnp.dot(pb, wg_ref[...], preferred_element_type=F32)
        y_ref[...] = (mx * sc_ref[...] * (z * jax.nn.sigmoid(z))).astype(BF16)
        p_ref[...] = pb
        mx_ref[...] = mx
        z_ref[...] = z

    out_spec = pl.BlockSpec((tile, gd), lambda i, g: (i, g))
    return pl.pallas_call(
        body, name="pool_fwd", grid=(seq // tile, n_g),
        out_shape=(jax.ShapeDtypeStruct((seq, dm), BF16), jax.ShapeDtypeStruct((seq, dm), BF16),
                   jax.ShapeDtypeStruct((seq, dm), F32), jax.ShapeDtypeStruct((seq, dm), F32)),
        in_specs=[pl.BlockSpec((tile, dm), lambda i, g: (i, 0)),
                  pl.BlockSpec((POOL_HALO, dm), lambda i, g: (jnp.maximum(i * halo_blocks - 1, 0), 0)),
                  pl.BlockSpec((dm, gd), lambda i, g: (0, g)),
                  pl.BlockSpec((dm, gd), lambda i, g: (0, n_g + g)),
                  pl.BlockSpec((None, gd, gd), lambda i, g: (g, 0, 0)),
                  pl.BlockSpec((1, gd), lambda i, g: (0, g))],
        out_specs=(out_spec, out_spec, out_spec, out_spec),
        compiler_params=_params("parallel", "parallel"),
    )(xb, xb, w_in, w_in, wg, scale)


def _pool_mid_backward(dy, mx, z, p, wg, scale, after):
    seq, dm = dy.shape
    gd = dm // len(POOL_WINDOWS)
    tile = _tile(seq, 256)
    n_i = seq // tile

    def body(dy_ref, mx_ref, z_ref, p_ref, wg_ref, sc_ref, after_ref, dh_ref, dwg_ref, st_ref, dwg_acc, carry):
        del after_ref
        i = pl.program_id(0)
        ti = n_i - 1 - i

        @pl.when(i == 0)
        def _():
            dwg_acc[...] = jnp.zeros_like(dwg_acc)
            carry[...] = jnp.zeros_like(carry)
            st_ref[...] = jnp.zeros_like(st_ref)

        row = ti * tile + lax.broadcasted_iota(jnp.int32, (tile, 1), 0)
        count = (row + 1).astype(F32)
        for g, w in enumerate(POOL_WINDOWS):
            cs = slice(g * gd, (g + 1) * gd)
            z = z_ref[:, cs]
            sz, dsz = _silu_parts(z)
            dyg = dy_ref[:, cs]
            mxg = mx_ref[:, cs]
            sc = sc_ref[:, cs]
            t1 = dyg * sz
            st_ref[0:1, cs] += _col_sum(t1 * mxg)
            dh_ref[1, :, cs] = (dyg * (mxg * sc) * dsz).astype(BF16)
            dmx = (t1 * sc).astype(BF16)
            dwg_acc[g] += lax.dot_general(p_ref[:, cs], dmx, TN, preferred_element_type=F32)
            dp = lax.dot_general(dmx, wg_ref[g], NT, preferred_element_type=F32)
            e = dp * (1.0 / jnp.minimum(count, float(w)))
            s = jnp.concatenate([e, carry[:, cs]], axis=0)
            n = tile + POOL_HALO
            sh = 1
            while sh < w:
                s = s + pltpu.roll(s, n - sh, 0)
                sh *= 2
            dh_ref[0, :, cs] = (s[:tile, :] - dp).astype(BF16)
            carry[:, cs] = e[:POOL_HALO, :]

        @pl.when(i == n_i - 1)
        def _():
            dwg_ref[...] = dwg_acc[...].astype(BF16)

    row_spec = pl.BlockSpec((tile, dm), lambda i: (n_i - 1 - i, 0))
    return pl.pallas_call(
        body, name="pool_mid_bwd", grid=(n_i,),
        out_shape=(jax.ShapeDtypeStruct((2, seq, dm), BF16), jax.ShapeDtypeStruct(wg.shape, BF16),
                   jax.ShapeDtypeStruct((STAT_ROWS, dm), F32)),
        in_specs=[row_spec, row_spec, row_spec, row_spec,
                  pl.BlockSpec(wg.shape, lambda i: (0, 0, 0)),
                  pl.BlockSpec((1, dm), lambda i: (0, 0)), ANY],
        out_specs=(pl.BlockSpec((2, tile, dm), lambda i: (0, n_i - 1 - i, 0)),
                   pl.BlockSpec(wg.shape, lambda i: (0, 0, 0)),
                   pl.BlockSpec((STAT_ROWS, dm), lambda i: (0, 0))),
        scratch_shapes=[pltpu.VMEM(wg.shape, F32), pltpu.VMEM((POOL_HALO, dm), F32)],
        compiler_params=_params("arbitrary"),
    )(dy, mx, z, p, wg, scale, after)


def _out_proj_norm(y, w, x, gain, bias):
    seq, dm = x.shape
    tile = _tile(seq, 512)

    def body(y_ref, w_ref, x_ref, g_ref, b_ref, xhat_ref, rstd_ref, xb_ref):
        for rows in _row_parts(tile):
            o = jnp.dot(y_ref[rows, :], w_ref[...], preferred_element_type=F32)
            xhat, rstd = _layer_norm_stats(ALPHA * x_ref[rows, :] + o)
            xhat_ref[rows, :] = xhat
            rstd_ref[rows, :] = rstd
            xb_ref[rows, :] = (xhat * g_ref[...] + b_ref[...]).astype(BF16)

    row_spec = pl.BlockSpec((tile, dm), lambda i: (i, 0))
    vec_spec = pl.BlockSpec((1, dm), lambda i: (0, 0))
    return pl.pallas_call(
        body, name="out_proj_norm_a", grid=(seq // tile,),
        out_shape=(jax.ShapeDtypeStruct((seq, dm), F32), jax.ShapeDtypeStruct((seq, 1), F32),
                   jax.ShapeDtypeStruct((seq, dm), BF16)),
        in_specs=[row_spec, pl.BlockSpec(w.shape, lambda i: (0, 0), pipeline_mode=pl.Buffered(1)), row_spec, vec_spec,
                  vec_spec],
        out_specs=(row_spec, pl.BlockSpec((tile, 1), lambda i: (i, 0)), row_spec),
        compiler_params=_params("parallel"),
    )(y, w, x, gain, bias)


def _kv_proj(xb, wkv, tables):
    seq, dm = xb.shape
    kvw = wkv.shape[1] // 2
    n_kv = kvw // HEAD_DIM
    tile = _tile(seq, 1024)

    def body(x_ref, w_ref, cos_ref, sa_ref, sb_ref, kd_ref, vd_ref, kt_ref, vt_ref):
        kv = jnp.dot(x_ref[...], w_ref[...], preferred_element_type=F32)
        low = lax.broadcasted_iota(jnp.int32, (1, LANES), 1) < HEAD_DIM
        cos, sa, sb = cos_ref[...], sa_ref[...], sb_ref[...]

        def put(pair, h, nat_ref, t_ref):
            swapped = pltpu.roll(pair, HEAD_DIM, 1)
            for head, dup in ((h, jnp.where(low, pair, swapped)), (h + 1, jnp.where(low, swapped, pair))):
                nat_ref[head] = dup.astype(BF16)
                t_ref[head] = dup.T.astype(BF16)

        for j in range(kvw // LANES):
            put(_rope(kv[:, j * LANES:(j + 1) * LANES], cos, sa, sb), 2 * j, kd_ref, kt_ref)
            put(kv[:, kvw + j * LANES:kvw + (j + 1) * LANES], 2 * j, vd_ref, vt_ref)

    tab_spec = pl.BlockSpec((tile, LANES), lambda i: (i, 0))
    dup_spec = pl.BlockSpec((n_kv, tile, LANES), lambda i: (0, i, 0))
    dup_shape = jax.ShapeDtypeStruct((n_kv, seq, LANES), BF16)
    t_spec = pl.BlockSpec((n_kv, LANES, tile), lambda i: (0, 0, i))
    t_shape = jax.ShapeDtypeStruct((n_kv, LANES, seq), BF16)
    return pl.pallas_call(
        body, name="kv_proj", grid=(seq // tile,),
        out_shape=(dup_shape, dup_shape, t_shape, t_shape),
        in_specs=[pl.BlockSpec((tile, dm), lambda i: (i, 0)), pl.BlockSpec(wkv.shape, lambda i: (0, 0)),
                  tab_spec, tab_spec, tab_spec],
        out_specs=(dup_spec, dup_spec, t_spec, t_spec),
        compiler_params=_params("parallel"),
    )(xb, wkv, *tables)


ATTN_STEP_BLOCKS = 16


def _head_queries(q_ref, rows, low):
    parts = []
    for j in range(GQA_GROUP // 2):
        q2 = q_ref[rows, j * LANES:(j + 1) * LANES]
        parts += [jnp.where(low, q2, 0), jnp.where(low, 0, q2)]
    return parts


def _key_window(prev_ref, cur_ref, b, axis):
    def block(i):
        sl = slice(i * ATTN_BLOCK, (i + 1) * ATTN_BLOCK)
        return cur_ref[sl, :] if axis == 0 else cur_ref[:, sl]
    return jnp.concatenate([prev_ref[...] if b == 0 else block(b - 1), block(b)], axis=axis)


def _mask_bias(n):
    key = lax.broadcasted_iota(jnp.int32, (2 * ATTN_BLOCK, ATTN_BLOCK), 0)
    qry = lax.broadcasted_iota(jnp.int32, (2 * ATTN_BLOCK, ATTN_BLOCK), 1)
    valid = (key > qry) & (key <= qry + ATTN_BLOCK) & ((key >= ATTN_BLOCK) | (n > 0))
    return jnp.where(valid, 0.0, NEG_INF)


def _head_probs_transposed(kcat, qm, bias, sink):
    st = lax.dot_general(kcat, qm, NT, preferred_element_type=F32) + bias
    m = jnp.maximum(jnp.max(st, axis=0, keepdims=True), sink)
    e = jnp.exp(st - m)
    e_sink = jnp.exp(sink - m)
    inv = 1.0 / (jnp.sum(e, axis=0, keepdims=True) + e_sink)
    return e * inv, e_sink * inv


def _probs_transposed(n, kh, kcat, q_all, sink_ref):
    st = lax.dot_general(kcat, q_all, NT, preferred_element_type=F32)
    st = st + jnp.tile(_mask_bias(n), (1, GQA_GROUP))
    sink = jnp.concatenate([jnp.full((1, ATTN_BLOCK), sink_ref[0, kh * GQA_GROUP + h], F32)
                            for h in range(GQA_GROUP)], axis=1)
    m = jnp.maximum(jnp.max(st, axis=0, keepdims=True), sink)
    e = jnp.exp(st - m)
    e_sink = jnp.exp(sink - m)
    inv = 1.0 / (jnp.sum(e, axis=0, keepdims=True) + e_sink)
    return e * inv, e_sink * inv


def _attn_specs(n_width, qb):
    rows = qb * ATTN_BLOCK
    before = lambda n: jnp.maximum(n * qb - 1, 0)
    q_spec = pl.BlockSpec((rows, n_width), lambda kh, n: (n, kh))
    cur = pl.BlockSpec((None, rows, LANES), lambda kh, n: (kh, n, 0))
    prev = pl.BlockSpec((None, ATTN_BLOCK, LANES), lambda kh, n: (kh, before(n), 0))
    cur_t = pl.BlockSpec((None, LANES, rows), lambda kh, n: (kh, 0, n))
    prev_t = pl.BlockSpec((None, LANES, ATTN_BLOCK), lambda kh, n: (kh, 0, before(n)))
    return q_spec, cur, prev, cur_t, prev_t


def _pair_product_transposed(mat_t, rhs, j, low_rows):
    head_a = slice(2 * j * ATTN_BLOCK, (2 * j + 1) * ATTN_BLOCK)
    head_b = slice((2 * j + 1) * ATTN_BLOCK, (2 * j + 2) * ATTN_BLOCK)
    out_t = (jnp.dot(jnp.where(low_rows, mat_t, 0), rhs[:, head_a], preferred_element_type=F32)
             + jnp.dot(jnp.where(low_rows, 0, mat_t), rhs[:, head_b], preferred_element_type=F32))
    return out_t.T


def _attn_forward(qs, kd, vt, zb, sinks):
    seq, dm = qs.shape
    n_kv = kd.shape[0]
    gw = GQA_GROUP * HEAD_DIM

    qb = ATTN_STEP_BLOCKS if (seq // ATTN_BLOCK) % ATTN_STEP_BLOCKS == 0 else 1

    def body(q_ref, kp_ref, kc_ref, vtp_ref, vtc_ref, z_ref, sink_ref, att_ref, yb_ref):
        kh, n = pl.program_id(0), pl.program_id(1)
        low = lax.broadcasted_iota(jnp.int32, (1, LANES), 1) < HEAD_DIM
        low_rows = lax.broadcasted_iota(jnp.int32, (LANES, 1), 0) < HEAD_DIM
        for b in range(qb):
            rows = slice(b * ATTN_BLOCK, (b + 1) * ATTN_BLOCK)
            kcat = _key_window(kp_ref, kc_ref, b, 0)
            vt = _key_window(vtp_ref, vtc_ref, b, 1)
            bias = _mask_bias(n * qb + b)
            pt = jnp.concatenate(
                [_head_probs_transposed(kcat, qm, bias, sink_ref[0, kh * GQA_GROUP + h])[0].astype(BF16)
                 for h, qm in enumerate(_head_queries(q_ref, rows, low))], axis=1)
            for j in range(GQA_GROUP // 2):
                cs = slice(j * LANES, (j + 1) * LANES)
                o2 = _pair_product_transposed(vt, pt, j, low_rows)
                att_ref[rows, cs] = o2
                z = z_ref[rows, cs]
                yb_ref[rows, cs] = (o2 * (z * jax.nn.sigmoid(z))).astype(BF16)

    q_spec, cur, prev, cur_t, prev_t = _attn_specs(gw, qb)
    return pl.pallas_call(
        body, name="attn_fwd", grid=(n_kv, seq // (qb * ATTN_BLOCK)),
        out_shape=(jax.ShapeDtypeStruct((seq, dm), F32), jax.ShapeDtypeStruct((seq, dm), BF16)),
        in_specs=[q_spec, prev, cur, prev_t, cur_t, q_spec, pl.BlockSpec(memory_space=pltpu.SMEM)],
        out_specs=(q_spec, q_spec),
        compiler_params=_params("parallel", "parallel"),
    )(qs, kd, kd, vt, vt, zb, sinks)


def _attn_backward(qs, kd, vd, kt, zb, att, dyb, sinks, tables, after):
    seq, dm = qs.shape
    n_kv = kd.shape[0]
    gw = GQA_GROUP * HEAD_DIM
    n_blocks = seq // ATTN_BLOCK
    qb = ATTN_STEP_BLOCKS if n_blocks % ATTN_STEP_BLOCKS == 0 else 1

    def body(q_ref, kp_ref, kc_ref, vp_ref, vc_ref, ktp_ref, ktc_ref, z_ref, att_ref, dyb_ref, sink_ref,
             cos_ref, sa_ref, sb_ref, after_ref, dh_ref, dk_ref, dv_ref, ds_ref):
        del after_ref
        kh, n = pl.program_id(0), pl.program_id(1)

        @pl.when(n == 0)
        def _():
            dk_ref[...] = jnp.zeros_like(dk_ref)
            dv_ref[...] = jnp.zeros_like(dv_ref)

        @pl.when(jnp.logical_and(n == 0, kh == 0))
        def _():
            ds_ref[...] = jnp.zeros_like(ds_ref)

        low = lax.broadcasted_iota(jnp.int32, (1, LANES), 1) < HEAD_DIM
        low_rows = lax.broadcasted_iota(jnp.int32, (LANES, 1), 0) < HEAD_DIM
        head_lane = lax.broadcasted_iota(jnp.int32, (1, LANES), 1)
        dsink = jnp.zeros((1, LANES), F32)
        for b in range(qb):
            rows = slice(b * ATTN_BLOCK, (b + 1) * ATTN_BLOCK)
            kcat = _key_window(kp_ref, kc_ref, b, 0)
            vcat = _key_window(vp_ref, vc_ref, b, 0)
            kt = _key_window(ktp_ref, ktc_ref, b, 1)
            cos, sa, sb = cos_ref[rows, :], sa_ref[rows, :], sb_ref[rows, :]
            q_all = jnp.concatenate(_head_queries(q_ref, rows, low), axis=0)
            d_parts = []
            for j in range(GQA_GROUP // 2):
                cs = slice(j * LANES, (j + 1) * LANES)
                sz, dsz = _silu_parts(z_ref[rows, cs])
                dy2 = dyb_ref[rows, cs]
                dh_ref[1, rows, cs] = (dy2 * att_ref[rows, cs] * dsz).astype(BF16)
                datt = (dy2 * sz).astype(BF16)
                d_parts += [jnp.where(low, datt, 0), jnp.where(low, 0, datt)]
            d_all = jnp.concatenate(d_parts, axis=0)
            probs_t, sink_p = _probs_transposed(n * qb + b, kh, kcat, q_all, sink_ref)
            dprobs_t = lax.dot_general(vcat, d_all, NT, preferred_element_type=F32)
            row_dot = jnp.sum(probs_t * dprobs_t, axis=0, keepdims=True)
            ds_t = (probs_t * (dprobs_t - row_dot)).astype(BF16)
            dk = jnp.dot(ds_t, q_all, preferred_element_type=F32)
            dv = jnp.dot(probs_t.astype(BF16), d_all, preferred_element_type=F32)
            for j in range(GQA_GROUP // 2):
                dq2 = _pair_product_transposed(kt, ds_t, j, low_rows)
                dh_ref[0, rows, j * LANES:(j + 1) * LANES] = (
                    _rope_transposed(dq2, cos, sa, sb) * 0.125).astype(BF16)
            sink_dot = sink_p * row_dot
            for h in range(GQA_GROUP):
                part = jnp.sum(sink_dot[:, h * ATTN_BLOCK:(h + 1) * ATTN_BLOCK], axis=1, keepdims=True)
                dsink = dsink - jnp.where(head_lane == kh * GQA_GROUP + h, part, 0.0)

            def add_window(dk=dk, dv=dv, b=b):
                start = pl.multiple_of((n * qb + b - 1) * ATTN_BLOCK, ATTN_BLOCK)
                dk_ref[pl.ds(start, 2 * ATTN_BLOCK), :] += dk
                dv_ref[pl.ds(start, 2 * ATTN_BLOCK), :] += dv

            if b > 0:
                add_window()
            else:
                pl.when(n > 0)(add_window)

                @pl.when(n == 0)
                def _(dk=dk, dv=dv):
                    dk_ref[pl.ds(0, ATTN_BLOCK), :] += dk[ATTN_BLOCK:, :]
                    dv_ref[pl.ds(0, ATTN_BLOCK), :] += dv[ATTN_BLOCK:, :]
        ds_ref[0:1, :] += dsink

    q_spec, cur, prev, cur_t, prev_t = _attn_specs(gw, qb)
    tab_spec = pl.BlockSpec((qb * ATTN_BLOCK, LANES), lambda kh, n: (n, 0))
    acc_spec = pl.BlockSpec((None, seq, LANES), lambda kh, n: (kh, 0, 0))
    acc_shape = jax.ShapeDtypeStruct((n_kv, seq, LANES), F32)
    return pl.pallas_call(
        body, name="attn_bwd", grid=(n_kv, n_blocks // qb),
        out_shape=(jax.ShapeDtypeStruct((2, seq, dm), BF16), acc_shape, acc_shape,
                   jax.ShapeDtypeStruct((STAT_ROWS, LANES), F32)),
        in_specs=[q_spec, prev, cur, prev, cur, prev_t, cur_t, q_spec, q_spec, q_spec,
                  pl.BlockSpec(memory_space=pltpu.SMEM), tab_spec, tab_spec, tab_spec, ANY],
        out_specs=(pl.BlockSpec((2, qb * ATTN_BLOCK, gw), lambda kh, n: (0, n, kh)), acc_spec, acc_spec,
                   pl.BlockSpec((STAT_ROWS, LANES), lambda kh, n: (0, 0))),
        compiler_params=_params("arbitrary", "arbitrary"),
    )(qs, kd, kd, vd, vd, kt, kt, zb, att, dyb, sinks, *tables, after)


def _kv_grad_fold(dk, dv, tables):
    n_kv, seq, _ = dk.shape
    kvw = n_kv * HEAD_DIM
    tile = _tile(seq, 512)

    def body(dk_ref, dv_ref, cos_ref, sa_ref, sb_ref, o_ref):
        low = lax.broadcasted_iota(jnp.int32, (1, LANES), 1) < HEAD_DIM
        cos, sa, sb = cos_ref[...], sa_ref[...], sb_ref[...]

        def folded(ref, h):
            t = ref[h]
            return t + pltpu.roll(t, HEAD_DIM, 1)

        for j in range(n_kv // 2):
            ka = _rope_transposed(folded(dk_ref, 2 * j), cos, sa, sb)
            kb = _rope_transposed(folded(dk_ref, 2 * j + 1), cos, sa, sb)
            o_ref[:, j * LANES:(j + 1) * LANES] = jnp.where(low, ka, kb).astype(BF16)
            o_ref[:, kvw + j * LANES:kvw + (j + 1) * LANES] = jnp.where(
                low, folded(dv_ref, 2 * j), folded(dv_ref, 2 * j + 1)).astype(BF16)

    tab_spec = pl.BlockSpec((tile, LANES), lambda i: (i, 0))
    in_spec = pl.BlockSpec((n_kv, tile, LANES), lambda i: (0, i, 0))
    return pl.pallas_call(
        body, name="kv_grad_fold", grid=(seq // tile,),
        out_shape=jax.ShapeDtypeStruct((seq, 2 * kvw), BF16),
        in_specs=[in_spec, in_spec, tab_spec, tab_spec, tab_spec],
        out_specs=pl.BlockSpec((tile, 2 * kvw), lambda i: (i, 0)),
        compiler_params=_params("parallel"),
    )(dk, dv, *tables)


def _out_proj_norm_loss(yb, w, xhat1, gain0, bias0, gain1, bias1, target):
    seq, dm = xhat1.shape
    tile = _tile(seq, 512)

    def body(y_ref, w_ref, xh1_ref, g0_ref, b0_ref, g1_ref, b1_ref, t_ref, dr_ref, drb_ref, st_ref):
        i = pl.program_id(0)

        @pl.when(i == 0)
        def _():
            st_ref[...] = jnp.zeros_like(st_ref)

        parts = _row_parts(tile)
        product = lambda rows: jnp.dot(y_ref[rows, :], w_ref[...], preferred_element_type=F32)
        ahead = product(parts[0])
        for k, rows in enumerate(parts):
            ob = ahead
            if k + 1 < len(parts):
                ahead = product(parts[k + 1])
            x1 = xh1_ref[rows, :] * g0_ref[...] + b0_ref[...]
            xhat, rstd = _layer_norm_stats(ALPHA * x1 + ob)
            err = xhat * g1_ref[...] + b1_ref[...] - t_ref[rows, :]
            dout = err * (1.0 / dm)
            dr = _layer_norm_backward(dout, xhat, rstd, g1_ref[...])
            dr_ref[rows, :] = dr
            drb_ref[rows, :] = dr.astype(BF16)
            st_ref[0:1, :] += _col_sum(dout * xhat)
            st_ref[1:2, :] += _col_sum(dout)
            st_ref[2:3, :] += _col_sum(err * err)

    row_spec = pl.BlockSpec((tile, dm), lambda i: (i, 0))
    vec_spec = pl.BlockSpec((1, dm), lambda i: (0, 0))
    return pl.pallas_call(
        body, name="out_proj_norm_loss_b", grid=(seq // tile,),
        out_shape=(jax.ShapeDtypeStruct((seq, dm), F32), jax.ShapeDtypeStruct((seq, dm), BF16),
                   jax.ShapeDtypeStruct((STAT_ROWS, dm), F32)),
        in_specs=[row_spec, pl.BlockSpec(w.shape, lambda i: (0, 0), pipeline_mode=pl.Buffered(1)), row_spec, vec_spec,
                  vec_spec, vec_spec,
                  vec_spec, row_spec],
        out_specs=(row_spec, row_spec, pl.BlockSpec((STAT_ROWS, dm), lambda i: (0, 0))),
        compiler_params=_params("arbitrary"),
    )(yb, w, xhat1, gain0, bias0, gain1, bias1, target)


def _stream_grad_norm_backward(dhq, wqg, dkv, wkv, dr2, xhat1, rstd1, gain0, after):
    seq, dm = dr2.shape
    tile = _tile(seq, 256)

    def body(dh_ref, wqg_ref, dkv_ref, wkv_ref, dr2_ref, xh_ref, rstd_ref, g_ref, after_ref, dr_ref, drb_ref, st_ref):
        del after_ref

        @pl.when(pl.program_id(0) == 0)
        def _():
            st_ref[...] = jnp.zeros_like(st_ref)

        dx1 = (lax.dot_general(dh_ref[0], wqg_ref[:, :dm], NT, preferred_element_type=F32)
               + lax.dot_general(dh_ref[1], wqg_ref[:, dm:], NT, preferred_element_type=F32)
               + lax.dot_general(dkv_ref[...], wkv_ref[...], NT, preferred_element_type=F32)
               + ALPHA * dr2_ref[...])
        xhat = xh_ref[...]
        dr = _layer_norm_backward(dx1, xhat, rstd_ref[...], g_ref[...])
        dr_ref[...] = dr
        drb_ref[...] = dr.astype(BF16)
        st_ref[0:1, :] += _col_sum(dx1 * xhat)
        st_ref[1:2, :] += _col_sum(dx1)

    row_spec = pl.BlockSpec((tile, dm), lambda i: (i, 0))
    resident = pl.Buffered(1)
    return pl.pallas_call(
        body, name="stream_grad_norm_bwd", grid=(seq // tile,),
        out_shape=(jax.ShapeDtypeStruct((seq, dm), F32), jax.ShapeDtypeStruct((seq, dm), BF16),
                   jax.ShapeDtypeStruct((STAT_ROWS, dm), F32)),
        in_specs=[pl.BlockSpec((2, tile, dm), lambda i: (0, i, 0)),
                  pl.BlockSpec(wqg.shape, lambda i: (0, 0), pipeline_mode=resident),
                  pl.BlockSpec((tile, dkv.shape[1]), lambda i: (i, 0)),
                  pl.BlockSpec(wkv.shape, lambda i: (0, 0), pipeline_mode=resident),
                  row_spec, row_spec, pl.BlockSpec((tile, 1), lambda i: (i, 0)),
                  pl.BlockSpec((1, dm), lambda i: (0, 0)), ANY],
        out_specs=(row_spec, row_spec, pl.BlockSpec((STAT_ROWS, dm), lambda i: (0, 0))),
        compiler_params=_params("arbitrary"),
    )(dhq, wqg, dkv, wkv, dr2, xhat1, rstd1, gain0, after)


def _adamw_math(w, g, m, v):
    m = ADAM_B1 * m + (1.0 - ADAM_B1) * g
    v = ADAM_B2 * v + (1.0 - ADAM_B2) * (g * g)
    m_hat = m / (1.0 - ADAM_B1 ** ADAM_STEP)
    v_hat = v / (1.0 - ADAM_B2 ** ADAM_STEP)
    delta = -ADAM_LR * (m_hat / (jnp.sqrt(v_hat) + ADAM_EPS) + ADAM_WD * w)
    return delta, m, v


def _sum_devices(ref):
    total = ref[0].astype(F32)
    for d in range(1, ref.shape[0]):
        total = total + ref[d].astype(F32)
    return total


def _adamw_shard(name, parts, w, m, v, after):
    rows, cols = w.shape
    n_parts = len(parts)
    part_rows = rows // n_parts
    tr = _tile(part_rows, max(8, (1 << 18) // cols)) if part_rows >= 8 else part_rows
    per_part = part_rows // tr

    def body(*refs):
        p_refs = refs[:n_parts]
        w_ref, m_ref, v_ref, _, g_out, d_out, m_out, v_out = refs[n_parts:]
        g = _sum_devices(p_refs[0])
        for k in range(1, n_parts):
            g = jnp.where(pl.program_id(0) >= k * per_part, _sum_devices(p_refs[k]), g)
        delta, m_new, v_new = _adamw_math(w_ref[...], g, m_ref[...], v_ref[...])
        g_out[...] = g
        d_out[...] = delta
        m_out[...] = m_new
        v_out[...] = v_new

    def part_spec(k):
        return pl.BlockSpec((parts[k].shape[0], tr, cols),
                            lambda i: (0, jnp.clip(i - k * per_part, 0, per_part - 1), 0))

    spec = pl.BlockSpec((tr, cols), lambda i: (i, 0))
    shape = jax.ShapeDtypeStruct((rows, cols), F32)
    return pl.pallas_call(
        body, name=name, grid=(rows // tr,),
        out_shape=(shape, shape, shape, shape),
        in_specs=[part_spec(k) for k in range(n_parts)] + [spec, spec, spec, ANY],
        out_specs=(spec, spec, spec, spec),
        compiler_params=_params("arbitrary"),
    )(*parts, w, m, v, after)


def _adamw_replicated(stats_b, stats_a, sink_parts, ln_g, ln_b, sinks, m_ln_g, m_ln_b, m_sinks, v_ln_g, v_ln_b,
                      v_sinks, after):
    n_q = sinks.shape[1]
    dm = ln_g.shape[1]

    def body(sb_ref, sa_ref, sk_ref, g_ref, b_ref, s_ref, mg_ref, mb_ref, ms_ref, vg_ref, vb_ref, vs_ref, after_ref,
             *outs):
        del after_ref
        layer_sums = (_sum_devices(sa_ref), _sum_devices(sb_ref))
        outs[12][...] = jnp.sum(layer_sums[1][2:3, :], axis=1, keepdims=True) * (0.5 / dm)
        for which, (w_ref, m_ref, v_ref) in enumerate(((g_ref, mg_ref, vg_ref), (b_ref, mb_ref, vb_ref))):
            for layer in range(DEPTH):
                row = slice(layer, layer + 1)
                g = layer_sums[layer][which:which + 1, :]
                res = (g,) + _adamw_math(w_ref[row, :], g, m_ref[row, :], v_ref[row, :])
                for o_ref, val in zip(outs[4 * which:4 * which + 4], res):
                    o_ref[row, :] = val
        g = _sum_devices(sk_ref)[0:1, 0:n_q]
        res = (g,) + _adamw_math(s_ref[...], g, ms_ref[...], vs_ref[...])
        for o_ref, val in zip(outs[8:12], res):
            o_ref[...] = val

    vmem = pl.BlockSpec(memory_space=pltpu.VMEM)
    shapes = [jax.ShapeDtypeStruct(a.shape, F32) for a in (ln_g, ln_b, sinks) for _ in range(4)]
    shapes.append(jax.ShapeDtypeStruct((1, 1), F32))
    return pl.pallas_call(
        body, name="adamw_replicated", out_shape=tuple(shapes),
        in_specs=[vmem] * 12 + [ANY], out_specs=tuple([vmem] * 13),
    )(stats_b, stats_a, sink_parts, ln_g, ln_b, sinks, m_ln_g, m_ln_b, m_sinks, v_ln_g, v_ln_b, v_sinks, after)


def kernel(x, ln_g, ln_b, a_w_in, a_w_group, a_scale, a_w_out, b_w_k, b_w_v, b_w_qg, b_sinks, b_w_out, loss_target, m_ln_g, m_ln_b, m_a_w_in, m_a_w_group, m_a_scale, m_a_w_out, m_b_w_k, m_b_w_v, m_b_w_qg, m_b_sinks, m_b_w_out, v_ln_g, v_ln_b, v_a_w_in, v_a_w_group, v_a_scale, v_a_w_out, v_b_w_k, v_b_w_v, v_b_w_qg, v_b_sinks, v_b_w_out):
    _, seq, dm = x.shape
    n_groups = len(POOL_WINDOWS)
    gd = dm // n_groups
    kvw = b_w_k.shape[1]
    cb = 2 * dm // N_DEV
    rb = dm // N_DEV
    gb = gd // N_DEV

    x2 = x.reshape(seq, dm)
    target = loss_target.reshape(seq, dm)
    w_in_s = a_w_in.reshape(dm, cb)
    w_g_s = a_w_group.reshape(n_groups, gb, gd)
    w_out_s = a_w_out.reshape(rb, dm)
    w_qg_s = b_w_qg.reshape(dm, cb)
    w_outb_s = b_w_out.reshape(rb, dm)

    def cols(ref, dev):
        return ref.at[:, pl.ds(pl.multiple_of(dev * cb, LANES), cb)]

    def rows(ref, dev):
        return ref.at[pl.ds(pl.multiple_of(dev * rb, 8), rb), :]

    def group_rows(ref, dev):
        return ref.at[:, pl.ds(pl.multiple_of(dev * gb, 8), gb), :]

    def k_rows(ref, dev):
        return ref.at[pl.ds(pl.multiple_of(dev * rb, 8), rb), pl.ds(0, kvw)]

    def v_rows(ref, dev):
        return ref.at[pl.ds(pl.multiple_of(dev * rb, 8), rb), pl.ds(kvw, kvw)]

    def scale_cols(ref, dev):
        return ref.at[:, pl.ds(pl.multiple_of(dev * rb, LANES), rb)]

    bf = lambda a: a.astype(BF16)
    wide, square = jax.ShapeDtypeStruct((dm, 2 * dm), BF16), jax.ShapeDtypeStruct((dm, dm), BF16)
    w_g, scale, w_in = _gather_weights(
        "gather_a_in", 0, [(bf(w_g_s), 0, group_rows), (a_scale, 1, scale_cols), (bf(w_in_s), 2, cols)],
        [jax.ShapeDtypeStruct((n_groups, gd, gd), BF16), jax.ShapeDtypeStruct((1, dm), F32), wide])
    (w_out,) = _gather_weights("gather_a_out", 1, [(bf(w_out_s), 0, rows)], [square])
    w_kv, w_qg = _gather_weights(
        "gather_b_in", 2, [(bf(b_w_k), 0, k_rows), (bf(b_w_v), 0, v_rows), (bf(w_qg_s), 1, cols)],
        [jax.ShapeDtypeStruct((dm, 2 * kvw), BF16), wide])
    (w_outb,) = _gather_weights("gather_b_out", 3, [(bf(w_outb_s), 0, rows)], [square])

    tables = _rope_tables(seq)
    bm = _tile(seq, 1024)
    bn = _tile(dm, 1024)
    g0, g1, b0, b1 = ln_g[0:1], ln_g[1:2], ln_b[0:1], ln_b[1:2]

    xb = _cast_bf16("cast_x", x2)
    y, pooled, mixed, z_a = _pool_forward(xb, w_in, w_g, scale)
    xhat1, rstd1, x1b = _out_proj_norm(y, w_out, x2, g0, b0)

    kd, vd, kt, vt = _kv_proj(x1b, w_kv, tables)
    bmq = bm
    tab_spec = pl.BlockSpec((bmq, LANES), lambda i, j: (i, 0))

    def rope_scale(val, cos_ref, sa_ref, sb_ref):
        cos, sa, sb = cos_ref[...], sa_ref[...], sb_ref[...]
        return jnp.concatenate([_rope(val[:, j * LANES:(j + 1) * LANES], cos, sa, sb) * 0.125
                                for j in range(val.shape[1] // LANES)], axis=1)

    qs = _mm("b_q_proj", x1b, w_qg, dims=NN, grid=(seq // bmq, dm // bn),
             a_spec=pl.BlockSpec((bmq, dm), lambda i, j: (i, 0)), b_spec=pl.BlockSpec((dm, bn), lambda i, j: (0, j)),
             out_shape=jax.ShapeDtypeStruct((seq, dm), BF16), out_spec=pl.BlockSpec((bmq, bn), lambda i, j: (i, j)),
             epilogue=rope_scale, extras=tables, extra_specs=(tab_spec,) * 3)
    zb = _mm("b_gate_proj", x1b, w_qg, dims=NN, grid=(seq // bm, dm // bn),
             a_spec=pl.BlockSpec((bm, dm), lambda i, j: (i, 0)),
             b_spec=pl.BlockSpec((dm, bn), lambda i, j: (0, j + dm // bn)),
             out_shape=jax.ShapeDtypeStruct((seq, dm), F32), out_spec=pl.BlockSpec((bm, bn), lambda i, j: (i, j)))
    att, yb = _attn_forward(qs, kd, vt, zb, b_sinks)
    dr2, dr2b, stats_b = _out_proj_norm_loss(yb, w_outb, xhat1, g0, b0, g1, b1, target)

    def weight_grad(name, a, b, n_cols, b_spec=None, part=(0, 1), after=None):
        m_cols = a.shape[1] // part[1]
        tm, tn = _tile(m_cols, 1024), _tile(n_cols, 512)
        first = part[0] * (m_cols // tm)
        return _mm(name, a, b, dims=TN, grid=(m_cols // tm, n_cols // tn),
                   a_spec=pl.BlockSpec((seq, tm), lambda i, j: (0, first + i)),
                   b_spec=b_spec(tn) if b_spec else pl.BlockSpec((seq, tn), lambda i, j: (0, j)),
                   out_shape=jax.ShapeDtypeStruct((m_cols, n_cols), BF16),
                   out_spec=pl.BlockSpec((tm, tn), lambda i, j: (i, j)),
                   extras=() if after is None else (after,), extra_specs=() if after is None else (ANY,))

    def halves_spec(tn):
        per = dm // tn
        return pl.BlockSpec((None, seq, tn), lambda i, j: (j // per, 0, j % per))

    def times_transposed(name, a, w):
        return _mm(name, a, w, dims=NT, grid=(seq // bm, dm // bn),
                   a_spec=pl.BlockSpec((bm, a.shape[1]), lambda i, j: (i, 0)),
                   b_spec=pl.BlockSpec((bn, w.shape[1]), lambda i, j: (j, 0)),
                   out_shape=jax.ShapeDtypeStruct((seq, dm), F32), out_spec=pl.BlockSpec((bm, bn), lambda i, j: (i, j)))

    def stat_row_cols(ref, dev):
        return ref.at[pl.ds(0, 1), pl.ds(pl.multiple_of(dev * rb, LANES), rb)]

    upd = {}
    last = [dr2b]
    my_core = lax.axis_index("c").astype(jnp.int32).reshape(1)

    def then(value):
        last[0] = value[0] if isinstance(value, (list, tuple)) else value
        return value

    def shard_update(key, parts, w, m, v):
        shape = w.shape
        flat = lambda a: a.reshape(-1, shape[-1])
        parts = list(parts) if isinstance(parts, (list, tuple)) else [parts]
        outs = then(_adamw_shard("adamw_" + key, [p.reshape(p.shape[0], -1, shape[-1]) for p in parts], flat(w),
                                 flat(m), flat(v), last[0]))
        upd[key] = [o.reshape(shape) for o in outs]

    def two_level_scatter(name, ids, streams):
        staged = _sibling_exchange(name + "_pair", streams, ids[0])

        def finish():
            sums = [then(_pair_sum(f"{name}_sum{s}", st[0], got, my_core, last[0]))
                    for s, (st, got) in enumerate(zip(streams, staged))]
            return _chip_exchange(name + "_chip", sums, ids[1])
        return finish

    d_w_outb = then(weight_grad("b_out_proj_dw", yb, dr2b, dm))
    (p_outb,) = _exchange_blocks("scatter_b_out", [(d_w_outb, rows, (rb, dm))], 4)
    dyb = times_transposed("b_out_proj_dx", dr2b, w_outb)
    dhq, dkd, dvd, dsink = then(_attn_backward(qs, kd, vd, kt, zb, att, dyb, b_sinks, tables, after=last[0]))
    dkv = _kv_grad_fold(dkd, dvd, tables)
    d_w_kv = weight_grad("b_kv_proj_dw", x1b, dkv, 2 * kvw)
    d_w_qg = then(weight_grad("b_qg_proj_dw", x1b, dhq, 2 * dm, halves_spec, after=d_w_kv))
    finish_b_in = two_level_scatter("scatter_b_in", (5, 11), [(d_w_qg, cols, (dm, cb))])
    dr1, dr1b, stats_a = _stream_grad_norm_backward(dhq, w_qg, dkv, w_kv, dr2, xhat1, rstd1, g0, after=last[0])
    last[0] = dr1b
    shard_update("b_w_out", p_outb, b_w_out, m_b_w_out, v_b_w_out)
    (p_qg,) = finish_b_in()
    all_b, all_a, all_sink = _exchange_blocks("gather_replicated_grads", [
        (stats_b, None, stats_b.shape), (stats_a, None, stats_a.shape), (dsink, None, dsink.shape)], 9)

    d_w_out = then(weight_grad("a_out_proj_dw", y, dr1b, dm, after=last[0]))
    p_out, p_k, p_v = _exchange_blocks("scatter_a_out", [
        (d_w_out, rows, (rb, dm)), (d_w_kv, k_rows, (rb, kvw)), (d_w_kv, v_rows, (rb, kvw))], 6)
    dy = times_transposed("a_out_proj_dx", dr1b, w_out)
    dh, d_w_g, stats_s = then(_pool_mid_backward(dy, mixed, z_a, pooled, w_g, scale, after=last[0]))
    p_g, p_scale = _exchange_blocks("scatter_a_mid", [
        (d_w_g, group_rows, (n_groups, gb, gd)), (stats_s, stat_row_cols, (1, rb))], 7)
    shard_update("b_w_qg", p_qg, b_w_qg, m_b_w_qg, v_b_w_qg)
    rep = then(_adamw_replicated(all_b, all_a, all_sink, ln_g, ln_b, b_sinks, m_ln_g, m_ln_b, m_b_sinks, v_ln_g,
                                 v_ln_b, v_b_sinks, last[0]))
    upd["ln_g"], upd["ln_b"], upd["b_sinks"] = list(rep[0:4]), list(rep[4:8]), list(rep[8:12])
    finish_a_in = []
    for k in range(2):
        d_w_in = then(weight_grad(f"a_in_proj_dw_{k}", xb, dh, 2 * dm, halves_spec, part=(k, 2), after=last[0]))
        finish_a_in.append(two_level_scatter(f"scatter_a_in_{k}", (8 + 2 * k, 12 + k), [(d_w_in, cols, (dm // 2, cb))]))
    shard_update("a_w_out", p_out, a_w_out, m_a_w_out, v_a_w_out)
    shard_update("b_w_k", p_k, b_w_k, m_b_w_k, v_b_w_k)
    shard_update("b_w_v", p_v, b_w_v, m_b_w_v, v_b_w_v)
    shard_update("a_w_group", p_g, a_w_group, m_a_w_group, v_a_w_group)
    shard_update("a_scale", p_scale, a_scale, m_a_scale, v_a_scale)
    p_in = list(finish_a_in[0]()) + list(finish_a_in[1]())
    grad_x = then(_input_grad(dh, w_in, dr1, last[0]))
    shard_update("a_w_in", p_in, a_w_in, m_a_w_in, v_a_w_in)

    loss = rep[12].reshape(())
    order = ["ln_g", "ln_b", "a_w_in", "a_w_group", "a_scale", "a_w_out", "b_w_k", "b_w_v", "b_w_qg", "b_sinks",
             "b_w_out"]
    return (loss, grad_x.reshape(x.shape), *[upd[n][0] for n in order], *[upd[n][1] for n in order],
            *[upd[n][2] for n in order], *[upd[n][3] for n in order])
```

```python
import functools

import jax
import jax.numpy as jnp
from jax import lax
from jax.experimental import pallas as pl
from jax.experimental.pallas import tpu as pltpu
from jax.experimental.pallas import tpu_sc as plsc

F32 = jnp.float32
BF16 = jnp.bfloat16
MESH = pl.DeviceIdType.MESH
AXES = ("x", "y", "c")
N_DEV = 8

POOL_WINDOWS = (2, 4, 8, 16)
POOL_HALO = 16
HEAD_DIM = 64
GQA_GROUP = 8
ATTN_BLOCK = 128
ROPE_THETA = 10000.0
LN_EPS = 1e-5
NEG_INF = -1e30
DEPTH = 2
ALPHA = (2 * DEPTH) ** 0.25
ADAM_LR = 0.001
ADAM_B1 = 0.9
ADAM_B2 = 0.999
ADAM_EPS = 1e-08
ADAM_WD = 0.01
ADAM_STEP = 10

LANES = 128
STAT_ROWS = 8


def _tile(n, want):
    t = min(n, want)
    while n % t:
        t //= 2
    return t


def _params(*sem):
    return pltpu.CompilerParams(dimension_semantics=sem)


ANY = pl.BlockSpec(memory_space=pl.ANY)


def _my_pos():
    return lax.axis_index("x"), lax.axis_index("y"), lax.axis_index("c")


def _dev_index(p):
    return 4 * p[0] + 2 * p[1] + p[2]


def _handshake(peers):
    barrier = pltpu.get_barrier_semaphore()
    for peer in peers:
        pl.semaphore_signal(barrier, inc=1, device_id=peer, device_id_type=MESH)
    pl.semaphore_wait(barrier, len(peers))


def _launch_on_sequencer(name, collective_id, body, operands, out_shapes, scratch):
    return pl.kernel(
        body, out_type=tuple(out_shapes), name=name,
        mesh=plsc.ScalarSubcoreMesh(axis_name="sequencer", num_cores=1), scratch_types=scratch,
        compiler_params=pltpu.CompilerParams(collective_id=collective_id),
    )(*operands)


def _gather_weights(name, collective_id, streams, out_shapes):
    n_s = len(streams)
    n_out = len(out_shapes)

    def body(*refs):
        srcs = refs[:n_s]
        outs = refs[n_s:n_s + n_out]
        send_sems, recv_sems, local_sems = refs[n_s + n_out:]
        x, y, c = _my_pos()
        me, sibling = (x, y, c), (x, y, 1 - c)
        x_nbr, y_nbr, diag = (1 - x, y), (x, 1 - y), (1 - x, 1 - y)
        _handshake([sibling, (*x_nbr, c), (*y_nbr, c)])
        south = c == 0
        relay_from = (jnp.where(south, 1 - x, x), jnp.where(south, y, 1 - y))
        relay_to = (jnp.where(south, x, 1 - x), jnp.where(south, 1 - y, y))
        early, late = jnp.where(south, 1, 2), jnp.where(south, 2, 1)

        def copy(s, k, block, to, from_shard=False):
            out_ref = outs[streams[s][1]]
            win = streams[s][2](out_ref, _dev_index(block))
            return pltpu.make_async_remote_copy(
                src_ref=srcs[s] if from_shard else win, dst_ref=win,
                send_sem=send_sems.at[7 * s + k], recv_sem=recv_sems.at[7 * s + k],
                device_id=to, device_id_type=MESH)

        mine = [pltpu.make_async_copy(srcs[s], streams[s][2](outs[streams[s][1]], _dev_index(me)), local_sems.at[s])
                for s in range(n_s)]
        for cp in mine:
            cp.start()
        sent = []
        for s in range(n_s):
            sent += [copy(s, 0, me, sibling, True), copy(s, 1, me, (*x_nbr, c), True), copy(s, 2, me, (*y_nbr, c), True)]
        for cp in sent:
            cp.start()
        for s in range(n_s):
            copy(s, early, (*relay_from, c), me).wait_recv()
            sent += [copy(s, 3, (*relay_from, c), (*relay_to, c)), copy(s, 3 + early, (*relay_from, c), sibling)]
            for cp in sent[-2:]:
                cp.start()
        for s in range(n_s):
            copy(s, late, (*relay_to, c), me).wait_recv()
            sent.append(copy(s, 3 + late, (*relay_to, c), sibling))
            sent[-1].start()
        for s in range(n_s):
            copy(s, 3, (*diag, c), me).wait_recv()
            sent.append(copy(s, 6, (*diag, c), sibling))
            sent[-1].start()
        for s in range(n_s):
            copy(s, 0, sibling, me).wait_recv()
            for k, chip in ((4, x_nbr), (5, y_nbr), (6, diag)):
                copy(s, k, (*chip, 1 - c), me).wait_recv()
        for cp in sent:
            cp.wait_send()
        for cp in mine:
            cp.wait()

    scratch = [pltpu.SemaphoreType.DMA((7 * n_s,)), pltpu.SemaphoreType.DMA((7 * n_s,)),
               pltpu.SemaphoreType.DMA((n_s,))]
    return _launch_on_sequencer(name, collective_id, body, [s[0] for s in streams], out_shapes, scratch)


def _exchange_blocks(name, streams, collective_id):
    n_s = len(streams)

    def body(*refs):
        srcs = refs[:n_s]
        outs = refs[n_s:2 * n_s]
        send_sems, recv_sems, local_sems = refs[2 * n_s:]
        x, y, c = _my_pos()
        me = _dev_index((x, y, c))
        _handshake([(1 - x if k & 4 else x, 1 - y if k & 2 else y, 1 - c if k & 1 else c) for k in range(1, N_DEV)])

        def window(s, dev):
            return srcs[s] if streams[s][1] is None else streams[s][1](srcs[s], dev)

        mine = [pltpu.make_async_copy(window(s, me), outs[s].at[me], local_sems.at[s]) for s in range(n_s)]
        for cp in mine:
            cp.start()
        copies = []
        for k in (2, 4, 6, 3, 5, 7, 1):
            peer = (1 - x if k & 4 else x, 1 - y if k & 2 else y, 1 - c if k & 1 else c)
            for s in range(n_s):
                copies.append(pltpu.make_async_remote_copy(
                    src_ref=window(s, _dev_index(peer)), dst_ref=outs[s].at[me],
                    send_sem=send_sems.at[7 * s + k - 1], recv_sem=recv_sems.at[7 * s + k - 1],
                    device_id=peer, device_id_type=MESH))
        for cp in copies:
            cp.start()
        for cp in copies:
            cp.wait()
        for cp in mine:
            cp.wait()

    out_shapes = [jax.ShapeDtypeStruct((N_DEV,) + tuple(s[2]), s[0].dtype) for s in streams]
    scratch = [pltpu.SemaphoreType.DMA((7 * n_s,)), pltpu.SemaphoreType.DMA((7 * n_s,)),
               pltpu.SemaphoreType.DMA((n_s,))]
    return _launch_on_sequencer(name, collective_id, body, [s[0] for s in streams], out_shapes, scratch)


N_CHIPS = 4


def _sibling_exchange(name, streams, collective_id):
    n_s = len(streams)

    def body(*refs):
        srcs = refs[:n_s]
        outs = refs[n_s:2 * n_s]
        send_sems, recv_sems = refs[2 * n_s:]
        x, y, c = _my_pos()
        sibling = (x, y, 1 - c)
        _handshake([sibling])
        copies = [pltpu.make_async_remote_copy(
            src_ref=streams[s][1](srcs[s], 2 * chip + (1 - c)), dst_ref=outs[s].at[chip],
            send_sem=send_sems.at[N_CHIPS * s + chip], recv_sem=recv_sems.at[N_CHIPS * s + chip],
            device_id=sibling, device_id_type=MESH) for s in range(n_s) for chip in range(N_CHIPS)]
        for cp in copies:
            cp.start()
        for cp in copies:
            cp.wait()

    out_shapes = [jax.ShapeDtypeStruct((N_CHIPS,) + tuple(s[2]), s[0].dtype) for s in streams]
    scratch = [pltpu.SemaphoreType.DMA((N_CHIPS * n_s,)), pltpu.SemaphoreType.DMA((N_CHIPS * n_s,))]
    return _launch_on_sequencer(name, collective_id, body, [s[0] for s in streams], out_shapes, scratch)


def _pair_sum(name, array, from_sibling, my_core, after):
    _, rows, cols = from_sibling.shape
    tr = _tile(rows, 2048)

    def body(core_ref, own_ref, sib_ref, after_ref, o_ref):
        del core_ref, after_ref
        o_ref[...] = (own_ref[...].astype(F32) + sib_ref[...].astype(F32)).astype(o_ref.dtype)

    staged_spec = pl.BlockSpec((None, tr, cols), lambda k, i, core: (k, i, 0))
    return pl.pallas_call(
        body, name=name, out_shape=jax.ShapeDtypeStruct(from_sibling.shape, array.dtype),
        grid_spec=pltpu.PrefetchScalarGridSpec(
            num_scalar_prefetch=1, grid=(N_CHIPS, rows // tr),
            in_specs=[pl.BlockSpec((tr, cols), lambda k, i, core: (i, 2 * k + core[0])), staged_spec, ANY],
            out_specs=staged_spec),
        compiler_params=_params("parallel", "parallel"),
    )(my_core, array, from_sibling, after)


def _chip_exchange(name, pair_sums, collective_id):
    n_s = len(pair_sums)

    def body(*refs):
        srcs = refs[:n_s]
        outs = refs[n_s:2 * n_s]
        send_sems, recv_sems, local_sems = refs[2 * n_s:]
        x, y, c = _my_pos()
        my_chip = 2 * x + y
        chips = [(1 - x, y), (x, 1 - y), (1 - x, 1 - y)]
        _handshake([(*chip, c) for chip in chips])
        mine = [pltpu.make_async_copy(srcs[s].at[my_chip], outs[s].at[my_chip], local_sems.at[s]) for s in range(n_s)]
        copies = [pltpu.make_async_remote_copy(
            src_ref=srcs[s].at[2 * chip[0] + chip[1]], dst_ref=outs[s].at[my_chip],
            send_sem=send_sems.at[3 * s + j], recv_sem=recv_sems.at[3 * s + j],
            device_id=(*chip, c), device_id_type=MESH) for s in range(n_s) for j, chip in enumerate(chips)]
        for cp in mine + copies:
            cp.start()
        for cp in copies:
            cp.wait()
        for cp in mine:
            cp.wait()

    out_shapes = [jax.ShapeDtypeStruct(p.shape, p.dtype) for p in pair_sums]
    scratch = [pltpu.SemaphoreType.DMA((3 * n_s,)), pltpu.SemaphoreType.DMA((3 * n_s,)),
               pltpu.SemaphoreType.DMA((n_s,))]
    return _launch_on_sequencer(name, collective_id, body, list(pair_sums), out_shapes, scratch)


NN = (((1,), (0,)), ((), ()))
NT = (((1,), (1,)), ((), ()))
TN = (((0,), (0,)), ((), ()))


def _mm(name, a, b, *, dims, grid, a_spec, b_spec, out_shape, out_spec, nk=1,
        add=None, add_spec=None, add_scale=1.0, epilogue=None, extras=(), extra_specs=()):
    n_extra = len(extras)
    has_add = add is not None

    def body(*refs):
        a_ref, b_ref = refs[:2]
        pos = 2
        add_ref = None
        if has_add:
            add_ref = refs[pos]
            pos += 1
        extra_refs = refs[pos:pos + n_extra]
        o_ref = refs[pos + n_extra]
        acc_ref = refs[pos + n_extra + 1] if nk > 1 else None

        def finish(val):
            if has_add:
                val = val + add_scale * add_ref[...]
            if epilogue is not None:
                val = epilogue(val, *extra_refs)
            o_ref[...] = val.astype(o_ref.dtype)

        part = lax.dot_general(a_ref[...].astype(BF16), b_ref[...].astype(BF16), dims,
                               preferred_element_type=F32)
        if nk == 1:
            finish(part)
        else:
            k = pl.program_id(2)

            @pl.when(k == 0)
            def _():
                acc_ref[...] = part

            @pl.when(jnp.logical_and(k > 0, k < nk - 1))
            def _():
                acc_ref[...] += part

            @pl.when(k == nk - 1)
            def _():
                finish(acc_ref[...] + part)

    in_specs = [a_spec, b_spec] + ([add_spec] if has_add else []) + list(extra_specs)
    operands = [a, b] + ([add] if has_add else []) + list(extras)
    scratch = [pltpu.VMEM(out_spec.block_shape, F32)] if nk > 1 else []
    sem = ("parallel", "parallel") + (("arbitrary",) if nk > 1 else ())
    return pl.pallas_call(
        body, name=name, grid=grid, out_shape=out_shape,
        in_specs=in_specs, out_specs=out_spec, scratch_shapes=scratch,
        compiler_params=_params(*sem),
    )(*operands)


def _input_grad(dh, w_in, dr1, after):
    _, seq, dm = dh.shape
    bm, bn = _tile(seq, 1024), _tile(dm, 512)

    def body(dh_ref, w_ref, dr_ref, after_ref, o_ref):
        del after_ref
        o_ref[...] = (lax.dot_general(dh_ref[0], w_ref[:, :dm], NT, preferred_element_type=F32)
                      + lax.dot_general(dh_ref[1], w_ref[:, dm:], NT, preferred_element_type=F32)
                      + ALPHA * dr_ref[...])

    tile_spec = pl.BlockSpec((bm, bn), lambda i, j: (i, j))
    return pl.pallas_call(
        body, name="a_in_proj_dx", grid=(seq // bm, dm // bn),
        out_shape=jax.ShapeDtypeStruct((seq, dm), F32),
        in_specs=[pl.BlockSpec((2, bm, dm), lambda i, j: (0, i, 0)), pl.BlockSpec((bn, 2 * dm), lambda i, j: (j, 0)),
                  tile_spec, ANY],
        out_specs=tile_spec,
        compiler_params=_params("parallel", "parallel"),
    )(dh, w_in, dr1, after)


def _cast_bf16(name, a):
    rows, cols = a.shape
    tr = _tile(rows, 512)

    def body(a_ref, o_ref):
        o_ref[...] = a_ref[...].astype(BF16)

    return pl.pallas_call(
        body, name=name, grid=(rows // tr,),
        out_shape=jax.ShapeDtypeStruct(a.shape, BF16),
        in_specs=[pl.BlockSpec((tr, cols), lambda i: (i, 0))],
        out_specs=pl.BlockSpec((tr, cols), lambda i: (i, 0)),
        compiler_params=_params("parallel"),
    )(a)


def _rope_tables(seq):
    inv_freq = ROPE_THETA ** (-jnp.arange(0, HEAD_DIM, 2, dtype=F32) / HEAD_DIM)
    ang = jnp.arange(seq, dtype=F32)[:, None] * inv_freq[None, :]
    cos, sin = jnp.cos(ang), jnp.sin(ang)
    cos, sin = (jnp.concatenate([t, t, t, t], axis=-1) for t in (cos, sin))
    first_half = (jnp.arange(LANES) % HEAD_DIM < HEAD_DIM // 2)[None, :]
    return cos, jnp.where(first_half, -sin, 0.0), jnp.where(first_half, 0.0, sin)


def _rot(t, sin_a, sin_b):
    return pltpu.roll(t, LANES - HEAD_DIM // 2, 1) * sin_a + pltpu.roll(t, HEAD_DIM // 2, 1) * sin_b


def _rope(t, cos, sin_a, sin_b):
    return t * cos + _rot(t, sin_a, sin_b)


def _rope_transposed(dy, cos, sin_a, sin_b):
    return dy * cos - _rot(dy, sin_a, sin_b)


def _silu_parts(z):
    sig = jax.nn.sigmoid(z)
    return z * sig, sig * (1.0 + z * (1.0 - sig))


def _layer_norm_stats(r):
    mu = jnp.mean(r, axis=-1, keepdims=True)
    d = r - mu
    var = jnp.mean(d * d, axis=-1, keepdims=True)
    rstd = lax.rsqrt(var + LN_EPS)
    return d * rstd, rstd


def _layer_norm_backward(dout, xhat, rstd, gain):
    dxh = dout * gain
    m1 = jnp.mean(dxh, axis=-1, keepdims=True)
    m2 = jnp.mean(dxh * xhat, axis=-1, keepdims=True)
    return rstd * (dxh - m1 - xhat * m2)


def _col_sum(v):
    return jnp.sum(v, axis=0, keepdims=True)


ROW_PART = 128


def _row_parts(tile):
    part = min(tile, ROW_PART)
    return [slice(r, r + part) for r in range(0, tile, part)]


def _pool_forward(xb, w_in, wg, scale):
    seq, dm = xb.shape
    n_g = len(POOL_WINDOWS)
    gd = dm // n_g
    tile = _tile(seq, 1024)
    halo_blocks = tile // POOL_HALO

    def body(x_ref, xp_ref, wu_ref, wz_ref, wg_ref, sc_ref, y_ref, p_ref, mx_ref, z_ref):
        i, g = pl.program_id(0), pl.program_id(1)
        u = jnp.dot(x_ref[...], wu_ref[...], preferred_element_type=F32)
        z = jnp.dot(x_ref[...], wz_ref[...], preferred_element_type=F32)
        prev = jnp.where(i > 0, jnp.dot(xp_ref[...], wu_ref[...], preferred_element_type=F32), 0.0)
        s = jnp.concatenate([prev, u], axis=0)
        sums, sh = [], 1
        while sh < POOL_WINDOWS[-1]:
            s = s + pltpu.roll(s, sh, 0)
            sums.append(s)
            sh *= 2
        win = sums[-1]
        for k in range(n_g - 2, -1, -1):
            win = jnp.where(g == k, sums[k], win)
        row = i * tile + lax.broadcasted_iota(jnp.int32, (tile, 1), 0)
        window = jnp.left_shift(2, g).astype(F32)
        p = win[POOL_HALO:, :] * (1.0 / jnp.minimum((row + 1).astype(F32), window)) - u
        pb = p.astype(BF16)
        mx = jnp.dot(pb, wg_ref[...], preferred_element_type=F32)
        y_ref[...] = (mx * sc_ref[...] * (z * jax.nn.sigmoid(z))).astype(BF16)
        p_ref[...] = pb
        mx_ref[...] = mx
        z_ref[...] = z

    out_spec = pl.BlockSpec((tile, gd), lambda i, g: (i, g))
    return pl.pallas_call(
        body, name="pool_fwd", grid=(seq // tile, n_g),
        out_shape=(jax.ShapeDtypeStruct((seq, dm), BF16), jax.ShapeDtypeStruct((seq, dm), BF16),
                   jax.ShapeDtypeStruct((seq, dm), F32), jax.ShapeDtypeStruct((seq, dm), F32)),
        in_specs=[pl.BlockSpec((tile, dm), lambda i, g: (i, 0)),
                  pl.BlockSpec((POOL_HALO, dm), lambda i, g: (jnp.maximum(i * halo_blocks - 1, 0), 0)),
                  pl.BlockSpec((dm, gd), lambda i, g: (0, g)),
                  pl.BlockSpec((dm, gd), lambda i, g: (0, n_g + g)),
                  pl.BlockSpec((None, gd, gd), lambda i, g: (g, 0, 0)),
                  pl.BlockSpec((1, gd), lambda i, g: (0, g))],
        out_specs=(out_spec, out_spec, out_spec, out_spec),
        compiler_params=_params("parallel", "parallel"),
    )(xb, xb, w_in, w_in, wg, scale)


def _pool_mid_backward(dy, mx, z, p, wg, scale, after):
    seq, dm = dy.shape
    gd = dm // len(POOL_WINDOWS)
    tile = _tile(seq, 256)
    n_i = seq // tile

    def body(dy_ref, mx_ref, z_ref, p_ref, wg_ref, sc_ref, after_ref, dh_ref, dwg_ref, st_ref, dwg_acc, carry):
        del after_ref
        i = pl.program_id(0)
        ti = n_i - 1 - i

        @pl.when(i == 0)
        def _():
            dwg_acc[...] = jnp.zeros_like(dwg_acc)
            carry[...] = jnp.zeros_like(carry)
            st_ref[...] = jnp.zeros_like(st_ref)

        row = ti * tile + lax.broadcasted_iota(jnp.int32, (tile, 1), 0)
        count = (row + 1).astype(F32)
        for g, w in enumerate(POOL_WINDOWS):
            cs = slice(g * gd, (g + 1) * gd)
            z = z_ref[:, cs]
            sz, dsz = _silu_parts(z)
            dyg = dy_ref[:, cs]
            mxg = mx_ref[:, cs]
            sc = sc_ref[:, cs]
            t1 = dyg * sz
            st_ref[0:1, cs] += _col_sum(t1 * mxg)
            dh_ref[1, :, cs] = (dyg * (mxg * sc) * dsz).astype(BF16)
            dmx = (t1 * sc).astype(BF16)
            dwg_acc[g] += lax.dot_general(p_ref[:, cs], dmx, TN, preferred_element_type=F32)
            dp = lax.dot_general(dmx, wg_ref[g], NT, preferred_element_type=F32)
            e = dp * (1.0 / jnp.minimum(count, float(w)))
            s = jnp.concatenate([e, carry[:, cs]], axis=0)
            n = tile + POOL_HALO
            sh = 1
            while sh < w:
                s = s + pltpu.roll(s, n - sh, 0)
                sh *= 2
            dh_ref[0, :, cs] = (s[:tile, :] - dp).astype(BF16)
            carry[:, cs] = e[:POOL_HALO, :]

        @pl.when(i == n_i - 1)
        def _():
            dwg_ref[...] = dwg_acc[...].astype(BF16)

    row_spec = pl.BlockSpec((tile, dm), lambda i: (n_i - 1 - i, 0))
    return pl.pallas_call(
        body, name="pool_mid_bwd", grid=(n_i,),
        out_shape=(jax.ShapeDtypeStruct((2, seq, dm), BF16), jax.ShapeDtypeStruct(wg.shape, BF16),
                   jax.ShapeDtypeStruct((STAT_ROWS, dm), F32)),
        in_specs=[row_spec, row_spec, row_spec, row_spec,
                  pl.BlockSpec(wg.shape, lambda i: (0, 0, 0)),
                  pl.BlockSpec((1, dm), lambda i: (0, 0)), ANY],
        out_specs=(pl.BlockSpec((2, tile, dm), lambda i: (0, n_i - 1 - i, 0)),
                   pl.BlockSpec(wg.shape, lambda i: (0, 0, 0)),
                   pl.BlockSpec((STAT_ROWS, dm), lambda i: (0, 0))),
        scratch_shapes=[pltpu.VMEM(wg.shape, F32), pltpu.VMEM((POOL_HALO, dm), F32)],
        compiler_params=_params("arbitrary"),
    )(dy, mx, z, p, wg, scale, after)


def _out_proj_norm(y, w, x, gain, bias):
    seq, dm = x.shape
    tile = _tile(seq, 512)

    def body(y_ref, w_ref, x_ref, g_ref, b_ref, xhat_ref, rstd_ref, xb_ref):
        for rows in _row_parts(tile):
            o = jnp.dot(y_ref[rows, :], w_ref[...], preferred_element_type=F32)
            xhat, rstd = _layer_norm_stats(ALPHA * x_ref[rows, :] + o)
            xhat_ref[rows, :] = xhat
            rstd_ref[rows, :] = rstd
            xb_ref[rows, :] = (xhat * g_ref[...] + b_ref[...]).astype(BF16)

    row_spec = pl.BlockSpec((tile, dm), lambda i: (i, 0))
    vec_spec = pl.BlockSpec((1, dm), lambda i: (0, 0))
    return pl.pallas_call(
        body, name="out_proj_norm_a", grid=(seq // tile,),
        out_shape=(jax.ShapeDtypeStruct((seq, dm), F32), jax.ShapeDtypeStruct((seq, 1), F32),
                   jax.ShapeDtypeStruct((seq, dm), BF16)),
        in_specs=[row_spec, pl.BlockSpec(w.shape, lambda i: (0, 0), pipeline_mode=pl.Buffered(1)), row_spec, vec_spec,
                  vec_spec],
        out_specs=(row_spec, pl.BlockSpec((tile, 1), lambda i: (i, 0)), row_spec),
        compiler_params=_params("parallel"),
    )(y, w, x, gain, bias)


def _kv_proj(xb, wkv, tables):
    seq, dm = xb.shape
    kvw = wkv.shape[1] // 2
    n_kv = kvw // HEAD_DIM
    tile = _tile(seq, 1024)

    def body(x_ref, w_ref, cos_ref, sa_ref, sb_ref, kd_ref, vd_ref, kt_ref, vt_ref):
        kv = jnp.dot(x_ref[...], w_ref[...], preferred_element_type=F32)
        low = lax.broadcasted_iota(jnp.int32, (1, LANES), 1) < HEAD_DIM
        cos, sa, sb = cos_ref[...], sa_ref[...], sb_ref[...]

        def put(pair, h, nat_ref, t_ref):
            swapped = pltpu.roll(pair, HEAD_DIM, 1)
            for head, dup in ((h, jnp.where(low, pair, swapped)), (h + 1, jnp.where(low, swapped, pair))):
                nat_ref[head] = dup.astype(BF16)
                t_ref[head] = dup.T.astype(BF16)

        for j in range(kvw // LANES):
            put(_rope(kv[:, j * LANES:(j + 1) * LANES], cos, sa, sb), 2 * j, kd_ref, kt_ref)
            put(kv[:, kvw + j * LANES:kvw + (j + 1) * LANES], 2 * j, vd_ref, vt_ref)

    tab_spec = pl.BlockSpec((tile, LANES), lambda i: (i, 0))
    dup_spec = pl.BlockSpec((n_kv, tile, LANES), lambda i: (0, i, 0))
    dup_shape = jax.ShapeDtypeStruct((n_kv, seq, LANES), BF16)
    t_spec = pl.BlockSpec((n_kv, LANES, tile), lambda i: (0, 0, i))
    t_shape = jax.ShapeDtypeStruct((n_kv, LANES, seq), BF16)
    return pl.pallas_call(
        body, name="kv_proj", grid=(seq // tile,),
        out_shape=(dup_shape, dup_shape, t_shape, t_shape),
        in_specs=[pl.BlockSpec((tile, dm), lambda i: (i, 0)), pl.BlockSpec(wkv.shape, lambda i: (0, 0)),
                  tab_spec, tab_spec, tab_spec],
        out_specs=(dup_spec, dup_spec, t_spec, t_spec),
        compiler_params=_params("parallel"),
    )(xb, wkv, *tables)


ATTN_STEP_BLOCKS = 16


def _head_queries(q_ref, rows, low):
    parts = []
    for j in range(GQA_GROUP // 2):
        q2 = q_ref[rows, j * LANES:(j + 1) * LANES]
        parts += [jnp.where(low, q2, 0), jnp.where(low, 0, q2)]
    return parts


def _key_window(prev_ref, cur_ref, b, axis):
    def block(i):
        sl = slice(i * ATTN_BLOCK, (i + 1) * ATTN_BLOCK)
        return cur_ref[sl, :] if axis == 0 else cur_ref[:, sl]
    return jnp.concatenate([prev_ref[...] if b == 0 else block(b - 1), block(b)], axis=axis)


def _mask_bias(n):
    key = lax.broadcasted_iota(jnp.int32, (2 * ATTN_BLOCK, ATTN_BLOCK), 0)
    qry = lax.broadcasted_iota(jnp.int32, (2 * ATTN_BLOCK, ATTN_BLOCK), 1)
    valid = (key > qry) & (key <= qry + ATTN_BLOCK) & ((key >= ATTN_BLOCK) | (n > 0))
    return jnp.where(valid, 0.0, NEG_INF)


def _head_probs_transposed(kcat, qm, bias, sink):
    st = lax.dot_general(kcat, qm, NT, preferred_element_type=F32) + bias
    m = jnp.maximum(jnp.max(st, axis=0, keepdims=True), sink)
    e = jnp.exp(st - m)
    e_sink = jnp.exp(sink - m)
    inv = 1.0 / (jnp.sum(e, axis=0, keepdims=True) + e_sink)
    return e * inv, e_sink * inv


def _probs_transposed(n, kh, kcat, q_all, sink_ref):
    st = lax.dot_general(kcat, q_all, NT, preferred_element_type=F32)
    st = st + jnp.tile(_mask_bias(n), (1, GQA_GROUP))
    sink = jnp.concatenate([jnp.full((1, ATTN_BLOCK), sink_ref[0, kh * GQA_GROUP + h], F32)
                            for h in range(GQA_GROUP)], axis=1)
    m = jnp.maximum(jnp.max(st, axis=0, keepdims=True), sink)
    e = jnp.exp(st - m)
    e_sink = jnp.exp(sink - m)
    inv = 1.0 / (jnp.sum(e, axis=0, keepdims=True) + e_sink)
    return e * inv, e_sink * inv


def _attn_specs(n_width, qb):
    rows = qb * ATTN_BLOCK
    before = lambda n: jnp.maximum(n * qb - 1, 0)
    q_spec = pl.BlockSpec((rows, n_width), lambda kh, n: (n, kh))
    cur = pl.BlockSpec((None, rows, LANES), lambda kh, n: (kh, n, 0))
    prev = pl.BlockSpec((None, ATTN_BLOCK, LANES), lambda kh, n: (kh, before(n), 0))
    cur_t = pl.BlockSpec((None, LANES, rows), lambda kh, n: (kh, 0, n))
    prev_t = pl.BlockSpec((None, LANES, ATTN_BLOCK), lambda kh, n: (kh, 0, before(n)))
    return q_spec, cur, prev, cur_t, prev_t


def _pair_product_transposed(mat_t, rhs, j, low_rows):
    head_a = slice(2 * j * ATTN_BLOCK, (2 * j + 1) * ATTN_BLOCK)
    head_b = slice((2 * j + 1) * ATTN_BLOCK, (2 * j + 2) * ATTN_BLOCK)
    out_t = (jnp.dot(jnp.where(low_rows, mat_t, 0), rhs[:, head_a], preferred_element_type=F32)
             + jnp.dot(jnp.where(low_rows, 0, mat_t), rhs[:, head_b], preferred_element_type=F32))
    return out_t.T


def _attn_forward(qs, kd, vt, zb, sinks):
    seq, dm = qs.shape
    n_kv = kd.shape[0]
    gw = GQA_GROUP * HEAD_DIM

    qb = ATTN_STEP_BLOCKS if (seq // ATTN_BLOCK) % ATTN_STEP_BLOCKS == 0 else 1

    def body(q_ref, kp_ref, kc_ref, vtp_ref, vtc_ref, z_ref, sink_ref, att_ref, yb_ref):
        kh, n = pl.program_id(0), pl.program_id(1)
        low = lax.broadcasted_iota(jnp.int32, (1, LANES), 1) < HEAD_DIM
        low_rows = lax.broadcasted_iota(jnp.int32, (LANES, 1), 0) < HEAD_DIM
        for b in range(qb):
            rows = slice(b * ATTN_BLOCK, (b + 1) * ATTN_BLOCK)
            kcat = _key_window(kp_ref, kc_ref, b, 0)
            vt = _key_window(vtp_ref, vtc_ref, b, 1)
            bias = _mask_bias(n * qb + b)
            pt = jnp.concatenate(
                [_head_probs_transposed(kcat, qm, bias, sink_ref[0, kh * GQA_GROUP + h])[0].astype(BF16)
                 for h, qm in enumerate(_head_queries(q_ref, rows, low))], axis=1)
            for j in range(GQA_GROUP // 2):
                cs = slice(j * LANES, (j + 1) * LANES)
                o2 = _pair_product_transposed(vt, pt, j, low_rows)
                att_ref[rows, cs] = o2
                z = z_ref[rows, cs]
                yb_ref[rows, cs] = (o2 * (z * jax.nn.sigmoid(z))).astype(BF16)

    q_spec, cur, prev, cur_t, prev_t = _attn_specs(gw, qb)
    return pl.pallas_call(
        body, name="attn_fwd", grid=(n_kv, seq // (qb * ATTN_BLOCK)),
        out_shape=(jax.ShapeDtypeStruct((seq, dm), F32), jax.ShapeDtypeStruct((seq, dm), BF16)),
        in_specs=[q_spec, prev, cur, prev_t, cur_t, q_spec, pl.BlockSpec(memory_space=pltpu.SMEM)],
        out_specs=(q_spec, q_spec),
        compiler_params=_params("parallel", "parallel"),
    )(qs, kd, kd, vt, vt, zb, sinks)


def _attn_backward(qs, kd, vd, kt, zb, att, dyb, sinks, tables, after):
    seq, dm = qs.shape
    n_kv = kd.shape[0]
    gw = GQA_GROUP * HEAD_DIM
    n_blocks = seq // ATTN_BLOCK
    qb = ATTN_STEP_BLOCKS if n_blocks % ATTN_STEP_BLOCKS == 0 else 1

    def body(q_ref, kp_ref, kc_ref, vp_ref, vc_ref, ktp_ref, ktc_ref, z_ref, att_ref, dyb_ref, sink_ref,
             cos_ref, sa_ref, sb_ref, after_ref, dh_ref, dk_ref, dv_ref, ds_ref):
        del after_ref
        kh, n = pl.program_id(0), pl.program_id(1)

        @pl.when(n == 0)
        def _():
            dk_ref[...] = jnp.zeros_like(dk_ref)
            dv_ref[...] = jnp.zeros_like(dv_ref)

        @pl.when(jnp.logical_and(n == 0, kh == 0))
        def _():
            ds_ref[...] = jnp.zeros_like(ds_ref)

        low = lax.broadcasted_iota(jnp.int32, (1, LANES), 1) < HEAD_DIM
        low_rows = lax.broadcasted_iota(jnp.int32, (LANES, 1), 0) < HEAD_DIM
        head_lane = lax.broadcasted_iota(jnp.int32, (1, LANES), 1)
        dsink = jnp.zeros((1, LANES), F32)
        for b in range(qb):
            rows = slice(b * ATTN_BLOCK, (b + 1) * ATTN_BLOCK)
            kcat = _key_window(kp_ref, kc_ref, b, 0)
            vcat = _key_window(vp_ref, vc_ref, b, 0)
            kt = _key_window(ktp_ref, ktc_ref, b, 1)
            cos, sa, sb = cos_ref[rows, :], sa_ref[rows, :], sb_ref[rows, :]
            q_all = jnp.concatenate(_head_queries(q_ref, rows, low), axis=0)
            d_parts = []
            for j in range(GQA_GROUP // 2):
                cs = slice(j * LANES, (j + 1) * LANES)
                sz, dsz = _silu_parts(z_ref[rows, cs])
                dy2 = dyb_ref[rows, cs]
                dh_ref[1, rows, cs] = (dy2 * att_ref[rows, cs] * dsz).astype(BF16)
                datt = (dy2 * sz).astype(BF16)
                d_parts += [jnp.where(low, datt, 0), jnp.where(low, 0, datt)]
            d_all = jnp.concatenate(d_parts, axis=0)
            probs_t, sink_p = _probs_transposed(n * qb + b, kh, kcat, q_all, sink_ref)
            dprobs_t = lax.dot_general(vcat, d_all, NT, preferred_element_type=F32)
            row_dot = jnp.sum(probs_t * dprobs_t, axis=0, keepdims=True)
            ds_t = (probs_t * (dprobs_t - row_dot)).astype(BF16)
            dk = jnp.dot(ds_t, q_all, preferred_element_type=F32)
            dv = jnp.dot(probs_t.astype(BF16), d_all, preferred_element_type=F32)
            for j in range(GQA_GROUP // 2):
                dq2 = _pair_product_transposed(kt, ds_t, j, low_rows)
                dh_ref[0, rows, j * LANES:(j + 1) * LANES] = (
                    _rope_transposed(dq2, cos, sa, sb) * 0.125).astype(BF16)
            sink_dot = sink_p * row_dot
            for h in range(GQA_GROUP):
                part = jnp.sum(sink_dot[:, h * ATTN_BLOCK:(h + 1) * ATTN_BLOCK], axis=1, keepdims=True)
                dsink = dsink - jnp.where(head_lane == kh * GQA_GROUP + h, part, 0.0)

            def add_window(dk=dk, dv=dv, b=b):
                start = pl.multiple_of((n * qb + b - 1) * ATTN_BLOCK, ATTN_BLOCK)
                dk_ref[pl.ds(start, 2 * ATTN_BLOCK), :] += dk
                dv_ref[pl.ds(start, 2 * ATTN_BLOCK), :] += dv

            if b > 0:
                add_window()
            else:
                pl.when(n > 0)(add_window)

                @pl.when(n == 0)
                def _(dk=dk, dv=dv):
                    dk_ref[pl.ds(0, ATTN_BLOCK), :] += dk[ATTN_BLOCK:, :]
                    dv_ref[pl.ds(0, ATTN_BLOCK), :] += dv[ATTN_BLOCK:, :]
        ds_ref[0:1, :] += dsink

    q_spec, cur, prev, cur_t, prev_t = _attn_specs(gw, qb)
    tab_spec = pl.BlockSpec((qb * ATTN_BLOCK, LANES), lambda kh, n: (n, 0))
    acc_spec = pl.BlockSpec((None, seq, LANES), lambda kh, n: (kh, 0, 0))
    acc_shape = jax.ShapeDtypeStruct((n_kv, seq, LANES), F32)
    return pl.pallas_call(
        body, name="attn_bwd", grid=(n_kv, n_blocks // qb),
        out_shape=(jax.ShapeDtypeStruct((2, seq, dm), BF16), acc_shape, acc_shape,
                   jax.ShapeDtypeStruct((STAT_ROWS, LANES), F32)),
        in_specs=[q_spec, prev, cur, prev, cur, prev_t, cur_t, q_spec, q_spec, q_spec,
                  pl.BlockSpec(memory_space=pltpu.SMEM), tab_spec, tab_spec, tab_spec, ANY],
        out_specs=(pl.BlockSpec((2, qb * ATTN_BLOCK, gw), lambda kh, n: (0, n, kh)), acc_spec, acc_spec,
                   pl.BlockSpec((STAT_ROWS, LANES), lambda kh, n: (0, 0))),
        compiler_params=_params("arbitrary", "arbitrary"),
    )(qs, kd, kd, vd, vd, kt, kt, zb, att, dyb, sinks, *tables, after)


def _kv_grad_fold(dk, dv, tables):
    n_kv, seq, _ = dk.shape
    kvw = n_kv * HEAD_DIM
    tile = _tile(seq, 512)

    def body(dk_ref, dv_ref, cos_ref, sa_ref, sb_ref, o_ref):
        low = lax.broadcasted_iota(jnp.int32, (1, LANES), 1) < HEAD_DIM
        cos, sa, sb = cos_ref[...], sa_ref[...], sb_ref[...]

        def folded(ref, h):
            t = ref[h]
            return t + pltpu.roll(t, HEAD_DIM, 1)

        for j in range(n_kv // 2):
            ka = _rope_transposed(folded(dk_ref, 2 * j), cos, sa, sb)
            kb = _rope_transposed(folded(dk_ref, 2 * j + 1), cos, sa, sb)
            o_ref[:, j * LANES:(j + 1) * LANES] = jnp.where(low, ka, kb).astype(BF16)
            o_ref[:, kvw + j * LANES:kvw + (j + 1) * LANES] = jnp.where(
                low, folded(dv_ref, 2 * j), folded(dv_ref, 2 * j + 1)).astype(BF16)

    tab_spec = pl.BlockSpec((tile, LANES), lambda i: (i, 0))
    in_spec = pl.BlockSpec((n_kv, tile, LANES), lambda i: (0, i, 0))
    return pl.pallas_call(
        body, name="kv_grad_fold", grid=(seq // tile,),
        out_shape=jax.ShapeDtypeStruct((seq, 2 * kvw), BF16),
        in_specs=[in_spec, in_spec, tab_spec, tab_spec, tab_spec],
        out_specs=pl.BlockSpec((tile, 2 * kvw), lambda i: (i, 0)),
        compiler_params=_params("parallel"),
    )(dk, dv, *tables)


def _out_proj_norm_loss(yb, w, xhat1, gain0, bias0, gain1, bias1, target):
    seq, dm = xhat1.shape
    tile = _tile(seq, 512)

    def body(y_ref, w_ref, xh1_ref, g0_ref, b0_ref, g1_ref, b1_ref, t_ref, dr_ref, drb_ref, st_ref):
        i = pl.program_id(0)

        @pl.when(i == 0)
        def _():
            st_ref[...] = jnp.zeros_like(st_ref)

        parts = _row_parts(tile)
        product = lambda rows: jnp.dot(y_ref[rows, :], w_ref[...], preferred_element_type=F32)
        ahead = product(parts[0])
        for k, rows in enumerate(parts):
            ob = ahead
            if k + 1 < len(parts):
                ahead = product(parts[k + 1])
            x1 = xh1_ref[rows, :] * g0_ref[...] + b0_ref[...]
            xhat, rstd = _layer_norm_stats(ALPHA * x1 + ob)
            err = xhat * g1_ref[...] + b1_ref[...] - t_ref[rows, :]
            dout = err * (1.0 / dm)
            dr = _layer_norm_backward(dout, xhat, rstd, g1_ref[...])
            dr_ref[rows, :] = dr
            drb_ref[rows, :] = dr.astype(BF16)
            st_ref[0:1, :] += _col_sum(dout * xhat)
            st_ref[1:2, :] += _col_sum(dout)
            st_ref[2:3, :] += _col_sum(err * err)

    row_spec = pl.BlockSpec((tile, dm), lambda i: (i, 0))
    vec_spec = pl.BlockSpec((1, dm), lambda i: (0, 0))
    return pl.pallas_call(
        body, name="out_proj_norm_loss_b", grid=(seq // tile,),
        out_shape=(jax.ShapeDtypeStruct((seq, dm), F32), jax.ShapeDtypeStruct((seq, dm), BF16),
                   jax.ShapeDtypeStruct((STAT_ROWS, dm), F32)),
        in_specs=[row_spec, pl.BlockSpec(w.shape, lambda i: (0, 0), pipeline_mode=pl.Buffered(1)), row_spec, vec_spec,
                  vec_spec, vec_spec,
                  vec_spec, row_spec],
        out_specs=(row_spec, row_spec, pl.BlockSpec((STAT_ROWS, dm), lambda i: (0, 0))),
        compiler_params=_params("arbitrary"),
    )(yb, w, xhat1, gain0, bias0, gain1, bias1, target)


def _stream_grad_norm_backward(dhq, wqg, dkv, wkv, dr2, xhat1, rstd1, gain0, after):
    seq, dm = dr2.shape
    tile = _tile(seq, 256)

    def body(dh_ref, wqg_ref, dkv_ref, wkv_ref, dr2_ref, xh_ref, rstd_ref, g_ref, after_ref, dr_ref, drb_ref, st_ref):
        del after_ref

        @pl.when(pl.program_id(0) == 0)
        def _():
            st_ref[...] = jnp.zeros_like(st_ref)

        dx1 = (lax.dot_general(dh_ref[0], wqg_ref[:, :dm], NT, preferred_element_type=F32)
               + lax.dot_general(dh_ref[1], wqg_ref[:, dm:], NT, preferred_element_type=F32)
               + lax.dot_general(dkv_ref[...], wkv_ref[...], NT, preferred_element_type=F32)
               + ALPHA * dr2_ref[...])
        xhat = xh_ref[...]
        dr = _layer_norm_backward(dx1, xhat, rstd_ref[...], g_ref[...])
        dr_ref[...] = dr
        drb_ref[...] = dr.astype(BF16)
        st_ref[0:1, :] += _col_sum(dx1 * xhat)
        st_ref[1:2, :] += _col_sum(dx1)

    row_spec = pl.BlockSpec((tile, dm), lambda i: (i, 0))
    resident = pl.Buffered(1)
    return pl.pallas_call(
        body, name="stream_grad_norm_bwd", grid=(seq // tile,),
        out_shape=(jax.ShapeDtypeStruct((seq, dm), F32), jax.ShapeDtypeStruct((seq, dm), BF16),
                   jax.ShapeDtypeStruct((STAT_ROWS, dm), F32)),
        in_specs=[pl.BlockSpec((2, tile, dm), lambda i: (0, i, 0)),
                  pl.BlockSpec(wqg.shape, lambda i: (0, 0), pipeline_mode=resident),
                  pl.BlockSpec((tile, dkv.shape[1]), lambda i: (i, 0)),
                  pl.BlockSpec(wkv.shape, lambda i: (0, 0), pipeline_mode=resident),
                  row_spec, row_spec, pl.BlockSpec((tile, 1), lambda i: (i, 0)),
                  pl.BlockSpec((1, dm), lambda i: (0, 0)), ANY],
        out_specs=(row_spec, row_spec, pl.BlockSpec((STAT_ROWS, dm), lambda i: (0, 0))),
        compiler_params=_params("arbitrary"),
    )(dhq, wqg, dkv, wkv, dr2, xhat1, rstd1, gain0, after)


def _adamw_math(w, g, m, v):
    m = ADAM_B1 * m + (1.0 - ADAM_B1) * g
    v = ADAM_B2 * v + (1.0 - ADAM_B2) * (g * g)
    m_hat = m / (1.0 - ADAM_B1 ** ADAM_STEP)
    v_hat = v / (1.0 - ADAM_B2 ** ADAM_STEP)
    delta = -ADAM_LR * (m_hat / (jnp.sqrt(v_hat) + ADAM_EPS) + ADAM_WD * w)
    return delta, m, v


def _sum_devices(ref):
    total = ref[0].astype(F32)
    for d in range(1, ref.shape[0]):
        total = total + ref[d].astype(F32)
    return total


def _adamw_shard(name, parts, w, m, v, after):
    rows, cols = w.shape
    n_parts = len(parts)
    part_rows = rows // n_parts
    tr = _tile(part_rows, max(8, (1 << 18) // cols)) if part_rows >= 8 else part_rows
    per_part = part_rows // tr

    def body(*refs):
        p_refs = refs[:n_parts]
        w_ref, m_ref, v_ref, _, g_out, d_out, m_out, v_out = refs[n_parts:]
        g = _sum_devices(p_refs[0])
        for k in range(1, n_parts):
            g = jnp.where(pl.program_id(0) >= k * per_part, _sum_devices(p_refs[k]), g)
        delta, m_new, v_new = _adamw_math(w_ref[...], g, m_ref[...], v_ref[...])
        g_out[...] = g
        d_out[...] = delta
        m_out[...] = m_new
        v_out[...] = v_new

    def part_spec(k):
        return pl.BlockSpec((parts[k].shape[0], tr, cols),
                            lambda i: (0, jnp.clip(i - k * per_part, 0, per_part - 1), 0))

    spec = pl.BlockSpec((tr, cols), lambda i: (i, 0))
    shape = jax.ShapeDtypeStruct((rows, cols), F32)
    return pl.pallas_call(
        body, name=name, grid=(rows // tr,),
        out_shape=(shape, shape, shape, shape),
        in_specs=[part_spec(k) for k in range(n_parts)] + [spec, spec, spec, ANY],
        out_specs=(spec, spec, spec, spec),
        compiler_params=_params("arbitrary"),
    )(*parts, w, m, v, after)


def _adamw_replicated(stats_b, stats_a, sink_parts, ln_g, ln_b, sinks, m_ln_g, m_ln_b, m_sinks, v_ln_g, v_ln_b,
                      v_sinks, after):
    n_q = sinks.shape[1]
    dm = ln_g.shape[1]

    def body(sb_ref, sa_ref, sk_ref, g_ref, b_ref, s_ref, mg_ref, mb_ref, ms_ref, vg_ref, vb_ref, vs_ref, after_ref,
             *outs):
        del after_ref
        layer_sums = (_sum_devices(sa_ref), _sum_devices(sb_ref))
        outs[12][...] = jnp.sum(layer_sums[1][2:3, :], axis=1, keepdims=True) * (0.5 / dm)
        for which, (w_ref, m_ref, v_ref) in enumerate(((g_ref, mg_ref, vg_ref), (b_ref, mb_ref, vb_ref))):
            for layer in range(DEPTH):
                row = slice(layer, layer + 1)
                g = layer_sums[layer][which:which + 1, :]
                res = (g,) + _adamw_math(w_ref[row, :], g, m_ref[row, :], v_ref[row, :])
                for o_ref, val in zip(outs[4 * which:4 * which + 4], res):
                    o_ref[row, :] = val
        g = _sum_devices(sk_ref)[0:1, 0:n_q]
        res = (g,) + _adamw_math(s_ref[...], g, ms_ref[...], vs_ref[...])
        for o_ref, val in zip(outs[8:12], res):
            o_ref[...] = val

    vmem = pl.BlockSpec(memory_space=pltpu.VMEM)
    shapes = [jax.ShapeDtypeStruct(a.shape, F32) for a in (ln_g, ln_b, sinks) for _ in range(4)]
    shapes.append(jax.ShapeDtypeStruct((1, 1), F32))
    return pl.pallas_call(
        body, name="adamw_replicated", out_shape=tuple(shapes),
        in_specs=[vmem] * 12 + [ANY], out_specs=tuple([vmem] * 13),
    )(stats_b, stats_a, sink_parts, ln_g, ln_b, sinks, m_ln_g, m_ln_b, m_sinks, v_ln_g, v_ln_b, v_sinks, after)


def kernel(x, ln_g, ln_b, a_w_in, a_w_group, a_scale, a_w_out, b_w_k, b_w_v, b_w_qg, b_sinks, b_w_out, loss_target, m_ln_g, m_ln_b, m_a_w_in, m_a_w_group, m_a_scale, m_a_w_out, m_b_w_k, m_b_w_v, m_b_w_qg, m_b_sinks, m_b_w_out, v_ln_g, v_ln_b, v_a_w_in, v_a_w_group, v_a_scale, v_a_w_out, v_b_w_k, v_b_w_v, v_b_w_qg, v_b_sinks, v_b_w_out):
    _, seq, dm = x.shape
    n_groups = len(POOL_WINDOWS)
    gd = dm // n_groups
    kvw = b_w_k.shape[1]
    cb = 2 * dm // N_DEV
    rb = dm // N_DEV
    gb = gd // N_DEV

    x2 = x.reshape(seq, dm)
    target = loss_target.reshape(seq, dm)
    w_in_s = a_w_in.reshape(dm, cb)
    w_g_s = a_w_group.reshape(n_groups, gb, gd)
    w_out_s = a_w_out.reshape(rb, dm)
    w_qg_s = b_w_qg.reshape(dm, cb)
    w_outb_s = b_w_out.reshape(rb, dm)

    def cols(ref, dev):
        return ref.at[:, pl.ds(pl.multiple_of(dev * cb, LANES), cb)]

    def rows(ref, dev):
        return ref.at[pl.ds(pl.multiple_of(dev * rb, 8), rb), :]

    def group_rows(ref, dev):
        return ref.at[:, pl.ds(pl.multiple_of(dev * gb, 8), gb), :]

    def k_rows(ref, dev):
        return ref.at[pl.ds(pl.multiple_of(dev * rb, 8), rb), pl.ds(0, kvw)]

    def v_rows(ref, dev):
        return ref.at[pl.ds(pl.multiple_of(dev * rb, 8), rb), pl.ds(kvw, kvw)]

    def scale_cols(ref, dev):
        return ref.at[:, pl.ds(pl.multiple_of(dev * rb, LANES), rb)]

    bf = lambda a: a.astype(BF16)
    wide, square = jax.ShapeDtypeStruct((dm, 2 * dm), BF16), jax.ShapeDtypeStruct((dm, dm), BF16)
    w_g, scale, w_in = _gather_weights(
        "gather_a_in", 0, [(bf(w_g_s), 0, group_rows), (a_scale, 1, scale_cols), (bf(w_in_s), 2, cols)],
        [jax.ShapeDtypeStruct((n_groups, gd, gd), BF16), jax.ShapeDtypeStruct((1, dm), F32), wide])
    (w_out,) = _gather_weights("gather_a_out", 1, [(bf(w_out_s), 0, rows)], [square])
    w_kv, w_qg = _gather_weights(
        "gather_b_in", 2, [(bf(b_w_k), 0, k_rows), (bf(b_w_v), 0, v_rows), (bf(w_qg_s), 1, cols)],
        [jax.ShapeDtypeStruct((dm, 2 * kvw), BF16), wide])
    (w_outb,) = _gather_weights("gather_b_out", 3, [(bf(w_outb_s), 0, rows)], [square])

    tables = _rope_tables(seq)
    bm = _tile(seq, 2048)
    bn = _tile(dm, 1024)
    g0, g1, b0, b1 = ln_g[0:1], ln_g[1:2], ln_b[0:1], ln_b[1:2]

    xb = _cast_bf16("cast_x", x2)
    y, pooled, mixed, z_a = _pool_forward(xb, w_in, w_g, scale)
    xhat1, rstd1, x1b = _out_proj_norm(y, w_out, x2, g0, b0)

    kd, vd, kt, vt = _kv_proj(x1b, w_kv, tables)
    bmq = bm
    tab_spec = pl.BlockSpec((bmq, LANES), lambda i, j: (i, 0))

    def rope_scale(val, cos_ref, sa_ref, sb_ref):
        cos, sa, sb = cos_ref[...], sa_ref[...], sb_ref[...]
        return jnp.concatenate([_rope(val[:, j * LANES:(j + 1) * LANES], cos, sa, sb) * 0.125
                                for j in range(val.shape[1] // LANES)], axis=1)

    qs = _mm("b_q_proj", x1b, w_qg, dims=NN, grid=(seq // bmq, dm // bn),
             a_spec=pl.BlockSpec((bmq, dm), lambda i, j: (i, 0)), b_spec=pl.BlockSpec((dm, bn), lambda i, j: (0, j)),
             out_shape=jax.ShapeDtypeStruct((seq, dm), BF16), out_spec=pl.BlockSpec((bmq, bn), lambda i, j: (i, j)),
             epilogue=rope_scale, extras=tables, extra_specs=(tab_spec,) * 3)
    zb = _mm("b_gate_proj", x1b, w_qg, dims=NN, grid=(seq // bm, dm // bn),
             a_spec=pl.BlockSpec((bm, dm), lambda i, j: (i, 0)),
             b_spec=pl.BlockSpec((dm, bn), lambda i, j: (0, j + dm // bn)),
             out_shape=jax.ShapeDtypeStruct((seq, dm), F32), out_spec=pl.BlockSpec((bm, bn), lambda i, j: (i, j)))
    att, yb = _attn_forward(qs, kd, vt, zb, b_sinks)
    dr2, dr2b, stats_b = _out_proj_norm_loss(yb, w_outb, xhat1, g0, b0, g1, b1, target)

    def weight_grad(name, a, b, n_cols, b_spec=None, part=(0, 1), after=None):
        m_cols = a.shape[1] // part[1]
        tm, tn = _tile(m_cols, 1024), _tile(n_cols, 1024)
        first = part[0] * (m_cols // tm)
        return _mm(name, a, b, dims=TN, grid=(m_cols // tm, n_cols // tn),
                   a_spec=pl.BlockSpec((seq, tm), lambda i, j: (0, first + i)),
                   b_spec=b_spec(tn) if b_spec else pl.BlockSpec((seq, tn), lambda i, j: (0, j)),
                   out_shape=jax.ShapeDtypeStruct((m_cols, n_cols), BF16),
                   out_spec=pl.BlockSpec((tm, tn), lambda i, j: (i, j)),
                   extras=() if after is None else (after,), extra_specs=() if after is None else (ANY,))

    def halves_spec(tn):
        per = dm // tn
        return pl.BlockSpec((None, seq, tn), lambda i, j: (j // per, 0, j % per))

    def times_transposed(name, a, w):
        return _mm(name, a, w, dims=NT, grid=(seq // bm, dm // bn),
                   a_spec=pl.BlockSpec((bm, a.shape[1]), lambda i, j: (i, 0)),
                   b_spec=pl.BlockSpec((bn, w.shape[1]), lambda i, j: (j, 0)),
                   out_shape=jax.ShapeDtypeStruct((seq, dm), F32), out_spec=pl.BlockSpec((bm, bn), lambda i, j: (i, j)))

    def stat_row_cols(ref, dev):
        return ref.at[pl.ds(0, 1), pl.ds(pl.multiple_of(dev * rb, LANES), rb)]

    upd = {}
    last = [dr2b]
    my_core = lax.axis_index("c").astype(jnp.int32).reshape(1)

    def then(value):
        last[0] = value[0] if isinstance(value, (list, tuple)) else value
        return value

    def shard_update(key, parts, w, m, v):
        shape = w.shape
        flat = lambda a: a.reshape(-1, shape[-1])
        parts = list(parts) if isinstance(parts, (list, tuple)) else [parts]
        outs = then(_adamw_shard("adamw_" + key, [p.reshape(p.shape[0], -1, shape[-1]) for p in parts], flat(w),
                                 flat(m), flat(v), last[0]))
        upd[key] = [o.reshape(shape) for o in outs]

    def two_level_scatter(name, ids, streams):
        staged = _sibling_exchange(name + "_pair", streams, ids[0])

        def finish():
            sums = [then(_pair_sum(f"{name}_sum{s}", st[0], got, my_core, last[0]))
                    for s, (st, got) in enumerate(zip(streams, staged))]
            return _chip_exchange(name + "_chip", sums, ids[1])
        return finish

    d_w_outb = then(weight_grad("b_out_proj_dw", yb, dr2b, dm))
    (p_outb,) = _exchange_blocks("scatter_b_out", [(d_w_outb, rows, (rb, dm))], 4)
    dyb = times_transposed("b_out_proj_dx", dr2b, w_outb)
    dhq, dkd, dvd, dsink = then(_attn_backward(qs, kd, vd, kt, zb, att, dyb, b_sinks, tables, after=last[0]))
    dkv = _kv_grad_fold(dkd, dvd, tables)
    d_w_kv = weight_grad("b_kv_proj_dw", x1b, dkv, 2 * kvw)
    d_w_qg = then(weight_grad("b_qg_proj_dw", x1b, dhq, 2 * dm, halves_spec, after=d_w_kv))
    finish_b_in = two_level_scatter("scatter_b_in", (5, 11), [(d_w_qg, cols, (dm, cb))])
    dr1, dr1b, stats_a = _stream_grad_norm_backward(dhq, w_qg, dkv, w_kv, dr2, xhat1, rstd1, g0, after=last[0])
    last[0] = dr1b
    shard_update("b_w_out", p_outb, b_w_out, m_b_w_out, v_b_w_out)
    (p_qg,) = finish_b_in()
    all_b, all_a, all_sink = _exchange_blocks("gather_replicated_grads", [
        (stats_b, None, stats_b.shape), (stats_a, None, stats_a.shape), (dsink, None, dsink.shape)], 9)

    d_w_out = then(weight_grad("a_out_proj_dw", y, dr1b, dm, after=last[0]))
    p_out, p_k, p_v = _exchange_blocks("scatter_a_out", [
        (d_w_out, rows, (rb, dm)), (d_w_kv, k_rows, (rb, kvw)), (d_w_kv, v_rows, (rb, kvw))], 6)
    dy = times_transposed("a_out_proj_dx", dr1b, w_out)
    dh, d_w_g, stats_s = then(_pool_mid_backward(dy, mixed, z_a, pooled, w_g, scale, after=last[0]))
    p_g, p_scale = _exchange_blocks("scatter_a_mid", [
        (d_w_g, group_rows, (n_groups, gb, gd)), (stats_s, stat_row_cols, (1, rb))], 7)
    shard_update("b_w_qg", p_qg, b_w_qg, m_b_w_qg, v_b_w_qg)
    rep = then(_adamw_replicated(all_b, all_a, all_sink, ln_g, ln_b, b_sinks, m_ln_g, m_ln_b, m_b_sinks, v_ln_g,
                                 v_ln_b, v_b_sinks, last[0]))
    upd["ln_g"], upd["ln_b"], upd["b_sinks"] = list(rep[0:4]), list(rep[4:8]), list(rep[8:12])
    finish_a_in = []
    for k in range(2):
        d_w_in = then(weight_grad(f"a_in_proj_dw_{k}", xb, dh, 2 * dm, halves_spec, part=(k, 2), after=last[0]))
        finish_a_in.append(two_level_scatter(f"scatter_a_in_{k}", (8 + 2 * k, 12 + k), [(d_w_in, cols, (dm // 2, cb))]))
    shard_update("a_w_out", p_out, a_w_out, m_a_w_out, v_a_w_out)
    shard_update("b_w_k", p_k, b_w_k, m_b_w_k, v_b_w_k)
    shard_update("b_w_v", p_v, b_w_v, m_b_w_v, v_b_w_v)
    shard_update("a_w_group", p_g, a_w_group, m_a_w_group, v_a_w_group)
    shard_update("a_scale", p_scale, a_scale, m_a_scale, v_a_scale)
    p_in = list(finish_a_in[0]()) + list(finish_a_in[1]())
    grad_x = then(_input_grad(dh, w_in, dr1, last[0]))
    shard_update("a_w_in", p_in, a_w_in, m_a_w_in, v_a_w_in)

    loss = rep[12].reshape(())
    order = ["ln_g", "ln_b", "a_w_in", "a_w_group", "a_scale", "a_w_out", "b_w_k", "b_w_v", "b_w_qg", "b_sinks",
             "b_w_out"]
    return (loss, grad_x.reshape(x.shape), *[upd[n][0] for n in order], *[upd[n][1] for n in order],
            *[upd[n][2] for n in order], *[upd[n][3] for n in order])
```

```python
import functools

import jax
import jax.numpy as jnp
from jax import lax
from jax.experimental import pallas as pl
from jax.experimental.pallas import tpu as pltpu
from jax.experimental.pallas import tpu_sc as plsc

F32 = jnp.float32
BF16 = jnp.bfloat16
MESH = pl.DeviceIdType.MESH
AXES = ("x", "y", "c")
N_DEV = 8

POOL_WINDOWS = (2, 4, 8, 16)
POOL_HALO = 16
HEAD_DIM = 64
GQA_GROUP = 8
ATTN_BLOCK = 128
ROPE_THETA = 10000.0
LN_EPS = 1e-5
NEG_INF = -1e30
DEPTH = 2
ALPHA = (2 * DEPTH) ** 0.25
ADAM_LR = 0.001
ADAM_B1 = 0.9
ADAM_B2 = 0.999
ADAM_EPS = 1e-08
ADAM_WD = 0.01
ADAM_STEP = 10

LANES = 128
STAT_ROWS = 8


def _tile(n, want):
    t = min(n, want)
    while n % t:
        t //= 2
    return t


def _params(*sem):
    return pltpu.CompilerParams(dimension_semantics=sem)


ANY = pl.BlockSpec(memory_space=pl.ANY)


def _my_pos():
    return lax.axis_index("x"), lax.axis_index("y"), lax.axis_index("c")


def _dev_index(p):
    return 4 * p[0] + 2 * p[1] + p[2]


def _handshake(peers):
    barrier = pltpu.get_barrier_semaphore()
    for peer in peers:
        pl.semaphore_signal(barrier, inc=1, device_id=peer, device_id_type=MESH)
    pl.semaphore_wait(barrier, len(peers))


def _launch_on_sequencer(name, collective_id, body, operands, out_shapes, scratch):
    return pl.kernel(
        body, out_type=tuple(out_shapes), name=name,
        mesh=plsc.ScalarSubcoreMesh(axis_name="sequencer", num_cores=1), scratch_types=scratch,
        compiler_params=pltpu.CompilerParams(collective_id=collective_id),
    )(*operands)


def _gather_weights(name, collective_id, streams, out_shapes):
    n_s = len(streams)
    n_out = len(out_shapes)

    def body(*refs):
        srcs = refs[:n_s]
        outs = refs[n_s:n_s + n_out]
        send_sems, recv_sems, local_sems = refs[n_s + n_out:]
        x, y, c = _my_pos()
        me, sibling = (x, y, c), (x, y, 1 - c)
        x_nbr, y_nbr, diag = (1 - x, y), (x, 1 - y), (1 - x, 1 - y)
        _handshake([sibling, (*x_nbr, c), (*y_nbr, c)])
        south = c == 0
        relay_from = (jnp.where(south, 1 - x, x), jnp.where(south, y, 1 - y))
        relay_to = (jnp.where(south, x, 1 - x), jnp.where(south, 1 - y, y))
        early, late = jnp.where(south, 1, 2), jnp.where(south, 2, 1)

        def copy(s, k, block, to, from_shard=False):
            out_ref = outs[streams[s][1]]
            win = streams[s][2](out_ref, _dev_index(block))
            return pltpu.make_async_remote_copy(
                src_ref=srcs[s] if from_shard else win, dst_ref=win,
                send_sem=send_sems.at[7 * s + k], recv_sem=recv_sems.at[7 * s + k],
                device_id=to, device_id_type=MESH)

        mine = [pltpu.make_async_copy(srcs[s], streams[s][2](outs[streams[s][1]], _dev_index(me)), local_sems.at[s])
                for s in range(n_s)]
        for cp in mine:
            cp.start()
        sent = []
        for s in range(n_s):
            sent += [copy(s, 0, me, sibling, True), copy(s, 1, me, (*x_nbr, c), True), copy(s, 2, me, (*y_nbr, c), True)]
        for cp in sent:
            cp.start()
        for s in range(n_s):
            copy(s, early, (*relay_from, c), me).wait_recv()
            sent += [copy(s, 3, (*relay_from, c), (*relay_to, c)), copy(s, 3 + early, (*relay_from, c), sibling)]
            for cp in sent[-2:]:
                cp.start()
        for s in range(n_s):
            copy(s, late, (*relay_to, c), me).wait_recv()
            sent.append(copy(s, 3 + late, (*relay_to, c), sibling))
            sent[-1].start()
        for s in range(n_s):
            copy(s, 3, (*diag, c), me).wait_recv()
            sent.append(copy(s, 6, (*diag, c), sibling))
            sent[-1].start()
        for s in range(n_s):
            copy(s, 0, sibling, me).wait_recv()
            for k, chip in ((4, x_nbr), (5, y_nbr), (6, diag)):
                copy(s, k, (*chip, 1 - c), me).wait_recv()
        for cp in sent:
            cp.wait_send()
        for cp in mine:
            cp.wait()

    scratch = [pltpu.SemaphoreType.DMA((7 * n_s,)), pltpu.SemaphoreType.DMA((7 * n_s,)),
               pltpu.SemaphoreType.DMA((n_s,))]
    return _launch_on_sequencer(name, collective_id, body, [s[0] for s in streams], out_shapes, scratch)


def _exchange_blocks(name, streams, collective_id):
    n_s = len(streams)

    def body(*refs):
        srcs = refs[:n_s]
        outs = refs[n_s:2 * n_s]
        send_sems, recv_sems, local_sems = refs[2 * n_s:]
        x, y, c = _my_pos()
        me = _dev_index((x, y, c))
        _handshake([(1 - x if k & 4 else x, 1 - y if k & 2 else y, 1 - c if k & 1 else c) for k in range(1, N_DEV)])

        def window(s, dev):
            return srcs[s] if streams[s][1] is None else streams[s][1](srcs[s], dev)

        mine = [pltpu.make_async_copy(window(s, me), outs[s].at[me], local_sems.at[s]) for s in range(n_s)]
        for cp in mine:
            cp.start()
        copies = []
        for k in (2, 4, 6, 3, 5, 7, 1):
            peer = (1 - x if k & 4 else x, 1 - y if k & 2 else y, 1 - c if k & 1 else c)
            for s in range(n_s):
                copies.append(pltpu.make_async_remote_copy(
                    src_ref=window(s, _dev_index(peer)), dst_ref=outs[s].at[me],
                    send_sem=send_sems.at[7 * s + k - 1], recv_sem=recv_sems.at[7 * s + k - 1],
                    device_id=peer, device_id_type=MESH))
        for cp in copies:
            cp.start()
        for cp in copies:
            cp.wait()
        for cp in mine:
            cp.wait()

    out_shapes = [jax.ShapeDtypeStruct((N_DEV,) + tuple(s[2]), s[0].dtype) for s in streams]
    scratch = [pltpu.SemaphoreType.DMA((7 * n_s,)), pltpu.SemaphoreType.DMA((7 * n_s,)),
               pltpu.SemaphoreType.DMA((n_s,))]
    return _launch_on_sequencer(name, collective_id, body, [s[0] for s in streams], out_shapes, scratch)


N_CHIPS = 4


def _sibling_exchange(name, streams, collective_id):
    n_s = len(streams)

    def body(*refs):
        srcs = refs[:n_s]
        outs = refs[n_s:2 * n_s]
        send_sems, recv_sems = refs[2 * n_s:]
        x, y, c = _my_pos()
        sibling = (x, y, 1 - c)
        _handshake([sibling])
        copies = [pltpu.make_async_remote_copy(
            src_ref=streams[s][1](srcs[s], 2 * chip + (1 - c)), dst_ref=outs[s].at[chip],
            send_sem=send_sems.at[N_CHIPS * s + chip], recv_sem=recv_sems.at[N_CHIPS * s + chip],
            device_id=sibling, device_id_type=MESH) for s in range(n_s) for chip in range(N_CHIPS)]
        for cp in copies:
            cp.start()
        for cp in copies:
            cp.wait()

    out_shapes = [jax.ShapeDtypeStruct((N_CHIPS,) + tuple(s[2]), s[0].dtype) for s in streams]
    scratch = [pltpu.SemaphoreType.DMA((N_CHIPS * n_s,)), pltpu.SemaphoreType.DMA((N_CHIPS * n_s,))]
    return _launch_on_sequencer(name, collective_id, body, [s[0] for s in streams], out_shapes, scratch)


def _pair_sum(name, array, from_sibling, my_core, after):
    _, rows, cols = from_sibling.shape
    tr = _tile(rows, 2048)

    def body(core_ref, own_ref, sib_ref, after_ref, o_ref):
        del core_ref, after_ref
        o_ref[...] = (own_ref[...].astype(F32) + sib_ref[...].astype(F32)).astype(o_ref.dtype)

    staged_spec = pl.BlockSpec((None, tr, cols), lambda k, i, core: (k, i, 0))
    return pl.pallas_call(
        body, name=name, out_shape=jax.ShapeDtypeStruct(from_sibling.shape, array.dtype),
        grid_spec=pltpu.PrefetchScalarGridSpec(
            num_scalar_prefetch=1, grid=(N_CHIPS, rows // tr),
            in_specs=[pl.BlockSpec((tr, cols), lambda k, i, core: (i, 2 * k + core[0])), staged_spec, ANY],
            out_specs=staged_spec),
        compiler_params=_params("parallel", "parallel"),
    )(my_core, array, from_sibling, after)


def _chip_exchange(name, pair_sums, collective_id):
    n_s = len(pair_sums)

    def body(*refs):
        srcs = refs[:n_s]
        outs = refs[n_s:2 * n_s]
        send_sems, recv_sems, local_sems = refs[2 * n_s:]
        x, y, c = _my_pos()
        my_chip = 2 * x + y
        chips = [(1 - x, y), (x, 1 - y), (1 - x, 1 - y)]
        _handshake([(*chip, c) for chip in chips])
        mine = [pltpu.make_async_copy(srcs[s].at[my_chip], outs[s].at[my_chip], local_sems.at[s]) for s in range(n_s)]
        copies = [pltpu.make_async_remote_copy(
            src_ref=srcs[s].at[2 * chip[0] + chip[1]], dst_ref=outs[s].at[my_chip],
            send_sem=send_sems.at[3 * s + j], recv_sem=recv_sems.at[3 * s + j],
            device_id=(*chip, c), device_id_type=MESH) for s in range(n_s) for j, chip in enumerate(chips)]
        for cp in mine + copies:
            cp.start()
        for cp in copies:
            cp.wait()
        for cp in mine:
            cp.wait()

    out_shapes = [jax.ShapeDtypeStruct(p.shape, p.dtype) for p in pair_sums]
    scratch = [pltpu.SemaphoreType.DMA((3 * n_s,)), pltpu.SemaphoreType.DMA((3 * n_s,)),
               pltpu.SemaphoreType.DMA((n_s,))]
    return _launch_on_sequencer(name, collective_id, body, list(pair_sums), out_shapes, scratch)


NN = (((1,), (0,)), ((), ()))
NT = (((1,), (1,)), ((), ()))
TN = (((0,), (0,)), ((), ()))


def _mm(name, a, b, *, dims, grid, a_spec, b_spec, out_shape, out_spec, nk=1,
        add=None, add_spec=None, add_scale=1.0, epilogue=None, extras=(), extra_specs=()):
    n_extra = len(extras)
    has_add = add is not None

    def body(*refs):
        a_ref, b_ref = refs[:2]
        pos = 2
        add_ref = None
        if has_add:
            add_ref = refs[pos]
            pos += 1
        extra_refs = refs[pos:pos + n_extra]
        o_ref = refs[pos + n_extra]
        acc_ref = refs[pos + n_extra + 1] if nk > 1 else None

        def finish(val):
            if has_add:
                val = val + add_scale * add_ref[...]
            if epilogue is not None:
                val = epilogue(val, *extra_refs)
            o_ref[...] = val.astype(o_ref.dtype)

        part = lax.dot_general(a_ref[...].astype(BF16), b_ref[...].astype(BF16), dims,
                               preferred_element_type=F32)
        if nk == 1:
            finish(part)
        else:
            k = pl.program_id(2)

            @pl.when(k == 0)
            def _():
                acc_ref[...] = part

            @pl.when(jnp.logical_and(k > 0, k < nk - 1))
            def _():
                acc_ref[...] += part

            @pl.when(k == nk - 1)
            def _():
                finish(acc_ref[...] + part)

    in_specs = [a_spec, b_spec] + ([add_spec] if has_add else []) + list(extra_specs)
    operands = [a, b] + ([add] if has_add else []) + list(extras)
    scratch = [pltpu.VMEM(out_spec.block_shape, F32)] if nk > 1 else []
    sem = ("parallel", "parallel") + (("arbitrary",) if nk > 1 else ())
    return pl.pallas_call(
        body, name=name, grid=grid, out_shape=out_shape,
        in_specs=in_specs, out_specs=out_spec, scratch_shapes=scratch,
        compiler_params=_params(*sem),
    )(*operands)


def _input_grad(dh, w_in, dr1, after):
    _, seq, dm = dh.shape
    bm, bn = _tile(seq, 1024), _tile(dm, 512)

    def body(dh_ref, w_ref, dr_ref, after_ref, o_ref):
        del after_ref
        o_ref[...] = (lax.dot_general(dh_ref[0], w_ref[:, :dm], NT, preferred_element_type=F32)
                      + lax.dot_general(dh_ref[1], w_ref[:, dm:], NT, preferred_element_type=F32)
                      + ALPHA * dr_ref[...])

    tile_spec = pl.BlockSpec((bm, bn), lambda i, j: (i, j))
    return pl.pallas_call(
        body, name="a_in_proj_dx", grid=(seq // bm, dm // bn),
        out_shape=jax.ShapeDtypeStruct((seq, dm), F32),
        in_specs=[pl.BlockSpec((2, bm, dm), lambda i, j: (0, i, 0)), pl.BlockSpec((bn, 2 * dm), lambda i, j: (j, 0)),
                  tile_spec, ANY],
        out_specs=tile_spec,
        compiler_params=_params("parallel", "parallel"),
    )(dh, w_in, dr1, after)


def _cast_bf16(name, a):
    rows, cols = a.shape
    tr = _tile(rows, 512)

    def body(a_ref, o_ref):
        o_ref[...] = a_ref[...].astype(BF16)

    return pl.pallas_call(
        body, name=name, grid=(rows // tr,),
        out_shape=jax.ShapeDtypeStruct(a.shape, BF16),
        in_specs=[pl.BlockSpec((tr, cols), lambda i: (i, 0))],
        out_specs=pl.BlockSpec((tr, cols), lambda i: (i, 0)),
        compiler_params=_params("parallel"),
    )(a)


def _rope_tables(seq):
    inv_freq = ROPE_THETA ** (-jnp.arange(0, HEAD_DIM, 2, dtype=F32) / HEAD_DIM)
    ang = jnp.arange(seq, dtype=F32)[:, None] * inv_freq[None, :]
    cos, sin = jnp.cos(ang), jnp.sin(ang)
    cos, sin = (jnp.concatenate([t, t, t, t], axis=-1) for t in (cos, sin))
    first_half = (jnp.arange(LANES) % HEAD_DIM < HEAD_DIM // 2)[None, :]
    return cos, jnp.where(first_half, -sin, 0.0), jnp.where(first_half, 0.0, sin)


def _rot(t, sin_a, sin_b):
    return pltpu.roll(t, LANES - HEAD_DIM // 2, 1) * sin_a + pltpu.roll(t, HEAD_DIM // 2, 1) * sin_b


def _rope(t, cos, sin_a, sin_b):
    return t * cos + _rot(t, sin_a, sin_b)


def _rope_transposed(dy, cos, sin_a, sin_b):
    return dy * cos - _rot(dy, sin_a, sin_b)


def _silu_parts(z):
    sig = jax.nn.sigmoid(z)
    return z * sig, sig * (1.0 + z * (1.0 - sig))


def _layer_norm_stats(r):
    mu = jnp.mean(r, axis=-1, keepdims=True)
    d = r - mu
    var = jnp.mean(d * d, axis=-1, keepdims=True)
    rstd = lax.rsqrt(var + LN_EPS)
    return d * rstd, rstd


def _layer_norm_backward(dout, xhat, rstd, gain):
    dxh = dout * gain
    m1 = jnp.mean(dxh, axis=-1, keepdims=True)
    m2 = jnp.mean(dxh * xhat, axis=-1, keepdims=True)
    return rstd * (dxh - m1 - xhat * m2)


def _col_sum(v):
    return jnp.sum(v, axis=0, keepdims=True)


ROW_PART = 128


def _row_parts(tile):
    part = min(tile, ROW_PART)
    return [slice(r, r + part) for r in range(0, tile, part)]


def _pool_forward(xb, w_in, wg, scale):
    seq, dm = xb.shape
    n_g = len(POOL_WINDOWS)
    gd = dm // n_g
    tile = _tile(seq, 1024)
    halo_blocks = tile // POOL_HALO

    def body(x_ref, xp_ref, wu_ref, wz_ref, wg_ref, sc_ref, y_ref, p_ref, mx_ref, z_ref):
        i, g = pl.program_id(0), pl.program_id(1)
        u = jnp.dot(x_ref[...], wu_ref[...], preferred_element_type=F32)
        z = jnp.dot(x_ref[...], wz_ref[...], preferred_element_type=F32)
        prev = jnp.where(i > 0, jnp.dot(xp_ref[...], wu_ref[...], preferred_element_type=F32), 0.0)
        s = jnp.concatenate([prev, u], axis=0)
        sums, sh = [], 1
        while sh < POOL_WINDOWS[-1]:
            s = s + pltpu.roll(s, sh, 0)
            sums.append(s)
            sh *= 2
        win = sums[-1]
        for k in range(n_g - 2, -1, -1):
            win = jnp.where(g == k, sums[k], win)
        row = i * tile + lax.broadcasted_iota(jnp.int32, (tile, 1), 0)
        window = jnp.left_shift(2, g).astype(F32)
        p = win[POOL_HALO:, :] * (1.0 / jnp.minimum((row + 1).astype(F32), window)) - u
        pb = p.astype(BF16)
        mx = jnp.dot(pb, wg_ref[...], preferred_element_type=F32)
        y_ref[...] = (mx * sc_ref[...] * (z * jax.nn.sigmoid(z))).astype(BF16)
        p_ref[...] = pb
        mx_ref[...] = mx
        z_ref[...] = z

    out_spec = pl.BlockSpec((tile, gd), lambda i, g: (i, g))
    return pl.pallas_call(
        body, name="pool_fwd", grid=(seq // tile, n_g),
        out_shape=(jax.ShapeDtypeStruct((seq, dm), BF16), jax.ShapeDtypeStruct((seq, dm), BF16),
                   jax.ShapeDtypeStruct((seq, dm), F32), jax.ShapeDtypeStruct((seq, dm), F32)),
        in_specs=[pl.BlockSpec((tile, dm), lambda i, g: (i, 0)),
                  pl.BlockSpec((POOL_HALO, dm), lambda i, g: (jnp.maximum(i * halo_blocks - 1, 0), 0)),
                  pl.BlockSpec((dm, gd), lambda i, g: (0, g)),
                  pl.BlockSpec((dm, gd), lambda i, g: (0, n_g + g)),
                  pl.BlockSpec((None, gd, gd), lambda i, g: (g, 0, 0)),
                  pl.BlockSpec((1, gd), lambda i, g: (0, g))],
        out_specs=(out_spec, out_spec, out_spec, out_spec),
        compiler_params=_params("parallel", "parallel"),
    )(xb, xb, w_in, w_in, wg, scale)


def _pool_mid_backward(dr, w_out, mx, z, p, wg, scale, after):
    seq, dm = mx.shape
    gd = dm // len(POOL_WINDOWS)
    tile = _tile(seq, 256)
    n_i = seq // tile

    def body(dr_ref, wout_ref, mx_ref, z_ref, p_ref, wg_ref, sc_ref, after_ref, dh_ref, dwg_ref, st_ref, dwg_acc,
             carry):
        del after_ref
        i = pl.program_id(0)

        def dy_of(g):
            return lax.dot_general(dr_ref[...], wout_ref[g * gd:(g + 1) * gd, :], NT, preferred_element_type=F32)

        dy_ahead = dy_of(0)
        ti = n_i - 1 - i

        @pl.when(i == 0)
        def _():
            dwg_acc[...] = jnp.zeros_like(dwg_acc)
            carry[...] = jnp.zeros_like(carry)
            st_ref[...] = jnp.zeros_like(st_ref)

        row = ti * tile + lax.broadcasted_iota(jnp.int32, (tile, 1), 0)
        count = (row + 1).astype(F32)
        for g, w in enumerate(POOL_WINDOWS):
            cs = slice(g * gd, (g + 1) * gd)
            z = z_ref[:, cs]
            sz, dsz = _silu_parts(z)
            dyg = dy_ahead
            if g + 1 < len(POOL_WINDOWS):
                dy_ahead = dy_of(g + 1)
            mxg = mx_ref[:, cs]
            sc = sc_ref[:, cs]
            t1 = dyg * sz
            st_ref[0:1, cs] += _col_sum(t1 * mxg)
            dh_ref[1, :, cs] = (dyg * (mxg * sc) * dsz).astype(BF16)
            dmx = (t1 * sc).astype(BF16)
            dwg_acc[g] += lax.dot_general(p_ref[:, cs], dmx, TN, preferred_element_type=F32)
            dp = lax.dot_general(dmx, wg_ref[g], NT, preferred_element_type=F32)
            e = dp * (1.0 / jnp.minimum(count, float(w)))
            s = jnp.concatenate([e, carry[:, cs]], axis=0)
            n = tile + POOL_HALO
            sh = 1
            while sh < w:
                s = s + pltpu.roll(s, n - sh, 0)
                sh *= 2
            dh_ref[0, :, cs] = (s[:tile, :] - dp).astype(BF16)
            carry[:, cs] = e[:POOL_HALO, :]

        @pl.when(i == n_i - 1)
        def _():
            dwg_ref[...] = dwg_acc[...].astype(BF16)

    row_spec = pl.BlockSpec((tile, dm), lambda i: (n_i - 1 - i, 0))
    return pl.pallas_call(
        body, name="pool_mid_bwd", grid=(n_i,),
        out_shape=(jax.ShapeDtypeStruct((2, seq, dm), BF16), jax.ShapeDtypeStruct(wg.shape, BF16),
                   jax.ShapeDtypeStruct((STAT_ROWS, dm), F32)),
        in_specs=[row_spec, pl.BlockSpec(w_out.shape, lambda i: (0, 0), pipeline_mode=pl.Buffered(1)),
                  row_spec, row_spec, row_spec,
                  pl.BlockSpec(wg.shape, lambda i: (0, 0, 0)),
                  pl.BlockSpec((1, dm), lambda i: (0, 0)), ANY],
        out_specs=(pl.BlockSpec((2, tile, dm), lambda i: (0, n_i - 1 - i, 0)),
                   pl.BlockSpec(wg.shape, lambda i: (0, 0, 0)),
                   pl.BlockSpec((STAT_ROWS, dm), lambda i: (0, 0))),
        scratch_shapes=[pltpu.VMEM(wg.shape, F32), pltpu.VMEM((POOL_HALO, dm), F32)],
        compiler_params=_params("arbitrary"),
    )(dr, w_out, mx, z, p, wg, scale, after)


def _out_proj_norm(y, w, x, gain, bias):
    seq, dm = x.shape
    tile = _tile(seq, 512)

    def body(y_ref, w_ref, x_ref, g_ref, b_ref, xhat_ref, rstd_ref, xb_ref):
        for rows in _row_parts(tile):
            o = jnp.dot(y_ref[rows, :], w_ref[...], preferred_element_type=F32)
            xhat, rstd = _layer_norm_stats(ALPHA * x_ref[rows, :] + o)
            xhat_ref[rows, :] = xhat
            rstd_ref[rows, :] = rstd
            xb_ref[rows, :] = (xhat * g_ref[...] + b_ref[...]).astype(BF16)

    row_spec = pl.BlockSpec((tile, dm), lambda i: (i, 0))
    vec_spec = pl.BlockSpec((1, dm), lambda i: (0, 0))
    return pl.pallas_call(
        body, name="out_proj_norm_a", grid=(seq // tile,),
        out_shape=(jax.ShapeDtypeStruct((seq, dm), F32), jax.ShapeDtypeStruct((seq, 1), F32),
                   jax.ShapeDtypeStruct((seq, dm), BF16)),
        in_specs=[row_spec, pl.BlockSpec(w.shape, lambda i: (0, 0), pipeline_mode=pl.Buffered(1)), row_spec, vec_spec,
                  vec_spec],
        out_specs=(row_spec, pl.BlockSpec((tile, 1), lambda i: (i, 0)), row_spec),
        compiler_params=_params("parallel"),
    )(y, w, x, gain, bias)


def _kv_proj(xb, wkv, tables):
    seq, dm = xb.shape
    kvw = wkv.shape[1] // 2
    n_kv = kvw // HEAD_DIM
    tile = _tile(seq, 1024)

    def body(x_ref, w_ref, cos_ref, sa_ref, sb_ref, kd_ref, vd_ref, kt_ref, vt_ref):
        kv = jnp.dot(x_ref[...], w_ref[...], preferred_element_type=F32)
        low = lax.broadcasted_iota(jnp.int32, (1, LANES), 1) < HEAD_DIM
        cos, sa, sb = cos_ref[...], sa_ref[...], sb_ref[...]

        def put(pair, h, nat_ref, t_ref):
            swapped = pltpu.roll(pair, HEAD_DIM, 1)
            for head, dup in ((h, jnp.where(low, pair, swapped)), (h + 1, jnp.where(low, swapped, pair))):
                nat_ref[head] = dup.astype(BF16)
                t_ref[head] = dup.T.astype(BF16)

        for j in range(kvw // LANES):
            put(_rope(kv[:, j * LANES:(j + 1) * LANES], cos, sa, sb), 2 * j, kd_ref, kt_ref)
            put(kv[:, kvw + j * LANES:kvw + (j + 1) * LANES], 2 * j, vd_ref, vt_ref)

    tab_spec = pl.BlockSpec((tile, LANES), lambda i: (i, 0))
    dup_spec = pl.BlockSpec((n_kv, tile, LANES), lambda i: (0, i, 0))
    dup_shape = jax.ShapeDtypeStruct((n_kv, seq, LANES), BF16)
    t_spec = pl.BlockSpec((n_kv, LANES, tile), lambda i: (0, 0, i))
    t_shape = jax.ShapeDtypeStruct((n_kv, LANES, seq), BF16)
    return pl.pallas_call(
        body, name="kv_proj", grid=(seq // tile,),
        out_shape=(dup_shape, dup_shape, t_shape, t_shape),
        in_specs=[pl.BlockSpec((tile, dm), lambda i: (i, 0)), pl.BlockSpec(wkv.shape, lambda i: (0, 0)),
                  tab_spec, tab_spec, tab_spec],
        out_specs=(dup_spec, dup_spec, t_spec, t_spec),
        compiler_params=_params("parallel"),
    )(xb, wkv, *tables)


ATTN_STEP_BLOCKS = 16


def _head_queries(q_ref, rows, low):
    parts = []
    for j in range(GQA_GROUP // 2):
        q2 = q_ref[rows, j * LANES:(j + 1) * LANES]
        parts += [jnp.where(low, q2, 0), jnp.where(low, 0, q2)]
    return parts


def _key_window(prev_ref, cur_ref, b, axis):
    def block(i):
        sl = slice(i * ATTN_BLOCK, (i + 1) * ATTN_BLOCK)
        return cur_ref[sl, :] if axis == 0 else cur_ref[:, sl]
    return jnp.concatenate([prev_ref[...] if b == 0 else block(b - 1), block(b)], axis=axis)


def _mask_bias(n):
    key = lax.broadcasted_iota(jnp.int32, (2 * ATTN_BLOCK, ATTN_BLOCK), 0)
    qry = lax.broadcasted_iota(jnp.int32, (2 * ATTN_BLOCK, ATTN_BLOCK), 1)
    valid = (key > qry) & (key <= qry + ATTN_BLOCK) & ((key >= ATTN_BLOCK) | (n > 0))
    return jnp.where(valid, 0.0, NEG_INF)


def _head_probs_transposed(kcat, qm, bias, sink):
    st = lax.dot_general(kcat, qm, NT, preferred_element_type=F32) + bias
    m = jnp.maximum(jnp.max(st, axis=0, keepdims=True), sink)
    e = jnp.exp(st - m)
    e_sink = jnp.exp(sink - m)
    inv = 1.0 / (jnp.sum(e, axis=0, keepdims=True) + e_sink)
    return e * inv, e_sink * inv


def _probs_transposed(n, kh, kcat, q_all, sink_ref):
    st = lax.dot_general(kcat, q_all, NT, preferred_element_type=F32)
    st = st + jnp.tile(_mask_bias(n), (1, GQA_GROUP))
    sink = jnp.concatenate([jnp.full((1, ATTN_BLOCK), sink_ref[0, kh * GQA_GROUP + h], F32)
                            for h in range(GQA_GROUP)], axis=1)
    m = jnp.maximum(jnp.max(st, axis=0, keepdims=True), sink)
    e = jnp.exp(st - m)
    e_sink = jnp.exp(sink - m)
    inv = 1.0 / (jnp.sum(e, axis=0, keepdims=True) + e_sink)
    return e * inv, e_sink * inv


def _attn_specs(n_width, qb):
    rows = qb * ATTN_BLOCK
    before = lambda n: jnp.maximum(n * qb - 1, 0)
    q_spec = pl.BlockSpec((rows, n_width), lambda kh, n: (n, kh))
    cur = pl.BlockSpec((None, rows, LANES), lambda kh, n: (kh, n, 0))
    prev = pl.BlockSpec((None, ATTN_BLOCK, LANES), lambda kh, n: (kh, before(n), 0))
    cur_t = pl.BlockSpec((None, LANES, rows), lambda kh, n: (kh, 0, n))
    prev_t = pl.BlockSpec((None, LANES, ATTN_BLOCK), lambda kh, n: (kh, 0, before(n)))
    return q_spec, cur, prev, cur_t, prev_t


def _pair_product_transposed(mat_t, rhs, j, low_rows):
    head_a = slice(2 * j * ATTN_BLOCK, (2 * j + 1) * ATTN_BLOCK)
    head_b = slice((2 * j + 1) * ATTN_BLOCK, (2 * j + 2) * ATTN_BLOCK)
    out_t = (jnp.dot(jnp.where(low_rows, mat_t, 0), rhs[:, head_a], preferred_element_type=F32)
             + jnp.dot(jnp.where(low_rows, 0, mat_t), rhs[:, head_b], preferred_element_type=F32))
    return out_t.T


def _attn_forward(qs, kd, vt, zb, sinks):
    seq, dm = qs.shape
    n_kv = kd.shape[0]
    gw = GQA_GROUP * HEAD_DIM

    qb = ATTN_STEP_BLOCKS if (seq // ATTN_BLOCK) % ATTN_STEP_BLOCKS == 0 else 1

    def body(q_ref, kp_ref, kc_ref, vtp_ref, vtc_ref, z_ref, sink_ref, att_ref, yb_ref):
        kh, n = pl.program_id(0), pl.program_id(1)
        low = lax.broadcasted_iota(jnp.int32, (1, LANES), 1) < HEAD_DIM
        low_rows = lax.broadcasted_iota(jnp.int32, (LANES, 1), 0) < HEAD_DIM
        for b in range(qb):
            rows = slice(b * ATTN_BLOCK, (b + 1) * ATTN_BLOCK)
            kcat = _key_window(kp_ref, kc_ref, b, 0)
            vt = _key_window(vtp_ref, vtc_ref, b, 1)
            bias = _mask_bias(n * qb + b)
            pt = jnp.concatenate(
                [_head_probs_transposed(kcat, qm, bias, sink_ref[0, kh * GQA_GROUP + h])[0].astype(BF16)
                 for h, qm in enumerate(_head_queries(q_ref, rows, low))], axis=1)
            for j in range(GQA_GROUP // 2):
                cs = slice(j * LANES, (j + 1) * LANES)
                o2 = _pair_product_transposed(vt, pt, j, low_rows)
                att_ref[rows, cs] = o2
                z = z_ref[rows, cs]
                yb_ref[rows, cs] = (o2 * (z * jax.nn.sigmoid(z))).astype(BF16)

    q_spec, cur, prev, cur_t, prev_t = _attn_specs(gw, qb)
    return pl.pallas_call(
        body, name="attn_fwd", grid=(n_kv, seq // (qb * ATTN_BLOCK)),
        out_shape=(jax.ShapeDtypeStruct((seq, dm), F32), jax.ShapeDtypeStruct((seq, dm), BF16)),
        in_specs=[q_spec, prev, cur, prev_t, cur_t, q_spec, pl.BlockSpec(memory_space=pltpu.SMEM)],
        out_specs=(q_spec, q_spec),
        compiler_params=_params("parallel", "parallel"),
    )(qs, kd, kd, vt, vt, zb, sinks)


def _attn_backward(qs, kd, vd, kt, zb, att, dyb, sinks, tables, after):
    seq, dm = qs.shape
    n_kv = kd.shape[0]
    gw = GQA_GROUP * HEAD_DIM
    n_blocks = seq // ATTN_BLOCK
    qb = ATTN_STEP_BLOCKS if n_blocks % ATTN_STEP_BLOCKS == 0 else 1

    def body(q_ref, kp_ref, kc_ref, vp_ref, vc_ref, ktp_ref, ktc_ref, z_ref, att_ref, dyb_ref, sink_ref,
             cos_ref, sa_ref, sb_ref, after_ref, dh_ref, dk_ref, dv_ref, ds_ref):
        del after_ref
        kh, n = pl.program_id(0), pl.program_id(1)

        @pl.when(n == 0)
        def _():
            dk_ref[...] = jnp.zeros_like(dk_ref)
            dv_ref[...] = jnp.zeros_like(dv_ref)

        @pl.when(jnp.logical_and(n == 0, kh == 0))
        def _():
            ds_ref[...] = jnp.zeros_like(ds_ref)

        low = lax.broadcasted_iota(jnp.int32, (1, LANES), 1) < HEAD_DIM
        low_rows = lax.broadcasted_iota(jnp.int32, (LANES, 1), 0) < HEAD_DIM
        head_lane = lax.broadcasted_iota(jnp.int32, (1, LANES), 1)
        dsink = jnp.zeros((1, LANES), F32)
        for b in range(qb):
            rows = slice(b * ATTN_BLOCK, (b + 1) * ATTN_BLOCK)
            kcat = _key_window(kp_ref, kc_ref, b, 0)
            vcat = _key_window(vp_ref, vc_ref, b, 0)
            kt = _key_window(ktp_ref, ktc_ref, b, 1)
            cos, sa, sb = cos_ref[rows, :], sa_ref[rows, :], sb_ref[rows, :]
            q_all = jnp.concatenate(_head_queries(q_ref, rows, low), axis=0)
            d_parts = []
            for j in range(GQA_GROUP // 2):
                cs = slice(j * LANES, (j + 1) * LANES)
                sz, dsz = _silu_parts(z_ref[rows, cs])
                dy2 = dyb_ref[rows, cs]
                dh_ref[1, rows, cs] = (dy2 * att_ref[rows, cs] * dsz).astype(BF16)
                datt = (dy2 * sz).astype(BF16)
                d_parts += [jnp.where(low, datt, 0), jnp.where(low, 0, datt)]
            d_all = jnp.concatenate(d_parts, axis=0)
            probs_t, sink_p = _probs_transposed(n * qb + b, kh, kcat, q_all, sink_ref)
            dprobs_t = lax.dot_general(vcat, d_all, NT, preferred_element_type=F32)
            row_dot = jnp.sum(probs_t * dprobs_t, axis=0, keepdims=True)
            ds_t = (probs_t * (dprobs_t - row_dot)).astype(BF16)
            dk = jnp.dot(ds_t, q_all, preferred_element_type=F32)
            dv = jnp.dot(probs_t.astype(BF16), d_all, preferred_element_type=F32)
            for j in range(GQA_GROUP // 2):
                dq2 = _pair_product_transposed(kt, ds_t, j, low_rows)
                dh_ref[0, rows, j * LANES:(j + 1) * LANES] = (
                    _rope_transposed(dq2, cos, sa, sb) * 0.125).astype(BF16)
            sink_dot = sink_p * row_dot
            for h in range(GQA_GROUP):
                part = jnp.sum(sink_dot[:, h * ATTN_BLOCK:(h + 1) * ATTN_BLOCK], axis=1, keepdims=True)
                dsink = dsink - jnp.where(head_lane == kh * GQA_GROUP + h, part, 0.0)

            def add_window(dk=dk, dv=dv, b=b):
                start = pl.multiple_of((n * qb + b - 1) * ATTN_BLOCK, ATTN_BLOCK)
                dk_ref[pl.ds(start, 2 * ATTN_BLOCK), :] += dk
                dv_ref[pl.ds(start, 2 * ATTN_BLOCK), :] += dv

            if b > 0:
                add_window()
            else:
                pl.when(n > 0)(add_window)

                @pl.when(n == 0)
                def _(dk=dk, dv=dv):
                    dk_ref[pl.ds(0, ATTN_BLOCK), :] += dk[ATTN_BLOCK:, :]
                    dv_ref[pl.ds(0, ATTN_BLOCK), :] += dv[ATTN_BLOCK:, :]
        ds_ref[0:1, :] += dsink

    q_spec, cur, prev, cur_t, prev_t = _attn_specs(gw, qb)
    tab_spec = pl.BlockSpec((qb * ATTN_BLOCK, LANES), lambda kh, n: (n, 0))
    acc_spec = pl.BlockSpec((None, seq, LANES), lambda kh, n: (kh, 0, 0))
    acc_shape = jax.ShapeDtypeStruct((n_kv, seq, LANES), F32)
    return pl.pallas_call(
        body, name="attn_bwd", grid=(n_kv, n_blocks // qb),
        out_shape=(jax.ShapeDtypeStruct((2, seq, dm), BF16), acc_shape, acc_shape,
                   jax.ShapeDtypeStruct((STAT_ROWS, LANES), F32)),
        in_specs=[q_spec, prev, cur, prev, cur, prev_t, cur_t, q_spec, q_spec, q_spec,
                  pl.BlockSpec(memory_space=pltpu.SMEM), tab_spec, tab_spec, tab_spec, ANY],
        out_specs=(pl.BlockSpec((2, qb * ATTN_BLOCK, gw), lambda kh, n: (0, n, kh)), acc_spec, acc_spec,
                   pl.BlockSpec((STAT_ROWS, LANES), lambda kh, n: (0, 0))),
        compiler_params=_params("arbitrary", "arbitrary"),
    )(qs, kd, kd, vd, vd, kt, kt, zb, att, dyb, sinks, *tables, after)


def _kv_grad_fold(dk, dv, tables):
    n_kv, seq, _ = dk.shape
    kvw = n_kv * HEAD_DIM
    tile = _tile(seq, 512)

    def body(dk_ref, dv_ref, cos_ref, sa_ref, sb_ref, o_ref):
        low = lax.broadcasted_iota(jnp.int32, (1, LANES), 1) < HEAD_DIM
        cos, sa, sb = cos_ref[...], sa_ref[...], sb_ref[...]

        def folded(ref, h):
            t = ref[h]
            return t + pltpu.roll(t, HEAD_DIM, 1)

        for j in range(n_kv // 2):
            ka = _rope_transposed(folded(dk_ref, 2 * j), cos, sa, sb)
            kb = _rope_transposed(folded(dk_ref, 2 * j + 1), cos, sa, sb)
            o_ref[:, j * LANES:(j + 1) * LANES] = jnp.where(low, ka, kb).astype(BF16)
            o_ref[:, kvw + j * LANES:kvw + (j + 1) * LANES] = jnp.where(
                low, folded(dv_ref, 2 * j), folded(dv_ref, 2 * j + 1)).astype(BF16)

    tab_spec = pl.BlockSpec((tile, LANES), lambda i: (i, 0))
    in_spec = pl.BlockSpec((n_kv, tile, LANES), lambda i: (0, i, 0))
    return pl.pallas_call(
        body, name="kv_grad_fold", grid=(seq // tile,),
        out_shape=jax.ShapeDtypeStruct((seq, 2 * kvw), BF16),
        in_specs=[in_spec, in_spec, tab_spec, tab_spec, tab_spec],
        out_specs=pl.BlockSpec((tile, 2 * kvw), lambda i: (i, 0)),
        compiler_params=_params("parallel"),
    )(dk, dv, *tables)


def _out_proj_norm_loss(yb, w, xhat1, gain0, bias0, gain1, bias1, target):
    seq, dm = xhat1.shape
    tile = _tile(seq, 512)

    def body(y_ref, w_ref, xh1_ref, g0_ref, b0_ref, g1_ref, b1_ref, t_ref, dr_ref, drb_ref, st_ref):
        i = pl.program_id(0)

        @pl.when(i == 0)
        def _():
            st_ref[...] = jnp.zeros_like(st_ref)

        parts = _row_parts(tile)
        product = lambda rows: jnp.dot(y_ref[rows, :], w_ref[...], preferred_element_type=F32)
        ahead = product(parts[0])
        for k, rows in enumerate(parts):
            ob = ahead
            if k + 1 < len(parts):
                ahead = product(parts[k + 1])
            x1 = xh1_ref[rows, :] * g0_ref[...] + b0_ref[...]
            xhat, rstd = _layer_norm_stats(ALPHA * x1 + ob)
            err = xhat * g1_ref[...] + b1_ref[...] - t_ref[rows, :]
            dout = err * (1.0 / dm)
            dr = _layer_norm_backward(dout, xhat, rstd, g1_ref[...])
            dr_ref[rows, :] = dr
            drb_ref[rows, :] = dr.astype(BF16)
            st_ref[0:1, :] += _col_sum(dout * xhat)
            st_ref[1:2, :] += _col_sum(dout)
            st_ref[2:3, :] += _col_sum(err * err)

    row_spec = pl.BlockSpec((tile, dm), lambda i: (i, 0))
    vec_spec = pl.BlockSpec((1, dm), lambda i: (0, 0))
    return pl.pallas_call(
        body, name="out_proj_norm_loss_b", grid=(seq // tile,),
        out_shape=(jax.ShapeDtypeStruct((seq, dm), F32), jax.ShapeDtypeStruct((seq, dm), BF16),
                   jax.ShapeDtypeStruct((STAT_ROWS, dm), F32)),
        in_specs=[row_spec, pl.BlockSpec(w.shape, lambda i: (0, 0), pipeline_mode=pl.Buffered(1)), row_spec, vec_spec,
                  vec_spec, vec_spec,
                  vec_spec, row_spec],
        out_specs=(row_spec, row_spec, pl.BlockSpec((STAT_ROWS, dm), lambda i: (0, 0))),
        compiler_params=_params("arbitrary"),
    )(yb, w, xhat1, gain0, bias0, gain1, bias1, target)


def _stream_grad_norm_backward(dhq, wqg, dkv, wkv, dr2, xhat1, rstd1, gain0, after):
    seq, dm = dr2.shape
    tile = _tile(seq, 256)

    def body(dh_ref, wqg_ref, dkv_ref, wkv_ref, dr2_ref, xh_ref, rstd_ref, g_ref, after_ref, dr_ref, drb_ref, st_ref):
        del after_ref

        @pl.when(pl.program_id(0) == 0)
        def _():
            st_ref[...] = jnp.zeros_like(st_ref)

        dx1 = (lax.dot_general(dh_ref[0], wqg_ref[:, :dm], NT, preferred_element_type=F32)
               + lax.dot_general(dh_ref[1], wqg_ref[:, dm:], NT, preferred_element_type=F32)
               + lax.dot_general(dkv_ref[...], wkv_ref[...], NT, preferred_element_type=F32)
               + ALPHA * dr2_ref[...])
        xhat = xh_ref[...]
        dr = _layer_norm_backward(dx1, xhat, rstd_ref[...], g_ref[...])
        dr_ref[...] = dr
        drb_ref[...] = dr.astype(BF16)
        st_ref[0:1, :] += _col_sum(dx1 * xhat)
        st_ref[1:2, :] += _col_sum(dx1)

    row_spec = pl.BlockSpec((tile, dm), lambda i: (i, 0))
    resident = pl.Buffered(1)
    return pl.pallas_call(
        body, name="stream_grad_norm_bwd", grid=(seq // tile,),
        out_shape=(jax.ShapeDtypeStruct((seq, dm), F32), jax.ShapeDtypeStruct((seq, dm), BF16),
                   jax.ShapeDtypeStruct((STAT_ROWS, dm), F32)),
        in_specs=[pl.BlockSpec((2, tile, dm), lambda i: (0, i, 0)),
                  pl.BlockSpec(wqg.shape, lambda i: (0, 0), pipeline_mode=resident),
                  pl.BlockSpec((tile, dkv.shape[1]), lambda i: (i, 0)),
                  pl.BlockSpec(wkv.shape, lambda i: (0, 0), pipeline_mode=resident),
                  row_spec, row_spec, pl.BlockSpec((tile, 1), lambda i: (i, 0)),
                  pl.BlockSpec((1, dm), lambda i: (0, 0)), ANY],
        out_specs=(row_spec, row_spec, pl.BlockSpec((STAT_ROWS, dm), lambda i: (0, 0))),
        compiler_params=_params("arbitrary"),
    )(dhq, wqg, dkv, wkv, dr2, xhat1, rstd1, gain0, after)


def _adamw_math(w, g, m, v):
    m = ADAM_B1 * m + (1.0 - ADAM_B1) * g
    v = ADAM_B2 * v + (1.0 - ADAM_B2) * (g * g)
    m_hat = m / (1.0 - ADAM_B1 ** ADAM_STEP)
    v_hat = v / (1.0 - ADAM_B2 ** ADAM_STEP)
    delta = -ADAM_LR * (m_hat / (jnp.sqrt(v_hat) + ADAM_EPS) + ADAM_WD * w)
    return delta, m, v


def _sum_devices(ref):
    total = ref[0].astype(F32)
    for d in range(1, ref.shape[0]):
        total = total + ref[d].astype(F32)
    return total


def _adamw_shard(name, parts, w, m, v, after):
    rows, cols = w.shape
    n_parts = len(parts)
    part_rows = rows // n_parts
    tr = _tile(part_rows, max(8, (1 << 18) // cols)) if part_rows >= 8 else part_rows
    per_part = part_rows // tr

    def body(*refs):
        p_refs = refs[:n_parts]
        w_ref, m_ref, v_ref, _, g_out, d_out, m_out, v_out = refs[n_parts:]
        g = _sum_devices(p_refs[0])
        for k in range(1, n_parts):
            g = jnp.where(pl.program_id(0) >= k * per_part, _sum_devices(p_refs[k]), g)
        delta, m_new, v_new = _adamw_math(w_ref[...], g, m_ref[...], v_ref[...])
        g_out[...] = g
        d_out[...] = delta
        m_out[...] = m_new
        v_out[...] = v_new

    def part_spec(k):
        return pl.BlockSpec((parts[k].shape[0], tr, cols),
                            lambda i: (0, jnp.clip(i - k * per_part, 0, per_part - 1), 0))

    spec = pl.BlockSpec((tr, cols), lambda i: (i, 0))
    shape = jax.ShapeDtypeStruct((rows, cols), F32)
    return pl.pallas_call(
        body, name=name, grid=(rows // tr,),
        out_shape=(shape, shape, shape, shape),
        in_specs=[part_spec(k) for k in range(n_parts)] + [spec, spec, spec, ANY],
        out_specs=(spec, spec, spec, spec),
        compiler_params=_params("arbitrary"),
    )(*parts, w, m, v, after)


def _adamw_replicated(stats_b, stats_a, sink_parts, ln_g, ln_b, sinks, m_ln_g, m_ln_b, m_sinks, v_ln_g, v_ln_b,
                      v_sinks, after):
    n_q = sinks.shape[1]
    dm = ln_g.shape[1]

    def body(sb_ref, sa_ref, sk_ref, g_ref, b_ref, s_ref, mg_ref, mb_ref, ms_ref, vg_ref, vb_ref, vs_ref, after_ref,
             *outs):
        del after_ref
        layer_sums = (_sum_devices(sa_ref), _sum_devices(sb_ref))
        outs[12][...] = jnp.sum(layer_sums[1][2:3, :], axis=1, keepdims=True) * (0.5 / dm)
        for which, (w_ref, m_ref, v_ref) in enumerate(((g_ref, mg_ref, vg_ref), (b_ref, mb_ref, vb_ref))):
            for layer in range(DEPTH):
                row = slice(layer, layer + 1)
                g = layer_sums[layer][which:which + 1, :]
                res = (g,) + _adamw_math(w_ref[row, :], g, m_ref[row, :], v_ref[row, :])
                for o_ref, val in zip(outs[4 * which:4 * which + 4], res):
                    o_ref[row, :] = val
        g = _sum_devices(sk_ref)[0:1, 0:n_q]
        res = (g,) + _adamw_math(s_ref[...], g, ms_ref[...], vs_ref[...])
        for o_ref, val in zip(outs[8:12], res):
            o_ref[...] = val

    vmem = pl.BlockSpec(memory_space=pltpu.VMEM)
    shapes = [jax.ShapeDtypeStruct(a.shape, F32) for a in (ln_g, ln_b, sinks) for _ in range(4)]
    shapes.append(jax.ShapeDtypeStruct((1, 1), F32))
    return pl.pallas_call(
        body, name="adamw_replicated", out_shape=tuple(shapes),
        in_specs=[vmem] * 12 + [ANY], out_specs=tuple([vmem] * 13),
    )(stats_b, stats_a, sink_parts, ln_g, ln_b, sinks, m_ln_g, m_ln_b, m_sinks, v_ln_g, v_ln_b, v_sinks, after)


def kernel(x, ln_g, ln_b, a_w_in, a_w_group, a_scale, a_w_out, b_w_k, b_w_v, b_w_qg, b_sinks, b_w_out, loss_target, m_ln_g, m_ln_b, m_a_w_in, m_a_w_group, m_a_scale, m_a_w_out, m_b_w_k, m_b_w_v, m_b_w_qg, m_b_sinks, m_b_w_out, v_ln_g, v_ln_b, v_a_w_in, v_a_w_group, v_a_scale, v_a_w_out, v_b_w_k, v_b_w_v, v_b_w_qg, v_b_sinks, v_b_w_out):
    _, seq, dm = x.shape
    n_groups = len(POOL_WINDOWS)
    gd = dm // n_groups
    kvw = b_w_k.shape[1]
    cb = 2 * dm // N_DEV
    rb = dm // N_DEV
    gb = gd // N_DEV

    x2 = x.reshape(seq, dm)
    target = loss_target.reshape(seq, dm)
    w_in_s = a_w_in.reshape(dm, cb)
    w_g_s = a_w_group.reshape(n_groups, gb, gd)
    w_out_s = a_w_out.reshape(rb, dm)
    w_qg_s = b_w_qg.reshape(dm, cb)
    w_outb_s = b_w_out.reshape(rb, dm)

    def cols(ref, dev):
        return ref.at[:, pl.ds(pl.multiple_of(dev * cb, LANES), cb)]

    def rows(ref, dev):
        return ref.at[pl.ds(pl.multiple_of(dev * rb, 8), rb), :]

    def group_rows(ref, dev):
        return ref.at[:, pl.ds(pl.multiple_of(dev * gb, 8), gb), :]

    def k_rows(ref, dev):
        return ref.at[pl.ds(pl.multiple_of(dev * rb, 8), rb), pl.ds(0, kvw)]

    def v_rows(ref, dev):
        return ref.at[pl.ds(pl.multiple_of(dev * rb, 8), rb), pl.ds(kvw, kvw)]

    def scale_cols(ref, dev):
        return ref.at[:, pl.ds(pl.multiple_of(dev * rb, LANES), rb)]

    bf = lambda a: a.astype(BF16)
    wide, square = jax.ShapeDtypeStruct((dm, 2 * dm), BF16), jax.ShapeDtypeStruct((dm, dm), BF16)
    w_g, scale, w_in = _gather_weights(
        "gather_a_in", 0, [(bf(w_g_s), 0, group_rows), (a_scale, 1, scale_cols), (bf(w_in_s), 2, cols)],
        [jax.ShapeDtypeStruct((n_groups, gd, gd), BF16), jax.ShapeDtypeStruct((1, dm), F32), wide])
    (w_out,) = _gather_weights("gather_a_out", 1, [(bf(w_out_s), 0, rows)], [square])
    w_kv, w_qg = _gather_weights(
        "gather_b_in", 2, [(bf(b_w_k), 0, k_rows), (bf(b_w_v), 0, v_rows), (bf(w_qg_s), 1, cols)],
        [jax.ShapeDtypeStruct((dm, 2 * kvw), BF16), wide])
    (w_outb,) = _gather_weights("gather_b_out", 3, [(bf(w_outb_s), 0, rows)], [square])

    tables = _rope_tables(seq)
    bm = _tile(seq, 2048)
    bn = _tile(dm, 1024)
    g0, g1, b0, b1 = ln_g[0:1], ln_g[1:2], ln_b[0:1], ln_b[1:2]

    xb = _cast_bf16("cast_x", x2)
    y, pooled, mixed, z_a = _pool_forward(xb, w_in, w_g, scale)
    xhat1, rstd1, x1b = _out_proj_norm(y, w_out, x2, g0, b0)

    kd, vd, kt, vt = _kv_proj(x1b, w_kv, tables)
    bmq = bm
    tab_spec = pl.BlockSpec((bmq, LANES), lambda i, j: (i, 0))

    def rope_scale(val, cos_ref, sa_ref, sb_ref):
        cos, sa, sb = cos_ref[...], sa_ref[...], sb_ref[...]
        return jnp.concatenate([_rope(val[:, j * LANES:(j + 1) * LANES], cos, sa, sb) * 0.125
                                for j in range(val.shape[1] // LANES)], axis=1)

    qs = _mm("b_q_proj", x1b, w_qg, dims=NN, grid=(seq // bmq, dm // bn),
             a_spec=pl.BlockSpec((bmq, dm), lambda i, j: (i, 0)), b_spec=pl.BlockSpec((dm, bn), lambda i, j: (0, j)),
             out_shape=jax.ShapeDtypeStruct((seq, dm), BF16), out_spec=pl.BlockSpec((bmq, bn), lambda i, j: (i, j)),
             epilogue=rope_scale, extras=tables, extra_specs=(tab_spec,) * 3)
    zb = _mm("b_gate_proj", x1b, w_qg, dims=NN, grid=(seq // bm, dm // bn),
             a_spec=pl.BlockSpec((bm, dm), lambda i, j: (i, 0)),
             b_spec=pl.BlockSpec((dm, bn), lambda i, j: (0, j + dm // bn)),
             out_shape=jax.ShapeDtypeStruct((seq, dm), F32), out_spec=pl.BlockSpec((bm, bn), lambda i, j: (i, j)))
    att, yb = _attn_forward(qs, kd, vt, zb, b_sinks)
    dr2, dr2b, stats_b = _out_proj_norm_loss(yb, w_outb, xhat1, g0, b0, g1, b1, target)

    def weight_grad(name, a, b, n_cols, b_spec=None, part=(0, 1), after=None):
        m_cols = a.shape[1] // part[1]
        tm, tn = _tile(m_cols, 1024), _tile(n_cols, 1024)
        first = part[0] * (m_cols // tm)
        return _mm(name, a, b, dims=TN, grid=(m_cols // tm, n_cols // tn),
                   a_spec=pl.BlockSpec((seq, tm), lambda i, j: (0, first + i)),
                   b_spec=b_spec(tn) if b_spec else pl.BlockSpec((seq, tn), lambda i, j: (0, j)),
                   out_shape=jax.ShapeDtypeStruct((m_cols, n_cols), BF16),
                   out_spec=pl.BlockSpec((tm, tn), lambda i, j: (i, j)),
                   extras=() if after is None else (after,), extra_specs=() if after is None else (ANY,))

    def halves_spec(tn):
        per = dm // tn
        return pl.BlockSpec((None, seq, tn), lambda i, j: (j // per, 0, j % per))

    def times_transposed(name, a, w):
        return _mm(name, a, w, dims=NT, grid=(seq // bm, dm // bn),
                   a_spec=pl.BlockSpec((bm, a.shape[1]), lambda i, j: (i, 0)),
                   b_spec=pl.BlockSpec((bn, w.shape[1]), lambda i, j: (j, 0)),
                   out_shape=jax.ShapeDtypeStruct((seq, dm), F32), out_spec=pl.BlockSpec((bm, bn), lambda i, j: (i, j)))

    def stat_row_cols(ref, dev):
        return ref.at[pl.ds(0, 1), pl.ds(pl.multiple_of(dev * rb, LANES), rb)]

    upd = {}
    last = [dr2b]
    my_core = lax.axis_index("c").astype(jnp.int32).reshape(1)

    def then(value):
        last[0] = value[0] if isinstance(value, (list, tuple)) else value
        return value

    def shard_update(key, parts, w, m, v):
        shape = w.shape
        flat = lambda a: a.reshape(-1, shape[-1])
        parts = list(parts) if isinstance(parts, (list, tuple)) else [parts]
        outs = then(_adamw_shard("adamw_" + key, [p.reshape(p.shape[0], -1, shape[-1]) for p in parts], flat(w),
                                 flat(m), flat(v), last[0]))
        upd[key] = [o.reshape(shape) for o in outs]

    def two_level_scatter(name, ids, streams):
        staged = _sibling_exchange(name + "_pair", streams, ids[0])

        def finish():
            sums = [then(_pair_sum(f"{name}_sum{s}", st[0], got, my_core, last[0]))
                    for s, (st, got) in enumerate(zip(streams, staged))]
            return _chip_exchange(name + "_chip", sums, ids[1])
        return finish

    d_w_outb = then(weight_grad("b_out_proj_dw", yb, dr2b, dm))
    (p_outb,) = _exchange_blocks("scatter_b_out", [(d_w_outb, rows, (rb, dm))], 4)
    dyb = times_transposed("b_out_proj_dx", dr2b, w_outb)
    dhq, dkd, dvd, dsink = then(_attn_backward(qs, kd, vd, kt, zb, att, dyb, b_sinks, tables, after=last[0]))
    dkv = _kv_grad_fold(dkd, dvd, tables)
    d_w_kv = weight_grad("b_kv_proj_dw", x1b, dkv, 2 * kvw)
    d_w_qg = then(weight_grad("b_qg_proj_dw", x1b, dhq, 2 * dm, halves_spec, after=d_w_kv))
    finish_b_in = two_level_scatter("scatter_b_in", (5, 11), [(d_w_qg, cols, (dm, cb))])
    dr1, dr1b, stats_a = _stream_grad_norm_backward(dhq, w_qg, dkv, w_kv, dr2, xhat1, rstd1, g0, after=last[0])
    last[0] = dr1b
    shard_update("b_w_out", p_outb, b_w_out, m_b_w_out, v_b_w_out)
    (p_qg,) = finish_b_in()
    all_b, all_a, all_sink = _exchange_blocks("gather_replicated_grads", [
        (stats_b, None, stats_b.shape), (stats_a, None, stats_a.shape), (dsink, None, dsink.shape)], 9)

    d_w_out = then(weight_grad("a_out_proj_dw", y, dr1b, dm, after=last[0]))
    p_out, p_k, p_v = _exchange_blocks("scatter_a_out", [
        (d_w_out, rows, (rb, dm)), (d_w_kv, k_rows, (rb, kvw)), (d_w_kv, v_rows, (rb, kvw))], 6)
    dh, d_w_g, stats_s = then(_pool_mid_backward(dr1b, w_out, mixed, z_a, pooled, w_g, scale, after=last[0]))
    p_g, p_scale = _exchange_blocks("scatter_a_mid", [
        (d_w_g, group_rows, (n_groups, gb, gd)), (stats_s, stat_row_cols, (1, rb))], 7)
    shard_update("b_w_qg", p_qg, b_w_qg, m_b_w_qg, v_b_w_qg)
    rep = then(_adamw_replicated(all_b, all_a, all_sink, ln_g, ln_b, b_sinks, m_ln_g, m_ln_b, m_b_sinks, v_ln_g,
                                 v_ln_b, v_b_sinks, last[0]))
    upd["ln_g"], upd["ln_b"], upd["b_sinks"] = list(rep[0:4]), list(rep[4:8]), list(rep[8:12])
    finish_a_in = []
    for k in range(2):
        d_w_in = then(weight_grad(f"a_in_proj_dw_{k}", xb, dh, 2 * dm, halves_spec, part=(k, 2), after=last[0]))
        finish_a_in.append(two_level_scatter(f"scatter_a_in_{k}", (8 + 2 * k, 12 + k), [(d_w_in, cols, (dm // 2, cb))]))
    shard_update("a_w_out", p_out, a_w_out, m_a_w_out, v_a_w_out)
    shard_update("b_w_k", p_k, b_w_k, m_b_w_k, v_b_w_k)
    shard_update("b_w_v", p_v, b_w_v, m_b_w_v, v_b_w_v)
    shard_update("a_w_group", p_g, a_w_group, m_a_w_group, v_a_w_group)
    shard_update("a_scale", p_scale, a_scale, m_a_scale, v_a_scale)
    p_in = list(finish_a_in[0]()) + list(finish_a_in[1]())
    grad_x = then(_input_grad(dh, w_in, dr1, last[0]))
    shard_update("a_w_in", p_in, a_w_in, m_a_w_in, v_a_w_in)

    loss = rep[12].reshape(())
    order = ["ln_g", "ln_b", "a_w_in", "a_w_group", "a_scale", "a_w_out", "b_w_k", "b_w_v", "b_w_qg", "b_sinks",
             "b_w_out"]
    return (loss, grad_x.reshape(x.shape), *[upd[n][0] for n in order], *[upd[n][1] for n in order],
            *[upd[n][2] for n in order], *[upd[n][3] for n in order])
```

```python
import functools

import jax
import jax.numpy as jnp
from jax import lax
from jax.experimental import pallas as pl
from jax.experimental.pallas import tpu as pltpu
from jax.experimental.pallas import tpu_sc as plsc

F32 = jnp.float32
BF16 = jnp.bfloat16
MESH = pl.DeviceIdType.MESH
AXES = ("x", "y", "c")
N_DEV = 8

POOL_WINDOWS = (2, 4, 8, 16)
POOL_HALO = 16
HEAD_DIM = 64
GQA_GROUP = 8
ATTN_BLOCK = 128
ROPE_THETA = 10000.0
LN_EPS = 1e-5
NEG_INF = -1e30
DEPTH = 2
ALPHA = (2 * DEPTH) ** 0.25
ADAM_LR = 0.001
ADAM_B1 = 0.9
ADAM_B2 = 0.999
ADAM_EPS = 1e-08
ADAM_WD = 0.01
ADAM_STEP = 10

LANES = 128
STAT_ROWS = 8


def _tile(n, want):
    t = min(n, want)
    while n % t:
        t //= 2
    return t


def _params(*sem):
    return pltpu.CompilerParams(dimension_semantics=sem)


ANY = pl.BlockSpec(memory_space=pl.ANY)


def _my_pos():
    return lax.axis_index("x"), lax.axis_index("y"), lax.axis_index("c")


def _dev_index(p):
    return 4 * p[0] + 2 * p[1] + p[2]


def _handshake(peers):
    barrier = pltpu.get_barrier_semaphore()
    for peer in peers:
        pl.semaphore_signal(barrier, inc=1, device_id=peer, device_id_type=MESH)
    pl.semaphore_wait(barrier, len(peers))


def _launch_on_sequencer(name, collective_id, body, operands, out_shapes, scratch):
    return pl.kernel(
        body, out_type=tuple(out_shapes), name=name,
        mesh=plsc.ScalarSubcoreMesh(axis_name="sequencer", num_cores=1), scratch_types=scratch,
        compiler_params=pltpu.CompilerParams(collective_id=collective_id),
    )(*operands)


def _gather_weights(name, collective_id, streams, out_shapes):
    n_s = len(streams)
    n_out = len(out_shapes)

    def body(*refs):
        srcs = refs[:n_s]
        outs = refs[n_s:n_s + n_out]
        send_sems, recv_sems, local_sems = refs[n_s + n_out:]
        x, y, c = _my_pos()
        me, sibling = (x, y, c), (x, y, 1 - c)
        x_nbr, y_nbr, diag = (1 - x, y), (x, 1 - y), (1 - x, 1 - y)
        _handshake([sibling, (*x_nbr, c), (*y_nbr, c)])
        south = c == 0
        relay_from = (jnp.where(south, 1 - x, x), jnp.where(south, y, 1 - y))
        relay_to = (jnp.where(south, x, 1 - x), jnp.where(south, 1 - y, y))
        early, late = jnp.where(south, 1, 2), jnp.where(south, 2, 1)

        def copy(s, k, block, to, from_shard=False):
            out_ref = outs[streams[s][1]]
            win = streams[s][2](out_ref, _dev_index(block))
            return pltpu.make_async_remote_copy(
                src_ref=srcs[s] if from_shard else win, dst_ref=win,
                send_sem=send_sems.at[7 * s + k], recv_sem=recv_sems.at[7 * s + k],
                device_id=to, device_id_type=MESH)

        mine = [pltpu.make_async_copy(srcs[s], streams[s][2](outs[streams[s][1]], _dev_index(me)), local_sems.at[s])
                for s in range(n_s)]
        for cp in mine:
            cp.start()
        sent = []
        for s in range(n_s):
            sent += [copy(s, 0, me, sibling, True), copy(s, 1, me, (*x_nbr, c), True), copy(s, 2, me, (*y_nbr, c), True)]
        for cp in sent:
            cp.start()
        for s in range(n_s):
            copy(s, early, (*relay_from, c), me).wait_recv()
            sent += [copy(s, 3, (*relay_from, c), (*relay_to, c)), copy(s, 3 + early, (*relay_from, c), sibling)]
            for cp in sent[-2:]:
                cp.start()
        for s in range(n_s):
            copy(s, late, (*relay_to, c), me).wait_recv()
            sent.append(copy(s, 3 + late, (*relay_to, c), sibling))
            sent[-1].start()
        for s in range(n_s):
            copy(s, 3, (*diag, c), me).wait_recv()
            sent.append(copy(s, 6, (*diag, c), sibling))
            sent[-1].start()
        for s in range(n_s):
            copy(s, 0, sibling, me).wait_recv()
            for k, chip in ((4, x_nbr), (5, y_nbr), (6, diag)):
                copy(s, k, (*chip, 1 - c), me).wait_recv()
        for cp in sent:
            cp.wait_send()
        for cp in mine:
            cp.wait()

    scratch = [pltpu.SemaphoreType.DMA((7 * n_s,)), pltpu.SemaphoreType.DMA((7 * n_s,)),
               pltpu.SemaphoreType.DMA((n_s,))]
    return _launch_on_sequencer(name, collective_id, body, [s[0] for s in streams], out_shapes, scratch)


def _exchange_blocks(name, streams, collective_id):
    n_s = len(streams)

    def body(*refs):
        srcs = refs[:n_s]
        outs = refs[n_s:2 * n_s]
        send_sems, recv_sems, local_sems = refs[2 * n_s:]
        x, y, c = _my_pos()
        me = _dev_index((x, y, c))
        _handshake([(1 - x if k & 4 else x, 1 - y if k & 2 else y, 1 - c if k & 1 else c) for k in range(1, N_DEV)])

        def window(s, dev):
            return srcs[s] if streams[s][1] is None else streams[s][1](srcs[s], dev)

        mine = [pltpu.make_async_copy(window(s, me), outs[s].at[me], local_sems.at[s]) for s in range(n_s)]
        for cp in mine:
            cp.start()
        copies = []
        for k in (2, 4, 6, 3, 5, 7, 1):
            peer = (1 - x if k & 4 else x, 1 - y if k & 2 else y, 1 - c if k & 1 else c)
            for s in range(n_s):
                copies.append(pltpu.make_async_remote_copy(
                    src_ref=window(s, _dev_index(peer)), dst_ref=outs[s].at[me],
                    send_sem=send_sems.at[7 * s + k - 1], recv_sem=recv_sems.at[7 * s + k - 1],
                    device_id=peer, device_id_type=MESH))
        for cp in copies:
            cp.start()
        for cp in copies:
            cp.wait()
        for cp in mine:
            cp.wait()

    out_shapes = [jax.ShapeDtypeStruct((N_DEV,) + tuple(s[2]), s[0].dtype) for s in streams]
    scratch = [pltpu.SemaphoreType.DMA((7 * n_s,)), pltpu.SemaphoreType.DMA((7 * n_s,)),
               pltpu.SemaphoreType.DMA((n_s,))]
    return _launch_on_sequencer(name, collective_id, body, [s[0] for s in streams], out_shapes, scratch)


N_CHIPS = 4


def _sibling_exchange(name, streams, collective_id):
    n_s = len(streams)

    def body(*refs):
        srcs = refs[:n_s]
        outs = refs[n_s:2 * n_s]
        send_sems, recv_sems = refs[2 * n_s:]
        x, y, c = _my_pos()
        sibling = (x, y, 1 - c)
        _handshake([sibling])
        copies = [pltpu.make_async_remote_copy(
            src_ref=streams[s][1](srcs[s], 2 * chip + (1 - c)), dst_ref=outs[s].at[chip],
            send_sem=send_sems.at[N_CHIPS * s + chip], recv_sem=recv_sems.at[N_CHIPS * s + chip],
            device_id=sibling, device_id_type=MESH) for s in range(n_s) for chip in range(N_CHIPS)]
        for cp in copies:
            cp.start()
        for cp in copies:
            cp.wait()

    out_shapes = [jax.ShapeDtypeStruct((N_CHIPS,) + tuple(s[2]), s[0].dtype) for s in streams]
    scratch = [pltpu.SemaphoreType.DMA((N_CHIPS * n_s,)), pltpu.SemaphoreType.DMA((N_CHIPS * n_s,))]
    return _launch_on_sequencer(name, collective_id, body, [s[0] for s in streams], out_shapes, scratch)


def _pair_sum(name, array, from_sibling, my_core, after):
    _, rows, cols = from_sibling.shape
    tr = _tile(rows, 2048)

    def body(core_ref, own_ref, sib_ref, after_ref, o_ref):
        del core_ref, after_ref
        o_ref[...] = (own_ref[...].astype(F32) + sib_ref[...].astype(F32)).astype(o_ref.dtype)

    staged_spec = pl.BlockSpec((None, tr, cols), lambda k, i, core: (k, i, 0))
    return pl.pallas_call(
        body, name=name, out_shape=jax.ShapeDtypeStruct(from_sibling.shape, array.dtype),
        grid_spec=pltpu.PrefetchScalarGridSpec(
            num_scalar_prefetch=1, grid=(N_CHIPS, rows // tr),
            in_specs=[pl.BlockSpec((tr, cols), lambda k, i, core: (i, 2 * k + core[0])), staged_spec, ANY],
            out_specs=staged_spec),
        compiler_params=_params("parallel", "parallel"),
    )(my_core, array, from_sibling, after)


def _chip_exchange(name, pair_sums, collective_id):
    n_s = len(pair_sums)

    def body(*refs):
        srcs = refs[:n_s]
        outs = refs[n_s:2 * n_s]
        send_sems, recv_sems, local_sems = refs[2 * n_s:]
        x, y, c = _my_pos()
        my_chip = 2 * x + y
        chips = [(1 - x, y), (x, 1 - y), (1 - x, 1 - y)]
        _handshake([(*chip, c) for chip in chips])
        mine = [pltpu.make_async_copy(srcs[s].at[my_chip], outs[s].at[my_chip], local_sems.at[s]) for s in range(n_s)]
        copies = [pltpu.make_async_remote_copy(
            src_ref=srcs[s].at[2 * chip[0] + chip[1]], dst_ref=outs[s].at[my_chip],
            send_sem=send_sems.at[3 * s + j], recv_sem=recv_sems.at[3 * s + j],
            device_id=(*chip, c), device_id_type=MESH) for s in range(n_s) for j, chip in enumerate(chips)]
        for cp in mine + copies:
            cp.start()
        for cp in copies:
            cp.wait()
        for cp in mine:
            cp.wait()

    out_shapes = [jax.ShapeDtypeStruct(p.shape, p.dtype) for p in pair_sums]
    scratch = [pltpu.SemaphoreType.DMA((3 * n_s,)), pltpu.SemaphoreType.DMA((3 * n_s,)),
               pltpu.SemaphoreType.DMA((n_s,))]
    return _launch_on_sequencer(name, collective_id, body, list(pair_sums), out_shapes, scratch)


NN = (((1,), (0,)), ((), ()))
NT = (((1,), (1,)), ((), ()))
TN = (((0,), (0,)), ((), ()))


def _mm(name, a, b, *, dims, grid, a_spec, b_spec, out_shape, out_spec, nk=1,
        add=None, add_spec=None, add_scale=1.0, epilogue=None, extras=(), extra_specs=()):
    n_extra = len(extras)
    has_add = add is not None

    def body(*refs):
        a_ref, b_ref = refs[:2]
        pos = 2
        add_ref = None
        if has_add:
            add_ref = refs[pos]
            pos += 1
        extra_refs = refs[pos:pos + n_extra]
        o_ref = refs[pos + n_extra]
        acc_ref = refs[pos + n_extra + 1] if nk > 1 else None

        def finish(val):
            if has_add:
                val = val + add_scale * add_ref[...]
            if epilogue is not None:
                val = epilogue(val, *extra_refs)
            o_ref[...] = val.astype(o_ref.dtype)

        part = lax.dot_general(a_ref[...].astype(BF16), b_ref[...].astype(BF16), dims,
                               preferred_element_type=F32)
        if nk == 1:
            finish(part)
        else:
            k = pl.program_id(2)

            @pl.when(k == 0)
            def _():
                acc_ref[...] = part

            @pl.when(jnp.logical_and(k > 0, k < nk - 1))
            def _():
                acc_ref[...] += part

            @pl.when(k == nk - 1)
            def _():
                finish(acc_ref[...] + part)

    in_specs = [a_spec, b_spec] + ([add_spec] if has_add else []) + list(extra_specs)
    operands = [a, b] + ([add] if has_add else []) + list(extras)
    scratch = [pltpu.VMEM(out_spec.block_shape, F32)] if nk > 1 else []
    sem = ("parallel", "parallel") + (("arbitrary",) if nk > 1 else ())
    return pl.pallas_call(
        body, name=name, grid=grid, out_shape=out_shape,
        in_specs=in_specs, out_specs=out_spec, scratch_shapes=scratch,
        compiler_params=_params(*sem),
    )(*operands)


def _input_grad(dh, w_in, dr1, after):
    _, seq, dm = dh.shape
    bm, bn = _tile(seq, 1024), _tile(dm, 512)

    def body(dh_ref, w_ref, dr_ref, after_ref, o_ref):
        del after_ref
        o_ref[...] = (lax.dot_general(dh_ref[0], w_ref[:, :dm], NT, preferred_element_type=F32)
                      + lax.dot_general(dh_ref[1], w_ref[:, dm:], NT, preferred_element_type=F32)
                      + ALPHA * dr_ref[...])

    tile_spec = pl.BlockSpec((bm, bn), lambda i, j: (i, j))
    return pl.pallas_call(
        body, name="a_in_proj_dx", grid=(seq // bm, dm // bn),
        out_shape=jax.ShapeDtypeStruct((seq, dm), F32),
        in_specs=[pl.BlockSpec((2, bm, dm), lambda i, j: (0, i, 0)), pl.BlockSpec((bn, 2 * dm), lambda i, j: (j, 0)),
                  tile_spec, ANY],
        out_specs=tile_spec,
        compiler_params=_params("parallel", "parallel"),
    )(dh, w_in, dr1, after)


def _cast_bf16(name, a):
    rows, cols = a.shape
    tr = _tile(rows, 512)

    def body(a_ref, o_ref):
        o_ref[...] = a_ref[...].astype(BF16)

    return pl.pallas_call(
        body, name=name, grid=(rows // tr,),
        out_shape=jax.ShapeDtypeStruct(a.shape, BF16),
        in_specs=[pl.BlockSpec((tr, cols), lambda i: (i, 0))],
        out_specs=pl.BlockSpec((tr, cols), lambda i: (i, 0)),
        compiler_params=_params("parallel"),
    )(a)


def _rope_tables(seq):
    inv_freq = ROPE_THETA ** (-jnp.arange(0, HEAD_DIM, 2, dtype=F32) / HEAD_DIM)
    ang = jnp.arange(seq, dtype=F32)[:, None] * inv_freq[None, :]
    cos, sin = jnp.cos(ang), jnp.sin(ang)
    cos, sin = (jnp.concatenate([t, t, t, t], axis=-1) for t in (cos, sin))
    first_half = (jnp.arange(LANES) % HEAD_DIM < HEAD_DIM // 2)[None, :]
    return cos, jnp.where(first_half, -sin, 0.0), jnp.where(first_half, 0.0, sin)


def _rot(t, sin_a, sin_b):
    return pltpu.roll(t, LANES - HEAD_DIM // 2, 1) * sin_a + pltpu.roll(t, HEAD_DIM // 2, 1) * sin_b


def _rope(t, cos, sin_a, sin_b):
    return t * cos + _rot(t, sin_a, sin_b)


def _rope_transposed(dy, cos, sin_a, sin_b):
    return dy * cos - _rot(dy, sin_a, sin_b)


def _silu_parts(z):
    sig = jax.nn.sigmoid(z)
    return z * sig, sig * (1.0 + z * (1.0 - sig))


def _layer_norm_stats(r):
    mu = jnp.mean(r, axis=-1, keepdims=True)
    d = r - mu
    var = jnp.mean(d * d, axis=-1, keepdims=True)
    rstd = lax.rsqrt(var + LN_EPS)
    return d * rstd, rstd


def _layer_norm_backward(dout, xhat, rstd, gain):
    dxh = dout * gain
    m1 = jnp.mean(dxh, axis=-1, keepdims=True)
    m2 = jnp.mean(dxh * xhat, axis=-1, keepdims=True)
    return rstd * (dxh - m1 - xhat * m2)


def _col_sum(v):
    return jnp.sum(v, axis=0, keepdims=True)


ROW_PART = 128


def _row_parts(tile):
    part = min(tile, ROW_PART)
    return [slice(r, r + part) for r in range(0, tile, part)]


def _pool_forward(xb, w_in, wg, scale):
    seq, dm = xb.shape
    n_g = len(POOL_WINDOWS)
    gd = dm // n_g
    tile = _tile(seq, 1024)
    halo_blocks = tile // POOL_HALO

    def body(x_ref, xp_ref, wu_ref, wz_ref, wg_ref, sc_ref, y_ref, p_ref, mx_ref, z_ref):
        i, g = pl.program_id(0), pl.program_id(1)
        u = jnp.dot(x_ref[...], wu_ref[...], preferred_element_type=F32)
        z = jnp.dot(x_ref[...], wz_ref[...], preferred_element_type=F32)
        prev = jnp.where(i > 0, jnp.dot(xp_ref[...], wu_ref[...], preferred_element_type=F32), 0.0)
        s = jnp.concatenate([prev, u], axis=0)
        sums, sh = [], 1
        while sh < POOL_WINDOWS[-1]:
            s = s + pltpu.roll(s, sh, 0)
            sums.append(s)
            sh *= 2
        win = sums[-1]
        for k in range(n_g - 2, -1, -1):
            win = jnp.where(g == k, sums[k], win)
        row = i * tile + lax.broadcasted_iota(jnp.int32, (tile, 1), 0)
        window = jnp.left_shift(2, g).astype(F32)
        p = win[POOL_HALO:, :] * (1.0 / jnp.minimum((row + 1).astype(F32), window)) - u
        pb = p.astype(BF16)
        mx = jnp.dot(pb, wg_ref[...], preferred_element_type=F32)
        y_ref[...] = (mx * sc_ref[...] * (z * jax.nn.sigmoid(z))).astype(BF16)
        p_ref[...] = pb
        mx_ref[...] = mx
        z_ref[...] = z

    out_spec = pl.BlockSpec((tile, gd), lambda i, g: (i, g))
    return pl.pallas_call(
        body, name="pool_fwd", grid=(seq // tile, n_g),
        out_shape=(jax.ShapeDtypeStruct((seq, dm), BF16), jax.ShapeDtypeStruct((seq, dm), BF16),
                   jax.ShapeDtypeStruct((seq, dm), F32), jax.ShapeDtypeStruct((seq, dm), F32)),
        in_specs=[pl.BlockSpec((tile, dm), lambda i, g: (i, 0)),
                  pl.BlockSpec((POOL_HALO, dm), lambda i, g: (jnp.maximum(i * halo_blocks - 1, 0), 0)),
                  pl.BlockSpec((dm, gd), lambda i, g: (0, g)),
                  pl.BlockSpec((dm, gd), lambda i, g: (0, n_g + g)),
                  pl.BlockSpec((None, gd, gd), lambda i, g: (g, 0, 0)),
                  pl.BlockSpec((1, gd), lambda i, g: (0, g))],
        out_specs=(out_spec, out_spec, out_spec, out_spec),
        compiler_params=_params("parallel", "parallel"),
    )(xb, xb, w_in, w_in, wg, scale)


def _pool_mid_backward(dr, w_out, mx, z, p, wg, scale, after):
    seq, dm = mx.shape
    gd = dm // len(POOL_WINDOWS)
    tile = _tile(seq, 256)
    n_i = seq // tile

    def body(dr_ref, wout_ref, mx_ref, z_ref, p_ref, wg_ref, sc_ref, after_ref, dh_ref, dwg_ref, st_ref, dwg_acc,
             carry):
        del after_ref
        i = pl.program_id(0)

        def dy_of(g):
            return lax.dot_general(dr_ref[...], wout_ref[g * gd:(g + 1) * gd, :], NT, preferred_element_type=F32)

        dy_ahead = dy_of(0)
        ti = n_i - 1 - i

        @pl.when(i == 0)
        def _():
            dwg_acc[...] = jnp.zeros_like(dwg_acc)
            carry[...] = jnp.zeros_like(carry)
            st_ref[...] = jnp.zeros_like(st_ref)

        row = ti * tile + lax.broadcasted_iota(jnp.int32, (tile, 1), 0)
        count = (row + 1).astype(F32)
        for g, w in enumerate(POOL_WINDOWS):
            cs = slice(g * gd, (g + 1) * gd)
            z = z_ref[:, cs]
            sz, dsz = _silu_parts(z)
            dyg = dy_ahead
            if g + 1 < len(POOL_WINDOWS):
                dy_ahead = dy_of(g + 1)
            mxg = mx_ref[:, cs]
            sc = sc_ref[:, cs]
            t1 = dyg * sz
            st_ref[0:1, cs] += _col_sum(t1 * mxg)
            dh_ref[1, :, cs] = (dyg * (mxg * sc) * dsz).astype(BF16)
            dmx = (t1 * sc).astype(BF16)
            dwg_acc[g] += lax.dot_general(p_ref[:, cs], dmx, TN, preferred_element_type=F32)
            dp = lax.dot_general(dmx, wg_ref[g], NT, preferred_element_type=F32)
            e = dp * (1.0 / jnp.minimum(count, float(w)))
            s = jnp.concatenate([e, carry[:, cs]], axis=0)
            n = tile + POOL_HALO
            sh = 1
            while sh < w:
                s = s + pltpu.roll(s, n - sh, 0)
                sh *= 2
            dh_ref[0, :, cs] = (s[:tile, :] - dp).astype(BF16)
            carry[:, cs] = e[:POOL_HALO, :]

        @pl.when(i == n_i - 1)
        def _():
            dwg_ref[...] = dwg_acc[...].astype(BF16)

    row_spec = pl.BlockSpec((tile, dm), lambda i: (n_i - 1 - i, 0))
    return pl.pallas_call(
        body, name="pool_mid_bwd", grid=(n_i,),
        out_shape=(jax.ShapeDtypeStruct((2, seq, dm), BF16), jax.ShapeDtypeStruct(wg.shape, BF16),
                   jax.ShapeDtypeStruct((STAT_ROWS, dm), F32)),
        in_specs=[row_spec, pl.BlockSpec(w_out.shape, lambda i: (0, 0), pipeline_mode=pl.Buffered(1)),
                  row_spec, row_spec, row_spec,
                  pl.BlockSpec(wg.shape, lambda i: (0, 0, 0)),
                  pl.BlockSpec((1, dm), lambda i: (0, 0)), ANY],
        out_specs=(pl.BlockSpec((2, tile, dm), lambda i: (0, n_i - 1 - i, 0)),
                   pl.BlockSpec(wg.shape, lambda i: (0, 0, 0)),
                   pl.BlockSpec((STAT_ROWS, dm), lambda i: (0, 0))),
        scratch_shapes=[pltpu.VMEM(wg.shape, F32), pltpu.VMEM((POOL_HALO, dm), F32)],
        compiler_params=_params("arbitrary"),
    )(dr, w_out, mx, z, p, wg, scale, after)


def _out_proj_norm(y, w, x, gain, bias):
    seq, dm = x.shape
    tile = _tile(seq, 512)

    def body(y_ref, w_ref, x_ref, g_ref, b_ref, xhat_ref, rstd_ref, xb_ref):
        for rows in _row_parts(tile):
            o = jnp.dot(y_ref[rows, :], w_ref[...], preferred_element_type=F32)
            xhat, rstd = _layer_norm_stats(ALPHA * x_ref[rows, :] + o)
            xhat_ref[rows, :] = xhat
            rstd_ref[rows, :] = rstd
            xb_ref[rows, :] = (xhat * g_ref[...] + b_ref[...]).astype(BF16)

    row_spec = pl.BlockSpec((tile, dm), lambda i: (i, 0))
    vec_spec = pl.BlockSpec((1, dm), lambda i: (0, 0))
    return pl.pallas_call(
        body, name="out_proj_norm_a", grid=(seq // tile,),
        out_shape=(jax.ShapeDtypeStruct((seq, dm), F32), jax.ShapeDtypeStruct((seq, 1), F32),
                   jax.ShapeDtypeStruct((seq, dm), BF16)),
        in_specs=[row_spec, pl.BlockSpec(w.shape, lambda i: (0, 0), pipeline_mode=pl.Buffered(1)), row_spec, vec_spec,
                  vec_spec],
        out_specs=(row_spec, pl.BlockSpec((tile, 1), lambda i: (i, 0)), row_spec),
        compiler_params=_params("parallel"),
    )(y, w, x, gain, bias)


def _kv_proj(xb, wkv, tables):
    seq, dm = xb.shape
    kvw = wkv.shape[1] // 2
    n_kv = kvw // HEAD_DIM
    tile = _tile(seq, 1024)

    def body(x_ref, w_ref, cos_ref, sa_ref, sb_ref, kd_ref, vd_ref, kt_ref, vt_ref):
        kv = jnp.dot(x_ref[...], w_ref[...], preferred_element_type=F32)
        low = lax.broadcasted_iota(jnp.int32, (1, LANES), 1) < HEAD_DIM
        cos, sa, sb = cos_ref[...], sa_ref[...], sb_ref[...]

        def put(pair, h, nat_ref, t_ref):
            swapped = pltpu.roll(pair, HEAD_DIM, 1)
            for head, dup in ((h, jnp.where(low, pair, swapped)), (h + 1, jnp.where(low, swapped, pair))):
                nat_ref[head] = dup.astype(BF16)
                t_ref[head] = dup.T.astype(BF16)

        for j in range(kvw // LANES):
            put(_rope(kv[:, j * LANES:(j + 1) * LANES], cos, sa, sb), 2 * j, kd_ref, kt_ref)
            put(kv[:, kvw + j * LANES:kvw + (j + 1) * LANES], 2 * j, vd_ref, vt_ref)

    tab_spec = pl.BlockSpec((tile, LANES), lambda i: (i, 0))
    dup_spec = pl.BlockSpec((n_kv, tile, LANES), lambda i: (0, i, 0))
    dup_shape = jax.ShapeDtypeStruct((n_kv, seq, LANES), BF16)
    t_spec = pl.BlockSpec((n_kv, LANES, tile), lambda i: (0, 0, i))
    t_shape = jax.ShapeDtypeStruct((n_kv, LANES, seq), BF16)
    return pl.pallas_call(
        body, name="kv_proj", grid=(seq // tile,),
        out_shape=(dup_shape, dup_shape, t_shape, t_shape),
        in_specs=[pl.BlockSpec((tile, dm), lambda i: (i, 0)), pl.BlockSpec(wkv.shape, lambda i: (0, 0)),
                  tab_spec, tab_spec, tab_spec],
        out_specs=(dup_spec, dup_spec, t_spec, t_spec),
        compiler_params=_params("parallel"),
    )(xb, wkv, *tables)


ATTN_STEP_BLOCKS = 16


def _head_queries(q_ref, rows, low):
    parts = []
    for j in range(GQA_GROUP // 2):
        q2 = q_ref[rows, j * LANES:(j + 1) * LANES]
        parts += [jnp.where(low, q2, 0), jnp.where(low, 0, q2)]
    return parts


def _key_window(prev_ref, cur_ref, b, axis):
    def block(i):
        sl = slice(i * ATTN_BLOCK, (i + 1) * ATTN_BLOCK)
        return cur_ref[sl, :] if axis == 0 else cur_ref[:, sl]
    return jnp.concatenate([prev_ref[...] if b == 0 else block(b - 1), block(b)], axis=axis)


def _mask_bias(n):
    key = lax.broadcasted_iota(jnp.int32, (2 * ATTN_BLOCK, ATTN_BLOCK), 0)
    qry = lax.broadcasted_iota(jnp.int32, (2 * ATTN_BLOCK, ATTN_BLOCK), 1)
    valid = (key > qry) & (key <= qry + ATTN_BLOCK) & ((key >= ATTN_BLOCK) | (n > 0))
    return jnp.where(valid, 0.0, NEG_INF)


def _head_probs_transposed(kcat, qm, bias, sink):
    st = lax.dot_general(kcat, qm, NT, preferred_element_type=F32) + bias
    m = jnp.maximum(jnp.max(st, axis=0, keepdims=True), sink)
    e = jnp.exp(st - m)
    e_sink = jnp.exp(sink - m)
    inv = 1.0 / (jnp.sum(e, axis=0, keepdims=True) + e_sink)
    return e * inv, e_sink * inv


def _probs_transposed(n, kh, kcat, q_all, sink_ref):
    st = lax.dot_general(kcat, q_all, NT, preferred_element_type=F32)
    st = st + jnp.tile(_mask_bias(n), (1, GQA_GROUP))
    sink = jnp.concatenate([jnp.full((1, ATTN_BLOCK), sink_ref[0, kh * GQA_GROUP + h], F32)
                            for h in range(GQA_GROUP)], axis=1)
    m = jnp.maximum(jnp.max(st, axis=0, keepdims=True), sink)
    e = jnp.exp(st - m)
    e_sink = jnp.exp(sink - m)
    inv = 1.0 / (jnp.sum(e, axis=0, keepdims=True) + e_sink)
    return e * inv, e_sink * inv


def _attn_specs(n_width, qb):
    rows = qb * ATTN_BLOCK
    before = lambda n: jnp.maximum(n * qb - 1, 0)
    q_spec = pl.BlockSpec((rows, n_width), lambda kh, n: (n, kh))
    cur = pl.BlockSpec((None, rows, LANES), lambda kh, n: (kh, n, 0))
    prev = pl.BlockSpec((None, ATTN_BLOCK, LANES), lambda kh, n: (kh, before(n), 0))
    cur_t = pl.BlockSpec((None, LANES, rows), lambda kh, n: (kh, 0, n))
    prev_t = pl.BlockSpec((None, LANES, ATTN_BLOCK), lambda kh, n: (kh, 0, before(n)))
    return q_spec, cur, prev, cur_t, prev_t


def _pair_product_transposed(mat_t, rhs, j, low_rows):
    head_a = slice(2 * j * ATTN_BLOCK, (2 * j + 1) * ATTN_BLOCK)
    head_b = slice((2 * j + 1) * ATTN_BLOCK, (2 * j + 2) * ATTN_BLOCK)
    out_t = (jnp.dot(jnp.where(low_rows, mat_t, 0), rhs[:, head_a], preferred_element_type=F32)
             + jnp.dot(jnp.where(low_rows, 0, mat_t), rhs[:, head_b], preferred_element_type=F32))
    return out_t.T


def _attn_forward(qs, kd, vt, zb, sinks):
    seq, dm = qs.shape
    n_kv = kd.shape[0]
    gw = GQA_GROUP * HEAD_DIM

    qb = ATTN_STEP_BLOCKS if (seq // ATTN_BLOCK) % ATTN_STEP_BLOCKS == 0 else 1

    def body(q_ref, kp_ref, kc_ref, vtp_ref, vtc_ref, z_ref, sink_ref, att_ref, yb_ref):
        kh, n = pl.program_id(0), pl.program_id(1)
        low = lax.broadcasted_iota(jnp.int32, (1, LANES), 1) < HEAD_DIM
        low_rows = lax.broadcasted_iota(jnp.int32, (LANES, 1), 0) < HEAD_DIM
        for b in range(qb):
            rows = slice(b * ATTN_BLOCK, (b + 1) * ATTN_BLOCK)
            kcat = _key_window(kp_ref, kc_ref, b, 0)
            vt = _key_window(vtp_ref, vtc_ref, b, 1)
            bias = _mask_bias(n * qb + b)
            pt = jnp.concatenate(
                [_head_probs_transposed(kcat, qm, bias, sink_ref[0, kh * GQA_GROUP + h])[0].astype(BF16)
                 for h, qm in enumerate(_head_queries(q_ref, rows, low))], axis=1)
            for j in range(GQA_GROUP // 2):
                cs = slice(j * LANES, (j + 1) * LANES)
                o2 = _pair_product_transposed(vt, pt, j, low_rows)
                att_ref[rows, cs] = o2
                z = z_ref[rows, cs]
                yb_ref[rows, cs] = (o2 * (z * jax.nn.sigmoid(z))).astype(BF16)

    q_spec, cur, prev, cur_t, prev_t = _attn_specs(gw, qb)
    return pl.pallas_call(
        body, name="attn_fwd", grid=(n_kv, seq // (qb * ATTN_BLOCK)),
        out_shape=(jax.ShapeDtypeStruct((seq, dm), F32), jax.ShapeDtypeStruct((seq, dm), BF16)),
        in_specs=[q_spec, prev, cur, prev_t, cur_t, q_spec, pl.BlockSpec(memory_space=pltpu.SMEM)],
        out_specs=(q_spec, q_spec),
        compiler_params=_params("parallel", "parallel"),
    )(qs, kd, kd, vt, vt, zb, sinks)


def _attn_backward(qs, kd, vd, kt, zb, att, dyb, sinks, tables, after):
    seq, dm = qs.shape
    n_kv = kd.shape[0]
    gw = GQA_GROUP * HEAD_DIM
    n_blocks = seq // ATTN_BLOCK
    qb = ATTN_STEP_BLOCKS if n_blocks % ATTN_STEP_BLOCKS == 0 else 1

    def body(q_ref, kp_ref, kc_ref, vp_ref, vc_ref, ktp_ref, ktc_ref, z_ref, att_ref, dyb_ref, sink_ref,
             cos_ref, sa_ref, sb_ref, after_ref, dh_ref, dk_ref, dv_ref, ds_ref):
        del after_ref
        kh, n = pl.program_id(0), pl.program_id(1)

        @pl.when(n == 0)
        def _():
            dk_ref[...] = jnp.zeros_like(dk_ref)
            dv_ref[...] = jnp.zeros_like(dv_ref)

        @pl.when(jnp.logical_and(n == 0, kh == 0))
        def _():
            ds_ref[...] = jnp.zeros_like(ds_ref)

        low = lax.broadcasted_iota(jnp.int32, (1, LANES), 1) < HEAD_DIM
        low_rows = lax.broadcasted_iota(jnp.int32, (LANES, 1), 0) < HEAD_DIM
        head_lane = lax.broadcasted_iota(jnp.int32, (1, LANES), 1)
        dsink = jnp.zeros((1, LANES), F32)
        for b in range(qb):
            rows = slice(b * ATTN_BLOCK, (b + 1) * ATTN_BLOCK)
            kcat = _key_window(kp_ref, kc_ref, b, 0)
            vcat = _key_window(vp_ref, vc_ref, b, 0)
            kt = _key_window(ktp_ref, ktc_ref, b, 1)
            cos, sa, sb = cos_ref[rows, :], sa_ref[rows, :], sb_ref[rows, :]
            q_all = jnp.concatenate(_head_queries(q_ref, rows, low), axis=0)
            d_parts = []
            for j in range(GQA_GROUP // 2):
                cs = slice(j * LANES, (j + 1) * LANES)
                sz, dsz = _silu_parts(z_ref[rows, cs])
                dy2 = dyb_ref[rows, cs]
                dh_ref[1, rows, cs] = (dy2 * att_ref[rows, cs] * dsz).astype(BF16)
                datt = (dy2 * sz).astype(BF16)
                d_parts += [jnp.where(low, datt, 0), jnp.where(low, 0, datt)]
            d_all = jnp.concatenate(d_parts, axis=0)
            probs_t, sink_p = _probs_transposed(n * qb + b, kh, kcat, q_all, sink_ref)
            dprobs_t = lax.dot_general(vcat, d_all, NT, preferred_element_type=F32)
            row_dot = jnp.sum(probs_t * dprobs_t, axis=0, keepdims=True)
            ds_t = (probs_t * (dprobs_t - row_dot)).astype(BF16)
            dk = jnp.dot(ds_t, q_all, preferred_element_type=F32)
            dv = jnp.dot(probs_t.astype(BF16), d_all, preferred_element_type=F32)
            for j in range(GQA_GROUP // 2):
                dq2 = _pair_product_transposed(kt, ds_t, j, low_rows)
                dh_ref[0, rows, j * LANES:(j + 1) * LANES] = (
                    _rope_transposed(dq2, cos, sa, sb) * 0.125).astype(BF16)
            sink_dot = sink_p * row_dot
            for h in range(GQA_GROUP):
                part = jnp.sum(sink_dot[:, h * ATTN_BLOCK:(h + 1) * ATTN_BLOCK], axis=1, keepdims=True)
                dsink = dsink - jnp.where(head_lane == kh * GQA_GROUP + h, part, 0.0)

            def add_window(dk=dk, dv=dv, b=b):
                start = pl.multiple_of((n * qb + b - 1) * ATTN_BLOCK, ATTN_BLOCK)
                dk_ref[pl.ds(start, 2 * ATTN_BLOCK), :] += dk
                dv_ref[pl.ds(start, 2 * ATTN_BLOCK), :] += dv

            if b > 0:
                add_window()
            else:
                pl.when(n > 0)(add_window)

                @pl.when(n == 0)
                def _(dk=dk, dv=dv):
                    dk_ref[pl.ds(0, ATTN_BLOCK), :] += dk[ATTN_BLOCK:, :]
                    dv_ref[pl.ds(0, ATTN_BLOCK), :] += dv[ATTN_BLOCK:, :]
        ds_ref[0:1, :] += dsink

    q_spec, cur, prev, cur_t, prev_t = _attn_specs(gw, qb)
    tab_spec = pl.BlockSpec((qb * ATTN_BLOCK, LANES), lambda kh, n: (n, 0))
    acc_spec = pl.BlockSpec((None, seq, LANES), lambda kh, n: (kh, 0, 0))
    acc_shape = jax.ShapeDtypeStruct((n_kv, seq, LANES), F32)
    return pl.pallas_call(
        body, name="attn_bwd", grid=(n_kv, n_blocks // qb),
        out_shape=(jax.ShapeDtypeStruct((2, seq, dm), BF16), acc_shape, acc_shape,
                   jax.ShapeDtypeStruct((STAT_ROWS, LANES), F32)),
        in_specs=[q_spec, prev, cur, prev, cur, prev_t, cur_t, q_spec, q_spec, q_spec,
                  pl.BlockSpec(memory_space=pltpu.SMEM), tab_spec, tab_spec, tab_spec, ANY],
        out_specs=(pl.BlockSpec((2, qb * ATTN_BLOCK, gw), lambda kh, n: (0, n, kh)), acc_spec, acc_spec,
                   pl.BlockSpec((STAT_ROWS, LANES), lambda kh, n: (0, 0))),
        compiler_params=_params("arbitrary", "arbitrary"),
    )(qs, kd, kd, vd, vd, kt, kt, zb, att, dyb, sinks, *tables, after)


def _kv_grad_fold(dk, dv, tables):
    n_kv, seq, _ = dk.shape
    kvw = n_kv * HEAD_DIM
    tile = _tile(seq, 512)

    def body(dk_ref, dv_ref, cos_ref, sa_ref, sb_ref, o_ref):
        low = lax.broadcasted_iota(jnp.int32, (1, LANES), 1) < HEAD_DIM
        cos, sa, sb = cos_ref[...], sa_ref[...], sb_ref[...]

        def folded(ref, h):
            t = ref[h]
            return t + pltpu.roll(t, HEAD_DIM, 1)

        for j in range(n_kv // 2):
            ka = _rope_transposed(folded(dk_ref, 2 * j), cos, sa, sb)
            kb = _rope_transposed(folded(dk_ref, 2 * j + 1), cos, sa, sb)
            o_ref[:, j * LANES:(j + 1) * LANES] = jnp.where(low, ka, kb).astype(BF16)
            o_ref[:, kvw + j * LANES:kvw + (j + 1) * LANES] = jnp.where(
                low, folded(dv_ref, 2 * j), folded(dv_ref, 2 * j + 1)).astype(BF16)

    tab_spec = pl.BlockSpec((tile, LANES), lambda i: (i, 0))
    in_spec = pl.BlockSpec((n_kv, tile, LANES), lambda i: (0, i, 0))
    return pl.pallas_call(
        body, name="kv_grad_fold", grid=(seq // tile,),
        out_shape=jax.ShapeDtypeStruct((seq, 2 * kvw), BF16),
        in_specs=[in_spec, in_spec, tab_spec, tab_spec, tab_spec],
        out_specs=pl.BlockSpec((tile, 2 * kvw), lambda i: (i, 0)),
        compiler_params=_params("parallel"),
    )(dk, dv, *tables)


def _out_proj_norm_loss(yb, w, xhat1, gain0, bias0, gain1, bias1, target):
    seq, dm = xhat1.shape
    tile = _tile(seq, 512)

    def body(y_ref, w_ref, xh1_ref, g0_ref, b0_ref, g1_ref, b1_ref, t_ref, dr_ref, drb_ref, st_ref):
        i = pl.program_id(0)

        @pl.when(i == 0)
        def _():
            st_ref[...] = jnp.zeros_like(st_ref)

        parts = _row_parts(tile)
        product = lambda rows: jnp.dot(y_ref[rows, :], w_ref[...], preferred_element_type=F32)
        ahead = product(parts[0])
        for k, rows in enumerate(parts):
            ob = ahead
            if k + 1 < len(parts):
                ahead = product(parts[k + 1])
            x1 = xh1_ref[rows, :] * g0_ref[...] + b0_ref[...]
            xhat, rstd = _layer_norm_stats(ALPHA * x1 + ob)
            err = xhat * g1_ref[...] + b1_ref[...] - t_ref[rows, :]
            dout = err * (1.0 / dm)
            dr = _layer_norm_backward(dout, xhat, rstd, g1_ref[...])
            dr_ref[rows, :] = dr
            drb_ref[rows, :] = dr.astype(BF16)
            st_ref[0:1, :] += _col_sum(dout * xhat)
            st_ref[1:2, :] += _col_sum(dout)
            st_ref[2:3, :] += _col_sum(err * err)

    row_spec = pl.BlockSpec((tile, dm), lambda i: (i, 0))
    vec_spec = pl.BlockSpec((1, dm), lambda i: (0, 0))
    return pl.pallas_call(
        body, name="out_proj_norm_loss_b", grid=(seq // tile,),
        out_shape=(jax.ShapeDtypeStruct((seq, dm), F32), jax.ShapeDtypeStruct((seq, dm), BF16),
                   jax.ShapeDtypeStruct((STAT_ROWS, dm), F32)),
        in_specs=[row_spec, pl.BlockSpec(w.shape, lambda i: (0, 0), pipeline_mode=pl.Buffered(1)), row_spec, vec_spec,
                  vec_spec, vec_spec,
                  vec_spec, row_spec],
        out_specs=(row_spec, row_spec, pl.BlockSpec((STAT_ROWS, dm), lambda i: (0, 0))),
        compiler_params=_params("arbitrary"),
    )(yb, w, xhat1, gain0, bias0, gain1, bias1, target)


def _stream_grad_norm_backward(dhq, wqg, dkv, wkv, dr2, xhat1, rstd1, gain0, after):
    seq, dm = dr2.shape
    tile = _tile(seq, 256)

    def body(dh_ref, wqg_ref, dkv_ref, wkv_ref, dr2_ref, xh_ref, rstd_ref, g_ref, after_ref, dr_ref, drb_ref, st_ref):
        del after_ref

        @pl.when(pl.program_id(0) == 0)
        def _():
            st_ref[...] = jnp.zeros_like(st_ref)

        dx1 = (lax.dot_general(dh_ref[0], wqg_ref[:, :dm], NT, preferred_element_type=F32)
               + lax.dot_general(dh_ref[1], wqg_ref[:, dm:], NT, preferred_element_type=F32)
               + lax.dot_general(dkv_ref[...], wkv_ref[...], NT, preferred_element_type=F32)
               + ALPHA * dr2_ref[...])
        xhat = xh_ref[...]
        dr = _layer_norm_backward(dx1, xhat, rstd_ref[...], g_ref[...])
        dr_ref[...] = dr
        drb_ref[...] = dr.astype(BF16)
        st_ref[0:1, :] += _col_sum(dx1 * xhat)
        st_ref[1:2, :] += _col_sum(dx1)

    row_spec = pl.BlockSpec((tile, dm), lambda i: (i, 0))
    resident = pl.Buffered(1)
    return pl.pallas_call(
        body, name="stream_grad_norm_bwd", grid=(seq // tile,),
        out_shape=(jax.ShapeDtypeStruct((seq, dm), F32), jax.ShapeDtypeStruct((seq, dm), BF16),
                   jax.ShapeDtypeStruct((STAT_ROWS, dm), F32)),
        in_specs=[pl.BlockSpec((2, tile, dm), lambda i: (0, i, 0)),
                  pl.BlockSpec(wqg.shape, lambda i: (0, 0), pipeline_mode=resident),
                  pl.BlockSpec((tile, dkv.shape[1]), lambda i: (i, 0)),
                  pl.BlockSpec(wkv.shape, lambda i: (0, 0), pipeline_mode=resident),
                  row_spec, row_spec, pl.BlockSpec((tile, 1), lambda i: (i, 0)),
                  pl.BlockSpec((1, dm), lambda i: (0, 0)), ANY],
        out_specs=(row_spec, row_spec, pl.BlockSpec((STAT_ROWS, dm), lambda i: (0, 0))),
        compiler_params=_params("arbitrary"),
    )(dhq, wqg, dkv, wkv, dr2, xhat1, rstd1, gain0, after)


def _adamw_math(w, g, m, v):
    m = ADAM_B1 * m + (1.0 - ADAM_B1) * g
    v = ADAM_B2 * v + (1.0 - ADAM_B2) * (g * g)
    m_hat = m / (1.0 - ADAM_B1 ** ADAM_STEP)
    v_hat = v / (1.0 - ADAM_B2 ** ADAM_STEP)
    delta = -ADAM_LR * (m_hat / (jnp.sqrt(v_hat) + ADAM_EPS) + ADAM_WD * w)
    return delta, m, v


def _sum_devices(ref):
    total = ref[0].astype(F32)
    for d in range(1, ref.shape[0]):
        total = total + ref[d].astype(F32)
    return total


def _adamw_shard(name, parts, w, m, v, after, row_range=None, into=None):
    rows, cols = w.shape
    first_row, end_row = row_range or (0, rows)
    n_parts = len(parts)
    part_rows = (end_row - first_row) // n_parts
    tr = _tile(part_rows, max(8, (1 << 18) // cols)) if part_rows >= 8 else part_rows
    per_part = part_rows // tr
    first = first_row // tr
    kept = list(into or ())

    def body(*refs):
        p_refs = refs[:n_parts]
        w_ref, m_ref, v_ref = refs[n_parts:n_parts + 3]
        g_out, d_out, m_out, v_out = refs[n_parts + 4 + len(kept):]
        g = _sum_devices(p_refs[0])
        for k in range(1, n_parts):
            g = jnp.where(pl.program_id(0) >= k * per_part, _sum_devices(p_refs[k]), g)
        delta, m_new, v_new = _adamw_math(w_ref[...], g, m_ref[...], v_ref[...])
        g_out[...] = g
        d_out[...] = delta
        m_out[...] = m_new
        v_out[...] = v_new

    def part_spec(k):
        return pl.BlockSpec((parts[k].shape[0], tr, cols),
                            lambda i: (0, jnp.clip(i - k * per_part, 0, per_part - 1), 0))

    spec = pl.BlockSpec((tr, cols), lambda i: (first + i, 0))
    shape = jax.ShapeDtypeStruct((rows, cols), F32)
    return pl.pallas_call(
        body, name=name, grid=((end_row - first_row) // tr,),
        out_shape=(shape, shape, shape, shape),
        in_specs=[part_spec(k) for k in range(n_parts)] + [spec, spec, spec, ANY] + [ANY] * len(kept),
        out_specs=(spec, spec, spec, spec),
        input_output_aliases={n_parts + 4 + k: k for k in range(len(kept))},
        compiler_params=_params("arbitrary"),
    )(*parts, w, m, v, after, *kept)


def _adamw_replicated(stats_b, stats_a, sink_parts, ln_g, ln_b, sinks, m_ln_g, m_ln_b, m_sinks, v_ln_g, v_ln_b,
                      v_sinks, after):
    n_q = sinks.shape[1]
    dm = ln_g.shape[1]

    def body(sb_ref, sa_ref, sk_ref, g_ref, b_ref, s_ref, mg_ref, mb_ref, ms_ref, vg_ref, vb_ref, vs_ref, after_ref,
             *outs):
        del after_ref
        layer_sums = (_sum_devices(sa_ref), _sum_devices(sb_ref))
        outs[12][...] = jnp.sum(layer_sums[1][2:3, :], axis=1, keepdims=True) * (0.5 / dm)
        for which, (w_ref, m_ref, v_ref) in enumerate(((g_ref, mg_ref, vg_ref), (b_ref, mb_ref, vb_ref))):
            for layer in range(DEPTH):
                row = slice(layer, layer + 1)
                g = layer_sums[layer][which:which + 1, :]
                res = (g,) + _adamw_math(w_ref[row, :], g, m_ref[row, :], v_ref[row, :])
                for o_ref, val in zip(outs[4 * which:4 * which + 4], res):
                    o_ref[row, :] = val
        g = _sum_devices(sk_ref)[0:1, 0:n_q]
        res = (g,) + _adamw_math(s_ref[...], g, ms_ref[...], vs_ref[...])
        for o_ref, val in zip(outs[8:12], res):
            o_ref[...] = val

    vmem = pl.BlockSpec(memory_space=pltpu.VMEM)
    shapes = [jax.ShapeDtypeStruct(a.shape, F32) for a in (ln_g, ln_b, sinks) for _ in range(4)]
    shapes.append(jax.ShapeDtypeStruct((1, 1), F32))
    return pl.pallas_call(
        body, name="adamw_replicated", out_shape=tuple(shapes),
        in_specs=[vmem] * 12 + [ANY], out_specs=tuple([vmem] * 13),
    )(stats_b, stats_a, sink_parts, ln_g, ln_b, sinks, m_ln_g, m_ln_b, m_sinks, v_ln_g, v_ln_b, v_sinks, after)


def kernel(x, ln_g, ln_b, a_w_in, a_w_group, a_scale, a_w_out, b_w_k, b_w_v, b_w_qg, b_sinks, b_w_out, loss_target, m_ln_g, m_ln_b, m_a_w_in, m_a_w_group, m_a_scale, m_a_w_out, m_b_w_k, m_b_w_v, m_b_w_qg, m_b_sinks, m_b_w_out, v_ln_g, v_ln_b, v_a_w_in, v_a_w_group, v_a_scale, v_a_w_out, v_b_w_k, v_b_w_v, v_b_w_qg, v_b_sinks, v_b_w_out):
    _, seq, dm = x.shape
    n_groups = len(POOL_WINDOWS)
    gd = dm // n_groups
    kvw = b_w_k.shape[1]
    cb = 2 * dm // N_DEV
    rb = dm // N_DEV
    gb = gd // N_DEV

    x2 = x.reshape(seq, dm)
    target = loss_target.reshape(seq, dm)
    w_in_s = a_w_in.reshape(dm, cb)
    w_g_s = a_w_group.reshape(n_groups, gb, gd)
    w_out_s = a_w_out.reshape(rb, dm)
    w_qg_s = b_w_qg.reshape(dm, cb)
    w_outb_s = b_w_out.reshape(rb, dm)

    def cols(ref, dev):
        return ref.at[:, pl.ds(pl.multiple_of(dev * cb, LANES), cb)]

    def rows(ref, dev):
        return ref.at[pl.ds(pl.multiple_of(dev * rb, 8), rb), :]

    def group_rows(ref, dev):
        return ref.at[:, pl.ds(pl.multiple_of(dev * gb, 8), gb), :]

    def k_rows(ref, dev):
        return ref.at[pl.ds(pl.multiple_of(dev * rb, 8), rb), pl.ds(0, kvw)]

    def v_rows(ref, dev):
        return ref.at[pl.ds(pl.multiple_of(dev * rb, 8), rb), pl.ds(kvw, kvw)]

    def scale_cols(ref, dev):
        return ref.at[:, pl.ds(pl.multiple_of(dev * rb, LANES), rb)]

    bf = lambda a: a.astype(BF16)
    wide, square = jax.ShapeDtypeStruct((dm, 2 * dm), BF16), jax.ShapeDtypeStruct((dm, dm), BF16)
    w_g, scale, w_in = _gather_weights(
        "gather_a_in", 0, [(bf(w_g_s), 0, group_rows), (a_scale, 1, scale_cols), (bf(w_in_s), 2, cols)],
        [jax.ShapeDtypeStruct((n_groups, gd, gd), BF16), jax.ShapeDtypeStruct((1, dm), F32), wide])
    (w_out,) = _gather_weights("gather_a_out", 1, [(bf(w_out_s), 0, rows)], [square])
    w_kv, w_qg = _gather_weights(
        "gather_b_in", 2, [(bf(b_w_k), 0, k_rows), (bf(b_w_v), 0, v_rows), (bf(w_qg_s), 1, cols)],
        [jax.ShapeDtypeStruct((dm, 2 * kvw), BF16), wide])
    (w_outb,) = _gather_weights("gather_b_out", 3, [(bf(w_outb_s), 0, rows)], [square])

    tables = _rope_tables(seq)
    bm = _tile(seq, 2048)
    bn = _tile(dm, 1024)
    g0, g1, b0, b1 = ln_g[0:1], ln_g[1:2], ln_b[0:1], ln_b[1:2]

    xb = _cast_bf16("cast_x", x2)
    y, pooled, mixed, z_a = _pool_forward(xb, w_in, w_g, scale)
    xhat1, rstd1, x1b = _out_proj_norm(y, w_out, x2, g0, b0)

    kd, vd, kt, vt = _kv_proj(x1b, w_kv, tables)
    bmq = bm
    tab_spec = pl.BlockSpec((bmq, LANES), lambda i, j: (i, 0))

    def rope_scale(val, cos_ref, sa_ref, sb_ref):
        cos, sa, sb = cos_ref[...], sa_ref[...], sb_ref[...]
        return jnp.concatenate([_rope(val[:, j * LANES:(j + 1) * LANES], cos, sa, sb) * 0.125
                                for j in range(val.shape[1] // LANES)], axis=1)

    qs = _mm("b_q_proj", x1b, w_qg, dims=NN, grid=(seq // bmq, dm // bn),
             a_spec=pl.BlockSpec((bmq, dm), lambda i, j: (i, 0)), b_spec=pl.BlockSpec((dm, bn), lambda i, j: (0, j)),
             out_shape=jax.ShapeDtypeStruct((seq, dm), BF16), out_spec=pl.BlockSpec((bmq, bn), lambda i, j: (i, j)),
             epilogue=rope_scale, extras=tables, extra_specs=(tab_spec,) * 3)
    zb = _mm("b_gate_proj", x1b, w_qg, dims=NN, grid=(seq // bm, dm // bn),
             a_spec=pl.BlockSpec((bm, dm), lambda i, j: (i, 0)),
             b_spec=pl.BlockSpec((dm, bn), lambda i, j: (0, j + dm // bn)),
             out_shape=jax.ShapeDtypeStruct((seq, dm), F32), out_spec=pl.BlockSpec((bm, bn), lambda i, j: (i, j)))
    att, yb = _attn_forward(qs, kd, vt, zb, b_sinks)
    dr2, dr2b, stats_b = _out_proj_norm_loss(yb, w_outb, xhat1, g0, b0, g1, b1, target)

    def weight_grad(name, a, b, n_cols, b_spec=None, part=(0, 1), after=None):
        m_cols = a.shape[1] // part[1]
        tm, tn = _tile(m_cols, 1024), _tile(n_cols, 1024)
        first = part[0] * (m_cols // tm)
        return _mm(name, a, b, dims=TN, grid=(m_cols // tm, n_cols // tn),
                   a_spec=pl.BlockSpec((seq, tm), lambda i, j: (0, first + i)),
                   b_spec=b_spec(tn) if b_spec else pl.BlockSpec((seq, tn), lambda i, j: (0, j)),
                   out_shape=jax.ShapeDtypeStruct((m_cols, n_cols), BF16),
                   out_spec=pl.BlockSpec((tm, tn), lambda i, j: (i, j)),
                   extras=() if after is None else (after,), extra_specs=() if after is None else (ANY,))

    def halves_spec(tn):
        per = dm // tn
        return pl.BlockSpec((None, seq, tn), lambda i, j: (j // per, 0, j % per))

    def times_transposed(name, a, w):
        return _mm(name, a, w, dims=NT, grid=(seq // bm, dm // bn),
                   a_spec=pl.BlockSpec((bm, a.shape[1]), lambda i, j: (i, 0)),
                   b_spec=pl.BlockSpec((bn, w.shape[1]), lambda i, j: (j, 0)),
                   out_shape=jax.ShapeDtypeStruct((seq, dm), F32), out_spec=pl.BlockSpec((bm, bn), lambda i, j: (i, j)))

    def stat_row_cols(ref, dev):
        return ref.at[pl.ds(0, 1), pl.ds(pl.multiple_of(dev * rb, LANES), rb)]

    upd = {}
    last = [dr2b]
    my_core = lax.axis_index("c").astype(jnp.int32).reshape(1)

    def then(value):
        last[0] = value[0] if isinstance(value, (list, tuple)) else value
        return value

    def shard_update(key, parts, w, m, v):
        shape = w.shape
        flat = lambda a: a.reshape(-1, shape[-1])
        parts = list(parts) if isinstance(parts, (list, tuple)) else [parts]
        outs = then(_adamw_shard("adamw_" + key, [p.reshape(p.shape[0], -1, shape[-1]) for p in parts], flat(w),
                                 flat(m), flat(v), last[0]))
        upd[key] = [o.reshape(shape) for o in outs]

    def two_level_scatter(name, ids, streams):
        staged = _sibling_exchange(name + "_pair", streams, ids[0])

        def finish():
            sums = [then(_pair_sum(f"{name}_sum{s}", st[0], got, my_core, last[0]))
                    for s, (st, got) in enumerate(zip(streams, staged))]
            return _chip_exchange(name + "_chip", sums, ids[1])
        return finish

    d_w_outb = then(weight_grad("b_out_proj_dw", yb, dr2b, dm))
    (p_outb,) = _exchange_blocks("scatter_b_out", [(d_w_outb, rows, (rb, dm))], 4)
    dyb = times_transposed("b_out_proj_dx", dr2b, w_outb)
    dhq, dkd, dvd, dsink = then(_attn_backward(qs, kd, vd, kt, zb, att, dyb, b_sinks, tables, after=last[0]))
    dkv = _kv_grad_fold(dkd, dvd, tables)
    d_w_kv = weight_grad("b_kv_proj_dw", x1b, dkv, 2 * kvw)
    d_w_qg = then(weight_grad("b_qg_proj_dw", x1b, dhq, 2 * dm, halves_spec, after=d_w_kv))
    finish_b_in = two_level_scatter("scatter_b_in", (5, 11), [(d_w_qg, cols, (dm, cb))])
    dr1, dr1b, stats_a = _stream_grad_norm_backward(dhq, w_qg, dkv, w_kv, dr2, xhat1, rstd1, g0, after=last[0])
    last[0] = dr1b
    shard_update("b_w_out", p_outb, b_w_out, m_b_w_out, v_b_w_out)
    (p_qg,) = finish_b_in()
    all_b, all_a, all_sink = _exchange_blocks("gather_replicated_grads", [
        (stats_b, None, stats_b.shape), (stats_a, None, stats_a.shape), (dsink, None, dsink.shape)], 9)

    d_w_out = then(weight_grad("a_out_proj_dw", y, dr1b, dm, after=last[0]))
    p_out, p_k, p_v = _exchange_blocks("scatter_a_out", [
        (d_w_out, rows, (rb, dm)), (d_w_kv, k_rows, (rb, kvw)), (d_w_kv, v_rows, (rb, kvw))], 6)
    dh, d_w_g, stats_s = then(_pool_mid_backward(dr1b, w_out, mixed, z_a, pooled, w_g, scale, after=last[0]))
    p_g, p_scale = _exchange_blocks("scatter_a_mid", [
        (d_w_g, group_rows, (n_groups, gb, gd)), (stats_s, stat_row_cols, (1, rb))], 7)
    shard_update("b_w_qg", p_qg, b_w_qg, m_b_w_qg, v_b_w_qg)
    rep = then(_adamw_replicated(all_b, all_a, all_sink, ln_g, ln_b, b_sinks, m_ln_g, m_ln_b, m_b_sinks, v_ln_g,
                                 v_ln_b, v_b_sinks, last[0]))
    upd["ln_g"], upd["ln_b"], upd["b_sinks"] = list(rep[0:4]), list(rep[4:8]), list(rep[8:12])
    finish_a_in = []
    for k in range(2):
        d_w_in = then(weight_grad(f"a_in_proj_dw_{k}", xb, dh, 2 * dm, halves_spec, part=(k, 2), after=last[0]))
        finish_a_in.append(two_level_scatter(f"scatter_a_in_{k}", (8 + 2 * k, 12 + k), [(d_w_in, cols, (dm // 2, cb))]))
    shard_update("a_w_out", p_out, a_w_out, m_a_w_out, v_a_w_out)
    shard_update("b_w_k", p_k, b_w_k, m_b_w_k, v_b_w_k)
    shard_update("b_w_v", p_v, b_w_v, m_b_w_v, v_b_w_v)
    shard_update("a_w_group", p_g, a_w_group, m_a_w_group, v_a_w_group)
    shard_update("a_scale", p_scale, a_scale, m_a_scale, v_a_scale)
    p_in = list(finish_a_in[0]()) + list(finish_a_in[1]())
    grad_x = then(_input_grad(dh, w_in, dr1, last[0]))
    flat_in = [a.reshape(dm, cb) for a in (a_w_in, m_a_w_in, v_a_w_in)]
    half = _adamw_shard("adamw_a_w_in_0", [p_in[0]], *flat_in, last[0], row_range=(0, dm // 2))
    full = _adamw_shard("adamw_a_w_in_1", [p_in[1]], *flat_in, half[0], row_range=(dm // 2, dm), into=half)
    upd["a_w_in"] = [o.reshape(a_w_in.shape) for o in full]

    loss = rep[12].reshape(())
    order = ["ln_g", "ln_b", "a_w_in", "a_w_group", "a_scale", "a_w_out", "b_w_k", "b_w_v", "b_w_qg", "b_sinks",
             "b_w_out"]
    return (loss, grad_x.reshape(x.shape), *[upd[n][0] for n in order], *[upd[n][1] for n in order],
            *[upd[n][2] for n in order], *[upd[n][3] for n in order])
```

```python
import functools

import jax
import jax.numpy as jnp
from jax import lax
from jax.experimental import pallas as pl
from jax.experimental.pallas import tpu as pltpu
from jax.experimental.pallas import tpu_sc as plsc

F32 = jnp.float32
BF16 = jnp.bfloat16
MESH = pl.DeviceIdType.MESH
AXES = ("x", "y", "c")
N_DEV = 8

POOL_WINDOWS = (2, 4, 8, 16)
POOL_HALO = 16
HEAD_DIM = 64
GQA_GROUP = 8
ATTN_BLOCK = 128
ROPE_THETA = 10000.0
LN_EPS = 1e-5
NEG_INF = -1e30
DEPTH = 2
ALPHA = (2 * DEPTH) ** 0.25
ADAM_LR = 0.001
ADAM_B1 = 0.9
ADAM_B2 = 0.999
ADAM_EPS = 1e-08
ADAM_WD = 0.01
ADAM_STEP = 10

LANES = 128
STAT_ROWS = 8


def _tile(n, want):
    t = min(n, want)
    while n % t:
        t //= 2
    return t


def _params(*sem):
    return pltpu.CompilerParams(dimension_semantics=sem)


ANY = pl.BlockSpec(memory_space=pl.ANY)


def _my_pos():
    return lax.axis_index("x"), lax.axis_index("y"), lax.axis_index("c")


def _dev_index(p):
    return 4 * p[0] + 2 * p[1] + p[2]


def _handshake(peers):
    barrier = pltpu.get_barrier_semaphore()
    for peer in peers:
        pl.semaphore_signal(barrier, inc=1, device_id=peer, device_id_type=MESH)
    pl.semaphore_wait(barrier, len(peers))


def _launch_on_sequencer(name, collective_id, body, operands, out_shapes, scratch):
    return pl.kernel(
        body, out_type=tuple(out_shapes), name=name,
        mesh=plsc.ScalarSubcoreMesh(axis_name="sequencer", num_cores=1), scratch_types=scratch,
        compiler_params=pltpu.CompilerParams(collective_id=collective_id),
    )(*operands)


def _gather_weights(name, collective_id, streams, out_shapes):
    n_s = len(streams)
    n_out = len(out_shapes)

    def body(*refs):
        srcs = refs[:n_s]
        outs = refs[n_s:n_s + n_out]
        send_sems, recv_sems, local_sems = refs[n_s + n_out:]
        x, y, c = _my_pos()
        me, sibling = (x, y, c), (x, y, 1 - c)
        x_nbr, y_nbr, diag = (1 - x, y), (x, 1 - y), (1 - x, 1 - y)
        _handshake([sibling, (*x_nbr, c), (*y_nbr, c)])
        south = c == 0
        relay_from = (jnp.where(south, 1 - x, x), jnp.where(south, y, 1 - y))
        relay_to = (jnp.where(south, x, 1 - x), jnp.where(south, 1 - y, y))
        early, late = jnp.where(south, 1, 2), jnp.where(south, 2, 1)

        def copy(s, k, block, to, from_shard=False):
            out_ref = outs[streams[s][1]]
            win = streams[s][2](out_ref, _dev_index(block))
            return pltpu.make_async_remote_copy(
                src_ref=srcs[s] if from_shard else win, dst_ref=win,
                send_sem=send_sems.at[7 * s + k], recv_sem=recv_sems.at[7 * s + k],
                device_id=to, device_id_type=MESH)

        mine = [pltpu.make_async_copy(srcs[s], streams[s][2](outs[streams[s][1]], _dev_index(me)), local_sems.at[s])
                for s in range(n_s)]
        for cp in mine:
            cp.start()
        sent = []
        for s in range(n_s):
            sent += [copy(s, 0, me, sibling, True), copy(s, 1, me, (*x_nbr, c), True), copy(s, 2, me, (*y_nbr, c), True)]
        for cp in sent:
            cp.start()
        for s in range(n_s):
            copy(s, early, (*relay_from, c), me).wait_recv()
            sent += [copy(s, 3, (*relay_from, c), (*relay_to, c)), copy(s, 3 + early, (*relay_from, c), sibling)]
            for cp in sent[-2:]:
                cp.start()
        for s in range(n_s):
            copy(s, late, (*relay_to, c), me).wait_recv()
            sent.append(copy(s, 3 + late, (*relay_to, c), sibling))
            sent[-1].start()
        for s in range(n_s):
            copy(s, 3, (*diag, c), me).wait_recv()
            sent.append(copy(s, 6, (*diag, c), sibling))
            sent[-1].start()
        for s in range(n_s):
            copy(s, 0, sibling, me).wait_recv()
            for k, chip in ((4, x_nbr), (5, y_nbr), (6, diag)):
                copy(s, k, (*chip, 1 - c), me).wait_recv()
        for cp in sent:
            cp.wait_send()
        for cp in mine:
            cp.wait()

    scratch = [pltpu.SemaphoreType.DMA((7 * n_s,)), pltpu.SemaphoreType.DMA((7 * n_s,)),
               pltpu.SemaphoreType.DMA((n_s,))]
    return _launch_on_sequencer(name, collective_id, body, [s[0] for s in streams], out_shapes, scratch)


def _exchange_blocks(name, streams, collective_id):
    n_s = len(streams)

    def body(*refs):
        srcs = refs[:n_s]
        outs = refs[n_s:2 * n_s]
        send_sems, recv_sems, local_sems = refs[2 * n_s:]
        x, y, c = _my_pos()
        me = _dev_index((x, y, c))
        _handshake([(1 - x if k & 4 else x, 1 - y if k & 2 else y, 1 - c if k & 1 else c) for k in range(1, N_DEV)])

        def window(s, dev):
            return srcs[s] if streams[s][1] is None else streams[s][1](srcs[s], dev)

        mine = [pltpu.make_async_copy(window(s, me), outs[s].at[me], local_sems.at[s]) for s in range(n_s)]
        for cp in mine:
            cp.start()
        copies = []
        for k in (2, 4, 6, 3, 5, 7, 1):
            peer = (1 - x if k & 4 else x, 1 - y if k & 2 else y, 1 - c if k & 1 else c)
            for s in range(n_s):
                copies.append(pltpu.make_async_remote_copy(
                    src_ref=window(s, _dev_index(peer)), dst_ref=outs[s].at[me],
                    send_sem=send_sems.at[7 * s + k - 1], recv_sem=recv_sems.at[7 * s + k - 1],
                    device_id=peer, device_id_type=MESH))
        for cp in copies:
            cp.start()
        for cp in copies:
            cp.wait()
        for cp in mine:
            cp.wait()

    out_shapes = [jax.ShapeDtypeStruct((N_DEV,) + tuple(s[2]), s[0].dtype) for s in streams]
    scratch = [pltpu.SemaphoreType.DMA((7 * n_s,)), pltpu.SemaphoreType.DMA((7 * n_s,)),
               pltpu.SemaphoreType.DMA((n_s,))]
    return _launch_on_sequencer(name, collective_id, body, [s[0] for s in streams], out_shapes, scratch)


N_CHIPS = 4


def _sibling_exchange(name, streams, collective_id):
    n_s = len(streams)

    def body(*refs):
        srcs = refs[:n_s]
        outs = refs[n_s:2 * n_s]
        send_sems, recv_sems = refs[2 * n_s:]
        x, y, c = _my_pos()
        sibling = (x, y, 1 - c)
        _handshake([sibling])
        copies = [pltpu.make_async_remote_copy(
            src_ref=streams[s][1](srcs[s], 2 * chip + (1 - c)), dst_ref=outs[s].at[chip],
            send_sem=send_sems.at[N_CHIPS * s + chip], recv_sem=recv_sems.at[N_CHIPS * s + chip],
            device_id=sibling, device_id_type=MESH) for s in range(n_s) for chip in range(N_CHIPS)]
        for cp in copies:
            cp.start()
        for cp in copies:
            cp.wait()

    out_shapes = [jax.ShapeDtypeStruct((N_CHIPS,) + tuple(s[2]), s[0].dtype) for s in streams]
    scratch = [pltpu.SemaphoreType.DMA((N_CHIPS * n_s,)), pltpu.SemaphoreType.DMA((N_CHIPS * n_s,))]
    return _launch_on_sequencer(name, collective_id, body, [s[0] for s in streams], out_shapes, scratch)


def _pair_sum(name, array, from_sibling, my_core, after):
    _, rows, cols = from_sibling.shape
    tr = _tile(rows, 2048)

    def body(core_ref, own_ref, sib_ref, after_ref, o_ref):
        del core_ref, after_ref
        o_ref[...] = (own_ref[...].astype(F32) + sib_ref[...].astype(F32)).astype(o_ref.dtype)

    staged_spec = pl.BlockSpec((None, tr, cols), lambda k, i, core: (k, i, 0))
    return pl.pallas_call(
        body, name=name, out_shape=jax.ShapeDtypeStruct(from_sibling.shape, array.dtype),
        grid_spec=pltpu.PrefetchScalarGridSpec(
            num_scalar_prefetch=1, grid=(N_CHIPS, rows // tr),
            in_specs=[pl.BlockSpec((tr, cols), lambda k, i, core: (i, 2 * k + core[0])), staged_spec, ANY],
            out_specs=staged_spec),
        compiler_params=_params("parallel", "parallel"),
    )(my_core, array, from_sibling, after)


def _chip_exchange(name, pair_sums, collective_id):
    n_s = len(pair_sums)

    def body(*refs):
        srcs = refs[:n_s]
        outs = refs[n_s:2 * n_s]
        send_sems, recv_sems, local_sems = refs[2 * n_s:]
        x, y, c = _my_pos()
        my_chip = 2 * x + y
        chips = [(1 - x, y), (x, 1 - y), (1 - x, 1 - y)]
        _handshake([(*chip, c) for chip in chips])
        mine = [pltpu.make_async_copy(srcs[s].at[my_chip], outs[s].at[my_chip], local_sems.at[s]) for s in range(n_s)]
        copies = [pltpu.make_async_remote_copy(
            src_ref=srcs[s].at[2 * chip[0] + chip[1]], dst_ref=outs[s].at[my_chip],
            send_sem=send_sems.at[3 * s + j], recv_sem=recv_sems.at[3 * s + j],
            device_id=(*chip, c), device_id_type=MESH) for s in range(n_s) for j, chip in enumerate(chips)]
        for cp in mine + copies:
            cp.start()
        for cp in copies:
            cp.wait()
        for cp in mine:
            cp.wait()

    out_shapes = [jax.ShapeDtypeStruct(p.shape, p.dtype) for p in pair_sums]
    scratch = [pltpu.SemaphoreType.DMA((3 * n_s,)), pltpu.SemaphoreType.DMA((3 * n_s,)),
               pltpu.SemaphoreType.DMA((n_s,))]
    return _launch_on_sequencer(name, collective_id, body, list(pair_sums), out_shapes, scratch)


NN = (((1,), (0,)), ((), ()))
NT = (((1,), (1,)), ((), ()))
TN = (((0,), (0,)), ((), ()))


def _mm(name, a, b, *, dims, grid, a_spec, b_spec, out_shape, out_spec, nk=1,
        add=None, add_spec=None, add_scale=1.0, epilogue=None, extras=(), extra_specs=()):
    n_extra = len(extras)
    has_add = add is not None

    def body(*refs):
        a_ref, b_ref = refs[:2]
        pos = 2
        add_ref = None
        if has_add:
            add_ref = refs[pos]
            pos += 1
        extra_refs = refs[pos:pos + n_extra]
        o_ref = refs[pos + n_extra]
        acc_ref = refs[pos + n_extra + 1] if nk > 1 else None

        def finish(val):
            if has_add:
                val = val + add_scale * add_ref[...]
            if epilogue is not None:
                val = epilogue(val, *extra_refs)
            o_ref[...] = val.astype(o_ref.dtype)

        part = lax.dot_general(a_ref[...].astype(BF16), b_ref[...].astype(BF16), dims,
                               preferred_element_type=F32)
        if nk == 1:
            finish(part)
        else:
            k = pl.program_id(2)

            @pl.when(k == 0)
            def _():
                acc_ref[...] = part

            @pl.when(jnp.logical_and(k > 0, k < nk - 1))
            def _():
                acc_ref[...] += part

            @pl.when(k == nk - 1)
            def _():
                finish(acc_ref[...] + part)

    in_specs = [a_spec, b_spec] + ([add_spec] if has_add else []) + list(extra_specs)
    operands = [a, b] + ([add] if has_add else []) + list(extras)
    scratch = [pltpu.VMEM(out_spec.block_shape, F32)] if nk > 1 else []
    sem = ("parallel", "parallel") + (("arbitrary",) if nk > 1 else ())
    return pl.pallas_call(
        body, name=name, grid=grid, out_shape=out_shape,
        in_specs=in_specs, out_specs=out_spec, scratch_shapes=scratch,
        compiler_params=_params(*sem),
    )(*operands)


def _input_grad(dh, w_in, dr1, after):
    _, seq, dm = dh.shape
    bm, bn = _tile(seq, 1024), _tile(dm, 512)

    def body(dh_ref, w_ref, dr_ref, after_ref, o_ref):
        del after_ref
        o_ref[...] = (lax.dot_general(dh_ref[0], w_ref[:, :dm], NT, preferred_element_type=F32)
                      + lax.dot_general(dh_ref[1], w_ref[:, dm:], NT, preferred_element_type=F32)
                      + ALPHA * dr_ref[...])

    tile_spec = pl.BlockSpec((bm, bn), lambda i, j: (i, j))
    return pl.pallas_call(
        body, name="a_in_proj_dx", grid=(seq // bm, dm // bn),
        out_shape=jax.ShapeDtypeStruct((seq, dm), F32),
        in_specs=[pl.BlockSpec((2, bm, dm), lambda i, j: (0, i, 0)), pl.BlockSpec((bn, 2 * dm), lambda i, j: (j, 0)),
                  tile_spec, ANY],
        out_specs=tile_spec,
        compiler_params=_params("parallel", "parallel"),
    )(dh, w_in, dr1, after)


def _cast_bf16(name, a):
    rows, cols = a.shape
    tr = _tile(rows, 512)

    def body(a_ref, o_ref):
        o_ref[...] = a_ref[...].astype(BF16)

    return pl.pallas_call(
        body, name=name, grid=(rows // tr,),
        out_shape=jax.ShapeDtypeStruct(a.shape, BF16),
        in_specs=[pl.BlockSpec((tr, cols), lambda i: (i, 0))],
        out_specs=pl.BlockSpec((tr, cols), lambda i: (i, 0)),
        compiler_params=_params("parallel"),
    )(a)


def _rope_tables(seq):
    inv_freq = ROPE_THETA ** (-jnp.arange(0, HEAD_DIM, 2, dtype=F32) / HEAD_DIM)
    ang = jnp.arange(seq, dtype=F32)[:, None] * inv_freq[None, :]
    cos, sin = jnp.cos(ang), jnp.sin(ang)
    cos, sin = (jnp.concatenate([t, t, t, t], axis=-1) for t in (cos, sin))
    first_half = (jnp.arange(LANES) % HEAD_DIM < HEAD_DIM // 2)[None, :]
    return cos, jnp.where(first_half, -sin, 0.0), jnp.where(first_half, 0.0, sin)


def _rot(t, sin_a, sin_b):
    return pltpu.roll(t, LANES - HEAD_DIM // 2, 1) * sin_a + pltpu.roll(t, HEAD_DIM // 2, 1) * sin_b


def _rope(t, cos, sin_a, sin_b):
    return t * cos + _rot(t, sin_a, sin_b)


def _rope_transposed(dy, cos, sin_a, sin_b):
    return dy * cos - _rot(dy, sin_a, sin_b)


def _silu_parts(z):
    sig = jax.nn.sigmoid(z)
    return z * sig, sig * (1.0 + z * (1.0 - sig))


def _layer_norm_stats(r):
    mu = jnp.mean(r, axis=-1, keepdims=True)
    d = r - mu
    var = jnp.mean(d * d, axis=-1, keepdims=True)
    rstd = lax.rsqrt(var + LN_EPS)
    return d * rstd, rstd


def _layer_norm_backward(dout, xhat, rstd, gain):
    dxh = dout * gain
    m1 = jnp.mean(dxh, axis=-1, keepdims=True)
    m2 = jnp.mean(dxh * xhat, axis=-1, keepdims=True)
    return rstd * (dxh - m1 - xhat * m2)


def _col_sum(v):
    return jnp.sum(v, axis=0, keepdims=True)


ROW_PART = 128


def _row_parts(tile):
    part = min(tile, ROW_PART)
    return [slice(r, r + part) for r in range(0, tile, part)]


def _pool_forward(xb, w_in, wg, scale):
    seq, dm = xb.shape
    n_g = len(POOL_WINDOWS)
    gd = dm // n_g
    tile = _tile(seq, 1024)
    halo_blocks = tile // POOL_HALO

    def body(x_ref, xp_ref, wu_ref, wz_ref, wg_ref, sc_ref, y_ref, p_ref, mx_ref, z_ref):
        i, g = pl.program_id(0), pl.program_id(1)
        u = jnp.dot(x_ref[...], wu_ref[...], preferred_element_type=F32)
        z = jnp.dot(x_ref[...], wz_ref[...], preferred_element_type=F32)
        prev = jnp.where(i > 0, jnp.dot(xp_ref[...], wu_ref[...], preferred_element_type=F32), 0.0)
        s = jnp.concatenate([prev, u], axis=0)
        sums, sh = [], 1
        while sh < POOL_WINDOWS[-1]:
            s = s + pltpu.roll(s, sh, 0)
            sums.append(s)
            sh *= 2
        win = sums[-1]
        for k in range(n_g - 2, -1, -1):
            win = jnp.where(g == k, sums[k], win)
        row = i * tile + lax.broadcasted_iota(jnp.int32, (tile, 1), 0)
        window = jnp.left_shift(2, g).astype(F32)
        p = win[POOL_HALO:, :] * (1.0 / jnp.minimum((row + 1).astype(F32), window)) - u
        pb = p.astype(BF16)
        mx = jnp.dot(pb, wg_ref[...], preferred_element_type=F32)
        y_ref[...] = (mx * sc_ref[...] * (z * jax.nn.sigmoid(z))).astype(BF16)
        p_ref[...] = pb
        mx_ref[...] = mx
        z_ref[...] = z

    out_spec = pl.BlockSpec((tile, gd), lambda i, g: (i, g))
    return pl.pallas_call(
        body, name="pool_fwd", grid=(seq // tile, n_g),
        out_shape=(jax.ShapeDtypeStruct((seq, dm), BF16), jax.ShapeDtypeStruct((seq, dm), BF16),
                   jax.ShapeDtypeStruct((seq, dm), F32), jax.ShapeDtypeStruct((seq, dm), F32)),
        in_specs=[pl.BlockSpec((tile, dm), lambda i, g: (i, 0)),
                  pl.BlockSpec((POOL_HALO, dm), lambda i, g: (jnp.maximum(i * halo_blocks - 1, 0), 0)),
                  pl.BlockSpec((dm, gd), lambda i, g: (0, g)),
                  pl.BlockSpec((dm, gd), lambda i, g: (0, n_g + g)),
                  pl.BlockSpec((None, gd, gd), lambda i, g: (g, 0, 0)),
                  pl.BlockSpec((1, gd), lambda i, g: (0, g))],
        out_specs=(out_spec, out_spec, out_spec, out_spec),
        compiler_params=_params("parallel", "parallel"),
    )(xb, xb, w_in, w_in, wg, scale)


def _pool_mid_backward(dr, w_out, mx, z, p, wg, scale, after):
    seq, dm = mx.shape
    gd = dm // len(POOL_WINDOWS)
    tile = _tile(seq, 256)
    n_i = seq // tile

    def body(dr_ref, wout_ref, mx_ref, z_ref, p_ref, wg_ref, sc_ref, after_ref, dh_ref, dwg_ref, st_ref, dwg_acc,
             carry):
        del after_ref
        i = pl.program_id(0)

        def dy_of(g):
            return lax.dot_general(dr_ref[...], wout_ref[g * gd:(g + 1) * gd, :], NT, preferred_element_type=F32)

        dy_ahead = dy_of(0)
        ti = n_i - 1 - i

        @pl.when(i == 0)
        def _():
            dwg_acc[...] = jnp.zeros_like(dwg_acc)
            carry[...] = jnp.zeros_like(carry)
            st_ref[...] = jnp.zeros_like(st_ref)

        row = ti * tile + lax.broadcasted_iota(jnp.int32, (tile, 1), 0)
        count = (row + 1).astype(F32)
        for g, w in enumerate(POOL_WINDOWS):
            cs = slice(g * gd, (g + 1) * gd)
            z = z_ref[:, cs]
            sz, dsz = _silu_parts(z)
            dyg = dy_ahead
            if g + 1 < len(POOL_WINDOWS):
                dy_ahead = dy_of(g + 1)
            mxg = mx_ref[:, cs]
            sc = sc_ref[:, cs]
            t1 = dyg * sz
            st_ref[0:1, cs] += _col_sum(t1 * mxg)
            dh_ref[1, :, cs] = (dyg * (mxg * sc) * dsz).astype(BF16)
            dmx = (t1 * sc).astype(BF16)
            dwg_acc[g] += lax.dot_general(p_ref[:, cs], dmx, TN, preferred_element_type=F32)
            dp = lax.dot_general(dmx, wg_ref[g], NT, preferred_element_type=F32)
            e = dp * (1.0 / jnp.minimum(count, float(w)))
            s = jnp.concatenate([e, carry[:, cs]], axis=0)
            n = tile + POOL_HALO
            sh = 1
            while sh < w:
                s = s + pltpu.roll(s, n - sh, 0)
                sh *= 2
            dh_ref[0, :, cs] = (s[:tile, :] - dp).astype(BF16)
            carry[:, cs] = e[:POOL_HALO, :]

        @pl.when(i == n_i - 1)
        def _():
            dwg_ref[...] = dwg_acc[...].astype(BF16)

    row_spec = pl.BlockSpec((tile, dm), lambda i: (n_i - 1 - i, 0))
    return pl.pallas_call(
        body, name="pool_mid_bwd", grid=(n_i,),
        out_shape=(jax.ShapeDtypeStruct((2, seq, dm), BF16), jax.ShapeDtypeStruct(wg.shape, BF16),
                   jax.ShapeDtypeStruct((STAT_ROWS, dm), F32)),
        in_specs=[row_spec, pl.BlockSpec(w_out.shape, lambda i: (0, 0), pipeline_mode=pl.Buffered(1)),
                  row_spec, row_spec, row_spec,
                  pl.BlockSpec(wg.shape, lambda i: (0, 0, 0)),
                  pl.BlockSpec((1, dm), lambda i: (0, 0)), ANY],
        out_specs=(pl.BlockSpec((2, tile, dm), lambda i: (0, n_i - 1 - i, 0)),
                   pl.BlockSpec(wg.shape, lambda i: (0, 0, 0)),
                   pl.BlockSpec((STAT_ROWS, dm), lambda i: (0, 0))),
        scratch_shapes=[pltpu.VMEM(wg.shape, F32), pltpu.VMEM((POOL_HALO, dm), F32)],
        compiler_params=_params("arbitrary"),
    )(dr, w_out, mx, z, p, wg, scale, after)


def _out_proj_norm(y, w, x, gain, bias):
    seq, dm = x.shape
    tile = _tile(seq, 512)

    def body(y_ref, w_ref, x_ref, g_ref, b_ref, xhat_ref, rstd_ref, xb_ref):
        for rows in _row_parts(tile):
            o = jnp.dot(y_ref[rows, :], w_ref[...], preferred_element_type=F32)
            xhat, rstd = _layer_norm_stats(ALPHA * x_ref[rows, :] + o)
            xhat_ref[rows, :] = xhat
            rstd_ref[rows, :] = rstd
            xb_ref[rows, :] = (xhat * g_ref[...] + b_ref[...]).astype(BF16)

    row_spec = pl.BlockSpec((tile, dm), lambda i: (i, 0))
    vec_spec = pl.BlockSpec((1, dm), lambda i: (0, 0))
    return pl.pallas_call(
        body, name="out_proj_norm_a", grid=(seq // tile,),
        out_shape=(jax.ShapeDtypeStruct((seq, dm), F32), jax.ShapeDtypeStruct((seq, 1), F32),
                   jax.ShapeDtypeStruct((seq, dm), BF16)),
        in_specs=[row_spec, pl.BlockSpec(w.shape, lambda i: (0, 0), pipeline_mode=pl.Buffered(1)), row_spec, vec_spec,
                  vec_spec],
        out_specs=(row_spec, pl.BlockSpec((tile, 1), lambda i: (i, 0)), row_spec),
        compiler_params=_params("parallel"),
    )(y, w, x, gain, bias)


def _kv_proj(xb, wkv, tables):
    seq, dm = xb.shape
    kvw = wkv.shape[1] // 2
    n_kv = kvw // HEAD_DIM
    tile = _tile(seq, 1024)

    def body(x_ref, w_ref, cos_ref, sa_ref, sb_ref, kd_ref, vd_ref, kt_ref, vt_ref):
        kv = jnp.dot(x_ref[...], w_ref[...], preferred_element_type=F32)
        low = lax.broadcasted_iota(jnp.int32, (1, LANES), 1) < HEAD_DIM
        cos, sa, sb = cos_ref[...], sa_ref[...], sb_ref[...]

        def put(pair, h, nat_ref, t_ref):
            swapped = pltpu.roll(pair, HEAD_DIM, 1)
            for head, dup in ((h, jnp.where(low, pair, swapped)), (h + 1, jnp.where(low, swapped, pair))):
                nat_ref[head] = dup.astype(BF16)
                t_ref[head] = dup.T.astype(BF16)

        for j in range(kvw // LANES):
            put(_rope(kv[:, j * LANES:(j + 1) * LANES], cos, sa, sb), 2 * j, kd_ref, kt_ref)
            put(kv[:, kvw + j * LANES:kvw + (j + 1) * LANES], 2 * j, vd_ref, vt_ref)

    tab_spec = pl.BlockSpec((tile, LANES), lambda i: (i, 0))
    dup_spec = pl.BlockSpec((n_kv, tile, LANES), lambda i: (0, i, 0))
    dup_shape = jax.ShapeDtypeStruct((n_kv, seq, LANES), BF16)
    t_spec = pl.BlockSpec((n_kv, LANES, tile), lambda i: (0, 0, i))
    t_shape = jax.ShapeDtypeStruct((n_kv, LANES, seq), BF16)
    return pl.pallas_call(
        body, name="kv_proj", grid=(seq // tile,),
        out_shape=(dup_shape, dup_shape, t_shape, t_shape),
        in_specs=[pl.BlockSpec((tile, dm), lambda i: (i, 0)), pl.BlockSpec(wkv.shape, lambda i: (0, 0)),
                  tab_spec, tab_spec, tab_spec],
        out_specs=(dup_spec, dup_spec, t_spec, t_spec),
        compiler_params=_params("parallel"),
    )(xb, wkv, *tables)


ATTN_STEP_BLOCKS = 16


def _head_queries(q_ref, rows, low):
    parts = []
    for j in range(GQA_GROUP // 2):
        q2 = q_ref[rows, j * LANES:(j + 1) * LANES]
        parts += [jnp.where(low, q2, 0), jnp.where(low, 0, q2)]
    return parts


def _key_window(prev_ref, cur_ref, b, axis):
    def block(i):
        sl = slice(i * ATTN_BLOCK, (i + 1) * ATTN_BLOCK)
        return cur_ref[sl, :] if axis == 0 else cur_ref[:, sl]
    return jnp.concatenate([prev_ref[...] if b == 0 else block(b - 1), block(b)], axis=axis)


def _mask_bias(n):
    key = lax.broadcasted_iota(jnp.int32, (2 * ATTN_BLOCK, ATTN_BLOCK), 0)
    qry = lax.broadcasted_iota(jnp.int32, (2 * ATTN_BLOCK, ATTN_BLOCK), 1)
    valid = (key > qry) & (key <= qry + ATTN_BLOCK) & ((key >= ATTN_BLOCK) | (n > 0))
    return jnp.where(valid, 0.0, NEG_INF)


def _head_probs_transposed(kcat, qm, bias, sink):
    st = lax.dot_general(kcat, qm, NT, preferred_element_type=F32) + bias
    m = jnp.maximum(jnp.max(st, axis=0, keepdims=True), sink)
    e = jnp.exp(st - m)
    e_sink = jnp.exp(sink - m)
    inv = 1.0 / (jnp.sum(e, axis=0, keepdims=True) + e_sink)
    return e * inv, e_sink * inv


def _probs_transposed(n, kh, kcat, q_all, sink_ref):
    st = lax.dot_general(kcat, q_all, NT, preferred_element_type=F32)
    st = st + jnp.tile(_mask_bias(n), (1, GQA_GROUP))
    sink = jnp.concatenate([jnp.full((1, ATTN_BLOCK), sink_ref[0, kh * GQA_GROUP + h], F32)
                            for h in range(GQA_GROUP)], axis=1)
    m = jnp.maximum(jnp.max(st, axis=0, keepdims=True), sink)
    e = jnp.exp(st - m)
    e_sink = jnp.exp(sink - m)
    inv = 1.0 / (jnp.sum(e, axis=0, keepdims=True) + e_sink)
    return e * inv, e_sink * inv


def _attn_specs(n_width, qb):
    rows = qb * ATTN_BLOCK
    before = lambda n: jnp.maximum(n * qb - 1, 0)
    q_spec = pl.BlockSpec((rows, n_width), lambda kh, n: (n, kh))
    cur = pl.BlockSpec((None, rows, LANES), lambda kh, n: (kh, n, 0))
    prev = pl.BlockSpec((None, ATTN_BLOCK, LANES), lambda kh, n: (kh, before(n), 0))
    cur_t = pl.BlockSpec((None, LANES, rows), lambda kh, n: (kh, 0, n))
    prev_t = pl.BlockSpec((None, LANES, ATTN_BLOCK), lambda kh, n: (kh, 0, before(n)))
    return q_spec, cur, prev, cur_t, prev_t


def _pair_product_transposed(mat_t, rhs, j, low_rows):
    head_a = slice(2 * j * ATTN_BLOCK, (2 * j + 1) * ATTN_BLOCK)
    head_b = slice((2 * j + 1) * ATTN_BLOCK, (2 * j + 2) * ATTN_BLOCK)
    out_t = (jnp.dot(jnp.where(low_rows, mat_t, 0), rhs[:, head_a], preferred_element_type=F32)
             + jnp.dot(jnp.where(low_rows, 0, mat_t), rhs[:, head_b], preferred_element_type=F32))
    return out_t.T


def _attn_forward(qs, kd, vt, zb, sinks):
    seq, dm = qs.shape
    n_kv = kd.shape[0]
    gw = GQA_GROUP * HEAD_DIM

    qb = ATTN_STEP_BLOCKS if (seq // ATTN_BLOCK) % ATTN_STEP_BLOCKS == 0 else 1

    def body(q_ref, kp_ref, kc_ref, vtp_ref, vtc_ref, z_ref, sink_ref, att_ref, yb_ref):
        kh, n = pl.program_id(0), pl.program_id(1)
        low = lax.broadcasted_iota(jnp.int32, (1, LANES), 1) < HEAD_DIM
        low_rows = lax.broadcasted_iota(jnp.int32, (LANES, 1), 0) < HEAD_DIM
        for b in range(qb):
            rows = slice(b * ATTN_BLOCK, (b + 1) * ATTN_BLOCK)
            kcat = _key_window(kp_ref, kc_ref, b, 0)
            vt = _key_window(vtp_ref, vtc_ref, b, 1)
            bias = _mask_bias(n * qb + b)
            pt = jnp.concatenate(
                [_head_probs_transposed(kcat, qm, bias, sink_ref[0, kh * GQA_GROUP + h])[0].astype(BF16)
                 for h, qm in enumerate(_head_queries(q_ref, rows, low))], axis=1)
            for j in range(GQA_GROUP // 2):
                cs = slice(j * LANES, (j + 1) * LANES)
                o2 = _pair_product_transposed(vt, pt, j, low_rows)
                att_ref[rows, cs] = o2
                z = z_ref[rows, cs]
                yb_ref[rows, cs] = (o2 * (z * jax.nn.sigmoid(z))).astype(BF16)

    q_spec, cur, prev, cur_t, prev_t = _attn_specs(gw, qb)
    return pl.pallas_call(
        body, name="attn_fwd", grid=(n_kv, seq // (qb * ATTN_BLOCK)),
        out_shape=(jax.ShapeDtypeStruct((seq, dm), F32), jax.ShapeDtypeStruct((seq, dm), BF16)),
        in_specs=[q_spec, prev, cur, prev_t, cur_t, q_spec, pl.BlockSpec(memory_space=pltpu.SMEM)],
        out_specs=(q_spec, q_spec),
        compiler_params=_params("parallel", "parallel"),
    )(qs, kd, kd, vt, vt, zb, sinks)


def _attn_backward(qs, kd, vd, kt, zb, att, dyb, sinks, tables, after):
    seq, dm = qs.shape
    n_kv = kd.shape[0]
    gw = GQA_GROUP * HEAD_DIM
    n_blocks = seq // ATTN_BLOCK
    qb = ATTN_STEP_BLOCKS if n_blocks % ATTN_STEP_BLOCKS == 0 else 1

    def body(q_ref, kp_ref, kc_ref, vp_ref, vc_ref, ktp_ref, ktc_ref, z_ref, att_ref, dyb_ref, sink_ref,
             cos_ref, sa_ref, sb_ref, after_ref, dh_ref, dk_ref, dv_ref, ds_ref):
        del after_ref
        kh, n = pl.program_id(0), pl.program_id(1)

        @pl.when(n == 0)
        def _():
            dk_ref[...] = jnp.zeros_like(dk_ref)
            dv_ref[...] = jnp.zeros_like(dv_ref)

        @pl.when(jnp.logical_and(n == 0, kh == 0))
        def _():
            ds_ref[...] = jnp.zeros_like(ds_ref)

        low = lax.broadcasted_iota(jnp.int32, (1, LANES), 1) < HEAD_DIM
        low_rows = lax.broadcasted_iota(jnp.int32, (LANES, 1), 0) < HEAD_DIM
        head_lane = lax.broadcasted_iota(jnp.int32, (1, LANES), 1)
        dsink = jnp.zeros((1, LANES), F32)
        for b in range(qb):
            rows = slice(b * ATTN_BLOCK, (b + 1) * ATTN_BLOCK)
            kcat = _key_window(kp_ref, kc_ref, b, 0)
            vcat = _key_window(vp_ref, vc_ref, b, 0)
            kt = _key_window(ktp_ref, ktc_ref, b, 1)
            cos, sa, sb = cos_ref[rows, :], sa_ref[rows, :], sb_ref[rows, :]
            q_all = jnp.concatenate(_head_queries(q_ref, rows, low), axis=0)
            d_parts = []
            for j in range(GQA_GROUP // 2):
                cs = slice(j * LANES, (j + 1) * LANES)
                sz, dsz = _silu_parts(z_ref[rows, cs])
                dy2 = dyb_ref[rows, cs]
                dh_ref[1, rows, cs] = (dy2 * att_ref[rows, cs] * dsz).astype(BF16)
                datt = (dy2 * sz).astype(BF16)
                d_parts += [jnp.where(low, datt, 0), jnp.where(low, 0, datt)]
            d_all = jnp.concatenate(d_parts, axis=0)
            probs_t, sink_p = _probs_transposed(n * qb + b, kh, kcat, q_all, sink_ref)
            dprobs_t = lax.dot_general(vcat, d_all, NT, preferred_element_type=F32)
            row_dot = jnp.sum(probs_t * dprobs_t, axis=0, keepdims=True)
            ds_t = (probs_t * (dprobs_t - row_dot)).astype(BF16)
            dk = jnp.dot(ds_t, q_all, preferred_element_type=F32)
            dv = jnp.dot(probs_t.astype(BF16), d_all, preferred_element_type=F32)
            for j in range(GQA_GROUP // 2):
                dq2 = _pair_product_transposed(kt, ds_t, j, low_rows)
                dh_ref[0, rows, j * LANES:(j + 1) * LANES] = (
                    _rope_transposed(dq2, cos, sa, sb) * 0.125).astype(BF16)
            sink_dot = sink_p * row_dot
            for h in range(GQA_GROUP):
                part = jnp.sum(sink_dot[:, h * ATTN_BLOCK:(h + 1) * ATTN_BLOCK], axis=1, keepdims=True)
                dsink = dsink - jnp.where(head_lane == kh * GQA_GROUP + h, part, 0.0)

            def add_window(dk=dk, dv=dv, b=b):
                start = pl.multiple_of((n * qb + b - 1) * ATTN_BLOCK, ATTN_BLOCK)
                dk_ref[pl.ds(start, 2 * ATTN_BLOCK), :] += dk
                dv_ref[pl.ds(start, 2 * ATTN_BLOCK), :] += dv

            if b > 0:
                add_window()
            else:
                pl.when(n > 0)(add_window)

                @pl.when(n == 0)
                def _(dk=dk, dv=dv):
                    dk_ref[pl.ds(0, ATTN_BLOCK), :] += dk[ATTN_BLOCK:, :]
                    dv_ref[pl.ds(0, ATTN_BLOCK), :] += dv[ATTN_BLOCK:, :]
        ds_ref[0:1, :] += dsink

    q_spec, cur, prev, cur_t, prev_t = _attn_specs(gw, qb)
    tab_spec = pl.BlockSpec((qb * ATTN_BLOCK, LANES), lambda kh, n: (n, 0))
    acc_spec = pl.BlockSpec((None, seq, LANES), lambda kh, n: (kh, 0, 0))
    acc_shape = jax.ShapeDtypeStruct((n_kv, seq, LANES), F32)
    return pl.pallas_call(
        body, name="attn_bwd", grid=(n_kv, n_blocks // qb),
        out_shape=(jax.ShapeDtypeStruct((2, seq, dm), BF16), acc_shape, acc_shape,
                   jax.ShapeDtypeStruct((STAT_ROWS, LANES), F32)),
        in_specs=[q_spec, prev, cur, prev, cur, prev_t, cur_t, q_spec, q_spec, q_spec,
                  pl.BlockSpec(memory_space=pltpu.SMEM), tab_spec, tab_spec, tab_spec, ANY],
        out_specs=(pl.BlockSpec((2, qb * ATTN_BLOCK, gw), lambda kh, n: (0, n, kh)), acc_spec, acc_spec,
                   pl.BlockSpec((STAT_ROWS, LANES), lambda kh, n: (0, 0))),
        compiler_params=_params("arbitrary", "arbitrary"),
    )(qs, kd, kd, vd, vd, kt, kt, zb, att, dyb, sinks, *tables, after)


def _kv_grad_fold(dk, dv, tables):
    n_kv, seq, _ = dk.shape
    kvw = n_kv * HEAD_DIM
    tile = _tile(seq, 512)

    def body(dk_ref, dv_ref, cos_ref, sa_ref, sb_ref, o_ref):
        low = lax.broadcasted_iota(jnp.int32, (1, LANES), 1) < HEAD_DIM
        cos, sa, sb = cos_ref[...], sa_ref[...], sb_ref[...]

        def folded(ref, h):
            t = ref[h]
            return t + pltpu.roll(t, HEAD_DIM, 1)

        for j in range(n_kv // 2):
            ka = _rope_transposed(folded(dk_ref, 2 * j), cos, sa, sb)
            kb = _rope_transposed(folded(dk_ref, 2 * j + 1), cos, sa, sb)
            o_ref[:, j * LANES:(j + 1) * LANES] = jnp.where(low, ka, kb).astype(BF16)
            o_ref[:, kvw + j * LANES:kvw + (j + 1) * LANES] = jnp.where(
                low, folded(dv_ref, 2 * j), folded(dv_ref, 2 * j + 1)).astype(BF16)

    tab_spec = pl.BlockSpec((tile, LANES), lambda i: (i, 0))
    in_spec = pl.BlockSpec((n_kv, tile, LANES), lambda i: (0, i, 0))
    return pl.pallas_call(
        body, name="kv_grad_fold", grid=(seq // tile,),
        out_shape=jax.ShapeDtypeStruct((seq, 2 * kvw), BF16),
        in_specs=[in_spec, in_spec, tab_spec, tab_spec, tab_spec],
        out_specs=pl.BlockSpec((tile, 2 * kvw), lambda i: (i, 0)),
        compiler_params=_params("parallel"),
    )(dk, dv, *tables)


def _out_proj_norm_loss(yb, w, xhat1, gain0, bias0, gain1, bias1, target):
    seq, dm = xhat1.shape
    tile = _tile(seq, 512)

    def body(y_ref, w_ref, xh1_ref, g0_ref, b0_ref, g1_ref, b1_ref, t_ref, dr_ref, drb_ref, st_ref):
        i = pl.program_id(0)

        @pl.when(i == 0)
        def _():
            st_ref[...] = jnp.zeros_like(st_ref)

        parts = _row_parts(tile)
        product = lambda rows: jnp.dot(y_ref[rows, :], w_ref[...], preferred_element_type=F32)
        ahead = product(parts[0])
        for k, rows in enumerate(parts):
            ob = ahead
            if k + 1 < len(parts):
                ahead = product(parts[k + 1])
            x1 = xh1_ref[rows, :] * g0_ref[...] + b0_ref[...]
            xhat, rstd = _layer_norm_stats(ALPHA * x1 + ob)
            err = xhat * g1_ref[...] + b1_ref[...] - t_ref[rows, :]
            dout = err * (1.0 / dm)
            dr = _layer_norm_backward(dout, xhat, rstd, g1_ref[...])
            dr_ref[rows, :] = dr
            drb_ref[rows, :] = dr.astype(BF16)
            st_ref[0:1, :] += _col_sum(dout * xhat)
            st_ref[1:2, :] += _col_sum(dout)
            st_ref[2:3, :] += _col_sum(err * err)

    row_spec = pl.BlockSpec((tile, dm), lambda i: (i, 0))
    vec_spec = pl.BlockSpec((1, dm), lambda i: (0, 0))
    return pl.pallas_call(
        body, name="out_proj_norm_loss_b", grid=(seq // tile,),
        out_shape=(jax.ShapeDtypeStruct((seq, dm), F32), jax.ShapeDtypeStruct((seq, dm), BF16),
                   jax.ShapeDtypeStruct((STAT_ROWS, dm), F32)),
        in_specs=[row_spec, pl.BlockSpec(w.shape, lambda i: (0, 0), pipeline_mode=pl.Buffered(1)), row_spec, vec_spec,
                  vec_spec, vec_spec,
                  vec_spec, row_spec],
        out_specs=(row_spec, row_spec, pl.BlockSpec((STAT_ROWS, dm), lambda i: (0, 0))),
        compiler_params=_params("arbitrary"),
    )(yb, w, xhat1, gain0, bias0, gain1, bias1, target)


def _stream_grad_norm_backward(dhq, wqg, dkv, wkv, dr2, xhat1, rstd1, gain0, after):
    seq, dm = dr2.shape
    tile = _tile(seq, 256)

    def body(dh_ref, wqg_ref, dkv_ref, wkv_ref, dr2_ref, xh_ref, rstd_ref, g_ref, after_ref, dr_ref, drb_ref, st_ref):
        del after_ref

        @pl.when(pl.program_id(0) == 0)
        def _():
            st_ref[...] = jnp.zeros_like(st_ref)

        dx1 = (lax.dot_general(dh_ref[0], wqg_ref[:, :dm], NT, preferred_element_type=F32)
               + lax.dot_general(dh_ref[1], wqg_ref[:, dm:], NT, preferred_element_type=F32)
               + lax.dot_general(dkv_ref[...], wkv_ref[...], NT, preferred_element_type=F32)
               + ALPHA * dr2_ref[...])
        xhat = xh_ref[...]
        dr = _layer_norm_backward(dx1, xhat, rstd_ref[...], g_ref[...])
        dr_ref[...] = dr
        drb_ref[...] = dr.astype(BF16)
        st_ref[0:1, :] += _col_sum(dx1 * xhat)
        st_ref[1:2, :] += _col_sum(dx1)

    row_spec = pl.BlockSpec((tile, dm), lambda i: (i, 0))
    resident = pl.Buffered(1)
    return pl.pallas_call(
        body, name="stream_grad_norm_bwd", grid=(seq // tile,),
        out_shape=(jax.ShapeDtypeStruct((seq, dm), F32), jax.ShapeDtypeStruct((seq, dm), BF16),
                   jax.ShapeDtypeStruct((STAT_ROWS, dm), F32)),
        in_specs=[pl.BlockSpec((2, tile, dm), lambda i: (0, i, 0)),
                  pl.BlockSpec(wqg.shape, lambda i: (0, 0), pipeline_mode=resident),
                  pl.BlockSpec((tile, dkv.shape[1]), lambda i: (i, 0)),
                  pl.BlockSpec(wkv.shape, lambda i: (0, 0), pipeline_mode=resident),
                  row_spec, row_spec, pl.BlockSpec((tile, 1), lambda i: (i, 0)),
                  pl.BlockSpec((1, dm), lambda i: (0, 0)), ANY],
        out_specs=(row_spec, row_spec, pl.BlockSpec((STAT_ROWS, dm), lambda i: (0, 0))),
        compiler_params=_params("arbitrary"),
    )(dhq, wqg, dkv, wkv, dr2, xhat1, rstd1, gain0, after)


def _adamw_math(w, g, m, v):
    m = ADAM_B1 * m + (1.0 - ADAM_B1) * g
    v = ADAM_B2 * v + (1.0 - ADAM_B2) * (g * g)
    m_hat = m / (1.0 - ADAM_B1 ** ADAM_STEP)
    v_hat = v / (1.0 - ADAM_B2 ** ADAM_STEP)
    delta = -ADAM_LR * (m_hat / (jnp.sqrt(v_hat) + ADAM_EPS) + ADAM_WD * w)
    return delta, m, v


def _sum_devices(ref):
    total = ref[0].astype(F32)
    for d in range(1, ref.shape[0]):
        total = total + ref[d].astype(F32)
    return total


def _adamw_shard(name, parts, w, m, v, after, row_range=None, into=None):
    rows, cols = w.shape
    first_row, end_row = row_range or (0, rows)
    n_parts = len(parts)
    part_rows = (end_row - first_row) // n_parts
    tr = _tile(part_rows, max(8, (1 << 18) // cols)) if part_rows >= 8 else part_rows
    per_part = part_rows // tr
    first = first_row // tr
    kept = list(into or ())

    def body(*refs):
        p_refs = refs[:n_parts]
        w_ref, m_ref, v_ref = refs[n_parts:n_parts + 3]
        g_out, d_out, m_out, v_out = refs[n_parts + 4 + len(kept):]
        g = _sum_devices(p_refs[0])
        for k in range(1, n_parts):
            g = jnp.where(pl.program_id(0) >= k * per_part, _sum_devices(p_refs[k]), g)
        delta, m_new, v_new = _adamw_math(w_ref[...], g, m_ref[...], v_ref[...])
        g_out[...] = g
        d_out[...] = delta
        m_out[...] = m_new
        v_out[...] = v_new

    def part_spec(k):
        return pl.BlockSpec((parts[k].shape[0], tr, cols),
                            lambda i: (0, jnp.clip(i - k * per_part, 0, per_part - 1), 0))

    spec = pl.BlockSpec((tr, cols), lambda i: (first + i, 0))
    shape = jax.ShapeDtypeStruct((rows, cols), F32)
    return pl.pallas_call(
        body, name=name, grid=((end_row - first_row) // tr,),
        out_shape=(shape, shape, shape, shape),
        in_specs=[part_spec(k) for k in range(n_parts)] + [spec, spec, spec, ANY] + [ANY] * len(kept),
        out_specs=(spec, spec, spec, spec),
        input_output_aliases={n_parts + 4 + k: k for k in range(len(kept))},
        compiler_params=_params("arbitrary"),
    )(*parts, w, m, v, after, *kept)


def _adamw_replicated(stats_b, stats_a, sink_parts, ln_g, ln_b, sinks, m_ln_g, m_ln_b, m_sinks, v_ln_g, v_ln_b,
                      v_sinks, after):
    n_q = sinks.shape[1]
    dm = ln_g.shape[1]

    def body(sb_ref, sa_ref, sk_ref, g_ref, b_ref, s_ref, mg_ref, mb_ref, ms_ref, vg_ref, vb_ref, vs_ref, after_ref,
             *outs):
        del after_ref
        layer_sums = (_sum_devices(sa_ref), _sum_devices(sb_ref))
        outs[12][...] = jnp.sum(layer_sums[1][2:3, :], axis=1, keepdims=True) * (0.5 / dm)
        for which, (w_ref, m_ref, v_ref) in enumerate(((g_ref, mg_ref, vg_ref), (b_ref, mb_ref, vb_ref))):
            for layer in range(DEPTH):
                row = slice(layer, layer + 1)
                g = layer_sums[layer][which:which + 1, :]
                res = (g,) + _adamw_math(w_ref[row, :], g, m_ref[row, :], v_ref[row, :])
                for o_ref, val in zip(outs[4 * which:4 * which + 4], res):
                    o_ref[row, :] = val
        g = _sum_devices(sk_ref)[0:1, 0:n_q]
        res = (g,) + _adamw_math(s_ref[...], g, ms_ref[...], vs_ref[...])
        for o_ref, val in zip(outs[8:12], res):
            o_ref[...] = val

    vmem = pl.BlockSpec(memory_space=pltpu.VMEM)
    shapes = [jax.ShapeDtypeStruct(a.shape, F32) for a in (ln_g, ln_b, sinks) for _ in range(4)]
    shapes.append(jax.ShapeDtypeStruct((1, 1), F32))
    return pl.pallas_call(
        body, name="adamw_replicated", out_shape=tuple(shapes),
        in_specs=[vmem] * 12 + [ANY], out_specs=tuple([vmem] * 13),
    )(stats_b, stats_a, sink_parts, ln_g, ln_b, sinks, m_ln_g, m_ln_b, m_sinks, v_ln_g, v_ln_b, v_sinks, after)


def kernel(x, ln_g, ln_b, a_w_in, a_w_group, a_scale, a_w_out, b_w_k, b_w_v, b_w_qg, b_sinks, b_w_out, loss_target, m_ln_g, m_ln_b, m_a_w_in, m_a_w_group, m_a_scale, m_a_w_out, m_b_w_k, m_b_w_v, m_b_w_qg, m_b_sinks, m_b_w_out, v_ln_g, v_ln_b, v_a_w_in, v_a_w_group, v_a_scale, v_a_w_out, v_b_w_k, v_b_w_v, v_b_w_qg, v_b_sinks, v_b_w_out):
    _, seq, dm = x.shape
    n_groups = len(POOL_WINDOWS)
    gd = dm // n_groups
    kvw = b_w_k.shape[1]
    cb = 2 * dm // N_DEV
    rb = dm // N_DEV
    gb = gd // N_DEV

    x2 = x.reshape(seq, dm)
    target = loss_target.reshape(seq, dm)
    w_in_s = a_w_in.reshape(dm, cb)
    w_g_s = a_w_group.reshape(n_groups, gb, gd)
    w_out_s = a_w_out.reshape(rb, dm)
    w_qg_s = b_w_qg.reshape(dm, cb)
    w_outb_s = b_w_out.reshape(rb, dm)

    def cols(ref, dev):
        return ref.at[:, pl.ds(pl.multiple_of(dev * cb, LANES), cb)]

    def rows(ref, dev):
        return ref.at[pl.ds(pl.multiple_of(dev * rb, 8), rb), :]

    def group_rows(ref, dev):
        return ref.at[:, pl.ds(pl.multiple_of(dev * gb, 8), gb), :]

    def k_rows(ref, dev):
        return ref.at[pl.ds(pl.multiple_of(dev * rb, 8), rb), pl.ds(0, kvw)]

    def v_rows(ref, dev):
        return ref.at[pl.ds(pl.multiple_of(dev * rb, 8), rb), pl.ds(kvw, kvw)]

    def scale_cols(ref, dev):
        return ref.at[:, pl.ds(pl.multiple_of(dev * rb, LANES), rb)]

    bf = lambda a: a.astype(BF16)
    wide, square = jax.ShapeDtypeStruct((dm, 2 * dm), BF16), jax.ShapeDtypeStruct((dm, dm), BF16)
    w_g, scale, w_in = _gather_weights(
        "gather_a_in", 0, [(bf(w_g_s), 0, group_rows), (a_scale, 1, scale_cols), (bf(w_in_s), 2, cols)],
        [jax.ShapeDtypeStruct((n_groups, gd, gd), BF16), jax.ShapeDtypeStruct((1, dm), F32), wide])
    (w_out,) = _gather_weights("gather_a_out", 1, [(bf(w_out_s), 0, rows)], [square])
    w_kv, w_qg = _gather_weights(
        "gather_b_in", 2, [(bf(b_w_k), 0, k_rows), (bf(b_w_v), 0, v_rows), (bf(w_qg_s), 1, cols)],
        [jax.ShapeDtypeStruct((dm, 2 * kvw), BF16), wide])
    (w_outb,) = _gather_weights("gather_b_out", 3, [(bf(w_outb_s), 0, rows)], [square])

    tables = _rope_tables(seq)
    bm = _tile(seq, 2048)
    bn = _tile(dm, 1024)
    g0, g1, b0, b1 = ln_g[0:1], ln_g[1:2], ln_b[0:1], ln_b[1:2]

    xb = _cast_bf16("cast_x", x2)
    y, pooled, mixed, z_a = _pool_forward(xb, w_in, w_g, scale)
    xhat1, rstd1, x1b = _out_proj_norm(y, w_out, x2, g0, b0)

    kd, vd, kt, vt = _kv_proj(x1b, w_kv, tables)
    bmq = bm
    tab_spec = pl.BlockSpec((bmq, LANES), lambda i, j: (i, 0))

    def rope_scale(val, cos_ref, sa_ref, sb_ref):
        cos, sa, sb = cos_ref[...], sa_ref[...], sb_ref[...]
        return jnp.concatenate([_rope(val[:, j * LANES:(j + 1) * LANES], cos, sa, sb) * 0.125
                                for j in range(val.shape[1] // LANES)], axis=1)

    qs = _mm("b_q_proj", x1b, w_qg, dims=NN, grid=(seq // bmq, dm // bn),
             a_spec=pl.BlockSpec((bmq, dm), lambda i, j: (i, 0)), b_spec=pl.BlockSpec((dm, bn), lambda i, j: (0, j)),
             out_shape=jax.ShapeDtypeStruct((seq, dm), BF16), out_spec=pl.BlockSpec((bmq, bn), lambda i, j: (i, j)),
             epilogue=rope_scale, extras=tables, extra_specs=(tab_spec,) * 3)
    zb = _mm("b_gate_proj", x1b, w_qg, dims=NN, grid=(seq // bm, dm // bn),
             a_spec=pl.BlockSpec((bm, dm), lambda i, j: (i, 0)),
             b_spec=pl.BlockSpec((dm, bn), lambda i, j: (0, j + dm // bn)),
             out_shape=jax.ShapeDtypeStruct((seq, dm), F32), out_spec=pl.BlockSpec((bm, bn), lambda i, j: (i, j)))
    att, yb = _attn_forward(qs, kd, vt, zb, b_sinks)
    dr2, dr2b, stats_b = _out_proj_norm_loss(yb, w_outb, xhat1, g0, b0, g1, b1, target)

    def weight_grad(name, a, b, n_cols, b_spec=None, part=(0, 1), after=None):
        m_cols = a.shape[1] // part[1]
        tm, tn = _tile(m_cols, 1024), _tile(n_cols, 1024)
        first = part[0] * (m_cols // tm)
        return _mm(name, a, b, dims=TN, grid=(m_cols // tm, n_cols // tn),
                   a_spec=pl.BlockSpec((seq, tm), lambda i, j: (0, first + i)),
                   b_spec=b_spec(tn) if b_spec else pl.BlockSpec((seq, tn), lambda i, j: (0, j)),
                   out_shape=jax.ShapeDtypeStruct((m_cols, n_cols), BF16),
                   out_spec=pl.BlockSpec((tm, tn), lambda i, j: (i, j)),
                   extras=() if after is None else (after,), extra_specs=() if after is None else (ANY,))

    def halves_spec(tn):
        per = dm // tn
        return pl.BlockSpec((None, seq, tn), lambda i, j: (j // per, 0, j % per))

    def times_transposed(name, a, w):
        return _mm(name, a, w, dims=NT, grid=(seq // bm, dm // bn),
                   a_spec=pl.BlockSpec((bm, a.shape[1]), lambda i, j: (i, 0)),
                   b_spec=pl.BlockSpec((bn, w.shape[1]), lambda i, j: (j, 0)),
                   out_shape=jax.ShapeDtypeStruct((seq, dm), F32), out_spec=pl.BlockSpec((bm, bn), lambda i, j: (i, j)))

    def stat_row_cols(ref, dev):
        return ref.at[pl.ds(0, 1), pl.ds(pl.multiple_of(dev * rb, LANES), rb)]

    upd = {}
    last = [dr2b]
    my_core = lax.axis_index("c").astype(jnp.int32).reshape(1)

    def then(value):
        last[0] = value[0] if isinstance(value, (list, tuple)) else value
        return value

    def shard_update(key, parts, w, m, v):
        shape = w.shape
        flat = lambda a: a.reshape(-1, shape[-1])
        parts = list(parts) if isinstance(parts, (list, tuple)) else [parts]
        outs = then(_adamw_shard("adamw_" + key, [p.reshape(p.shape[0], -1, shape[-1]) for p in parts], flat(w),
                                 flat(m), flat(v), last[0]))
        upd[key] = [o.reshape(shape) for o in outs]

    def two_level_scatter(name, ids, streams):
        staged = _sibling_exchange(name + "_pair", streams, ids[0])

        def finish():
            sums = [then(_pair_sum(f"{name}_sum{s}", st[0], got, my_core, last[0]))
                    for s, (st, got) in enumerate(zip(streams, staged))]
            return _chip_exchange(name + "_chip", sums, ids[1])
        return finish

    d_w_outb = then(weight_grad("b_out_proj_dw", yb, dr2b, dm))
    (p_outb,) = _exchange_blocks("scatter_b_out", [(d_w_outb, rows, (rb, dm))], 4)
    dyb = times_transposed("b_out_proj_dx", dr2b, w_outb)
    dhq, dkd, dvd, dsink = then(_attn_backward(qs, kd, vd, kt, zb, att, dyb, b_sinks, tables, after=last[0]))
    dkv = _kv_grad_fold(dkd, dvd, tables)
    d_w_kv = weight_grad("b_kv_proj_dw", x1b, dkv, 2 * kvw)
    d_w_qg = then(weight_grad("b_qg_proj_dw", x1b, dhq, 2 * dm, halves_spec, after=d_w_kv))
    finish_b_in = two_level_scatter("scatter_b_in", (5, 11), [(d_w_qg, cols, (dm, cb))])
    p_k, p_v = _exchange_blocks("scatter_b_kv", [(d_w_kv, k_rows, (rb, kvw)), (d_w_kv, v_rows, (rb, kvw))], 14)
    dr1, dr1b, stats_a = _stream_grad_norm_backward(dhq, w_qg, dkv, w_kv, dr2, xhat1, rstd1, g0, after=last[0])
    last[0] = dr1b
    shard_update("b_w_out", p_outb, b_w_out, m_b_w_out, v_b_w_out)
    (p_qg,) = finish_b_in()
    shard_update("b_w_k", p_k, b_w_k, m_b_w_k, v_b_w_k)
    shard_update("b_w_v", p_v, b_w_v, m_b_w_v, v_b_w_v)
    all_b, all_a, all_sink = _exchange_blocks("gather_replicated_grads", [
        (stats_b, None, stats_b.shape), (stats_a, None, stats_a.shape), (dsink, None, dsink.shape)], 9)

    d_w_out = then(weight_grad("a_out_proj_dw", y, dr1b, dm, after=last[0]))
    (p_out,) = _exchange_blocks("scatter_a_out", [(d_w_out, rows, (rb, dm))], 6)
    dh, d_w_g, stats_s = then(_pool_mid_backward(dr1b, w_out, mixed, z_a, pooled, w_g, scale, after=last[0]))
    p_g, p_scale = _exchange_blocks("scatter_a_mid", [
        (d_w_g, group_rows, (n_groups, gb, gd)), (stats_s, stat_row_cols, (1, rb))], 7)
    shard_update("b_w_qg", p_qg, b_w_qg, m_b_w_qg, v_b_w_qg)
    rep = then(_adamw_replicated(all_b, all_a, all_sink, ln_g, ln_b, b_sinks, m_ln_g, m_ln_b, m_b_sinks, v_ln_g,
                                 v_ln_b, v_b_sinks, last[0]))
    upd["ln_g"], upd["ln_b"], upd["b_sinks"] = list(rep[0:4]), list(rep[4:8]), list(rep[8:12])
    finish_a_in = []
    for k in range(2):
        d_w_in = then(weight_grad(f"a_in_proj_dw_{k}", xb, dh, 2 * dm, halves_spec, part=(k, 2), after=last[0]))
        finish_a_in.append(two_level_scatter(f"scatter_a_in_{k}", (8 + 2 * k, 12 + k), [(d_w_in, cols, (dm // 2, cb))]))
    shard_update("a_w_out", p_out, a_w_out, m_a_w_out, v_a_w_out)
    shard_update("a_w_group", p_g, a_w_group, m_a_w_group, v_a_w_group)
    shard_update("a_scale", p_scale, a_scale, m_a_scale, v_a_scale)
    p_in = list(finish_a_in[0]()) + list(finish_a_in[1]())
    grad_x = then(_input_grad(dh, w_in, dr1, last[0]))
    flat_in = [a.reshape(dm, cb) for a in (a_w_in, m_a_w_in, v_a_w_in)]
    half = _adamw_shard("adamw_a_w_in_0", [p_in[0]], *flat_in, last[0], row_range=(0, dm // 2))
    full = _adamw_shard("adamw_a_w_in_1", [p_in[1]], *flat_in, last[0], row_range=(dm // 2, dm), into=half)
    upd["a_w_in"] = [o.reshape(a_w_in.shape) for o in full]

    loss = rep[12].reshape(())
    order = ["ln_g", "ln_b", "a_w_in", "a_w_group", "a_scale", "a_w_out", "b_w_k", "b_w_v", "b_w_qg", "b_sinks",
             "b_w_out"]
    return (loss, grad_x.reshape(x.shape), *[upd[n][0] for n in order], *[upd[n][1] for n in order],
            *[upd[n][2] for n in order], *[upd[n][3] for n in order])
```

```python
import functools

import jax
import jax.numpy as jnp
from jax import lax
from jax.experimental import pallas as pl
from jax.experimental.pallas import tpu as pltpu
from jax.experimental.pallas import tpu_sc as plsc

F32 = jnp.float32
BF16 = jnp.bfloat16
MESH = pl.DeviceIdType.MESH
AXES = ("x", "y", "c")
N_DEV = 8

POOL_WINDOWS = (2, 4, 8, 16)
POOL_HALO = 16
HEAD_DIM = 64
GQA_GROUP = 8
ATTN_BLOCK = 128
ROPE_THETA = 10000.0
LN_EPS = 1e-5
NEG_INF = -1e30
DEPTH = 2
ALPHA = (2 * DEPTH) ** 0.25
ADAM_LR = 0.001
ADAM_B1 = 0.9
ADAM_B2 = 0.999
ADAM_EPS = 1e-08
ADAM_WD = 0.01
ADAM_STEP = 10

LANES = 128
STAT_ROWS = 8


def _tile(n, want):
    t = min(n, want)
    while n % t:
        t //= 2
    return t


def _params(*sem):
    return pltpu.CompilerParams(dimension_semantics=sem)


ANY = pl.BlockSpec(memory_space=pl.ANY)


def _my_pos():
    return lax.axis_index("x"), lax.axis_index("y"), lax.axis_index("c")


def _dev_index(p):
    return 4 * p[0] + 2 * p[1] + p[2]


def _handshake(peers):
    barrier = pltpu.get_barrier_semaphore()
    for peer in peers:
        pl.semaphore_signal(barrier, inc=1, device_id=peer, device_id_type=MESH)
    pl.semaphore_wait(barrier, len(peers))


def _launch_on_sequencer(name, collective_id, body, operands, out_shapes, scratch):
    return pl.kernel(
        body, out_type=tuple(out_shapes), name=name,
        mesh=plsc.ScalarSubcoreMesh(axis_name="sequencer", num_cores=1), scratch_types=scratch,
        compiler_params=pltpu.CompilerParams(collective_id=collective_id),
    )(*operands)


def _gather_weights(name, collective_id, streams, out_shapes):
    n_s = len(streams)
    n_out = len(out_shapes)

    def body(*refs):
        srcs = refs[:n_s]
        outs = refs[n_s:n_s + n_out]
        send_sems, recv_sems, local_sems = refs[n_s + n_out:]
        x, y, c = _my_pos()
        me, sibling = (x, y, c), (x, y, 1 - c)
        x_nbr, y_nbr, diag = (1 - x, y), (x, 1 - y), (1 - x, 1 - y)
        _handshake([sibling, (*x_nbr, c), (*y_nbr, c)])
        south = c == 0
        relay_from = (jnp.where(south, 1 - x, x), jnp.where(south, y, 1 - y))
        relay_to = (jnp.where(south, x, 1 - x), jnp.where(south, 1 - y, y))
        early, late = jnp.where(south, 1, 2), jnp.where(south, 2, 1)

        def copy(s, k, block, to, from_shard=False):
            out_ref = outs[streams[s][1]]
            win = streams[s][2](out_ref, _dev_index(block))
            return pltpu.make_async_remote_copy(
                src_ref=srcs[s] if from_shard else win, dst_ref=win,
                send_sem=send_sems.at[7 * s + k], recv_sem=recv_sems.at[7 * s + k],
                device_id=to, device_id_type=MESH)

        mine = [pltpu.make_async_copy(srcs[s], streams[s][2](outs[streams[s][1]], _dev_index(me)), local_sems.at[s])
                for s in range(n_s)]
        for cp in mine:
            cp.start()
        sent = []
        for s in range(n_s):
            sent += [copy(s, 0, me, sibling, True), copy(s, 1, me, (*x_nbr, c), True), copy(s, 2, me, (*y_nbr, c), True)]
        for cp in sent:
            cp.start()
        for s in range(n_s):
            copy(s, early, (*relay_from, c), me).wait_recv()
            sent += [copy(s, 3, (*relay_from, c), (*relay_to, c)), copy(s, 3 + early, (*relay_from, c), sibling)]
            for cp in sent[-2:]:
                cp.start()
        for s in range(n_s):
            copy(s, late, (*relay_to, c), me).wait_recv()
            sent.append(copy(s, 3 + late, (*relay_to, c), sibling))
            sent[-1].start()
        for s in range(n_s):
            copy(s, 3, (*diag, c), me).wait_recv()
            sent.append(copy(s, 6, (*diag, c), sibling))
            sent[-1].start()
        for s in range(n_s):
            copy(s, 0, sibling, me).wait_recv()
            for k, chip in ((4, x_nbr), (5, y_nbr), (6, diag)):
                copy(s, k, (*chip, 1 - c), me).wait_recv()
        for cp in sent:
            cp.wait_send()
        for cp in mine:
            cp.wait()

    scratch = [pltpu.SemaphoreType.DMA((7 * n_s,)), pltpu.SemaphoreType.DMA((7 * n_s,)),
               pltpu.SemaphoreType.DMA((n_s,))]
    return _launch_on_sequencer(name, collective_id, body, [s[0] for s in streams], out_shapes, scratch)


def _exchange_blocks(name, streams, collective_id):
    n_s = len(streams)

    def body(*refs):
        srcs = refs[:n_s]
        outs = refs[n_s:2 * n_s]
        send_sems, recv_sems, local_sems = refs[2 * n_s:]
        x, y, c = _my_pos()
        me = _dev_index((x, y, c))
        _handshake([(1 - x if k & 4 else x, 1 - y if k & 2 else y, 1 - c if k & 1 else c) for k in range(1, N_DEV)])

        def window(s, dev):
            return srcs[s] if streams[s][1] is None else streams[s][1](srcs[s], dev)

        mine = [pltpu.make_async_copy(window(s, me), outs[s].at[me], local_sems.at[s]) for s in range(n_s)]
        for cp in mine:
            cp.start()
        copies = []
        for k in (2, 4, 6, 3, 5, 7, 1):
            peer = (1 - x if k & 4 else x, 1 - y if k & 2 else y, 1 - c if k & 1 else c)
            for s in range(n_s):
                copies.append(pltpu.make_async_remote_copy(
                    src_ref=window(s, _dev_index(peer)), dst_ref=outs[s].at[me],
                    send_sem=send_sems.at[7 * s + k - 1], recv_sem=recv_sems.at[7 * s + k - 1],
                    device_id=peer, device_id_type=MESH))
        for cp in copies:
            cp.start()
        for cp in copies:
            cp.wait()
        for cp in mine:
            cp.wait()

    out_shapes = [jax.ShapeDtypeStruct((N_DEV,) + tuple(s[2]), s[0].dtype) for s in streams]
    scratch = [pltpu.SemaphoreType.DMA((7 * n_s,)), pltpu.SemaphoreType.DMA((7 * n_s,)),
               pltpu.SemaphoreType.DMA((n_s,))]
    return _launch_on_sequencer(name, collective_id, body, [s[0] for s in streams], out_shapes, scratch)


N_CHIPS = 4


def _sibling_exchange(name, streams, collective_id):
    n_s = len(streams)

    def body(*refs):
        srcs = refs[:n_s]
        outs = refs[n_s:2 * n_s]
        send_sems, recv_sems = refs[2 * n_s:]
        x, y, c = _my_pos()
        sibling = (x, y, 1 - c)
        _handshake([sibling])
        copies = [pltpu.make_async_remote_copy(
            src_ref=streams[s][1](srcs[s], 2 * chip + (1 - c)), dst_ref=outs[s].at[chip],
            send_sem=send_sems.at[N_CHIPS * s + chip], recv_sem=recv_sems.at[N_CHIPS * s + chip],
            device_id=sibling, device_id_type=MESH) for s in range(n_s) for chip in range(N_CHIPS)]
        for cp in copies:
            cp.start()
        for cp in copies:
            cp.wait()

    out_shapes = [jax.ShapeDtypeStruct((N_CHIPS,) + tuple(s[2]), s[0].dtype) for s in streams]
    scratch = [pltpu.SemaphoreType.DMA((N_CHIPS * n_s,)), pltpu.SemaphoreType.DMA((N_CHIPS * n_s,))]
    return _launch_on_sequencer(name, collective_id, body, [s[0] for s in streams], out_shapes, scratch)


def _pair_sum(name, array, from_sibling, my_core, after):
    _, rows, cols = from_sibling.shape
    tr = _tile(rows, 2048)

    def body(core_ref, own_ref, sib_ref, after_ref, o_ref):
        del core_ref, after_ref
        o_ref[...] = (own_ref[...].astype(F32) + sib_ref[...].astype(F32)).astype(o_ref.dtype)

    staged_spec = pl.BlockSpec((None, tr, cols), lambda k, i, core: (k, i, 0))
    return pl.pallas_call(
        body, name=name, out_shape=jax.ShapeDtypeStruct(from_sibling.shape, array.dtype),
        grid_spec=pltpu.PrefetchScalarGridSpec(
            num_scalar_prefetch=1, grid=(N_CHIPS, rows // tr),
            in_specs=[pl.BlockSpec((tr, cols), lambda k, i, core: (i, 2 * k + core[0])), staged_spec, ANY],
            out_specs=staged_spec),
        compiler_params=_params("parallel", "parallel"),
    )(my_core, array, from_sibling, after)


def _chip_exchange(name, pair_sums, collective_id):
    n_s = len(pair_sums)

    def body(*refs):
        srcs = refs[:n_s]
        outs = refs[n_s:2 * n_s]
        send_sems, recv_sems, local_sems = refs[2 * n_s:]
        x, y, c = _my_pos()
        my_chip = 2 * x + y
        chips = [(1 - x, y), (x, 1 - y), (1 - x, 1 - y)]
        _handshake([(*chip, c) for chip in chips])
        mine = [pltpu.make_async_copy(srcs[s].at[my_chip], outs[s].at[my_chip], local_sems.at[s]) for s in range(n_s)]
        copies = [pltpu.make_async_remote_copy(
            src_ref=srcs[s].at[2 * chip[0] + chip[1]], dst_ref=outs[s].at[my_chip],
            send_sem=send_sems.at[3 * s + j], recv_sem=recv_sems.at[3 * s + j],
            device_id=(*chip, c), device_id_type=MESH) for s in range(n_s) for j, chip in enumerate(chips)]
        for cp in mine + copies:
            cp.start()
        for cp in copies:
            cp.wait()
        for cp in mine:
            cp.wait()

    out_shapes = [jax.ShapeDtypeStruct(p.shape, p.dtype) for p in pair_sums]
    scratch = [pltpu.SemaphoreType.DMA((3 * n_s,)), pltpu.SemaphoreType.DMA((3 * n_s,)),
               pltpu.SemaphoreType.DMA((n_s,))]
    return _launch_on_sequencer(name, collective_id, body, list(pair_sums), out_shapes, scratch)


NN = (((1,), (0,)), ((), ()))
NT = (((1,), (1,)), ((), ()))
TN = (((0,), (0,)), ((), ()))


def _mm(name, a, b, *, dims, grid, a_spec, b_spec, out_shape, out_spec, nk=1,
        add=None, add_spec=None, add_scale=1.0, epilogue=None, extras=(), extra_specs=()):
    n_extra = len(extras)
    has_add = add is not None

    def body(*refs):
        a_ref, b_ref = refs[:2]
        pos = 2
        add_ref = None
        if has_add:
            add_ref = refs[pos]
            pos += 1
        extra_refs = refs[pos:pos + n_extra]
        o_ref = refs[pos + n_extra]
        acc_ref = refs[pos + n_extra + 1] if nk > 1 else None

        def finish(val):
            if has_add:
                val = val + add_scale * add_ref[...]
            if epilogue is not None:
                val = epilogue(val, *extra_refs)
            o_ref[...] = val.astype(o_ref.dtype)

        part = lax.dot_general(a_ref[...].astype(BF16), b_ref[...].astype(BF16), dims,
                               preferred_element_type=F32)
        if nk == 1:
            finish(part)
        else:
            k = pl.program_id(2)

            @pl.when(k == 0)
            def _():
                acc_ref[...] = part

            @pl.when(jnp.logical_and(k > 0, k < nk - 1))
            def _():
                acc_ref[...] += part

            @pl.when(k == nk - 1)
            def _():
                finish(acc_ref[...] + part)

    in_specs = [a_spec, b_spec] + ([add_spec] if has_add else []) + list(extra_specs)
    operands = [a, b] + ([add] if has_add else []) + list(extras)
    scratch = [pltpu.VMEM(out_spec.block_shape, F32)] if nk > 1 else []
    sem = ("parallel", "parallel") + (("arbitrary",) if nk > 1 else ())
    return pl.pallas_call(
        body, name=name, grid=grid, out_shape=out_shape,
        in_specs=in_specs, out_specs=out_spec, scratch_shapes=scratch,
        compiler_params=_params(*sem),
    )(*operands)


def _input_grad(dh, w_in, dr1, after):
    _, seq, dm = dh.shape
    bm, bn = _tile(seq, 1024), _tile(dm, 512)

    def body(dh_ref, w_ref, dr_ref, after_ref, o_ref):
        del after_ref
        o_ref[...] = (lax.dot_general(dh_ref[0], w_ref[:, :dm], NT, preferred_element_type=F32)
                      + lax.dot_general(dh_ref[1], w_ref[:, dm:], NT, preferred_element_type=F32)
                      + ALPHA * dr_ref[...])

    tile_spec = pl.BlockSpec((bm, bn), lambda i, j: (i, j))
    return pl.pallas_call(
        body, name="a_in_proj_dx", grid=(seq // bm, dm // bn),
        out_shape=jax.ShapeDtypeStruct((seq, dm), F32),
        in_specs=[pl.BlockSpec((2, bm, dm), lambda i, j: (0, i, 0)), pl.BlockSpec((bn, 2 * dm), lambda i, j: (j, 0)),
                  tile_spec, ANY],
        out_specs=tile_spec,
        compiler_params=_params("parallel", "parallel"),
    )(dh, w_in, dr1, after)


def _cast_bf16(name, a):
    rows, cols = a.shape
    tr = _tile(rows, 512)

    def body(a_ref, o_ref):
        o_ref[...] = a_ref[...].astype(BF16)

    return pl.pallas_call(
        body, name=name, grid=(rows // tr,),
        out_shape=jax.ShapeDtypeStruct(a.shape, BF16),
        in_specs=[pl.BlockSpec((tr, cols), lambda i: (i, 0))],
        out_specs=pl.BlockSpec((tr, cols), lambda i: (i, 0)),
        compiler_params=_params("parallel"),
    )(a)


def _rope_tables(seq):
    inv_freq = ROPE_THETA ** (-jnp.arange(0, HEAD_DIM, 2, dtype=F32) / HEAD_DIM)
    ang = jnp.arange(seq, dtype=F32)[:, None] * inv_freq[None, :]
    cos, sin = jnp.cos(ang), jnp.sin(ang)
    cos, sin = (jnp.concatenate([t, t, t, t], axis=-1) for t in (cos, sin))
    first_half = (jnp.arange(LANES) % HEAD_DIM < HEAD_DIM // 2)[None, :]
    return cos, jnp.where(first_half, -sin, 0.0), jnp.where(first_half, 0.0, sin)


def _rot(t, sin_a, sin_b):
    return pltpu.roll(t, LANES - HEAD_DIM // 2, 1) * sin_a + pltpu.roll(t, HEAD_DIM // 2, 1) * sin_b


def _rope(t, cos, sin_a, sin_b):
    return t * cos + _rot(t, sin_a, sin_b)


def _rope_transposed(dy, cos, sin_a, sin_b):
    return dy * cos - _rot(dy, sin_a, sin_b)


def _silu_parts(z):
    sig = jax.nn.sigmoid(z)
    return z * sig, sig * (1.0 + z * (1.0 - sig))


def _layer_norm_stats(r):
    mu = jnp.mean(r, axis=-1, keepdims=True)
    d = r - mu
    var = jnp.mean(d * d, axis=-1, keepdims=True)
    rstd = lax.rsqrt(var + LN_EPS)
    return d * rstd, rstd


def _layer_norm_backward(dout, xhat, rstd, gain):
    dxh = dout * gain
    m1 = jnp.mean(dxh, axis=-1, keepdims=True)
    m2 = jnp.mean(dxh * xhat, axis=-1, keepdims=True)
    return rstd * (dxh - m1 - xhat * m2)


def _col_sum(v):
    return jnp.sum(v, axis=0, keepdims=True)


ROW_PART = 128


def _row_parts(tile):
    part = min(tile, ROW_PART)
    return [slice(r, r + part) for r in range(0, tile, part)]


def _pool_forward(xb, w_in, wg, scale):
    seq, dm = xb.shape
    n_g = len(POOL_WINDOWS)
    gd = dm // n_g
    tile = _tile(seq, 1024)
    halo_blocks = tile // POOL_HALO

    def body(x_ref, xp_ref, wu_ref, wz_ref, wg_ref, sc_ref, y_ref, p_ref, mx_ref, z_ref):
        i, g = pl.program_id(0), pl.program_id(1)
        u = jnp.dot(x_ref[...], wu_ref[...], preferred_element_type=F32)
        z = jnp.dot(x_ref[...], wz_ref[...], preferred_element_type=F32)
        prev = jnp.where(i > 0, jnp.dot(xp_ref[...], wu_ref[...], preferred_element_type=F32), 0.0)
        s = jnp.concatenate([prev, u], axis=0)
        sums, sh = [], 1
        while sh < POOL_WINDOWS[-1]:
            s = s + pltpu.roll(s, sh, 0)
            sums.append(s)
            sh *= 2
        win = sums[-1]
        for k in range(n_g - 2, -1, -1):
            win = jnp.where(g == k, sums[k], win)
        row = i * tile + lax.broadcasted_iota(jnp.int32, (tile, 1), 0)
        window = jnp.left_shift(2, g).astype(F32)
        p = win[POOL_HALO:, :] * (1.0 / jnp.minimum((row + 1).astype(F32), window)) - u
        pb = p.astype(BF16)
        mx = jnp.dot(pb, wg_ref[...], preferred_element_type=F32)
        y_ref[...] = (mx * sc_ref[...] * (z * jax.nn.sigmoid(z))).astype(BF16)
        p_ref[...] = pb
        mx_ref[...] = mx
        z_ref[...] = z

    out_spec = pl.BlockSpec((tile, gd), lambda i, g: (i, g))
    return pl.pallas_call(
        body, name="pool_fwd", grid=(seq // tile, n_g),
        out_shape=(jax.ShapeDtypeStruct((seq, dm), BF16), jax.ShapeDtypeStruct((seq, dm), BF16),
                   jax.ShapeDtypeStruct((seq, dm), F32), jax.ShapeDtypeStruct((seq, dm), F32)),
        in_specs=[pl.BlockSpec((tile, dm), lambda i, g: (i, 0)),
                  pl.BlockSpec((POOL_HALO, dm), lambda i, g: (jnp.maximum(i * halo_blocks - 1, 0), 0)),
                  pl.BlockSpec((dm, gd), lambda i, g: (0, g)),
                  pl.BlockSpec((dm, gd), lambda i, g: (0, n_g + g)),
                  pl.BlockSpec((None, gd, gd), lambda i, g: (g, 0, 0)),
                  pl.BlockSpec((1, gd), lambda i, g: (0, g))],
        out_specs=(out_spec, out_spec, out_spec, out_spec),
        compiler_params=_params("parallel", "parallel"),
    )(xb, xb, w_in, w_in, wg, scale)


def _pool_mid_backward(dr, w_out, mx, z, p, wg, scale, after):
    seq, dm = mx.shape
    gd = dm // len(POOL_WINDOWS)
    tile = _tile(seq, 256)
    n_i = seq // tile

    def body(dr_ref, wout_ref, mx_ref, z_ref, p_ref, wg_ref, sc_ref, after_ref, dh_ref, dwg_ref, st_ref, dwg_acc,
             carry):
        del after_ref
        i = pl.program_id(0)

        def dy_of(g):
            return lax.dot_general(dr_ref[...], wout_ref[g * gd:(g + 1) * gd, :], NT, preferred_element_type=F32)

        dy_ahead = dy_of(0)
        ti = n_i - 1 - i

        @pl.when(i == 0)
        def _():
            dwg_acc[...] = jnp.zeros_like(dwg_acc)
            carry[...] = jnp.zeros_like(carry)
            st_ref[...] = jnp.zeros_like(st_ref)

        row = ti * tile + lax.broadcasted_iota(jnp.int32, (tile, 1), 0)
        count = (row + 1).astype(F32)
        for g, w in enumerate(POOL_WINDOWS):
            cs = slice(g * gd, (g + 1) * gd)
            z = z_ref[:, cs]
            sz, dsz = _silu_parts(z)
            dyg = dy_ahead
            if g + 1 < len(POOL_WINDOWS):
                dy_ahead = dy_of(g + 1)
            mxg = mx_ref[:, cs]
            sc = sc_ref[:, cs]
            t1 = dyg * sz
            st_ref[0:1, cs] += _col_sum(t1 * mxg)
            dh_ref[1, :, cs] = (dyg * (mxg * sc) * dsz).astype(BF16)
            dmx = (t1 * sc).astype(BF16)
            dwg_acc[g] += lax.dot_general(p_ref[:, cs], dmx, TN, preferred_element_type=F32)
            dp = lax.dot_general(dmx, wg_ref[g], NT, preferred_element_type=F32)
            e = dp * (1.0 / jnp.minimum(count, float(w)))
            s = jnp.concatenate([e, carry[:, cs]], axis=0)
            n = tile + POOL_HALO
            sh = 1
            while sh < w:
                s = s + pltpu.roll(s, n - sh, 0)
                sh *= 2
            dh_ref[0, :, cs] = (s[:tile, :] - dp).astype(BF16)
            carry[:, cs] = e[:POOL_HALO, :]

        @pl.when(i == n_i - 1)
        def _():
            dwg_ref[...] = dwg_acc[...].astype(BF16)

    row_spec = pl.BlockSpec((tile, dm), lambda i: (n_i - 1 - i, 0))
    return pl.pallas_call(
        body, name="pool_mid_bwd", grid=(n_i,),
        out_shape=(jax.ShapeDtypeStruct((2, seq, dm), BF16), jax.ShapeDtypeStruct(wg.shape, BF16),
                   jax.ShapeDtypeStruct((STAT_ROWS, dm), F32)),
        in_specs=[row_spec, pl.BlockSpec(w_out.shape, lambda i: (0, 0), pipeline_mode=pl.Buffered(1)),
                  row_spec, row_spec, row_spec,
                  pl.BlockSpec(wg.shape, lambda i: (0, 0, 0)),
                  pl.BlockSpec((1, dm), lambda i: (0, 0)), ANY],
        out_specs=(pl.BlockSpec((2, tile, dm), lambda i: (0, n_i - 1 - i, 0)),
                   pl.BlockSpec(wg.shape, lambda i: (0, 0, 0)),
                   pl.BlockSpec((STAT_ROWS, dm), lambda i: (0, 0))),
        scratch_shapes=[pltpu.VMEM(wg.shape, F32), pltpu.VMEM((POOL_HALO, dm), F32)],
        compiler_params=_params("arbitrary"),
    )(dr, w_out, mx, z, p, wg, scale, after)


def _out_proj_norm(y, w, x, gain, bias):
    seq, dm = x.shape
    tile = _tile(seq, 512)

    def body(y_ref, w_ref, x_ref, g_ref, b_ref, xhat_ref, rstd_ref, xb_ref):
        for rows in _row_parts(tile):
            o = jnp.dot(y_ref[rows, :], w_ref[...], preferred_element_type=F32)
            xhat, rstd = _layer_norm_stats(ALPHA * x_ref[rows, :] + o)
            xhat_ref[rows, :] = xhat
            rstd_ref[rows, :] = rstd
            xb_ref[rows, :] = (xhat * g_ref[...] + b_ref[...]).astype(BF16)

    row_spec = pl.BlockSpec((tile, dm), lambda i: (i, 0))
    vec_spec = pl.BlockSpec((1, dm), lambda i: (0, 0))
    return pl.pallas_call(
        body, name="out_proj_norm_a", grid=(seq // tile,),
        out_shape=(jax.ShapeDtypeStruct((seq, dm), F32), jax.ShapeDtypeStruct((seq, 1), F32),
                   jax.ShapeDtypeStruct((seq, dm), BF16)),
        in_specs=[row_spec, pl.BlockSpec(w.shape, lambda i: (0, 0), pipeline_mode=pl.Buffered(1)), row_spec, vec_spec,
                  vec_spec],
        out_specs=(row_spec, pl.BlockSpec((tile, 1), lambda i: (i, 0)), row_spec),
        compiler_params=_params("parallel"),
    )(y, w, x, gain, bias)


def _kv_proj(xb, wkv, tables):
    seq, dm = xb.shape
    kvw = wkv.shape[1] // 2
    n_kv = kvw // HEAD_DIM
    tile = _tile(seq, 1024)

    def body(x_ref, w_ref, cos_ref, sa_ref, sb_ref, kd_ref, vd_ref, kt_ref, vt_ref):
        kv = jnp.dot(x_ref[...], w_ref[...], preferred_element_type=F32)
        low = lax.broadcasted_iota(jnp.int32, (1, LANES), 1) < HEAD_DIM
        cos, sa, sb = cos_ref[...], sa_ref[...], sb_ref[...]

        def put(pair, h, nat_ref, t_ref):
            swapped = pltpu.roll(pair, HEAD_DIM, 1)
            for head, dup in ((h, jnp.where(low, pair, swapped)), (h + 1, jnp.where(low, swapped, pair))):
                nat_ref[head] = dup.astype(BF16)
                t_ref[head] = dup.T.astype(BF16)

        for j in range(kvw // LANES):
            put(_rope(kv[:, j * LANES:(j + 1) * LANES], cos, sa, sb), 2 * j, kd_ref, kt_ref)
            put(kv[:, kvw + j * LANES:kvw + (j + 1) * LANES], 2 * j, vd_ref, vt_ref)

    tab_spec = pl.BlockSpec((tile, LANES), lambda i: (i, 0))
    dup_spec = pl.BlockSpec((n_kv, tile, LANES), lambda i: (0, i, 0))
    dup_shape = jax.ShapeDtypeStruct((n_kv, seq, LANES), BF16)
    t_spec = pl.BlockSpec((n_kv, LANES, tile), lambda i: (0, 0, i))
    t_shape = jax.ShapeDtypeStruct((n_kv, LANES, seq), BF16)
    return pl.pallas_call(
        body, name="kv_proj", grid=(seq // tile,),
        out_shape=(dup_shape, dup_shape, t_shape, t_shape),
        in_specs=[pl.BlockSpec((tile, dm), lambda i: (i, 0)), pl.BlockSpec(wkv.shape, lambda i: (0, 0)),
                  tab_spec, tab_spec, tab_spec],
        out_specs=(dup_spec, dup_spec, t_spec, t_spec),
        compiler_params=_params("parallel"),
    )(xb, wkv, *tables)


ATTN_STEP_BLOCKS = 16


def _head_queries(q_ref, rows, low):
    parts = []
    for j in range(GQA_GROUP // 2):
        q2 = q_ref[rows, j * LANES:(j + 1) * LANES]
        parts += [jnp.where(low, q2, 0), jnp.where(low, 0, q2)]
    return parts


def _key_window(prev_ref, cur_ref, b, axis):
    def block(i):
        sl = slice(i * ATTN_BLOCK, (i + 1) * ATTN_BLOCK)
        return cur_ref[sl, :] if axis == 0 else cur_ref[:, sl]
    return jnp.concatenate([prev_ref[...] if b == 0 else block(b - 1), block(b)], axis=axis)


def _mask_bias(n):
    key = lax.broadcasted_iota(jnp.int32, (2 * ATTN_BLOCK, ATTN_BLOCK), 0)
    qry = lax.broadcasted_iota(jnp.int32, (2 * ATTN_BLOCK, ATTN_BLOCK), 1)
    valid = (key > qry) & (key <= qry + ATTN_BLOCK) & ((key >= ATTN_BLOCK) | (n > 0))
    return jnp.where(valid, 0.0, NEG_INF)


def _head_probs_transposed(kcat, qm, bias, sink):
    st = lax.dot_general(kcat, qm, NT, preferred_element_type=F32) + bias
    m = jnp.maximum(jnp.max(st, axis=0, keepdims=True), sink)
    e = jnp.exp(st - m)
    e_sink = jnp.exp(sink - m)
    inv = 1.0 / (jnp.sum(e, axis=0, keepdims=True) + e_sink)
    return e * inv, e_sink * inv


def _probs_transposed(n, kh, kcat, q_all, sink_ref):
    st = lax.dot_general(kcat, q_all, NT, preferred_element_type=F32)
    st = st + jnp.tile(_mask_bias(n), (1, GQA_GROUP))
    sink = jnp.concatenate([jnp.full((1, ATTN_BLOCK), sink_ref[0, kh * GQA_GROUP + h], F32)
                            for h in range(GQA_GROUP)], axis=1)
    m = jnp.maximum(jnp.max(st, axis=0, keepdims=True), sink)
    e = jnp.exp(st - m)
    e_sink = jnp.exp(sink - m)
    inv = 1.0 / (jnp.sum(e, axis=0, keepdims=True) + e_sink)
    return e * inv, e_sink * inv


def _attn_specs(n_width, qb):
    rows = qb * ATTN_BLOCK
    before = lambda n: jnp.maximum(n * qb - 1, 0)
    q_spec = pl.BlockSpec((rows, n_width), lambda kh, n: (n, kh))
    cur = pl.BlockSpec((None, rows, LANES), lambda kh, n: (kh, n, 0))
    prev = pl.BlockSpec((None, ATTN_BLOCK, LANES), lambda kh, n: (kh, before(n), 0))
    cur_t = pl.BlockSpec((None, LANES, rows), lambda kh, n: (kh, 0, n))
    prev_t = pl.BlockSpec((None, LANES, ATTN_BLOCK), lambda kh, n: (kh, 0, before(n)))
    return q_spec, cur, prev, cur_t, prev_t


def _pair_product_transposed(mat_t, rhs, j, low_rows):
    head_a = slice(2 * j * ATTN_BLOCK, (2 * j + 1) * ATTN_BLOCK)
    head_b = slice((2 * j + 1) * ATTN_BLOCK, (2 * j + 2) * ATTN_BLOCK)
    out_t = (jnp.dot(jnp.where(low_rows, mat_t, 0), rhs[:, head_a], preferred_element_type=F32)
             + jnp.dot(jnp.where(low_rows, 0, mat_t), rhs[:, head_b], preferred_element_type=F32))
    return out_t.T


def _attn_forward(qs, kd, vt, zb, sinks):
    seq, dm = qs.shape
    n_kv = kd.shape[0]
    gw = GQA_GROUP * HEAD_DIM

    qb = ATTN_STEP_BLOCKS if (seq // ATTN_BLOCK) % ATTN_STEP_BLOCKS == 0 else 1

    def body(q_ref, kp_ref, kc_ref, vtp_ref, vtc_ref, z_ref, sink_ref, att_ref, yb_ref):
        kh, n = pl.program_id(0), pl.program_id(1)
        low = lax.broadcasted_iota(jnp.int32, (1, LANES), 1) < HEAD_DIM
        low_rows = lax.broadcasted_iota(jnp.int32, (LANES, 1), 0) < HEAD_DIM
        for b in range(qb):
            rows = slice(b * ATTN_BLOCK, (b + 1) * ATTN_BLOCK)
            kcat = _key_window(kp_ref, kc_ref, b, 0)
            vt = _key_window(vtp_ref, vtc_ref, b, 1)
            bias = _mask_bias(n * qb + b)
            pt = jnp.concatenate(
                [_head_probs_transposed(kcat, qm, bias, sink_ref[0, kh * GQA_GROUP + h])[0].astype(BF16)
                 for h, qm in enumerate(_head_queries(q_ref, rows, low))], axis=1)
            for j in range(GQA_GROUP // 2):
                cs = slice(j * LANES, (j + 1) * LANES)
                o2 = _pair_product_transposed(vt, pt, j, low_rows)
                att_ref[rows, cs] = o2
                z = z_ref[rows, cs]
                yb_ref[rows, cs] = (o2 * (z * jax.nn.sigmoid(z))).astype(BF16)

    q_spec, cur, prev, cur_t, prev_t = _attn_specs(gw, qb)
    return pl.pallas_call(
        body, name="attn_fwd", grid=(n_kv, seq // (qb * ATTN_BLOCK)),
        out_shape=(jax.ShapeDtypeStruct((seq, dm), F32), jax.ShapeDtypeStruct((seq, dm), BF16)),
        in_specs=[q_spec, prev, cur, prev_t, cur_t, q_spec, pl.BlockSpec(memory_space=pltpu.SMEM)],
        out_specs=(q_spec, q_spec),
        compiler_params=_params("parallel", "parallel"),
    )(qs, kd, kd, vt, vt, zb, sinks)


def _attn_backward(qs, kd, vd, kt, zb, att, dyb, sinks, tables, after):
    seq, dm = qs.shape
    n_kv = kd.shape[0]
    gw = GQA_GROUP * HEAD_DIM
    n_blocks = seq // ATTN_BLOCK
    qb = ATTN_STEP_BLOCKS if n_blocks % ATTN_STEP_BLOCKS == 0 else 1

    def body(q_ref, kp_ref, kc_ref, vp_ref, vc_ref, ktp_ref, ktc_ref, z_ref, att_ref, dyb_ref, sink_ref,
             cos_ref, sa_ref, sb_ref, after_ref, dh_ref, dk_ref, dv_ref, ds_ref):
        del after_ref
        kh, n = pl.program_id(0), pl.program_id(1)

        @pl.when(n == 0)
        def _():
            dk_ref[...] = jnp.zeros_like(dk_ref)
            dv_ref[...] = jnp.zeros_like(dv_ref)

        @pl.when(jnp.logical_and(n == 0, kh == 0))
        def _():
            ds_ref[...] = jnp.zeros_like(ds_ref)

        low = lax.broadcasted_iota(jnp.int32, (1, LANES), 1) < HEAD_DIM
        low_rows = lax.broadcasted_iota(jnp.int32, (LANES, 1), 0) < HEAD_DIM
        head_lane = lax.broadcasted_iota(jnp.int32, (1, LANES), 1)
        dsink = jnp.zeros((1, LANES), F32)
        for b in range(qb):
            rows = slice(b * ATTN_BLOCK, (b + 1) * ATTN_BLOCK)
            kcat = _key_window(kp_ref, kc_ref, b, 0)
            vcat = _key_window(vp_ref, vc_ref, b, 0)
            kt = _key_window(ktp_ref, ktc_ref, b, 1)
            cos, sa, sb = cos_ref[rows, :], sa_ref[rows, :], sb_ref[rows, :]
            q_all = jnp.concatenate(_head_queries(q_ref, rows, low), axis=0)
            d_parts = []
            for j in range(GQA_GROUP // 2):
                cs = slice(j * LANES, (j + 1) * LANES)
                sz, dsz = _silu_parts(z_ref[rows, cs])
                dy2 = dyb_ref[rows, cs]
                dh_ref[1, rows, cs] = (dy2 * att_ref[rows, cs] * dsz).astype(BF16)
                datt = (dy2 * sz).astype(BF16)
                d_parts += [jnp.where(low, datt, 0), jnp.where(low, 0, datt)]
            d_all = jnp.concatenate(d_parts, axis=0)
            probs_t, sink_p = _probs_transposed(n * qb + b, kh, kcat, q_all, sink_ref)
            dprobs_t = lax.dot_general(vcat, d_all, NT, preferred_element_type=F32)
            row_dot = jnp.sum(probs_t * dprobs_t, axis=0, keepdims=True)
            ds_t = (probs_t * (dprobs_t - row_dot)).astype(BF16)
            dk = jnp.dot(ds_t, q_all, preferred_element_type=F32)
            dv = jnp.dot(probs_t.astype(BF16), d_all, preferred_element_type=F32)
            for j in range(GQA_GROUP // 2):
                dq2 = _pair_product_transposed(kt, ds_t, j, low_rows)
                dh_ref[0, rows, j * LANES:(j + 1) * LANES] = (
                    _rope_transposed(dq2, cos, sa, sb) * 0.125).astype(BF16)
            sink_dot = sink_p * row_dot
            for h in range(GQA_GROUP):
                part = jnp.sum(sink_dot[:, h * ATTN_BLOCK:(h + 1) * ATTN_BLOCK], axis=1, keepdims=True)
                dsink = dsink - jnp.where(head_lane == kh * GQA_GROUP + h, part, 0.0)

            def add_window(dk=dk, dv=dv, b=b):
                start = pl.multiple_of((n * qb + b - 1) * ATTN_BLOCK, ATTN_BLOCK)
                dk_ref[pl.ds(start, 2 * ATTN_BLOCK), :] += dk
                dv_ref[pl.ds(start, 2 * ATTN_BLOCK), :] += dv

            if b > 0:
                add_window()
            else:
                pl.when(n > 0)(add_window)

                @pl.when(n == 0)
                def _(dk=dk, dv=dv):
                    dk_ref[pl.ds(0, ATTN_BLOCK), :] += dk[ATTN_BLOCK:, :]
                    dv_ref[pl.ds(0, ATTN_BLOCK), :] += dv[ATTN_BLOCK:, :]
        ds_ref[0:1, :] += dsink

    q_spec, cur, prev, cur_t, prev_t = _attn_specs(gw, qb)
    tab_spec = pl.BlockSpec((qb * ATTN_BLOCK, LANES), lambda kh, n: (n, 0))
    acc_spec = pl.BlockSpec((None, seq, LANES), lambda kh, n: (kh, 0, 0))
    acc_shape = jax.ShapeDtypeStruct((n_kv, seq, LANES), F32)
    return pl.pallas_call(
        body, name="attn_bwd", grid=(n_kv, n_blocks // qb),
        out_shape=(jax.ShapeDtypeStruct((2, seq, dm), BF16), acc_shape, acc_shape,
                   jax.ShapeDtypeStruct((STAT_ROWS, LANES), F32)),
        in_specs=[q_spec, prev, cur, prev, cur, prev_t, cur_t, q_spec, q_spec, q_spec,
                  pl.BlockSpec(memory_space=pltpu.SMEM), tab_spec, tab_spec, tab_spec, ANY],
        out_specs=(pl.BlockSpec((2, qb * ATTN_BLOCK, gw), lambda kh, n: (0, n, kh)), acc_spec, acc_spec,
                   pl.BlockSpec((STAT_ROWS, LANES), lambda kh, n: (0, 0))),
        compiler_params=_params("arbitrary", "arbitrary"),
    )(qs, kd, kd, vd, vd, kt, kt, zb, att, dyb, sinks, *tables, after)


def _kv_grad_fold(dk, dv, tables):
    n_kv, seq, _ = dk.shape
    kvw = n_kv * HEAD_DIM
    tile = _tile(seq, 512)

    def body(dk_ref, dv_ref, cos_ref, sa_ref, sb_ref, o_ref):
        low = lax.broadcasted_iota(jnp.int32, (1, LANES), 1) < HEAD_DIM
        cos, sa, sb = cos_ref[...], sa_ref[...], sb_ref[...]

        def folded(ref, h):
            t = ref[h]
            return t + pltpu.roll(t, HEAD_DIM, 1)

        for j in range(n_kv // 2):
            ka = _rope_transposed(folded(dk_ref, 2 * j), cos, sa, sb)
            kb = _rope_transposed(folded(dk_ref, 2 * j + 1), cos, sa, sb)
            o_ref[:, j * LANES:(j + 1) * LANES] = jnp.where(low, ka, kb).astype(BF16)
            o_ref[:, kvw + j * LANES:kvw + (j + 1) * LANES] = jnp.where(
                low, folded(dv_ref, 2 * j), folded(dv_ref, 2 * j + 1)).astype(BF16)

    tab_spec = pl.BlockSpec((tile, LANES), lambda i: (i, 0))
    in_spec = pl.BlockSpec((n_kv, tile, LANES), lambda i: (0, i, 0))
    return pl.pallas_call(
        body, name="kv_grad_fold", grid=(seq // tile,),
        out_shape=jax.ShapeDtypeStruct((seq, 2 * kvw), BF16),
        in_specs=[in_spec, in_spec, tab_spec, tab_spec, tab_spec],
        out_specs=pl.BlockSpec((tile, 2 * kvw), lambda i: (i, 0)),
        compiler_params=_params("parallel"),
    )(dk, dv, *tables)


def _out_proj_norm_loss(yb, w, xhat1, gain0, bias0, gain1, bias1, target):
    seq, dm = xhat1.shape
    tile = _tile(seq, 512)

    def body(y_ref, w_ref, xh1_ref, g0_ref, b0_ref, g1_ref, b1_ref, t_ref, dr_ref, drb_ref, st_ref):
        i = pl.program_id(0)

        @pl.when(i == 0)
        def _():
            st_ref[...] = jnp.zeros_like(st_ref)

        parts = _row_parts(tile)
        product = lambda rows: jnp.dot(y_ref[rows, :], w_ref[...], preferred_element_type=F32)
        ahead = product(parts[0])
        for k, rows in enumerate(parts):
            ob = ahead
            if k + 1 < len(parts):
                ahead = product(parts[k + 1])
            x1 = xh1_ref[rows, :] * g0_ref[...] + b0_ref[...]
            xhat, rstd = _layer_norm_stats(ALPHA * x1 + ob)
            err = xhat * g1_ref[...] + b1_ref[...] - t_ref[rows, :]
            dout = err * (1.0 / dm)
            dr = _layer_norm_backward(dout, xhat, rstd, g1_ref[...])
            dr_ref[rows, :] = dr
            drb_ref[rows, :] = dr.astype(BF16)
            st_ref[0:1, :] += _col_sum(dout * xhat)
            st_ref[1:2, :] += _col_sum(dout)
            st_ref[2:3, :] += _col_sum(err * err)

    row_spec = pl.BlockSpec((tile, dm), lambda i: (i, 0))
    vec_spec = pl.BlockSpec((1, dm), lambda i: (0, 0))
    return pl.pallas_call(
        body, name="out_proj_norm_loss_b", grid=(seq // tile,),
        out_shape=(jax.ShapeDtypeStruct((seq, dm), F32), jax.ShapeDtypeStruct((seq, dm), BF16),
                   jax.ShapeDtypeStruct((STAT_ROWS, dm), F32)),
        in_specs=[row_spec, pl.BlockSpec(w.shape, lambda i: (0, 0), pipeline_mode=pl.Buffered(1)), row_spec, vec_spec,
                  vec_spec, vec_spec,
                  vec_spec, row_spec],
        out_specs=(row_spec, row_spec, pl.BlockSpec((STAT_ROWS, dm), lambda i: (0, 0))),
        compiler_params=_params("arbitrary"),
    )(yb, w, xhat1, gain0, bias0, gain1, bias1, target)


def _stream_grad_norm_backward(dhq, wqg, dkv, wkv, dr2, xhat1, rstd1, gain0, after):
    seq, dm = dr2.shape
    tile = _tile(seq, 256)

    def body(dh_ref, wqg_ref, dkv_ref, wkv_ref, dr2_ref, xh_ref, rstd_ref, g_ref, after_ref, dr_ref, drb_ref, st_ref):
        del after_ref

        @pl.when(pl.program_id(0) == 0)
        def _():
            st_ref[...] = jnp.zeros_like(st_ref)

        dx1 = (lax.dot_general(dh_ref[0], wqg_ref[:, :dm], NT, preferred_element_type=F32)
               + lax.dot_general(dh_ref[1], wqg_ref[:, dm:], NT, preferred_element_type=F32)
               + lax.dot_general(dkv_ref[...], wkv_ref[...], NT, preferred_element_type=F32)
               + ALPHA * dr2_ref[...])
        xhat = xh_ref[...]
        dr = _layer_norm_backward(dx1, xhat, rstd_ref[...], g_ref[...])
        dr_ref[...] = dr
        drb_ref[...] = dr.astype(BF16)
        st_ref[0:1, :] += _col_sum(dx1 * xhat)
        st_ref[1:2, :] += _col_sum(dx1)

    row_spec = pl.BlockSpec((tile, dm), lambda i: (i, 0))
    resident = pl.Buffered(1)
    return pl.pallas_call(
        body, name="stream_grad_norm_bwd", grid=(seq // tile,),
        out_shape=(jax.ShapeDtypeStruct((seq, dm), F32), jax.ShapeDtypeStruct((seq, dm), BF16),
                   jax.ShapeDtypeStruct((STAT_ROWS, dm), F32)),
        in_specs=[pl.BlockSpec((2, tile, dm), lambda i: (0, i, 0)),
                  pl.BlockSpec(wqg.shape, lambda i: (0, 0), pipeline_mode=resident),
                  pl.BlockSpec((tile, dkv.shape[1]), lambda i: (i, 0)),
                  pl.BlockSpec(wkv.shape, lambda i: (0, 0), pipeline_mode=resident),
                  row_spec, row_spec, pl.BlockSpec((tile, 1), lambda i: (i, 0)),
                  pl.BlockSpec((1, dm), lambda i: (0, 0)), ANY],
        out_specs=(row_spec, row_spec, pl.BlockSpec((STAT_ROWS, dm), lambda i: (0, 0))),
        compiler_params=_params("arbitrary"),
    )(dhq, wqg, dkv, wkv, dr2, xhat1, rstd1, gain0, after)


def _adamw_math(w, g, m, v):
    m = ADAM_B1 * m + (1.0 - ADAM_B1) * g
    v = ADAM_B2 * v + (1.0 - ADAM_B2) * (g * g)
    m_hat = m / (1.0 - ADAM_B1 ** ADAM_STEP)
    v_hat = v / (1.0 - ADAM_B2 ** ADAM_STEP)
    delta = -ADAM_LR * (m_hat / (jnp.sqrt(v_hat) + ADAM_EPS) + ADAM_WD * w)
    return delta, m, v


def _sum_devices(ref):
    total = ref[0].astype(F32)
    for d in range(1, ref.shape[0]):
        total = total + ref[d].astype(F32)
    return total


def _adamw_shard(name, parts, w, m, v, after, row_range=None, into=None):
    (parts,) = parts
    rows, cols = w.shape
    first_row, end_row = row_range or (0, rows)
    n_slots, part_rows, _ = parts.shape
    assert part_rows == end_row - first_row
    tr = _tile(part_rows, max(8, (1 << 17) // cols)) if part_rows >= 8 else part_rows
    n_steps = part_rows // tr
    first = first_row // tr
    kept = list(into or ())
    ring = 3

    def body(p_hbm, w_ref, m_ref, v_ref, *rest):
        g_out, d_out, m_out, v_out, buf, sems = rest[1 + len(kept):]
        i = pl.program_id(0)

        def fetch(step):
            slot = step % ring
            start = step * tr
            rows_of_step = pl.ds(start if isinstance(start, int) else pl.multiple_of(start, tr), tr)
            return pltpu.make_async_copy(p_hbm.at[:, rows_of_step, :], buf.at[slot], sems.at[slot])

        @pl.when(i == 0)
        def _():
            for s in range(min(ring - 1, n_steps)):
                fetch(s).start()

        @pl.when(i + ring - 1 < n_steps)
        def _():
            fetch(i + ring - 1).start()

        fetch(i).wait()
        g = _sum_devices(buf.at[i % ring])
        delta, m_new, v_new = _adamw_math(w_ref[...], g, m_ref[...], v_ref[...])
        g_out[...] = g
        d_out[...] = delta
        m_out[...] = m_new
        v_out[...] = v_new

    spec = pl.BlockSpec((tr, cols), lambda i: (first + i, 0))
    shape = jax.ShapeDtypeStruct((rows, cols), F32)
    return pl.pallas_call(
        body, name=name, grid=(n_steps,),
        out_shape=(shape, shape, shape, shape),
        in_specs=[ANY, spec, spec, spec, ANY] + [ANY] * len(kept),
        out_specs=(spec, spec, spec, spec),
        scratch_shapes=[pltpu.VMEM((ring, n_slots, tr, cols), parts.dtype), pltpu.SemaphoreType.DMA((ring,))],
        input_output_aliases={5 + k: k for k in range(len(kept))},
        compiler_params=_params("arbitrary"),
    )(parts, w, m, v, after, *kept)


def _adamw_replicated(stats_b, stats_a, sink_parts, ln_g, ln_b, sinks, m_ln_g, m_ln_b, m_sinks, v_ln_g, v_ln_b,
                      v_sinks, after):
    n_q = sinks.shape[1]
    dm = ln_g.shape[1]

    def body(sb_ref, sa_ref, sk_ref, g_ref, b_ref, s_ref, mg_ref, mb_ref, ms_ref, vg_ref, vb_ref, vs_ref, after_ref,
             *outs):
        del after_ref
        layer_sums = (_sum_devices(sa_ref), _sum_devices(sb_ref))
        outs[12][...] = jnp.sum(layer_sums[1][2:3, :], axis=1, keepdims=True) * (0.5 / dm)
        for which, (w_ref, m_ref, v_ref) in enumerate(((g_ref, mg_ref, vg_ref), (b_ref, mb_ref, vb_ref))):
            for layer in range(DEPTH):
                row = slice(layer, layer + 1)
                g = layer_sums[layer][which:which + 1, :]
                res = (g,) + _adamw_math(w_ref[row, :], g, m_ref[row, :], v_ref[row, :])
                for o_ref, val in zip(outs[4 * which:4 * which + 4], res):
                    o_ref[row, :] = val
        g = _sum_devices(sk_ref)[0:1, 0:n_q]
        res = (g,) + _adamw_math(s_ref[...], g, ms_ref[...], vs_ref[...])
        for o_ref, val in zip(outs[8:12], res):
            o_ref[...] = val

    vmem = pl.BlockSpec(memory_space=pltpu.VMEM)
    shapes = [jax.ShapeDtypeStruct(a.shape, F32) for a in (ln_g, ln_b, sinks) for _ in range(4)]
    shapes.append(jax.ShapeDtypeStruct((1, 1), F32))
    return pl.pallas_call(
        body, name="adamw_replicated", out_shape=tuple(shapes),
        in_specs=[vmem] * 12 + [ANY], out_specs=tuple([vmem] * 13),
    )(stats_b, stats_a, sink_parts, ln_g, ln_b, sinks, m_ln_g, m_ln_b, m_sinks, v_ln_g, v_ln_b, v_sinks, after)


def kernel(x, ln_g, ln_b, a_w_in, a_w_group, a_scale, a_w_out, b_w_k, b_w_v, b_w_qg, b_sinks, b_w_out, loss_target, m_ln_g, m_ln_b, m_a_w_in, m_a_w_group, m_a_scale, m_a_w_out, m_b_w_k, m_b_w_v, m_b_w_qg, m_b_sinks, m_b_w_out, v_ln_g, v_ln_b, v_a_w_in, v_a_w_group, v_a_scale, v_a_w_out, v_b_w_k, v_b_w_v, v_b_w_qg, v_b_sinks, v_b_w_out):
    _, seq, dm = x.shape
    n_groups = len(POOL_WINDOWS)
    gd = dm // n_groups
    kvw = b_w_k.shape[1]
    cb = 2 * dm // N_DEV
    rb = dm // N_DEV
    gb = gd // N_DEV

    x2 = x.reshape(seq, dm)
    target = loss_target.reshape(seq, dm)
    w_in_s = a_w_in.reshape(dm, cb)
    w_g_s = a_w_group.reshape(n_groups, gb, gd)
    w_out_s = a_w_out.reshape(rb, dm)
    w_qg_s = b_w_qg.reshape(dm, cb)
    w_outb_s = b_w_out.reshape(rb, dm)

    def cols(ref, dev):
        return ref.at[:, pl.ds(pl.multiple_of(dev * cb, LANES), cb)]

    def rows(ref, dev):
        return ref.at[pl.ds(pl.multiple_of(dev * rb, 8), rb), :]

    def group_rows(ref, dev):
        return ref.at[:, pl.ds(pl.multiple_of(dev * gb, 8), gb), :]

    def k_rows(ref, dev):
        return ref.at[pl.ds(pl.multiple_of(dev * rb, 8), rb), pl.ds(0, kvw)]

    def v_rows(ref, dev):
        return ref.at[pl.ds(pl.multiple_of(dev * rb, 8), rb), pl.ds(kvw, kvw)]

    def scale_cols(ref, dev):
        return ref.at[:, pl.ds(pl.multiple_of(dev * rb, LANES), rb)]

    bf = lambda a: a.astype(BF16)
    wide, square = jax.ShapeDtypeStruct((dm, 2 * dm), BF16), jax.ShapeDtypeStruct((dm, dm), BF16)
    w_g, scale, w_in = _gather_weights(
        "gather_a_in", 0, [(bf(w_g_s), 0, group_rows), (a_scale, 1, scale_cols), (bf(w_in_s), 2, cols)],
        [jax.ShapeDtypeStruct((n_groups, gd, gd), BF16), jax.ShapeDtypeStruct((1, dm), F32), wide])
    (w_out,) = _gather_weights("gather_a_out", 1, [(bf(w_out_s), 0, rows)], [square])
    w_kv, w_qg = _gather_weights(
        "gather_b_in", 2, [(bf(b_w_k), 0, k_rows), (bf(b_w_v), 0, v_rows), (bf(w_qg_s), 1, cols)],
        [jax.ShapeDtypeStruct((dm, 2 * kvw), BF16), wide])
    (w_outb,) = _gather_weights("gather_b_out", 3, [(bf(w_outb_s), 0, rows)], [square])

    tables = _rope_tables(seq)
    bm = _tile(seq, 2048)
    bn = _tile(dm, 1024)
    g0, g1, b0, b1 = ln_g[0:1], ln_g[1:2], ln_b[0:1], ln_b[1:2]

    xb = _cast_bf16("cast_x", x2)
    y, pooled, mixed, z_a = _pool_forward(xb, w_in, w_g, scale)
    xhat1, rstd1, x1b = _out_proj_norm(y, w_out, x2, g0, b0)

    kd, vd, kt, vt = _kv_proj(x1b, w_kv, tables)
    bmq = bm
    tab_spec = pl.BlockSpec((bmq, LANES), lambda i, j: (i, 0))

    def rope_scale(val, cos_ref, sa_ref, sb_ref):
        cos, sa, sb = cos_ref[...], sa_ref[...], sb_ref[...]
        return jnp.concatenate([_rope(val[:, j * LANES:(j + 1) * LANES], cos, sa, sb) * 0.125
                                for j in range(val.shape[1] // LANES)], axis=1)

    qs = _mm("b_q_proj", x1b, w_qg, dims=NN, grid=(seq // bmq, dm // bn),
             a_spec=pl.BlockSpec((bmq, dm), lambda i, j: (i, 0)), b_spec=pl.BlockSpec((dm, bn), lambda i, j: (0, j)),
             out_shape=jax.ShapeDtypeStruct((seq, dm), BF16), out_spec=pl.BlockSpec((bmq, bn), lambda i, j: (i, j)),
             epilogue=rope_scale, extras=tables, extra_specs=(tab_spec,) * 3)
    zb = _mm("b_gate_proj", x1b, w_qg, dims=NN, grid=(seq // bm, dm // bn),
             a_spec=pl.BlockSpec((bm, dm), lambda i, j: (i, 0)),
             b_spec=pl.BlockSpec((dm, bn), lambda i, j: (0, j + dm // bn)),
             out_shape=jax.ShapeDtypeStruct((seq, dm), F32), out_spec=pl.BlockSpec((bm, bn), lambda i, j: (i, j)))
    att, yb = _attn_forward(qs, kd, vt, zb, b_sinks)
    dr2, dr2b, stats_b = _out_proj_norm_loss(yb, w_outb, xhat1, g0, b0, g1, b1, target)

    def weight_grad(name, a, b, n_cols, b_spec=None, part=(0, 1), after=None):
        m_cols = a.shape[1] // part[1]
        tm, tn = _tile(m_cols, 1024), _tile(n_cols, 1024)
        first = part[0] * (m_cols // tm)
        return _mm(name, a, b, dims=TN, grid=(m_cols // tm, n_cols // tn),
                   a_spec=pl.BlockSpec((seq, tm), lambda i, j: (0, first + i)),
                   b_spec=b_spec(tn) if b_spec else pl.BlockSpec((seq, tn), lambda i, j: (0, j)),
                   out_shape=jax.ShapeDtypeStruct((m_cols, n_cols), BF16),
                   out_spec=pl.BlockSpec((tm, tn), lambda i, j: (i, j)),
                   extras=() if after is None else (after,), extra_specs=() if after is None else (ANY,))

    def halves_spec(tn):
        per = dm // tn
        return pl.BlockSpec((None, seq, tn), lambda i, j: (j // per, 0, j % per))

    def times_transposed(name, a, w):
        return _mm(name, a, w, dims=NT, grid=(seq // bm, dm // bn),
                   a_spec=pl.BlockSpec((bm, a.shape[1]), lambda i, j: (i, 0)),
                   b_spec=pl.BlockSpec((bn, w.shape[1]), lambda i, j: (j, 0)),
                   out_shape=jax.ShapeDtypeStruct((seq, dm), F32), out_spec=pl.BlockSpec((bm, bn), lambda i, j: (i, j)))

    def stat_row_cols(ref, dev):
        return ref.at[pl.ds(0, 1), pl.ds(pl.multiple_of(dev * rb, LANES), rb)]

    upd = {}
    last = [dr2b]
    my_core = lax.axis_index("c").astype(jnp.int32).reshape(1)

    def then(value):
        last[0] = value[0] if isinstance(value, (list, tuple)) else value
        return value

    def shard_update(key, parts, w, m, v):
        shape = w.shape
        flat = lambda a: a.reshape(-1, shape[-1])
        parts = list(parts) if isinstance(parts, (list, tuple)) else [parts]
        outs = then(_adamw_shard("adamw_" + key, [p.reshape(p.shape[0], -1, shape[-1]) for p in parts], flat(w),
                                 flat(m), flat(v), last[0]))
        upd[key] = [o.reshape(shape) for o in outs]

    def two_level_scatter(name, ids, streams):
        staged = _sibling_exchange(name + "_pair", streams, ids[0])

        def finish():
            sums = [then(_pair_sum(f"{name}_sum{s}", st[0], got, my_core, last[0]))
                    for s, (st, got) in enumerate(zip(streams, staged))]
            return _chip_exchange(name + "_chip", sums, ids[1])
        return finish

    d_w_outb = then(weight_grad("b_out_proj_dw", yb, dr2b, dm))
    (p_outb,) = _exchange_blocks("scatter_b_out", [(d_w_outb, rows, (rb, dm))], 4)
    dyb = times_transposed("b_out_proj_dx", dr2b, w_outb)
    dhq, dkd, dvd, dsink = then(_attn_backward(qs, kd, vd, kt, zb, att, dyb, b_sinks, tables, after=last[0]))
    dkv = _kv_grad_fold(dkd, dvd, tables)
    d_w_kv = weight_grad("b_kv_proj_dw", x1b, dkv, 2 * kvw)
    d_w_qg = then(weight_grad("b_qg_proj_dw", x1b, dhq, 2 * dm, halves_spec, after=d_w_kv))
    finish_b_in = two_level_scatter("scatter_b_in", (5, 11), [(d_w_qg, cols, (dm, cb))])
    p_k, p_v = _exchange_blocks("scatter_b_kv", [(d_w_kv, k_rows, (rb, kvw)), (d_w_kv, v_rows, (rb, kvw))], 14)
    dr1, dr1b, stats_a = _stream_grad_norm_backward(dhq, w_qg, dkv, w_kv, dr2, xhat1, rstd1, g0, after=last[0])
    last[0] = dr1b
    shard_update("b_w_out", p_outb, b_w_out, m_b_w_out, v_b_w_out)
    (p_qg,) = finish_b_in()
    shard_update("b_w_k", p_k, b_w_k, m_b_w_k, v_b_w_k)
    shard_update("b_w_v", p_v, b_w_v, m_b_w_v, v_b_w_v)
    all_b, all_a, all_sink = _exchange_blocks("gather_replicated_grads", [
        (stats_b, None, stats_b.shape), (stats_a, None, stats_a.shape), (dsink, None, dsink.shape)], 9)

    d_w_out = then(weight_grad("a_out_proj_dw", y, dr1b, dm, after=last[0]))
    (p_out,) = _exchange_blocks("scatter_a_out", [(d_w_out, rows, (rb, dm))], 6)
    dh, d_w_g, stats_s = then(_pool_mid_backward(dr1b, w_out, mixed, z_a, pooled, w_g, scale, after=last[0]))
    p_g, p_scale = _exchange_blocks("scatter_a_mid", [
        (d_w_g, group_rows, (n_groups, gb, gd)), (stats_s, stat_row_cols, (1, rb))], 7)
    shard_update("b_w_qg", p_qg, b_w_qg, m_b_w_qg, v_b_w_qg)
    rep = then(_adamw_replicated(all_b, all_a, all_sink, ln_g, ln_b, b_sinks, m_ln_g, m_ln_b, m_b_sinks, v_ln_g,
                                 v_ln_b, v_b_sinks, last[0]))
    upd["ln_g"], upd["ln_b"], upd["b_sinks"] = list(rep[0:4]), list(rep[4:8]), list(rep[8:12])
    finish_a_in = []
    for k in range(2):
        d_w_in = then(weight_grad(f"a_in_proj_dw_{k}", xb, dh, 2 * dm, halves_spec, part=(k, 2), after=last[0]))
        finish_a_in.append(two_level_scatter(f"scatter_a_in_{k}", (8 + 2 * k, 12 + k), [(d_w_in, cols, (dm // 2, cb))]))
    shard_update("a_w_out", p_out, a_w_out, m_a_w_out, v_a_w_out)
    shard_update("a_w_group", p_g, a_w_group, m_a_w_group, v_a_w_group)
    shard_update("a_scale", p_scale, a_scale, m_a_scale, v_a_scale)
    p_in = list(finish_a_in[0]()) + list(finish_a_in[1]())
    grad_x = then(_input_grad(dh, w_in, dr1, last[0]))
    flat_in = [a.reshape(dm, cb) for a in (a_w_in, m_a_w_in, v_a_w_in)]
    half = _adamw_shard("adamw_a_w_in_0", [p_in[0]], *flat_in, last[0], row_range=(0, dm // 2))
    full = _adamw_shard("adamw_a_w_in_1", [p_in[1]], *flat_in, last[0], row_range=(dm // 2, dm), into=half)
    upd["a_w_in"] = [o.reshape(a_w_in.shape) for o in full]

    loss = rep[12].reshape(())
    order = ["ln_g", "ln_b", "a_w_in", "a_w_group", "a_scale", "a_w_out", "b_w_k", "b_w_v", "b_w_qg", "b_sinks",
             "b_w_out"]
    return (loss, grad_x.reshape(x.shape), *[upd[n][0] for n in order], *[upd[n][1] for n in order],
            *[upd[n][2] for n in order], *[upd[n][3] for n in order])
```

```python
import functools

import jax
import jax.numpy as jnp
from jax import lax
from jax.experimental import pallas as pl
from jax.experimental.pallas import tpu as pltpu
from jax.experimental.pallas import tpu_sc as plsc

F32 = jnp.float32
BF16 = jnp.bfloat16
MESH = pl.DeviceIdType.MESH
AXES = ("x", "y", "c")
N_DEV = 8

POOL_WINDOWS = (2, 4, 8, 16)
POOL_HALO = 16
HEAD_DIM = 64
GQA_GROUP = 8
ATTN_BLOCK = 128
ROPE_THETA = 10000.0
LN_EPS = 1e-5
NEG_INF = -1e30
DEPTH = 2
ALPHA = (2 * DEPTH) ** 0.25
ADAM_LR = 0.001
ADAM_B1 = 0.9
ADAM_B2 = 0.999
ADAM_EPS = 1e-08
ADAM_WD = 0.01
ADAM_STEP = 10

LANES = 128
STAT_ROWS = 8


def _tile(n, want):
    t = min(n, want)
    while n % t:
        t //= 2
    return t


def _params(*sem):
    return pltpu.CompilerParams(dimension_semantics=sem)


ANY = pl.BlockSpec(memory_space=pl.ANY)


def _my_pos():
    return lax.axis_index("x"), lax.axis_index("y"), lax.axis_index("c")


def _dev_index(p):
    return 4 * p[0] + 2 * p[1] + p[2]


def _handshake(peers):
    barrier = pltpu.get_barrier_semaphore()
    for peer in peers:
        pl.semaphore_signal(barrier, inc=1, device_id=peer, device_id_type=MESH)
    pl.semaphore_wait(barrier, len(peers))


def _launch_on_sequencer(name, collective_id, body, operands, out_shapes, scratch):
    return pl.kernel(
        body, out_type=tuple(out_shapes), name=name,
        mesh=plsc.ScalarSubcoreMesh(axis_name="sequencer", num_cores=1), scratch_types=scratch,
        compiler_params=pltpu.CompilerParams(collective_id=collective_id),
    )(*operands)


def _gather_weights(name, collective_id, streams, out_shapes):
    n_s = len(streams)
    n_out = len(out_shapes)

    def body(*refs):
        srcs = refs[:n_s]
        outs = refs[n_s:n_s + n_out]
        send_sems, recv_sems, local_sems = refs[n_s + n_out:]
        x, y, c = _my_pos()
        me, sibling = (x, y, c), (x, y, 1 - c)
        x_nbr, y_nbr, diag = (1 - x, y), (x, 1 - y), (1 - x, 1 - y)
        _handshake([sibling, (*x_nbr, c), (*y_nbr, c)])
        south = c == 0
        relay_from = (jnp.where(south, 1 - x, x), jnp.where(south, y, 1 - y))
        relay_to = (jnp.where(south, x, 1 - x), jnp.where(south, 1 - y, y))
        early, late = jnp.where(south, 1, 2), jnp.where(south, 2, 1)

        def copy(s, k, block, to, from_shard=False):
            out_ref = outs[streams[s][1]]
            win = streams[s][2](out_ref, _dev_index(block))
            return pltpu.make_async_remote_copy(
                src_ref=srcs[s] if from_shard else win, dst_ref=win,
                send_sem=send_sems.at[7 * s + k], recv_sem=recv_sems.at[7 * s + k],
                device_id=to, device_id_type=MESH)

        mine = [pltpu.make_async_copy(srcs[s], streams[s][2](outs[streams[s][1]], _dev_index(me)), local_sems.at[s])
                for s in range(n_s)]
        for cp in mine:
            cp.start()
        sent = []
        for s in range(n_s):
            sent += [copy(s, 0, me, sibling, True), copy(s, 1, me, (*x_nbr, c), True), copy(s, 2, me, (*y_nbr, c), True)]
        for cp in sent:
            cp.start()
        for s in range(n_s):
            copy(s, early, (*relay_from, c), me).wait_recv()
            sent += [copy(s, 3, (*relay_from, c), (*relay_to, c)), copy(s, 3 + early, (*relay_from, c), sibling)]
            for cp in sent[-2:]:
                cp.start()
        for s in range(n_s):
            copy(s, late, (*relay_to, c), me).wait_recv()
            sent.append(copy(s, 3 + late, (*relay_to, c), sibling))
            sent[-1].start()
        for s in range(n_s):
            copy(s, 3, (*diag, c), me).wait_recv()
            sent.append(copy(s, 6, (*diag, c), sibling))
            sent[-1].start()
        for s in range(n_s):
            copy(s, 0, sibling, me).wait_recv()
            for k, chip in ((4, x_nbr), (5, y_nbr), (6, diag)):
                copy(s, k, (*chip, 1 - c), me).wait_recv()
        for cp in sent:
            cp.wait_send()
        for cp in mine:
            cp.wait()

    scratch = [pltpu.SemaphoreType.DMA((7 * n_s,)), pltpu.SemaphoreType.DMA((7 * n_s,)),
               pltpu.SemaphoreType.DMA((n_s,))]
    return _launch_on_sequencer(name, collective_id, body, [s[0] for s in streams], out_shapes, scratch)


def _exchange_blocks(name, streams, collective_id):
    n_s = len(streams)

    def body(*refs):
        srcs = refs[:n_s]
        outs = refs[n_s:2 * n_s]
        send_sems, recv_sems, local_sems = refs[2 * n_s:]
        x, y, c = _my_pos()
        me = _dev_index((x, y, c))
        _handshake([(1 - x if k & 4 else x, 1 - y if k & 2 else y, 1 - c if k & 1 else c) for k in range(1, N_DEV)])

        def window(s, dev):
            return srcs[s] if streams[s][1] is None else streams[s][1](srcs[s], dev)

        mine = [pltpu.make_async_copy(window(s, me), outs[s].at[me], local_sems.at[s]) for s in range(n_s)]
        for cp in mine:
            cp.start()
        copies = []
        for k in (2, 4, 6, 3, 5, 7, 1):
            peer = (1 - x if k & 4 else x, 1 - y if k & 2 else y, 1 - c if k & 1 else c)
            for s in range(n_s):
                copies.append(pltpu.make_async_remote_copy(
                    src_ref=window(s, _dev_index(peer)), dst_ref=outs[s].at[me],
                    send_sem=send_sems.at[7 * s + k - 1], recv_sem=recv_sems.at[7 * s + k - 1],
                    device_id=peer, device_id_type=MESH))
        for cp in copies:
            cp.start()
        for cp in copies:
            cp.wait()
        for cp in mine:
            cp.wait()

    out_shapes = [jax.ShapeDtypeStruct((N_DEV,) + tuple(s[2]), s[0].dtype) for s in streams]
    scratch = [pltpu.SemaphoreType.DMA((7 * n_s,)), pltpu.SemaphoreType.DMA((7 * n_s,)),
               pltpu.SemaphoreType.DMA((n_s,))]
    return _launch_on_sequencer(name, collective_id, body, [s[0] for s in streams], out_shapes, scratch)


N_CHIPS = 4


def _sibling_exchange(name, streams, collective_id):
    n_s = len(streams)

    def body(*refs):
        srcs = refs[:n_s]
        outs = refs[n_s:2 * n_s]
        send_sems, recv_sems = refs[2 * n_s:]
        x, y, c = _my_pos()
        sibling = (x, y, 1 - c)
        _handshake([sibling])
        copies = [pltpu.make_async_remote_copy(
            src_ref=streams[s][1](srcs[s], 2 * chip + (1 - c)), dst_ref=outs[s].at[chip],
            send_sem=send_sems.at[N_CHIPS * s + chip], recv_sem=recv_sems.at[N_CHIPS * s + chip],
            device_id=sibling, device_id_type=MESH) for s in range(n_s) for chip in range(N_CHIPS)]
        for cp in copies:
            cp.start()
        for cp in copies:
            cp.wait()

    out_shapes = [jax.ShapeDtypeStruct((N_CHIPS,) + tuple(s[2]), s[0].dtype) for s in streams]
    scratch = [pltpu.SemaphoreType.DMA((N_CHIPS * n_s,)), pltpu.SemaphoreType.DMA((N_CHIPS * n_s,))]
    return _launch_on_sequencer(name, collective_id, body, [s[0] for s in streams], out_shapes, scratch)


def _pair_sum(name, array, from_sibling, my_core, after):
    _, rows, cols = from_sibling.shape
    tr = _tile(rows, 2048)

    def body(core_ref, own_ref, sib_ref, after_ref, o_ref):
        del core_ref, after_ref
        o_ref[...] = (own_ref[...].astype(F32) + sib_ref[...].astype(F32)).astype(o_ref.dtype)

    staged_spec = pl.BlockSpec((None, tr, cols), lambda k, i, core: (k, i, 0))
    return pl.pallas_call(
        body, name=name, out_shape=jax.ShapeDtypeStruct(from_sibling.shape, array.dtype),
        grid_spec=pltpu.PrefetchScalarGridSpec(
            num_scalar_prefetch=1, grid=(N_CHIPS, rows // tr),
            in_specs=[pl.BlockSpec((tr, cols), lambda k, i, core: (i, 2 * k + core[0])), staged_spec, ANY],
            out_specs=staged_spec),
        compiler_params=_params("parallel", "parallel"),
    )(my_core, array, from_sibling, after)


def _chip_exchange(name, pair_sums, collective_id):
    n_s = len(pair_sums)

    def body(*refs):
        srcs = refs[:n_s]
        outs = refs[n_s:2 * n_s]
        send_sems, recv_sems, local_sems = refs[2 * n_s:]
        x, y, c = _my_pos()
        my_chip = 2 * x + y
        chips = [(1 - x, y), (x, 1 - y), (1 - x, 1 - y)]
        _handshake([(*chip, c) for chip in chips])
        mine = [pltpu.make_async_copy(srcs[s].at[my_chip], outs[s].at[my_chip], local_sems.at[s]) for s in range(n_s)]
        copies = [pltpu.make_async_remote_copy(
            src_ref=srcs[s].at[2 * chip[0] + chip[1]], dst_ref=outs[s].at[my_chip],
            send_sem=send_sems.at[3 * s + j], recv_sem=recv_sems.at[3 * s + j],
            device_id=(*chip, c), device_id_type=MESH) for s in range(n_s) for j, chip in enumerate(chips)]
        for cp in mine + copies:
            cp.start()
        for cp in copies:
            cp.wait()
        for cp in mine:
            cp.wait()

    out_shapes = [jax.ShapeDtypeStruct(p.shape, p.dtype) for p in pair_sums]
    scratch = [pltpu.SemaphoreType.DMA((3 * n_s,)), pltpu.SemaphoreType.DMA((3 * n_s,)),
               pltpu.SemaphoreType.DMA((n_s,))]
    return _launch_on_sequencer(name, collective_id, body, list(pair_sums), out_shapes, scratch)


NN = (((1,), (0,)), ((), ()))
NT = (((1,), (1,)), ((), ()))
TN = (((0,), (0,)), ((), ()))


def _mm(name, a, b, *, dims, grid, a_spec, b_spec, out_shape, out_spec, nk=1,
        add=None, add_spec=None, add_scale=1.0, epilogue=None, extras=(), extra_specs=()):
    n_extra = len(extras)
    has_add = add is not None

    def body(*refs):
        a_ref, b_ref = refs[:2]
        pos = 2
        add_ref = None
        if has_add:
            add_ref = refs[pos]
            pos += 1
        extra_refs = refs[pos:pos + n_extra]
        o_ref = refs[pos + n_extra]
        acc_ref = refs[pos + n_extra + 1] if nk > 1 else None

        def finish(val):
            if has_add:
                val = val + add_scale * add_ref[...]
            if epilogue is not None:
                val = epilogue(val, *extra_refs)
            o_ref[...] = val.astype(o_ref.dtype)

        part = lax.dot_general(a_ref[...].astype(BF16), b_ref[...].astype(BF16), dims,
                               preferred_element_type=F32)
        if nk == 1:
            finish(part)
        else:
            k = pl.program_id(2)

            @pl.when(k == 0)
            def _():
                acc_ref[...] = part

            @pl.when(jnp.logical_and(k > 0, k < nk - 1))
            def _():
                acc_ref[...] += part

            @pl.when(k == nk - 1)
            def _():
                finish(acc_ref[...] + part)

    in_specs = [a_spec, b_spec] + ([add_spec] if has_add else []) + list(extra_specs)
    operands = [a, b] + ([add] if has_add else []) + list(extras)
    scratch = [pltpu.VMEM(out_spec.block_shape, F32)] if nk > 1 else []
    sem = ("parallel", "parallel") + (("arbitrary",) if nk > 1 else ())
    return pl.pallas_call(
        body, name=name, grid=grid, out_shape=out_shape,
        in_specs=in_specs, out_specs=out_spec, scratch_shapes=scratch,
        compiler_params=_params(*sem),
    )(*operands)


def _input_grad(dh, w_in, dr1, after):
    _, seq, dm = dh.shape
    bm, bn = _tile(seq, 1024), _tile(dm, 512)

    def body(dh_ref, w_ref, dr_ref, after_ref, o_ref):
        del after_ref
        o_ref[...] = (lax.dot_general(dh_ref[0], w_ref[:, :dm], NT, preferred_element_type=F32)
                      + lax.dot_general(dh_ref[1], w_ref[:, dm:], NT, preferred_element_type=F32)
                      + ALPHA * dr_ref[...])

    tile_spec = pl.BlockSpec((bm, bn), lambda i, j: (i, j))
    return pl.pallas_call(
        body, name="a_in_proj_dx", grid=(seq // bm, dm // bn),
        out_shape=jax.ShapeDtypeStruct((seq, dm), F32),
        in_specs=[pl.BlockSpec((2, bm, dm), lambda i, j: (0, i, 0)), pl.BlockSpec((bn, 2 * dm), lambda i, j: (j, 0)),
                  tile_spec, ANY],
        out_specs=tile_spec,
        compiler_params=_params("parallel", "parallel"),
    )(dh, w_in, dr1, after)


def _cast_bf16(name, a):
    rows, cols = a.shape
    tr = _tile(rows, 512)

    def body(a_ref, o_ref):
        o_ref[...] = a_ref[...].astype(BF16)

    return pl.pallas_call(
        body, name=name, grid=(rows // tr,),
        out_shape=jax.ShapeDtypeStruct(a.shape, BF16),
        in_specs=[pl.BlockSpec((tr, cols), lambda i: (i, 0))],
        out_specs=pl.BlockSpec((tr, cols), lambda i: (i, 0)),
        compiler_params=_params("parallel"),
    )(a)


def _rope_tables(seq):
    inv_freq = ROPE_THETA ** (-jnp.arange(0, HEAD_DIM, 2, dtype=F32) / HEAD_DIM)
    ang = jnp.arange(seq, dtype=F32)[:, None] * inv_freq[None, :]
    cos, sin = jnp.cos(ang), jnp.sin(ang)
    cos, sin = (jnp.concatenate([t, t, t, t], axis=-1) for t in (cos, sin))
    first_half = (jnp.arange(LANES) % HEAD_DIM < HEAD_DIM // 2)[None, :]
    return cos, jnp.where(first_half, -sin, 0.0), jnp.where(first_half, 0.0, sin)


def _rot(t, sin_a, sin_b):
    return pltpu.roll(t, LANES - HEAD_DIM // 2, 1) * sin_a + pltpu.roll(t, HEAD_DIM // 2, 1) * sin_b


def _rope(t, cos, sin_a, sin_b):
    return t * cos + _rot(t, sin_a, sin_b)


def _rope_transposed(dy, cos, sin_a, sin_b):
    return dy * cos - _rot(dy, sin_a, sin_b)


def _silu_parts(z):
    sig = jax.nn.sigmoid(z)
    return z * sig, sig * (1.0 + z * (1.0 - sig))


def _layer_norm_stats(r):
    mu = jnp.mean(r, axis=-1, keepdims=True)
    d = r - mu
    var = jnp.mean(d * d, axis=-1, keepdims=True)
    rstd = lax.rsqrt(var + LN_EPS)
    return d * rstd, rstd


def _layer_norm_backward(dout, xhat, rstd, gain):
    dxh = dout * gain
    m1 = jnp.mean(dxh, axis=-1, keepdims=True)
    m2 = jnp.mean(dxh * xhat, axis=-1, keepdims=True)
    return rstd * (dxh - m1 - xhat * m2)


def _col_sum(v):
    return jnp.sum(v, axis=0, keepdims=True)


ROW_PART = 128


def _row_parts(tile):
    part = min(tile, ROW_PART)
    return [slice(r, r + part) for r in range(0, tile, part)]


def _pool_forward(xb, w_in, wg, scale):
    seq, dm = xb.shape
    n_g = len(POOL_WINDOWS)
    gd = dm // n_g
    tile = _tile(seq, 1024)

    def body(x_ref, wu_ref, wz_ref, wg_ref, sc_ref, y_ref, p_ref, mx_ref, z_ref, carry):
        g, i = pl.program_id(0), pl.program_id(1)

        @pl.when(i == 0)
        def _():
            carry[...] = jnp.zeros_like(carry)

        u = jnp.dot(x_ref[...], wu_ref[...], preferred_element_type=F32)
        z = jnp.dot(x_ref[...], wz_ref[...], preferred_element_type=F32)
        s = jnp.concatenate([carry[...], u], axis=0)
        carry[...] = u[tile - POOL_HALO:, :]
        sums, sh = [], 1
        while sh < POOL_WINDOWS[-1]:
            s = s + pltpu.roll(s, sh, 0)
            sums.append(s)
            sh *= 2
        win = sums[-1]
        for k in range(n_g - 2, -1, -1):
            win = jnp.where(g == k, sums[k], win)
        row = i * tile + lax.broadcasted_iota(jnp.int32, (tile, 1), 0)
        window = jnp.left_shift(2, g).astype(F32)
        p = win[POOL_HALO:, :] * (1.0 / jnp.minimum((row + 1).astype(F32), window)) - u
        pb = p.astype(BF16)
        mx = jnp.dot(pb, wg_ref[...], preferred_element_type=F32)
        y_ref[...] = (mx * sc_ref[...] * (z * jax.nn.sigmoid(z))).astype(BF16)
        p_ref[...] = pb
        mx_ref[...] = mx
        z_ref[...] = z

    out_spec = pl.BlockSpec((tile, gd), lambda g, i: (i, g))
    return pl.pallas_call(
        body, name="pool_fwd", grid=(n_g, seq // tile),
        out_shape=(jax.ShapeDtypeStruct((seq, dm), BF16), jax.ShapeDtypeStruct((seq, dm), BF16),
                   jax.ShapeDtypeStruct((seq, dm), F32), jax.ShapeDtypeStruct((seq, dm), F32)),
        in_specs=[pl.BlockSpec((tile, dm), lambda g, i: (i, 0)),
                  pl.BlockSpec((dm, gd), lambda g, i: (0, g)),
                  pl.BlockSpec((dm, gd), lambda g, i: (0, n_g + g)),
                  pl.BlockSpec((None, gd, gd), lambda g, i: (g, 0, 0)),
                  pl.BlockSpec((1, gd), lambda g, i: (0, g))],
        out_specs=(out_spec, out_spec, out_spec, out_spec),
        scratch_shapes=[pltpu.VMEM((POOL_HALO, gd), F32)],
        compiler_params=_params("arbitrary", "arbitrary"),
    )(xb, w_in, w_in, wg, scale)


def _pool_mid_backward(dr, w_out, mx, z, p, wg, scale, after):
    seq, dm = mx.shape
    gd = dm // len(POOL_WINDOWS)
    tile = _tile(seq, 256)
    n_i = seq // tile

    def body(dr_ref, wout_ref, mx_ref, z_ref, p_ref, wg_ref, sc_ref, after_ref, dh_ref, dwg_ref, st_ref, dwg_acc,
             carry):
        del after_ref
        i = pl.program_id(0)

        def dy_of(g):
            return lax.dot_general(dr_ref[...], wout_ref[g * gd:(g + 1) * gd, :], NT, preferred_element_type=F32)

        dy_ahead = dy_of(0)
        ti = n_i - 1 - i

        @pl.when(i == 0)
        def _():
            dwg_acc[...] = jnp.zeros_like(dwg_acc)
            carry[...] = jnp.zeros_like(carry)
            st_ref[...] = jnp.zeros_like(st_ref)

        row = ti * tile + lax.broadcasted_iota(jnp.int32, (tile, 1), 0)
        count = (row + 1).astype(F32)
        for g, w in enumerate(POOL_WINDOWS):
            cs = slice(g * gd, (g + 1) * gd)
            z = z_ref[:, cs]
            sz, dsz = _silu_parts(z)
            dyg = dy_ahead
            if g + 1 < len(POOL_WINDOWS):
                dy_ahead = dy_of(g + 1)
            mxg = mx_ref[:, cs]
            sc = sc_ref[:, cs]
            t1 = dyg * sz
            st_ref[0:1, cs] += _col_sum(t1 * mxg)
            dh_ref[1, :, cs] = (dyg * (mxg * sc) * dsz).astype(BF16)
            dmx = (t1 * sc).astype(BF16)
            dwg_acc[g] += lax.dot_general(p_ref[:, cs], dmx, TN, preferred_element_type=F32)
            dp = lax.dot_general(dmx, wg_ref[g], NT, preferred_element_type=F32)
            e = dp * (1.0 / jnp.minimum(count, float(w)))
            s = jnp.concatenate([e, carry[:, cs]], axis=0)
            n = tile + POOL_HALO
            sh = 1
            while sh < w:
                s = s + pltpu.roll(s, n - sh, 0)
                sh *= 2
            dh_ref[0, :, cs] = (s[:tile, :] - dp).astype(BF16)
            carry[:, cs] = e[:POOL_HALO, :]

        @pl.when(i == n_i - 1)
        def _():
            dwg_ref[...] = dwg_acc[...].astype(BF16)

    row_spec = pl.BlockSpec((tile, dm), lambda i: (n_i - 1 - i, 0))
    return pl.pallas_call(
        body, name="pool_mid_bwd", grid=(n_i,),
        out_shape=(jax.ShapeDtypeStruct((2, seq, dm), BF16), jax.ShapeDtypeStruct(wg.shape, BF16),
                   jax.ShapeDtypeStruct((STAT_ROWS, dm), F32)),
        in_specs=[row_spec, pl.BlockSpec(w_out.shape, lambda i: (0, 0), pipeline_mode=pl.Buffered(1)),
                  row_spec, row_spec, row_spec,
                  pl.BlockSpec(wg.shape, lambda i: (0, 0, 0)),
                  pl.BlockSpec((1, dm), lambda i: (0, 0)), ANY],
        out_specs=(pl.BlockSpec((2, tile, dm), lambda i: (0, n_i - 1 - i, 0)),
                   pl.BlockSpec(wg.shape, lambda i: (0, 0, 0)),
                   pl.BlockSpec((STAT_ROWS, dm), lambda i: (0, 0))),
        scratch_shapes=[pltpu.VMEM(wg.shape, F32), pltpu.VMEM((POOL_HALO, dm), F32)],
        compiler_params=_params("arbitrary"),
    )(dr, w_out, mx, z, p, wg, scale, after)


def _out_proj_norm(y, w, x, gain, bias):
    seq, dm = x.shape
    tile = _tile(seq, 512)

    def body(y_ref, w_ref, x_ref, g_ref, b_ref, xhat_ref, rstd_ref, xb_ref):
        for rows in _row_parts(tile):
            o = jnp.dot(y_ref[rows, :], w_ref[...], preferred_element_type=F32)
            xhat, rstd = _layer_norm_stats(ALPHA * x_ref[rows, :] + o)
            xhat_ref[rows, :] = xhat
            rstd_ref[rows, :] = rstd
            xb_ref[rows, :] = (xhat * g_ref[...] + b_ref[...]).astype(BF16)

    row_spec = pl.BlockSpec((tile, dm), lambda i: (i, 0))
    vec_spec = pl.BlockSpec((1, dm), lambda i: (0, 0))
    return pl.pallas_call(
        body, name="out_proj_norm_a", grid=(seq // tile,),
        out_shape=(jax.ShapeDtypeStruct((seq, dm), F32), jax.ShapeDtypeStruct((seq, 1), F32),
                   jax.ShapeDtypeStruct((seq, dm), BF16)),
        in_specs=[row_spec, pl.BlockSpec(w.shape, lambda i: (0, 0), pipeline_mode=pl.Buffered(1)), row_spec, vec_spec,
                  vec_spec],
        out_specs=(row_spec, pl.BlockSpec((tile, 1), lambda i: (i, 0)), row_spec),
        compiler_params=_params("parallel"),
    )(y, w, x, gain, bias)


def _kv_proj(xb, wkv, tables):
    seq, dm = xb.shape
    kvw = wkv.shape[1] // 2
    n_kv = kvw // HEAD_DIM
    tile = _tile(seq, 1024)

    def body(x_ref, w_ref, cos_ref, sa_ref, sb_ref, kd_ref, vd_ref, kt_ref, vt_ref):
        kv = jnp.dot(x_ref[...], w_ref[...], preferred_element_type=F32)
        low = lax.broadcasted_iota(jnp.int32, (1, LANES), 1) < HEAD_DIM
        cos, sa, sb = cos_ref[...], sa_ref[...], sb_ref[...]

        def put(pair, h, nat_ref, t_ref):
            swapped = pltpu.roll(pair, HEAD_DIM, 1)
            for head, dup in ((h, jnp.where(low, pair, swapped)), (h + 1, jnp.where(low, swapped, pair))):
                nat_ref[head] = dup.astype(BF16)
                t_ref[head] = dup.T.astype(BF16)

        for j in range(kvw // LANES):
            put(_rope(kv[:, j * LANES:(j + 1) * LANES], cos, sa, sb), 2 * j, kd_ref, kt_ref)
            put(kv[:, kvw + j * LANES:kvw + (j + 1) * LANES], 2 * j, vd_ref, vt_ref)

    tab_spec = pl.BlockSpec((tile, LANES), lambda i: (i, 0))
    dup_spec = pl.BlockSpec((n_kv, tile, LANES), lambda i: (0, i, 0))
    dup_shape = jax.ShapeDtypeStruct((n_kv, seq, LANES), BF16)
    t_spec = pl.BlockSpec((n_kv, LANES, tile), lambda i: (0, 0, i))
    t_shape = jax.ShapeDtypeStruct((n_kv, LANES, seq), BF16)
    return pl.pallas_call(
        body, name="kv_proj", grid=(seq // tile,),
        out_shape=(dup_shape, dup_shape, t_shape, t_shape),
        in_specs=[pl.BlockSpec((tile, dm), lambda i: (i, 0)), pl.BlockSpec(wkv.shape, lambda i: (0, 0)),
                  tab_spec, tab_spec, tab_spec],
        out_specs=(dup_spec, dup_spec, t_spec, t_spec),
        compiler_params=_params("parallel"),
    )(xb, wkv, *tables)


ATTN_STEP_BLOCKS = 16


def _head_queries(q_ref, rows, low):
    parts = []
    for j in range(GQA_GROUP // 2):
        q2 = q_ref[rows, j * LANES:(j + 1) * LANES]
        parts += [jnp.where(low, q2, 0), jnp.where(low, 0, q2)]
    return parts


def _key_window(prev_ref, cur_ref, b, axis):
    def block(i):
        sl = slice(i * ATTN_BLOCK, (i + 1) * ATTN_BLOCK)
        return cur_ref[sl, :] if axis == 0 else cur_ref[:, sl]
    return jnp.concatenate([prev_ref[...] if b == 0 else block(b - 1), block(b)], axis=axis)


def _mask_bias(n):
    key = lax.broadcasted_iota(jnp.int32, (2 * ATTN_BLOCK, ATTN_BLOCK), 0)
    qry = lax.broadcasted_iota(jnp.int32, (2 * ATTN_BLOCK, ATTN_BLOCK), 1)
    valid = (key > qry) & (key <= qry + ATTN_BLOCK) & ((key >= ATTN_BLOCK) | (n > 0))
    return jnp.where(valid, 0.0, NEG_INF)


def _head_probs_transposed(kcat, qm, bias, sink):
    st = lax.dot_general(kcat, qm, NT, preferred_element_type=F32) + bias
    m = jnp.maximum(jnp.max(st, axis=0, keepdims=True), sink)
    e = jnp.exp(st - m)
    e_sink = jnp.exp(sink - m)
    inv = 1.0 / (jnp.sum(e, axis=0, keepdims=True) + e_sink)
    return e * inv, e_sink * inv


def _probs_transposed(n, kh, kcat, q_all, sink_ref):
    st = lax.dot_general(kcat, q_all, NT, preferred_element_type=F32)
    st = st + jnp.tile(_mask_bias(n), (1, GQA_GROUP))
    sink = jnp.concatenate([jnp.full((1, ATTN_BLOCK), sink_ref[0, kh * GQA_GROUP + h], F32)
                            for h in range(GQA_GROUP)], axis=1)
    m = jnp.maximum(jnp.max(st, axis=0, keepdims=True), sink)
    e = jnp.exp(st - m)
    e_sink = jnp.exp(sink - m)
    inv = 1.0 / (jnp.sum(e, axis=0, keepdims=True) + e_sink)
    return e * inv, e_sink * inv


def _attn_specs(n_width, qb):
    rows = qb * ATTN_BLOCK
    before = lambda n: jnp.maximum(n * qb - 1, 0)
    q_spec = pl.BlockSpec((rows, n_width), lambda kh, n: (n, kh))
    cur = pl.BlockSpec((None, rows, LANES), lambda kh, n: (kh, n, 0))
    prev = pl.BlockSpec((None, ATTN_BLOCK, LANES), lambda kh, n: (kh, before(n), 0))
    cur_t = pl.BlockSpec((None, LANES, rows), lambda kh, n: (kh, 0, n))
    prev_t = pl.BlockSpec((None, LANES, ATTN_BLOCK), lambda kh, n: (kh, 0, before(n)))
    return q_spec, cur, prev, cur_t, prev_t


def _pair_product_transposed(mat_t, rhs, j, low_rows):
    head_a = slice(2 * j * ATTN_BLOCK, (2 * j + 1) * ATTN_BLOCK)
    head_b = slice((2 * j + 1) * ATTN_BLOCK, (2 * j + 2) * ATTN_BLOCK)
    out_t = (jnp.dot(jnp.where(low_rows, mat_t, 0), rhs[:, head_a], preferred_element_type=F32)
             + jnp.dot(jnp.where(low_rows, 0, mat_t), rhs[:, head_b], preferred_element_type=F32))
    return out_t.T


def _attn_forward(qs, kd, vt, zb, sinks):
    seq, dm = qs.shape
    n_kv = kd.shape[0]
    gw = GQA_GROUP * HEAD_DIM

    qb = ATTN_STEP_BLOCKS if (seq // ATTN_BLOCK) % ATTN_STEP_BLOCKS == 0 else 1

    def body(q_ref, kp_ref, kc_ref, vtp_ref, vtc_ref, z_ref, sink_ref, att_ref, yb_ref):
        kh, n = pl.program_id(0), pl.program_id(1)
        low = lax.broadcasted_iota(jnp.int32, (1, LANES), 1) < HEAD_DIM
        low_rows = lax.broadcasted_iota(jnp.int32, (LANES, 1), 0) < HEAD_DIM
        for b in range(qb):
            rows = slice(b * ATTN_BLOCK, (b + 1) * ATTN_BLOCK)
            kcat = _key_window(kp_ref, kc_ref, b, 0)
            vt = _key_window(vtp_ref, vtc_ref, b, 1)
            bias = _mask_bias(n * qb + b)
            pt = jnp.concatenate(
                [_head_probs_transposed(kcat, qm, bias, sink_ref[0, kh * GQA_GROUP + h])[0].astype(BF16)
                 for h, qm in enumerate(_head_queries(q_ref, rows, low))], axis=1)
            for j in range(GQA_GROUP // 2):
                cs = slice(j * LANES, (j + 1) * LANES)
                o2 = _pair_product_transposed(vt, pt, j, low_rows)
                att_ref[rows, cs] = o2
                z = z_ref[rows, cs]
                yb_ref[rows, cs] = (o2 * (z * jax.nn.sigmoid(z))).astype(BF16)

    q_spec, cur, prev, cur_t, prev_t = _attn_specs(gw, qb)
    return pl.pallas_call(
        body, name="attn_fwd", grid=(n_kv, seq // (qb * ATTN_BLOCK)),
        out_shape=(jax.ShapeDtypeStruct((seq, dm), F32), jax.ShapeDtypeStruct((seq, dm), BF16)),
        in_specs=[q_spec, prev, cur, prev_t, cur_t, q_spec, pl.BlockSpec(memory_space=pltpu.SMEM)],
        out_specs=(q_spec, q_spec),
        compiler_params=_params("parallel", "parallel"),
    )(qs, kd, kd, vt, vt, zb, sinks)


def _attn_backward(qs, kd, vd, kt, zb, att, dyb, sinks, tables, after):
    seq, dm = qs.shape
    n_kv = kd.shape[0]
    gw = GQA_GROUP * HEAD_DIM
    n_blocks = seq // ATTN_BLOCK
    qb = ATTN_STEP_BLOCKS if n_blocks % ATTN_STEP_BLOCKS == 0 else 1

    def body(q_ref, kp_ref, kc_ref, vp_ref, vc_ref, ktp_ref, ktc_ref, z_ref, att_ref, dyb_ref, sink_ref,
             cos_ref, sa_ref, sb_ref, after_ref, dh_ref, dk_ref, dv_ref, ds_ref):
        del after_ref
        kh, n = pl.program_id(0), pl.program_id(1)

        @pl.when(n == 0)
        def _():
            dk_ref[...] = jnp.zeros_like(dk_ref)
            dv_ref[...] = jnp.zeros_like(dv_ref)

        @pl.when(jnp.logical_and(n == 0, kh == 0))
        def _():
            ds_ref[...] = jnp.zeros_like(ds_ref)

        low = lax.broadcasted_iota(jnp.int32, (1, LANES), 1) < HEAD_DIM
        low_rows = lax.broadcasted_iota(jnp.int32, (LANES, 1), 0) < HEAD_DIM
        head_lane = lax.broadcasted_iota(jnp.int32, (1, LANES), 1)
        dsink = jnp.zeros((1, LANES), F32)
        for b in range(qb):
            rows = slice(b * ATTN_BLOCK, (b + 1) * ATTN_BLOCK)
            kcat = _key_window(kp_ref, kc_ref, b, 0)
            vcat = _key_window(vp_ref, vc_ref, b, 0)
            kt = _key_window(ktp_ref, ktc_ref, b, 1)
            cos, sa, sb = cos_ref[rows, :], sa_ref[rows, :], sb_ref[rows, :]
            q_all = jnp.concatenate(_head_queries(q_ref, rows, low), axis=0)
            d_parts = []
            for j in range(GQA_GROUP // 2):
                cs = slice(j * LANES, (j + 1) * LANES)
                sz, dsz = _silu_parts(z_ref[rows, cs])
                dy2 = dyb_ref[rows, cs]
                dh_ref[1, rows, cs] = (dy2 * att_ref[rows, cs] * dsz).astype(BF16)
                datt = (dy2 * sz).astype(BF16)
                d_parts += [jnp.where(low, datt, 0), jnp.where(low, 0, datt)]
            d_all = jnp.concatenate(d_parts, axis=0)
            probs_t, sink_p = _probs_transposed(n * qb + b, kh, kcat, q_all, sink_ref)
            dprobs_t = lax.dot_general(vcat, d_all, NT, preferred_element_type=F32)
            row_dot = jnp.sum(probs_t * dprobs_t, axis=0, keepdims=True)
            ds_t = (probs_t * (dprobs_t - row_dot)).astype(BF16)
            dk = jnp.dot(ds_t, q_all, preferred_element_type=F32)
            dv = jnp.dot(probs_t.astype(BF16), d_all, preferred_element_type=F32)
            for j in range(GQA_GROUP // 2):
                dq2 = _pair_product_transposed(kt, ds_t, j, low_rows)
                dh_ref[0, rows, j * LANES:(j + 1) * LANES] = (
                    _rope_transposed(dq2, cos, sa, sb) * 0.125).astype(BF16)
            sink_dot = sink_p * row_dot
            for h in range(GQA_GROUP):
                part = jnp.sum(sink_dot[:, h * ATTN_BLOCK:(h + 1) * ATTN_BLOCK], axis=1, keepdims=True)
                dsink = dsink - jnp.where(head_lane == kh * GQA_GROUP + h, part, 0.0)

            def add_window(dk=dk, dv=dv, b=b):
                start = pl.multiple_of((n * qb + b - 1) * ATTN_BLOCK, ATTN_BLOCK)
                dk_ref[pl.ds(start, 2 * ATTN_BLOCK), :] += dk
                dv_ref[pl.ds(start, 2 * ATTN_BLOCK), :] += dv

            if b > 0:
                add_window()
            else:
                pl.when(n > 0)(add_window)

                @pl.when(n == 0)
                def _(dk=dk, dv=dv):
                    dk_ref[pl.ds(0, ATTN_BLOCK), :] += dk[ATTN_BLOCK:, :]
                    dv_ref[pl.ds(0, ATTN_BLOCK), :] += dv[ATTN_BLOCK:, :]
        ds_ref[0:1, :] += dsink

    q_spec, cur, prev, cur_t, prev_t = _attn_specs(gw, qb)
    tab_spec = pl.BlockSpec((qb * ATTN_BLOCK, LANES), lambda kh, n: (n, 0))
    acc_spec = pl.BlockSpec((None, seq, LANES), lambda kh, n: (kh, 0, 0))
    acc_shape = jax.ShapeDtypeStruct((n_kv, seq, LANES), F32)
    return pl.pallas_call(
        body, name="attn_bwd", grid=(n_kv, n_blocks // qb),
        out_shape=(jax.ShapeDtypeStruct((2, seq, dm), BF16), acc_shape, acc_shape,
                   jax.ShapeDtypeStruct((STAT_ROWS, LANES), F32)),
        in_specs=[q_spec, prev, cur, prev, cur, prev_t, cur_t, q_spec, q_spec, q_spec,
                  pl.BlockSpec(memory_space=pltpu.SMEM), tab_spec, tab_spec, tab_spec, ANY],
        out_specs=(pl.BlockSpec((2, qb * ATTN_BLOCK, gw), lambda kh, n: (0, n, kh)), acc_spec, acc_spec,
                   pl.BlockSpec((STAT_ROWS, LANES), lambda kh, n: (0, 0))),
        compiler_params=_params("arbitrary", "arbitrary"),
    )(qs, kd, kd, vd, vd, kt, kt, zb, att, dyb, sinks, *tables, after)


def _kv_grad_fold(dk, dv, tables):
    n_kv, seq, _ = dk.shape
    kvw = n_kv * HEAD_DIM
    tile = _tile(seq, 512)

    def body(dk_ref, dv_ref, cos_ref, sa_ref, sb_ref, o_ref):
        low = lax.broadcasted_iota(jnp.int32, (1, LANES), 1) < HEAD_DIM
        cos, sa, sb = cos_ref[...], sa_ref[...], sb_ref[...]

        def folded(ref, h):
            t = ref[h]
            return t + pltpu.roll(t, HEAD_DIM, 1)

        for j in range(n_kv // 2):
            ka = _rope_transposed(folded(dk_ref, 2 * j), cos, sa, sb)
            kb = _rope_transposed(folded(dk_ref, 2 * j + 1), cos, sa, sb)
            o_ref[:, j * LANES:(j + 1) * LANES] = jnp.where(low, ka, kb).astype(BF16)
            o_ref[:, kvw + j * LANES:kvw + (j + 1) * LANES] = jnp.where(
                low, folded(dv_ref, 2 * j), folded(dv_ref, 2 * j + 1)).astype(BF16)

    tab_spec = pl.BlockSpec((tile, LANES), lambda i: (i, 0))
    in_spec = pl.BlockSpec((n_kv, tile, LANES), lambda i: (0, i, 0))
    return pl.pallas_call(
        body, name="kv_grad_fold", grid=(seq // tile,),
        out_shape=jax.ShapeDtypeStruct((seq, 2 * kvw), BF16),
        in_specs=[in_spec, in_spec, tab_spec, tab_spec, tab_spec],
        out_specs=pl.BlockSpec((tile, 2 * kvw), lambda i: (i, 0)),
        compiler_params=_params("parallel"),
    )(dk, dv, *tables)


def _out_proj_norm_loss(yb, w, xhat1, gain0, bias0, gain1, bias1, target):
    seq, dm = xhat1.shape
    tile = _tile(seq, 512)

    def body(y_ref, w_ref, xh1_ref, g0_ref, b0_ref, g1_ref, b1_ref, t_ref, dr_ref, drb_ref, st_ref):
        i = pl.program_id(0)

        @pl.when(i == 0)
        def _():
            st_ref[...] = jnp.zeros_like(st_ref)

        parts = _row_parts(tile)
        product = lambda rows: jnp.dot(y_ref[rows, :], w_ref[...], preferred_element_type=F32)
        ahead = product(parts[0])
        for k, rows in enumerate(parts):
            ob = ahead
            if k + 1 < len(parts):
                ahead = product(parts[k + 1])
            x1 = xh1_ref[rows, :] * g0_ref[...] + b0_ref[...]
            xhat, rstd = _layer_norm_stats(ALPHA * x1 + ob)
            err = xhat * g1_ref[...] + b1_ref[...] - t_ref[rows, :]
            dout = err * (1.0 / dm)
            dr = _layer_norm_backward(dout, xhat, rstd, g1_ref[...])
            dr_ref[rows, :] = dr
            drb_ref[rows, :] = dr.astype(BF16)
            st_ref[0:1, :] += _col_sum(dout * xhat)
            st_ref[1:2, :] += _col_sum(dout)
            st_ref[2:3, :] += _col_sum(err * err)

    row_spec = pl.BlockSpec((tile, dm), lambda i: (i, 0))
    vec_spec = pl.BlockSpec((1, dm), lambda i: (0, 0))
    return pl.pallas_call(
        body, name="out_proj_norm_loss_b", grid=(seq // tile,),
        out_shape=(jax.ShapeDtypeStruct((seq, dm), F32), jax.ShapeDtypeStruct((seq, dm), BF16),
                   jax.ShapeDtypeStruct((STAT_ROWS, dm), F32)),
        in_specs=[row_spec, pl.BlockSpec(w.shape, lambda i: (0, 0), pipeline_mode=pl.Buffered(1)), row_spec, vec_spec,
                  vec_spec, vec_spec,
                  vec_spec, row_spec],
        out_specs=(row_spec, row_spec, pl.BlockSpec((STAT_ROWS, dm), lambda i: (0, 0))),
        compiler_params=_params("arbitrary"),
    )(yb, w, xhat1, gain0, bias0, gain1, bias1, target)


def _stream_grad_norm_backward(dhq, wqg, dkv, wkv, dr2, xhat1, rstd1, gain0, after):
    seq, dm = dr2.shape
    tile = _tile(seq, 256)

    def body(dh_ref, wqg_ref, dkv_ref, wkv_ref, dr2_ref, xh_ref, rstd_ref, g_ref, after_ref, dr_ref, drb_ref, st_ref):
        del after_ref

        @pl.when(pl.program_id(0) == 0)
        def _():
            st_ref[...] = jnp.zeros_like(st_ref)

        dx1 = (lax.dot_general(dh_ref[0], wqg_ref[:, :dm], NT, preferred_element_type=F32)
               + lax.dot_general(dh_ref[1], wqg_ref[:, dm:], NT, preferred_element_type=F32)
               + lax.dot_general(dkv_ref[...], wkv_ref[...], NT, preferred_element_type=F32)
               + ALPHA * dr2_ref[...])
        xhat = xh_ref[...]
        dr = _layer_norm_backward(dx1, xhat, rstd_ref[...], g_ref[...])
        dr_ref[...] = dr
        drb_ref[...] = dr.astype(BF16)
        st_ref[0:1, :] += _col_sum(dx1 * xhat)
        st_ref[1:2, :] += _col_sum(dx1)

    row_spec = pl.BlockSpec((tile, dm), lambda i: (i, 0))
    resident = pl.Buffered(1)
    return pl.pallas_call(
        body, name="stream_grad_norm_bwd", grid=(seq // tile,),
        out_shape=(jax.ShapeDtypeStruct((seq, dm), F32), jax.ShapeDtypeStruct((seq, dm), BF16),
                   jax.ShapeDtypeStruct((STAT_ROWS, dm), F32)),
        in_specs=[pl.BlockSpec((2, tile, dm), lambda i: (0, i, 0)),
                  pl.BlockSpec(wqg.shape, lambda i: (0, 0), pipeline_mode=resident),
                  pl.BlockSpec((tile, dkv.shape[1]), lambda i: (i, 0)),
                  pl.BlockSpec(wkv.shape, lambda i: (0, 0), pipeline_mode=resident),
                  row_spec, row_spec, pl.BlockSpec((tile, 1), lambda i: (i, 0)),
                  pl.BlockSpec((1, dm), lambda i: (0, 0)), ANY],
        out_specs=(row_spec, row_spec, pl.BlockSpec((STAT_ROWS, dm), lambda i: (0, 0))),
        compiler_params=_params("arbitrary"),
    )(dhq, wqg, dkv, wkv, dr2, xhat1, rstd1, gain0, after)


def _adamw_math(w, g, m, v):
    m = ADAM_B1 * m + (1.0 - ADAM_B1) * g
    v = ADAM_B2 * v + (1.0 - ADAM_B2) * (g * g)
    m_hat = m / (1.0 - ADAM_B1 ** ADAM_STEP)
    v_hat = v / (1.0 - ADAM_B2 ** ADAM_STEP)
    delta = -ADAM_LR * (m_hat / (jnp.sqrt(v_hat) + ADAM_EPS) + ADAM_WD * w)
    return delta, m, v


def _sum_devices(ref):
    total = ref[0].astype(F32)
    for d in range(1, ref.shape[0]):
        total = total + ref[d].astype(F32)
    return total


def _adamw_shard(name, parts, w, m, v, after, row_range=None, into=None):
    rows, cols = w.shape
    first_row, end_row = row_range or (0, rows)
    n_parts = len(parts)
    part_rows = (end_row - first_row) // n_parts
    tr = _tile(part_rows, max(8, (1 << 18) // cols)) if part_rows >= 8 else part_rows
    per_part = part_rows // tr
    first = first_row // tr
    kept = list(into or ())

    def body(*refs):
        p_refs = refs[:n_parts]
        w_ref, m_ref, v_ref = refs[n_parts:n_parts + 3]
        g_out, d_out, m_out, v_out = refs[n_parts + 4 + len(kept):]
        g = _sum_devices(p_refs[0])
        for k in range(1, n_parts):
            g = jnp.where(pl.program_id(0) >= k * per_part, _sum_devices(p_refs[k]), g)
        delta, m_new, v_new = _adamw_math(w_ref[...], g, m_ref[...], v_ref[...])
        g_out[...] = g
        d_out[...] = delta
        m_out[...] = m_new
        v_out[...] = v_new

    def part_spec(k):
        return pl.BlockSpec((parts[k].shape[0], tr, cols),
                            lambda i: (0, jnp.clip(i - k * per_part, 0, per_part - 1), 0))

    spec = pl.BlockSpec((tr, cols), lambda i: (first + i, 0))
    shape = jax.ShapeDtypeStruct((rows, cols), F32)
    return pl.pallas_call(
        body, name=name, grid=((end_row - first_row) // tr,),
        out_shape=(shape, shape, shape, shape),
        in_specs=[part_spec(k) for k in range(n_parts)] + [spec, spec, spec, ANY] + [ANY] * len(kept),
        out_specs=(spec, spec, spec, spec),
        input_output_aliases={n_parts + 4 + k: k for k in range(len(kept))},
        compiler_params=_params("arbitrary"),
    )(*parts, w, m, v, after, *kept)


def _adamw_replicated(stats_b, stats_a, sink_parts, ln_g, ln_b, sinks, m_ln_g, m_ln_b, m_sinks, v_ln_g, v_ln_b,
                      v_sinks, after):
    n_q = sinks.shape[1]
    dm = ln_g.shape[1]

    def body(sb_ref, sa_ref, sk_ref, g_ref, b_ref, s_ref, mg_ref, mb_ref, ms_ref, vg_ref, vb_ref, vs_ref, after_ref,
             *outs):
        del after_ref
        layer_sums = (_sum_devices(sa_ref), _sum_devices(sb_ref))
        outs[12][...] = jnp.sum(layer_sums[1][2:3, :], axis=1, keepdims=True) * (0.5 / dm)
        for which, (w_ref, m_ref, v_ref) in enumerate(((g_ref, mg_ref, vg_ref), (b_ref, mb_ref, vb_ref))):
            for layer in range(DEPTH):
                row = slice(layer, layer + 1)
                g = layer_sums[layer][which:which + 1, :]
                res = (g,) + _adamw_math(w_ref[row, :], g, m_ref[row, :], v_ref[row, :])
                for o_ref, val in zip(outs[4 * which:4 * which + 4], res):
                    o_ref[row, :] = val
        g = _sum_devices(sk_ref)[0:1, 0:n_q]
        res = (g,) + _adamw_math(s_ref[...], g, ms_ref[...], vs_ref[...])
        for o_ref, val in zip(outs[8:12], res):
            o_ref[...] = val

    vmem = pl.BlockSpec(memory_space=pltpu.VMEM)
    shapes = [jax.ShapeDtypeStruct(a.shape, F32) for a in (ln_g, ln_b, sinks) for _ in range(4)]
    shapes.append(jax.ShapeDtypeStruct((1, 1), F32))
    return pl.pallas_call(
        body, name="adamw_replicated", out_shape=tuple(shapes),
        in_specs=[vmem] * 12 + [ANY], out_specs=tuple([vmem] * 13),
    )(stats_b, stats_a, sink_parts, ln_g, ln_b, sinks, m_ln_g, m_ln_b, m_sinks, v_ln_g, v_ln_b, v_sinks, after)


def kernel(x, ln_g, ln_b, a_w_in, a_w_group, a_scale, a_w_out, b_w_k, b_w_v, b_w_qg, b_sinks, b_w_out, loss_target, m_ln_g, m_ln_b, m_a_w_in, m_a_w_group, m_a_scale, m_a_w_out, m_b_w_k, m_b_w_v, m_b_w_qg, m_b_sinks, m_b_w_out, v_ln_g, v_ln_b, v_a_w_in, v_a_w_group, v_a_scale, v_a_w_out, v_b_w_k, v_b_w_v, v_b_w_qg, v_b_sinks, v_b_w_out):
    _, seq, dm = x.shape
    n_groups = len(POOL_WINDOWS)
    gd = dm // n_groups
    kvw = b_w_k.shape[1]
    cb = 2 * dm // N_DEV
    rb = dm // N_DEV
    gb = gd // N_DEV

    x2 = x.reshape(seq, dm)
    target = loss_target.reshape(seq, dm)
    w_in_s = a_w_in.reshape(dm, cb)
    w_g_s = a_w_group.reshape(n_groups, gb, gd)
    w_out_s = a_w_out.reshape(rb, dm)
    w_qg_s = b_w_qg.reshape(dm, cb)
    w_outb_s = b_w_out.reshape(rb, dm)

    def cols(ref, dev):
        return ref.at[:, pl.ds(pl.multiple_of(dev * cb, LANES), cb)]

    def rows(ref, dev):
        return ref.at[pl.ds(pl.multiple_of(dev * rb, 8), rb), :]

    def group_rows(ref, dev):
        return ref.at[:, pl.ds(pl.multiple_of(dev * gb, 8), gb), :]

    def k_rows(ref, dev):
        return ref.at[pl.ds(pl.multiple_of(dev * rb, 8), rb), pl.ds(0, kvw)]

    def v_rows(ref, dev):
        return ref.at[pl.ds(pl.multiple_of(dev * rb, 8), rb), pl.ds(kvw, kvw)]

    def scale_cols(ref, dev):
        return ref.at[:, pl.ds(pl.multiple_of(dev * rb, LANES), rb)]

    bf = lambda a: a.astype(BF16)
    wide, square = jax.ShapeDtypeStruct((dm, 2 * dm), BF16), jax.ShapeDtypeStruct((dm, dm), BF16)
    w_g, scale, w_in = _gather_weights(
        "gather_a_in", 0, [(bf(w_g_s), 0, group_rows), (a_scale, 1, scale_cols), (bf(w_in_s), 2, cols)],
        [jax.ShapeDtypeStruct((n_groups, gd, gd), BF16), jax.ShapeDtypeStruct((1, dm), F32), wide])
    (w_out,) = _gather_weights("gather_a_out", 1, [(bf(w_out_s), 0, rows)], [square])
    w_kv, w_qg = _gather_weights(
        "gather_b_in", 2, [(bf(b_w_k), 0, k_rows), (bf(b_w_v), 0, v_rows), (bf(w_qg_s), 1, cols)],
        [jax.ShapeDtypeStruct((dm, 2 * kvw), BF16), wide])
    (w_outb,) = _gather_weights("gather_b_out", 3, [(bf(w_outb_s), 0, rows)], [square])

    tables = _rope_tables(seq)
    bm = _tile(seq, 2048)
    bn = _tile(dm, 1024)
    g0, g1, b0, b1 = ln_g[0:1], ln_g[1:2], ln_b[0:1], ln_b[1:2]

    xb = _cast_bf16("cast_x", x2)
    y, pooled, mixed, z_a = _pool_forward(xb, w_in, w_g, scale)
    xhat1, rstd1, x1b = _out_proj_norm(y, w_out, x2, g0, b0)

    kd, vd, kt, vt = _kv_proj(x1b, w_kv, tables)
    bmq = bm
    tab_spec = pl.BlockSpec((bmq, LANES), lambda i, j: (i, 0))

    def rope_scale(val, cos_ref, sa_ref, sb_ref):
        cos, sa, sb = cos_ref[...], sa_ref[...], sb_ref[...]
        return jnp.concatenate([_rope(val[:, j * LANES:(j + 1) * LANES], cos, sa, sb) * 0.125
                                for j in range(val.shape[1] // LANES)], axis=1)

    qs = _mm("b_q_proj", x1b, w_qg, dims=NN, grid=(seq // bmq, dm // bn),
             a_spec=pl.BlockSpec((bmq, dm), lambda i, j: (i, 0)), b_spec=pl.BlockSpec((dm, bn), lambda i, j: (0, j)),
             out_shape=jax.ShapeDtypeStruct((seq, dm), BF16), out_spec=pl.BlockSpec((bmq, bn), lambda i, j: (i, j)),
             epilogue=rope_scale, extras=tables, extra_specs=(tab_spec,) * 3)
    zb = _mm("b_gate_proj", x1b, w_qg, dims=NN, grid=(seq // bm, dm // bn),
             a_spec=pl.BlockSpec((bm, dm), lambda i, j: (i, 0)),
             b_spec=pl.BlockSpec((dm, bn), lambda i, j: (0, j + dm // bn)),
             out_shape=jax.ShapeDtypeStruct((seq, dm), F32), out_spec=pl.BlockSpec((bm, bn), lambda i, j: (i, j)))
    att, yb = _attn_forward(qs, kd, vt, zb, b_sinks)
    dr2, dr2b, stats_b = _out_proj_norm_loss(yb, w_outb, xhat1, g0, b0, g1, b1, target)

    def weight_grad(name, a, b, n_cols, b_spec=None, part=(0, 1), after=None):
        m_cols = a.shape[1] // part[1]
        tm, tn = _tile(m_cols, 1024), _tile(n_cols, 1024)
        first = part[0] * (m_cols // tm)
        return _mm(name, a, b, dims=TN, grid=(m_cols // tm, n_cols // tn),
                   a_spec=pl.BlockSpec((seq, tm), lambda i, j: (0, first + i)),
                   b_spec=b_spec(tn) if b_spec else pl.BlockSpec((seq, tn), lambda i, j: (0, j)),
                   out_shape=jax.ShapeDtypeStruct((m_cols, n_cols), BF16),
                   out_spec=pl.BlockSpec((tm, tn), lambda i, j: (i, j)),
                   extras=() if after is None else (after,), extra_specs=() if after is None else (ANY,))

    def halves_spec(tn):
        per = dm // tn
        return pl.BlockSpec((None, seq, tn), lambda i, j: (j // per, 0, j % per))

    def times_transposed(name, a, w):
        return _mm(name, a, w, dims=NT, grid=(seq // bm, dm // bn),
                   a_spec=pl.BlockSpec((bm, a.shape[1]), lambda i, j: (i, 0)),
                   b_spec=pl.BlockSpec((bn, w.shape[1]), lambda i, j: (j, 0)),
                   out_shape=jax.ShapeDtypeStruct((seq, dm), F32), out_spec=pl.BlockSpec((bm, bn), lambda i, j: (i, j)))

    def stat_row_cols(ref, dev):
        return ref.at[pl.ds(0, 1), pl.ds(pl.multiple_of(dev * rb, LANES), rb)]

    upd = {}
    last = [dr2b]
    my_core = lax.axis_index("c").astype(jnp.int32).reshape(1)

    def then(value):
        last[0] = value[0] if isinstance(value, (list, tuple)) else value
        return value

    def shard_update(key, parts, w, m, v):
        shape = w.shape
        flat = lambda a: a.reshape(-1, shape[-1])
        parts = list(parts) if isinstance(parts, (list, tuple)) else [parts]
        outs = then(_adamw_shard("adamw_" + key, [p.reshape(p.shape[0], -1, shape[-1]) for p in parts], flat(w),
                                 flat(m), flat(v), last[0]))
        upd[key] = [o.reshape(shape) for o in outs]

    def two_level_scatter(name, ids, streams):
        staged = _sibling_exchange(name + "_pair", streams, ids[0])

        def finish():
            sums = [then(_pair_sum(f"{name}_sum{s}", st[0], got, my_core, last[0]))
                    for s, (st, got) in enumerate(zip(streams, staged))]
            return _chip_exchange(name + "_chip", sums, ids[1])
        return finish

    d_w_outb = then(weight_grad("b_out_proj_dw", yb, dr2b, dm))
    (p_outb,) = _exchange_blocks("scatter_b_out", [(d_w_outb, rows, (rb, dm))], 4)
    dyb = times_transposed("b_out_proj_dx", dr2b, w_outb)
    dhq, dkd, dvd, dsink = then(_attn_backward(qs, kd, vd, kt, zb, att, dyb, b_sinks, tables, after=last[0]))
    dkv = _kv_grad_fold(dkd, dvd, tables)
    d_w_kv = weight_grad("b_kv_proj_dw", x1b, dkv, 2 * kvw)
    d_w_qg = then(weight_grad("b_qg_proj_dw", x1b, dhq, 2 * dm, halves_spec, after=d_w_kv))
    finish_b_in = two_level_scatter("scatter_b_in", (5, 11), [(d_w_qg, cols, (dm, cb))])
    p_k, p_v = _exchange_blocks("scatter_b_kv", [(d_w_kv, k_rows, (rb, kvw)), (d_w_kv, v_rows, (rb, kvw))], 14)
    dr1, dr1b, stats_a = _stream_grad_norm_backward(dhq, w_qg, dkv, w_kv, dr2, xhat1, rstd1, g0, after=last[0])
    last[0] = dr1b
    shard_update("b_w_out", p_outb, b_w_out, m_b_w_out, v_b_w_out)
    (p_qg,) = finish_b_in()
    shard_update("b_w_k", p_k, b_w_k, m_b_w_k, v_b_w_k)
    shard_update("b_w_v", p_v, b_w_v, m_b_w_v, v_b_w_v)
    all_b, all_a, all_sink = _exchange_blocks("gather_replicated_grads", [
        (stats_b, None, stats_b.shape), (stats_a, None, stats_a.shape), (dsink, None, dsink.shape)], 9)

    d_w_out = then(weight_grad("a_out_proj_dw", y, dr1b, dm, after=last[0]))
    (p_out,) = _exchange_blocks("scatter_a_out", [(d_w_out, rows, (rb, dm))], 6)
    dh, d_w_g, stats_s = then(_pool_mid_backward(dr1b, w_out, mixed, z_a, pooled, w_g, scale, after=last[0]))
    p_g, p_scale = _exchange_blocks("scatter_a_mid", [
        (d_w_g, group_rows, (n_groups, gb, gd)), (stats_s, stat_row_cols, (1, rb))], 7)
    shard_update("b_w_qg", p_qg, b_w_qg, m_b_w_qg, v_b_w_qg)
    rep = then(_adamw_replicated(all_b, all_a, all_sink, ln_g, ln_b, b_sinks, m_ln_g, m_ln_b, m_b_sinks, v_ln_g,
                                 v_ln_b, v_b_sinks, last[0]))
    upd["ln_g"], upd["ln_b"], upd["b_sinks"] = list(rep[0:4]), list(rep[4:8]), list(rep[8:12])
    finish_a_in = []
    for k in range(2):
        d_w_in = then(weight_grad(f"a_in_proj_dw_{k}", xb, dh, 2 * dm, halves_spec, part=(k, 2), after=last[0]))
        finish_a_in.append(two_level_scatter(f"scatter_a_in_{k}", (8 + 2 * k, 12 + k), [(d_w_in, cols, (dm // 2, cb))]))
    shard_update("a_w_out", p_out, a_w_out, m_a_w_out, v_a_w_out)
    shard_update("a_w_group", p_g, a_w_group, m_a_w_group, v_a_w_group)
    shard_update("a_scale", p_scale, a_scale, m_a_scale, v_a_scale)
    p_in = list(finish_a_in[0]()) + list(finish_a_in[1]())
    grad_x = then(_input_grad(dh, w_in, dr1, last[0]))
    flat_in = [a.reshape(dm, cb) for a in (a_w_in, m_a_w_in, v_a_w_in)]
    half = _adamw_shard("adamw_a_w_in_0", [p_in[0]], *flat_in, last[0], row_range=(0, dm // 2))
    full = _adamw_shard("adamw_a_w_in_1", [p_in[1]], *flat_in, last[0], row_range=(dm // 2, dm), into=half)
    upd["a_w_in"] = [o.reshape(a_w_in.shape) for o in full]

    loss = rep[12].reshape(())
    order = ["ln_g", "ln_b", "a_w_in", "a_w_group", "a_scale", "a_w_out", "b_w_k", "b_w_v", "b_w_qg", "b_sinks",
             "b_w_out"]
    return (loss, grad_x.reshape(x.shape), *[upd[n][0] for n in order], *[upd[n][1] for n in order],
            *[upd[n][2] for n in order], *[upd[n][3] for n in order])
```
